```python
import math
import jax, jax.numpy as jnp
from jax import lax
import numpy as np

D_MODEL = 1024
BATCH = 16
SEQ = 4096
DEPTH = 1

D_MIX = D_MODEL
D_CONV = D_MIX // 2
D_SSM = D_MIX - D_CONV
CONV_HEADS = 8
CONV_HEAD_DIM = D_CONV // CONV_HEADS
CONV_WIDTH = 31
SSM_GROUP = 16
SSM_GROUPS = D_SSM // SSM_GROUP
SSM_STATE = 64
D_FF = 128 * ((8 * D_MODEL // 3 + 127) // 128)
D_IN = 2 * D_CONV + D_SSM
FFN_RES = 0.5
EPS = 1e-6

kernel_name = "macaron_conv_s5_hybrid_layer"


def rmsnorm(x, g):
    xf = x.astype(jnp.float32)
    xf = xf * lax.rsqrt(jnp.mean(xf * xf, axis=-1, keepdims=True) + EPS)
    return (xf * g.astype(jnp.float32)).astype(x.dtype)


def layernorm(x, g, b):
    xf = x.astype(jnp.float32)
    mu = jnp.mean(xf, axis=-1, keepdims=True)
    xc = xf - mu
    var = jnp.mean(xc * xc, axis=-1, keepdims=True)
    y = xc * lax.rsqrt(var + EPS) * g.astype(jnp.float32) + b.astype(jnp.float32)
    return y.astype(x.dtype)


def swiglu(h, w1, w3, w2):
    return (jax.nn.silu(h @ w1) * (h @ w3)) @ w2


def conv_module(a_val, a_gate, conv_w, conv_b, ln_g, ln_b):
    a = a_val * jax.nn.sigmoid(a_gate)
    a = lax.conv_general_dilated(
        a, conv_w[:, None, :].astype(a.dtype),
        window_strides=(1,), padding=[(CONV_WIDTH - 1, 0)],
        dimension_numbers=("NWC", "WIO", "NWC"),
        feature_group_count=D_CONV) + conv_b
    a = layernorm(a, ln_g, ln_b)
    return jax.nn.silu(a)


def _complex_affine_combine(e1, e2):
    a1r, a1i, b1r, b1i = e1
    a2r, a2i, b2r, b2i = e2
    ar = a2r * a1r - a2i * a1i
    ai = a2r * a1i + a2i * a1r
    br = a2r * b1r - a2i * b1i + b2r
    bi = a2r * b1i + a2i * b1r + b2i
    return (ar, ai, br, bi)


def s5_layer(u, A_re, A_im, log_dt, B_re, B_im, C_re, C_im, D_skip, glu_w, glu_b):
    bsz, seq = u.shape[0], u.shape[1]
    ug = u.astype(jnp.float32).reshape(bsz, seq, SSM_GROUPS, SSM_GROUP)
    dt = jnp.exp(log_dt.astype(jnp.float32))[:, None]
    lr = A_re.astype(jnp.float32)
    li = A_im.astype(jnp.float32)
    zr, zi = lr * dt, li * dt
    mag = jnp.exp(zr)
    abar_r, abar_i = mag * jnp.cos(zi), mag * jnp.sin(zi)
    den = lr * lr + li * li
    nr = abar_r - 1.0
    coef_r = (nr * lr + abar_i * li) / den
    coef_i = (abar_i * lr - nr * li) / den
    br_, bi_ = B_re.astype(jnp.float32), B_im.astype(jnp.float32)
    bb_r = coef_r[..., None] * br_ - coef_i[..., None] * bi_
    bb_i = coef_r[..., None] * bi_ + coef_i[..., None] * br_
    bu_r = jnp.einsum("bsgh,gph->bsgp", ug, bb_r)
    bu_i = jnp.einsum("bsgh,gph->bsgp", ug, bb_i)
    ar_all = jnp.broadcast_to(abar_r, bu_r.shape)
    ai_all = jnp.broadcast_to(abar_i, bu_r.shape)
    _, _, xr, xi = lax.associative_scan(
        _complex_affine_combine, (ar_all, ai_all, bu_r, bu_i), axis=1)
    y = (jnp.einsum("bsgp,ghp->bsgh", xr, C_re.astype(jnp.float32))
         - jnp.einsum("bsgp,ghp->bsgh", xi, C_im.astype(jnp.float32)))
    y = y + D_skip.astype(jnp.float32).reshape(SSM_GROUPS, SSM_GROUP) * ug
    y = y.reshape(bsz, seq, D_SSM).astype(u.dtype)
    y = jax.nn.gelu(y)
    return y * jax.nn.sigmoid(y @ glu_w + glu_b)


def _fwd_setup_inputs(seed: int = 0) -> dict:
    key = jax.random.key(seed)
    ks = iter(jax.random.split(key, 48))
    L = DEPTH
    f32 = jnp.float32

    def nrm(shape, scale):
        return scale * jax.random.normal(next(ks), shape, f32)

    def gain(shape):
        return 1.0 + nrm(shape, 0.02)

    n_idx = jnp.arange(SSM_STATE, dtype=f32)
    A_re = -0.5 + nrm((L, SSM_GROUPS, SSM_STATE), 0.01)
    A_im = jnp.pi * n_idx[None, None, :] + nrm((L, SSM_GROUPS, SSM_STATE), 0.01)
    log_dt = jax.random.uniform(next(ks), (L, SSM_GROUPS), f32,
                                minval=math.log(1e-3), maxval=math.log(1e-1))
    b_scale = (SSM_GROUP ** -0.5) / math.sqrt(2.0)
    c_scale = (SSM_STATE ** -0.5) / math.sqrt(2.0)
    return {
        "x": nrm((BATCH, SEQ, D_MODEL), 1.0),
        "norm_ffn1": gain((L, D_MODEL)),
        "ffn1_w1": nrm((L, D_MODEL, D_FF), D_MODEL ** -0.5),
        "ffn1_w3": nrm((L, D_MODEL, D_FF), D_MODEL ** -0.5),
        "ffn1_w2": nrm((L, D_FF, D_MODEL), D_FF ** -0.5),
        "norm_mix": gain((L, D_MODEL)),
        "w_in": nrm((L, D_MODEL, D_IN), D_MODEL ** -0.5),
        "conv_w": nrm((L, CONV_WIDTH, D_CONV), CONV_WIDTH ** -0.5),
        "conv_b": nrm((L, D_CONV), 0.02),
        "conv_ln_g": gain((L, D_CONV)),
        "conv_ln_b": nrm((L, D_CONV), 0.02),
        "conv_out_g": gain((L, D_CONV)),
        "ssm_A_re": A_re,
        "ssm_A_im": A_im,
        "ssm_log_dt": log_dt,
        "ssm_B_re": nrm((L, SSM_GROUPS, SSM_STATE, SSM_GROUP), b_scale),
        "ssm_B_im": nrm((L, SSM_GROUPS, SSM_STATE, SSM_GROUP), b_scale),
        "ssm_C_re": nrm((L, SSM_GROUPS, SSM_GROUP, SSM_STATE), c_scale),
        "ssm_C_im": nrm((L, SSM_GROUPS, SSM_GROUP, SSM_STATE), c_scale),
        "ssm_D": 1.0 + nrm((L, D_SSM), 0.1),
        "ssm_glu_w": nrm((L, D_SSM, D_SSM), D_SSM ** -0.5),
        "ssm_glu_b": nrm((L, D_SSM), 0.02),
        "ssm_out_g": gain((L, D_SSM)),
        "w_out": nrm((L, D_MIX, D_MODEL), D_MIX ** -0.5),
        "norm_ffn2": gain((L, D_MODEL)),
        "ffn2_w1": nrm((L, D_MODEL, D_FF), D_MODEL ** -0.5),
        "ffn2_w3": nrm((L, D_MODEL, D_FF), D_MODEL ** -0.5),
        "ffn2_w2": nrm((L, D_FF, D_MODEL), D_FF ** -0.5),
        "norm_final": gain((D_MODEL,)),
    }


def _fwd_reference(x, norm_ffn1, ffn1_w1, ffn1_w3, ffn1_w2, norm_mix, w_in,
              conv_w, conv_b, conv_ln_g, conv_ln_b, conv_out_g,
              ssm_A_re, ssm_A_im, ssm_log_dt, ssm_B_re, ssm_B_im, ssm_C_re, ssm_C_im,
              ssm_D, ssm_glu_w, ssm_glu_b, ssm_out_g, w_out,
              norm_ffn2, ffn2_w1, ffn2_w3, ffn2_w2, norm_final):
    for l in range(DEPTH):
        x = x + FFN_RES * swiglu(rmsnorm(x, norm_ffn1[l]), ffn1_w1[l], ffn1_w3[l], ffn1_w2[l])

        h = rmsnorm(x, norm_mix[l])
        proj = h @ w_in[l]
        a_val = proj[..., :D_CONV]
        a_gate = proj[..., D_CONV:2 * D_CONV]
        u = proj[..., 2 * D_CONV:]

        a = conv_module(a_val, a_gate, conv_w[l], conv_b[l], conv_ln_g[l], conv_ln_b[l])
        a = rmsnorm(a, conv_out_g[l])

        s = s5_layer(u, ssm_A_re[l], ssm_A_im[l], ssm_log_dt[l], ssm_B_re[l], ssm_B_im[l],
                     ssm_C_re[l], ssm_C_im[l], ssm_D[l], ssm_glu_w[l], ssm_glu_b[l])
        s = rmsnorm(s, ssm_out_g[l])

        mixed = jnp.concatenate([a, s], axis=-1)
        x = x + mixed @ w_out[l]

        x = x + FFN_RES * swiglu(rmsnorm(x, norm_ffn2[l]), ffn2_w1[l], ffn2_w3[l], ffn2_w2[l])
    return rmsnorm(x, norm_final)


import jax as _jax
import jax.numpy as _jnp

TWIN_FORMAT = 'train_step'
FWD_PARAMS = ['x', 'norm_ffn1', 'ffn1_w1', 'ffn1_w3', 'ffn1_w2', 'norm_mix', 'w_in', 'conv_w', 'conv_b', 'conv_ln_g', 'conv_ln_b', 'conv_out_g', 'ssm_A_re', 'ssm_A_im', 'ssm_log_dt', 'ssm_B_re', 'ssm_B_im', 'ssm_C_re', 'ssm_C_im', 'ssm_D', 'ssm_glu_w', 'ssm_glu_b', 'ssm_out_g', 'w_out', 'norm_ffn2', 'ffn2_w1', 'ffn2_w3', 'ffn2_w2', 'norm_final']
TWIN_WEIGHTS = ['norm_ffn1', 'ffn1_w1', 'ffn1_w3', 'ffn1_w2', 'norm_mix', 'w_in', 'conv_w', 'conv_b', 'conv_ln_g', 'conv_ln_b', 'conv_out_g', 'ssm_A_re', 'ssm_A_im', 'ssm_log_dt', 'ssm_B_re', 'ssm_B_im', 'ssm_C_re', 'ssm_C_im', 'ssm_D', 'ssm_glu_w', 'ssm_glu_b', 'ssm_out_g', 'w_out', 'norm_ffn2', 'ffn2_w1', 'ffn2_w3', 'ffn2_w2', 'norm_final']
TWIN_DIFF_INPUT = 'x'
TWIN_INPUTS = ['x', 'norm_ffn1', 'ffn1_w1', 'ffn1_w3', 'ffn1_w2', 'norm_mix', 'w_in', 'conv_w', 'conv_b', 'conv_ln_g', 'conv_ln_b', 'conv_out_g', 'ssm_A_re', 'ssm_A_im', 'ssm_log_dt', 'ssm_B_re', 'ssm_B_im', 'ssm_C_re', 'ssm_C_im', 'ssm_D', 'ssm_glu_w', 'ssm_glu_b', 'ssm_out_g', 'w_out', 'norm_ffn2', 'ffn2_w1', 'ffn2_w3', 'ffn2_w2', 'norm_final', 'loss_target', 'm_norm_ffn1', 'm_ffn1_w1', 'm_ffn1_w3', 'm_ffn1_w2', 'm_norm_mix', 'm_w_in', 'm_conv_w', 'm_conv_b', 'm_conv_ln_g', 'm_conv_ln_b', 'm_conv_out_g', 'm_ssm_A_re', 'm_ssm_A_im', 'm_ssm_log_dt', 'm_ssm_B_re', 'm_ssm_B_im', 'm_ssm_C_re', 'm_ssm_C_im', 'm_ssm_D', 'm_ssm_glu_w', 'm_ssm_glu_b', 'm_ssm_out_g', 'm_w_out', 'm_norm_ffn2', 'm_ffn2_w1', 'm_ffn2_w3', 'm_ffn2_w2', 'm_norm_final', 'v_norm_ffn1', 'v_ffn1_w1', 'v_ffn1_w3', 'v_ffn1_w2', 'v_norm_mix', 'v_w_in', 'v_conv_w', 'v_conv_b', 'v_conv_ln_g', 'v_conv_ln_b', 'v_conv_out_g', 'v_ssm_A_re', 'v_ssm_A_im', 'v_ssm_log_dt', 'v_ssm_B_re', 'v_ssm_B_im', 'v_ssm_C_re', 'v_ssm_C_im', 'v_ssm_D', 'v_ssm_glu_w', 'v_ssm_glu_b', 'v_ssm_out_g', 'v_w_out', 'v_norm_ffn2', 'v_ffn2_w1', 'v_ffn2_w3', 'v_ffn2_w2', 'v_norm_final']
TWIN_OUTPUTS = ['loss', 'grad_x', 'grad_norm_ffn1', 'grad_ffn1_w1', 'grad_ffn1_w3', 'grad_ffn1_w2', 'grad_norm_mix', 'grad_w_in', 'grad_conv_w', 'grad_conv_b', 'grad_conv_ln_g', 'grad_conv_ln_b', 'grad_conv_out_g', 'grad_ssm_A_re', 'grad_ssm_A_im', 'grad_ssm_log_dt', 'grad_ssm_B_re', 'grad_ssm_B_im', 'grad_ssm_C_re', 'grad_ssm_C_im', 'grad_ssm_D', 'grad_ssm_glu_w', 'grad_ssm_glu_b', 'grad_ssm_out_g', 'grad_w_out', 'grad_norm_ffn2', 'grad_ffn2_w1', 'grad_ffn2_w3', 'grad_ffn2_w2', 'grad_norm_final', 'delta_norm_ffn1', 'delta_ffn1_w1', 'delta_ffn1_w3', 'delta_ffn1_w2', 'delta_norm_mix', 'delta_w_in', 'delta_conv_w', 'delta_conv_b', 'delta_conv_ln_g', 'delta_conv_ln_b', 'delta_conv_out_g', 'delta_ssm_A_re', 'delta_ssm_A_im', 'delta_ssm_log_dt', 'delta_ssm_B_re', 'delta_ssm_B_im', 'delta_ssm_C_re', 'delta_ssm_C_im', 'delta_ssm_D', 'delta_ssm_glu_w', 'delta_ssm_glu_b', 'delta_ssm_out_g', 'delta_w_out', 'delta_norm_ffn2', 'delta_ffn2_w1', 'delta_ffn2_w3', 'delta_ffn2_w2', 'delta_norm_final', 'new_m_norm_ffn1', 'new_m_ffn1_w1', 'new_m_ffn1_w3', 'new_m_ffn1_w2', 'new_m_norm_mix', 'new_m_w_in', 'new_m_conv_w', 'new_m_conv_b', 'new_m_conv_ln_g', 'new_m_conv_ln_b', 'new_m_conv_out_g', 'new_m_ssm_A_re', 'new_m_ssm_A_im', 'new_m_ssm_log_dt', 'new_m_ssm_B_re', 'new_m_ssm_B_im', 'new_m_ssm_C_re', 'new_m_ssm_C_im', 'new_m_ssm_D', 'new_m_ssm_glu_w', 'new_m_ssm_glu_b', 'new_m_ssm_out_g', 'new_m_w_out', 'new_m_norm_ffn2', 'new_m_ffn2_w1', 'new_m_ffn2_w3', 'new_m_ffn2_w2', 'new_m_norm_final', 'new_v_norm_ffn1', 'new_v_ffn1_w1', 'new_v_ffn1_w3', 'new_v_ffn1_w2', 'new_v_norm_mix', 'new_v_w_in', 'new_v_conv_w', 'new_v_conv_b', 'new_v_conv_ln_g', 'new_v_conv_ln_b', 'new_v_conv_out_g', 'new_v_ssm_A_re', 'new_v_ssm_A_im', 'new_v_ssm_log_dt', 'new_v_ssm_B_re', 'new_v_ssm_B_im', 'new_v_ssm_C_re', 'new_v_ssm_C_im', 'new_v_ssm_D', 'new_v_ssm_glu_w', 'new_v_ssm_glu_b', 'new_v_ssm_out_g', 'new_v_w_out', 'new_v_norm_ffn2', 'new_v_ffn2_w1', 'new_v_ffn2_w3', 'new_v_ffn2_w2', 'new_v_norm_final']
TWIN_LEAF_KINDS = {'loss': 'loss', 'grad_x': 'grad_x', 'grad_norm_ffn1': 'grad_w', 'grad_ffn1_w1': 'grad_w', 'grad_ffn1_w3': 'grad_w', 'grad_ffn1_w2': 'grad_w', 'grad_norm_mix': 'grad_w', 'grad_w_in': 'grad_w', 'grad_conv_w': 'grad_w', 'grad_conv_b': 'grad_w', 'grad_conv_ln_g': 'grad_w', 'grad_conv_ln_b': 'grad_w', 'grad_conv_out_g': 'grad_w', 'grad_ssm_A_re': 'grad_w', 'grad_ssm_A_im': 'grad_w', 'grad_ssm_log_dt': 'grad_w', 'grad_ssm_B_re': 'grad_w', 'grad_ssm_B_im': 'grad_w', 'grad_ssm_C_re': 'grad_w', 'grad_ssm_C_im': 'grad_w', 'grad_ssm_D': 'grad_w', 'grad_ssm_glu_w': 'grad_w', 'grad_ssm_glu_b': 'grad_w', 'grad_ssm_out_g': 'grad_w', 'grad_w_out': 'grad_w', 'grad_norm_ffn2': 'grad_w', 'grad_ffn2_w1': 'grad_w', 'grad_ffn2_w3': 'grad_w', 'grad_ffn2_w2': 'grad_w', 'grad_norm_final': 'grad_w', 'delta_norm_ffn1': 'delta_w', 'delta_ffn1_w1': 'delta_w', 'delta_ffn1_w3': 'delta_w', 'delta_ffn1_w2': 'delta_w', 'delta_norm_mix': 'delta_w', 'delta_w_in': 'delta_w', 'delta_conv_w': 'delta_w', 'delta_conv_b': 'delta_w', 'delta_conv_ln_g': 'delta_w', 'delta_conv_ln_b': 'delta_w', 'delta_conv_out_g': 'delta_w', 'delta_ssm_A_re': 'delta_w', 'delta_ssm_A_im': 'delta_w', 'delta_ssm_log_dt': 'delta_w', 'delta_ssm_B_re': 'delta_w', 'delta_ssm_B_im': 'delta_w', 'delta_ssm_C_re': 'delta_w', 'delta_ssm_C_im': 'delta_w', 'delta_ssm_D': 'delta_w', 'delta_ssm_glu_w': 'delta_w', 'delta_ssm_glu_b': 'delta_w', 'delta_ssm_out_g': 'delta_w', 'delta_w_out': 'delta_w', 'delta_norm_ffn2': 'delta_w', 'delta_ffn2_w1': 'delta_w', 'delta_ffn2_w3': 'delta_w', 'delta_ffn2_w2': 'delta_w', 'delta_norm_final': 'delta_w', 'new_m_norm_ffn1': 'new_m', 'new_m_ffn1_w1': 'new_m', 'new_m_ffn1_w3': 'new_m', 'new_m_ffn1_w2': 'new_m', 'new_m_norm_mix': 'new_m', 'new_m_w_in': 'new_m', 'new_m_conv_w': 'new_m', 'new_m_conv_b': 'new_m', 'new_m_conv_ln_g': 'new_m', 'new_m_conv_ln_b': 'new_m', 'new_m_conv_out_g': 'new_m', 'new_m_ssm_A_re': 'new_m', 'new_m_ssm_A_im': 'new_m', 'new_m_ssm_log_dt': 'new_m', 'new_m_ssm_B_re': 'new_m', 'new_m_ssm_B_im': 'new_m', 'new_m_ssm_C_re': 'new_m', 'new_m_ssm_C_im': 'new_m', 'new_m_ssm_D': 'new_m', 'new_m_ssm_glu_w': 'new_m', 'new_m_ssm_glu_b': 'new_m', 'new_m_ssm_out_g': 'new_m', 'new_m_w_out': 'new_m', 'new_m_norm_ffn2': 'new_m', 'new_m_ffn2_w1': 'new_m', 'new_m_ffn2_w3': 'new_m', 'new_m_ffn2_w2': 'new_m', 'new_m_norm_final': 'new_m', 'new_v_norm_ffn1': 'new_v', 'new_v_ffn1_w1': 'new_v', 'new_v_ffn1_w3': 'new_v', 'new_v_ffn1_w2': 'new_v', 'new_v_norm_mix': 'new_v', 'new_v_w_in': 'new_v', 'new_v_conv_w': 'new_v', 'new_v_conv_b': 'new_v', 'new_v_conv_ln_g': 'new_v', 'new_v_conv_ln_b': 'new_v', 'new_v_conv_out_g': 'new_v', 'new_v_ssm_A_re': 'new_v', 'new_v_ssm_A_im': 'new_v', 'new_v_ssm_log_dt': 'new_v', 'new_v_ssm_B_re': 'new_v', 'new_v_ssm_B_im': 'new_v', 'new_v_ssm_C_re': 'new_v', 'new_v_ssm_C_im': 'new_v', 'new_v_ssm_D': 'new_v', 'new_v_ssm_glu_w': 'new_v', 'new_v_ssm_glu_b': 'new_v', 'new_v_ssm_out_g': 'new_v', 'new_v_w_out': 'new_v', 'new_v_norm_ffn2': 'new_v', 'new_v_ffn2_w1': 'new_v', 'new_v_ffn2_w3': 'new_v', 'new_v_ffn2_w2': 'new_v', 'new_v_norm_final': 'new_v'}


def _forward(args):
    return _fwd_reference(*[args[k] for k in FWD_PARAMS])


def _output_shape():
    out = _jax.eval_shape(lambda: _forward(_fwd_setup_inputs(0)))
    return out.shape, out.dtype

N_MICROBATCH = 1
ADAM_LR = 0.001
ADAM_B1 = 0.9
ADAM_B2 = 0.999
ADAM_EPS = 1e-08
ADAM_WD = 0.01
ADAM_STEP = 10
PER_EXAMPLE_BATCH_AXIS = {'x': 0, 'loss_target': 0}
SHARED_INPUTS = []
_WEIGHT_DTYPES = {'norm_ffn1': _jnp.float32, 'ffn1_w1': _jnp.float32, 'ffn1_w3': _jnp.float32, 'ffn1_w2': _jnp.float32, 'norm_mix': _jnp.float32, 'w_in': _jnp.float32, 'conv_w': _jnp.float32, 'conv_b': _jnp.float32, 'conv_ln_g': _jnp.float32, 'conv_ln_b': _jnp.float32, 'conv_out_g': _jnp.float32, 'ssm_A_re': _jnp.float32, 'ssm_A_im': _jnp.float32, 'ssm_log_dt': _jnp.float32, 'ssm_B_re': _jnp.float32, 'ssm_B_im': _jnp.float32, 'ssm_C_re': _jnp.float32, 'ssm_C_im': _jnp.float32, 'ssm_D': _jnp.float32, 'ssm_glu_w': _jnp.float32, 'ssm_glu_b': _jnp.float32, 'ssm_out_g': _jnp.float32, 'w_out': _jnp.float32, 'norm_ffn2': _jnp.float32, 'ffn2_w1': _jnp.float32, 'ffn2_w3': _jnp.float32, 'ffn2_w2': _jnp.float32, 'norm_final': _jnp.float32}
MOMENT_SCALE = {'norm_ffn1': 1.081477e-01, 'ffn1_w1': 4.757948e-02, 'ffn1_w3': 4.620844e-02, 'ffn1_w2': 7.635220e-02, 'norm_mix': 2.061853e-01, 'w_in': 1.581852e-01, 'conv_w': 1.887481e-01, 'conv_b': 4.297885e-01, 'conv_ln_g': 2.694869e-01, 'conv_ln_b': 2.790459e-01, 'conv_out_g': 2.142947e-01, 'ssm_A_re': 1.137551e-02, 'ssm_A_im': 1.204671e-02, 'ssm_log_dt': 1.160448e+01, 'ssm_B_re': 6.849674e-03, 'ssm_B_im': 7.104989e-03, 'ssm_C_re': 1.357394e-02, 'ssm_C_im': 1.392035e-02, 'ssm_D': 2.252102e-01, 'ssm_glu_w': 5.053383e-02, 'ssm_glu_b': 7.726442e-02, 'ssm_out_g': 1.793964e-01, 'w_out': 1.922881e-01, 'norm_ffn2': 7.867474e-02, 'ffn2_w1': 3.168756e-02, 'ffn2_w3': 3.072497e-02, 'ffn2_w2': 5.124326e-02, 'norm_final': 6.409702e+01}


def _to_microbatches(a, axis):
    t = _jnp.moveaxis(a, axis, 0)
    t = t.reshape((N_MICROBATCH, t.shape[0] // N_MICROBATCH) + t.shape[1:])
    return _jnp.moveaxis(t, 1, axis + 1)


def setup_inputs(seed: int = 0) -> dict:
    inp = _fwd_setup_inputs(seed)
    key = _jax.random.fold_in(_jax.random.key(seed), 7919)
    shape, _ = _output_shape()
    out = dict(inp)
    out["loss_target"] = _jax.random.normal(_jax.random.fold_in(key, 0), shape, _jnp.float32)
    for i, name in enumerate(TWIN_WEIGHTS):
        w = inp[name].astype(_jnp.float32)
        if MOMENT_SCALE is None:
            s = _jnp.sqrt(_jnp.mean(_jnp.square(w)) + 1e-30)
        else:
            s = MOMENT_SCALE[name]
        km, kv = _jax.random.split(_jax.random.fold_in(key, i + 1))
        out[name] = w
        out["m_" + name] = s * _jax.random.normal(km, w.shape, _jnp.float32)
        out["v_" + name] = (s * s) * _jax.random.uniform(kv, w.shape, _jnp.float32, 0.5, 1.5)
    if N_MICROBATCH > 1:
        for name, axis in PER_EXAMPLE_BATCH_AXIS.items():
            out[name] = _to_microbatches(out[name], axis)
    return {'x': out['x'], 'norm_ffn1': out['norm_ffn1'], 'ffn1_w1': out['ffn1_w1'], 'ffn1_w3': out['ffn1_w3'], 'ffn1_w2': out['ffn1_w2'], 'norm_mix': out['norm_mix'], 'w_in': out['w_in'], 'conv_w': out['conv_w'], 'conv_b': out['conv_b'], 'conv_ln_g': out['conv_ln_g'], 'conv_ln_b': out['conv_ln_b'], 'conv_out_g': out['conv_out_g'], 'ssm_A_re': out['ssm_A_re'], 'ssm_A_im': out['ssm_A_im'], 'ssm_log_dt': out['ssm_log_dt'], 'ssm_B_re': out['ssm_B_re'], 'ssm_B_im': out['ssm_B_im'], 'ssm_C_re': out['ssm_C_re'], 'ssm_C_im': out['ssm_C_im'], 'ssm_D': out['ssm_D'], 'ssm_glu_w': out['ssm_glu_w'], 'ssm_glu_b': out['ssm_glu_b'], 'ssm_out_g': out['ssm_out_g'], 'w_out': out['w_out'], 'norm_ffn2': out['norm_ffn2'], 'ffn2_w1': out['ffn2_w1'], 'ffn2_w3': out['ffn2_w3'], 'ffn2_w2': out['ffn2_w2'], 'norm_final': out['norm_final'], 'loss_target': out['loss_target'], 'm_norm_ffn1': out['m_norm_ffn1'], 'm_ffn1_w1': out['m_ffn1_w1'], 'm_ffn1_w3': out['m_ffn1_w3'], 'm_ffn1_w2': out['m_ffn1_w2'], 'm_norm_mix': out['m_norm_mix'], 'm_w_in': out['m_w_in'], 'm_conv_w': out['m_conv_w'], 'm_conv_b': out['m_conv_b'], 'm_conv_ln_g': out['m_conv_ln_g'], 'm_conv_ln_b': out['m_conv_ln_b'], 'm_conv_out_g': out['m_conv_out_g'], 'm_ssm_A_re': out['m_ssm_A_re'], 'm_ssm_A_im': out['m_ssm_A_im'], 'm_ssm_log_dt': out['m_ssm_log_dt'], 'm_ssm_B_re': out['m_ssm_B_re'], 'm_ssm_B_im': out['m_ssm_B_im'], 'm_ssm_C_re': out['m_ssm_C_re'], 'm_ssm_C_im': out['m_ssm_C_im'], 'm_ssm_D': out['m_ssm_D'], 'm_ssm_glu_w': out['m_ssm_glu_w'], 'm_ssm_glu_b': out['m_ssm_glu_b'], 'm_ssm_out_g': out['m_ssm_out_g'], 'm_w_out': out['m_w_out'], 'm_norm_ffn2': out['m_norm_ffn2'], 'm_ffn2_w1': out['m_ffn2_w1'], 'm_ffn2_w3': out['m_ffn2_w3'], 'm_ffn2_w2': out['m_ffn2_w2'], 'm_norm_final': out['m_norm_final'], 'v_norm_ffn1': out['v_norm_ffn1'], 'v_ffn1_w1': out['v_ffn1_w1'], 'v_ffn1_w3': out['v_ffn1_w3'], 'v_ffn1_w2': out['v_ffn1_w2'], 'v_norm_mix': out['v_norm_mix'], 'v_w_in': out['v_w_in'], 'v_conv_w': out['v_conv_w'], 'v_conv_b': out['v_conv_b'], 'v_conv_ln_g': out['v_conv_ln_g'], 'v_conv_ln_b': out['v_conv_ln_b'], 'v_conv_out_g': out['v_conv_out_g'], 'v_ssm_A_re': out['v_ssm_A_re'], 'v_ssm_A_im': out['v_ssm_A_im'], 'v_ssm_log_dt': out['v_ssm_log_dt'], 'v_ssm_B_re': out['v_ssm_B_re'], 'v_ssm_B_im': out['v_ssm_B_im'], 'v_ssm_C_re': out['v_ssm_C_re'], 'v_ssm_C_im': out['v_ssm_C_im'], 'v_ssm_D': out['v_ssm_D'], 'v_ssm_glu_w': out['v_ssm_glu_w'], 'v_ssm_glu_b': out['v_ssm_glu_b'], 'v_ssm_out_g': out['v_ssm_out_g'], 'v_w_out': out['v_w_out'], 'v_norm_ffn2': out['v_norm_ffn2'], 'v_ffn2_w1': out['v_ffn2_w1'], 'v_ffn2_w3': out['v_ffn2_w3'], 'v_ffn2_w2': out['v_ffn2_w2'], 'v_norm_final': out['v_norm_final']}


def _loss(weights, diff, rest, loss_target):
    with _jax.named_scope("forward"):
        args = {**rest, TWIN_DIFF_INPUT: diff, **{k: w.astype(_WEIGHT_DTYPES[k]) for k, w in weights.items()}}
        y = _forward(args)
    with _jax.named_scope("loss_head"):
        err = _jnp.square(y.astype(_jnp.float32) - loss_target)
        return 0.5 * _jnp.sum(_jnp.mean(err, axis=-1)) if err.ndim else 0.5 * err


def _adamw(w, g, m, v):
    m = ADAM_B1 * m + (1.0 - ADAM_B1) * g
    v = ADAM_B2 * v + (1.0 - ADAM_B2) * _jnp.square(g)
    m_hat = m / (1.0 - ADAM_B1 ** ADAM_STEP)
    v_hat = v / (1.0 - ADAM_B2 ** ADAM_STEP)
    delta = -ADAM_LR * (m_hat / (_jnp.sqrt(v_hat) + ADAM_EPS) + ADAM_WD * w)
    return delta, m, v


def reference(x, norm_ffn1, ffn1_w1, ffn1_w3, ffn1_w2, norm_mix, w_in, conv_w, conv_b, conv_ln_g, conv_ln_b, conv_out_g, ssm_A_re, ssm_A_im, ssm_log_dt, ssm_B_re, ssm_B_im, ssm_C_re, ssm_C_im, ssm_D, ssm_glu_w, ssm_glu_b, ssm_out_g, w_out, norm_ffn2, ffn2_w1, ffn2_w3, ffn2_w2, norm_final, loss_target, m_norm_ffn1, m_ffn1_w1, m_ffn1_w3, m_ffn1_w2, m_norm_mix, m_w_in, m_conv_w, m_conv_b, m_conv_ln_g, m_conv_ln_b, m_conv_out_g, m_ssm_A_re, m_ssm_A_im, m_ssm_log_dt, m_ssm_B_re, m_ssm_B_im, m_ssm_C_re, m_ssm_C_im, m_ssm_D, m_ssm_glu_w, m_ssm_glu_b, m_ssm_out_g, m_w_out, m_norm_ffn2, m_ffn2_w1, m_ffn2_w3, m_ffn2_w2, m_norm_final, v_norm_ffn1, v_ffn1_w1, v_ffn1_w3, v_ffn1_w2, v_norm_mix, v_w_in, v_conv_w, v_conv_b, v_conv_ln_g, v_conv_ln_b, v_conv_out_g, v_ssm_A_re, v_ssm_A_im, v_ssm_log_dt, v_ssm_B_re, v_ssm_B_im, v_ssm_C_re, v_ssm_C_im, v_ssm_D, v_ssm_glu_w, v_ssm_glu_b, v_ssm_out_g, v_w_out, v_norm_ffn2, v_ffn2_w1, v_ffn2_w3, v_ffn2_w2, v_norm_final):
    given = dict(x=x, norm_ffn1=norm_ffn1, ffn1_w1=ffn1_w1, ffn1_w3=ffn1_w3, ffn1_w2=ffn1_w2, norm_mix=norm_mix, w_in=w_in, conv_w=conv_w, conv_b=conv_b, conv_ln_g=conv_ln_g, conv_ln_b=conv_ln_b, conv_out_g=conv_out_g, ssm_A_re=ssm_A_re, ssm_A_im=ssm_A_im, ssm_log_dt=ssm_log_dt, ssm_B_re=ssm_B_re, ssm_B_im=ssm_B_im, ssm_C_re=ssm_C_re, ssm_C_im=ssm_C_im, ssm_D=ssm_D, ssm_glu_w=ssm_glu_w, ssm_glu_b=ssm_glu_b, ssm_out_g=ssm_out_g, w_out=w_out, norm_ffn2=norm_ffn2, ffn2_w1=ffn2_w1, ffn2_w3=ffn2_w3, ffn2_w2=ffn2_w2, norm_final=norm_final, loss_target=loss_target, m_norm_ffn1=m_norm_ffn1, m_ffn1_w1=m_ffn1_w1, m_ffn1_w3=m_ffn1_w3, m_ffn1_w2=m_ffn1_w2, m_norm_mix=m_norm_mix, m_w_in=m_w_in, m_conv_w=m_conv_w, m_conv_b=m_conv_b, m_conv_ln_g=m_conv_ln_g, m_conv_ln_b=m_conv_ln_b, m_conv_out_g=m_conv_out_g, m_ssm_A_re=m_ssm_A_re, m_ssm_A_im=m_ssm_A_im, m_ssm_log_dt=m_ssm_log_dt, m_ssm_B_re=m_ssm_B_re, m_ssm_B_im=m_ssm_B_im, m_ssm_C_re=m_ssm_C_re, m_ssm_C_im=m_ssm_C_im, m_ssm_D=m_ssm_D, m_ssm_glu_w=m_ssm_glu_w, m_ssm_glu_b=m_ssm_glu_b, m_ssm_out_g=m_ssm_out_g, m_w_out=m_w_out, m_norm_ffn2=m_norm_ffn2, m_ffn2_w1=m_ffn2_w1, m_ffn2_w3=m_ffn2_w3, m_ffn2_w2=m_ffn2_w2, m_norm_final=m_norm_final, v_norm_ffn1=v_norm_ffn1, v_ffn1_w1=v_ffn1_w1, v_ffn1_w3=v_ffn1_w3, v_ffn1_w2=v_ffn1_w2, v_norm_mix=v_norm_mix, v_w_in=v_w_in, v_conv_w=v_conv_w, v_conv_b=v_conv_b, v_conv_ln_g=v_conv_ln_g, v_conv_ln_b=v_conv_ln_b, v_conv_out_g=v_conv_out_g, v_ssm_A_re=v_ssm_A_re, v_ssm_A_im=v_ssm_A_im, v_ssm_log_dt=v_ssm_log_dt, v_ssm_B_re=v_ssm_B_re, v_ssm_B_im=v_ssm_B_im, v_ssm_C_re=v_ssm_C_re, v_ssm_C_im=v_ssm_C_im, v_ssm_D=v_ssm_D, v_ssm_glu_w=v_ssm_glu_w, v_ssm_glu_b=v_ssm_glu_b, v_ssm_out_g=v_ssm_out_g, v_w_out=v_w_out, v_norm_ffn2=v_norm_ffn2, v_ffn2_w1=v_ffn2_w1, v_ffn2_w3=v_ffn2_w3, v_ffn2_w2=v_ffn2_w2, v_norm_final=v_norm_final)
    weights = {n: given[n] for n in TWIN_WEIGHTS}
    shared = {n: given[n] for n in SHARED_INPUTS}
    per_example = {n: given[n] for n in ['x']}
    grad_fn = _jax.value_and_grad(_loss, argnums=(0, 1))

    def one_microbatch(ex, loss_target):
        ex = dict(ex)
        diff = ex.pop(TWIN_DIFF_INPUT)
        return grad_fn(weights, diff, {**shared, **ex}, loss_target)

    if N_MICROBATCH == 1:
        loss, (grad_w, grad_x) = one_microbatch(per_example, given["loss_target"])
    else:
        def body(carry, xs):
            loss_sum, grad_sum = carry
            l_k, (gw_k, gx_k) = one_microbatch(xs[0], xs[1])
            with _jax.named_scope("update"):
                return (loss_sum + l_k, _jax.tree.map(_jnp.add, grad_sum, gw_k)), gx_k

        init = (_jnp.zeros((), _jnp.float32), _jax.tree.map(_jnp.zeros_like, weights))
        (loss, grad_w), grad_x = _jax.lax.scan(body, init, (per_example, given["loss_target"]))
    with _jax.named_scope("update"):
        delta_w, new_m, new_v = {}, {}, {}
        for n in TWIN_WEIGHTS:
            delta_w[n], new_m[n], new_v[n] = _adamw(weights[n], grad_w[n], given["m_" + n], given["v_" + n])
    return (loss, grad_x, *[grad_w[n] for n in TWIN_WEIGHTS], *[delta_w[n] for n in TWIN_WEIGHTS],
            *[new_m[n] for n in TWIN_WEIGHTS], *[new_v[n] for n in TWIN_WEIGHTS])
```

```python
import functools
import math

import jax
import jax.numpy as jnp
from jax import lax
from jax.experimental import pallas as pl
from jax.experimental.pallas import tpu as pltpu

F32 = jnp.float32
BF16 = jnp.bfloat16

EPS = 1e-6
FFN_RES = 0.5
CONV_WIDTH = 31
CONV_HALO = 32
SSM_GROUP = 16
SSM_STATE = 64
ADAM_LR, ADAM_B1, ADAM_B2, ADAM_EPS, ADAM_WD, ADAM_STEP = 0.001, 0.9, 0.999, 1e-08, 0.01, 10

N_DEV = 8
MESH_AXES = ("x", "y", "c")
SUBLANES = 8
LANES = 128
PACK_W = 1024
V7X_VMEM_BYTES = 64 * 2**20
VMEM_LIMIT = V7X_VMEM_BYTES - 8 * 2**20

TILE = dict(row=256, mm_m=512, mm_n=512, mm_k=1024, up_m=512, up_n=256, conv_t=512, scan_t=256, adam_r=192)

_GELU_K = math.sqrt(2.0 / math.pi)
_GELU_C = 0.044715


def _pick(n, target, mult):
    best = None
    for t in range(mult, min(n, target) + 1, mult):
        if n % t == 0:
            best = t
    return n if best is None else best


def _cparams(*sem):
    return pltpu.CompilerParams(dimension_semantics=sem, vmem_limit_bytes=VMEM_LIMIT)


def _sds(shape, dtype):
    return jax.ShapeDtypeStruct(shape, dtype)


def _dot(a, b):
    return jnp.dot(a, b, preferred_element_type=F32)


def _dot_nt(a, b):
    return lax.dot_general(a, b, (((1,), (1,)), ((), ())), preferred_element_type=F32)


def _dot_tn(a, b):
    return lax.dot_general(a, b, (((0,), (0,)), ((), ())), preferred_element_type=F32)


def _sigmoid(x):
    return 1.0 / (1.0 + jnp.exp(-x))


def _rms_stats(x):
    r = lax.rsqrt(jnp.mean(x * x, axis=-1, keepdims=True) + EPS)
    return r, x * r


def _rms_bwd(x, g, dy):
    r, xh = _rms_stats(x)
    dxh = dy * g
    dx = r * (dxh - xh * jnp.mean(dxh * xh, axis=-1, keepdims=True))
    return dx, jnp.sum(dy * xh, axis=0, keepdims=True)


def _rms_mm(name, x, g, ws, out_dtype):
    n, d = x.shape
    f = ws[0].shape[1]
    nw = len(ws)
    tm, tn = _pick(n, TILE["up_m"], 16), _pick(f, TILE["up_n"], LANES)

    def body(x_ref, g_ref, *refs):
        w_refs, o_refs, h_ref = refs[:nw], refs[nw:2 * nw], refs[2 * nw]

        @pl.when(pl.program_id(1) == 0)
        def _():
            _, xh = _rms_stats(x_ref[...])
            h_ref[...] = (xh * g_ref[...]).astype(BF16)

        h = h_ref[...]
        for w_ref, o_ref in zip(w_refs, o_refs):
            o_ref[...] = _dot(h, w_ref[...]).astype(o_ref.dtype)

    outs = pl.pallas_call(
        body, name=name, grid=(n // tm, f // tn),
        in_specs=[pl.BlockSpec((tm, d), lambda i, j: (i, 0)), pl.BlockSpec((1, d), lambda i, j: (0, 0))]
        + [pl.BlockSpec((d, tn), lambda i, j: (0, j))] * nw,
        out_specs=[pl.BlockSpec((tm, tn), lambda i, j: (i, j))] * nw + [pl.BlockSpec((tm, d), lambda i, j: (i, 0))],
        out_shape=[_sds((n, f), out_dtype)] * nw + [_sds((n, d), BF16)],
        compiler_params=_cparams("parallel", "arbitrary"),
    )(x, g, *ws)
    return outs[:nw], outs[nw]


def _swiglu_down(name, x, a, b, w2):
    n, d = x.shape
    f = a.shape[1]
    tm = _pick(n, TILE["row"], 16)

    def body(x_ref, a_ref, b_ref, w_ref, o_ref):
        av = a_ref[...].astype(F32)
        hid = (av * _sigmoid(av) * b_ref[...].astype(F32)).astype(BF16)
        o_ref[...] = x_ref[...] + FFN_RES * _dot(hid, w_ref[...])

    return pl.pallas_call(
        body, name=name, grid=(n // tm,),
        in_specs=[pl.BlockSpec((tm, d), lambda i: (i, 0)), pl.BlockSpec((tm, f), lambda i: (i, 0)),
                  pl.BlockSpec((tm, f), lambda i: (i, 0)), pl.BlockSpec((f, d), lambda i: (0, 0))],
        out_specs=pl.BlockSpec((tm, d), lambda i: (i, 0)),
        out_shape=_sds((n, d), F32),
        compiler_params=_cparams("parallel"),
    )(x, a, b, w2)


def _loss_head(name, x, g, target):
    n, d = x.shape
    tm = _pick(n, TILE["row"], SUBLANES)

    def body(x_ref, g_ref, t_ref, dx_ref, loss_ref, dg_ref):
        @pl.when(pl.program_id(0) == 0)
        def _():
            loss_ref[...] = jnp.zeros_like(loss_ref)
            dg_ref[...] = jnp.zeros_like(dg_ref)

        xv, gv = x_ref[...], g_ref[...]
        r, xh = _rms_stats(xv)
        err = xh * gv - t_ref[...]
        loss_ref[...] += 0.5 * jnp.sum(jnp.mean(err * err, axis=-1, keepdims=True))
        dy = err * (1.0 / d)
        dxh = dy * gv
        dx_ref[...] = r * (dxh - xh * jnp.mean(dxh * xh, axis=-1, keepdims=True))
        dg_ref[...] += jnp.sum(dy * xh, axis=0, keepdims=True)

    return pl.pallas_call(
        body, name=name, grid=(n // tm,),
        in_specs=[pl.BlockSpec((tm, d), lambda i: (i, 0)), pl.BlockSpec((1, d), lambda i: (0, 0)),
                  pl.BlockSpec((tm, d), lambda i: (i, 0))],
        out_specs=[pl.BlockSpec((tm, d), lambda i: (i, 0)), pl.BlockSpec((SUBLANES, LANES), lambda i: (0, 0)),
                   pl.BlockSpec((1, d), lambda i: (0, 0))],
        out_shape=[_sds((n, d), F32), _sds((SUBLANES, LANES), F32), _sds((1, d), F32)],
        compiler_params=_cparams("arbitrary"),
    )(x, g, target)


def _ffn_bwd_hidden(name, dxo, a, b, w2):
    n, d = dxo.shape
    f = a.shape[1]
    tm, tn = _pick(n, TILE["up_m"], 16), _pick(f, TILE["up_n"], LANES)

    def body(dx_ref, a_ref, b_ref, w_ref, da_ref, db_ref, hid_ref, dxh_ref):
        @pl.when(pl.program_id(1) == 0)
        def _():
            dxh_ref[...] = (FFN_RES * dx_ref[...]).astype(BF16)

        dhid = _dot_nt(dxh_ref[...], w_ref[...])
        av, bv = a_ref[...].astype(F32), b_ref[...].astype(F32)
        sig = _sigmoid(av)
        silu = av * sig
        da_ref[...] = (dhid * bv * (sig * (1.0 + av * (1.0 - sig)))).astype(BF16)
        db_ref[...] = (dhid * silu).astype(BF16)
        hid_ref[...] = (silu * bv).astype(BF16)

    tile = pl.BlockSpec((tm, tn), lambda i, j: (i, j))
    return pl.pallas_call(
        body, name=name, grid=(n // tm, f // tn),
        in_specs=[pl.BlockSpec((tm, d), lambda i, j: (i, 0)), tile, tile, pl.BlockSpec((tn, d), lambda i, j: (j, 0))],
        out_specs=[tile, tile, tile, pl.BlockSpec((tm, d), lambda i, j: (i, 0))],
        out_shape=[_sds((n, f), BF16)] * 3 + [_sds((n, d), BF16)],
        compiler_params=_cparams("parallel", "arbitrary"),
    )(dxo, a, b, w2)


def _dx_rms_bwd(name, pairs, dxo, x, g):
    n, dm = x.shape
    tm = _pick(n, TILE["row"], 16)
    npair = len(pairs)

    def body(*refs):
        d_refs, w_refs = refs[:npair], refs[npair:2 * npair]
        dxo_ref, x_ref, g_ref, dx_ref, dg_ref = refs[2 * npair:]

        @pl.when(pl.program_id(0) == 0)
        def _():
            dg_ref[...] = jnp.zeros_like(dg_ref)

        dh = None
        for d_ref, w_ref in zip(d_refs, w_refs):
            t = _dot_nt(d_ref[...].astype(BF16), w_ref[...])
            dh = t if dh is None else dh + t
        dx, dg = _rms_bwd(x_ref[...], g_ref[...], dh)
        dx_ref[...] = dxo_ref[...] + dx
        dg_ref[...] += dg

    row = pl.BlockSpec((tm, dm), lambda i: (i, 0))
    d_specs = [pl.BlockSpec((tm, p[1]), functools.partial(lambda i, cb: (i, cb), cb=p[2])) for p in pairs]
    w_specs = [pl.BlockSpec((dm, p[4]), functools.partial(lambda i, cb: (0, cb), cb=p[5])) for p in pairs]
    return pl.pallas_call(
        body, name=name, grid=(n // tm,),
        in_specs=d_specs + w_specs + [row, row, pl.BlockSpec((1, dm), lambda i: (0, 0))],
        out_specs=[row, pl.BlockSpec((1, dm), lambda i: (0, 0))],
        out_shape=[_sds((n, dm), F32), _sds((1, dm), F32)],
        compiler_params=_cparams("arbitrary"),
    )(*[p[0] for p in pairs], *[p[3] for p in pairs], dxo, x, g)


def _mm_tn(name, a, b, a_cols=None, b_cols=None):
    n = a.shape[0]
    a0, ma = a_cols if a_cols else (0, a.shape[1])
    b0, mb = b_cols if b_cols else (0, b.shape[1])
    tm, tn, tk = _pick(ma, TILE["mm_m"], LANES), _pick(mb, TILE["mm_n"], LANES), _pick(n, TILE["mm_k"], 16)
    assert a0 % tm == 0 and b0 % tn == 0
    ab, bb = a0 // tm, b0 // tn

    def body(a_ref, b_ref, o_ref):
        @pl.when(pl.program_id(2) == 0)
        def _():
            o_ref[...] = jnp.zeros_like(o_ref)

        o_ref[...] += _dot_tn(a_ref[...].astype(BF16), b_ref[...].astype(BF16))

    return pl.pallas_call(
        body, name=name, grid=(ma // tm, mb // tn, n // tk),
        in_specs=[pl.BlockSpec((tk, tm), lambda i, j, k: (k, ab + i)), pl.BlockSpec((tk, tn), lambda i, j, k: (k, bb + j))],
        out_specs=pl.BlockSpec((tm, tn), lambda i, j, k: (i, j)),
        out_shape=_sds((ma, mb), F32),
        compiler_params=_cparams("parallel", "parallel", "arbitrary"),
    )(a, b)


def _row_mm(name, pairs, out_w, out_dtype, add=None):
    n = pairs[0][0].shape[0]
    tm = _pick(n, TILE["row"], 16)
    npair = len(pairs)

    def body(*refs):
        a_refs, w_refs = refs[:npair], refs[npair:2 * npair]
        add_ref = refs[2 * npair] if add is not None else None
        o_ref = refs[-1]
        acc = None
        for a_ref, w_ref, p in zip(a_refs, w_refs, pairs):
            av = a_ref[...].astype(BF16)
            t = _dot_nt(av, w_ref[...]) if p[6] else _dot(av, w_ref[...])
            acc = t if acc is None else acc + t
        if add_ref is not None:
            acc = acc + add_ref[...].astype(F32)
        o_ref[...] = acc.astype(o_ref.dtype)

    a_specs = [pl.BlockSpec((tm, p[1]), functools.partial(lambda i, cb: (i, cb), cb=p[2])) for p in pairs]
    w_specs = [pl.BlockSpec((p[4], p[3].shape[1]), functools.partial(lambda i, rb: (rb, 0), rb=p[5])) for p in pairs]
    add_specs = [pl.BlockSpec((tm, out_w), lambda i: (i, 0))] if add is not None else []
    return pl.pallas_call(
        body, name=name, grid=(n // tm,),
        in_specs=a_specs + w_specs + add_specs,
        out_specs=pl.BlockSpec((tm, out_w), lambda i: (i, 0)),
        out_shape=_sds((n, out_w), out_dtype),
        compiler_params=_cparams("parallel"),
    )(*[p[0] for p in pairs], *[p[3] for p in pairs], *([add] if add is not None else []))


def _conv_post(c, ln_g, ln_b, out_g):
    mu = jnp.mean(c, axis=-1, keepdims=True)
    xc = c - mu
    rstd = lax.rsqrt(jnp.mean(xc * xc, axis=-1, keepdims=True) + EPS)
    nrm = xc * rstd
    l = nrm * ln_g + ln_b
    sig = _sigmoid(l)
    s = l * sig
    r, sh = _rms_stats(s)
    return sh * out_g, (rstd, nrm, l, sig, r, sh)


def _conv_taps(a_ref, w_ref, first, rows):
    acc = None
    for k in range(CONV_WIDTH):
        t = w_ref[k:k + 1, :] * a_ref[pl.ds(first + k, rows), :]
        acc = t if acc is None else acc + t
    return acc


def _conv_fwd(name, proj3, conv_w, conv_b, ln_g, ln_b, out_g):
    bsz, seq, _ = proj3.shape
    c = conv_w.shape[1]
    tt = _pick(seq, TILE["conv_t"], CONV_HALO)
    hb = tt // CONV_HALO
    first = CONV_HALO - (CONV_WIDTH - 1)

    def body(v_ref, g_ref, vp_ref, gp_ref, w_ref, cb_ref, lg_ref, lb_ref, og_ref, o_ref, a_ref):
        keep = (pl.program_id(1) > 0).astype(F32)
        a_ref[pl.ds(0, CONV_HALO), :] = keep * vp_ref[0] * _sigmoid(gp_ref[0])
        a_ref[pl.ds(CONV_HALO, tt), :] = v_ref[0] * _sigmoid(g_ref[0])
        cv = _conv_taps(a_ref, w_ref, first, tt) + cb_ref[...]
        out, _ = _conv_post(cv, lg_ref[...], lb_ref[...], og_ref[...])
        o_ref[0] = out.astype(BF16)

    vec = pl.BlockSpec((1, c), lambda b, i: (0, 0))
    prev = lambda col: pl.BlockSpec((1, CONV_HALO, c), lambda b, i: (b, jnp.maximum(i * hb - 1, 0), col))
    return pl.pallas_call(
        body, name=name, grid=(bsz, seq // tt),
        in_specs=[pl.BlockSpec((1, tt, c), lambda b, i: (b, i, 0)), pl.BlockSpec((1, tt, c), lambda b, i: (b, i, 1)),
                  prev(0), prev(1), pl.BlockSpec(conv_w.shape, lambda b, i: (0, 0)), vec, vec, vec, vec],
        out_specs=pl.BlockSpec((1, tt, c), lambda b, i: (b, i, 0)),
        out_shape=_sds((bsz, seq, c), BF16),
        scratch_shapes=[pltpu.VMEM((CONV_HALO + tt, c), F32)],
        compiler_params=_cparams("parallel", "arbitrary"),
    )(proj3, proj3, proj3, proj3, conv_w, conv_b, ln_g, ln_b, out_g)


def _conv_bwd(name, dmix3, proj3, conv_w, conv_b, ln_g, ln_b, out_g):
    bsz, seq, _ = proj3.shape
    c = conv_w.shape[1]
    tt = _pick(seq, TILE["conv_t"], CONV_HALO)
    hb = tt // CONV_HALO
    nt = seq // tt
    last_hb = seq // CONV_HALO - 1
    ext = tt + CONV_HALO
    first = CONV_HALO - (CONV_WIDTH - 1)

    def body(v_ref, g_ref, vp_ref, gp_ref, vn_ref, gn_ref, d_ref, dn_ref, w_ref, cb_ref, lg_ref, lb_ref, og_ref,
             o_ref, dw_ref, dcb_ref, dlg_ref, dlb_ref, dog_ref, a_ref, dc_ref):
        i = pl.program_id(1)

        @pl.when((pl.program_id(0) == 0) & (i == 0))
        def _():
            for r in (dw_ref, dcb_ref, dlg_ref, dlb_ref, dog_ref):
                r[...] = jnp.zeros_like(r)

        keep_prev = (i > 0).astype(F32)
        keep_next = (i < nt - 1).astype(F32)
        sig_g = _sigmoid(g_ref[0])
        a_ref[pl.ds(0, CONV_HALO), :] = keep_prev * vp_ref[0] * _sigmoid(gp_ref[0])
        a_ref[pl.ds(CONV_HALO, tt), :] = v_ref[0] * sig_g
        a_ref[pl.ds(CONV_HALO + tt, CONV_HALO), :] = keep_next * vn_ref[0] * _sigmoid(gn_ref[0])

        lg, og = lg_ref[...], og_ref[...]
        cv = _conv_taps(a_ref, w_ref, first, ext) + cb_ref[...]
        _, (rstd, nrm, l, sig, r, sh) = _conv_post(cv, lg, lb_ref[...], og)
        own = (lax.broadcasted_iota(jnp.int32, (ext, 1), 0) < tt).astype(F32)
        dout = jnp.concatenate([d_ref[0], keep_next * dn_ref[0]], axis=0)
        dog_ref[...] += jnp.sum(own * dout * sh, axis=0, keepdims=True)
        dsh = dout * og
        ds = r * (dsh - sh * jnp.mean(dsh * sh, axis=-1, keepdims=True))
        dl = ds * (sig * (1.0 + l * (1.0 - sig)))
        dlg_ref[...] += jnp.sum(own * dl * nrm, axis=0, keepdims=True)
        dlb_ref[...] += jnp.sum(own * dl, axis=0, keepdims=True)
        dn = dl * lg
        dc = rstd * (dn - jnp.mean(dn, axis=-1, keepdims=True) - nrm * jnp.mean(dn * nrm, axis=-1, keepdims=True))
        dc_ref[...] = dc
        dc_own = dc_ref[pl.ds(0, tt), :]
        dcb_ref[...] += jnp.sum(dc_own, axis=0, keepdims=True)

        da = None
        for k in range(CONV_WIDTH):
            t = w_ref[k:k + 1, :] * dc_ref[pl.ds(CONV_WIDTH - 1 - k, tt), :]
            da = t if da is None else da + t
            dw_ref[k:k + 1, :] += jnp.sum(dc_own * a_ref[pl.ds(first + k, tt), :], axis=0, keepdims=True)
        val = v_ref[0]
        o_ref[0] = jnp.concatenate([da * sig_g, da * val * sig_g * (1.0 - sig_g)], axis=-1).astype(BF16)

    vec = pl.BlockSpec((1, c), lambda b, i: (0, 0))
    cur = lambda col: pl.BlockSpec((1, tt, c), lambda b, i: (b, i, col))
    prev = lambda col: pl.BlockSpec((1, CONV_HALO, c), lambda b, i: (b, jnp.maximum(i * hb - 1, 0), col))
    nxt = lambda col: pl.BlockSpec((1, CONV_HALO, c), lambda b, i: (b, jnp.minimum((i + 1) * hb, last_hb), col))
    wspec = pl.BlockSpec(conv_w.shape, lambda b, i: (0, 0))
    return pl.pallas_call(
        body, name=name, grid=(bsz, nt),
        in_specs=[cur(0), cur(1), prev(0), prev(1), nxt(0), nxt(1), cur(0), nxt(0), wspec, vec, vec, vec, vec],
        out_specs=[pl.BlockSpec((1, tt, 2 * c), lambda b, i: (b, i, 0)), wspec, vec, vec, vec, vec],
        out_shape=[_sds((bsz, seq, 2 * c), BF16), _sds(conv_w.shape, F32)] + [_sds((1, c), F32)] * 4,
        scratch_shapes=[pltpu.VMEM((CONV_HALO + tt + CONV_HALO, c), F32), pltpu.VMEM((ext, c), F32)],
        compiler_params=_cparams("arbitrary", "arbitrary"),
    )(proj3, proj3, proj3, proj3, proj3, proj3, dmix3, dmix3, conv_w, conv_b, ln_g, ln_b, out_g)


def _ssm_discretise(a_re, a_im, log_dt):
    dt = jnp.exp(log_dt)
    zr, zi = a_re * dt, a_im * dt
    mag = jnp.exp(zr)
    ar, ai = mag * jnp.cos(zi), mag * jnp.sin(zi)
    den = a_re * a_re + a_im * a_im
    nr = ar - 1.0
    return ar, ai, (nr * a_re + ai * a_im) / den, (ai * a_re - nr * a_im) / den


def _ssm_system(a_re, a_im, log_dt, a_re_x, a_im_x, log_dt_x, bt_re, bt_im):
    ar, ai, _, _ = _ssm_discretise(a_re, a_im, log_dt)
    _, _, cr, ci = _ssm_discretise(a_re_x, a_im_x, log_dt_x)
    return ar, ai, cr * bt_re - ci * bt_im, cr * bt_im + ci * bt_re


def _ssm_prep(name, prim):
    g, p = prim[0].shape

    def body(*refs):
        pwr_ref, pwi_ref, bbr_ref, bbi_ref = refs[8:]
        ar, ai, bbr, bbi = _ssm_system(*[r[...] for r in refs[:8]])
        bbr_ref[...] = bbr
        bbi_ref[...] = bbi
        pr, pi = ar, ai
        for k in range(SUBLANES):
            pwr_ref[k] = pr
            pwi_ref[k] = pi
            pr, pi = pr * ar - pi * ai, pr * ai + pi * ar

    return pl.pallas_call(
        body, name=name,
        out_shape=[_sds((SUBLANES, g, p), F32)] * 2 + [_sds(prim[6].shape, F32)] * 2,
        compiler_params=pltpu.CompilerParams(vmem_limit_bytes=VMEM_LIMIT),
    )(*prim)


def _ssm_param_grads(name, prim, dab_r, dab_i, dbb_r, dbb_i):
    g, p = prim[0].shape
    h = prim[6].shape[0] // g

    def body(*refs):
        dar_ref, dai_ref, dbr_ref, dbi_ref = refs[8:12]
        o_ar, o_ai, o_dt, o_br, o_bi = refs[12:]
        _, vjp = jax.vjp(_ssm_system, *[r[...] for r in refs[:8]])
        ct = (jnp.sum(dar_ref[...], axis=0), jnp.sum(dai_ref[...], axis=0), dbr_ref[...], dbi_ref[...])
        d_ar, d_ai, d_dt, d_arx, d_aix, d_dtx, d_br, d_bi = vjp(ct)
        per_group = lambda t: jnp.sum(t.reshape(g, h, p), axis=1)
        o_ar[...] = d_ar + per_group(d_arx)
        o_ai[...] = d_ai + per_group(d_aix)
        o_dt[...] = d_dt + jnp.sum(per_group(d_dtx), axis=1, keepdims=True)
        o_br[...] = d_br
        o_bi[...] = d_bi

    return pl.pallas_call(
        body, name=name,
        out_shape=[_sds(prim[k].shape, F32) for k in (0, 1, 2, 6, 7)],
        compiler_params=pltpu.CompilerParams(vmem_limit_bytes=VMEM_LIMIT),
    )(*prim, dab_r, dab_i, dbb_r, dbb_i)


def _cfma(xr, xi, cr, ci, sr, si):
    return xr + (cr * sr - ci * si), xi + (cr * si + ci * sr)


def _scan_tables(pw_r, pw_i, reverse):
    gp = pw_r.shape[1] * pw_r.shape[2]
    pr, pi = pw_r.reshape(SUBLANES, gp), pw_i.reshape(SUBLANES, gp)
    if reverse:
        pi = -pi
    row = jnp.arange(SUBLANES)[:, None]
    tabs = []
    for d in (1, 2, 4):
        keep = (row < SUBLANES - d) if reverse else (row >= d)
        tabs += [jnp.where(keep, pr[d - 1][None, :], 0.0), jnp.where(keep, pi[d - 1][None, :], 0.0)]
    tabs += [pr[::-1], pi[::-1]] if reverse else [pr, pi]
    return jnp.concatenate(tabs, axis=0)


def _scan_fwd(name, tab, bu3):
    bsz, seq, w = bu3.shape
    gp = w // 2
    tt = _pick(seq, TILE["scan_t"], SUBLANES)
    nblk = tt // SUBLANES

    def body(tab_ref, bu_ref, xs_ref, carry_ref):
        @pl.when(pl.program_id(1) == 0)
        def _():
            carry_ref[...] = jnp.zeros_like(carry_ref)

        for ch in range(gp // LANES):
            re, im = pl.ds(ch * LANES, LANES), pl.ds(gp + ch * LANES, LANES)
            tabs = [tab_ref[pl.ds(SUBLANES * k, SUBLANES), re] for k in range(8)]

            def blk(r, carry, re=re, im=im, tabs=tabs):
                rows = pl.ds(pl.multiple_of(r * SUBLANES, SUBLANES), SUBLANES)
                xr, xi = bu_ref[0, rows, re], bu_ref[0, rows, im]
                for j, d in enumerate((1, 2, 4)):
                    xr, xi = _cfma(xr, xi, tabs[2 * j], tabs[2 * j + 1], pltpu.roll(xr, d, 0), pltpu.roll(xi, d, 0))
                xr, xi = _cfma(xr, xi, tabs[6], tabs[7], carry[0], carry[1])
                xs_ref[0, rows, re] = xr
                xs_ref[0, rows, im] = xi
                last = SUBLANES - 1
                return (jnp.broadcast_to(xr[last:, :], xr.shape), jnp.broadcast_to(xi[last:, :], xi.shape))

            cr, ci = lax.fori_loop(0, nblk, blk, (carry_ref[:, re], carry_ref[:, im]))
            carry_ref[:, re] = cr
            carry_ref[:, im] = ci

    return pl.pallas_call(
        body, name=name, grid=(bsz, seq // tt),
        in_specs=[pl.BlockSpec(tab.shape, lambda b, t: (0, 0)), pl.BlockSpec((1, tt, w), lambda b, t: (b, t, 0))],
        out_specs=pl.BlockSpec((1, tt, w), lambda b, t: (b, t, 0)),
        out_shape=_sds(bu3.shape, F32),
        scratch_shapes=[pltpu.VMEM((SUBLANES, w), F32)],
        compiler_params=_cparams("arbitrary", "arbitrary"),
    )(tab, bu3)


def _scan_bwd(name, tab, g3, xs3):
    bsz, seq, w = g3.shape
    gp = w // 2
    tt = _pick(seq, TILE["scan_t"], SUBLANES)
    nblk = tt // SUBLANES
    nt = seq // tt

    def body(tab_ref, g_ref, xs_ref, halo_ref, lam_ref, dar_ref, dai_ref, carry_ref):
        t = pl.program_id(1)

        @pl.when(t == 0)
        def _():
            carry_ref[...] = jnp.zeros_like(carry_ref)

        @pl.when((pl.program_id(0) == 0) & (t == 0))
        def _():
            dar_ref[...] = jnp.zeros_like(dar_ref)
            dai_ref[...] = jnp.zeros_like(dai_ref)

        has_prev = (t < nt - 1).astype(F32)
        row0 = lax.broadcasted_iota(jnp.int32, (SUBLANES, LANES), 0) == 0
        last = SUBLANES - 1

        for ch in range(gp // LANES):
            re, im = pl.ds(ch * LANES, LANES), pl.ds(gp + ch * LANES, LANES)
            tabs = [tab_ref[pl.ds(SUBLANES * k, SUBLANES), re] for k in range(8)]

            def step(rows, xm1r, xm1i, state, re=re, im=im, tabs=tabs):
                cr, ci, accr, acci = state
                lr, li = g_ref[0, rows, re], g_ref[0, rows, im]
                for j, d in enumerate((1, 2, 4)):
                    lr, li = _cfma(lr, li, tabs[2 * j], tabs[2 * j + 1],
                                   pltpu.roll(lr, SUBLANES - d, 0), pltpu.roll(li, SUBLANES - d, 0))
                lr, li = _cfma(lr, li, tabs[6], tabs[7], cr, ci)
                lam_ref[0, rows, re] = lr
                lam_ref[0, rows, im] = li
                xr, xi = xs_ref[0, rows, re], xs_ref[0, rows, im]
                xpr = jnp.where(row0, jnp.broadcast_to(xm1r[last:, :], xr.shape), pltpu.roll(xr, 1, 0))
                xpi = jnp.where(row0, jnp.broadcast_to(xm1i[last:, :], xi.shape), pltpu.roll(xi, 1, 0))
                accr = accr + (lr * xpr + li * xpi)
                acci = acci + (li * xpr - lr * xpi)
                return (jnp.broadcast_to(lr[:1, :], lr.shape), jnp.broadcast_to(li[:1, :], li.shape), accr, acci)

            def blk(k, state, re=re, im=im, step=step):
                r = nblk - 1 - k
                rows = pl.ds(pl.multiple_of(r * SUBLANES, SUBLANES), SUBLANES)
                prev = pl.ds(pl.multiple_of((r - 1) * SUBLANES, SUBLANES), SUBLANES)
                return step(rows, xs_ref[0, prev, re], xs_ref[0, prev, im], state)

            zero = jnp.zeros((SUBLANES, LANES), F32)
            state = lax.fori_loop(0, nblk - 1, blk, (carry_ref[:, re], carry_ref[:, im], zero, zero))
            cr, ci, accr, acci = step(pl.ds(0, SUBLANES), has_prev * halo_ref[0, :, re], has_prev * halo_ref[0, :, im], state)
            carry_ref[:, re] = cr
            carry_ref[:, im] = ci
            dar_ref[:, pl.ds(ch * LANES, LANES)] += accr
            dai_ref[:, pl.ds(ch * LANES, LANES)] += acci

    tile = pl.BlockSpec((1, tt, w), lambda b, t: (b, nt - 1 - t, 0))
    halo = pl.BlockSpec((1, SUBLANES, w), lambda b, t: (b, jnp.maximum((nt - 1 - t) * nblk - 1, 0), 0))
    acc = pl.BlockSpec((SUBLANES, gp), lambda b, t: (0, 0))
    return pl.pallas_call(
        body, name=name, grid=(bsz, nt),
        in_specs=[pl.BlockSpec(tab.shape, lambda b, t: (0, 0)), tile, tile, halo],
        out_specs=[tile, acc, acc],
        out_shape=[_sds(g3.shape, F32), _sds((SUBLANES, gp), F32), _sds((SUBLANES, gp), F32)],
        scratch_shapes=[pltpu.VMEM((SUBLANES, w), F32)],
        compiler_params=_cparams("arbitrary", "arbitrary"),
    )(tab, g3, xs3, xs3)


def _gelu_parts(y):
    inner = _GELU_K * (y + _GELU_C * y * y * y)
    t = jnp.tanh(inner)
    return 0.5 * y * (1.0 + t), t


def _ssm_out_fwd(name, xs, proj, u_block, cdt, d_skip, glu_w, glu_b, out_g):
    n, w = xs.shape
    c = cdt.shape[0]
    tm = _pick(n, TILE["row"], 16)

    def body(xs_ref, u_ref, cdt_ref, d_ref, gw_ref, gb_ref, og_ref, y_ref, o_ref):
        y = _dot_nt(xs_ref[...].astype(BF16), cdt_ref[...]) + d_ref[...] * u_ref[...]
        y_ref[...] = y
        gy, _ = _gelu_parts(y)
        z = _dot(gy.astype(BF16), gw_ref[...]) + gb_ref[...]
        _, sh = _rms_stats(gy * _sigmoid(z))
        o_ref[...] = (sh * og_ref[...]).astype(BF16)

    vec = pl.BlockSpec((1, c), lambda i: (0, 0))
    row = pl.BlockSpec((tm, c), lambda i: (i, 0))
    return pl.pallas_call(
        body, name=name, grid=(n // tm,),
        in_specs=[pl.BlockSpec((tm, w), lambda i: (i, 0)), pl.BlockSpec((tm, c), lambda i: (i, u_block)),
                  pl.BlockSpec(cdt.shape, lambda i: (0, 0)), vec, pl.BlockSpec(glu_w.shape, lambda i: (0, 0)), vec, vec],
        out_specs=[row, row],
        out_shape=[_sds((n, c), F32), _sds((n, c), BF16)],
        compiler_params=_cparams("parallel"),
    )(xs, proj, cdt, d_skip, glu_w, glu_b, out_g)


def _ssm_out_bwd(name, dmix, d_block, y, proj, u_block, d_skip, glu_w, glu_b, out_g):
    n, c = y.shape
    tm = _pick(n, TILE["row"], 16)

    def body(d_ref, y_ref, u_ref, dk_ref, gw_ref, gb_ref, og_ref, dy_ref, du_ref, dgw_ref, dgb_ref, dog_ref, dd_ref):
        @pl.when(pl.program_id(0) == 0)
        def _():
            for r in (dgw_ref, dgb_ref, dog_ref, dd_ref):
                r[...] = jnp.zeros_like(r)

        yv = y_ref[...]
        gy, th = _gelu_parts(yv)
        gy16 = gy.astype(BF16)
        sz = _sigmoid(_dot(gy16, gw_ref[...]) + gb_ref[...])
        r, sh = _rms_stats(gy * sz)
        dout = d_ref[...]
        dog_ref[...] += jnp.sum(dout * sh, axis=0, keepdims=True)
        dsh = dout * og_ref[...]
        ds = r * (dsh - sh * jnp.mean(dsh * sh, axis=-1, keepdims=True))
        dz = ds * gy * sz * (1.0 - sz)
        dz16 = dz.astype(BF16)
        dgb_ref[...] += jnp.sum(dz, axis=0, keepdims=True)
        dgw_ref[...] += _dot_tn(gy16, dz16)
        dgy = ds * sz + _dot_nt(dz16, gw_ref[...])
        dgelu = 0.5 * (1.0 + th) + 0.5 * yv * (1.0 - th * th) * (_GELU_K * (1.0 + 3.0 * _GELU_C * yv * yv))
        dy = dgy * dgelu
        dy_ref[...] = dy.astype(BF16)
        du_ref[...] = dy * dk_ref[...]
        dd_ref[...] += jnp.sum(dy * u_ref[...], axis=0, keepdims=True)

    vec = pl.BlockSpec((1, c), lambda i: (0, 0))
    row = pl.BlockSpec((tm, c), lambda i: (i, 0))
    mat = pl.BlockSpec(glu_w.shape, lambda i: (0, 0))
    return pl.pallas_call(
        body, name=name, grid=(n // tm,),
        in_specs=[pl.BlockSpec((tm, c), lambda i: (i, d_block)), row, pl.BlockSpec((tm, c), lambda i: (i, u_block)),
                  vec, mat, vec, vec],
        out_specs=[row, row, mat, vec, vec, vec],
        out_shape=[_sds((n, c), BF16), _sds((n, c), F32), _sds(glu_w.shape, F32)] + [_sds((1, c), F32)] * 3,
        compiler_params=_cparams("arbitrary"),
    )(dmix, y, proj, d_skip, glu_w, glu_b, out_g)


def _peer(k):
    x, y, c = (lax.axis_index(a) for a in MESH_AXES)
    px = 1 - x if k & 4 else x
    py = 1 - y if k & 2 else y
    pc = 1 - c if k & 1 else c
    return (px, py, pc), 4 * px + 2 * py + pc


def _all_gather(name, block):
    def body(x_ref, o_ref, send_sems, recv_sems, local_sem):
        _, me = _peer(0)
        own = pltpu.make_async_copy(x_ref, o_ref.at[me], local_sem)
        own.start()
        copies = []
        for k in range(1, N_DEV):
            dev, _ = _peer(k)
            cp = pltpu.make_async_remote_copy(src_ref=x_ref, dst_ref=o_ref.at[me], send_sem=send_sems.at[k - 1],
                                              recv_sem=recv_sems.at[k - 1], device_id=dev,
                                              device_id_type=pl.DeviceIdType.MESH)
            cp.start()
            copies.append(cp)
        for k in range(1, N_DEV):
            _, src = _peer(k)
            pltpu.make_async_remote_copy(src_ref=x_ref, dst_ref=o_ref.at[src], send_sem=send_sems.at[k - 1],
                                         recv_sem=recv_sems.at[k - 1], device_id=_peer(k)[0],
                                         device_id_type=pl.DeviceIdType.MESH).wait_recv()
        for cp in copies:
            cp.wait_send()
        own.wait()

    return pl.pallas_call(
        body, name=name,
        in_specs=[pl.BlockSpec(memory_space=pltpu.HBM)],
        out_specs=pl.BlockSpec(memory_space=pltpu.HBM),
        out_shape=_sds((N_DEV,) + block.shape, block.dtype),
        scratch_shapes=[pltpu.SemaphoreType.DMA((N_DEV - 1,)), pltpu.SemaphoreType.DMA((N_DEV - 1,)),
                        pltpu.SemaphoreType.DMA],
    )(block)


def _exchange_blocks(name, blocks):
    def body(x_ref, o_ref, send_sems, recv_sems, local_sem):
        _, me = _peer(0)
        own = pltpu.make_async_copy(x_ref.at[me], o_ref.at[me], local_sem)
        own.start()
        copies = []
        for k in range(1, N_DEV):
            dev, idx = _peer(k)
            cp = pltpu.make_async_remote_copy(src_ref=x_ref.at[idx], dst_ref=o_ref.at[me], send_sem=send_sems.at[k - 1],
                                              recv_sem=recv_sems.at[k - 1], device_id=dev,
                                              device_id_type=pl.DeviceIdType.MESH)
            cp.start()
            copies.append(cp)
        for k in range(1, N_DEV):
            dev, src = _peer(k)
            pltpu.make_async_remote_copy(src_ref=x_ref.at[src], dst_ref=o_ref.at[src], send_sem=send_sems.at[k - 1],
                                         recv_sem=recv_sems.at[k - 1], device_id=dev,
                                         device_id_type=pl.DeviceIdType.MESH).wait_recv()
        for cp in copies:
            cp.wait_send()
        own.wait()

    return pl.pallas_call(
        body, name=name,
        in_specs=[pl.BlockSpec(memory_space=pltpu.HBM)],
        out_specs=pl.BlockSpec(memory_space=pltpu.HBM),
        out_shape=_sds(blocks.shape, blocks.dtype),
        scratch_shapes=[pltpu.SemaphoreType.DMA((N_DEV - 1,)), pltpu.SemaphoreType.DMA((N_DEV - 1,)),
                        pltpu.SemaphoreType.DMA],
    )(blocks)


def _adamw(name, parts, w, m, v):
    rows = w.shape[0]
    tr = _pick(rows, TILE["adam_r"], SUBLANES)
    c1 = 1.0 - ADAM_B1 ** ADAM_STEP
    c2 = 1.0 - ADAM_B2 ** ADAM_STEP

    def body(p_ref, w_ref, m_ref, v_ref, g_ref, d_ref, nm_ref, nv_ref):
        g = p_ref[0]
        for k in range(1, N_DEV):
            g = g + p_ref[k]
        nm = ADAM_B1 * m_ref[...] + (1.0 - ADAM_B1) * g
        nv = ADAM_B2 * v_ref[...] + (1.0 - ADAM_B2) * (g * g)
        g_ref[...] = g
        nm_ref[...] = nm
        nv_ref[...] = nv
        d_ref[...] = -ADAM_LR * ((nm / c1) / (jnp.sqrt(nv / c2) + ADAM_EPS) + ADAM_WD * w_ref[...])

    row = pl.BlockSpec((tr, PACK_W), lambda i: (i, 0))
    return pl.pallas_call(
        body, name=name, grid=(rows // tr,),
        in_specs=[pl.BlockSpec((N_DEV, tr, PACK_W), lambda i: (0, i, 0)), row, row, row],
        out_specs=[row] * 4,
        out_shape=[_sds((rows, PACK_W), F32)] * 4,
        compiler_params=_cparams("parallel"),
    )(parts, w, m, v)


def _pack(pieces, row_mult, lead=()):
    nl = len(lead)
    flat, spans, off = [], [], 0
    for p in pieces:
        p = p.reshape(lead + (-1,))
        size = p.shape[-1]
        padded = -(-size // PACK_W) * PACK_W
        flat.append(jnp.pad(p, [(0, 0)] * nl + [(0, padded - size)]))
        spans.append((off, size))
        off += padded
    rows = -(-(off // PACK_W) // row_mult) * row_mult
    if rows * PACK_W > off:
        flat.append(jnp.zeros(lead + (rows * PACK_W - off,), flat[0].dtype))
    return jnp.concatenate(flat, axis=-1).reshape(lead + (rows, PACK_W)), spans


def _unpack(buf, spans, shapes, lead=0):
    flat = buf.reshape(buf.shape[:lead] + (-1,))
    return [flat[..., o:o + s].reshape(buf.shape[:lead] + tuple(shape)) for (o, s), shape in zip(spans, shapes)]


def _cols_to_blocks(full):
    d0 = full.shape[0]
    return full.reshape(d0, N_DEV, -1).transpose(1, 0, 2)


def _blocks_to_cols(blocks):
    return blocks.transpose(1, 0, 2).reshape(blocks.shape[1], -1)


def _block_diag(rows_gh, groups):
    gh, p = rows_gh.shape
    own = (jnp.arange(gh)[:, None] // (gh // groups) == jnp.arange(groups)[None, :]).astype(rows_gh.dtype)
    return (own[:, :, None] * rows_gh[:, None, :]).reshape(gh, groups * p)


def _block_diag_take(dense, groups):
    gh = dense.shape[0]
    p = dense.shape[1] // groups
    own = (jnp.arange(gh)[:, None] // (gh // groups) == jnp.arange(groups)[None, :]).astype(dense.dtype)
    return jnp.sum(dense.reshape(gh, groups, p) * own[:, :, None], axis=1)


BIG = ("ffn1_w1", "ffn1_w3", "ffn1_w2", "w_in", "ssm_glu_w", "w_out", "ffn2_w1", "ffn2_w3", "ffn2_w2")
COL_SHARDED = ("ffn1_w1", "ffn1_w3", "w_in", "ffn2_w1", "ffn2_w3", "conv_w")
SMALL = ("norm_ffn1", "norm_mix", "conv_b", "conv_ln_g", "conv_ln_b", "conv_out_g", "ssm_A_re", "ssm_A_im",
         "ssm_log_dt", "ssm_B_re", "ssm_B_im", "ssm_C_re", "ssm_C_im", "ssm_D", "ssm_glu_b", "ssm_out_g",
         "norm_ffn2", "norm_final")
WEIGHTS = ("norm_ffn1", "ffn1_w1", "ffn1_w3", "ffn1_w2", "norm_mix", "w_in", "conv_w", "conv_b", "conv_ln_g",
           "conv_ln_b", "conv_out_g", "ssm_A_re", "ssm_A_im", "ssm_log_dt", "ssm_B_re", "ssm_B_im", "ssm_C_re",
           "ssm_C_im", "ssm_D", "ssm_glu_w", "ssm_glu_b", "ssm_out_g", "w_out", "norm_ffn2", "ffn2_w1", "ffn2_w3",
           "ffn2_w2", "norm_final")


def _ffn_fwd(tag, x, g, w1, w3, w2):
    (a, b), h = _rms_mm(tag + "_up", x, g, [w1, w3], BF16)
    return _swiglu_down(tag + "_down", x, a, b, w2), (a, b, h)


def _ffn_bwd(tag, dxo, x, g, w1, w3, w2, saved):
    a, b, h = saved
    da, db, hid, dxh = _ffn_bwd_hidden(tag + "_bwd_hidden", dxo, a, b, w2)
    f = a.shape[1]
    dx, dg = _dx_rms_bwd(tag + "_bwd_dx", [(da, f, 0, w1, f, 0), (db, f, 0, w3, f, 0)], dxo, x, g)
    return dx, dg, _mm_tn(tag + "_dw1", h, da), _mm_tn(tag + "_dw3", h, db), _mm_tn(tag + "_dw2", hid, dxh)


def kernel(x, norm_ffn1, ffn1_w1, ffn1_w3, ffn1_w2, norm_mix, w_in, conv_w, conv_b, conv_ln_g, conv_ln_b, conv_out_g, ssm_A_re, ssm_A_im, ssm_log_dt, ssm_B_re, ssm_B_im, ssm_C_re, ssm_C_im, ssm_D, ssm_glu_w, ssm_glu_b, ssm_out_g, w_out, norm_ffn2, ffn2_w1, ffn2_w3, ffn2_w2, norm_final, loss_target, m_norm_ffn1, m_ffn1_w1, m_ffn1_w3, m_ffn1_w2, m_norm_mix, m_w_in, m_conv_w, m_conv_b, m_conv_ln_g, m_conv_ln_b, m_conv_out_g, m_ssm_A_re, m_ssm_A_im, m_ssm_log_dt, m_ssm_B_re, m_ssm_B_im, m_ssm_C_re, m_ssm_C_im, m_ssm_D, m_ssm_glu_w, m_ssm_glu_b, m_ssm_out_g, m_w_out, m_norm_ffn2, m_ffn2_w1, m_ffn2_w3, m_ffn2_w2, m_norm_final, v_norm_ffn1, v_ffn1_w1, v_ffn1_w3, v_ffn1_w2, v_norm_mix, v_w_in, v_conv_w, v_conv_b, v_conv_ln_g, v_conv_ln_b, v_conv_out_g, v_ssm_A_re, v_ssm_A_im, v_ssm_log_dt, v_ssm_B_re, v_ssm_B_im, v_ssm_C_re, v_ssm_C_im, v_ssm_D, v_ssm_glu_w, v_ssm_glu_b, v_ssm_out_g, v_w_out, v_norm_ffn2, v_ffn2_w1, v_ffn2_w3, v_ffn2_w2, v_norm_final):
    args = dict(locals())
    wt = {n: args[n] for n in WEIGHTS}
    mom = {n: args["m_" + n] for n in WEIGHTS}
    var = {n: args["v_" + n] for n in WEIGHTS}

    bsz, seq, d = x.shape
    n = bsz * seq
    c = conv_b.shape[-1]
    groups = c // SSM_GROUP
    gp = groups * SSM_STATE
    val_b, gate_b, u_b = 0, 1, 2

    shard = {k: wt[k][0] for k in BIG + ("conv_w",)}
    pieces = [shard[k].astype(BF16) for k in BIG] + [lax.bitcast_convert_type(shard["conv_w"], BF16)]
    packed, spans = _pack(pieces, 16)
    gathered = _all_gather("gather_weights", packed)
    parts = _unpack(gathered, spans, [p.shape for p in pieces], lead=1)
    full = {}
    for k, p in zip(BIG, parts):
        full[k] = _blocks_to_cols(p) if k in COL_SHARDED else p.reshape(-1, p.shape[-1])
    conv_w_full = _blocks_to_cols(lax.bitcast_convert_type(parts[-1], F32))
    conv_w_pad = jnp.pad(conv_w_full, ((0, CONV_HALO - CONV_WIDTH), (0, 0)))

    vec = lambda k: wt[k].reshape(1, -1)
    g_ffn1, g_mix, g_ffn2, g_fin = vec("norm_ffn1"), vec("norm_mix"), vec("norm_ffn2"), vec("norm_final")
    cb, lng, lnb, cog = vec("conv_b"), vec("conv_ln_g"), vec("conv_ln_b"), vec("conv_out_g")
    d_skip, glu_b, sog = vec("ssm_D"), vec("ssm_glu_b"), vec("ssm_out_g")

    a_re, a_im = wt["ssm_A_re"][0], wt["ssm_A_im"][0]
    log_dt = wt["ssm_log_dt"][0].reshape(groups, 1)
    bt_re = wt["ssm_B_re"][0].transpose(0, 2, 1).reshape(groups * SSM_GROUP, SSM_STATE)
    bt_im = wt["ssm_B_im"][0].transpose(0, 2, 1).reshape(groups * SSM_GROUP, SSM_STATE)
    c_re = wt["ssm_C_re"][0].reshape(groups * SSM_GROUP, SSM_STATE)
    c_im = wt["ssm_C_im"][0].reshape(groups * SSM_GROUP, SSM_STATE)
    per_chan = lambda t: jnp.repeat(t, SSM_GROUP, axis=0)
    ssm_prim = (a_re, a_im, log_dt, per_chan(a_re), per_chan(a_im), per_chan(jnp.broadcast_to(log_dt, a_re.shape)),
                bt_re, bt_im)
    pw_r, pw_i, bb_r, bb_i = _ssm_prep("ssm_prep", ssm_prim)
    tab_f = _scan_tables(pw_r, pw_i, False)
    tab_b = _scan_tables(pw_r, pw_i, True)
    bbd = jnp.concatenate([_block_diag(bb_r, groups), _block_diag(bb_i, groups)], axis=1).astype(BF16)
    cdt = jnp.concatenate([_block_diag(c_re, groups), -_block_diag(c_im, groups)], axis=1).astype(BF16)

    x0 = x.reshape(n, d)
    x1, ffn1_saved = _ffn_fwd("ffn1", x0, g_ffn1, full["ffn1_w1"], full["ffn1_w3"], full["ffn1_w2"])
    (proj,), h2 = _rms_mm("mix_in", x1, g_mix, [full["w_in"]], F32)
    proj3 = proj.reshape(bsz, seq, 3 * c)
    an = _conv_fwd("conv_fwd", proj3, conv_w_pad, cb, lng, lnb, cog).reshape(n, c)
    bu = _row_mm("ssm_bu", [(proj, c, u_b, bbd, c, 0, False)], 2 * gp, F32)
    xs = _scan_fwd("scan_fwd", tab_f, bu.reshape(bsz, seq, 2 * gp)).reshape(n, 2 * gp)
    y, sn = _ssm_out_fwd("ssm_out_fwd", xs, proj, u_b, cdt, d_skip, full["ssm_glu_w"], glu_b, sog)
    w_o = full["w_out"]
    x2 = _row_mm("mix_out", [(an, c, 0, w_o, c, 0, False), (sn, c, 0, w_o, c, 1, False)], d, F32, add=x1)
    x3, ffn2_saved = _ffn_fwd("ffn2", x2, g_ffn2, full["ffn2_w1"], full["ffn2_w3"], full["ffn2_w2"])
    dx3, loss_tile, d_gfin = _loss_head("loss_head", x3, g_fin, loss_target.reshape(n, d))
    loss = lax.psum(loss_tile[0, 0], MESH_AXES)

    grads = {}
    dx2, grads["norm_ffn2"], grads["ffn2_w1"], grads["ffn2_w3"], grads["ffn2_w2"] = _ffn_bwd(
        "ffn2", dx3, x2, g_ffn2, full["ffn2_w1"], full["ffn2_w3"], full["ffn2_w2"], ffn2_saved)

    dmix = _row_mm("mix_out_bwd", [(dx2, d, 0, w_o, 2 * c, 0, True)], 2 * c, F32)
    grads["w_out"] = jnp.concatenate([_mm_tn("dw_out_a", an, dx2), _mm_tn("dw_out_s", sn, dx2)], axis=0)

    dy, du_skip, grads["ssm_glu_w"], grads["ssm_glu_b"], grads["ssm_out_g"], grads["ssm_D"] = _ssm_out_bwd(
        "ssm_out_bwd", dmix, 1, y, proj, u_b, d_skip, full["ssm_glu_w"], glu_b, sog)
    gx = _row_mm("ssm_dx", [(dy, c, 0, cdt, c, 0, False)], 2 * gp, F32)
    lam3, dab_r, dab_i = _scan_bwd("scan_bwd", tab_b, gx.reshape(bsz, seq, 2 * gp), xs.reshape(bsz, seq, 2 * gp))
    lam = lam3.reshape(n, 2 * gp)
    du = _row_mm("ssm_du", [(lam, 2 * gp, 0, bbd, c, 0, True)], c, BF16, add=du_skip)
    d_bbd = _mm_tn("ssm_dbb", proj, lam, a_cols=(u_b * c, c))
    d_cdt = _mm_tn("ssm_dc", dy, xs)
    d_are, d_aim, d_ldt, d_btr, d_bti = _ssm_param_grads(
        "ssm_param_grads", ssm_prim,
        dab_r.reshape(SUBLANES, groups, SSM_STATE), dab_i.reshape(SUBLANES, groups, SSM_STATE),
        _block_diag_take(d_bbd[:, :gp], groups), _block_diag_take(d_bbd[:, gp:], groups))
    grads["ssm_A_re"], grads["ssm_A_im"], grads["ssm_log_dt"] = d_are, d_aim, d_ldt
    grads["ssm_B_re"] = d_btr.reshape(groups, SSM_GROUP, SSM_STATE).transpose(0, 2, 1)
    grads["ssm_B_im"] = d_bti.reshape(groups, SSM_GROUP, SSM_STATE).transpose(0, 2, 1)
    grads["ssm_C_re"] = _block_diag_take(d_cdt[:, :gp], groups)
    grads["ssm_C_im"] = -_block_diag_take(d_cdt[:, gp:], groups)

    dconv3, d_cw, grads["conv_b"], grads["conv_ln_g"], grads["conv_ln_b"], grads["conv_out_g"] = _conv_bwd(
        "conv_bwd", dmix.reshape(bsz, seq, 2 * c), proj3, conv_w_pad, cb, lng, lnb, cog)
    dconv = dconv3.reshape(n, 2 * c)
    grads["conv_w"] = d_cw[:CONV_WIDTH]
    grads["w_in"] = jnp.concatenate([_mm_tn("dw_in_conv", h2, dconv), _mm_tn("dw_in_ssm", h2, du)], axis=1)
    w_i = full["w_in"]
    dx1, grads["norm_mix"] = _dx_rms_bwd("mix_in_bwd", [(dconv, 2 * c, 0, w_i, 2 * c, 0), (du, c, 0, w_i, c, 2)], dx2, x1, g_mix)

    dx0, grads["norm_ffn1"], grads["ffn1_w1"], grads["ffn1_w3"], grads["ffn1_w2"] = _ffn_bwd(
        "ffn1", dx1, x0, g_ffn1, full["ffn1_w1"], full["ffn1_w3"], full["ffn1_w2"], ffn1_saved)
    grads["norm_final"] = d_gfin

    big_names = BIG + ("conv_w",)
    blocks = [_cols_to_blocks(grads[k]) if k in COL_SHARDED else grads[k].reshape((N_DEV, -1) + grads[k].shape[1:])
              for k in big_names]
    send, gspans = _pack(blocks, SUBLANES * 24, lead=(N_DEV,))
    recv = _exchange_blocks("exchange_grads", send)
    w_pk, _ = _pack([wt[k][0] for k in big_names], SUBLANES * 24)
    m_pk, _ = _pack([mom[k][0] for k in big_names], SUBLANES * 24)
    v_pk, _ = _pack([var[k][0] for k in big_names], SUBLANES * 24)
    big_out = _adamw("adamw_sharded", recv, w_pk, m_pk, v_pk)
    big_shapes = [wt[k].shape for k in big_names]
    big_res = [dict(zip(big_names, _unpack(o, gspans, big_shapes))) for o in big_out]

    part, sspans = _pack([grads[k] for k in SMALL], SUBLANES)
    all_parts = _all_gather("gather_small_grads", part)
    ws_pk, _ = _pack([wt[k] for k in SMALL], SUBLANES)
    ms_pk, _ = _pack([mom[k] for k in SMALL], SUBLANES)
    vs_pk, _ = _pack([var[k] for k in SMALL], SUBLANES)
    small_out = _adamw("adamw_replicated", all_parts, ws_pk, ms_pk, vs_pk)
    small_shapes = [wt[k].shape for k in SMALL]
    small_res = [dict(zip(SMALL, _unpack(o, sspans, small_shapes))) for o in small_out]

    res = [{**b, **s} for b, s in zip(big_res, small_res)]
    outs = [loss, dx0.reshape(bsz, seq, d)]
    for r in res:
        outs += [r[k] for k in WEIGHTS]
    return tuple(outs)
```

```python
import functools
import math

import jax
import jax.numpy as jnp
from jax import lax
from jax.experimental import pallas as pl
from jax.experimental.pallas import tpu as pltpu

F32 = jnp.float32
BF16 = jnp.bfloat16

EPS = 1e-6
FFN_RES = 0.5
CONV_WIDTH = 31
CONV_HALO = 32
SSM_GROUP = 16
SSM_STATE = 64
ADAM_LR, ADAM_B1, ADAM_B2, ADAM_EPS, ADAM_WD, ADAM_STEP = 0.001, 0.9, 0.999, 1e-08, 0.01, 10

N_DEV = 8
MESH_AXES = ("x", "y", "c")
SUBLANES = 8
LANES = 128
PACK_W = 1024
V7X_VMEM_BYTES = 64 * 2**20
VMEM_LIMIT = V7X_VMEM_BYTES - 8 * 2**20

TILE = dict(row=256, mm_m=512, mm_n=512, mm_k=1024, up_m=512, up_n=256, conv_t=512, scan_t=256, sum_bytes=4 * 2**20)

_GELU_K = math.sqrt(2.0 / math.pi)
_GELU_C = 0.044715


def _pick(n, target, mult):
    best = None
    for t in range(mult, min(n, target) + 1, mult):
        if n % t == 0:
            best = t
    return n if best is None else best


def _cparams(*sem):
    return pltpu.CompilerParams(dimension_semantics=sem, vmem_limit_bytes=VMEM_LIMIT)


def _sds(shape, dtype):
    return jax.ShapeDtypeStruct(shape, dtype)


def _dot(a, b):
    return jnp.dot(a, b, preferred_element_type=F32)


def _dot_nt(a, b):
    return lax.dot_general(a, b, (((1,), (1,)), ((), ())), preferred_element_type=F32)


def _dot_tn(a, b):
    return lax.dot_general(a, b, (((0,), (0,)), ((), ())), preferred_element_type=F32)


def _sigmoid(x):
    return 1.0 / (1.0 + jnp.exp(-x))


def _rms_stats(x):
    r = lax.rsqrt(jnp.mean(x * x, axis=-1, keepdims=True) + EPS)
    return r, x * r


def _rms_bwd(x, g, dy):
    r, xh = _rms_stats(x)
    dxh = dy * g
    dx = r * (dxh - xh * jnp.mean(dxh * xh, axis=-1, keepdims=True))
    return dx, jnp.sum(dy * xh, axis=0, keepdims=True)


def _rms_mm(name, x, g, ws, out_dtype):
    n, d = x.shape
    f = ws[0].shape[0]
    nw = len(ws)
    tm, tn = _pick(n, TILE["up_m"], 16), _pick(f, TILE["up_n"], LANES)

    def body(x_ref, g_ref, *refs):
        w_refs, o_refs, h_ref = refs[:nw], refs[nw:2 * nw], refs[2 * nw]

        @pl.when(pl.program_id(1) == 0)
        def _():
            _, xh = _rms_stats(x_ref[...])
            h_ref[...] = (xh * g_ref[...]).astype(BF16)

        h = h_ref[...]
        for w_ref, o_ref in zip(w_refs, o_refs):
            o_ref[...] = _dot_nt(h, w_ref[...]).astype(o_ref.dtype)

    outs = pl.pallas_call(
        body, name=name, grid=(n // tm, f // tn),
        in_specs=[pl.BlockSpec((tm, d), lambda i, j: (i, 0)), pl.BlockSpec((1, d), lambda i, j: (0, 0))]
        + [pl.BlockSpec((tn, d), lambda i, j: (j, 0))] * nw,
        out_specs=[pl.BlockSpec((tm, tn), lambda i, j: (i, j))] * nw + [pl.BlockSpec((tm, d), lambda i, j: (i, 0))],
        out_shape=[_sds((n, f), out_dtype)] * nw + [_sds((n, d), BF16)],
        compiler_params=_cparams("parallel", "arbitrary"),
    )(x, g, *ws)
    return outs[:nw], outs[nw]


def _swiglu_down(name, x, a, b, w2):
    n, d = x.shape
    f = a.shape[1]
    tm = _pick(n, TILE["row"], 16)

    def body(x_ref, a_ref, b_ref, w_ref, o_ref):
        av = a_ref[...].astype(F32)
        hid = (av * _sigmoid(av) * b_ref[...].astype(F32)).astype(BF16)
        o_ref[...] = x_ref[...] + FFN_RES * _dot(hid, w_ref[...])

    return pl.pallas_call(
        body, name=name, grid=(n // tm,),
        in_specs=[pl.BlockSpec((tm, d), lambda i: (i, 0)), pl.BlockSpec((tm, f), lambda i: (i, 0)),
                  pl.BlockSpec((tm, f), lambda i: (i, 0)), pl.BlockSpec((f, d), lambda i: (0, 0))],
        out_specs=pl.BlockSpec((tm, d), lambda i: (i, 0)),
        out_shape=_sds((n, d), F32),
        compiler_params=_cparams("parallel"),
    )(x, a, b, w2)


def _loss_head(name, x, g, target):
    n, d = x.shape
    tm = _pick(n, TILE["row"], SUBLANES)

    def body(x_ref, g_ref, t_ref, dx_ref, loss_ref, dg_ref):
        @pl.when(pl.program_id(0) == 0)
        def _():
            loss_ref[...] = jnp.zeros_like(loss_ref)
            dg_ref[...] = jnp.zeros_like(dg_ref)

        xv, gv = x_ref[...], g_ref[...]
        r, xh = _rms_stats(xv)
        err = xh * gv - t_ref[...]
        loss_ref[...] += 0.5 * jnp.sum(jnp.mean(err * err, axis=-1, keepdims=True))
        dy = err * (1.0 / d)
        dxh = dy * gv
        dx_ref[...] = r * (dxh - xh * jnp.mean(dxh * xh, axis=-1, keepdims=True))
        dg_ref[...] += jnp.sum(dy * xh, axis=0, keepdims=True)

    return pl.pallas_call(
        body, name=name, grid=(n // tm,),
        in_specs=[pl.BlockSpec((tm, d), lambda i: (i, 0)), pl.BlockSpec((1, d), lambda i: (0, 0)),
                  pl.BlockSpec((tm, d), lambda i: (i, 0))],
        out_specs=[pl.BlockSpec((tm, d), lambda i: (i, 0)), pl.BlockSpec((SUBLANES, LANES), lambda i: (0, 0)),
                   pl.BlockSpec((1, d), lambda i: (0, 0))],
        out_shape=[_sds((n, d), F32), _sds((SUBLANES, LANES), F32), _sds((1, d), F32)],
        compiler_params=_cparams("arbitrary"),
    )(x, g, target)


def _ffn_bwd_hidden(name, dxo, a, b, w2):
    n, d = dxo.shape
    f = a.shape[1]
    tm, tn = _pick(n, TILE["up_m"], 16), _pick(f, TILE["up_n"], LANES)

    def body(dx_ref, a_ref, b_ref, w_ref, da_ref, db_ref, hid_ref, dxh_ref):
        @pl.when(pl.program_id(1) == 0)
        def _():
            dxh_ref[...] = (FFN_RES * dx_ref[...]).astype(BF16)

        dhid = _dot_nt(dxh_ref[...], w_ref[...])
        av, bv = a_ref[...].astype(F32), b_ref[...].astype(F32)
        sig = _sigmoid(av)
        silu = av * sig
        da_ref[...] = (dhid * bv * (sig * (1.0 + av * (1.0 - sig)))).astype(BF16)
        db_ref[...] = (dhid * silu).astype(BF16)
        hid_ref[...] = (silu * bv).astype(BF16)

    tile = pl.BlockSpec((tm, tn), lambda i, j: (i, j))
    return pl.pallas_call(
        body, name=name, grid=(n // tm, f // tn),
        in_specs=[pl.BlockSpec((tm, d), lambda i, j: (i, 0)), tile, tile, pl.BlockSpec((tn, d), lambda i, j: (j, 0))],
        out_specs=[tile, tile, tile, pl.BlockSpec((tm, d), lambda i, j: (i, 0))],
        out_shape=[_sds((n, f), BF16)] * 3 + [_sds((n, d), BF16)],
        compiler_params=_cparams("parallel", "arbitrary"),
    )(dxo, a, b, w2)


def _dx_rms_bwd(name, pairs, dxo, x, g):
    n, dm = x.shape
    tm = _pick(n, TILE["row"], 16)
    npair = len(pairs)

    def body(*refs):
        d_refs, w_refs = refs[:npair], refs[npair:2 * npair]
        dxo_ref, x_ref, g_ref, dx_ref, dg_ref = refs[2 * npair:]

        @pl.when(pl.program_id(0) == 0)
        def _():
            dg_ref[...] = jnp.zeros_like(dg_ref)

        dh = None
        for d_ref, w_ref in zip(d_refs, w_refs):
            t = _dot(d_ref[...].astype(BF16), w_ref[...])
            dh = t if dh is None else dh + t
        dx, dg = _rms_bwd(x_ref[...], g_ref[...], dh)
        dx_ref[...] = dxo_ref[...] + dx
        dg_ref[...] += dg

    row = pl.BlockSpec((tm, dm), lambda i: (i, 0))
    d_specs = [pl.BlockSpec((tm, p[1]), functools.partial(lambda i, cb: (i, cb), cb=p[2])) for p in pairs]
    w_specs = [pl.BlockSpec((p[4], dm), functools.partial(lambda i, rb: (rb, 0), rb=p[5])) for p in pairs]
    return pl.pallas_call(
        body, name=name, grid=(n // tm,),
        in_specs=d_specs + w_specs + [row, row, pl.BlockSpec((1, dm), lambda i: (0, 0))],
        out_specs=[row, pl.BlockSpec((1, dm), lambda i: (0, 0))],
        out_shape=[_sds((n, dm), F32), _sds((1, dm), F32)],
        compiler_params=_cparams("arbitrary"),
    )(*[p[0] for p in pairs], *[p[3] for p in pairs], dxo, x, g)


def _mm_tn(name, a, b, a_cols=None, b_cols=None):
    n = a.shape[0]
    a0, ma = a_cols if a_cols else (0, a.shape[1])
    b0, mb = b_cols if b_cols else (0, b.shape[1])
    tm, tn, tk = _pick(ma, TILE["mm_m"], LANES), _pick(mb, TILE["mm_n"], LANES), _pick(n, TILE["mm_k"], 16)
    assert a0 % tm == 0 and b0 % tn == 0
    ab, bb = a0 // tm, b0 // tn

    def body(a_ref, b_ref, o_ref):
        @pl.when(pl.program_id(2) == 0)
        def _():
            o_ref[...] = jnp.zeros_like(o_ref)

        o_ref[...] += _dot_tn(a_ref[...].astype(BF16), b_ref[...].astype(BF16))

    return pl.pallas_call(
        body, name=name, grid=(ma // tm, mb // tn, n // tk),
        in_specs=[pl.BlockSpec((tk, tm), lambda i, j, k: (k, ab + i)), pl.BlockSpec((tk, tn), lambda i, j, k: (k, bb + j))],
        out_specs=pl.BlockSpec((tm, tn), lambda i, j, k: (i, j)),
        out_shape=_sds((ma, mb), F32),
        compiler_params=_cparams("parallel", "parallel", "arbitrary"),
    )(a, b)


def _row_mm(name, pairs, out_w, out_dtype, add=None):
    n = pairs[0][0].shape[0]
    tm = _pick(n, TILE["row"], 16)
    npair = len(pairs)

    def body(*refs):
        a_refs, w_refs = refs[:npair], refs[npair:2 * npair]
        add_ref = refs[2 * npair] if add is not None else None
        o_ref = refs[-1]
        acc = None
        for a_ref, w_ref, p in zip(a_refs, w_refs, pairs):
            av = a_ref[...].astype(BF16)
            t = _dot_nt(av, w_ref[...]) if p[6] else _dot(av, w_ref[...])
            acc = t if acc is None else acc + t
        if add_ref is not None:
            acc = acc + add_ref[...].astype(F32)
        o_ref[...] = acc.astype(o_ref.dtype)

    a_specs = [pl.BlockSpec((tm, p[1]), functools.partial(lambda i, cb: (i, cb), cb=p[2])) for p in pairs]
    w_specs = [pl.BlockSpec((p[4], p[3].shape[1]), functools.partial(lambda i, rb: (rb, 0), rb=p[5])) for p in pairs]
    add_specs = [pl.BlockSpec((tm, out_w), lambda i: (i, 0))] if add is not None else []
    return pl.pallas_call(
        body, name=name, grid=(n // tm,),
        in_specs=a_specs + w_specs + add_specs,
        out_specs=pl.BlockSpec((tm, out_w), lambda i: (i, 0)),
        out_shape=_sds((n, out_w), out_dtype),
        compiler_params=_cparams("parallel"),
    )(*[p[0] for p in pairs], *[p[3] for p in pairs], *([add] if add is not None else []))


def _conv_post(c, ln_g, ln_b, out_g):
    mu = jnp.mean(c, axis=-1, keepdims=True)
    xc = c - mu
    rstd = lax.rsqrt(jnp.mean(xc * xc, axis=-1, keepdims=True) + EPS)
    nrm = xc * rstd
    l = nrm * ln_g + ln_b
    sig = _sigmoid(l)
    s = l * sig
    r, sh = _rms_stats(s)
    return sh * out_g, (rstd, nrm, l, sig, r, sh)


def _conv_taps(a_ref, w_ref, first, rows):
    acc = None
    for k in range(CONV_WIDTH):
        t = w_ref[k:k + 1, :] * a_ref[pl.ds(first + k, rows), :]
        acc = t if acc is None else acc + t
    return acc


def _conv_fwd(name, proj3, conv_w, conv_b, ln_g, ln_b, out_g):
    bsz, seq, _ = proj3.shape
    c = conv_w.shape[1]
    tt = _pick(seq, TILE["conv_t"], CONV_HALO)
    hb = tt // CONV_HALO
    first = CONV_HALO - (CONV_WIDTH - 1)

    def body(v_ref, g_ref, vp_ref, gp_ref, w_ref, cb_ref, lg_ref, lb_ref, og_ref, o_ref, a_ref):
        keep = (pl.program_id(1) > 0).astype(F32)
        a_ref[pl.ds(0, CONV_HALO), :] = keep * vp_ref[0] * _sigmoid(gp_ref[0])
        a_ref[pl.ds(CONV_HALO, tt), :] = v_ref[0] * _sigmoid(g_ref[0])
        cv = _conv_taps(a_ref, w_ref, first, tt) + cb_ref[...]
        out, _ = _conv_post(cv, lg_ref[...], lb_ref[...], og_ref[...])
        o_ref[0] = out.astype(BF16)

    vec = pl.BlockSpec((1, c), lambda b, i: (0, 0))
    prev = lambda col: pl.BlockSpec((1, CONV_HALO, c), lambda b, i: (b, jnp.maximum(i * hb - 1, 0), col))
    return pl.pallas_call(
        body, name=name, grid=(bsz, seq // tt),
        in_specs=[pl.BlockSpec((1, tt, c), lambda b, i: (b, i, 0)), pl.BlockSpec((1, tt, c), lambda b, i: (b, i, 1)),
                  prev(0), prev(1), pl.BlockSpec(conv_w.shape, lambda b, i: (0, 0)), vec, vec, vec, vec],
        out_specs=pl.BlockSpec((1, tt, c), lambda b, i: (b, i, 0)),
        out_shape=_sds((bsz, seq, c), BF16),
        scratch_shapes=[pltpu.VMEM((CONV_HALO + tt, c), F32)],
        compiler_params=_cparams("parallel", "arbitrary"),
    )(proj3, proj3, proj3, proj3, conv_w, conv_b, ln_g, ln_b, out_g)


def _conv_bwd(name, dmix3, proj3, conv_w, conv_b, ln_g, ln_b, out_g):
    bsz, seq, _ = proj3.shape
    c = conv_w.shape[1]
    tt = _pick(seq, TILE["conv_t"], CONV_HALO)
    hb = tt // CONV_HALO
    nt = seq // tt
    last_hb = seq // CONV_HALO - 1
    ext = tt + CONV_HALO
    first = CONV_HALO - (CONV_WIDTH - 1)

    def body(v_ref, g_ref, vp_ref, gp_ref, vn_ref, gn_ref, d_ref, dn_ref, w_ref, cb_ref, lg_ref, lb_ref, og_ref,
             o_ref, dw_ref, dcb_ref, dlg_ref, dlb_ref, dog_ref, a_ref, dc_ref):
        i = pl.program_id(1)

        @pl.when((pl.program_id(0) == 0) & (i == 0))
        def _():
            for r in (dw_ref, dcb_ref, dlg_ref, dlb_ref, dog_ref):
                r[...] = jnp.zeros_like(r)

        keep_prev = (i > 0).astype(F32)
        keep_next = (i < nt - 1).astype(F32)
        sig_g = _sigmoid(g_ref[0])
        a_ref[pl.ds(0, CONV_HALO), :] = keep_prev * vp_ref[0] * _sigmoid(gp_ref[0])
        a_ref[pl.ds(CONV_HALO, tt), :] = v_ref[0] * sig_g
        a_ref[pl.ds(CONV_HALO + tt, CONV_HALO), :] = keep_next * vn_ref[0] * _sigmoid(gn_ref[0])

        lg, og = lg_ref[...], og_ref[...]
        cv = _conv_taps(a_ref, w_ref, first, ext) + cb_ref[...]
        _, (rstd, nrm, l, sig, r, sh) = _conv_post(cv, lg, lb_ref[...], og)
        own = (lax.broadcasted_iota(jnp.int32, (ext, 1), 0) < tt).astype(F32)
        dout = jnp.concatenate([d_ref[0], keep_next * dn_ref[0]], axis=0)
        dog_ref[...] += jnp.sum(own * dout * sh, axis=0, keepdims=True)
        dsh = dout * og
        ds = r * (dsh - sh * jnp.mean(dsh * sh, axis=-1, keepdims=True))
        dl = ds * (sig * (1.0 + l * (1.0 - sig)))
        dlg_ref[...] += jnp.sum(own * dl * nrm, axis=0, keepdims=True)
        dlb_ref[...] += jnp.sum(own * dl, axis=0, keepdims=True)
        dn = dl * lg
        dc = rstd * (dn - jnp.mean(dn, axis=-1, keepdims=True) - nrm * jnp.mean(dn * nrm, axis=-1, keepdims=True))
        dc_ref[...] = dc
        dc_own = dc_ref[pl.ds(0, tt), :]
        dcb_ref[...] += jnp.sum(dc_own, axis=0, keepdims=True)

        da = None
        for k in range(CONV_WIDTH):
            t = w_ref[k:k + 1, :] * dc_ref[pl.ds(CONV_WIDTH - 1 - k, tt), :]
            da = t if da is None else da + t
            dw_ref[k:k + 1, :] += jnp.sum(dc_own * a_ref[pl.ds(first + k, tt), :], axis=0, keepdims=True)
        val = v_ref[0]
        o_ref[0] = jnp.concatenate([da * sig_g, da * val * sig_g * (1.0 - sig_g)], axis=-1).astype(BF16)

    vec = pl.BlockSpec((1, c), lambda b, i: (0, 0))
    cur = lambda col: pl.BlockSpec((1, tt, c), lambda b, i: (b, i, col))
    prev = lambda col: pl.BlockSpec((1, CONV_HALO, c), lambda b, i: (b, jnp.maximum(i * hb - 1, 0), col))
    nxt = lambda col: pl.BlockSpec((1, CONV_HALO, c), lambda b, i: (b, jnp.minimum((i + 1) * hb, last_hb), col))
    wspec = pl.BlockSpec(conv_w.shape, lambda b, i: (0, 0))
    return pl.pallas_call(
        body, name=name, grid=(bsz, nt),
        in_specs=[cur(0), cur(1), prev(0), prev(1), nxt(0), nxt(1), cur(0), nxt(0), wspec, vec, vec, vec, vec],
        out_specs=[pl.BlockSpec((1, tt, 2 * c), lambda b, i: (b, i, 0)), wspec, vec, vec, vec, vec],
        out_shape=[_sds((bsz, seq, 2 * c), BF16), _sds(conv_w.shape, F32)] + [_sds((1, c), F32)] * 4,
        scratch_shapes=[pltpu.VMEM((CONV_HALO + tt + CONV_HALO, c), F32), pltpu.VMEM((ext, c), F32)],
        compiler_params=_cparams("arbitrary", "arbitrary"),
    )(proj3, proj3, proj3, proj3, proj3, proj3, dmix3, dmix3, conv_w, conv_b, ln_g, ln_b, out_g)


def _ssm_discretise(a_re, a_im, log_dt):
    dt = jnp.exp(log_dt)
    zr, zi = a_re * dt, a_im * dt
    mag = jnp.exp(zr)
    ar, ai = mag * jnp.cos(zi), mag * jnp.sin(zi)
    den = a_re * a_re + a_im * a_im
    nr = ar - 1.0
    return ar, ai, (nr * a_re + ai * a_im) / den, (ai * a_re - nr * a_im) / den


def _ssm_system(a_re, a_im, log_dt, a_re_x, a_im_x, log_dt_x, bt_re, bt_im):
    ar, ai, _, _ = _ssm_discretise(a_re, a_im, log_dt)
    _, _, cr, ci = _ssm_discretise(a_re_x, a_im_x, log_dt_x)
    return ar, ai, cr * bt_re - ci * bt_im, cr * bt_im + ci * bt_re


def _ssm_prep(name, prim):
    g, p = prim[0].shape

    def body(*refs):
        pwr_ref, pwi_ref, bbr_ref, bbi_ref = refs[8:]
        ar, ai, bbr, bbi = _ssm_system(*[r[...] for r in refs[:8]])
        bbr_ref[...] = bbr
        bbi_ref[...] = bbi
        pr, pi = ar, ai
        for k in range(SUBLANES):
            pwr_ref[k] = pr
            pwi_ref[k] = pi
            pr, pi = pr * ar - pi * ai, pr * ai + pi * ar

    return pl.pallas_call(
        body, name=name,
        out_shape=[_sds((SUBLANES, g, p), F32)] * 2 + [_sds(prim[6].shape, F32)] * 2,
        compiler_params=pltpu.CompilerParams(vmem_limit_bytes=VMEM_LIMIT),
    )(*prim)


def _ssm_param_grads(name, prim, dab_r, dab_i, dbb_r, dbb_i):
    g, p = prim[0].shape
    h = prim[6].shape[0] // g

    def body(*refs):
        dar_ref, dai_ref, dbr_ref, dbi_ref = refs[8:12]
        o_ar, o_ai, o_dt, o_br, o_bi = refs[12:]
        _, vjp = jax.vjp(_ssm_system, *[r[...] for r in refs[:8]])
        ct = (jnp.sum(dar_ref[...], axis=0), jnp.sum(dai_ref[...], axis=0), dbr_ref[...], dbi_ref[...])
        d_ar, d_ai, d_dt, d_arx, d_aix, d_dtx, d_br, d_bi = vjp(ct)
        per_group = lambda t: jnp.sum(t.reshape(g, h, p), axis=1)
        o_ar[...] = d_ar + per_group(d_arx)
        o_ai[...] = d_ai + per_group(d_aix)
        o_dt[...] = d_dt + jnp.sum(per_group(d_dtx), axis=1, keepdims=True)
        o_br[...] = d_br
        o_bi[...] = d_bi

    return pl.pallas_call(
        body, name=name,
        out_shape=[_sds(prim[k].shape, F32) for k in (0, 1, 2, 6, 7)],
        compiler_params=pltpu.CompilerParams(vmem_limit_bytes=VMEM_LIMIT),
    )(*prim, dab_r, dab_i, dbb_r, dbb_i)


def _cfma(xr, xi, cr, ci, sr, si):
    return xr + (cr * sr - ci * si), xi + (cr * si + ci * sr)


def _scan_tables(pw_r, pw_i, reverse):
    gp = pw_r.shape[1] * pw_r.shape[2]
    pr, pi = pw_r.reshape(SUBLANES, gp), pw_i.reshape(SUBLANES, gp)
    if reverse:
        pi = -pi
    row = jnp.arange(SUBLANES)[:, None]
    tabs = []
    for d in (1, 2, 4):
        keep = (row < SUBLANES - d) if reverse else (row >= d)
        tabs += [jnp.where(keep, pr[d - 1][None, :], 0.0), jnp.where(keep, pi[d - 1][None, :], 0.0)]
    tabs += [pr[::-1], pi[::-1]] if reverse else [pr, pi]
    return jnp.concatenate(tabs, axis=0)


def _scan_fwd(name, tab, bu3):
    bsz, seq, w = bu3.shape
    gp = w // 2
    tt = _pick(seq, TILE["scan_t"], SUBLANES)
    nblk = tt // SUBLANES

    def body(tab_ref, bu_ref, xs_ref, carry_ref):
        @pl.when(pl.program_id(1) == 0)
        def _():
            carry_ref[...] = jnp.zeros_like(carry_ref)

        for ch in range(gp // LANES):
            re, im = pl.ds(ch * LANES, LANES), pl.ds(gp + ch * LANES, LANES)
            tabs = [tab_ref[pl.ds(SUBLANES * k, SUBLANES), re] for k in range(8)]

            def blk(r, carry, re=re, im=im, tabs=tabs):
                rows = pl.ds(pl.multiple_of(r * SUBLANES, SUBLANES), SUBLANES)
                xr, xi = bu_ref[0, rows, re], bu_ref[0, rows, im]
                for j, d in enumerate((1, 2, 4)):
                    xr, xi = _cfma(xr, xi, tabs[2 * j], tabs[2 * j + 1], pltpu.roll(xr, d, 0), pltpu.roll(xi, d, 0))
                xr, xi = _cfma(xr, xi, tabs[6], tabs[7], carry[0], carry[1])
                xs_ref[0, rows, re] = xr
                xs_ref[0, rows, im] = xi
                last = SUBLANES - 1
                return (jnp.broadcast_to(xr[last:, :], xr.shape), jnp.broadcast_to(xi[last:, :], xi.shape))

            cr, ci = lax.fori_loop(0, nblk, blk, (carry_ref[:, re], carry_ref[:, im]))
            carry_ref[:, re] = cr
            carry_ref[:, im] = ci

    return pl.pallas_call(
        body, name=name, grid=(bsz, seq // tt),
        in_specs=[pl.BlockSpec(tab.shape, lambda b, t: (0, 0)), pl.BlockSpec((1, tt, w), lambda b, t: (b, t, 0))],
        out_specs=pl.BlockSpec((1, tt, w), lambda b, t: (b, t, 0)),
        out_shape=_sds(bu3.shape, F32),
        scratch_shapes=[pltpu.VMEM((SUBLANES, w), F32)],
        compiler_params=_cparams("arbitrary", "arbitrary"),
    )(tab, bu3)


def _scan_bwd(name, tab, g3, xs3):
    bsz, seq, w = g3.shape
    gp = w // 2
    tt = _pick(seq, TILE["scan_t"], SUBLANES)
    nblk = tt // SUBLANES
    nt = seq // tt

    def body(tab_ref, g_ref, xs_ref, halo_ref, lam_ref, dar_ref, dai_ref, carry_ref):
        t = pl.program_id(1)

        @pl.when(t == 0)
        def _():
            carry_ref[...] = jnp.zeros_like(carry_ref)

        @pl.when((pl.program_id(0) == 0) & (t == 0))
        def _():
            dar_ref[...] = jnp.zeros_like(dar_ref)
            dai_ref[...] = jnp.zeros_like(dai_ref)

        has_prev = (t < nt - 1).astype(F32)
        row0 = lax.broadcasted_iota(jnp.int32, (SUBLANES, LANES), 0) == 0
        last = SUBLANES - 1

        for ch in range(gp // LANES):
            re, im = pl.ds(ch * LANES, LANES), pl.ds(gp + ch * LANES, LANES)
            tabs = [tab_ref[pl.ds(SUBLANES * k, SUBLANES), re] for k in range(8)]

            def step(rows, xm1r, xm1i, state, re=re, im=im, tabs=tabs):
                cr, ci, accr, acci = state
                lr, li = g_ref[0, rows, re], g_ref[0, rows, im]
                for j, d in enumerate((1, 2, 4)):
                    lr, li = _cfma(lr, li, tabs[2 * j], tabs[2 * j + 1],
                                   pltpu.roll(lr, SUBLANES - d, 0), pltpu.roll(li, SUBLANES - d, 0))
                lr, li = _cfma(lr, li, tabs[6], tabs[7], cr, ci)
                lam_ref[0, rows, re] = lr
                lam_ref[0, rows, im] = li
                xr, xi = xs_ref[0, rows, re], xs_ref[0, rows, im]
                xpr = jnp.where(row0, jnp.broadcast_to(xm1r[last:, :], xr.shape), pltpu.roll(xr, 1, 0))
                xpi = jnp.where(row0, jnp.broadcast_to(xm1i[last:, :], xi.shape), pltpu.roll(xi, 1, 0))
                accr = accr + (lr * xpr + li * xpi)
                acci = acci + (li * xpr - lr * xpi)
                return (jnp.broadcast_to(lr[:1, :], lr.shape), jnp.broadcast_to(li[:1, :], li.shape), accr, acci)

            def blk(k, state, re=re, im=im, step=step):
                r = nblk - 1 - k
                rows = pl.ds(pl.multiple_of(r * SUBLANES, SUBLANES), SUBLANES)
                prev = pl.ds(pl.multiple_of((r - 1) * SUBLANES, SUBLANES), SUBLANES)
                return step(rows, xs_ref[0, prev, re], xs_ref[0, prev, im], state)

            zero = jnp.zeros((SUBLANES, LANES), F32)
            state = lax.fori_loop(0, nblk - 1, blk, (carry_ref[:, re], carry_ref[:, im], zero, zero))
            cr, ci, accr, acci = step(pl.ds(0, SUBLANES), has_prev * halo_ref[0, :, re], has_prev * halo_ref[0, :, im], state)
            carry_ref[:, re] = cr
            carry_ref[:, im] = ci
            dar_ref[:, pl.ds(ch * LANES, LANES)] += accr
            dai_ref[:, pl.ds(ch * LANES, LANES)] += acci

    tile = pl.BlockSpec((1, tt, w), lambda b, t: (b, nt - 1 - t, 0))
    halo = pl.BlockSpec((1, SUBLANES, w), lambda b, t: (b, jnp.maximum((nt - 1 - t) * nblk - 1, 0), 0))
    acc = pl.BlockSpec((SUBLANES, gp), lambda b, t: (0, 0))
    return pl.pallas_call(
        body, name=name, grid=(bsz, nt),
        in_specs=[pl.BlockSpec(tab.shape, lambda b, t: (0, 0)), tile, tile, halo],
        out_specs=[tile, acc, acc],
        out_shape=[_sds(g3.shape, F32), _sds((SUBLANES, gp), F32), _sds((SUBLANES, gp), F32)],
        scratch_shapes=[pltpu.VMEM((SUBLANES, w), F32)],
        compiler_params=_cparams("arbitrary", "arbitrary"),
    )(tab, g3, xs3, xs3)


def _gelu_parts(y):
    inner = _GELU_K * (y + _GELU_C * y * y * y)
    t = jnp.tanh(inner)
    return 0.5 * y * (1.0 + t), t


def _ssm_out_fwd(name, xs, proj, u_block, cdt, d_skip, glu_w, glu_b, out_g):
    n, w = xs.shape
    c = cdt.shape[0]
    tm = _pick(n, TILE["row"], 16)

    def body(xs_ref, u_ref, cdt_ref, d_ref, gw_ref, gb_ref, og_ref, y_ref, o_ref):
        y = _dot_nt(xs_ref[...].astype(BF16), cdt_ref[...]) + d_ref[...] * u_ref[...]
        y_ref[...] = y
        gy, _ = _gelu_parts(y)
        z = _dot(gy.astype(BF16), gw_ref[...]) + gb_ref[...]
        _, sh = _rms_stats(gy * _sigmoid(z))
        o_ref[...] = (sh * og_ref[...]).astype(BF16)

    vec = pl.BlockSpec((1, c), lambda i: (0, 0))
    row = pl.BlockSpec((tm, c), lambda i: (i, 0))
    return pl.pallas_call(
        body, name=name, grid=(n // tm,),
        in_specs=[pl.BlockSpec((tm, w), lambda i: (i, 0)), pl.BlockSpec((tm, c), lambda i: (i, u_block)),
                  pl.BlockSpec(cdt.shape, lambda i: (0, 0)), vec, pl.BlockSpec(glu_w.shape, lambda i: (0, 0)), vec, vec],
        out_specs=[row, row],
        out_shape=[_sds((n, c), F32), _sds((n, c), BF16)],
        compiler_params=_cparams("parallel"),
    )(xs, proj, cdt, d_skip, glu_w, glu_b, out_g)


def _ssm_out_bwd(name, dmix, d_block, y, proj, u_block, d_skip, glu_w, glu_b, out_g):
    n, c = y.shape
    tm = _pick(n, TILE["row"], 16)

    def body(d_ref, y_ref, u_ref, dk_ref, gw_ref, gb_ref, og_ref, dy_ref, du_ref, dgw_ref, dgb_ref, dog_ref, dd_ref):
        @pl.when(pl.program_id(0) == 0)
        def _():
            for r in (dgw_ref, dgb_ref, dog_ref, dd_ref):
                r[...] = jnp.zeros_like(r)

        yv = y_ref[...]
        gy, th = _gelu_parts(yv)
        gy16 = gy.astype(BF16)
        sz = _sigmoid(_dot(gy16, gw_ref[...]) + gb_ref[...])
        r, sh = _rms_stats(gy * sz)
        dout = d_ref[...]
        dog_ref[...] += jnp.sum(dout * sh, axis=0, keepdims=True)
        dsh = dout * og_ref[...]
        ds = r * (dsh - sh * jnp.mean(dsh * sh, axis=-1, keepdims=True))
        dz = ds * gy * sz * (1.0 - sz)
        dz16 = dz.astype(BF16)
        dgb_ref[...] += jnp.sum(dz, axis=0, keepdims=True)
        dgw_ref[...] += _dot_tn(gy16, dz16)
        dgy = ds * sz + _dot_nt(dz16, gw_ref[...])
        dgelu = 0.5 * (1.0 + th) + 0.5 * yv * (1.0 - th * th) * (_GELU_K * (1.0 + 3.0 * _GELU_C * yv * yv))
        dy = dgy * dgelu
        dy_ref[...] = dy.astype(BF16)
        du_ref[...] = dy * dk_ref[...]
        dd_ref[...] += jnp.sum(dy * u_ref[...], axis=0, keepdims=True)

    vec = pl.BlockSpec((1, c), lambda i: (0, 0))
    row = pl.BlockSpec((tm, c), lambda i: (i, 0))
    mat = pl.BlockSpec(glu_w.shape, lambda i: (0, 0))
    return pl.pallas_call(
        body, name=name, grid=(n // tm,),
        in_specs=[pl.BlockSpec((tm, c), lambda i: (i, d_block)), row, pl.BlockSpec((tm, c), lambda i: (i, u_block)),
                  vec, mat, vec, vec],
        out_specs=[row, row, mat, vec, vec, vec],
        out_shape=[_sds((n, c), BF16), _sds((n, c), F32), _sds(glu_w.shape, F32)] + [_sds((1, c), F32)] * 3,
        compiler_params=_cparams("arbitrary"),
    )(dmix, y, proj, d_skip, glu_w, glu_b, out_g)


def _mesh_pos():
    return tuple(lax.axis_index(a) for a in MESH_AXES)


def _other_chips(x, y):
    return [(1 - x, y), (x, 1 - y), (1 - x, 1 - y)]


def _remote(src, dst, send_sem, recv_sem, dev):
    return pltpu.make_async_remote_copy(src_ref=src, dst_ref=dst, send_sem=send_sem, recv_sem=recv_sem,
                                        device_id=dev, device_id_type=pl.DeviceIdType.MESH)


def _hbm_call(name, body, operands, out_shapes, scratch):
    hbm = pl.BlockSpec(memory_space=pltpu.HBM)
    return pl.pallas_call(body, name=name, in_specs=[hbm] * len(operands), out_specs=[hbm] * len(out_shapes),
                          out_shape=out_shapes, scratch_shapes=scratch)(*operands)


def _all_gather(name, blocks):
    nop = len(blocks)

    def body(*refs):
        x_refs, o_refs = refs[:nop], refs[nop:2 * nop]
        send_sems, recv_sems, local_sems = refs[2 * nop:]
        x, y, c = _mesh_pos()
        me, sibling = (x, y, c), (x, y, 1 - c)
        chips = _other_chips(x, y)

        def copy(i, k, block_of, to, src=None):
            dst = o_refs[i].at[4 * block_of[0] + 2 * block_of[1] + block_of[2]]
            return _remote(dst if src is None else src, dst, send_sems.at[i, k], recv_sems.at[i, k], to)

        own = [pltpu.make_async_copy(x_refs[i], o_refs[i].at[4 * x + 2 * y + c], local_sems.at[i]) for i in range(nop)]
        for cp in own:
            cp.start()
        first = []
        for i in range(nop):
            first.append(copy(i, 0, me, sibling, src=x_refs[i]))
            first += [copy(i, 1 + j, me, (*chip, c), src=x_refs[i]) for j, chip in enumerate(chips)]
        for cp in first:
            cp.start()
        passed = []
        for i in range(nop):
            for j, chip in enumerate(chips):
                copy(i, 1 + j, (*chip, c), me).wait_recv()
                passed.append(copy(i, 4 + j, (*chip, c), sibling))
                passed[-1].start()
        for i in range(nop):
            copy(i, 0, sibling, me).wait_recv()
            for j, chip in enumerate(chips):
                copy(i, 4 + j, (*chip, 1 - c), me).wait_recv()
        for cp in first + passed:
            cp.wait_send()
        for cp in own:
            cp.wait()

    return _hbm_call(name, body, blocks, [_sds((N_DEV,) + b.shape, b.dtype) for b in blocks],
                     [pltpu.SemaphoreType.DMA((nop, N_DEV - 1)), pltpu.SemaphoreType.DMA((nop, N_DEV - 1)),
                      pltpu.SemaphoreType.DMA((nop,))])


def _exchange_sibling(name, grads):
    nop = len(grads)

    def body(*refs):
        x_refs, o_refs = refs[:nop], refs[nop:2 * nop]
        send_sems, recv_sems = refs[2 * nop:]
        x, y, c = _mesh_pos()
        copies = [_remote(x_refs[i].at[2 * q + (1 - c)], o_refs[i].at[q], send_sems.at[i, q], recv_sems.at[i, q],
                          (x, y, 1 - c)) for i in range(nop) for q in range(N_DEV // 2)]
        for cp in copies:
            cp.start()
        for cp in copies:
            cp.wait_recv()
        for cp in copies:
            cp.wait_send()

    return _hbm_call(name, body, grads, [_sds((N_DEV // 2,) + g.shape[1:], g.dtype) for g in grads],
                     [pltpu.SemaphoreType.DMA((nop, N_DEV // 2)), pltpu.SemaphoreType.DMA((nop, N_DEV // 2))])


def _pair_sum(name, grad, other):
    nchip, _, r, c = grad.shape
    tr = _pick(r, max(SUBLANES, TILE["sum_bytes"] // (8 * c)), SUBLANES)

    def body(g_ref, o_ref, s_ref):
        mine = jnp.where(lax.axis_index("c") == 0, g_ref[0, 0], g_ref[0, 1])
        s_ref[0] = mine + o_ref[0]

    return pl.pallas_call(
        body, name=name, grid=(nchip, r // tr),
        in_specs=[pl.BlockSpec((1, 2, tr, c), lambda q, t: (q, 0, t, 0)), pl.BlockSpec((1, tr, c), lambda q, t: (q, t, 0))],
        out_specs=pl.BlockSpec((1, tr, c), lambda q, t: (q, t, 0)),
        out_shape=_sds((nchip, r, c), F32),
        compiler_params=_cparams("parallel", "parallel"),
    )(grad, other)


def _exchange_chips(name, sums):
    nop = len(sums)

    def body(*refs):
        x_refs, o_refs = refs[:nop], refs[nop:2 * nop]
        send_sems, recv_sems, local_sems = refs[2 * nop:]
        x, y, c = _mesh_pos()
        mine = 2 * x + y
        own = [pltpu.make_async_copy(x_refs[i].at[mine], o_refs[i].at[mine], local_sems.at[i]) for i in range(nop)]
        for cp in own:
            cp.start()
        sends, recvs = [], []
        for i in range(nop):
            for j, (px, py) in enumerate(_other_chips(x, y)):
                theirs = 2 * px + py
                sends.append(_remote(x_refs[i].at[theirs], o_refs[i].at[mine], send_sems.at[i, j], recv_sems.at[i, j],
                                     (px, py, c)))
                recvs.append(_remote(x_refs[i].at[mine], o_refs[i].at[theirs], send_sems.at[i, j], recv_sems.at[i, j],
                                     (px, py, c)))
        for cp in sends:
            cp.start()
        for cp in recvs:
            cp.wait_recv()
        for cp in sends:
            cp.wait_send()
        for cp in own:
            cp.wait()

    return _hbm_call(name, body, sums, [_sds(s.shape, s.dtype) for s in sums],
                     [pltpu.SemaphoreType.DMA((nop, 3)), pltpu.SemaphoreType.DMA((nop, 3)), pltpu.SemaphoreType.DMA((nop,))])


def _part_rows(npart, r, c):
    return _pick(r, max(SUBLANES, TILE["sum_bytes"] // (4 * npart * c)), SUBLANES)


def _sum_parts(name, parts):
    npart, r, c = parts.shape
    tr = _part_rows(npart, r, c)

    def body(p_ref, o_ref):
        g = p_ref[0]
        for k in range(1, npart):
            g = g + p_ref[k]
        o_ref[...] = g

    return pl.pallas_call(
        body, name=name, grid=(r // tr,),
        in_specs=[pl.BlockSpec((npart, tr, c), lambda i: (0, i, 0))],
        out_specs=pl.BlockSpec((tr, c), lambda i: (i, 0)),
        out_shape=_sds((r, c), F32),
        compiler_params=_cparams("parallel"),
    )(parts)


def _adamw(name, parts, w, m, v):
    npart, r, c = parts.shape
    tr = _part_rows(npart, r, c)
    c1 = 1.0 - ADAM_B1 ** ADAM_STEP
    c2 = 1.0 - ADAM_B2 ** ADAM_STEP

    def body(p_ref, w_ref, m_ref, v_ref, g_ref, d_ref, nm_ref, nv_ref):
        g = p_ref[0]
        for k in range(1, npart):
            g = g + p_ref[k]
        nm = ADAM_B1 * m_ref[...] + (1.0 - ADAM_B1) * g
        nv = ADAM_B2 * v_ref[...] + (1.0 - ADAM_B2) * (g * g)
        g_ref[...] = g
        nm_ref[...] = nm
        nv_ref[...] = nv
        d_ref[...] = -ADAM_LR * ((nm / c1) / (jnp.sqrt(nv / c2) + ADAM_EPS) + ADAM_WD * w_ref[...])

    row = pl.BlockSpec((tr, c), lambda i: (i, 0))
    return pl.pallas_call(
        body, name=name, grid=(r // tr,),
        in_specs=[pl.BlockSpec((npart, tr, c), lambda i: (0, i, 0)), row, row, row],
        out_specs=[row] * 4,
        out_shape=[_sds((r, c), F32)] * 4,
        compiler_params=_cparams("parallel"),
    )(parts, w, m, v)


def _pack(pieces, row_mult, lead=()):
    nl = len(lead)
    flat, spans, off = [], [], 0
    for p in pieces:
        p = p.reshape(lead + (-1,))
        size = p.shape[-1]
        padded = -(-size // PACK_W) * PACK_W
        flat.append(jnp.pad(p, [(0, 0)] * nl + [(0, padded - size)]))
        spans.append((off, size))
        off += padded
    rows = -(-(off // PACK_W) // row_mult) * row_mult
    if rows * PACK_W > off:
        flat.append(jnp.zeros(lead + (rows * PACK_W - off,), flat[0].dtype))
    return jnp.concatenate(flat, axis=-1).reshape(lead + (rows, PACK_W)), spans


def _unpack(buf, spans, shapes, lead=0):
    flat = buf.reshape(buf.shape[:lead] + (-1,))
    return [flat[..., o:o + s].reshape(buf.shape[:lead] + tuple(shape)) for (o, s), shape in zip(spans, shapes)]


def _block_diag(rows_gh, groups):
    gh, p = rows_gh.shape
    own = (jnp.arange(gh)[:, None] // (gh // groups) == jnp.arange(groups)[None, :]).astype(rows_gh.dtype)
    return (own[:, :, None] * rows_gh[:, None, :]).reshape(gh, groups * p)


def _block_diag_take(dense, groups):
    gh = dense.shape[0]
    p = dense.shape[1] // groups
    own = (jnp.arange(gh)[:, None] // (gh // groups) == jnp.arange(groups)[None, :]).astype(dense.dtype)
    return jnp.sum(dense.reshape(gh, groups, p) * own[:, :, None], axis=1)


BIG = ("ffn1_w1", "ffn1_w3", "ffn1_w2", "w_in", "ssm_glu_w", "w_out", "ffn2_w1", "ffn2_w3", "ffn2_w2")
COL_SHARDED = ("ffn1_w1", "ffn1_w3", "w_in", "ffn2_w1", "ffn2_w3", "conv_w")
SMALL = ("norm_ffn1", "norm_mix", "conv_b", "conv_ln_g", "conv_ln_b", "conv_out_g", "ssm_A_re", "ssm_A_im",
         "ssm_log_dt", "ssm_B_re", "ssm_B_im", "ssm_C_re", "ssm_C_im", "ssm_D", "ssm_glu_b", "ssm_out_g",
         "norm_ffn2", "norm_final")
WEIGHTS = ("norm_ffn1", "ffn1_w1", "ffn1_w3", "ffn1_w2", "norm_mix", "w_in", "conv_w", "conv_b", "conv_ln_g",
           "conv_ln_b", "conv_out_g", "ssm_A_re", "ssm_A_im", "ssm_log_dt", "ssm_B_re", "ssm_B_im", "ssm_C_re",
           "ssm_C_im", "ssm_D", "ssm_glu_w", "ssm_glu_b", "ssm_out_g", "w_out", "norm_ffn2", "ffn2_w1", "ffn2_w3",
           "ffn2_w2", "norm_final")


def _ffn_fwd(tag, x, g, w1, w3, w2):
    (a, b), h = _rms_mm(tag + "_up", x, g, [w1, w3], BF16)
    return _swiglu_down(tag + "_down", x, a, b, w2), (a, b, h)


def _ffn_bwd(tag, dxo, x, g, w1, w3, w2, saved):
    a, b, h = saved
    da, db, hid, dxh = _ffn_bwd_hidden(tag + "_bwd_hidden", dxo, a, b, w2)
    f = a.shape[1]
    dx, dg = _dx_rms_bwd(tag + "_bwd_dx", [(da, f, 0, w1, f, 0), (db, f, 0, w3, f, 0)], dxo, x, g)
    return dx, dg, _mm_tn(tag + "_dw1", da, h), _mm_tn(tag + "_dw3", db, h), _mm_tn(tag + "_dw2", hid, dxh)


def kernel(x, norm_ffn1, ffn1_w1, ffn1_w3, ffn1_w2, norm_mix, w_in, conv_w, conv_b, conv_ln_g, conv_ln_b, conv_out_g, ssm_A_re, ssm_A_im, ssm_log_dt, ssm_B_re, ssm_B_im, ssm_C_re, ssm_C_im, ssm_D, ssm_glu_w, ssm_glu_b, ssm_out_g, w_out, norm_ffn2, ffn2_w1, ffn2_w3, ffn2_w2, norm_final, loss_target, m_norm_ffn1, m_ffn1_w1, m_ffn1_w3, m_ffn1_w2, m_norm_mix, m_w_in, m_conv_w, m_conv_b, m_conv_ln_g, m_conv_ln_b, m_conv_out_g, m_ssm_A_re, m_ssm_A_im, m_ssm_log_dt, m_ssm_B_re, m_ssm_B_im, m_ssm_C_re, m_ssm_C_im, m_ssm_D, m_ssm_glu_w, m_ssm_glu_b, m_ssm_out_g, m_w_out, m_norm_ffn2, m_ffn2_w1, m_ffn2_w3, m_ffn2_w2, m_norm_final, v_norm_ffn1, v_ffn1_w1, v_ffn1_w3, v_ffn1_w2, v_norm_mix, v_w_in, v_conv_w, v_conv_b, v_conv_ln_g, v_conv_ln_b, v_conv_out_g, v_ssm_A_re, v_ssm_A_im, v_ssm_log_dt, v_ssm_B_re, v_ssm_B_im, v_ssm_C_re, v_ssm_C_im, v_ssm_D, v_ssm_glu_w, v_ssm_glu_b, v_ssm_out_g, v_w_out, v_norm_ffn2, v_ffn2_w1, v_ffn2_w3, v_ffn2_w2, v_norm_final):
    args = dict(locals())
    wt = {n: args[n] for n in WEIGHTS}
    mom = {n: args["m_" + n] for n in WEIGHTS}
    var = {n: args["v_" + n] for n in WEIGHTS}

    bsz, seq, d = x.shape
    n = bsz * seq
    c = conv_b.shape[-1]
    groups = c // SSM_GROUP
    gp = groups * SSM_STATE
    u_b = 2

    shards = [(wt[k][0].T if k in COL_SHARDED else wt[k][0]).astype(BF16) for k in BIG] + [wt["conv_w"][0]]
    gathered = _all_gather("gather_weights", shards)
    full = {k: g.reshape(-1, g.shape[-1]) for k, g in zip(BIG, gathered)}
    conv_w_full = gathered[-1].transpose(1, 0, 2).reshape(CONV_WIDTH, c)
    conv_w_pad = jnp.pad(conv_w_full, ((0, CONV_HALO - CONV_WIDTH), (0, 0)))

    vec = lambda k: wt[k].reshape(1, -1)
    g_ffn1, g_mix, g_ffn2, g_fin = vec("norm_ffn1"), vec("norm_mix"), vec("norm_ffn2"), vec("norm_final")
    cb, lng, lnb, cog = vec("conv_b"), vec("conv_ln_g"), vec("conv_ln_b"), vec("conv_out_g")
    d_skip, glu_b, sog = vec("ssm_D"), vec("ssm_glu_b"), vec("ssm_out_g")

    a_re, a_im = wt["ssm_A_re"][0], wt["ssm_A_im"][0]
    log_dt = wt["ssm_log_dt"][0].reshape(groups, 1)
    bt_re = wt["ssm_B_re"][0].transpose(0, 2, 1).reshape(groups * SSM_GROUP, SSM_STATE)
    bt_im = wt["ssm_B_im"][0].transpose(0, 2, 1).reshape(groups * SSM_GROUP, SSM_STATE)
    c_re = wt["ssm_C_re"][0].reshape(groups * SSM_GROUP, SSM_STATE)
    c_im = wt["ssm_C_im"][0].reshape(groups * SSM_GROUP, SSM_STATE)
    per_chan = lambda t: jnp.repeat(t, SSM_GROUP, axis=0)
    ssm_prim = (a_re, a_im, log_dt, per_chan(a_re), per_chan(a_im), per_chan(jnp.broadcast_to(log_dt, a_re.shape)),
                bt_re, bt_im)
    pw_r, pw_i, bb_r, bb_i = _ssm_prep("ssm_prep", ssm_prim)
    tab_f = _scan_tables(pw_r, pw_i, False)
    tab_b = _scan_tables(pw_r, pw_i, True)
    bbd = jnp.concatenate([_block_diag(bb_r, groups), _block_diag(bb_i, groups)], axis=1).astype(BF16)
    cdt = jnp.concatenate([_block_diag(c_re, groups), -_block_diag(c_im, groups)], axis=1).astype(BF16)

    x0 = x.reshape(n, d)
    x1, ffn1_saved = _ffn_fwd("ffn1", x0, g_ffn1, full["ffn1_w1"], full["ffn1_w3"], full["ffn1_w2"])
    (proj,), h2 = _rms_mm("mix_in", x1, g_mix, [full["w_in"]], F32)
    proj3 = proj.reshape(bsz, seq, 3 * c)
    an = _conv_fwd("conv_fwd", proj3, conv_w_pad, cb, lng, lnb, cog).reshape(n, c)
    bu = _row_mm("ssm_bu", [(proj, c, u_b, bbd, c, 0, False)], 2 * gp, F32)
    xs = _scan_fwd("scan_fwd", tab_f, bu.reshape(bsz, seq, 2 * gp)).reshape(n, 2 * gp)
    y, sn = _ssm_out_fwd("ssm_out_fwd", xs, proj, u_b, cdt, d_skip, full["ssm_glu_w"], glu_b, sog)
    w_o = full["w_out"]
    x2 = _row_mm("mix_out", [(an, c, 0, w_o, c, 0, False), (sn, c, 0, w_o, c, 1, False)], d, F32, add=x1)
    x3, ffn2_saved = _ffn_fwd("ffn2", x2, g_ffn2, full["ffn2_w1"], full["ffn2_w3"], full["ffn2_w2"])
    dx3, loss_tile, d_gfin = _loss_head("loss_head", x3, g_fin, loss_target.reshape(n, d))
    loss = lax.psum(loss_tile[0, 0], MESH_AXES)

    grads = {}
    dx2, grads["norm_ffn2"], grads["ffn2_w1"], grads["ffn2_w3"], grads["ffn2_w2"] = _ffn_bwd(
        "ffn2", dx3, x2, g_ffn2, full["ffn2_w1"], full["ffn2_w3"], full["ffn2_w2"], ffn2_saved)

    dmix = _row_mm("mix_out_bwd", [(dx2, d, 0, w_o, 2 * c, 0, True)], 2 * c, F32)
    grads["w_out"] = jnp.concatenate([_mm_tn("dw_out_a", an, dx2), _mm_tn("dw_out_s", sn, dx2)], axis=0)

    dy, du_skip, grads["ssm_glu_w"], grads["ssm_glu_b"], grads["ssm_out_g"], grads["ssm_D"] = _ssm_out_bwd(
        "ssm_out_bwd", dmix, 1, y, proj, u_b, d_skip, full["ssm_glu_w"], glu_b, sog)
    gx = _row_mm("ssm_dx", [(dy, c, 0, cdt, c, 0, False)], 2 * gp, F32)
    lam3, dab_r, dab_i = _scan_bwd("scan_bwd", tab_b, gx.reshape(bsz, seq, 2 * gp), xs.reshape(bsz, seq, 2 * gp))
    lam = lam3.reshape(n, 2 * gp)
    du = _row_mm("ssm_du", [(lam, 2 * gp, 0, bbd, c, 0, True)], c, BF16, add=du_skip)
    d_bbd = _mm_tn("ssm_dbb", proj, lam, a_cols=(u_b * c, c))
    d_cdt = _mm_tn("ssm_dc", dy, xs)
    d_are, d_aim, d_ldt, d_btr, d_bti = _ssm_param_grads(
        "ssm_param_grads", ssm_prim,
        dab_r.reshape(SUBLANES, groups, SSM_STATE), dab_i.reshape(SUBLANES, groups, SSM_STATE),
        _block_diag_take(d_bbd[:, :gp], groups), _block_diag_take(d_bbd[:, gp:], groups))
    grads["ssm_A_re"], grads["ssm_A_im"], grads["ssm_log_dt"] = d_are, d_aim, d_ldt
    grads["ssm_B_re"] = d_btr.reshape(groups, SSM_GROUP, SSM_STATE).transpose(0, 2, 1)
    grads["ssm_B_im"] = d_bti.reshape(groups, SSM_GROUP, SSM_STATE).transpose(0, 2, 1)
    grads["ssm_C_re"] = _block_diag_take(d_cdt[:, :gp], groups)
    grads["ssm_C_im"] = -_block_diag_take(d_cdt[:, gp:], groups)

    dconv3, d_cw, grads["conv_b"], grads["conv_ln_g"], grads["conv_ln_b"], grads["conv_out_g"] = _conv_bwd(
        "conv_bwd", dmix.reshape(bsz, seq, 2 * c), proj3, conv_w_pad, cb, lng, lnb, cog)
    dconv = dconv3.reshape(n, 2 * c)
    grads["conv_w"] = d_cw[:CONV_WIDTH]
    grads["w_in"] = jnp.concatenate([_mm_tn("dw_in_conv", dconv, h2), _mm_tn("dw_in_ssm", du, h2)], axis=0)
    w_i = full["w_in"]
    dx1, grads["norm_mix"] = _dx_rms_bwd("mix_in_bwd", [(dconv, 2 * c, 0, w_i, 2 * c, 0), (du, c, 0, w_i, c, 2)], dx2, x1, g_mix)

    dx0, grads["norm_ffn1"], grads["ffn1_w1"], grads["ffn1_w3"], grads["ffn1_w2"] = _ffn_bwd(
        "ffn1", dx1, x0, g_ffn1, full["ffn1_w1"], full["ffn1_w3"], full["ffn1_w2"], ffn1_saved)
    grads["norm_final"] = d_gfin

    send = [grads[k].reshape((N_DEV, -1) + grads[k].shape[1:]) for k in BIG]
    from_core = _exchange_sibling("exchange_grads_core", send)
    sums = [_pair_sum("pair_sum_" + k, s.reshape((N_DEV // 2, 2) + s.shape[1:]), o)
            for k, s, o in zip(BIG, send, from_core)]
    from_chips = _exchange_chips("exchange_grads_chip", sums)
    res = {}
    for k, parts in zip(BIG, from_chips):
        if k in COL_SHARDED:
            parts = _sum_parts("sum_" + k, parts).T[None]
        res[k] = _adamw("adamw_" + k, parts, wt[k][0], mom[k][0], var[k][0])

    small_names = SMALL + ("conv_w",)
    no_state = jnp.zeros_like(grads["conv_w"])
    part, spans = _pack([grads[k] for k in small_names], SUBLANES)
    (all_parts,) = _all_gather("gather_small_grads", [part])
    w_pk, _ = _pack([wt[k] for k in SMALL] + [no_state], SUBLANES)
    m_pk, _ = _pack([mom[k] for k in SMALL] + [no_state], SUBLANES)
    v_pk, _ = _pack([var[k] for k in SMALL] + [no_state], SUBLANES)
    small_out = _adamw("adamw_replicated", all_parts, w_pk, m_pk, v_pk)
    small_shapes = [wt[k].shape for k in SMALL] + [grads["conv_w"].shape]
    small_res = [dict(zip(small_names, _unpack(o, spans, small_shapes))) for o in small_out]
    x_pos, y_pos, c_pos = (lax.axis_index(a) for a in MESH_AXES)
    cw_cols = c // N_DEV
    own_cw = lax.dynamic_slice_in_dim(small_res[0]["conv_w"], (4 * x_pos + 2 * y_pos + c_pos) * cw_cols, cw_cols, axis=1)
    res["conv_w"] = _adamw("adamw_conv_w", own_cw[None], wt["conv_w"][0], mom["conv_w"][0], var["conv_w"][0])

    outs = [loss, dx0.reshape(bsz, seq, d)]
    for kind in range(4):
        outs += [res[k][kind].reshape(wt[k].shape) if k in res else small_res[kind][k] for k in WEIGHTS]
    return tuple(outs)
```

```python
import functools
import math

import jax
import jax.numpy as jnp
from jax import lax
from jax.experimental import pallas as pl
from jax.experimental.pallas import tpu as pltpu

F32 = jnp.float32
BF16 = jnp.bfloat16

EPS = 1e-6
FFN_RES = 0.5
CONV_WIDTH = 31
CONV_HALO = 32
SSM_GROUP = 16
SSM_STATE = 64
ADAM_LR, ADAM_B1, ADAM_B2, ADAM_EPS, ADAM_WD, ADAM_STEP = 0.001, 0.9, 0.999, 1e-08, 0.01, 10

N_DEV = 8
MESH_AXES = ("x", "y", "c")
SUBLANES = 8
LANES = 128
PACK_W = 1024
V7X_VMEM_BYTES = 64 * 2**20
VMEM_LIMIT = V7X_VMEM_BYTES - 8 * 2**20

TILE = dict(row=256, mm_bytes=8 * 2**20, up_m=1024, up_n=256, conv_t=512, scan_t=256, scan_w=512,
            sum_bytes=4 * 2**20)

_GELU_K = math.sqrt(2.0 / math.pi)
_GELU_C = 0.044715


def _pick(n, target, mult):
    best = None
    for t in range(mult, min(n, target) + 1, mult):
        if n % t == 0:
            best = t
    return n if best is None else best


def _cparams(*sem):
    return pltpu.CompilerParams(dimension_semantics=sem, vmem_limit_bytes=VMEM_LIMIT)


def _sds(shape, dtype):
    return jax.ShapeDtypeStruct(shape, dtype)


def _dot(a, b):
    return jnp.dot(a, b, preferred_element_type=F32)


def _dot_nt(a, b):
    return lax.dot_general(a, b, (((1,), (1,)), ((), ())), preferred_element_type=F32)


def _dot_tn(a, b):
    return lax.dot_general(a, b, (((0,), (0,)), ((), ())), preferred_element_type=F32)


def _sigmoid(x):
    return 1.0 / (1.0 + jnp.exp(-x))


def _rms_stats(x):
    r = lax.rsqrt(jnp.mean(x * x, axis=-1, keepdims=True) + EPS)
    return r, x * r


def _rms_bwd(x, g, dy):
    r, xh = _rms_stats(x)
    dxh = dy * g
    dx = r * (dxh - xh * jnp.mean(dxh * xh, axis=-1, keepdims=True))
    return dx, jnp.sum(dy * xh, axis=0, keepdims=True)


def _rms_mm(name, x, g, ws, out_dtype):
    n, d = x.shape
    f = ws[0].shape[0]
    nw = len(ws)
    tm, tn = _pick(n, TILE["up_m"], 16), _pick(f, TILE["up_n"], LANES)

    def body(x_ref, g_ref, *refs):
        w_refs, o_refs, h_ref = refs[:nw], refs[nw:2 * nw], refs[2 * nw]

        @pl.when(pl.program_id(1) == 0)
        def _():
            _, xh = _rms_stats(x_ref[...])
            h_ref[...] = (xh * g_ref[...]).astype(BF16)

        h = h_ref[...]
        for w_ref, o_ref in zip(w_refs, o_refs):
            o_ref[...] = _dot_nt(h, w_ref[...]).astype(o_ref.dtype)

    outs = pl.pallas_call(
        body, name=name, grid=(n // tm, f // tn),
        in_specs=[pl.BlockSpec((tm, d), lambda i, j: (i, 0)), pl.BlockSpec((1, d), lambda i, j: (0, 0))]
        + [pl.BlockSpec((tn, d), lambda i, j: (j, 0))] * nw,
        out_specs=[pl.BlockSpec((tm, tn), lambda i, j: (i, j))] * nw + [pl.BlockSpec((tm, d), lambda i, j: (i, 0))],
        out_shape=[_sds((n, f), out_dtype)] * nw + [_sds((n, d), BF16)],
        compiler_params=_cparams("parallel", "arbitrary"),
    )(x, g, *ws)
    return outs[:nw], outs[nw]


def _swiglu_down(name, x, a, b, w2):
    n, d = x.shape
    f = a.shape[1]
    tm = _pick(n, TILE["row"], 16)

    def body(x_ref, a_ref, b_ref, w_ref, o_ref):
        av = a_ref[...].astype(F32)
        hid = (av * _sigmoid(av) * b_ref[...].astype(F32)).astype(BF16)
        o_ref[...] = x_ref[...] + FFN_RES * _dot(hid, w_ref[...])

    return pl.pallas_call(
        body, name=name, grid=(n // tm,),
        in_specs=[pl.BlockSpec((tm, d), lambda i: (i, 0)), pl.BlockSpec((tm, f), lambda i: (i, 0)),
                  pl.BlockSpec((tm, f), lambda i: (i, 0)), pl.BlockSpec((f, d), lambda i: (0, 0))],
        out_specs=pl.BlockSpec((tm, d), lambda i: (i, 0)),
        out_shape=_sds((n, d), F32),
        compiler_params=_cparams("parallel"),
    )(x, a, b, w2)


def _loss_head(name, x, g, target):
    n, d = x.shape
    tm = _pick(n, TILE["row"], SUBLANES)

    def body(x_ref, g_ref, t_ref, dx_ref, loss_ref, dg_ref):
        @pl.when(pl.program_id(0) == 0)
        def _():
            loss_ref[...] = jnp.zeros_like(loss_ref)
            dg_ref[...] = jnp.zeros_like(dg_ref)

        xv, gv = x_ref[...], g_ref[...]
        r, xh = _rms_stats(xv)
        err = xh * gv - t_ref[...]
        loss_ref[...] += 0.5 * jnp.sum(jnp.mean(err * err, axis=-1, keepdims=True))
        dy = err * (1.0 / d)
        dxh = dy * gv
        dx_ref[...] = r * (dxh - xh * jnp.mean(dxh * xh, axis=-1, keepdims=True))
        dg_ref[...] += jnp.sum(dy * xh, axis=0, keepdims=True)

    return pl.pallas_call(
        body, name=name, grid=(n // tm,),
        in_specs=[pl.BlockSpec((tm, d), lambda i: (i, 0)), pl.BlockSpec((1, d), lambda i: (0, 0)),
                  pl.BlockSpec((tm, d), lambda i: (i, 0))],
        out_specs=[pl.BlockSpec((tm, d), lambda i: (i, 0)), pl.BlockSpec((SUBLANES, LANES), lambda i: (0, 0)),
                   pl.BlockSpec((1, d), lambda i: (0, 0))],
        out_shape=[_sds((n, d), F32), _sds((SUBLANES, LANES), F32), _sds((1, d), F32)],
        compiler_params=_cparams("arbitrary"),
    )(x, g, target)


def _ffn_bwd_hidden(name, dxo, a, b, w2):
    n, d = dxo.shape
    f = a.shape[1]
    tm, tn = _pick(n, TILE["up_m"], 16), _pick(f, TILE["up_n"], LANES)

    def body(dx_ref, a_ref, b_ref, w_ref, da_ref, db_ref, hid_ref, dxh_ref):
        @pl.when(pl.program_id(1) == 0)
        def _():
            dxh_ref[...] = (FFN_RES * dx_ref[...]).astype(BF16)

        dhid = _dot_nt(dxh_ref[...], w_ref[...])
        av, bv = a_ref[...].astype(F32), b_ref[...].astype(F32)
        sig = _sigmoid(av)
        silu = av * sig
        da_ref[...] = (dhid * bv * (sig * (1.0 + av * (1.0 - sig)))).astype(BF16)
        db_ref[...] = (dhid * silu).astype(BF16)
        hid_ref[...] = (silu * bv).astype(BF16)

    tile = pl.BlockSpec((tm, tn), lambda i, j: (i, j))
    return pl.pallas_call(
        body, name=name, grid=(n // tm, f // tn),
        in_specs=[pl.BlockSpec((tm, d), lambda i, j: (i, 0)), tile, tile, pl.BlockSpec((tn, d), lambda i, j: (j, 0))],
        out_specs=[tile, tile, tile, pl.BlockSpec((tm, d), lambda i, j: (i, 0))],
        out_shape=[_sds((n, f), BF16)] * 3 + [_sds((n, d), BF16)],
        compiler_params=_cparams("parallel", "arbitrary"),
    )(dxo, a, b, w2)


def _dx_rms_bwd(name, pairs, dxo, x, g):
    n, dm = x.shape
    tm = _pick(n, TILE["row"], 16)
    npair = len(pairs)

    def body(*refs):
        d_refs, w_refs = refs[:npair], refs[npair:2 * npair]
        dxo_ref, x_ref, g_ref, dx_ref, dg_ref = refs[2 * npair:]

        @pl.when(pl.program_id(0) == 0)
        def _():
            dg_ref[...] = jnp.zeros_like(dg_ref)

        dh = None
        for d_ref, w_ref in zip(d_refs, w_refs):
            t = _dot(d_ref[...].astype(BF16), w_ref[...])
            dh = t if dh is None else dh + t
        dx, dg = _rms_bwd(x_ref[...], g_ref[...], dh)
        dx_ref[...] = dxo_ref[...] + dx
        dg_ref[...] += dg

    row = pl.BlockSpec((tm, dm), lambda i: (i, 0))
    d_specs = [pl.BlockSpec((tm, p[1]), functools.partial(lambda i, cb: (i, cb), cb=p[2])) for p in pairs]
    w_specs = [pl.BlockSpec((p[4], dm), functools.partial(lambda i, rb: (rb, 0), rb=p[5])) for p in pairs]
    return pl.pallas_call(
        body, name=name, grid=(n // tm,),
        in_specs=d_specs + w_specs + [row, row, pl.BlockSpec((1, dm), lambda i: (0, 0))],
        out_specs=[row, pl.BlockSpec((1, dm), lambda i: (0, 0))],
        out_shape=[_sds((n, dm), F32), _sds((1, dm), F32)],
        compiler_params=_cparams("arbitrary"),
    )(*[p[0] for p in pairs], *[p[3] for p in pairs], dxo, x, g)


def _mm_tn(name, a, b, a_cols=None, b_cols=None):
    n = a.shape[0]
    a0, ma = a_cols if a_cols else (0, a.shape[1])
    b0, mb = b_cols if b_cols else (0, b.shape[1])
    assert a0 % ma == 0 and b0 % mb == 0
    ab, bb = a0 // ma, b0 // mb
    tk = _pick(n, TILE["mm_bytes"] // (ma * a.dtype.itemsize + mb * b.dtype.itemsize), 16)

    def body(a_ref, b_ref, o_ref):
        @pl.when(pl.program_id(0) == 0)
        def _():
            o_ref[...] = jnp.zeros_like(o_ref)

        o_ref[...] += _dot_tn(a_ref[...].astype(BF16), b_ref[...].astype(BF16))

    return pl.pallas_call(
        body, name=name, grid=(n // tk,),
        in_specs=[pl.BlockSpec((tk, ma), lambda k: (k, ab)), pl.BlockSpec((tk, mb), lambda k: (k, bb))],
        out_specs=pl.BlockSpec((ma, mb), lambda k: (0, 0)),
        out_shape=_sds((ma, mb), F32),
        compiler_params=_cparams("arbitrary"),
    )(a, b)


def _row_mm(name, pairs, out_w, out_dtype, add=None):
    n = pairs[0][0].shape[0]
    tm = _pick(n, TILE["row"], 16)
    npair = len(pairs)

    def body(*refs):
        a_refs, w_refs = refs[:npair], refs[npair:2 * npair]
        add_ref = refs[2 * npair] if add is not None else None
        o_ref = refs[-1]
        acc = None
        for a_ref, w_ref, p in zip(a_refs, w_refs, pairs):
            av = a_ref[...].astype(BF16)
            t = _dot_nt(av, w_ref[...]) if p[6] else _dot(av, w_ref[...])
            acc = t if acc is None else acc + t
        if add_ref is not None:
            acc = acc + add_ref[...].astype(F32)
        o_ref[...] = acc.astype(o_ref.dtype)

    a_specs = [pl.BlockSpec((tm, p[1]), functools.partial(lambda i, cb: (i, cb), cb=p[2])) for p in pairs]
    w_specs = [pl.BlockSpec((p[4], p[3].shape[1]), functools.partial(lambda i, rb: (rb, 0), rb=p[5])) for p in pairs]
    add_specs = [pl.BlockSpec((tm, out_w), lambda i: (i, 0))] if add is not None else []
    return pl.pallas_call(
        body, name=name, grid=(n // tm,),
        in_specs=a_specs + w_specs + add_specs,
        out_specs=pl.BlockSpec((tm, out_w), lambda i: (i, 0)),
        out_shape=_sds((n, out_w), out_dtype),
        compiler_params=_cparams("parallel"),
    )(*[p[0] for p in pairs], *[p[3] for p in pairs], *([add] if add is not None else []))


def _conv_post(c, ln_g, ln_b, out_g):
    mu = jnp.mean(c, axis=-1, keepdims=True)
    xc = c - mu
    rstd = lax.rsqrt(jnp.mean(xc * xc, axis=-1, keepdims=True) + EPS)
    nrm = xc * rstd
    l = nrm * ln_g + ln_b
    sig = _sigmoid(l)
    s = l * sig
    r, sh = _rms_stats(s)
    return sh * out_g, (rstd, nrm, l, sig, r, sh)


def _conv_taps(a_ref, w_ref, first, rows):
    acc = None
    for k in range(CONV_WIDTH):
        t = w_ref[k:k + 1, :] * a_ref[pl.ds(first + k, rows), :]
        acc = t if acc is None else acc + t
    return acc


def _conv_fwd(name, proj3, conv_w, conv_b, ln_g, ln_b, out_g):
    bsz, seq, _ = proj3.shape
    c = conv_w.shape[1]
    tt = _pick(seq, TILE["conv_t"], CONV_HALO)
    hb = tt // CONV_HALO
    first = CONV_HALO - (CONV_WIDTH - 1)

    def body(v_ref, g_ref, vp_ref, gp_ref, w_ref, cb_ref, lg_ref, lb_ref, og_ref, o_ref, a_ref):
        keep = (pl.program_id(1) > 0).astype(F32)
        a_ref[pl.ds(0, CONV_HALO), :] = keep * vp_ref[0] * _sigmoid(gp_ref[0])
        a_ref[pl.ds(CONV_HALO, tt), :] = v_ref[0] * _sigmoid(g_ref[0])
        cv = _conv_taps(a_ref, w_ref, first, tt) + cb_ref[...]
        out, _ = _conv_post(cv, lg_ref[...], lb_ref[...], og_ref[...])
        o_ref[0] = out.astype(BF16)

    vec = pl.BlockSpec((1, c), lambda b, i: (0, 0))
    prev = lambda col: pl.BlockSpec((1, CONV_HALO, c), lambda b, i: (b, jnp.maximum(i * hb - 1, 0), col))
    return pl.pallas_call(
        body, name=name, grid=(bsz, seq // tt),
        in_specs=[pl.BlockSpec((1, tt, c), lambda b, i: (b, i, 0)), pl.BlockSpec((1, tt, c), lambda b, i: (b, i, 1)),
                  prev(0), prev(1), pl.BlockSpec(conv_w.shape, lambda b, i: (0, 0)), vec, vec, vec, vec],
        out_specs=pl.BlockSpec((1, tt, c), lambda b, i: (b, i, 0)),
        out_shape=_sds((bsz, seq, c), BF16),
        scratch_shapes=[pltpu.VMEM((CONV_HALO + tt, c), F32)],
        compiler_params=_cparams("parallel", "arbitrary"),
    )(proj3, proj3, proj3, proj3, conv_w, conv_b, ln_g, ln_b, out_g)


def _conv_bwd(name, dmix3, proj3, conv_w, conv_b, ln_g, ln_b, out_g):
    bsz, seq, _ = proj3.shape
    c = conv_w.shape[1]
    tt = _pick(seq, TILE["conv_t"], CONV_HALO)
    hb = tt // CONV_HALO
    nt = seq // tt
    last_hb = seq // CONV_HALO - 1
    ext = tt + CONV_HALO
    first = CONV_HALO - (CONV_WIDTH - 1)

    def body(v_ref, g_ref, vp_ref, gp_ref, vn_ref, gn_ref, d_ref, dn_ref, w_ref, cb_ref, lg_ref, lb_ref, og_ref,
             o_ref, dw_ref, dcb_ref, dlg_ref, dlb_ref, dog_ref, a_ref, dc_ref):
        i = pl.program_id(1)

        @pl.when((pl.program_id(0) == 0) & (i == 0))
        def _():
            for r in (dw_ref, dcb_ref, dlg_ref, dlb_ref, dog_ref):
                r[...] = jnp.zeros_like(r)

        keep_prev = (i > 0).astype(F32)
        keep_next = (i < nt - 1).astype(F32)
        sig_g = _sigmoid(g_ref[0])
        a_ref[pl.ds(0, CONV_HALO), :] = keep_prev * vp_ref[0] * _sigmoid(gp_ref[0])
        a_ref[pl.ds(CONV_HALO, tt), :] = v_ref[0] * sig_g
        a_ref[pl.ds(CONV_HALO + tt, CONV_HALO), :] = keep_next * vn_ref[0] * _sigmoid(gn_ref[0])

        lg, og = lg_ref[...], og_ref[...]
        cv = _conv_taps(a_ref, w_ref, first, ext) + cb_ref[...]
        _, (rstd, nrm, l, sig, r, sh) = _conv_post(cv, lg, lb_ref[...], og)
        own = (lax.broadcasted_iota(jnp.int32, (ext, 1), 0) < tt).astype(F32)
        dout = jnp.concatenate([d_ref[0], keep_next * dn_ref[0]], axis=0)
        dog_ref[...] += jnp.sum(own * dout * sh, axis=0, keepdims=True)
        dsh = dout * og
        ds = r * (dsh - sh * jnp.mean(dsh * sh, axis=-1, keepdims=True))
        dl = ds * (sig * (1.0 + l * (1.0 - sig)))
        dlg_ref[...] += jnp.sum(own * dl * nrm, axis=0, keepdims=True)
        dlb_ref[...] += jnp.sum(own * dl, axis=0, keepdims=True)
        dn = dl * lg
        dc = rstd * (dn - jnp.mean(dn, axis=-1, keepdims=True) - nrm * jnp.mean(dn * nrm, axis=-1, keepdims=True))
        dc_ref[...] = dc
        dc_own = dc_ref[pl.ds(0, tt), :]
        dcb_ref[...] += jnp.sum(dc_own, axis=0, keepdims=True)

        da = None
        for k in range(CONV_WIDTH):
            t = w_ref[k:k + 1, :] * dc_ref[pl.ds(CONV_WIDTH - 1 - k, tt), :]
            da = t if da is None else da + t
            dw_ref[k:k + 1, :] += jnp.sum(dc_own * a_ref[pl.ds(first + k, tt), :], axis=0, keepdims=True)
        val = v_ref[0]
        o_ref[0] = jnp.concatenate([da * sig_g, da * val * sig_g * (1.0 - sig_g)], axis=-1).astype(BF16)

    vec = pl.BlockSpec((1, c), lambda b, i: (0, 0))
    cur = lambda col: pl.BlockSpec((1, tt, c), lambda b, i: (b, i, col))
    prev = lambda col: pl.BlockSpec((1, CONV_HALO, c), lambda b, i: (b, jnp.maximum(i * hb - 1, 0), col))
    nxt = lambda col: pl.BlockSpec((1, CONV_HALO, c), lambda b, i: (b, jnp.minimum((i + 1) * hb, last_hb), col))
    wspec = pl.BlockSpec(conv_w.shape, lambda b, i: (0, 0))
    return pl.pallas_call(
        body, name=name, grid=(bsz, nt),
        in_specs=[cur(0), cur(1), prev(0), prev(1), nxt(0), nxt(1), cur(0), nxt(0), wspec, vec, vec, vec, vec],
        out_specs=[pl.BlockSpec((1, tt, 2 * c), lambda b, i: (b, i, 0)), wspec, vec, vec, vec, vec],
        out_shape=[_sds((bsz, seq, 2 * c), BF16), _sds(conv_w.shape, F32)] + [_sds((1, c), F32)] * 4,
        scratch_shapes=[pltpu.VMEM((CONV_HALO + tt + CONV_HALO, c), F32), pltpu.VMEM((ext, c), F32)],
        compiler_params=_cparams("arbitrary", "arbitrary"),
    )(proj3, proj3, proj3, proj3, proj3, proj3, dmix3, dmix3, conv_w, conv_b, ln_g, ln_b, out_g)


def _ssm_discretise(a_re, a_im, log_dt):
    dt = jnp.exp(log_dt)
    zr, zi = a_re * dt, a_im * dt
    mag = jnp.exp(zr)
    ar, ai = mag * jnp.cos(zi), mag * jnp.sin(zi)
    den = a_re * a_re + a_im * a_im
    nr = ar - 1.0
    return ar, ai, (nr * a_re + ai * a_im) / den, (ai * a_re - nr * a_im) / den


def _ssm_system(a_re, a_im, log_dt, a_re_x, a_im_x, log_dt_x, bt_re, bt_im):
    ar, ai, _, _ = _ssm_discretise(a_re, a_im, log_dt)
    _, _, cr, ci = _ssm_discretise(a_re_x, a_im_x, log_dt_x)
    return ar, ai, cr * bt_re - ci * bt_im, cr * bt_im + ci * bt_re


def _ssm_prep(name, prim):
    g, p = prim[0].shape

    def body(*refs):
        pwr_ref, pwi_ref, bbr_ref, bbi_ref = refs[8:]
        ar, ai, bbr, bbi = _ssm_system(*[r[...] for r in refs[:8]])
        bbr_ref[...] = bbr
        bbi_ref[...] = bbi
        pr, pi = ar, ai
        for k in range(SUBLANES):
            pwr_ref[k] = pr
            pwi_ref[k] = pi
            pr, pi = pr * ar - pi * ai, pr * ai + pi * ar

    return pl.pallas_call(
        body, name=name,
        out_shape=[_sds((SUBLANES, g, p), F32)] * 2 + [_sds(prim[6].shape, F32)] * 2,
        compiler_params=pltpu.CompilerParams(vmem_limit_bytes=VMEM_LIMIT),
    )(*prim)


def _ssm_param_grads(name, prim, dab_r, dab_i, dbb_r, dbb_i):
    g, p = prim[0].shape
    h = prim[6].shape[0] // g

    def body(*refs):
        dar_ref, dai_ref, dbr_ref, dbi_ref = refs[8:12]
        o_ar, o_ai, o_dt, o_br, o_bi = refs[12:]
        _, vjp = jax.vjp(_ssm_system, *[r[...] for r in refs[:8]])
        ct = (jnp.sum(dar_ref[...], axis=0), jnp.sum(dai_ref[...], axis=0), dbr_ref[...], dbi_ref[...])
        d_ar, d_ai, d_dt, d_arx, d_aix, d_dtx, d_br, d_bi = vjp(ct)
        per_group = lambda t: jnp.sum(t.reshape(g, h, p), axis=1)
        o_ar[...] = d_ar + per_group(d_arx)
        o_ai[...] = d_ai + per_group(d_aix)
        o_dt[...] = d_dt + jnp.sum(per_group(d_dtx), axis=1, keepdims=True)
        o_br[...] = d_br
        o_bi[...] = d_bi

    return pl.pallas_call(
        body, name=name,
        out_shape=[_sds(prim[k].shape, F32) for k in (0, 1, 2, 6, 7)],
        compiler_params=pltpu.CompilerParams(vmem_limit_bytes=VMEM_LIMIT),
    )(*prim, dab_r, dab_i, dbb_r, dbb_i)


def _cfma(xr, xi, cr, ci, sr, si):
    return xr + (cr * sr - ci * si), xi + (cr * si + ci * sr)


def _scan_tables(pw_r, pw_i, reverse):
    gp = pw_r.shape[1] * pw_r.shape[2]
    pr, pi = pw_r.reshape(SUBLANES, gp), pw_i.reshape(SUBLANES, gp)
    if reverse:
        pi = -pi
    row = jnp.arange(SUBLANES)[:, None]
    tabs = []
    for d in (1, 2, 4):
        keep = (row < SUBLANES - d) if reverse else (row >= d)
        tabs += [jnp.where(keep, pr[d - 1][None, :], 0.0), jnp.where(keep, pi[d - 1][None, :], 0.0)]
    tabs += [pr[::-1], pi[::-1]] if reverse else [pr, pi]
    return jnp.concatenate(tabs, axis=0)


def _scan_fwd(name, tab, bu3):
    bsz, seq, w = bu3.shape
    gp = w // 2
    tt = _pick(seq, TILE["scan_t"], SUBLANES)
    nblk = tt // SUBLANES
    cw = _pick(gp, TILE["scan_w"], LANES)

    def body(tab_ref, bu_ref, xs_ref, carry_ref):
        @pl.when(pl.program_id(1) == 0)
        def _():
            carry_ref[...] = jnp.zeros_like(carry_ref)

        for ch in range(gp // cw):
            re, im = pl.ds(ch * cw, cw), pl.ds(gp + ch * cw, cw)

            def blk(r, carry, re=re, im=im):
                tabs = [tab_ref[pl.ds(SUBLANES * k, SUBLANES), re] for k in range(8)]
                rows = pl.ds(pl.multiple_of(r * SUBLANES, SUBLANES), SUBLANES)
                xr, xi = bu_ref[0, rows, re], bu_ref[0, rows, im]
                for j, d in enumerate((1, 2, 4)):
                    xr, xi = _cfma(xr, xi, tabs[2 * j], tabs[2 * j + 1], pltpu.roll(xr, d, 0), pltpu.roll(xi, d, 0))
                xr, xi = _cfma(xr, xi, tabs[6], tabs[7], carry[0], carry[1])
                xs_ref[0, rows, re] = xr
                xs_ref[0, rows, im] = xi
                last = SUBLANES - 1
                return (jnp.broadcast_to(xr[last:, :], xr.shape), jnp.broadcast_to(xi[last:, :], xi.shape))

            cr, ci = lax.fori_loop(0, nblk, blk, (carry_ref[:, re], carry_ref[:, im]))
            carry_ref[:, re] = cr
            carry_ref[:, im] = ci

    return pl.pallas_call(
        body, name=name, grid=(bsz, seq // tt),
        in_specs=[pl.BlockSpec(tab.shape, lambda b, t: (0, 0)), pl.BlockSpec((1, tt, w), lambda b, t: (b, t, 0))],
        out_specs=pl.BlockSpec((1, tt, w), lambda b, t: (b, t, 0)),
        out_shape=_sds(bu3.shape, F32),
        scratch_shapes=[pltpu.VMEM((SUBLANES, w), F32)],
        compiler_params=_cparams("arbitrary", "arbitrary"),
    )(tab, bu3)


def _scan_bwd(name, tab, g3, xs3):
    bsz, seq, w = g3.shape
    gp = w // 2
    tt = _pick(seq, TILE["scan_t"], SUBLANES)
    nblk = tt // SUBLANES
    cw = _pick(gp, TILE["scan_w"], LANES)
    nt = seq // tt

    def body(tab_ref, g_ref, xs_ref, halo_ref, lam_ref, dar_ref, dai_ref, carry_ref):
        t = pl.program_id(1)

        @pl.when(t == 0)
        def _():
            carry_ref[...] = jnp.zeros_like(carry_ref)

        @pl.when((pl.program_id(0) == 0) & (t == 0))
        def _():
            dar_ref[...] = jnp.zeros_like(dar_ref)
            dai_ref[...] = jnp.zeros_like(dai_ref)

        has_prev = (t < nt - 1).astype(F32)
        row0 = lax.broadcasted_iota(jnp.int32, (SUBLANES, cw), 0) == 0
        last = SUBLANES - 1

        for ch in range(gp // cw):
            re, im = pl.ds(ch * cw, cw), pl.ds(gp + ch * cw, cw)

            def step(rows, xm1r, xm1i, state, re=re, im=im):
                tabs = [tab_ref[pl.ds(SUBLANES * k, SUBLANES), re] for k in range(8)]
                cr, ci, accr, acci = state
                lr, li = g_ref[0, rows, re], g_ref[0, rows, im]
                for j, d in enumerate((1, 2, 4)):
                    lr, li = _cfma(lr, li, tabs[2 * j], tabs[2 * j + 1],
                                   pltpu.roll(lr, SUBLANES - d, 0), pltpu.roll(li, SUBLANES - d, 0))
                lr, li = _cfma(lr, li, tabs[6], tabs[7], cr, ci)
                lam_ref[0, rows, re] = lr
                lam_ref[0, rows, im] = li
                xr, xi = xs_ref[0, rows, re], xs_ref[0, rows, im]
                xpr = jnp.where(row0, jnp.broadcast_to(xm1r[last:, :], xr.shape), pltpu.roll(xr, 1, 0))
                xpi = jnp.where(row0, jnp.broadcast_to(xm1i[last:, :], xi.shape), pltpu.roll(xi, 1, 0))
                accr = accr + (lr * xpr + li * xpi)
                acci = acci + (li * xpr - lr * xpi)
                return (jnp.broadcast_to(lr[:1, :], lr.shape), jnp.broadcast_to(li[:1, :], li.shape), accr, acci)

            def blk(k, state, re=re, im=im, step=step):
                r = nblk - 1 - k
                rows = pl.ds(pl.multiple_of(r * SUBLANES, SUBLANES), SUBLANES)
                prev = pl.ds(pl.multiple_of((r - 1) * SUBLANES, SUBLANES), SUBLANES)
                return step(rows, xs_ref[0, prev, re], xs_ref[0, prev, im], state)

            zero = jnp.zeros((SUBLANES, cw), F32)
            state = lax.fori_loop(0, nblk - 1, blk, (carry_ref[:, re], carry_ref[:, im], zero, zero))
            cr, ci, accr, acci = step(pl.ds(0, SUBLANES), has_prev * halo_ref[0, :, re], has_prev * halo_ref[0, :, im], state)
            carry_ref[:, re] = cr
            carry_ref[:, im] = ci
            dar_ref[:, re] += accr
            dai_ref[:, re] += acci

    tile = pl.BlockSpec((1, tt, w), lambda b, t: (b, nt - 1 - t, 0))
    halo = pl.BlockSpec((1, SUBLANES, w), lambda b, t: (b, jnp.maximum((nt - 1 - t) * nblk - 1, 0), 0))
    acc = pl.BlockSpec((SUBLANES, gp), lambda b, t: (0, 0))
    return pl.pallas_call(
        body, name=name, grid=(bsz, nt),
        in_specs=[pl.BlockSpec(tab.shape, lambda b, t: (0, 0)), tile, tile, halo],
        out_specs=[tile, acc, acc],
        out_shape=[_sds(g3.shape, F32), _sds((SUBLANES, gp), F32), _sds((SUBLANES, gp), F32)],
        scratch_shapes=[pltpu.VMEM((SUBLANES, w), F32)],
        compiler_params=_cparams("arbitrary", "arbitrary"),
    )(tab, g3, xs3, xs3)


def _gelu_parts(y):
    inner = _GELU_K * (y + _GELU_C * y * y * y)
    t = jnp.tanh(inner)
    return 0.5 * y * (1.0 + t), t


def _ssm_out_fwd(name, xs, proj, u_block, cdt, d_skip, glu_w, glu_b, out_g):
    n, w = xs.shape
    c = cdt.shape[0]
    tm = _pick(n, TILE["row"], 16)

    def body(xs_ref, u_ref, cdt_ref, d_ref, gw_ref, gb_ref, og_ref, y_ref, o_ref):
        y = _dot_nt(xs_ref[...].astype(BF16), cdt_ref[...]) + d_ref[...] * u_ref[...]
        y_ref[...] = y
        gy, _ = _gelu_parts(y)
        z = _dot(gy.astype(BF16), gw_ref[...]) + gb_ref[...]
        _, sh = _rms_stats(gy * _sigmoid(z))
        o_ref[...] = (sh * og_ref[...]).astype(BF16)

    vec = pl.BlockSpec((1, c), lambda i: (0, 0))
    row = pl.BlockSpec((tm, c), lambda i: (i, 0))
    return pl.pallas_call(
        body, name=name, grid=(n // tm,),
        in_specs=[pl.BlockSpec((tm, w), lambda i: (i, 0)), pl.BlockSpec((tm, c), lambda i: (i, u_block)),
                  pl.BlockSpec(cdt.shape, lambda i: (0, 0)), vec, pl.BlockSpec(glu_w.shape, lambda i: (0, 0)), vec, vec],
        out_specs=[row, row],
        out_shape=[_sds((n, c), F32), _sds((n, c), BF16)],
        compiler_params=_cparams("parallel"),
    )(xs, proj, cdt, d_skip, glu_w, glu_b, out_g)


def _ssm_out_bwd(name, dmix, d_block, y, proj, u_block, d_skip, glu_w, glu_b, out_g):
    n, c = y.shape
    tm = _pick(n, TILE["row"], 16)

    def body(d_ref, y_ref, u_ref, dk_ref, gw_ref, gb_ref, og_ref, dy_ref, du_ref, dgw_ref, dgb_ref, dog_ref, dd_ref):
        @pl.when(pl.program_id(0) == 0)
        def _():
            for r in (dgw_ref, dgb_ref, dog_ref, dd_ref):
                r[...] = jnp.zeros_like(r)

        yv = y_ref[...]
        gy, th = _gelu_parts(yv)
        gy16 = gy.astype(BF16)
        sz = _sigmoid(_dot(gy16, gw_ref[...]) + gb_ref[...])
        r, sh = _rms_stats(gy * sz)
        dout = d_ref[...]
        dog_ref[...] += jnp.sum(dout * sh, axis=0, keepdims=True)
        dsh = dout * og_ref[...]
        ds = r * (dsh - sh * jnp.mean(dsh * sh, axis=-1, keepdims=True))
        dz = ds * gy * sz * (1.0 - sz)
        dz16 = dz.astype(BF16)
        dgb_ref[...] += jnp.sum(dz, axis=0, keepdims=True)
        dgw_ref[...] += _dot_tn(gy16, dz16)
        dgy = ds * sz + _dot_nt(dz16, gw_ref[...])
        dgelu = 0.5 * (1.0 + th) + 0.5 * yv * (1.0 - th * th) * (_GELU_K * (1.0 + 3.0 * _GELU_C * yv * yv))
        dy = dgy * dgelu
        dy_ref[...] = dy.astype(BF16)
        du_ref[...] = dy * dk_ref[...]
        dd_ref[...] += jnp.sum(dy * u_ref[...], axis=0, keepdims=True)

    vec = pl.BlockSpec((1, c), lambda i: (0, 0))
    row = pl.BlockSpec((tm, c), lambda i: (i, 0))
    mat = pl.BlockSpec(glu_w.shape, lambda i: (0, 0))
    return pl.pallas_call(
        body, name=name, grid=(n // tm,),
        in_specs=[pl.BlockSpec((tm, c), lambda i: (i, d_block)), row, pl.BlockSpec((tm, c), lambda i: (i, u_block)),
                  vec, mat, vec, vec],
        out_specs=[row, row, mat, vec, vec, vec],
        out_shape=[_sds((n, c), BF16), _sds((n, c), F32), _sds(glu_w.shape, F32)] + [_sds((1, c), F32)] * 3,
        compiler_params=_cparams("arbitrary"),
    )(dmix, y, proj, d_skip, glu_w, glu_b, out_g)


def _mesh_pos():
    return tuple(lax.axis_index(a) for a in MESH_AXES)


def _other_chips(x, y):
    return [(1 - x, y), (x, 1 - y), (1 - x, 1 - y)]


def _remote(src, dst, send_sem, recv_sem, dev):
    return pltpu.make_async_remote_copy(src_ref=src, dst_ref=dst, send_sem=send_sem, recv_sem=recv_sem,
                                        device_id=dev, device_id_type=pl.DeviceIdType.MESH)


def _hbm_call(name, body, operands, out_shapes, scratch):
    hbm = pl.BlockSpec(memory_space=pltpu.HBM)
    return pl.pallas_call(body, name=name, in_specs=[hbm] * len(operands), out_specs=[hbm] * len(out_shapes),
                          out_shape=out_shapes, scratch_shapes=scratch)(*operands)


def _all_gather(name, blocks):
    nop = len(blocks)

    def body(*refs):
        x_refs, o_refs = refs[:nop], refs[nop:2 * nop]
        send_sems, recv_sems, local_sems = refs[2 * nop:]
        x, y, c = _mesh_pos()
        me, sibling = (x, y, c), (x, y, 1 - c)
        chips = _other_chips(x, y)

        def copy(i, k, block_of, to, src=None):
            dst = o_refs[i].at[4 * block_of[0] + 2 * block_of[1] + block_of[2]]
            return _remote(dst if src is None else src, dst, send_sems.at[i, k], recv_sems.at[i, k], to)

        own = [pltpu.make_async_copy(x_refs[i], o_refs[i].at[4 * x + 2 * y + c], local_sems.at[i]) for i in range(nop)]
        for cp in own:
            cp.start()
        first = []
        for i in range(nop):
            first.append(copy(i, 0, me, sibling, src=x_refs[i]))
            first += [copy(i, 1 + j, me, (*chip, c), src=x_refs[i]) for j, chip in enumerate(chips)]
        for cp in first:
            cp.start()
        passed = []
        for i in range(nop):
            for j, chip in enumerate(chips):
                copy(i, 1 + j, (*chip, c), me).wait_recv()
                passed.append(copy(i, 4 + j, (*chip, c), sibling))
                passed[-1].start()
        for i in range(nop):
            copy(i, 0, sibling, me).wait_recv()
            for j, chip in enumerate(chips):
                copy(i, 4 + j, (*chip, 1 - c), me).wait_recv()
        for cp in first + passed:
            cp.wait_send()
        for cp in own:
            cp.wait()

    return _hbm_call(name, body, blocks, [_sds((N_DEV,) + b.shape, b.dtype) for b in blocks],
                     [pltpu.SemaphoreType.DMA((nop, N_DEV - 1)), pltpu.SemaphoreType.DMA((nop, N_DEV - 1)),
                      pltpu.SemaphoreType.DMA((nop,))])


def _exchange_sibling(name, grads):
    nop = len(grads)

    def body(*refs):
        x_refs, o_refs = refs[:nop], refs[nop:2 * nop]
        send_sems, recv_sems = refs[2 * nop:]
        x, y, c = _mesh_pos()
        copies = [_remote(x_refs[i].at[2 * q + (1 - c)], o_refs[i].at[q], send_sems.at[i, q], recv_sems.at[i, q],
                          (x, y, 1 - c)) for i in range(nop) for q in range(N_DEV // 2)]
        for cp in copies:
            cp.start()
        for cp in copies:
            cp.wait_recv()
        for cp in copies:
            cp.wait_send()

    return _hbm_call(name, body, grads, [_sds((N_DEV // 2,) + g.shape[1:], g.dtype) for g in grads],
                     [pltpu.SemaphoreType.DMA((nop, N_DEV // 2)), pltpu.SemaphoreType.DMA((nop, N_DEV // 2))])


def _pair_sum(name, grad, other):
    nchip, _, r, c = grad.shape
    tr = _pick(r, max(SUBLANES, TILE["sum_bytes"] // (8 * c)), SUBLANES)

    def body(g_ref, o_ref, s_ref):
        mine = jnp.where(lax.axis_index("c") == 0, g_ref[0, 0], g_ref[0, 1])
        s_ref[0] = mine + o_ref[0]

    return pl.pallas_call(
        body, name=name, grid=(nchip, r // tr),
        in_specs=[pl.BlockSpec((1, 2, tr, c), lambda q, t: (q, 0, t, 0)), pl.BlockSpec((1, tr, c), lambda q, t: (q, t, 0))],
        out_specs=pl.BlockSpec((1, tr, c), lambda q, t: (q, t, 0)),
        out_shape=_sds((nchip, r, c), F32),
        compiler_params=_cparams("parallel", "parallel"),
    )(grad, other)


def _exchange_chips(name, sums):
    nop = len(sums)

    def body(*refs):
        x_refs, o_refs = refs[:nop], refs[nop:2 * nop]
        send_sems, recv_sems, local_sems = refs[2 * nop:]
        x, y, c = _mesh_pos()
        mine = 2 * x + y
        own = [pltpu.make_async_copy(x_refs[i].at[mine], o_refs[i].at[mine], local_sems.at[i]) for i in range(nop)]
        for cp in own:
            cp.start()
        sends, recvs = [], []
        for i in range(nop):
            for j, (px, py) in enumerate(_other_chips(x, y)):
                theirs = 2 * px + py
                sends.append(_remote(x_refs[i].at[theirs], o_refs[i].at[mine], send_sems.at[i, j], recv_sems.at[i, j],
                                     (px, py, c)))
                recvs.append(_remote(x_refs[i].at[mine], o_refs[i].at[theirs], send_sems.at[i, j], recv_sems.at[i, j],
                                     (px, py, c)))
        for cp in sends:
            cp.start()
        for cp in recvs:
            cp.wait_recv()
        for cp in sends:
            cp.wait_send()
        for cp in own:
            cp.wait()

    return _hbm_call(name, body, sums, [_sds(s.shape, s.dtype) for s in sums],
                     [pltpu.SemaphoreType.DMA((nop, 3)), pltpu.SemaphoreType.DMA((nop, 3)), pltpu.SemaphoreType.DMA((nop,))])


def _part_rows(npart, r, c):
    return _pick(r, max(SUBLANES, TILE["sum_bytes"] // (4 * npart * c)), SUBLANES)


def _sum_parts(name, parts):
    npart, r, c = parts.shape
    tr = _part_rows(npart, r, c)

    def body(p_ref, o_ref):
        g = p_ref[0]
        for k in range(1, npart):
            g = g + p_ref[k]
        o_ref[...] = g

    return pl.pallas_call(
        body, name=name, grid=(r // tr,),
        in_specs=[pl.BlockSpec((npart, tr, c), lambda i: (0, i, 0))],
        out_specs=pl.BlockSpec((tr, c), lambda i: (i, 0)),
        out_shape=_sds((r, c), F32),
        compiler_params=_cparams("parallel"),
    )(parts)


def _adamw(name, parts, w, m, v):
    npart, r, c = parts.shape
    tr = _part_rows(npart, r, c)
    c1 = 1.0 - ADAM_B1 ** ADAM_STEP
    c2 = 1.0 - ADAM_B2 ** ADAM_STEP

    def body(p_ref, w_ref, m_ref, v_ref, g_ref, d_ref, nm_ref, nv_ref):
        g = p_ref[0]
        for k in range(1, npart):
            g = g + p_ref[k]
        nm = ADAM_B1 * m_ref[...] + (1.0 - ADAM_B1) * g
        nv = ADAM_B2 * v_ref[...] + (1.0 - ADAM_B2) * (g * g)
        g_ref[...] = g
        nm_ref[...] = nm
        nv_ref[...] = nv
        d_ref[...] = -ADAM_LR * ((nm / c1) / (jnp.sqrt(nv / c2) + ADAM_EPS) + ADAM_WD * w_ref[...])

    row = pl.BlockSpec((tr, c), lambda i: (i, 0))
    return pl.pallas_call(
        body, name=name, grid=(r // tr,),
        in_specs=[pl.BlockSpec((npart, tr, c), lambda i: (0, i, 0)), row, row, row],
        out_specs=[row] * 4,
        out_shape=[_sds((r, c), F32)] * 4,
        compiler_params=_cparams("parallel"),
    )(parts, w, m, v)


def _pack(pieces, row_mult, lead=()):
    nl = len(lead)
    flat, spans, off = [], [], 0
    for p in pieces:
        p = p.reshape(lead + (-1,))
        size = p.shape[-1]
        padded = -(-size // PACK_W) * PACK_W
        flat.append(jnp.pad(p, [(0, 0)] * nl + [(0, padded - size)]))
        spans.append((off, size))
        off += padded
    rows = -(-(off // PACK_W) // row_mult) * row_mult
    if rows * PACK_W > off:
        flat.append(jnp.zeros(lead + (rows * PACK_W - off,), flat[0].dtype))
    return jnp.concatenate(flat, axis=-1).reshape(lead + (rows, PACK_W)), spans


def _unpack(buf, spans, shapes, lead=0):
    flat = buf.reshape(buf.shape[:lead] + (-1,))
    return [flat[..., o:o + s].reshape(buf.shape[:lead] + tuple(shape)) for (o, s), shape in zip(spans, shapes)]


def _block_diag(rows_gh, groups):
    gh, p = rows_gh.shape
    own = (jnp.arange(gh)[:, None] // (gh // groups) == jnp.arange(groups)[None, :]).astype(rows_gh.dtype)
    return (own[:, :, None] * rows_gh[:, None, :]).reshape(gh, groups * p)


def _block_diag_take(dense, groups):
    gh = dense.shape[0]
    p = dense.shape[1] // groups
    own = (jnp.arange(gh)[:, None] // (gh // groups) == jnp.arange(groups)[None, :]).astype(dense.dtype)
    return jnp.sum(dense.reshape(gh, groups, p) * own[:, :, None], axis=1)


BIG = ("ffn1_w1", "ffn1_w3", "ffn1_w2", "w_in", "ssm_glu_w", "w_out", "ffn2_w1", "ffn2_w3", "ffn2_w2")
COL_SHARDED = ("ffn1_w1", "ffn1_w3", "w_in", "ffn2_w1", "ffn2_w3", "conv_w")
SMALL = ("norm_ffn1", "norm_mix", "conv_b", "conv_ln_g", "conv_ln_b", "conv_out_g", "ssm_A_re", "ssm_A_im",
         "ssm_log_dt", "ssm_B_re", "ssm_B_im", "ssm_C_re", "ssm_C_im", "ssm_D", "ssm_glu_b", "ssm_out_g",
         "norm_ffn2", "norm_final")
WEIGHTS = ("norm_ffn1", "ffn1_w1", "ffn1_w3", "ffn1_w2", "norm_mix", "w_in", "conv_w", "conv_b", "conv_ln_g",
           "conv_ln_b", "conv_out_g", "ssm_A_re", "ssm_A_im", "ssm_log_dt", "ssm_B_re", "ssm_B_im", "ssm_C_re",
           "ssm_C_im", "ssm_D", "ssm_glu_w", "ssm_glu_b", "ssm_out_g", "w_out", "norm_ffn2", "ffn2_w1", "ffn2_w3",
           "ffn2_w2", "norm_final")


def _ffn_fwd(tag, x, g, w1, w3, w2):
    (a, b), h = _rms_mm(tag + "_up", x, g, [w1, w3], BF16)
    return _swiglu_down(tag + "_down", x, a, b, w2), (a, b, h)


def _ffn_bwd(tag, dxo, x, g, w1, w3, w2, saved):
    a, b, h = saved
    da, db, hid, dxh = _ffn_bwd_hidden(tag + "_bwd_hidden", dxo, a, b, w2)
    f = a.shape[1]
    dx, dg = _dx_rms_bwd(tag + "_bwd_dx", [(da, f, 0, w1, f, 0), (db, f, 0, w3, f, 0)], dxo, x, g)
    return dx, dg, _mm_tn(tag + "_dw1", da, h), _mm_tn(tag + "_dw3", db, h), _mm_tn(tag + "_dw2", hid, dxh)


def kernel(x, norm_ffn1, ffn1_w1, ffn1_w3, ffn1_w2, norm_mix, w_in, conv_w, conv_b, conv_ln_g, conv_ln_b, conv_out_g, ssm_A_re, ssm_A_im, ssm_log_dt, ssm_B_re, ssm_B_im, ssm_C_re, ssm_C_im, ssm_D, ssm_glu_w, ssm_glu_b, ssm_out_g, w_out, norm_ffn2, ffn2_w1, ffn2_w3, ffn2_w2, norm_final, loss_target, m_norm_ffn1, m_ffn1_w1, m_ffn1_w3, m_ffn1_w2, m_norm_mix, m_w_in, m_conv_w, m_conv_b, m_conv_ln_g, m_conv_ln_b, m_conv_out_g, m_ssm_A_re, m_ssm_A_im, m_ssm_log_dt, m_ssm_B_re, m_ssm_B_im, m_ssm_C_re, m_ssm_C_im, m_ssm_D, m_ssm_glu_w, m_ssm_glu_b, m_ssm_out_g, m_w_out, m_norm_ffn2, m_ffn2_w1, m_ffn2_w3, m_ffn2_w2, m_norm_final, v_norm_ffn1, v_ffn1_w1, v_ffn1_w3, v_ffn1_w2, v_norm_mix, v_w_in, v_conv_w, v_conv_b, v_conv_ln_g, v_conv_ln_b, v_conv_out_g, v_ssm_A_re, v_ssm_A_im, v_ssm_log_dt, v_ssm_B_re, v_ssm_B_im, v_ssm_C_re, v_ssm_C_im, v_ssm_D, v_ssm_glu_w, v_ssm_glu_b, v_ssm_out_g, v_w_out, v_norm_ffn2, v_ffn2_w1, v_ffn2_w3, v_ffn2_w2, v_norm_final):
    args = dict(locals())
    wt = {n: args[n] for n in WEIGHTS}
    mom = {n: args["m_" + n] for n in WEIGHTS}
    var = {n: args["v_" + n] for n in WEIGHTS}

    bsz, seq, d = x.shape
    n = bsz * seq
    c = conv_b.shape[-1]
    groups = c // SSM_GROUP
    gp = groups * SSM_STATE
    u_b = 2

    shards = [(wt[k][0].T if k in COL_SHARDED else wt[k][0]).astype(BF16) for k in BIG] + [wt["conv_w"][0]]
    gathered = _all_gather("gather_weights", shards)
    full = {k: g.reshape(-1, g.shape[-1]) for k, g in zip(BIG, gathered)}
    conv_w_full = gathered[-1].transpose(1, 0, 2).reshape(CONV_WIDTH, c)
    conv_w_pad = jnp.pad(conv_w_full, ((0, CONV_HALO - CONV_WIDTH), (0, 0)))

    vec = lambda k: wt[k].reshape(1, -1)
    g_ffn1, g_mix, g_ffn2, g_fin = vec("norm_ffn1"), vec("norm_mix"), vec("norm_ffn2"), vec("norm_final")
    cb, lng, lnb, cog = vec("conv_b"), vec("conv_ln_g"), vec("conv_ln_b"), vec("conv_out_g")
    d_skip, glu_b, sog = vec("ssm_D"), vec("ssm_glu_b"), vec("ssm_out_g")

    a_re, a_im = wt["ssm_A_re"][0], wt["ssm_A_im"][0]
    log_dt = wt["ssm_log_dt"][0].reshape(groups, 1)
    bt_re = wt["ssm_B_re"][0].transpose(0, 2, 1).reshape(groups * SSM_GROUP, SSM_STATE)
    bt_im = wt["ssm_B_im"][0].transpose(0, 2, 1).reshape(groups * SSM_GROUP, SSM_STATE)
    c_re = wt["ssm_C_re"][0].reshape(groups * SSM_GROUP, SSM_STATE)
    c_im = wt["ssm_C_im"][0].reshape(groups * SSM_GROUP, SSM_STATE)
    per_chan = lambda t: jnp.repeat(t, SSM_GROUP, axis=0)
    ssm_prim = (a_re, a_im, log_dt, per_chan(a_re), per_chan(a_im), per_chan(jnp.broadcast_to(log_dt, a_re.shape)),
                bt_re, bt_im)
    pw_r, pw_i, bb_r, bb_i = _ssm_prep("ssm_prep", ssm_prim)
    tab_f = _scan_tables(pw_r, pw_i, False)
    tab_b = _scan_tables(pw_r, pw_i, True)
    bbd = jnp.concatenate([_block_diag(bb_r, groups), _block_diag(bb_i, groups)], axis=1).astype(BF16)
    cdt = jnp.concatenate([_block_diag(c_re, groups), -_block_diag(c_im, groups)], axis=1).astype(BF16)

    x0 = x.reshape(n, d)
    x1, ffn1_saved = _ffn_fwd("ffn1", x0, g_ffn1, full["ffn1_w1"], full["ffn1_w3"], full["ffn1_w2"])
    (proj,), h2 = _rms_mm("mix_in", x1, g_mix, [full["w_in"]], F32)
    proj3 = proj.reshape(bsz, seq, 3 * c)
    an = _conv_fwd("conv_fwd", proj3, conv_w_pad, cb, lng, lnb, cog).reshape(n, c)
    bu = _row_mm("ssm_bu", [(proj, c, u_b, bbd, c, 0, False)], 2 * gp, F32)
    xs = _scan_fwd("scan_fwd", tab_f, bu.reshape(bsz, seq, 2 * gp)).reshape(n, 2 * gp)
    y, sn = _ssm_out_fwd("ssm_out_fwd", xs, proj, u_b, cdt, d_skip, full["ssm_glu_w"], glu_b, sog)
    w_o = full["w_out"]
    x2 = _row_mm("mix_out", [(an, c, 0, w_o, c, 0, False), (sn, c, 0, w_o, c, 1, False)], d, F32, add=x1)
    x3, ffn2_saved = _ffn_fwd("ffn2", x2, g_ffn2, full["ffn2_w1"], full["ffn2_w3"], full["ffn2_w2"])
    dx3, loss_tile, d_gfin = _loss_head("loss_head", x3, g_fin, loss_target.reshape(n, d))
    loss = lax.psum(loss_tile[0, 0], MESH_AXES)

    grads = {}
    dx2, grads["norm_ffn2"], grads["ffn2_w1"], grads["ffn2_w3"], grads["ffn2_w2"] = _ffn_bwd(
        "ffn2", dx3, x2, g_ffn2, full["ffn2_w1"], full["ffn2_w3"], full["ffn2_w2"], ffn2_saved)

    dmix = _row_mm("mix_out_bwd", [(dx2, d, 0, w_o, 2 * c, 0, True)], 2 * c, F32)
    grads["w_out"] = jnp.concatenate([_mm_tn("dw_out_a", an, dx2), _mm_tn("dw_out_s", sn, dx2)], axis=0)

    dy, du_skip, grads["ssm_glu_w"], grads["ssm_glu_b"], grads["ssm_out_g"], grads["ssm_D"] = _ssm_out_bwd(
        "ssm_out_bwd", dmix, 1, y, proj, u_b, d_skip, full["ssm_glu_w"], glu_b, sog)
    gx = _row_mm("ssm_dx", [(dy, c, 0, cdt, c, 0, False)], 2 * gp, F32)
    lam3, dab_r, dab_i = _scan_bwd("scan_bwd", tab_b, gx.reshape(bsz, seq, 2 * gp), xs.reshape(bsz, seq, 2 * gp))
    lam = lam3.reshape(n, 2 * gp)
    du = _row_mm("ssm_du", [(lam, 2 * gp, 0, bbd, c, 0, True)], c, BF16, add=du_skip)
    d_bbd = _mm_tn("ssm_dbb", proj, lam, a_cols=(u_b * c, c))
    d_cdt = _mm_tn("ssm_dc", dy, xs)
    d_are, d_aim, d_ldt, d_btr, d_bti = _ssm_param_grads(
        "ssm_param_grads", ssm_prim,
        dab_r.reshape(SUBLANES, groups, SSM_STATE), dab_i.reshape(SUBLANES, groups, SSM_STATE),
        _block_diag_take(d_bbd[:, :gp], groups), _block_diag_take(d_bbd[:, gp:], groups))
    grads["ssm_A_re"], grads["ssm_A_im"], grads["ssm_log_dt"] = d_are, d_aim, d_ldt
    grads["ssm_B_re"] = d_btr.reshape(groups, SSM_GROUP, SSM_STATE).transpose(0, 2, 1)
    grads["ssm_B_im"] = d_bti.reshape(groups, SSM_GROUP, SSM_STATE).transpose(0, 2, 1)
    grads["ssm_C_re"] = _block_diag_take(d_cdt[:, :gp], groups)
    grads["ssm_C_im"] = -_block_diag_take(d_cdt[:, gp:], groups)

    dconv3, d_cw, grads["conv_b"], grads["conv_ln_g"], grads["conv_ln_b"], grads["conv_out_g"] = _conv_bwd(
        "conv_bwd", dmix.reshape(bsz, seq, 2 * c), proj3, conv_w_pad, cb, lng, lnb, cog)
    dconv = dconv3.reshape(n, 2 * c)
    grads["conv_w"] = d_cw[:CONV_WIDTH]
    grads["w_in"] = jnp.concatenate([_mm_tn("dw_in_conv", dconv, h2), _mm_tn("dw_in_ssm", du, h2)], axis=0)
    w_i = full["w_in"]
    dx1, grads["norm_mix"] = _dx_rms_bwd("mix_in_bwd", [(dconv, 2 * c, 0, w_i, 2 * c, 0), (du, c, 0, w_i, c, 2)], dx2, x1, g_mix)

    dx0, grads["norm_ffn1"], grads["ffn1_w1"], grads["ffn1_w3"], grads["ffn1_w2"] = _ffn_bwd(
        "ffn1", dx1, x0, g_ffn1, full["ffn1_w1"], full["ffn1_w3"], full["ffn1_w2"], ffn1_saved)
    grads["norm_final"] = d_gfin

    send = [grads[k].reshape((N_DEV, -1) + grads[k].shape[1:]) for k in BIG]
    from_core = _exchange_sibling("exchange_grads_core", send)
    sums = [_pair_sum("pair_sum_" + k, s.reshape((N_DEV // 2, 2) + s.shape[1:]), o)
            for k, s, o in zip(BIG, send, from_core)]
    from_chips = _exchange_chips("exchange_grads_chip", sums)
    res = {}
    for k, parts in zip(BIG, from_chips):
        if k in COL_SHARDED:
            parts = _sum_parts("sum_" + k, parts).T[None]
        res[k] = _adamw("adamw_" + k, parts, wt[k][0], mom[k][0], var[k][0])

    small_names = SMALL + ("conv_w",)
    no_state = jnp.zeros_like(grads["conv_w"])
    part, spans = _pack([grads[k] for k in small_names], SUBLANES)
    (all_parts,) = _all_gather("gather_small_grads", [part])
    w_pk, _ = _pack([wt[k] for k in SMALL] + [no_state], SUBLANES)
    m_pk, _ = _pack([mom[k] for k in SMALL] + [no_state], SUBLANES)
    v_pk, _ = _pack([var[k] for k in SMALL] + [no_state], SUBLANES)
    small_out = _adamw("adamw_replicated", all_parts, w_pk, m_pk, v_pk)
    small_shapes = [wt[k].shape for k in SMALL] + [grads["conv_w"].shape]
    small_res = [dict(zip(small_names, _unpack(o, spans, small_shapes))) for o in small_out]
    x_pos, y_pos, c_pos = (lax.axis_index(a) for a in MESH_AXES)
    cw_cols = c // N_DEV
    own_cw = lax.dynamic_slice_in_dim(small_res[0]["conv_w"], (4 * x_pos + 2 * y_pos + c_pos) * cw_cols, cw_cols, axis=1)
    res["conv_w"] = _adamw("adamw_conv_w", own_cw[None], wt["conv_w"][0], mom["conv_w"][0], var["conv_w"][0])

    outs = [loss, dx0.reshape(bsz, seq, d)]
    for kind in range(4):
        outs += [res[k][kind].reshape(wt[k].shape) if k in res else small_res[kind][k] for k in WEIGHTS]
    return tuple(outs)
```

```python
import functools
import math

import jax
import jax.numpy as jnp
from jax import lax
from jax.experimental import pallas as pl
from jax.experimental.pallas import tpu as pltpu

F32 = jnp.float32
BF16 = jnp.bfloat16

EPS = 1e-6
FFN_RES = 0.5
CONV_WIDTH = 31
CONV_HALO = 32
SSM_GROUP = 16
SSM_STATE = 64
ADAM_LR, ADAM_B1, ADAM_B2, ADAM_EPS, ADAM_WD, ADAM_STEP = 0.001, 0.9, 0.999, 1e-08, 0.01, 10

N_DEV = 8
MESH_AXES = ("x", "y", "c")
SUBLANES = 8
LANES = 128
PACK_W = 1024
V7X_VMEM_BYTES = 64 * 2**20
VMEM_LIMIT = V7X_VMEM_BYTES - 8 * 2**20

TILE = dict(row=256, mm_bytes=8 * 2**20, up_m=1024, up_n=256, conv_t=512, scan_t=256, scan_w=512,
            sum_bytes=4 * 2**20)

_GELU_K = math.sqrt(2.0 / math.pi)
_GELU_C = 0.044715


def _pick(n, target, mult):
    best = None
    for t in range(mult, min(n, target) + 1, mult):
        if n % t == 0:
            best = t
    return n if best is None else best


def _cparams(*sem):
    return pltpu.CompilerParams(dimension_semantics=sem, vmem_limit_bytes=VMEM_LIMIT)


def _sds(shape, dtype):
    return jax.ShapeDtypeStruct(shape, dtype)


def _dot(a, b):
    return jnp.dot(a, b, preferred_element_type=F32)


def _dot_nt(a, b):
    return lax.dot_general(a, b, (((1,), (1,)), ((), ())), preferred_element_type=F32)


def _dot_tn(a, b):
    return lax.dot_general(a, b, (((0,), (0,)), ((), ())), preferred_element_type=F32)


def _sigmoid(x):
    return 1.0 / (1.0 + jnp.exp(-x))


def _rms_stats(x):
    r = lax.rsqrt(jnp.mean(x * x, axis=-1, keepdims=True) + EPS)
    return r, x * r


def _rms_bwd(x, g, dy):
    r, xh = _rms_stats(x)
    dxh = dy * g
    dx = r * (dxh - xh * jnp.mean(dxh * xh, axis=-1, keepdims=True))
    return dx, jnp.sum(dy * xh, axis=0, keepdims=True)


def _rms_mm(name, x, g, ws, out_dtype):
    n, d = x.shape
    f = ws[0].shape[0]
    nw = len(ws)
    tm, tn = _pick(n, TILE["up_m"], 16), _pick(f, TILE["up_n"], LANES)

    def body(x_ref, g_ref, *refs):
        w_refs, o_refs, h_ref = refs[:nw], refs[nw:2 * nw], refs[2 * nw]

        @pl.when(pl.program_id(1) == 0)
        def _():
            _, xh = _rms_stats(x_ref[...])
            h_ref[...] = (xh * g_ref[...]).astype(BF16)

        h = h_ref[...]
        for w_ref, o_ref in zip(w_refs, o_refs):
            o_ref[...] = _dot_nt(h, w_ref[...]).astype(o_ref.dtype)

    outs = pl.pallas_call(
        body, name=name, grid=(n // tm, f // tn),
        in_specs=[pl.BlockSpec((tm, d), lambda i, j: (i, 0)), pl.BlockSpec((1, d), lambda i, j: (0, 0))]
        + [pl.BlockSpec((tn, d), lambda i, j: (j, 0))] * nw,
        out_specs=[pl.BlockSpec((tm, tn), lambda i, j: (i, j))] * nw + [pl.BlockSpec((tm, d), lambda i, j: (i, 0))],
        out_shape=[_sds((n, f), out_dtype)] * nw + [_sds((n, d), BF16)],
        compiler_params=_cparams("parallel", "arbitrary"),
    )(x, g, *ws)
    return outs[:nw], outs[nw]


def _swiglu_down(name, x, a, b, w2):
    n, d = x.shape
    f = a.shape[1]
    tm = _pick(n, TILE["row"], 16)

    def body(x_ref, a_ref, b_ref, w_ref, o_ref):
        av = a_ref[...].astype(F32)
        hid = (av * _sigmoid(av) * b_ref[...].astype(F32)).astype(BF16)
        o_ref[...] = x_ref[...] + FFN_RES * _dot(hid, w_ref[...])

    return pl.pallas_call(
        body, name=name, grid=(n // tm,),
        in_specs=[pl.BlockSpec((tm, d), lambda i: (i, 0)), pl.BlockSpec((tm, f), lambda i: (i, 0)),
                  pl.BlockSpec((tm, f), lambda i: (i, 0)), pl.BlockSpec((f, d), lambda i: (0, 0))],
        out_specs=pl.BlockSpec((tm, d), lambda i: (i, 0)),
        out_shape=_sds((n, d), F32),
        compiler_params=_cparams("parallel"),
    )(x, a, b, w2)


def _loss_head(name, x, g, target):
    n, d = x.shape
    tm = _pick(n, TILE["row"], SUBLANES)

    def body(x_ref, g_ref, t_ref, dx_ref, loss_ref, dg_ref):
        @pl.when(pl.program_id(0) == 0)
        def _():
            loss_ref[...] = jnp.zeros_like(loss_ref)
            dg_ref[...] = jnp.zeros_like(dg_ref)

        xv, gv = x_ref[...], g_ref[...]
        r, xh = _rms_stats(xv)
        err = xh * gv - t_ref[...]
        loss_ref[...] += 0.5 * jnp.sum(jnp.mean(err * err, axis=-1, keepdims=True))
        dy = err * (1.0 / d)
        dxh = dy * gv
        dx_ref[...] = r * (dxh - xh * jnp.mean(dxh * xh, axis=-1, keepdims=True))
        dg_ref[...] += jnp.sum(dy * xh, axis=0, keepdims=True)

    return pl.pallas_call(
        body, name=name, grid=(n // tm,),
        in_specs=[pl.BlockSpec((tm, d), lambda i: (i, 0)), pl.BlockSpec((1, d), lambda i: (0, 0)),
                  pl.BlockSpec((tm, d), lambda i: (i, 0))],
        out_specs=[pl.BlockSpec((tm, d), lambda i: (i, 0)), pl.BlockSpec((SUBLANES, LANES), lambda i: (0, 0)),
                   pl.BlockSpec((1, d), lambda i: (0, 0))],
        out_shape=[_sds((n, d), F32), _sds((SUBLANES, LANES), F32), _sds((1, d), F32)],
        compiler_params=_cparams("arbitrary"),
    )(x, g, target)


def _ffn_bwd_hidden(name, dxo, a, b, w2):
    n, d = dxo.shape
    f = a.shape[1]
    tm, tn = _pick(n, TILE["up_m"], 16), _pick(f, TILE["up_n"], LANES)

    def body(dx_ref, a_ref, b_ref, w_ref, da_ref, db_ref, hid_ref, dxh_ref):
        @pl.when(pl.program_id(1) == 0)
        def _():
            dxh_ref[...] = (FFN_RES * dx_ref[...]).astype(BF16)

        dhid = _dot_nt(dxh_ref[...], w_ref[...])
        av, bv = a_ref[...].astype(F32), b_ref[...].astype(F32)
        sig = _sigmoid(av)
        silu = av * sig
        da_ref[...] = (dhid * bv * (sig * (1.0 + av * (1.0 - sig)))).astype(BF16)
        db_ref[...] = (dhid * silu).astype(BF16)
        hid_ref[...] = (silu * bv).astype(BF16)

    tile = pl.BlockSpec((tm, tn), lambda i, j: (i, j))
    return pl.pallas_call(
        body, name=name, grid=(n // tm, f // tn),
        in_specs=[pl.BlockSpec((tm, d), lambda i, j: (i, 0)), tile, tile, pl.BlockSpec((tn, d), lambda i, j: (j, 0))],
        out_specs=[tile, tile, tile, pl.BlockSpec((tm, d), lambda i, j: (i, 0))],
        out_shape=[_sds((n, f), BF16)] * 3 + [_sds((n, d), BF16)],
        compiler_params=_cparams("parallel", "arbitrary"),
    )(dxo, a, b, w2)


def _dx_rms_bwd(name, pairs, dxo, x, g):
    n, dm = x.shape
    tm = _pick(n, TILE["row"], 16)
    npair = len(pairs)

    def body(*refs):
        d_refs, w_refs = refs[:npair], refs[npair:2 * npair]
        dxo_ref, x_ref, g_ref, dx_ref, dg_ref = refs[2 * npair:]

        @pl.when(pl.program_id(0) == 0)
        def _():
            dg_ref[...] = jnp.zeros_like(dg_ref)

        dh = None
        for d_ref, w_ref in zip(d_refs, w_refs):
            t = _dot(d_ref[...].astype(BF16), w_ref[...])
            dh = t if dh is None else dh + t
        dx, dg = _rms_bwd(x_ref[...], g_ref[...], dh)
        dx_ref[...] = dxo_ref[...] + dx
        dg_ref[...] += dg

    row = pl.BlockSpec((tm, dm), lambda i: (i, 0))
    d_specs = [pl.BlockSpec((tm, p[1]), functools.partial(lambda i, cb: (i, cb), cb=p[2])) for p in pairs]
    w_specs = [pl.BlockSpec((p[4], dm), functools.partial(lambda i, rb: (rb, 0), rb=p[5])) for p in pairs]
    return pl.pallas_call(
        body, name=name, grid=(n // tm,),
        in_specs=d_specs + w_specs + [row, row, pl.BlockSpec((1, dm), lambda i: (0, 0))],
        out_specs=[row, pl.BlockSpec((1, dm), lambda i: (0, 0))],
        out_shape=[_sds((n, dm), F32), _sds((1, dm), F32)],
        compiler_params=_cparams("arbitrary"),
    )(*[p[0] for p in pairs], *[p[3] for p in pairs], dxo, x, g)


def _mm_tn(name, a, b, a_cols=None, b_cols=None):
    n = a.shape[0]
    a0, ma = a_cols if a_cols else (0, a.shape[1])
    b0, mb = b_cols if b_cols else (0, b.shape[1])
    assert a0 % ma == 0 and b0 % mb == 0
    ab, bb = a0 // ma, b0 // mb
    tk = _pick(n, TILE["mm_bytes"] // (ma * a.dtype.itemsize + mb * b.dtype.itemsize), 16)

    def body(a_ref, b_ref, o_ref):
        @pl.when(pl.program_id(0) == 0)
        def _():
            o_ref[...] = jnp.zeros_like(o_ref)

        o_ref[...] += _dot_tn(a_ref[...].astype(BF16), b_ref[...].astype(BF16))

    return pl.pallas_call(
        body, name=name, grid=(n // tk,),
        in_specs=[pl.BlockSpec((tk, ma), lambda k: (k, ab)), pl.BlockSpec((tk, mb), lambda k: (k, bb))],
        out_specs=pl.BlockSpec((ma, mb), lambda k: (0, 0)),
        out_shape=_sds((ma, mb), F32),
        compiler_params=_cparams("arbitrary"),
    )(a, b)


def _row_mm(name, pairs, out_w, out_dtype, add=None):
    n = pairs[0][0].shape[0]
    tm = _pick(n, TILE["row"], 16)
    npair = len(pairs)

    def body(*refs):
        a_refs, w_refs = refs[:npair], refs[npair:2 * npair]
        add_ref = refs[2 * npair] if add is not None else None
        o_ref = refs[-1]
        acc = None
        for a_ref, w_ref, p in zip(a_refs, w_refs, pairs):
            av = a_ref[...].astype(BF16)
            t = _dot_nt(av, w_ref[...]) if p[6] else _dot(av, w_ref[...])
            acc = t if acc is None else acc + t
        if add_ref is not None:
            acc = acc + add_ref[...].astype(F32)
        o_ref[...] = acc.astype(o_ref.dtype)

    a_specs = [pl.BlockSpec((tm, p[1]), functools.partial(lambda i, cb: (i, cb), cb=p[2])) for p in pairs]
    w_specs = [pl.BlockSpec((p[4], p[3].shape[1]), functools.partial(lambda i, rb: (rb, 0), rb=p[5])) for p in pairs]
    add_specs = [pl.BlockSpec((tm, out_w), lambda i: (i, 0))] if add is not None else []
    return pl.pallas_call(
        body, name=name, grid=(n // tm,),
        in_specs=a_specs + w_specs + add_specs,
        out_specs=pl.BlockSpec((tm, out_w), lambda i: (i, 0)),
        out_shape=_sds((n, out_w), out_dtype),
        compiler_params=_cparams("parallel"),
    )(*[p[0] for p in pairs], *[p[3] for p in pairs], *([add] if add is not None else []))


def _conv_post(c, ln_g, ln_b, out_g):
    mu = jnp.mean(c, axis=-1, keepdims=True)
    xc = c - mu
    rstd = lax.rsqrt(jnp.mean(xc * xc, axis=-1, keepdims=True) + EPS)
    nrm = xc * rstd
    l = nrm * ln_g + ln_b
    sig = _sigmoid(l)
    s = l * sig
    r, sh = _rms_stats(s)
    return sh * out_g, (rstd, nrm, l, sig, r, sh)


def _conv_taps(a_ref, w_ref, first, rows):
    acc = None
    for k in range(CONV_WIDTH):
        t = w_ref[k:k + 1, :] * a_ref[pl.ds(first + k, rows), :]
        acc = t if acc is None else acc + t
    return acc


def _conv_post_bwd(cv, dout, ln_g, ln_b, out_g):
    _, (rstd, nrm, l, sig, r, sh) = _conv_post(cv, ln_g, ln_b, out_g)
    dsh = dout * out_g
    ds = r * (dsh - sh * jnp.mean(dsh * sh, axis=-1, keepdims=True))
    dl = ds * (sig * (1.0 + l * (1.0 - sig)))
    dn = dl * ln_g
    dc = rstd * (dn - jnp.mean(dn, axis=-1, keepdims=True) - nrm * jnp.mean(dn * nrm, axis=-1, keepdims=True))
    col_sum = lambda t: jnp.sum(t, axis=0, keepdims=True)
    return dc, col_sum(dout * sh), col_sum(dl * nrm), col_sum(dl)


def _conv_fwd(name, proj3, conv_w, conv_b, ln_g, ln_b, out_g):
    bsz, seq, _ = proj3.shape
    c = conv_w.shape[1]
    tt = _pick(seq, TILE["conv_t"], CONV_HALO)
    hb = tt // CONV_HALO
    first = CONV_HALO - (CONV_WIDTH - 1)

    def body(v_ref, g_ref, vp_ref, gp_ref, w_ref, cb_ref, lg_ref, lb_ref, og_ref, o_ref, cv_ref, a_ref):
        keep = (pl.program_id(1) > 0).astype(F32)
        a_ref[pl.ds(0, CONV_HALO), :] = keep * vp_ref[0] * _sigmoid(gp_ref[0])
        a_ref[pl.ds(CONV_HALO, tt), :] = v_ref[0] * _sigmoid(g_ref[0])
        cv = _conv_taps(a_ref, w_ref, first, tt) + cb_ref[...]
        cv_ref[0] = cv
        out, _ = _conv_post(cv, lg_ref[...], lb_ref[...], og_ref[...])
        o_ref[0] = out.astype(BF16)

    vec = pl.BlockSpec((1, c), lambda b, i: (0, 0))
    prev = lambda col: pl.BlockSpec((1, CONV_HALO, c), lambda b, i: (b, jnp.maximum(i * hb - 1, 0), col))
    tile = pl.BlockSpec((1, tt, c), lambda b, i: (b, i, 0))
    return pl.pallas_call(
        body, name=name, grid=(bsz, seq // tt),
        in_specs=[tile, pl.BlockSpec((1, tt, c), lambda b, i: (b, i, 1)),
                  prev(0), prev(1), pl.BlockSpec(conv_w.shape, lambda b, i: (0, 0)), vec, vec, vec, vec],
        out_specs=[tile, tile],
        out_shape=[_sds((bsz, seq, c), BF16), _sds((bsz, seq, c), F32)],
        scratch_shapes=[pltpu.VMEM((CONV_HALO + tt, c), F32)],
        compiler_params=_cparams("parallel", "arbitrary"),
    )(proj3, proj3, proj3, proj3, conv_w, conv_b, ln_g, ln_b, out_g)


def _conv_bwd(name, dmix3, proj3, cv3, conv_w, ln_g, ln_b, out_g):
    bsz, seq, _ = proj3.shape
    c = conv_w.shape[1]
    tt = _pick(seq, TILE["conv_t"], CONV_HALO)
    hb = tt // CONV_HALO
    nt = seq // tt
    last_hb = seq // CONV_HALO - 1
    ext = tt + CONV_HALO
    first = CONV_HALO - (CONV_WIDTH - 1)

    def body(v_ref, g_ref, vp_ref, gp_ref, cv_ref, cvn_ref, d_ref, dn_ref, w_ref, lg_ref, lb_ref, og_ref,
             o_ref, dw_ref, dcb_ref, dlg_ref, dlb_ref, dog_ref, a_ref, dc_ref):
        i = pl.program_id(1)

        @pl.when((pl.program_id(0) == 0) & (i == 0))
        def _():
            for r in (dw_ref, dcb_ref, dlg_ref, dlb_ref, dog_ref):
                r[...] = jnp.zeros_like(r)

        keep_prev = (i > 0).astype(F32)
        keep_next = (i < nt - 1).astype(F32)
        sig_g = _sigmoid(g_ref[0])
        a_ref[pl.ds(0, CONV_HALO), :] = keep_prev * vp_ref[0] * _sigmoid(gp_ref[0])
        a_ref[pl.ds(CONV_HALO, tt), :] = v_ref[0] * sig_g

        lg, lb, og = lg_ref[...], lb_ref[...], og_ref[...]
        dc_own, d_og, d_lg, d_lb = _conv_post_bwd(cv_ref[0], d_ref[0], lg, lb, og)
        dc_next, _, _, _ = _conv_post_bwd(cvn_ref[0], keep_next * dn_ref[0], lg, lb, og)
        dog_ref[...] += d_og
        dlg_ref[...] += d_lg
        dlb_ref[...] += d_lb
        dcb_ref[...] += jnp.sum(dc_own, axis=0, keepdims=True)
        dc_ref[pl.ds(0, tt), :] = dc_own
        dc_ref[pl.ds(tt, CONV_HALO), :] = dc_next

        da = None
        for k in range(CONV_WIDTH):
            t = w_ref[k:k + 1, :] * dc_ref[pl.ds(CONV_WIDTH - 1 - k, tt), :]
            da = t if da is None else da + t
            dw_ref[k:k + 1, :] += jnp.sum(dc_own * a_ref[pl.ds(first + k, tt), :], axis=0, keepdims=True)
        val = v_ref[0]
        o_ref[0] = jnp.concatenate([da * sig_g, da * val * sig_g * (1.0 - sig_g)], axis=-1).astype(BF16)

    vec = pl.BlockSpec((1, c), lambda b, i: (0, 0))
    cur = lambda col: pl.BlockSpec((1, tt, c), lambda b, i: (b, i, col))
    prev = lambda col: pl.BlockSpec((1, CONV_HALO, c), lambda b, i: (b, jnp.maximum(i * hb - 1, 0), col))
    nxt = lambda col: pl.BlockSpec((1, CONV_HALO, c), lambda b, i: (b, jnp.minimum((i + 1) * hb, last_hb), col))
    wspec = pl.BlockSpec(conv_w.shape, lambda b, i: (0, 0))
    return pl.pallas_call(
        body, name=name, grid=(bsz, nt),
        in_specs=[cur(0), cur(1), prev(0), prev(1), cur(0), nxt(0), cur(0), nxt(0), wspec, vec, vec, vec],
        out_specs=[pl.BlockSpec((1, tt, 2 * c), lambda b, i: (b, i, 0)), wspec, vec, vec, vec, vec],
        out_shape=[_sds((bsz, seq, 2 * c), BF16), _sds(conv_w.shape, F32)] + [_sds((1, c), F32)] * 4,
        scratch_shapes=[pltpu.VMEM((CONV_HALO + tt, c), F32), pltpu.VMEM((ext, c), F32)],
        compiler_params=_cparams("arbitrary", "arbitrary"),
    )(proj3, proj3, proj3, proj3, cv3, cv3, dmix3, dmix3, conv_w, ln_g, ln_b, out_g)


def _ssm_discretise(a_re, a_im, log_dt):
    dt = jnp.exp(log_dt)
    zr, zi = a_re * dt, a_im * dt
    mag = jnp.exp(zr)
    ar, ai = mag * jnp.cos(zi), mag * jnp.sin(zi)
    den = a_re * a_re + a_im * a_im
    nr = ar - 1.0
    return ar, ai, (nr * a_re + ai * a_im) / den, (ai * a_re - nr * a_im) / den


def _ssm_system(a_re, a_im, log_dt, a_re_x, a_im_x, log_dt_x, bt_re, bt_im):
    ar, ai, _, _ = _ssm_discretise(a_re, a_im, log_dt)
    _, _, cr, ci = _ssm_discretise(a_re_x, a_im_x, log_dt_x)
    return ar, ai, cr * bt_re - ci * bt_im, cr * bt_im + ci * bt_re


def _ssm_prep(name, prim):
    g, p = prim[0].shape

    def body(*refs):
        pwr_ref, pwi_ref, bbr_ref, bbi_ref = refs[8:]
        ar, ai, bbr, bbi = _ssm_system(*[r[...] for r in refs[:8]])
        bbr_ref[...] = bbr
        bbi_ref[...] = bbi
        pr, pi = ar, ai
        for k in range(SUBLANES):
            pwr_ref[k] = pr
            pwi_ref[k] = pi
            pr, pi = pr * ar - pi * ai, pr * ai + pi * ar

    return pl.pallas_call(
        body, name=name,
        out_shape=[_sds((SUBLANES, g, p), F32)] * 2 + [_sds(prim[6].shape, F32)] * 2,
        compiler_params=pltpu.CompilerParams(vmem_limit_bytes=VMEM_LIMIT),
    )(*prim)


def _ssm_param_grads(name, prim, dab_r, dab_i, dbb_r, dbb_i):
    g, p = prim[0].shape
    h = prim[6].shape[0] // g

    def body(*refs):
        dar_ref, dai_ref, dbr_ref, dbi_ref = refs[8:12]
        o_ar, o_ai, o_dt, o_br, o_bi = refs[12:]
        _, vjp = jax.vjp(_ssm_system, *[r[...] for r in refs[:8]])
        ct = (jnp.sum(dar_ref[...], axis=0), jnp.sum(dai_ref[...], axis=0), dbr_ref[...], dbi_ref[...])
        d_ar, d_ai, d_dt, d_arx, d_aix, d_dtx, d_br, d_bi = vjp(ct)
        per_group = lambda t: jnp.sum(t.reshape(g, h, p), axis=1)
        o_ar[...] = d_ar + per_group(d_arx)
        o_ai[...] = d_ai + per_group(d_aix)
        o_dt[...] = d_dt + jnp.sum(per_group(d_dtx), axis=1, keepdims=True)
        o_br[...] = d_br
        o_bi[...] = d_bi

    return pl.pallas_call(
        body, name=name,
        out_shape=[_sds(prim[k].shape, F32) for k in (0, 1, 2, 6, 7)],
        compiler_params=pltpu.CompilerParams(vmem_limit_bytes=VMEM_LIMIT),
    )(*prim, dab_r, dab_i, dbb_r, dbb_i)


def _cfma(xr, xi, cr, ci, sr, si):
    return xr + (cr * sr - ci * si), xi + (cr * si + ci * sr)


def _scan_tables(pw_r, pw_i, reverse):
    gp = pw_r.shape[1] * pw_r.shape[2]
    pr, pi = pw_r.reshape(SUBLANES, gp), pw_i.reshape(SUBLANES, gp)
    if reverse:
        pi = -pi
    row = jnp.arange(SUBLANES)[:, None]
    tabs = []
    for d in (1, 2, 4):
        keep = (row < SUBLANES - d) if reverse else (row >= d)
        tabs += [jnp.where(keep, pr[d - 1][None, :], 0.0), jnp.where(keep, pi[d - 1][None, :], 0.0)]
    tabs += [pr[::-1], pi[::-1]] if reverse else [pr, pi]
    return jnp.concatenate(tabs, axis=0)


def _scan_fwd(name, tab, proj3, u_block, bbd, cdt):
    bsz, seq, _ = proj3.shape
    c, w = bbd.shape
    gp = w // 2
    tt = _pick(seq, TILE["scan_t"], 16)
    nblk = tt // SUBLANES
    cw = _pick(gp, TILE["scan_w"], LANES)

    def body(tab_ref, u_ref, bbd_ref, cdt_ref, xs_ref, y_ref, carry_ref, bu_ref):
        @pl.when(pl.program_id(1) == 0)
        def _():
            carry_ref[...] = jnp.zeros_like(carry_ref)

        bu_ref[0] = _dot(u_ref[0].astype(BF16), bbd_ref[...])

        for ch in range(gp // cw):
            re, im = pl.ds(ch * cw, cw), pl.ds(gp + ch * cw, cw)

            def blk(r, carry, re=re, im=im):
                tabs = [tab_ref[pl.ds(SUBLANES * k, SUBLANES), re] for k in range(8)]
                rows = pl.ds(pl.multiple_of(r * SUBLANES, SUBLANES), SUBLANES)
                xr, xi = bu_ref[0, rows, re], bu_ref[0, rows, im]
                for j, d in enumerate((1, 2, 4)):
                    xr, xi = _cfma(xr, xi, tabs[2 * j], tabs[2 * j + 1], pltpu.roll(xr, d, 0), pltpu.roll(xi, d, 0))
                xr, xi = _cfma(xr, xi, tabs[6], tabs[7], carry[0], carry[1])
                xs_ref[0, rows, re] = xr
                xs_ref[0, rows, im] = xi
                last = SUBLANES - 1
                return (jnp.broadcast_to(xr[last:, :], xr.shape), jnp.broadcast_to(xi[last:, :], xi.shape))

            cr, ci = lax.fori_loop(0, nblk, blk, (carry_ref[:, re], carry_ref[:, im]))
            carry_ref[:, re] = cr
            carry_ref[:, im] = ci

        y_ref[0] = _dot_nt(xs_ref[0].astype(BF16), cdt_ref[...])

    whole = lambda arr: pl.BlockSpec(arr.shape, lambda b, t: (0, 0))
    return pl.pallas_call(
        body, name=name, grid=(bsz, seq // tt),
        in_specs=[whole(tab), pl.BlockSpec((1, tt, c), lambda b, t: (b, t, u_block)), whole(bbd), whole(cdt)],
        out_specs=[pl.BlockSpec((1, tt, w), lambda b, t: (b, t, 0)), pl.BlockSpec((1, tt, c), lambda b, t: (b, t, 0))],
        out_shape=[_sds((bsz, seq, w), F32), _sds((bsz, seq, c), F32)],
        scratch_shapes=[pltpu.VMEM((SUBLANES, w), F32), pltpu.VMEM((1, tt, w), F32)],
        compiler_params=_cparams("arbitrary", "arbitrary"),
    )(tab, proj3, bbd, cdt)


def _scan_bwd(name, tab, dy3, xs3, du_skip3, bbd, cdt):
    bsz, seq, w = xs3.shape
    c = bbd.shape[0]
    gp = w // 2
    tt = _pick(seq, TILE["scan_t"], 16)
    nblk = tt // SUBLANES
    cw = _pick(gp, TILE["scan_w"], LANES)
    nt = seq // tt

    def body(tab_ref, dy_ref, xs_ref, halo_ref, skip_ref, bbd_ref, cdt_ref, lam_ref, du_ref, dar_ref, dai_ref,
             carry_ref, g_ref):
        t = pl.program_id(1)

        @pl.when(t == 0)
        def _():
            carry_ref[...] = jnp.zeros_like(carry_ref)

        @pl.when((pl.program_id(0) == 0) & (t == 0))
        def _():
            dar_ref[...] = jnp.zeros_like(dar_ref)
            dai_ref[...] = jnp.zeros_like(dai_ref)

        g_ref[0] = _dot(dy_ref[0], cdt_ref[...])

        has_prev = (t < nt - 1).astype(F32)
        row0 = lax.broadcasted_iota(jnp.int32, (SUBLANES, cw), 0) == 0
        last = SUBLANES - 1

        for ch in range(gp // cw):
            re, im = pl.ds(ch * cw, cw), pl.ds(gp + ch * cw, cw)

            def step(rows, xm1r, xm1i, state, re=re, im=im):
                tabs = [tab_ref[pl.ds(SUBLANES * k, SUBLANES), re] for k in range(8)]
                cr, ci, accr, acci = state
                lr, li = g_ref[0, rows, re], g_ref[0, rows, im]
                for j, d in enumerate((1, 2, 4)):
                    lr, li = _cfma(lr, li, tabs[2 * j], tabs[2 * j + 1],
                                   pltpu.roll(lr, SUBLANES - d, 0), pltpu.roll(li, SUBLANES - d, 0))
                lr, li = _cfma(lr, li, tabs[6], tabs[7], cr, ci)
                lam_ref[0, rows, re] = lr
                lam_ref[0, rows, im] = li
                xr, xi = xs_ref[0, rows, re], xs_ref[0, rows, im]
                xpr = jnp.where(row0, jnp.broadcast_to(xm1r[last:, :], xr.shape), pltpu.roll(xr, 1, 0))
                xpi = jnp.where(row0, jnp.broadcast_to(xm1i[last:, :], xi.shape), pltpu.roll(xi, 1, 0))
                accr = accr + (lr * xpr + li * xpi)
                acci = acci + (li * xpr - lr * xpi)
                return (jnp.broadcast_to(lr[:1, :], lr.shape), jnp.broadcast_to(li[:1, :], li.shape), accr, acci)

            def blk(k, state, re=re, im=im, step=step):
                r = nblk - 1 - k
                rows = pl.ds(pl.multiple_of(r * SUBLANES, SUBLANES), SUBLANES)
                prev = pl.ds(pl.multiple_of((r - 1) * SUBLANES, SUBLANES), SUBLANES)
                return step(rows, xs_ref[0, prev, re], xs_ref[0, prev, im], state)

            zero = jnp.zeros((SUBLANES, cw), F32)
            state = lax.fori_loop(0, nblk - 1, blk, (carry_ref[:, re], carry_ref[:, im], zero, zero))
            cr, ci, accr, acci = step(pl.ds(0, SUBLANES), has_prev * halo_ref[0, :, re], has_prev * halo_ref[0, :, im], state)
            carry_ref[:, re] = cr
            carry_ref[:, im] = ci
            dar_ref[:, re] += accr
            dai_ref[:, re] += acci

        du_ref[0] = (_dot_nt(lam_ref[0].astype(BF16), bbd_ref[...]) + skip_ref[0]).astype(BF16)

    tile = pl.BlockSpec((1, tt, w), lambda b, t: (b, nt - 1 - t, 0))
    thin = pl.BlockSpec((1, tt, c), lambda b, t: (b, nt - 1 - t, 0))
    halo = pl.BlockSpec((1, SUBLANES, w), lambda b, t: (b, jnp.maximum((nt - 1 - t) * nblk - 1, 0), 0))
    acc = pl.BlockSpec((SUBLANES, gp), lambda b, t: (0, 0))
    whole = lambda arr: pl.BlockSpec(arr.shape, lambda b, t: (0, 0))
    return pl.pallas_call(
        body, name=name, grid=(bsz, nt),
        in_specs=[whole(tab), thin, tile, halo, thin, whole(bbd), whole(cdt)],
        out_specs=[tile, thin, acc, acc],
        out_shape=[_sds(xs3.shape, F32), _sds((bsz, seq, c), BF16), _sds((SUBLANES, gp), F32), _sds((SUBLANES, gp), F32)],
        scratch_shapes=[pltpu.VMEM((SUBLANES, w), F32), pltpu.VMEM((1, tt, w), F32)],
        compiler_params=_cparams("arbitrary", "arbitrary"),
    )(tab, dy3, xs3, xs3, du_skip3, bbd, cdt)


def _gelu_parts(y):
    inner = _GELU_K * (y + _GELU_C * y * y * y)
    t = jnp.tanh(inner)
    return 0.5 * y * (1.0 + t), t


def _ssm_out_fwd(name, cx, proj, u_block, d_skip, glu_w, glu_b, out_g):
    n, c = cx.shape
    tm = _pick(n, TILE["row"], 16)

    def body(cx_ref, u_ref, d_ref, gw_ref, gb_ref, og_ref, y_ref, o_ref):
        y = cx_ref[...] + d_ref[...] * u_ref[...]
        y_ref[...] = y
        gy, _ = _gelu_parts(y)
        z = _dot(gy.astype(BF16), gw_ref[...]) + gb_ref[...]
        _, sh = _rms_stats(gy * _sigmoid(z))
        o_ref[...] = (sh * og_ref[...]).astype(BF16)

    vec = pl.BlockSpec((1, c), lambda i: (0, 0))
    row = pl.BlockSpec((tm, c), lambda i: (i, 0))
    return pl.pallas_call(
        body, name=name, grid=(n // tm,),
        in_specs=[row, pl.BlockSpec((tm, c), lambda i: (i, u_block)), vec, pl.BlockSpec(glu_w.shape, lambda i: (0, 0)),
                  vec, vec],
        out_specs=[row, row],
        out_shape=[_sds((n, c), F32), _sds((n, c), BF16)],
        compiler_params=_cparams("parallel"),
    )(cx, proj, d_skip, glu_w, glu_b, out_g)


def _ssm_out_bwd(name, dmix, d_block, y, proj, u_block, d_skip, glu_w, glu_b, out_g):
    n, c = y.shape
    tm = _pick(n, TILE["row"], 16)

    def body(d_ref, y_ref, u_ref, dk_ref, gw_ref, gb_ref, og_ref, dy_ref, du_ref, dgw_ref, dgb_ref, dog_ref, dd_ref):
        @pl.when(pl.program_id(0) == 0)
        def _():
            for r in (dgw_ref, dgb_ref, dog_ref, dd_ref):
                r[...] = jnp.zeros_like(r)

        yv = y_ref[...]
        gy, th = _gelu_parts(yv)
        gy16 = gy.astype(BF16)
        sz = _sigmoid(_dot(gy16, gw_ref[...]) + gb_ref[...])
        r, sh = _rms_stats(gy * sz)
        dout = d_ref[...]
        dog_ref[...] += jnp.sum(dout * sh, axis=0, keepdims=True)
        dsh = dout * og_ref[...]
        ds = r * (dsh - sh * jnp.mean(dsh * sh, axis=-1, keepdims=True))
        dz = ds * gy * sz * (1.0 - sz)
        dz16 = dz.astype(BF16)
        dgb_ref[...] += jnp.sum(dz, axis=0, keepdims=True)
        dgw_ref[...] += _dot_tn(gy16, dz16)
        dgy = ds * sz + _dot_nt(dz16, gw_ref[...])
        dgelu = 0.5 * (1.0 + th) + 0.5 * yv * (1.0 - th * th) * (_GELU_K * (1.0 + 3.0 * _GELU_C * yv * yv))
        dy = dgy * dgelu
        dy_ref[...] = dy.astype(BF16)
        du_ref[...] = dy * dk_ref[...]
        dd_ref[...] += jnp.sum(dy * u_ref[...], axis=0, keepdims=True)

    vec = pl.BlockSpec((1, c), lambda i: (0, 0))
    row = pl.BlockSpec((tm, c), lambda i: (i, 0))
    mat = pl.BlockSpec(glu_w.shape, lambda i: (0, 0))
    return pl.pallas_call(
        body, name=name, grid=(n // tm,),
        in_specs=[pl.BlockSpec((tm, c), lambda i: (i, d_block)), row, pl.BlockSpec((tm, c), lambda i: (i, u_block)),
                  vec, mat, vec, vec],
        out_specs=[row, row, mat, vec, vec, vec],
        out_shape=[_sds((n, c), BF16), _sds((n, c), F32), _sds(glu_w.shape, F32)] + [_sds((1, c), F32)] * 3,
        compiler_params=_cparams("arbitrary"),
    )(dmix, y, proj, d_skip, glu_w, glu_b, out_g)


def _mesh_pos():
    return tuple(lax.axis_index(a) for a in MESH_AXES)


def _other_chips(x, y):
    return [(1 - x, y), (x, 1 - y), (1 - x, 1 - y)]


def _remote(src, dst, send_sem, recv_sem, dev):
    return pltpu.make_async_remote_copy(src_ref=src, dst_ref=dst, send_sem=send_sem, recv_sem=recv_sem,
                                        device_id=dev, device_id_type=pl.DeviceIdType.MESH)


def _hbm_call(name, body, operands, out_shapes, scratch):
    hbm = pl.BlockSpec(memory_space=pltpu.HBM)
    return pl.pallas_call(body, name=name, in_specs=[hbm] * len(operands), out_specs=[hbm] * len(out_shapes),
                          out_shape=out_shapes, scratch_shapes=scratch)(*operands)


def _all_gather(name, blocks):
    nop = len(blocks)

    def body(*refs):
        x_refs, o_refs = refs[:nop], refs[nop:2 * nop]
        send_sems, recv_sems, local_sems = refs[2 * nop:]
        x, y, c = _mesh_pos()
        me, sibling = (x, y, c), (x, y, 1 - c)
        chips = _other_chips(x, y)

        def copy(i, k, block_of, to, src=None):
            dst = o_refs[i].at[4 * block_of[0] + 2 * block_of[1] + block_of[2]]
            return _remote(dst if src is None else src, dst, send_sems.at[i, k], recv_sems.at[i, k], to)

        own = [pltpu.make_async_copy(x_refs[i], o_refs[i].at[4 * x + 2 * y + c], local_sems.at[i]) for i in range(nop)]
        for cp in own:
            cp.start()
        first = []
        for i in range(nop):
            first.append(copy(i, 0, me, sibling, src=x_refs[i]))
            first += [copy(i, 1 + j, me, (*chip, c), src=x_refs[i]) for j, chip in enumerate(chips)]
        for cp in first:
            cp.start()
        passed = []
        for i in range(nop):
            for j, chip in enumerate(chips):
                copy(i, 1 + j, (*chip, c), me).wait_recv()
                passed.append(copy(i, 4 + j, (*chip, c), sibling))
                passed[-1].start()
        for i in range(nop):
            copy(i, 0, sibling, me).wait_recv()
            for j, chip in enumerate(chips):
                copy(i, 4 + j, (*chip, 1 - c), me).wait_recv()
        for cp in first + passed:
            cp.wait_send()
        for cp in own:
            cp.wait()

    return _hbm_call(name, body, blocks, [_sds((N_DEV,) + b.shape, b.dtype) for b in blocks],
                     [pltpu.SemaphoreType.DMA((nop, N_DEV - 1)), pltpu.SemaphoreType.DMA((nop, N_DEV - 1)),
                      pltpu.SemaphoreType.DMA((nop,))])


def _exchange_sibling(name, grads):
    nop = len(grads)

    def body(*refs):
        x_refs, o_refs = refs[:nop], refs[nop:2 * nop]
        send_sems, recv_sems = refs[2 * nop:]
        x, y, c = _mesh_pos()
        copies = [_remote(x_refs[i].at[2 * q + (1 - c)], o_refs[i].at[q], send_sems.at[i, q], recv_sems.at[i, q],
                          (x, y, 1 - c)) for i in range(nop) for q in range(N_DEV // 2)]
        for cp in copies:
            cp.start()
        for cp in copies:
            cp.wait_recv()
        for cp in copies:
            cp.wait_send()

    return _hbm_call(name, body, grads, [_sds((N_DEV // 2,) + g.shape[1:], g.dtype) for g in grads],
                     [pltpu.SemaphoreType.DMA((nop, N_DEV // 2)), pltpu.SemaphoreType.DMA((nop, N_DEV // 2))])


def _pair_sum(name, grad, other):
    nchip, _, r, c = grad.shape
    tr = _pick(r, max(SUBLANES, TILE["sum_bytes"] // (8 * c)), SUBLANES)

    def body(g_ref, o_ref, s_ref):
        mine = jnp.where(lax.axis_index("c") == 0, g_ref[0, 0], g_ref[0, 1])
        s_ref[0] = (mine + o_ref[0]).astype(s_ref.dtype)

    return pl.pallas_call(
        body, name=name, grid=(nchip, r // tr),
        in_specs=[pl.BlockSpec((1, 2, tr, c), lambda q, t: (q, 0, t, 0)), pl.BlockSpec((1, tr, c), lambda q, t: (q, t, 0))],
        out_specs=pl.BlockSpec((1, tr, c), lambda q, t: (q, t, 0)),
        out_shape=_sds((nchip, r, c), BF16),
        compiler_params=_cparams("parallel", "parallel"),
    )(grad, other)


def _exchange_chips(name, sums):
    nop = len(sums)

    def body(*refs):
        x_refs, o_refs = refs[:nop], refs[nop:2 * nop]
        send_sems, recv_sems, local_sems = refs[2 * nop:]
        x, y, c = _mesh_pos()
        mine = 2 * x + y
        own = [pltpu.make_async_copy(x_refs[i].at[mine], o_refs[i].at[mine], local_sems.at[i]) for i in range(nop)]
        for cp in own:
            cp.start()
        sends, recvs = [], []
        for i in range(nop):
            for j, (px, py) in enumerate(_other_chips(x, y)):
                theirs = 2 * px + py
                sends.append(_remote(x_refs[i].at[theirs], o_refs[i].at[mine], send_sems.at[i, j], recv_sems.at[i, j],
                                     (px, py, c)))
                recvs.append(_remote(x_refs[i].at[mine], o_refs[i].at[theirs], send_sems.at[i, j], recv_sems.at[i, j],
                                     (px, py, c)))
        for cp in sends:
            cp.start()
        for cp in recvs:
            cp.wait_recv()
        for cp in sends:
            cp.wait_send()
        for cp in own:
            cp.wait()

    return _hbm_call(name, body, sums, [_sds(s.shape, s.dtype) for s in sums],
                     [pltpu.SemaphoreType.DMA((nop, 3)), pltpu.SemaphoreType.DMA((nop, 3)), pltpu.SemaphoreType.DMA((nop,))])


def _part_rows(npart, r, c):
    return _pick(r, max(SUBLANES, TILE["sum_bytes"] // (4 * npart * c)), SUBLANES)


def _sum_parts(name, parts):
    npart, r, c = parts.shape
    tr = _part_rows(npart, r, c)

    def body(p_ref, o_ref):
        g = p_ref[0].astype(F32)
        for k in range(1, npart):
            g = g + p_ref[k].astype(F32)
        o_ref[...] = g

    return pl.pallas_call(
        body, name=name, grid=(r // tr,),
        in_specs=[pl.BlockSpec((npart, tr, c), lambda i: (0, i, 0))],
        out_specs=pl.BlockSpec((tr, c), lambda i: (i, 0)),
        out_shape=_sds((r, c), F32),
        compiler_params=_cparams("parallel"),
    )(parts)


def _adamw(name, parts, w, m, v):
    npart, r, c = parts.shape
    tr = _part_rows(npart, r, c)
    c1 = 1.0 - ADAM_B1 ** ADAM_STEP
    c2 = 1.0 - ADAM_B2 ** ADAM_STEP

    def body(p_ref, w_ref, m_ref, v_ref, g_ref, d_ref, nm_ref, nv_ref):
        g = p_ref[0].astype(F32)
        for k in range(1, npart):
            g = g + p_ref[k].astype(F32)
        nm = ADAM_B1 * m_ref[...] + (1.0 - ADAM_B1) * g
        nv = ADAM_B2 * v_ref[...] + (1.0 - ADAM_B2) * (g * g)
        g_ref[...] = g
        nm_ref[...] = nm
        nv_ref[...] = nv
        d_ref[...] = -ADAM_LR * ((nm / c1) / (jnp.sqrt(nv / c2) + ADAM_EPS) + ADAM_WD * w_ref[...])

    row = pl.BlockSpec((tr, c), lambda i: (i, 0))
    return pl.pallas_call(
        body, name=name, grid=(r // tr,),
        in_specs=[pl.BlockSpec((npart, tr, c), lambda i: (0, i, 0)), row, row, row],
        out_specs=[row] * 4,
        out_shape=[_sds((r, c), F32)] * 4,
        compiler_params=_cparams("parallel"),
    )(parts, w, m, v)


def _pack(pieces, row_mult, lead=()):
    nl = len(lead)
    flat, spans, off = [], [], 0
    for p in pieces:
        p = p.reshape(lead + (-1,))
        size = p.shape[-1]
        padded = -(-size // PACK_W) * PACK_W
        flat.append(jnp.pad(p, [(0, 0)] * nl + [(0, padded - size)]))
        spans.append((off, size))
        off += padded
    rows = -(-(off // PACK_W) // row_mult) * row_mult
    if rows * PACK_W > off:
        flat.append(jnp.zeros(lead + (rows * PACK_W - off,), flat[0].dtype))
    return jnp.concatenate(flat, axis=-1).reshape(lead + (rows, PACK_W)), spans


def _unpack(buf, spans, shapes, lead=0):
    flat = buf.reshape(buf.shape[:lead] + (-1,))
    return [flat[..., o:o + s].reshape(buf.shape[:lead] + tuple(shape)) for (o, s), shape in zip(spans, shapes)]


def _block_diag(rows_gh, groups):
    gh, p = rows_gh.shape
    own = (jnp.arange(gh)[:, None] // (gh // groups) == jnp.arange(groups)[None, :]).astype(rows_gh.dtype)
    return (own[:, :, None] * rows_gh[:, None, :]).reshape(gh, groups * p)


def _block_diag_take(dense, groups):
    gh = dense.shape[0]
    p = dense.shape[1] // groups
    own = (jnp.arange(gh)[:, None] // (gh // groups) == jnp.arange(groups)[None, :]).astype(dense.dtype)
    return jnp.sum(dense.reshape(gh, groups, p) * own[:, :, None], axis=1)


BIG = ("ffn1_w1", "ffn1_w3", "ffn1_w2", "w_in", "ssm_glu_w", "w_out", "ffn2_w1", "ffn2_w3", "ffn2_w2")
COL_SHARDED = ("ffn1_w1", "ffn1_w3", "w_in", "ffn2_w1", "ffn2_w3", "conv_w")
SMALL = ("norm_ffn1", "norm_mix", "conv_b", "conv_ln_g", "conv_ln_b", "conv_out_g", "ssm_A_re", "ssm_A_im",
         "ssm_log_dt", "ssm_B_re", "ssm_B_im", "ssm_C_re", "ssm_C_im", "ssm_D", "ssm_glu_b", "ssm_out_g",
         "norm_ffn2", "norm_final")
WEIGHTS = ("norm_ffn1", "ffn1_w1", "ffn1_w3", "ffn1_w2", "norm_mix", "w_in", "conv_w", "conv_b", "conv_ln_g",
           "conv_ln_b", "conv_out_g", "ssm_A_re", "ssm_A_im", "ssm_log_dt", "ssm_B_re", "ssm_B_im", "ssm_C_re",
           "ssm_C_im", "ssm_D", "ssm_glu_w", "ssm_glu_b", "ssm_out_g", "w_out", "norm_ffn2", "ffn2_w1", "ffn2_w3",
           "ffn2_w2", "norm_final")


def _ffn_fwd(tag, x, g, w1, w3, w2):
    (a, b), h = _rms_mm(tag + "_up", x, g, [w1, w3], BF16)
    return _swiglu_down(tag + "_down", x, a, b, w2), (a, b, h)


def _ffn_bwd(tag, dxo, x, g, w1, w3, w2, saved):
    a, b, h = saved
    da, db, hid, dxh = _ffn_bwd_hidden(tag + "_bwd_hidden", dxo, a, b, w2)
    f = a.shape[1]
    dx, dg = _dx_rms_bwd(tag + "_bwd_dx", [(da, f, 0, w1, f, 0), (db, f, 0, w3, f, 0)], dxo, x, g)
    return dx, dg, _mm_tn(tag + "_dw1", da, h), _mm_tn(tag + "_dw3", db, h), _mm_tn(tag + "_dw2", hid, dxh)


def kernel(x, norm_ffn1, ffn1_w1, ffn1_w3, ffn1_w2, norm_mix, w_in, conv_w, conv_b, conv_ln_g, conv_ln_b, conv_out_g, ssm_A_re, ssm_A_im, ssm_log_dt, ssm_B_re, ssm_B_im, ssm_C_re, ssm_C_im, ssm_D, ssm_glu_w, ssm_glu_b, ssm_out_g, w_out, norm_ffn2, ffn2_w1, ffn2_w3, ffn2_w2, norm_final, loss_target, m_norm_ffn1, m_ffn1_w1, m_ffn1_w3, m_ffn1_w2, m_norm_mix, m_w_in, m_conv_w, m_conv_b, m_conv_ln_g, m_conv_ln_b, m_conv_out_g, m_ssm_A_re, m_ssm_A_im, m_ssm_log_dt, m_ssm_B_re, m_ssm_B_im, m_ssm_C_re, m_ssm_C_im, m_ssm_D, m_ssm_glu_w, m_ssm_glu_b, m_ssm_out_g, m_w_out, m_norm_ffn2, m_ffn2_w1, m_ffn2_w3, m_ffn2_w2, m_norm_final, v_norm_ffn1, v_ffn1_w1, v_ffn1_w3, v_ffn1_w2, v_norm_mix, v_w_in, v_conv_w, v_conv_b, v_conv_ln_g, v_conv_ln_b, v_conv_out_g, v_ssm_A_re, v_ssm_A_im, v_ssm_log_dt, v_ssm_B_re, v_ssm_B_im, v_ssm_C_re, v_ssm_C_im, v_ssm_D, v_ssm_glu_w, v_ssm_glu_b, v_ssm_out_g, v_w_out, v_norm_ffn2, v_ffn2_w1, v_ffn2_w3, v_ffn2_w2, v_norm_final):
    args = dict(locals())
    wt = {n: args[n] for n in WEIGHTS}
    mom = {n: args["m_" + n] for n in WEIGHTS}
    var = {n: args["v_" + n] for n in WEIGHTS}

    bsz, seq, d = x.shape
    n = bsz * seq
    c = conv_b.shape[-1]
    groups = c // SSM_GROUP
    gp = groups * SSM_STATE
    u_b = 2

    shards = [(wt[k][0].T if k in COL_SHARDED else wt[k][0]).astype(BF16) for k in BIG] + [wt["conv_w"][0]]
    gathered = _all_gather("gather_weights", shards)
    full = {k: g.reshape(-1, g.shape[-1]) for k, g in zip(BIG, gathered)}
    conv_w_full = gathered[-1].transpose(1, 0, 2).reshape(CONV_WIDTH, c)
    conv_w_pad = jnp.pad(conv_w_full, ((0, CONV_HALO - CONV_WIDTH), (0, 0)))

    vec = lambda k: wt[k].reshape(1, -1)
    g_ffn1, g_mix, g_ffn2, g_fin = vec("norm_ffn1"), vec("norm_mix"), vec("norm_ffn2"), vec("norm_final")
    cb, lng, lnb, cog = vec("conv_b"), vec("conv_ln_g"), vec("conv_ln_b"), vec("conv_out_g")
    d_skip, glu_b, sog = vec("ssm_D"), vec("ssm_glu_b"), vec("ssm_out_g")

    a_re, a_im = wt["ssm_A_re"][0], wt["ssm_A_im"][0]
    log_dt = wt["ssm_log_dt"][0].reshape(groups, 1)
    bt_re = wt["ssm_B_re"][0].transpose(0, 2, 1).reshape(groups * SSM_GROUP, SSM_STATE)
    bt_im = wt["ssm_B_im"][0].transpose(0, 2, 1).reshape(groups * SSM_GROUP, SSM_STATE)
    c_re = wt["ssm_C_re"][0].reshape(groups * SSM_GROUP, SSM_STATE)
    c_im = wt["ssm_C_im"][0].reshape(groups * SSM_GROUP, SSM_STATE)
    per_chan = lambda t: jnp.repeat(t, SSM_GROUP, axis=0)
    ssm_prim = (a_re, a_im, log_dt, per_chan(a_re), per_chan(a_im), per_chan(jnp.broadcast_to(log_dt, a_re.shape)),
                bt_re, bt_im)
    pw_r, pw_i, bb_r, bb_i = _ssm_prep("ssm_prep", ssm_prim)
    tab_f = _scan_tables(pw_r, pw_i, False)
    tab_b = _scan_tables(pw_r, pw_i, True)
    bbd = jnp.concatenate([_block_diag(bb_r, groups), _block_diag(bb_i, groups)], axis=1).astype(BF16)
    cdt = jnp.concatenate([_block_diag(c_re, groups), -_block_diag(c_im, groups)], axis=1).astype(BF16)

    x0 = x.reshape(n, d)
    x1, ffn1_saved = _ffn_fwd("ffn1", x0, g_ffn1, full["ffn1_w1"], full["ffn1_w3"], full["ffn1_w2"])
    (proj,), h2 = _rms_mm("mix_in", x1, g_mix, [full["w_in"]], F32)
    proj3 = proj.reshape(bsz, seq, 3 * c)
    an3, cv3 = _conv_fwd("conv_fwd", proj3, conv_w_pad, cb, lng, lnb, cog)
    an = an3.reshape(n, c)
    xs3, cx3 = _scan_fwd("scan_fwd", tab_f, proj3, u_b, bbd, cdt)
    xs = xs3.reshape(n, 2 * gp)
    y, sn = _ssm_out_fwd("ssm_out_fwd", cx3.reshape(n, c), proj, u_b, d_skip, full["ssm_glu_w"], glu_b, sog)
    w_o = full["w_out"]
    x2 = _row_mm("mix_out", [(an, c, 0, w_o, c, 0, False), (sn, c, 0, w_o, c, 1, False)], d, F32, add=x1)
    x3, ffn2_saved = _ffn_fwd("ffn2", x2, g_ffn2, full["ffn2_w1"], full["ffn2_w3"], full["ffn2_w2"])
    dx3, loss_tile, d_gfin = _loss_head("loss_head", x3, g_fin, loss_target.reshape(n, d))
    loss = lax.psum(loss_tile[0, 0], MESH_AXES)

    grads = {}
    dx2, grads["norm_ffn2"], grads["ffn2_w1"], grads["ffn2_w3"], grads["ffn2_w2"] = _ffn_bwd(
        "ffn2", dx3, x2, g_ffn2, full["ffn2_w1"], full["ffn2_w3"], full["ffn2_w2"], ffn2_saved)

    dmix = _row_mm("mix_out_bwd", [(dx2, d, 0, w_o, 2 * c, 0, True)], 2 * c, F32)
    grads["w_out"] = jnp.concatenate([_mm_tn("dw_out_a", an, dx2), _mm_tn("dw_out_s", sn, dx2)], axis=0)

    dy, du_skip, grads["ssm_glu_w"], grads["ssm_glu_b"], grads["ssm_out_g"], grads["ssm_D"] = _ssm_out_bwd(
        "ssm_out_bwd", dmix, 1, y, proj, u_b, d_skip, full["ssm_glu_w"], glu_b, sog)
    lam3, du3, dab_r, dab_i = _scan_bwd("scan_bwd", tab_b, dy.reshape(bsz, seq, c), xs3,
                                        du_skip.reshape(bsz, seq, c), bbd, cdt)
    lam, du = lam3.reshape(n, 2 * gp), du3.reshape(n, c)
    d_bbd = _mm_tn("ssm_dbb", proj, lam, a_cols=(u_b * c, c))
    d_cdt = _mm_tn("ssm_dc", dy, xs)
    d_are, d_aim, d_ldt, d_btr, d_bti = _ssm_param_grads(
        "ssm_param_grads", ssm_prim,
        dab_r.reshape(SUBLANES, groups, SSM_STATE), dab_i.reshape(SUBLANES, groups, SSM_STATE),
        _block_diag_take(d_bbd[:, :gp], groups), _block_diag_take(d_bbd[:, gp:], groups))
    grads["ssm_A_re"], grads["ssm_A_im"], grads["ssm_log_dt"] = d_are, d_aim, d_ldt
    grads["ssm_B_re"] = d_btr.reshape(groups, SSM_GROUP, SSM_STATE).transpose(0, 2, 1)
    grads["ssm_B_im"] = d_bti.reshape(groups, SSM_GROUP, SSM_STATE).transpose(0, 2, 1)
    grads["ssm_C_re"] = _block_diag_take(d_cdt[:, :gp], groups)
    grads["ssm_C_im"] = -_block_diag_take(d_cdt[:, gp:], groups)

    dconv3, d_cw, grads["conv_b"], grads["conv_ln_g"], grads["conv_ln_b"], grads["conv_out_g"] = _conv_bwd(
        "conv_bwd", dmix.reshape(bsz, seq, 2 * c), proj3, cv3, conv_w_pad, lng, lnb, cog)
    dconv = dconv3.reshape(n, 2 * c)
    grads["conv_w"] = d_cw[:CONV_WIDTH]
    grads["w_in"] = jnp.concatenate([_mm_tn("dw_in_conv", dconv, h2), _mm_tn("dw_in_ssm", du, h2)], axis=0)
    w_i = full["w_in"]
    dx1, grads["norm_mix"] = _dx_rms_bwd("mix_in_bwd", [(dconv, 2 * c, 0, w_i, 2 * c, 0), (du, c, 0, w_i, c, 2)], dx2, x1, g_mix)

    dx0, grads["norm_ffn1"], grads["ffn1_w1"], grads["ffn1_w3"], grads["ffn1_w2"] = _ffn_bwd(
        "ffn1", dx1, x0, g_ffn1, full["ffn1_w1"], full["ffn1_w3"], full["ffn1_w2"], ffn1_saved)
    grads["norm_final"] = d_gfin

    send = [grads[k].reshape((N_DEV, -1) + grads[k].shape[1:]) for k in BIG]
    from_core = _exchange_sibling("exchange_grads_core", send)
    sums = [_pair_sum("pair_sum_" + k, s.reshape((N_DEV // 2, 2) + s.shape[1:]), o)
            for k, s, o in zip(BIG, send, from_core)]
    from_chips = _exchange_chips("exchange_grads_chip", sums)
    res = {}
    for k, parts in zip(BIG, from_chips):
        if k in COL_SHARDED:
            parts = _sum_parts("sum_" + k, parts).T[None]
        res[k] = _adamw("adamw_" + k, parts, wt[k][0], mom[k][0], var[k][0])

    small_names = SMALL + ("conv_w",)
    no_state = jnp.zeros_like(grads["conv_w"])
    part, spans = _pack([grads[k] for k in small_names], SUBLANES)
    (all_parts,) = _all_gather("gather_small_grads", [part])
    w_pk, _ = _pack([wt[k] for k in SMALL] + [no_state], SUBLANES)
    m_pk, _ = _pack([mom[k] for k in SMALL] + [no_state], SUBLANES)
    v_pk, _ = _pack([var[k] for k in SMALL] + [no_state], SUBLANES)
    small_out = _adamw("adamw_replicated", all_parts, w_pk, m_pk, v_pk)
    small_shapes = [wt[k].shape for k in SMALL] + [grads["conv_w"].shape]
    small_res = [dict(zip(small_names, _unpack(o, spans, small_shapes))) for o in small_out]
    x_pos, y_pos, c_pos = (lax.axis_index(a) for a in MESH_AXES)
    cw_cols = c // N_DEV
    own_cw = lax.dynamic_slice_in_dim(small_res[0]["conv_w"], (4 * x_pos + 2 * y_pos + c_pos) * cw_cols, cw_cols, axis=1)
    res["conv_w"] = _adamw("adamw_conv_w", own_cw[None], wt["conv_w"][0], mom["conv_w"][0], var["conv_w"][0])

    outs = [loss, dx0.reshape(bsz, seq, d)]
    for kind in range(4):
        outs += [res[k][kind].reshape(wt[k].shape) if k in res else small_res[kind][k] for k in WEIGHTS]
    return tuple(outs)
```

```python
import functools
import math

import jax
import jax.numpy as jnp
from jax import lax
from jax.experimental import pallas as pl
from jax.experimental.pallas import tpu as pltpu

F32 = jnp.float32
BF16 = jnp.bfloat16

EPS = 1e-6
FFN_RES = 0.5
CONV_WIDTH = 31
CONV_HALO = 32
SSM_GROUP = 16
SSM_STATE = 64
ADAM_LR, ADAM_B1, ADAM_B2, ADAM_EPS, ADAM_WD, ADAM_STEP = 0.001, 0.9, 0.999, 1e-08, 0.01, 10

N_DEV = 8
MESH_AXES = ("x", "y", "c")
SUBLANES = 8
LANES = 128
PACK_W = 1024
V7X_VMEM_BYTES = 64 * 2**20
VMEM_LIMIT = V7X_VMEM_BYTES - 8 * 2**20

TILE = dict(row=256, mm_bytes=8 * 2**20, up_m=1024, up_n=256, conv_t=512, scan_t=256, scan_w=512,
            sum_bytes=4 * 2**20)

_GELU_K = math.sqrt(2.0 / math.pi)
_GELU_C = 0.044715


def _pick(n, target, mult):
    best = None
    for t in range(mult, min(n, target) + 1, mult):
        if n % t == 0:
            best = t
    return n if best is None else best


def _cparams(*sem):
    return pltpu.CompilerParams(dimension_semantics=sem, vmem_limit_bytes=VMEM_LIMIT)


def _sds(shape, dtype):
    return jax.ShapeDtypeStruct(shape, dtype)


def _dot(a, b):
    return jnp.dot(a, b, preferred_element_type=F32)


def _dot_nt(a, b):
    return lax.dot_general(a, b, (((1,), (1,)), ((), ())), preferred_element_type=F32)


def _dot_tn(a, b):
    return lax.dot_general(a, b, (((0,), (0,)), ((), ())), preferred_element_type=F32)


def _sigmoid(x):
    return 0.5 * jnp.tanh(0.5 * x) + 0.5


def _rms_stats(x):
    r = lax.rsqrt(jnp.mean(x * x, axis=-1, keepdims=True) + EPS)
    return r, x * r


def _rms_bwd(x, g, dy):
    r, xh = _rms_stats(x)
    dxh = dy * g
    dx = r * (dxh - xh * jnp.mean(dxh * xh, axis=-1, keepdims=True))
    return dx, jnp.sum(dy * xh, axis=0, keepdims=True)


def _rms_mm(name, x, g, ws, out_dtype):
    n, d = x.shape
    f = ws[0].shape[0]
    nw = len(ws)
    tm, tn = _pick(n, TILE["up_m"], 16), _pick(f, TILE["up_n"], LANES)

    def body(x_ref, g_ref, *refs):
        w_refs, o_refs, h_ref = refs[:nw], refs[nw:2 * nw], refs[2 * nw]

        @pl.when(pl.program_id(1) == 0)
        def _():
            _, xh = _rms_stats(x_ref[...])
            h_ref[...] = (xh * g_ref[...]).astype(BF16)

        h = h_ref[...]
        for w_ref, o_ref in zip(w_refs, o_refs):
            o_ref[...] = _dot_nt(h, w_ref[...]).astype(o_ref.dtype)

    outs = pl.pallas_call(
        body, name=name, grid=(n // tm, f // tn),
        in_specs=[pl.BlockSpec((tm, d), lambda i, j: (i, 0)), pl.BlockSpec((1, d), lambda i, j: (0, 0))]
        + [pl.BlockSpec((tn, d), lambda i, j: (j, 0))] * nw,
        out_specs=[pl.BlockSpec((tm, tn), lambda i, j: (i, j))] * nw + [pl.BlockSpec((tm, d), lambda i, j: (i, 0))],
        out_shape=[_sds((n, f), out_dtype)] * nw + [_sds((n, d), BF16)],
        compiler_params=_cparams("parallel", "arbitrary"),
    )(x, g, *ws)
    return outs[:nw], outs[nw]


def _row_chunks(rows):
    step = _pick(rows, TILE["row"], SUBLANES)
    return [pl.ds(r0, step) for r0 in range(0, rows, step)]


def _ffn_fwd(name, x, g, w1t, w3t, w2):
    n, d = x.shape
    f = w2.shape[0]
    tm, tn = _pick(n, TILE["up_m"], 16), _pick(f, TILE["up_n"], LANES)
    nj = f // tn

    def body(x_ref, g_ref, w1_ref, w3_ref, w2_ref, o_ref, a_ref, b_ref, h_ref, acc_ref):
        j = pl.program_id(1)

        @pl.when(j == 0)
        def _():
            for rows in _row_chunks(tm):
                _, xh = _rms_stats(x_ref[rows, :])
                h_ref[rows, :] = (xh * g_ref[...]).astype(BF16)
            acc_ref[...] = jnp.zeros_like(acc_ref)

        h = h_ref[...]
        av, bv = _dot_nt(h, w1_ref[...]), _dot_nt(h, w3_ref[...])
        a_ref[...] = av.astype(BF16)
        b_ref[...] = bv.astype(BF16)
        acc_ref[...] += _dot((av * _sigmoid(av) * bv).astype(BF16), w2_ref[...])

        @pl.when(j == nj - 1)
        def _():
            o_ref[...] = x_ref[...] + FFN_RES * acc_ref[...]

    row = pl.BlockSpec((tm, d), lambda i, j: (i, 0))
    tile = pl.BlockSpec((tm, tn), lambda i, j: (i, j))
    wblk = pl.BlockSpec((tn, d), lambda i, j: (j, 0))
    return pl.pallas_call(
        body, name=name, grid=(n // tm, nj),
        in_specs=[row, pl.BlockSpec((1, d), lambda i, j: (0, 0)), wblk, wblk, wblk],
        out_specs=[row, tile, tile, row],
        out_shape=[_sds((n, d), F32), _sds((n, f), BF16), _sds((n, f), BF16), _sds((n, d), BF16)],
        scratch_shapes=[pltpu.VMEM((tm, d), F32)],
        compiler_params=_cparams("parallel", "arbitrary"),
    )(x, g, w1t, w3t, w2)


def _ffn_bwd(name, dxo, a, b, x, g, w1t, w3t, w2):
    n, d = dxo.shape
    f = a.shape[1]
    tm, tn = _pick(n, TILE["up_m"], 16), _pick(f, TILE["up_n"], LANES)
    nj = f // tn

    def body(dxo_ref, a_ref, b_ref, x_ref, g_ref, w1_ref, w3_ref, w2_ref,
             dx_ref, dg_ref, da_ref, db_ref, hid_ref, dxh_ref, acc_ref):
        i, j = pl.program_id(0), pl.program_id(1)

        @pl.when((i == 0) & (j == 0))
        def _():
            dg_ref[...] = jnp.zeros_like(dg_ref)

        @pl.when(j == 0)
        def _():
            dxh_ref[...] = (FFN_RES * dxo_ref[...]).astype(BF16)
            acc_ref[...] = jnp.zeros_like(acc_ref)

        dhid = _dot_nt(dxh_ref[...], w2_ref[...])
        av, bv = a_ref[...].astype(F32), b_ref[...].astype(F32)
        sig = _sigmoid(av)
        silu = av * sig
        da = (dhid * bv * (sig * (1.0 + av * (1.0 - sig)))).astype(BF16)
        db = (dhid * silu).astype(BF16)
        da_ref[...] = da
        db_ref[...] = db
        hid_ref[...] = (silu * bv).astype(BF16)
        acc_ref[...] += _dot(da, w1_ref[...]) + _dot(db, w3_ref[...])

        @pl.when(j == nj - 1)
        def _():
            for rows in _row_chunks(tm):
                dx, dg = _rms_bwd(x_ref[rows, :], g_ref[...], acc_ref[rows, :])
                dx_ref[rows, :] = dxo_ref[rows, :] + dx
                dg_ref[...] += dg

    row = pl.BlockSpec((tm, d), lambda i, j: (i, 0))
    tile = pl.BlockSpec((tm, tn), lambda i, j: (i, j))
    wblk = pl.BlockSpec((tn, d), lambda i, j: (j, 0))
    vec = pl.BlockSpec((1, d), lambda i, j: (0, 0))
    return pl.pallas_call(
        body, name=name, grid=(n // tm, nj),
        in_specs=[row, tile, tile, row, vec, wblk, wblk, wblk],
        out_specs=[row, vec, tile, tile, tile, row],
        out_shape=[_sds((n, d), F32), _sds((1, d), F32)] + [_sds((n, f), BF16)] * 3 + [_sds((n, d), BF16)],
        scratch_shapes=[pltpu.VMEM((tm, d), F32)],
        compiler_params=_cparams("arbitrary", "arbitrary"),
    )(dxo, a, b, x, g, w1t, w3t, w2)


def _loss_head(name, x, g, target):
    n, d = x.shape
    tm = _pick(n, TILE["row"], SUBLANES)

    def body(x_ref, g_ref, t_ref, dx_ref, loss_ref, dg_ref):
        @pl.when(pl.program_id(0) == 0)
        def _():
            loss_ref[...] = jnp.zeros_like(loss_ref)
            dg_ref[...] = jnp.zeros_like(dg_ref)

        xv, gv = x_ref[...], g_ref[...]
        r, xh = _rms_stats(xv)
        err = xh * gv - t_ref[...]
        loss_ref[...] += 0.5 * jnp.sum(jnp.mean(err * err, axis=-1, keepdims=True))
        dy = err * (1.0 / d)
        dxh = dy * gv
        dx_ref[...] = r * (dxh - xh * jnp.mean(dxh * xh, axis=-1, keepdims=True))
        dg_ref[...] += jnp.sum(dy * xh, axis=0, keepdims=True)

    return pl.pallas_call(
        body, name=name, grid=(n // tm,),
        in_specs=[pl.BlockSpec((tm, d), lambda i: (i, 0)), pl.BlockSpec((1, d), lambda i: (0, 0)),
                  pl.BlockSpec((tm, d), lambda i: (i, 0))],
        out_specs=[pl.BlockSpec((tm, d), lambda i: (i, 0)), pl.BlockSpec((SUBLANES, LANES), lambda i: (0, 0)),
                   pl.BlockSpec((1, d), lambda i: (0, 0))],
        out_shape=[_sds((n, d), F32), _sds((SUBLANES, LANES), F32), _sds((1, d), F32)],
        compiler_params=_cparams("arbitrary"),
    )(x, g, target)


def _dx_rms_bwd(name, pairs, dxo, x, g):
    n, dm = x.shape
    tm = _pick(n, TILE["row"], 16)
    npair = len(pairs)

    def body(*refs):
        d_refs, w_refs = refs[:npair], refs[npair:2 * npair]
        dxo_ref, x_ref, g_ref, dx_ref, dg_ref = refs[2 * npair:]

        @pl.when(pl.program_id(0) == 0)
        def _():
            dg_ref[...] = jnp.zeros_like(dg_ref)

        dh = None
        for d_ref, w_ref in zip(d_refs, w_refs):
            t = _dot(d_ref[...].astype(BF16), w_ref[...])
            dh = t if dh is None else dh + t
        dx, dg = _rms_bwd(x_ref[...], g_ref[...], dh)
        dx_ref[...] = dxo_ref[...] + dx
        dg_ref[...] += dg

    row = pl.BlockSpec((tm, dm), lambda i: (i, 0))
    d_specs = [pl.BlockSpec((tm, p[1]), functools.partial(lambda i, cb: (i, cb), cb=p[2])) for p in pairs]
    w_specs = [pl.BlockSpec((p[4], dm), functools.partial(lambda i, rb: (rb, 0), rb=p[5])) for p in pairs]
    return pl.pallas_call(
        body, name=name, grid=(n // tm,),
        in_specs=d_specs + w_specs + [row, row, pl.BlockSpec((1, dm), lambda i: (0, 0))],
        out_specs=[row, pl.BlockSpec((1, dm), lambda i: (0, 0))],
        out_shape=[_sds((n, dm), F32), _sds((1, dm), F32)],
        compiler_params=_cparams("arbitrary"),
    )(*[p[0] for p in pairs], *[p[3] for p in pairs], dxo, x, g)


def _mm_tn(name, a, b, a_cols=None, b_cols=None):
    n = a.shape[0]
    a0, ma = a_cols if a_cols else (0, a.shape[1])
    b0, mb = b_cols if b_cols else (0, b.shape[1])
    assert a0 % ma == 0 and b0 % mb == 0
    ab, bb = a0 // ma, b0 // mb
    tk = _pick(n, TILE["mm_bytes"] // (ma * a.dtype.itemsize + mb * b.dtype.itemsize), 16)

    def body(a_ref, b_ref, o_ref):
        @pl.when(pl.program_id(0) == 0)
        def _():
            o_ref[...] = jnp.zeros_like(o_ref)

        o_ref[...] += _dot_tn(a_ref[...].astype(BF16), b_ref[...].astype(BF16))

    return pl.pallas_call(
        body, name=name, grid=(n // tk,),
        in_specs=[pl.BlockSpec((tk, ma), lambda k: (k, ab)), pl.BlockSpec((tk, mb), lambda k: (k, bb))],
        out_specs=pl.BlockSpec((ma, mb), lambda k: (0, 0)),
        out_shape=_sds((ma, mb), F32),
        compiler_params=_cparams("arbitrary"),
    )(a, b)


def _row_mm(name, pairs, out_w, out_dtype, add=None):
    n = pairs[0][0].shape[0]
    tm = _pick(n, TILE["row"], 16)
    npair = len(pairs)

    def body(*refs):
        a_refs, w_refs = refs[:npair], refs[npair:2 * npair]
        add_ref = refs[2 * npair] if add is not None else None
        o_ref = refs[-1]
        acc = None
        for a_ref, w_ref, p in zip(a_refs, w_refs, pairs):
            av = a_ref[...].astype(BF16)
            t = _dot_nt(av, w_ref[...]) if p[6] else _dot(av, w_ref[...])
            acc = t if acc is None else acc + t
        if add_ref is not None:
            acc = acc + add_ref[...].astype(F32)
        o_ref[...] = acc.astype(o_ref.dtype)

    a_specs = [pl.BlockSpec((tm, p[1]), functools.partial(lambda i, cb: (i, cb), cb=p[2])) for p in pairs]
    w_specs = [pl.BlockSpec((p[4], p[3].shape[1]), functools.partial(lambda i, rb: (rb, 0), rb=p[5])) for p in pairs]
    add_specs = [pl.BlockSpec((tm, out_w), lambda i: (i, 0))] if add is not None else []
    return pl.pallas_call(
        body, name=name, grid=(n // tm,),
        in_specs=a_specs + w_specs + add_specs,
        out_specs=pl.BlockSpec((tm, out_w), lambda i: (i, 0)),
        out_shape=_sds((n, out_w), out_dtype),
        compiler_params=_cparams("parallel"),
    )(*[p[0] for p in pairs], *[p[3] for p in pairs], *([add] if add is not None else []))


def _conv_post(c, ln_g, ln_b, out_g):
    mu = jnp.mean(c, axis=-1, keepdims=True)
    xc = c - mu
    rstd = lax.rsqrt(jnp.mean(xc * xc, axis=-1, keepdims=True) + EPS)
    nrm = xc * rstd
    l = nrm * ln_g + ln_b
    sig = _sigmoid(l)
    s = l * sig
    r, sh = _rms_stats(s)
    return sh * out_g, (rstd, nrm, l, sig, r, sh)


def _conv_taps(a_ref, w_ref, first, rows):
    acc = None
    for k in range(CONV_WIDTH):
        t = w_ref[k:k + 1, :] * a_ref[pl.ds(first + k, rows), :]
        acc = t if acc is None else acc + t
    return acc


def _conv_post_bwd(cv, dout, ln_g, ln_b, out_g):
    _, (rstd, nrm, l, sig, r, sh) = _conv_post(cv, ln_g, ln_b, out_g)
    dsh = dout * out_g
    ds = r * (dsh - sh * jnp.mean(dsh * sh, axis=-1, keepdims=True))
    dl = ds * (sig * (1.0 + l * (1.0 - sig)))
    dn = dl * ln_g
    dc = rstd * (dn - jnp.mean(dn, axis=-1, keepdims=True) - nrm * jnp.mean(dn * nrm, axis=-1, keepdims=True))
    col_sum = lambda t: jnp.sum(t, axis=0, keepdims=True)
    return dc, col_sum(dout * sh), col_sum(dl * nrm), col_sum(dl)


def _conv_fwd(name, proj3, conv_w, conv_b, ln_g, ln_b, out_g):
    bsz, seq, _ = proj3.shape
    c = conv_w.shape[1]
    tt = _pick(seq, TILE["conv_t"], CONV_HALO)
    hb = tt // CONV_HALO
    first = CONV_HALO - (CONV_WIDTH - 1)

    def body(v_ref, g_ref, vp_ref, gp_ref, w_ref, cb_ref, lg_ref, lb_ref, og_ref, o_ref, cv_ref, a_ref):
        keep = (pl.program_id(1) > 0).astype(F32)
        a_ref[pl.ds(0, CONV_HALO), :] = keep * vp_ref[0] * _sigmoid(gp_ref[0])
        a_ref[pl.ds(CONV_HALO, tt), :] = v_ref[0] * _sigmoid(g_ref[0])
        cv = _conv_taps(a_ref, w_ref, first, tt) + cb_ref[...]
        cv_ref[0] = cv
        out, _ = _conv_post(cv, lg_ref[...], lb_ref[...], og_ref[...])
        o_ref[0] = out.astype(BF16)

    vec = pl.BlockSpec((1, c), lambda b, i: (0, 0))
    prev = lambda col: pl.BlockSpec((1, CONV_HALO, c), lambda b, i: (b, jnp.maximum(i * hb - 1, 0), col))
    tile = pl.BlockSpec((1, tt, c), lambda b, i: (b, i, 0))
    return pl.pallas_call(
        body, name=name, grid=(bsz, seq // tt),
        in_specs=[tile, pl.BlockSpec((1, tt, c), lambda b, i: (b, i, 1)),
                  prev(0), prev(1), pl.BlockSpec(conv_w.shape, lambda b, i: (0, 0)), vec, vec, vec, vec],
        out_specs=[tile, tile],
        out_shape=[_sds((bsz, seq, c), BF16), _sds((bsz, seq, c), F32)],
        scratch_shapes=[pltpu.VMEM((CONV_HALO + tt, c), F32)],
        compiler_params=_cparams("parallel", "arbitrary"),
    )(proj3, proj3, proj3, proj3, conv_w, conv_b, ln_g, ln_b, out_g)


def _conv_bwd(name, dmix3, proj3, cv3, conv_w, ln_g, ln_b, out_g):
    bsz, seq, _ = proj3.shape
    c = conv_w.shape[1]
    tt = _pick(seq, TILE["conv_t"], CONV_HALO)
    hb = tt // CONV_HALO
    nt = seq // tt
    last_hb = seq // CONV_HALO - 1
    ext = tt + CONV_HALO
    first = CONV_HALO - (CONV_WIDTH - 1)

    def body(v_ref, g_ref, vp_ref, gp_ref, cv_ref, cvn_ref, d_ref, dn_ref, w_ref, lg_ref, lb_ref, og_ref,
             o_ref, dw_ref, dcb_ref, dlg_ref, dlb_ref, dog_ref, a_ref, dc_ref):
        i = pl.program_id(1)

        @pl.when((pl.program_id(0) == 0) & (i == 0))
        def _():
            for r in (dw_ref, dcb_ref, dlg_ref, dlb_ref, dog_ref):
                r[...] = jnp.zeros_like(r)

        keep_prev = (i > 0).astype(F32)
        keep_next = (i < nt - 1).astype(F32)
        sig_g = _sigmoid(g_ref[0])
        a_ref[pl.ds(0, CONV_HALO), :] = keep_prev * vp_ref[0] * _sigmoid(gp_ref[0])
        a_ref[pl.ds(CONV_HALO, tt), :] = v_ref[0] * sig_g

        lg, lb, og = lg_ref[...], lb_ref[...], og_ref[...]
        dc_own, d_og, d_lg, d_lb = _conv_post_bwd(cv_ref[0], d_ref[0], lg, lb, og)
        dc_next, _, _, _ = _conv_post_bwd(cvn_ref[0], keep_next * dn_ref[0], lg, lb, og)
        dog_ref[...] += d_og
        dlg_ref[...] += d_lg
        dlb_ref[...] += d_lb
        dcb_ref[...] += jnp.sum(dc_own, axis=0, keepdims=True)
        dc_ref[pl.ds(0, tt), :] = dc_own
        dc_ref[pl.ds(tt, CONV_HALO), :] = dc_next

        da = None
        for k in range(CONV_WIDTH):
            t = w_ref[k:k + 1, :] * dc_ref[pl.ds(CONV_WIDTH - 1 - k, tt), :]
            da = t if da is None else da + t
            dw_ref[k:k + 1, :] += jnp.sum(dc_own * a_ref[pl.ds(first + k, tt), :], axis=0, keepdims=True)
        val = v_ref[0]
        o_ref[0] = jnp.concatenate([da * sig_g, da * val * sig_g * (1.0 - sig_g)], axis=-1).astype(BF16)

    vec = pl.BlockSpec((1, c), lambda b, i: (0, 0))
    cur = lambda col: pl.BlockSpec((1, tt, c), lambda b, i: (b, i, col))
    prev = lambda col: pl.BlockSpec((1, CONV_HALO, c), lambda b, i: (b, jnp.maximum(i * hb - 1, 0), col))
    nxt = lambda col: pl.BlockSpec((1, CONV_HALO, c), lambda b, i: (b, jnp.minimum((i + 1) * hb, last_hb), col))
    wspec = pl.BlockSpec(conv_w.shape, lambda b, i: (0, 0))
    return pl.pallas_call(
        body, name=name, grid=(bsz, nt),
        in_specs=[cur(0), cur(1), prev(0), prev(1), cur(0), nxt(0), cur(0), nxt(0), wspec, vec, vec, vec],
        out_specs=[pl.BlockSpec((1, tt, 2 * c), lambda b, i: (b, i, 0)), wspec, vec, vec, vec, vec],
        out_shape=[_sds((bsz, seq, 2 * c), BF16), _sds(conv_w.shape, F32)] + [_sds((1, c), F32)] * 4,
        scratch_shapes=[pltpu.VMEM((CONV_HALO + tt, c), F32), pltpu.VMEM((ext, c), F32)],
        compiler_params=_cparams("arbitrary", "arbitrary"),
    )(proj3, proj3, proj3, proj3, cv3, cv3, dmix3, dmix3, conv_w, ln_g, ln_b, out_g)


def _ssm_discretise(a_re, a_im, log_dt):
    dt = jnp.exp(log_dt)
    zr, zi = a_re * dt, a_im * dt
    mag = jnp.exp(zr)
    ar, ai = mag * jnp.cos(zi), mag * jnp.sin(zi)
    den = a_re * a_re + a_im * a_im
    nr = ar - 1.0
    return ar, ai, (nr * a_re + ai * a_im) / den, (ai * a_re - nr * a_im) / den


def _ssm_system(a_re, a_im, log_dt, a_re_x, a_im_x, log_dt_x, bt_re, bt_im):
    ar, ai, _, _ = _ssm_discretise(a_re, a_im, log_dt)
    _, _, cr, ci = _ssm_discretise(a_re_x, a_im_x, log_dt_x)
    return ar, ai, cr * bt_re - ci * bt_im, cr * bt_im + ci * bt_re


def _ssm_prep(name, prim):
    g, p = prim[0].shape

    def body(*refs):
        pwr_ref, pwi_ref, bbr_ref, bbi_ref = refs[8:]
        ar, ai, bbr, bbi = _ssm_system(*[r[...] for r in refs[:8]])
        bbr_ref[...] = bbr
        bbi_ref[...] = bbi
        pr, pi = ar, ai
        for k in range(SUBLANES):
            pwr_ref[k] = pr
            pwi_ref[k] = pi
            pr, pi = pr * ar - pi * ai, pr * ai + pi * ar

    return pl.pallas_call(
        body, name=name,
        out_shape=[_sds((SUBLANES, g, p), F32)] * 2 + [_sds(prim[6].shape, F32)] * 2,
        compiler_params=pltpu.CompilerParams(vmem_limit_bytes=VMEM_LIMIT),
    )(*prim)


def _ssm_param_grads(name, prim, dab_r, dab_i, dbb_r, dbb_i):
    g, p = prim[0].shape
    h = prim[6].shape[0] // g

    def body(*refs):
        dar_ref, dai_ref, dbr_ref, dbi_ref = refs[8:12]
        o_ar, o_ai, o_dt, o_br, o_bi = refs[12:]
        _, vjp = jax.vjp(_ssm_system, *[r[...] for r in refs[:8]])
        ct = (jnp.sum(dar_ref[...], axis=0), jnp.sum(dai_ref[...], axis=0), dbr_ref[...], dbi_ref[...])
        d_ar, d_ai, d_dt, d_arx, d_aix, d_dtx, d_br, d_bi = vjp(ct)
        per_group = lambda t: jnp.sum(t.reshape(g, h, p), axis=1)
        o_ar[...] = d_ar + per_group(d_arx)
        o_ai[...] = d_ai + per_group(d_aix)
        o_dt[...] = d_dt + jnp.sum(per_group(d_dtx), axis=1, keepdims=True)
        o_br[...] = d_br
        o_bi[...] = d_bi

    return pl.pallas_call(
        body, name=name,
        out_shape=[_sds(prim[k].shape, F32) for k in (0, 1, 2, 6, 7)],
        compiler_params=pltpu.CompilerParams(vmem_limit_bytes=VMEM_LIMIT),
    )(*prim, dab_r, dab_i, dbb_r, dbb_i)


def _cfma(xr, xi, cr, ci, sr, si):
    return xr + (cr * sr - ci * si), xi + (cr * si + ci * sr)


def _scan_tables(pw_r, pw_i, reverse):
    gp = pw_r.shape[1] * pw_r.shape[2]
    pr, pi = pw_r.reshape(SUBLANES, gp), pw_i.reshape(SUBLANES, gp)
    if reverse:
        pi = -pi
    row = jnp.arange(SUBLANES)[:, None]
    tabs = []
    for d in (1, 2, 4):
        keep = (row < SUBLANES - d) if reverse else (row >= d)
        tabs += [jnp.where(keep, pr[d - 1][None, :], 0.0), jnp.where(keep, pi[d - 1][None, :], 0.0)]
    tabs += [pr[::-1], pi[::-1]] if reverse else [pr, pi]
    return jnp.concatenate(tabs, axis=0)


def _scan_fwd(name, tab, proj3, u_block, bbd, cdt):
    bsz, seq, _ = proj3.shape
    c, w = bbd.shape
    gp = w // 2
    tt = _pick(seq, TILE["scan_t"], 16)
    nblk = tt // SUBLANES
    cw = _pick(gp, TILE["scan_w"], LANES)

    def body(tab_ref, u_ref, bbd_ref, cdt_ref, xs_ref, y_ref, carry_ref, bu_ref):
        @pl.when(pl.program_id(1) == 0)
        def _():
            carry_ref[...] = jnp.zeros_like(carry_ref)

        bu_ref[0] = _dot(u_ref[0].astype(BF16), bbd_ref[...])

        for ch in range(gp // cw):
            re, im = pl.ds(ch * cw, cw), pl.ds(gp + ch * cw, cw)

            def blk(r, carry, re=re, im=im):
                tabs = [tab_ref[pl.ds(SUBLANES * k, SUBLANES), re] for k in range(8)]
                rows = pl.ds(pl.multiple_of(r * SUBLANES, SUBLANES), SUBLANES)
                xr, xi = bu_ref[0, rows, re], bu_ref[0, rows, im]
                for j, d in enumerate((1, 2, 4)):
                    xr, xi = _cfma(xr, xi, tabs[2 * j], tabs[2 * j + 1], pltpu.roll(xr, d, 0), pltpu.roll(xi, d, 0))
                xr, xi = _cfma(xr, xi, tabs[6], tabs[7], carry[0], carry[1])
                xs_ref[0, rows, re] = xr
                xs_ref[0, rows, im] = xi
                last = SUBLANES - 1
                return (jnp.broadcast_to(xr[last:, :], xr.shape), jnp.broadcast_to(xi[last:, :], xi.shape))

            cr, ci = lax.fori_loop(0, nblk, blk, (carry_ref[:, re], carry_ref[:, im]))
            carry_ref[:, re] = cr
            carry_ref[:, im] = ci

        y_ref[0] = _dot_nt(xs_ref[0].astype(BF16), cdt_ref[...])

    whole = lambda arr: pl.BlockSpec(arr.shape, lambda b, t: (0, 0))
    return pl.pallas_call(
        body, name=name, grid=(bsz, seq // tt),
        in_specs=[whole(tab), pl.BlockSpec((1, tt, c), lambda b, t: (b, t, u_block)), whole(bbd), whole(cdt)],
        out_specs=[pl.BlockSpec((1, tt, w), lambda b, t: (b, t, 0)), pl.BlockSpec((1, tt, c), lambda b, t: (b, t, 0))],
        out_shape=[_sds((bsz, seq, w), F32), _sds((bsz, seq, c), F32)],
        scratch_shapes=[pltpu.VMEM((SUBLANES, w), F32), pltpu.VMEM((1, tt, w), F32)],
        compiler_params=_cparams("arbitrary", "arbitrary"),
    )(tab, proj3, bbd, cdt)


def _scan_bwd(name, tab, dy3, xs3, du_skip3, bbd, cdt):
    bsz, seq, w = xs3.shape
    c = bbd.shape[0]
    gp = w // 2
    tt = _pick(seq, TILE["scan_t"], 16)
    nblk = tt // SUBLANES
    cw = _pick(gp, TILE["scan_w"], LANES)
    nt = seq // tt

    def body(tab_ref, dy_ref, xs_ref, halo_ref, skip_ref, bbd_ref, cdt_ref, lam_ref, du_ref, dar_ref, dai_ref,
             carry_ref, g_ref):
        t = pl.program_id(1)

        @pl.when(t == 0)
        def _():
            carry_ref[...] = jnp.zeros_like(carry_ref)

        @pl.when((pl.program_id(0) == 0) & (t == 0))
        def _():
            dar_ref[...] = jnp.zeros_like(dar_ref)
            dai_ref[...] = jnp.zeros_like(dai_ref)

        g_ref[0] = _dot(dy_ref[0], cdt_ref[...])

        has_prev = (t < nt - 1).astype(F32)
        row0 = lax.broadcasted_iota(jnp.int32, (SUBLANES, cw), 0) == 0
        last = SUBLANES - 1

        for ch in range(gp // cw):
            re, im = pl.ds(ch * cw, cw), pl.ds(gp + ch * cw, cw)

            def step(rows, xm1r, xm1i, state, re=re, im=im):
                tabs = [tab_ref[pl.ds(SUBLANES * k, SUBLANES), re] for k in range(8)]
                cr, ci, accr, acci = state
                lr, li = g_ref[0, rows, re], g_ref[0, rows, im]
                for j, d in enumerate((1, 2, 4)):
                    lr, li = _cfma(lr, li, tabs[2 * j], tabs[2 * j + 1],
                                   pltpu.roll(lr, SUBLANES - d, 0), pltpu.roll(li, SUBLANES - d, 0))
                lr, li = _cfma(lr, li, tabs[6], tabs[7], cr, ci)
                lam_ref[0, rows, re] = lr
                lam_ref[0, rows, im] = li
                xr, xi = xs_ref[0, rows, re], xs_ref[0, rows, im]
                xpr = jnp.where(row0, jnp.broadcast_to(xm1r[last:, :], xr.shape), pltpu.roll(xr, 1, 0))
                xpi = jnp.where(row0, jnp.broadcast_to(xm1i[last:, :], xi.shape), pltpu.roll(xi, 1, 0))
                accr = accr + (lr * xpr + li * xpi)
                acci = acci + (li * xpr - lr * xpi)
                return (jnp.broadcast_to(lr[:1, :], lr.shape), jnp.broadcast_to(li[:1, :], li.shape), accr, acci)

            def blk(k, state, re=re, im=im, step=step):
                r = nblk - 1 - k
                rows = pl.ds(pl.multiple_of(r * SUBLANES, SUBLANES), SUBLANES)
                prev = pl.ds(pl.multiple_of((r - 1) * SUBLANES, SUBLANES), SUBLANES)
                return step(rows, xs_ref[0, prev, re], xs_ref[0, prev, im], state)

            zero = jnp.zeros((SUBLANES, cw), F32)
            state = lax.fori_loop(0, nblk - 1, blk, (carry_ref[:, re], carry_ref[:, im], zero, zero))
            cr, ci, accr, acci = step(pl.ds(0, SUBLANES), has_prev * halo_ref[0, :, re], has_prev * halo_ref[0, :, im], state)
            carry_ref[:, re] = cr
            carry_ref[:, im] = ci
            dar_ref[:, re] += accr
            dai_ref[:, re] += acci

        du_ref[0] = (_dot_nt(lam_ref[0].astype(BF16), bbd_ref[...]) + skip_ref[0]).astype(BF16)

    tile = pl.BlockSpec((1, tt, w), lambda b, t: (b, nt - 1 - t, 0))
    thin = pl.BlockSpec((1, tt, c), lambda b, t: (b, nt - 1 - t, 0))
    halo = pl.BlockSpec((1, SUBLANES, w), lambda b, t: (b, jnp.maximum((nt - 1 - t) * nblk - 1, 0), 0))
    acc = pl.BlockSpec((SUBLANES, gp), lambda b, t: (0, 0))
    whole = lambda arr: pl.BlockSpec(arr.shape, lambda b, t: (0, 0))
    return pl.pallas_call(
        body, name=name, grid=(bsz, nt),
        in_specs=[whole(tab), thin, tile, halo, thin, whole(bbd), whole(cdt)],
        out_specs=[tile, thin, acc, acc],
        out_shape=[_sds(xs3.shape, F32), _sds((bsz, seq, c), BF16), _sds((SUBLANES, gp), F32), _sds((SUBLANES, gp), F32)],
        scratch_shapes=[pltpu.VMEM((SUBLANES, w), F32), pltpu.VMEM((1, tt, w), F32)],
        compiler_params=_cparams("arbitrary", "arbitrary"),
    )(tab, dy3, xs3, xs3, du_skip3, bbd, cdt)


def _gelu_parts(y):
    inner = _GELU_K * (y + _GELU_C * y * y * y)
    t = jnp.tanh(inner)
    return 0.5 * y * (1.0 + t), t


def _ssm_out_fwd(name, cx, proj, u_block, d_skip, glu_w, glu_b, out_g):
    n, c = cx.shape
    tm = _pick(n, TILE["row"], 16)

    def body(cx_ref, u_ref, d_ref, gw_ref, gb_ref, og_ref, y_ref, o_ref):
        y = cx_ref[...] + d_ref[...] * u_ref[...]
        y_ref[...] = y
        gy, _ = _gelu_parts(y)
        z = _dot(gy.astype(BF16), gw_ref[...]) + gb_ref[...]
        _, sh = _rms_stats(gy * _sigmoid(z))
        o_ref[...] = (sh * og_ref[...]).astype(BF16)

    vec = pl.BlockSpec((1, c), lambda i: (0, 0))
    row = pl.BlockSpec((tm, c), lambda i: (i, 0))
    return pl.pallas_call(
        body, name=name, grid=(n // tm,),
        in_specs=[row, pl.BlockSpec((tm, c), lambda i: (i, u_block)), vec, pl.BlockSpec(glu_w.shape, lambda i: (0, 0)),
                  vec, vec],
        out_specs=[row, row],
        out_shape=[_sds((n, c), F32), _sds((n, c), BF16)],
        compiler_params=_cparams("parallel"),
    )(cx, proj, d_skip, glu_w, glu_b, out_g)


def _ssm_out_bwd(name, dmix, d_block, y, proj, u_block, d_skip, glu_w, glu_b, out_g):
    n, c = y.shape
    tm = _pick(n, TILE["row"], 16)

    def body(d_ref, y_ref, u_ref, dk_ref, gw_ref, gb_ref, og_ref, dy_ref, du_ref, dgw_ref, dgb_ref, dog_ref, dd_ref):
        @pl.when(pl.program_id(0) == 0)
        def _():
            for r in (dgw_ref, dgb_ref, dog_ref, dd_ref):
                r[...] = jnp.zeros_like(r)

        yv = y_ref[...]
        gy, th = _gelu_parts(yv)
        gy16 = gy.astype(BF16)
        sz = _sigmoid(_dot(gy16, gw_ref[...]) + gb_ref[...])
        r, sh = _rms_stats(gy * sz)
        dout = d_ref[...]
        dog_ref[...] += jnp.sum(dout * sh, axis=0, keepdims=True)
        dsh = dout * og_ref[...]
        ds = r * (dsh - sh * jnp.mean(dsh * sh, axis=-1, keepdims=True))
        dz = ds * gy * sz * (1.0 - sz)
        dz16 = dz.astype(BF16)
        dgb_ref[...] += jnp.sum(dz, axis=0, keepdims=True)
        dgw_ref[...] += _dot_tn(gy16, dz16)
        dgy = ds * sz + _dot_nt(dz16, gw_ref[...])
        dgelu = 0.5 * (1.0 + th) + 0.5 * yv * (1.0 - th * th) * (_GELU_K * (1.0 + 3.0 * _GELU_C * yv * yv))
        dy = dgy * dgelu
        dy_ref[...] = dy.astype(BF16)
        du_ref[...] = dy * dk_ref[...]
        dd_ref[...] += jnp.sum(dy * u_ref[...], axis=0, keepdims=True)

    vec = pl.BlockSpec((1, c), lambda i: (0, 0))
    row = pl.BlockSpec((tm, c), lambda i: (i, 0))
    mat = pl.BlockSpec(glu_w.shape, lambda i: (0, 0))
    return pl.pallas_call(
        body, name=name, grid=(n // tm,),
        in_specs=[pl.BlockSpec((tm, c), lambda i: (i, d_block)), row, pl.BlockSpec((tm, c), lambda i: (i, u_block)),
                  vec, mat, vec, vec],
        out_specs=[row, row, mat, vec, vec, vec],
        out_shape=[_sds((n, c), BF16), _sds((n, c), F32), _sds(glu_w.shape, F32)] + [_sds((1, c), F32)] * 3,
        compiler_params=_cparams("arbitrary"),
    )(dmix, y, proj, d_skip, glu_w, glu_b, out_g)


def _mesh_pos():
    return tuple(lax.axis_index(a) for a in MESH_AXES)


def _other_chips(x, y):
    return [(1 - x, y), (x, 1 - y), (1 - x, 1 - y)]


def _remote(src, dst, send_sem, recv_sem, dev):
    return pltpu.make_async_remote_copy(src_ref=src, dst_ref=dst, send_sem=send_sem, recv_sem=recv_sem,
                                        device_id=dev, device_id_type=pl.DeviceIdType.MESH)


def _hbm_call(name, body, operands, out_shapes, scratch):
    hbm = pl.BlockSpec(memory_space=pltpu.HBM)
    return pl.pallas_call(body, name=name, in_specs=[hbm] * len(operands), out_specs=[hbm] * len(out_shapes),
                          out_shape=out_shapes, scratch_shapes=scratch)(*operands)


def _all_gather(name, blocks):
    nop = len(blocks)

    def body(*refs):
        x_refs, o_refs = refs[:nop], refs[nop:2 * nop]
        send_sems, recv_sems, local_sems = refs[2 * nop:]
        x, y, c = _mesh_pos()
        me, sibling = (x, y, c), (x, y, 1 - c)
        chips = _other_chips(x, y)

        def copy(i, k, block_of, to, src=None):
            dst = o_refs[i].at[4 * block_of[0] + 2 * block_of[1] + block_of[2]]
            return _remote(dst if src is None else src, dst, send_sems.at[i, k], recv_sems.at[i, k], to)

        own = [pltpu.make_async_copy(x_refs[i], o_refs[i].at[4 * x + 2 * y + c], local_sems.at[i]) for i in range(nop)]
        for cp in own:
            cp.start()
        first = []
        for i in range(nop):
            first.append(copy(i, 0, me, sibling, src=x_refs[i]))
            first += [copy(i, 1 + j, me, (*chip, c), src=x_refs[i]) for j, chip in enumerate(chips)]
        for cp in first:
            cp.start()
        passed = []
        for i in range(nop):
            for j, chip in enumerate(chips):
                copy(i, 1 + j, (*chip, c), me).wait_recv()
                passed.append(copy(i, 4 + j, (*chip, c), sibling))
                passed[-1].start()
        for i in range(nop):
            copy(i, 0, sibling, me).wait_recv()
            for j, chip in enumerate(chips):
                copy(i, 4 + j, (*chip, 1 - c), me).wait_recv()
        for cp in first + passed:
            cp.wait_send()
        for cp in own:
            cp.wait()

    return _hbm_call(name, body, blocks, [_sds((N_DEV,) + b.shape, b.dtype) for b in blocks],
                     [pltpu.SemaphoreType.DMA((nop, N_DEV - 1)), pltpu.SemaphoreType.DMA((nop, N_DEV - 1)),
                      pltpu.SemaphoreType.DMA((nop,))])


def _exchange_sibling(name, grads):
    nop = len(grads)

    def body(*refs):
        x_refs, o_refs = refs[:nop], refs[nop:2 * nop]
        send_sems, recv_sems = refs[2 * nop:]
        x, y, c = _mesh_pos()
        copies = [_remote(x_refs[i].at[2 * q + (1 - c)], o_refs[i].at[q], send_sems.at[i, q], recv_sems.at[i, q],
                          (x, y, 1 - c)) for i in range(nop) for q in range(N_DEV // 2)]
        for cp in copies:
            cp.start()
        for cp in copies:
            cp.wait_recv()
        for cp in copies:
            cp.wait_send()

    return _hbm_call(name, body, grads, [_sds((N_DEV // 2,) + g.shape[1:], g.dtype) for g in grads],
                     [pltpu.SemaphoreType.DMA((nop, N_DEV // 2)), pltpu.SemaphoreType.DMA((nop, N_DEV // 2))])


def _pair_sum(name, grad, other):
    nchip, _, r, c = grad.shape
    tr = _pick(r, max(SUBLANES, TILE["sum_bytes"] // (8 * c)), SUBLANES)

    def body(g_ref, o_ref, s_ref):
        mine = jnp.where(lax.axis_index("c") == 0, g_ref[0, 0], g_ref[0, 1])
        s_ref[0] = (mine + o_ref[0]).astype(s_ref.dtype)

    return pl.pallas_call(
        body, name=name, grid=(nchip, r // tr),
        in_specs=[pl.BlockSpec((1, 2, tr, c), lambda q, t: (q, 0, t, 0)), pl.BlockSpec((1, tr, c), lambda q, t: (q, t, 0))],
        out_specs=pl.BlockSpec((1, tr, c), lambda q, t: (q, t, 0)),
        out_shape=_sds((nchip, r, c), BF16),
        compiler_params=_cparams("parallel", "parallel"),
    )(grad, other)


def _exchange_chips(name, sums):
    nop = len(sums)

    def body(*refs):
        x_refs, o_refs = refs[:nop], refs[nop:2 * nop]
        send_sems, recv_sems, local_sems = refs[2 * nop:]
        x, y, c = _mesh_pos()
        mine = 2 * x + y
        own = [pltpu.make_async_copy(x_refs[i].at[mine], o_refs[i].at[mine], local_sems.at[i]) for i in range(nop)]
        for cp in own:
            cp.start()
        sends, recvs = [], []
        for i in range(nop):
            for j, (px, py) in enumerate(_other_chips(x, y)):
                theirs = 2 * px + py
                sends.append(_remote(x_refs[i].at[theirs], o_refs[i].at[mine], send_sems.at[i, j], recv_sems.at[i, j],
                                     (px, py, c)))
                recvs.append(_remote(x_refs[i].at[mine], o_refs[i].at[theirs], send_sems.at[i, j], recv_sems.at[i, j],
                                     (px, py, c)))
        for cp in sends:
            cp.start()
        for cp in recvs:
            cp.wait_recv()
        for cp in sends:
            cp.wait_send()
        for cp in own:
            cp.wait()

    return _hbm_call(name, body, sums, [_sds(s.shape, s.dtype) for s in sums],
                     [pltpu.SemaphoreType.DMA((nop, 3)), pltpu.SemaphoreType.DMA((nop, 3)), pltpu.SemaphoreType.DMA((nop,))])


def _part_rows(npart, r, c):
    return _pick(r, max(SUBLANES, TILE["sum_bytes"] // (4 * npart * c)), SUBLANES)


def _sum_parts(name, parts):
    npart, r, c = parts.shape
    tr = _part_rows(npart, r, c)

    def body(p_ref, o_ref):
        g = p_ref[0].astype(F32)
        for k in range(1, npart):
            g = g + p_ref[k].astype(F32)
        o_ref[...] = g

    return pl.pallas_call(
        body, name=name, grid=(r // tr,),
        in_specs=[pl.BlockSpec((npart, tr, c), lambda i: (0, i, 0))],
        out_specs=pl.BlockSpec((tr, c), lambda i: (i, 0)),
        out_shape=_sds((r, c), F32),
        compiler_params=_cparams("parallel"),
    )(parts)


def _adamw(name, parts, w, m, v):
    npart, r, c = parts.shape
    lead = len(w.shape) - 2
    tr = _part_rows(npart, r, c)
    c1 = 1.0 - ADAM_B1 ** ADAM_STEP
    c2 = 1.0 - ADAM_B2 ** ADAM_STEP
    at = (0,) * lead + (slice(None), slice(None))

    def body(p_ref, w_ref, m_ref, v_ref, g_ref, d_ref, nm_ref, nv_ref):
        g = p_ref[0].astype(F32)
        for k in range(1, npart):
            g = g + p_ref[k].astype(F32)
        nm = ADAM_B1 * m_ref[at] + (1.0 - ADAM_B1) * g
        nv = ADAM_B2 * v_ref[at] + (1.0 - ADAM_B2) * (g * g)
        g_ref[at] = g
        nm_ref[at] = nm
        nv_ref[at] = nv
        d_ref[at] = -ADAM_LR * ((nm / c1) / (jnp.sqrt(nv / c2) + ADAM_EPS) + ADAM_WD * w_ref[at])

    row = pl.BlockSpec((1,) * lead + (tr, c), lambda i: (0,) * lead + (i, 0))
    return pl.pallas_call(
        body, name=name, grid=(r // tr,),
        in_specs=[pl.BlockSpec((npart, tr, c), lambda i: (0, i, 0)), row, row, row],
        out_specs=[row] * 4,
        out_shape=[_sds(w.shape, F32)] * 4,
        compiler_params=_cparams("parallel"),
    )(parts, w, m, v)


def _pack(pieces, row_mult, lead=()):
    nl = len(lead)
    flat, spans, off = [], [], 0
    for p in pieces:
        p = p.reshape(lead + (-1,))
        size = p.shape[-1]
        padded = -(-size // PACK_W) * PACK_W
        flat.append(jnp.pad(p, [(0, 0)] * nl + [(0, padded - size)]))
        spans.append((off, size))
        off += padded
    rows = -(-(off // PACK_W) // row_mult) * row_mult
    if rows * PACK_W > off:
        flat.append(jnp.zeros(lead + (rows * PACK_W - off,), flat[0].dtype))
    return jnp.concatenate(flat, axis=-1).reshape(lead + (rows, PACK_W)), spans


def _unpack(buf, spans, shapes, lead=0):
    flat = buf.reshape(buf.shape[:lead] + (-1,))
    return [flat[..., o:o + s].reshape(buf.shape[:lead] + tuple(shape)) for (o, s), shape in zip(spans, shapes)]


def _block_diag(rows_gh, groups):
    gh, p = rows_gh.shape
    own = (jnp.arange(gh)[:, None] // (gh // groups) == jnp.arange(groups)[None, :]).astype(rows_gh.dtype)
    return (own[:, :, None] * rows_gh[:, None, :]).reshape(gh, groups * p)


def _block_diag_take(dense, groups):
    gh = dense.shape[0]
    p = dense.shape[1] // groups
    own = (jnp.arange(gh)[:, None] // (gh // groups) == jnp.arange(groups)[None, :]).astype(dense.dtype)
    return jnp.sum(dense.reshape(gh, groups, p) * own[:, :, None], axis=1)


BIG = ("ffn1_w1", "ffn1_w3", "ffn1_w2", "w_in", "ssm_glu_w", "w_out", "ffn2_w1", "ffn2_w3", "ffn2_w2")
COL_SHARDED = ("ffn1_w1", "ffn1_w3", "w_in", "ffn2_w1", "ffn2_w3", "conv_w")
SMALL = ("norm_ffn1", "norm_mix", "conv_b", "conv_ln_g", "conv_ln_b", "conv_out_g", "ssm_A_re", "ssm_A_im",
         "ssm_log_dt", "ssm_B_re", "ssm_B_im", "ssm_C_re", "ssm_C_im", "ssm_D", "ssm_glu_b", "ssm_out_g",
         "norm_ffn2", "norm_final")
WEIGHTS = ("norm_ffn1", "ffn1_w1", "ffn1_w3", "ffn1_w2", "norm_mix", "w_in", "conv_w", "conv_b", "conv_ln_g",
           "conv_ln_b", "conv_out_g", "ssm_A_re", "ssm_A_im", "ssm_log_dt", "ssm_B_re", "ssm_B_im", "ssm_C_re",
           "ssm_C_im", "ssm_D", "ssm_glu_w", "ssm_glu_b", "ssm_out_g", "w_out", "norm_ffn2", "ffn2_w1", "ffn2_w3",
           "ffn2_w2", "norm_final")


def _ffn_forward(tag, x, g, w1, w3, w2):
    out, a, b, h = _ffn_fwd(tag + "_fwd", x, g, w1, w3, w2)
    return out, (a, b, h)


def _ffn_backward(tag, dxo, x, g, w1, w3, w2, saved):
    a, b, h = saved
    dx, dg, da, db, hid, dxh = _ffn_bwd(tag + "_bwd", dxo, a, b, x, g, w1, w3, w2)
    return dx, dg, _mm_tn(tag + "_dw1", da, h), _mm_tn(tag + "_dw3", db, h), _mm_tn(tag + "_dw2", hid, dxh)


def kernel(x, norm_ffn1, ffn1_w1, ffn1_w3, ffn1_w2, norm_mix, w_in, conv_w, conv_b, conv_ln_g, conv_ln_b, conv_out_g, ssm_A_re, ssm_A_im, ssm_log_dt, ssm_B_re, ssm_B_im, ssm_C_re, ssm_C_im, ssm_D, ssm_glu_w, ssm_glu_b, ssm_out_g, w_out, norm_ffn2, ffn2_w1, ffn2_w3, ffn2_w2, norm_final, loss_target, m_norm_ffn1, m_ffn1_w1, m_ffn1_w3, m_ffn1_w2, m_norm_mix, m_w_in, m_conv_w, m_conv_b, m_conv_ln_g, m_conv_ln_b, m_conv_out_g, m_ssm_A_re, m_ssm_A_im, m_ssm_log_dt, m_ssm_B_re, m_ssm_B_im, m_ssm_C_re, m_ssm_C_im, m_ssm_D, m_ssm_glu_w, m_ssm_glu_b, m_ssm_out_g, m_w_out, m_norm_ffn2, m_ffn2_w1, m_ffn2_w3, m_ffn2_w2, m_norm_final, v_norm_ffn1, v_ffn1_w1, v_ffn1_w3, v_ffn1_w2, v_norm_mix, v_w_in, v_conv_w, v_conv_b, v_conv_ln_g, v_conv_ln_b, v_conv_out_g, v_ssm_A_re, v_ssm_A_im, v_ssm_log_dt, v_ssm_B_re, v_ssm_B_im, v_ssm_C_re, v_ssm_C_im, v_ssm_D, v_ssm_glu_w, v_ssm_glu_b, v_ssm_out_g, v_w_out, v_norm_ffn2, v_ffn2_w1, v_ffn2_w3, v_ffn2_w2, v_norm_final):
    args = dict(locals())
    wt = {n: args[n] for n in WEIGHTS}
    mom = {n: args["m_" + n] for n in WEIGHTS}
    var = {n: args["v_" + n] for n in WEIGHTS}

    bsz, seq, d = x.shape
    n = bsz * seq
    c = conv_b.shape[-1]
    groups = c // SSM_GROUP
    gp = groups * SSM_STATE
    u_b = 2

    shards = [(wt[k][0].T if k in COL_SHARDED else wt[k][0]).astype(BF16) for k in BIG] + [wt["conv_w"][0]]
    gathered = _all_gather("gather_weights", shards)
    full = {k: g.reshape(-1, g.shape[-1]) for k, g in zip(BIG, gathered)}
    conv_w_full = gathered[-1].transpose(1, 0, 2).reshape(CONV_WIDTH, c)
    conv_w_pad = jnp.pad(conv_w_full, ((0, CONV_HALO - CONV_WIDTH), (0, 0)))

    vec = lambda k: wt[k].reshape(1, -1)
    g_ffn1, g_mix, g_ffn2, g_fin = vec("norm_ffn1"), vec("norm_mix"), vec("norm_ffn2"), vec("norm_final")
    cb, lng, lnb, cog = vec("conv_b"), vec("conv_ln_g"), vec("conv_ln_b"), vec("conv_out_g")
    d_skip, glu_b, sog = vec("ssm_D"), vec("ssm_glu_b"), vec("ssm_out_g")

    a_re, a_im = wt["ssm_A_re"][0], wt["ssm_A_im"][0]
    log_dt = wt["ssm_log_dt"][0].reshape(groups, 1)
    bt_re = wt["ssm_B_re"][0].transpose(0, 2, 1).reshape(groups * SSM_GROUP, SSM_STATE)
    bt_im = wt["ssm_B_im"][0].transpose(0, 2, 1).reshape(groups * SSM_GROUP, SSM_STATE)
    c_re = wt["ssm_C_re"][0].reshape(groups * SSM_GROUP, SSM_STATE)
    c_im = wt["ssm_C_im"][0].reshape(groups * SSM_GROUP, SSM_STATE)
    per_chan = lambda t: jnp.repeat(t, SSM_GROUP, axis=0)
    ssm_prim = (a_re, a_im, log_dt, per_chan(a_re), per_chan(a_im), per_chan(jnp.broadcast_to(log_dt, a_re.shape)),
                bt_re, bt_im)
    pw_r, pw_i, bb_r, bb_i = _ssm_prep("ssm_prep", ssm_prim)
    tab_f = _scan_tables(pw_r, pw_i, False)
    tab_b = _scan_tables(pw_r, pw_i, True)
    bbd = jnp.concatenate([_block_diag(bb_r, groups), _block_diag(bb_i, groups)], axis=1).astype(BF16)
    cdt = jnp.concatenate([_block_diag(c_re, groups), -_block_diag(c_im, groups)], axis=1).astype(BF16)

    x0 = x.reshape(n, d)
    x1, ffn1_saved = _ffn_forward("ffn1", x0, g_ffn1, full["ffn1_w1"], full["ffn1_w3"], full["ffn1_w2"])
    (proj,), h2 = _rms_mm("mix_in", x1, g_mix, [full["w_in"]], F32)
    proj3 = proj.reshape(bsz, seq, 3 * c)
    an3, cv3 = _conv_fwd("conv_fwd", proj3, conv_w_pad, cb, lng, lnb, cog)
    an = an3.reshape(n, c)
    xs3, cx3 = _scan_fwd("scan_fwd", tab_f, proj3, u_b, bbd, cdt)
    xs = xs3.reshape(n, 2 * gp)
    y, sn = _ssm_out_fwd("ssm_out_fwd", cx3.reshape(n, c), proj, u_b, d_skip, full["ssm_glu_w"], glu_b, sog)
    w_o = full["w_out"]
    x2 = _row_mm("mix_out", [(an, c, 0, w_o, c, 0, False), (sn, c, 0, w_o, c, 1, False)], d, F32, add=x1)
    x3, ffn2_saved = _ffn_forward("ffn2", x2, g_ffn2, full["ffn2_w1"], full["ffn2_w3"], full["ffn2_w2"])
    dx3, loss_tile, d_gfin = _loss_head("loss_head", x3, g_fin, loss_target.reshape(n, d))
    loss = lax.psum(loss_tile[0, 0], MESH_AXES)

    grads = {}
    dx2, grads["norm_ffn2"], grads["ffn2_w1"], grads["ffn2_w3"], grads["ffn2_w2"] = _ffn_backward(
        "ffn2", dx3, x2, g_ffn2, full["ffn2_w1"], full["ffn2_w3"], full["ffn2_w2"], ffn2_saved)

    dmix = _row_mm("mix_out_bwd", [(dx2, d, 0, w_o, 2 * c, 0, True)], 2 * c, F32)
    grads["w_out"] = jnp.concatenate([_mm_tn("dw_out_a", an, dx2), _mm_tn("dw_out_s", sn, dx2)], axis=0)

    dy, du_skip, grads["ssm_glu_w"], grads["ssm_glu_b"], grads["ssm_out_g"], grads["ssm_D"] = _ssm_out_bwd(
        "ssm_out_bwd", dmix, 1, y, proj, u_b, d_skip, full["ssm_glu_w"], glu_b, sog)
    lam3, du3, dab_r, dab_i = _scan_bwd("scan_bwd", tab_b, dy.reshape(bsz, seq, c), xs3,
                                        du_skip.reshape(bsz, seq, c), bbd, cdt)
    lam, du = lam3.reshape(n, 2 * gp), du3.reshape(n, c)
    d_bbd = _mm_tn("ssm_dbb", proj, lam, a_cols=(u_b * c, c))
    d_cdt = _mm_tn("ssm_dc", dy, xs)
    d_are, d_aim, d_ldt, d_btr, d_bti = _ssm_param_grads(
        "ssm_param_grads", ssm_prim,
        dab_r.reshape(SUBLANES, groups, SSM_STATE), dab_i.reshape(SUBLANES, groups, SSM_STATE),
        _block_diag_take(d_bbd[:, :gp], groups), _block_diag_take(d_bbd[:, gp:], groups))
    grads["ssm_A_re"], grads["ssm_A_im"], grads["ssm_log_dt"] = d_are, d_aim, d_ldt
    grads["ssm_B_re"] = d_btr.reshape(groups, SSM_GROUP, SSM_STATE).transpose(0, 2, 1)
    grads["ssm_B_im"] = d_bti.reshape(groups, SSM_GROUP, SSM_STATE).transpose(0, 2, 1)
    grads["ssm_C_re"] = _block_diag_take(d_cdt[:, :gp], groups)
    grads["ssm_C_im"] = -_block_diag_take(d_cdt[:, gp:], groups)

    dconv3, d_cw, grads["conv_b"], grads["conv_ln_g"], grads["conv_ln_b"], grads["conv_out_g"] = _conv_bwd(
        "conv_bwd", dmix.reshape(bsz, seq, 2 * c), proj3, cv3, conv_w_pad, lng, lnb, cog)
    dconv = dconv3.reshape(n, 2 * c)
    grads["conv_w"] = d_cw[:CONV_WIDTH]
    grads["w_in"] = jnp.concatenate([_mm_tn("dw_in_conv", dconv, h2), _mm_tn("dw_in_ssm", du, h2)], axis=0)
    w_i = full["w_in"]
    dx1, grads["norm_mix"] = _dx_rms_bwd("mix_in_bwd", [(dconv, 2 * c, 0, w_i, 2 * c, 0), (du, c, 0, w_i, c, 2)], dx2, x1, g_mix)

    dx0, grads["norm_ffn1"], grads["ffn1_w1"], grads["ffn1_w3"], grads["ffn1_w2"] = _ffn_backward(
        "ffn1", dx1, x0, g_ffn1, full["ffn1_w1"], full["ffn1_w3"], full["ffn1_w2"], ffn1_saved)
    grads["norm_final"] = d_gfin

    send = [grads[k].reshape((N_DEV, -1) + grads[k].shape[1:]) for k in BIG]
    from_core = _exchange_sibling("exchange_grads_core", send)
    sums = [_pair_sum("pair_sum_" + k, s.reshape((N_DEV // 2, 2) + s.shape[1:]), o)
            for k, s, o in zip(BIG, send, from_core)]
    from_chips = _exchange_chips("exchange_grads_chip", sums)
    res = {}
    for k, parts in zip(BIG, from_chips):
        if k in COL_SHARDED:
            parts = _sum_parts("sum_" + k, parts).T[None]
        res[k] = _adamw("adamw_" + k, parts, wt[k], mom[k], var[k])

    small_names = SMALL + ("conv_w",)
    no_state = jnp.zeros_like(grads["conv_w"])
    part, spans = _pack([grads[k] for k in small_names], SUBLANES)
    (all_parts,) = _all_gather("gather_small_grads", [part])
    w_pk, _ = _pack([wt[k] for k in SMALL] + [no_state], SUBLANES)
    m_pk, _ = _pack([mom[k] for k in SMALL] + [no_state], SUBLANES)
    v_pk, _ = _pack([var[k] for k in SMALL] + [no_state], SUBLANES)
    small_out = _adamw("adamw_replicated", all_parts, w_pk, m_pk, v_pk)
    small_shapes = [wt[k].shape for k in SMALL] + [grads["conv_w"].shape]
    small_res = [dict(zip(small_names, _unpack(o, spans, small_shapes))) for o in small_out]
    x_pos, y_pos, c_pos = (lax.axis_index(a) for a in MESH_AXES)
    cw_cols = c // N_DEV
    own_cw = lax.dynamic_slice_in_dim(small_res[0]["conv_w"], (4 * x_pos + 2 * y_pos + c_pos) * cw_cols, cw_cols, axis=1)
    res["conv_w"] = _adamw("adamw_conv_w", own_cw[None], wt["conv_w"], mom["conv_w"], var["conv_w"])

    outs = [loss, dx0.reshape(bsz, seq, d)]
    for kind in range(4):
        outs += [res[k][kind] if k in res else small_res[kind][k] for k in WEIGHTS]
    return tuple(outs)
```

```python
import functools
import math

import jax
import jax.numpy as jnp
from jax import lax
from jax.experimental import pallas as pl
from jax.experimental.pallas import tpu as pltpu

F32 = jnp.float32
BF16 = jnp.bfloat16

EPS = 1e-6
FFN_RES = 0.5
CONV_WIDTH = 31
CONV_HALO = 32
SSM_GROUP = 16
SSM_STATE = 64
ADAM_LR, ADAM_B1, ADAM_B2, ADAM_EPS, ADAM_WD, ADAM_STEP = 0.001, 0.9, 0.999, 1e-08, 0.01, 10

N_DEV = 8
MESH_AXES = ("x", "y", "c")
SUBLANES = 8
LANES = 128
PACK_W = 1024
V7X_VMEM_BYTES = 64 * 2**20
VMEM_LIMIT = V7X_VMEM_BYTES - 8 * 2**20

TILE = dict(row=256, mm_bytes=8 * 2**20, up_m=1024, up_n=256, conv_t=512, scan_t=256, scan_w=512,
            sum_bytes=4 * 2**20)

_GELU_K = math.sqrt(2.0 / math.pi)
_GELU_C = 0.044715


def _pick(n, target, mult):
    best = None
    for t in range(mult, min(n, target) + 1, mult):
        if n % t == 0:
            best = t
    return n if best is None else best


def _cparams(*sem):
    return pltpu.CompilerParams(dimension_semantics=sem, vmem_limit_bytes=VMEM_LIMIT)


def _sds(shape, dtype):
    return jax.ShapeDtypeStruct(shape, dtype)


def _dot(a, b):
    return jnp.dot(a, b, preferred_element_type=F32)


def _dot_nt(a, b):
    return lax.dot_general(a, b, (((1,), (1,)), ((), ())), preferred_element_type=F32)


def _dot_tn(a, b):
    return lax.dot_general(a, b, (((0,), (0,)), ((), ())), preferred_element_type=F32)


def _sigmoid(x):
    return 0.5 * jnp.tanh(0.5 * x) + 0.5


def _rms_stats(x):
    r = lax.rsqrt(jnp.mean(x * x, axis=-1, keepdims=True) + EPS)
    return r, x * r


def _rms_bwd(x, g, dy):
    r, xh = _rms_stats(x)
    dxh = dy * g
    dx = r * (dxh - xh * jnp.mean(dxh * xh, axis=-1, keepdims=True))
    return dx, jnp.sum(dy * xh, axis=0, keepdims=True)


def _rms_mm(name, x, g, ws, out_dtype):
    n, d = x.shape
    f = ws[0].shape[0]
    nw = len(ws)
    tm, tn = _pick(n, TILE["up_m"], 16), _pick(f, TILE["up_n"], LANES)

    def body(x_ref, g_ref, *refs):
        w_refs, o_refs, h_ref = refs[:nw], refs[nw:2 * nw], refs[2 * nw]

        @pl.when(pl.program_id(1) == 0)
        def _():
            _, xh = _rms_stats(x_ref[...])
            h_ref[...] = (xh * g_ref[...]).astype(BF16)

        h = h_ref[...]
        for w_ref, o_ref in zip(w_refs, o_refs):
            o_ref[...] = _dot_nt(h, w_ref[...]).astype(o_ref.dtype)

    outs = pl.pallas_call(
        body, name=name, grid=(n // tm, f // tn),
        in_specs=[pl.BlockSpec((tm, d), lambda i, j: (i, 0)), pl.BlockSpec((1, d), lambda i, j: (0, 0))]
        + [pl.BlockSpec((tn, d), lambda i, j: (j, 0))] * nw,
        out_specs=[pl.BlockSpec((tm, tn), lambda i, j: (i, j))] * nw + [pl.BlockSpec((tm, d), lambda i, j: (i, 0))],
        out_shape=[_sds((n, f), out_dtype)] * nw + [_sds((n, d), BF16)],
        compiler_params=_cparams("parallel", "arbitrary"),
    )(x, g, *ws)
    return outs[:nw], outs[nw]


def _row_chunks(rows):
    step = _pick(rows, TILE["row"], SUBLANES)
    return [pl.ds(r0, step) for r0 in range(0, rows, step)]


def _ffn_fwd(name, x, g, w1t, w3t, w2):
    n, d = x.shape
    f = w2.shape[0]
    tm, tn = _pick(n, TILE["up_m"], 16), _pick(f, TILE["up_n"], LANES)
    nj = f // tn

    def body(x_ref, g_ref, w1_ref, w3_ref, w2_ref, o_ref, a_ref, b_ref, h_ref, acc_ref):
        j = pl.program_id(1)

        @pl.when(j == 0)
        def _():
            for rows in _row_chunks(tm):
                _, xh = _rms_stats(x_ref[rows, :])
                h_ref[rows, :] = (xh * g_ref[...]).astype(BF16)
            acc_ref[...] = jnp.zeros_like(acc_ref)

        h = h_ref[...]
        av, bv = _dot_nt(h, w1_ref[...]), _dot_nt(h, w3_ref[...])
        a_ref[...] = av.astype(BF16)
        b_ref[...] = bv.astype(BF16)
        acc_ref[...] += _dot((av * _sigmoid(av) * bv).astype(BF16), w2_ref[...])

        @pl.when(j == nj - 1)
        def _():
            o_ref[...] = x_ref[...] + FFN_RES * acc_ref[...]

    row = pl.BlockSpec((tm, d), lambda i, j: (i, 0))
    tile = pl.BlockSpec((tm, tn), lambda i, j: (i, j))
    wblk = pl.BlockSpec((tn, d), lambda i, j: (j, 0))
    return pl.pallas_call(
        body, name=name, grid=(n // tm, nj),
        in_specs=[row, pl.BlockSpec((1, d), lambda i, j: (0, 0)), wblk, wblk, wblk],
        out_specs=[row, tile, tile, row],
        out_shape=[_sds((n, d), F32), _sds((n, f), BF16), _sds((n, f), BF16), _sds((n, d), BF16)],
        scratch_shapes=[pltpu.VMEM((tm, d), F32)],
        compiler_params=_cparams("parallel", "arbitrary"),
    )(x, g, w1t, w3t, w2)


def _ffn_bwd_hidden(name, dxo, a, b, w2):
    n, d = dxo.shape
    f = a.shape[1]
    tm, tn = _pick(n, TILE["up_m"], 16), _pick(f, TILE["up_n"], LANES)

    def body(dx_ref, a_ref, b_ref, w_ref, da_ref, db_ref, hid_ref, dxh_ref):
        @pl.when(pl.program_id(1) == 0)
        def _():
            dxh_ref[...] = (FFN_RES * dx_ref[...]).astype(BF16)

        dhid = _dot_nt(dxh_ref[...], w_ref[...])
        av, bv = a_ref[...].astype(F32), b_ref[...].astype(F32)
        sig = _sigmoid(av)
        silu = av * sig
        da_ref[...] = (dhid * bv * (sig * (1.0 + av * (1.0 - sig)))).astype(BF16)
        db_ref[...] = (dhid * silu).astype(BF16)
        hid_ref[...] = (silu * bv).astype(BF16)

    tile = pl.BlockSpec((tm, tn), lambda i, j: (i, j))
    return pl.pallas_call(
        body, name=name, grid=(n // tm, f // tn),
        in_specs=[pl.BlockSpec((tm, d), lambda i, j: (i, 0)), tile, tile, pl.BlockSpec((tn, d), lambda i, j: (j, 0))],
        out_specs=[tile, tile, tile, pl.BlockSpec((tm, d), lambda i, j: (i, 0))],
        out_shape=[_sds((n, f), BF16)] * 3 + [_sds((n, d), BF16)],
        compiler_params=_cparams("parallel", "arbitrary"),
    )(dxo, a, b, w2)


def _loss_head(name, x, g, target):
    n, d = x.shape
    tm = _pick(n, TILE["row"], SUBLANES)

    def body(x_ref, g_ref, t_ref, dx_ref, loss_ref, dg_ref):
        @pl.when(pl.program_id(0) == 0)
        def _():
            loss_ref[...] = jnp.zeros_like(loss_ref)
            dg_ref[...] = jnp.zeros_like(dg_ref)

        xv, gv = x_ref[...], g_ref[...]
        r, xh = _rms_stats(xv)
        err = xh * gv - t_ref[...]
        loss_ref[...] += 0.5 * jnp.sum(jnp.mean(err * err, axis=-1, keepdims=True))
        dy = err * (1.0 / d)
        dxh = dy * gv
        dx_ref[...] = r * (dxh - xh * jnp.mean(dxh * xh, axis=-1, keepdims=True))
        dg_ref[...] += jnp.sum(dy * xh, axis=0, keepdims=True)

    return pl.pallas_call(
        body, name=name, grid=(n // tm,),
        in_specs=[pl.BlockSpec((tm, d), lambda i: (i, 0)), pl.BlockSpec((1, d), lambda i: (0, 0)),
                  pl.BlockSpec((tm, d), lambda i: (i, 0))],
        out_specs=[pl.BlockSpec((tm, d), lambda i: (i, 0)), pl.BlockSpec((SUBLANES, LANES), lambda i: (0, 0)),
                   pl.BlockSpec((1, d), lambda i: (0, 0))],
        out_shape=[_sds((n, d), F32), _sds((SUBLANES, LANES), F32), _sds((1, d), F32)],
        compiler_params=_cparams("arbitrary"),
    )(x, g, target)


def _dx_rms_bwd(name, pairs, dxo, x, g):
    n, dm = x.shape
    tm = _pick(n, TILE["row"], 16)
    npair = len(pairs)

    def body(*refs):
        d_refs, w_refs = refs[:npair], refs[npair:2 * npair]
        dxo_ref, x_ref, g_ref, dx_ref, dg_ref = refs[2 * npair:]

        @pl.when(pl.program_id(0) == 0)
        def _():
            dg_ref[...] = jnp.zeros_like(dg_ref)

        dh = None
        for d_ref, w_ref in zip(d_refs, w_refs):
            t = _dot(d_ref[...].astype(BF16), w_ref[...])
            dh = t if dh is None else dh + t
        dx, dg = _rms_bwd(x_ref[...], g_ref[...], dh)
        dx_ref[...] = dxo_ref[...] + dx
        dg_ref[...] += dg

    row = pl.BlockSpec((tm, dm), lambda i: (i, 0))
    d_specs = [pl.BlockSpec((tm, p[1]), functools.partial(lambda i, cb: (i, cb), cb=p[2])) for p in pairs]
    w_specs = [pl.BlockSpec((p[4], dm), functools.partial(lambda i, rb: (rb, 0), rb=p[5])) for p in pairs]
    return pl.pallas_call(
        body, name=name, grid=(n // tm,),
        in_specs=d_specs + w_specs + [row, row, pl.BlockSpec((1, dm), lambda i: (0, 0))],
        out_specs=[row, pl.BlockSpec((1, dm), lambda i: (0, 0))],
        out_shape=[_sds((n, dm), F32), _sds((1, dm), F32)],
        compiler_params=_cparams("arbitrary"),
    )(*[p[0] for p in pairs], *[p[3] for p in pairs], dxo, x, g)


def _mm_tn(name, a, b, a_cols=None, b_cols=None):
    n = a.shape[0]
    a0, ma = a_cols if a_cols else (0, a.shape[1])
    b0, mb = b_cols if b_cols else (0, b.shape[1])
    assert a0 % ma == 0 and b0 % mb == 0
    ab, bb = a0 // ma, b0 // mb
    tk = _pick(n, TILE["mm_bytes"] // (ma * a.dtype.itemsize + mb * b.dtype.itemsize), 16)

    def body(a_ref, b_ref, o_ref):
        @pl.when(pl.program_id(0) == 0)
        def _():
            o_ref[...] = jnp.zeros_like(o_ref)

        o_ref[...] += _dot_tn(a_ref[...].astype(BF16), b_ref[...].astype(BF16))

    return pl.pallas_call(
        body, name=name, grid=(n // tk,),
        in_specs=[pl.BlockSpec((tk, ma), lambda k: (k, ab)), pl.BlockSpec((tk, mb), lambda k: (k, bb))],
        out_specs=pl.BlockSpec((ma, mb), lambda k: (0, 0)),
        out_shape=_sds((ma, mb), F32),
        compiler_params=_cparams("arbitrary"),
    )(a, b)


def _row_mm(name, pairs, out_w, out_dtype, add=None):
    n = pairs[0][0].shape[0]
    tm = _pick(n, TILE["row"], 16)
    npair = len(pairs)

    def body(*refs):
        a_refs, w_refs = refs[:npair], refs[npair:2 * npair]
        add_ref = refs[2 * npair] if add is not None else None
        o_ref = refs[-1]
        acc = None
        for a_ref, w_ref, p in zip(a_refs, w_refs, pairs):
            av = a_ref[...].astype(BF16)
            t = _dot_nt(av, w_ref[...]) if p[6] else _dot(av, w_ref[...])
            acc = t if acc is None else acc + t
        if add_ref is not None:
            acc = acc + add_ref[...].astype(F32)
        o_ref[...] = acc.astype(o_ref.dtype)

    a_specs = [pl.BlockSpec((tm, p[1]), functools.partial(lambda i, cb: (i, cb), cb=p[2])) for p in pairs]
    w_specs = [pl.BlockSpec((p[4], p[3].shape[1]), functools.partial(lambda i, rb: (rb, 0), rb=p[5])) for p in pairs]
    add_specs = [pl.BlockSpec((tm, out_w), lambda i: (i, 0))] if add is not None else []
    return pl.pallas_call(
        body, name=name, grid=(n // tm,),
        in_specs=a_specs + w_specs + add_specs,
        out_specs=pl.BlockSpec((tm, out_w), lambda i: (i, 0)),
        out_shape=_sds((n, out_w), out_dtype),
        compiler_params=_cparams("parallel"),
    )(*[p[0] for p in pairs], *[p[3] for p in pairs], *([add] if add is not None else []))


def _conv_post(c, ln_g, ln_b, out_g):
    mu = jnp.mean(c, axis=-1, keepdims=True)
    xc = c - mu
    rstd = lax.rsqrt(jnp.mean(xc * xc, axis=-1, keepdims=True) + EPS)
    nrm = xc * rstd
    l = nrm * ln_g + ln_b
    sig = _sigmoid(l)
    s = l * sig
    r, sh = _rms_stats(s)
    return sh * out_g, (rstd, nrm, l, sig, r, sh)


def _conv_taps(a_ref, w_ref, first, rows):
    acc = None
    for k in range(CONV_WIDTH):
        t = w_ref[k:k + 1, :] * a_ref[pl.ds(first + k, rows), :]
        acc = t if acc is None else acc + t
    return acc


def _conv_post_bwd(cv, dout, ln_g, ln_b, out_g):
    _, (rstd, nrm, l, sig, r, sh) = _conv_post(cv, ln_g, ln_b, out_g)
    dsh = dout * out_g
    ds = r * (dsh - sh * jnp.mean(dsh * sh, axis=-1, keepdims=True))
    dl = ds * (sig * (1.0 + l * (1.0 - sig)))
    dn = dl * ln_g
    dc = rstd * (dn - jnp.mean(dn, axis=-1, keepdims=True) - nrm * jnp.mean(dn * nrm, axis=-1, keepdims=True))
    col_sum = lambda t: jnp.sum(t, axis=0, keepdims=True)
    return dc, col_sum(dout * sh), col_sum(dl * nrm), col_sum(dl)


def _conv_fwd(name, proj3, conv_w, conv_b, ln_g, ln_b, out_g):
    bsz, seq, _ = proj3.shape
    c = conv_w.shape[1]
    tt = _pick(seq, TILE["conv_t"], CONV_HALO)
    hb = tt // CONV_HALO
    first = CONV_HALO - (CONV_WIDTH - 1)

    def body(v_ref, g_ref, vp_ref, gp_ref, w_ref, cb_ref, lg_ref, lb_ref, og_ref, o_ref, cv_ref, a_ref):
        keep = (pl.program_id(1) > 0).astype(F32)
        a_ref[pl.ds(0, CONV_HALO), :] = keep * vp_ref[0] * _sigmoid(gp_ref[0])
        a_ref[pl.ds(CONV_HALO, tt), :] = v_ref[0] * _sigmoid(g_ref[0])
        cv = _conv_taps(a_ref, w_ref, first, tt) + cb_ref[...]
        cv_ref[0] = cv
        out, _ = _conv_post(cv, lg_ref[...], lb_ref[...], og_ref[...])
        o_ref[0] = out.astype(BF16)

    vec = pl.BlockSpec((1, c), lambda b, i: (0, 0))
    prev = lambda col: pl.BlockSpec((1, CONV_HALO, c), lambda b, i: (b, jnp.maximum(i * hb - 1, 0), col))
    tile = pl.BlockSpec((1, tt, c), lambda b, i: (b, i, 0))
    return pl.pallas_call(
        body, name=name, grid=(bsz, seq // tt),
        in_specs=[tile, pl.BlockSpec((1, tt, c), lambda b, i: (b, i, 1)),
                  prev(0), prev(1), pl.BlockSpec(conv_w.shape, lambda b, i: (0, 0)), vec, vec, vec, vec],
        out_specs=[tile, tile],
        out_shape=[_sds((bsz, seq, c), BF16), _sds((bsz, seq, c), F32)],
        scratch_shapes=[pltpu.VMEM((CONV_HALO + tt, c), F32)],
        compiler_params=_cparams("parallel", "arbitrary"),
    )(proj3, proj3, proj3, proj3, conv_w, conv_b, ln_g, ln_b, out_g)


def _conv_bwd(name, dmix3, proj3, cv3, conv_w, ln_g, ln_b, out_g):
    bsz, seq, _ = proj3.shape
    c = conv_w.shape[1]
    tt = _pick(seq, TILE["conv_t"], CONV_HALO)
    hb = tt // CONV_HALO
    nt = seq // tt
    last_hb = seq // CONV_HALO - 1
    ext = tt + CONV_HALO
    first = CONV_HALO - (CONV_WIDTH - 1)

    def body(v_ref, g_ref, vp_ref, gp_ref, cv_ref, cvn_ref, d_ref, dn_ref, w_ref, lg_ref, lb_ref, og_ref,
             o_ref, dw_ref, dcb_ref, dlg_ref, dlb_ref, dog_ref, a_ref, dc_ref):
        i = pl.program_id(1)

        @pl.when((pl.program_id(0) == 0) & (i == 0))
        def _():
            for r in (dw_ref, dcb_ref, dlg_ref, dlb_ref, dog_ref):
                r[...] = jnp.zeros_like(r)

        keep_prev = (i > 0).astype(F32)
        keep_next = (i < nt - 1).astype(F32)
        sig_g = _sigmoid(g_ref[0])
        a_ref[pl.ds(0, CONV_HALO), :] = keep_prev * vp_ref[0] * _sigmoid(gp_ref[0])
        a_ref[pl.ds(CONV_HALO, tt), :] = v_ref[0] * sig_g

        lg, lb, og = lg_ref[...], lb_ref[...], og_ref[...]
        dc_own, d_og, d_lg, d_lb = _conv_post_bwd(cv_ref[0], d_ref[0], lg, lb, og)
        dc_next, _, _, _ = _conv_post_bwd(cvn_ref[0], keep_next * dn_ref[0], lg, lb, og)
        dog_ref[...] += d_og
        dlg_ref[...] += d_lg
        dlb_ref[...] += d_lb
        dcb_ref[...] += jnp.sum(dc_own, axis=0, keepdims=True)
        dc_ref[pl.ds(0, tt), :] = dc_own
        dc_ref[pl.ds(tt, CONV_HALO), :] = dc_next

        da = None
        for k in range(CONV_WIDTH):
            t = w_ref[k:k + 1, :] * dc_ref[pl.ds(CONV_WIDTH - 1 - k, tt), :]
            da = t if da is None else da + t
            dw_ref[k:k + 1, :] += jnp.sum(dc_own * a_ref[pl.ds(first + k, tt), :], axis=0, keepdims=True)
        val = v_ref[0]
        o_ref[0] = jnp.concatenate([da * sig_g, da * val * sig_g * (1.0 - sig_g)], axis=-1).astype(BF16)

    vec = pl.BlockSpec((1, c), lambda b, i: (0, 0))
    cur = lambda col: pl.BlockSpec((1, tt, c), lambda b, i: (b, i, col))
    prev = lambda col: pl.BlockSpec((1, CONV_HALO, c), lambda b, i: (b, jnp.maximum(i * hb - 1, 0), col))
    nxt = lambda col: pl.BlockSpec((1, CONV_HALO, c), lambda b, i: (b, jnp.minimum((i + 1) * hb, last_hb), col))
    wspec = pl.BlockSpec(conv_w.shape, lambda b, i: (0, 0))
    return pl.pallas_call(
        body, name=name, grid=(bsz, nt),
        in_specs=[cur(0), cur(1), prev(0), prev(1), cur(0), nxt(0), cur(0), nxt(0), wspec, vec, vec, vec],
        out_specs=[pl.BlockSpec((1, tt, 2 * c), lambda b, i: (b, i, 0)), wspec, vec, vec, vec, vec],
        out_shape=[_sds((bsz, seq, 2 * c), BF16), _sds(conv_w.shape, F32)] + [_sds((1, c), F32)] * 4,
        scratch_shapes=[pltpu.VMEM((CONV_HALO + tt, c), F32), pltpu.VMEM((ext, c), F32)],
        compiler_params=_cparams("arbitrary", "arbitrary"),
    )(proj3, proj3, proj3, proj3, cv3, cv3, dmix3, dmix3, conv_w, ln_g, ln_b, out_g)


def _ssm_discretise(a_re, a_im, log_dt):
    dt = jnp.exp(log_dt)
    zr, zi = a_re * dt, a_im * dt
    mag = jnp.exp(zr)
    ar, ai = mag * jnp.cos(zi), mag * jnp.sin(zi)
    den = a_re * a_re + a_im * a_im
    nr = ar - 1.0
    return ar, ai, (nr * a_re + ai * a_im) / den, (ai * a_re - nr * a_im) / den


def _ssm_system(a_re, a_im, log_dt, a_re_x, a_im_x, log_dt_x, bt_re, bt_im):
    ar, ai, _, _ = _ssm_discretise(a_re, a_im, log_dt)
    _, _, cr, ci = _ssm_discretise(a_re_x, a_im_x, log_dt_x)
    return ar, ai, cr * bt_re - ci * bt_im, cr * bt_im + ci * bt_re


def _ssm_prep(name, prim):
    g, p = prim[0].shape

    def body(*refs):
        pwr_ref, pwi_ref, bbr_ref, bbi_ref = refs[8:]
        ar, ai, bbr, bbi = _ssm_system(*[r[...] for r in refs[:8]])
        bbr_ref[...] = bbr
        bbi_ref[...] = bbi
        pr, pi = ar, ai
        for k in range(SUBLANES):
            pwr_ref[k] = pr
            pwi_ref[k] = pi
            pr, pi = pr * ar - pi * ai, pr * ai + pi * ar

    return pl.pallas_call(
        body, name=name,
        out_shape=[_sds((SUBLANES, g, p), F32)] * 2 + [_sds(prim[6].shape, F32)] * 2,
        compiler_params=pltpu.CompilerParams(vmem_limit_bytes=VMEM_LIMIT),
    )(*prim)


def _ssm_param_grads(name, prim, dab_r, dab_i, dbb_r, dbb_i):
    g, p = prim[0].shape
    h = prim[6].shape[0] // g

    def body(*refs):
        dar_ref, dai_ref, dbr_ref, dbi_ref = refs[8:12]
        o_ar, o_ai, o_dt, o_br, o_bi = refs[12:]
        _, vjp = jax.vjp(_ssm_system, *[r[...] for r in refs[:8]])
        ct = (jnp.sum(dar_ref[...], axis=0), jnp.sum(dai_ref[...], axis=0), dbr_ref[...], dbi_ref[...])
        d_ar, d_ai, d_dt, d_arx, d_aix, d_dtx, d_br, d_bi = vjp(ct)
        per_group = lambda t: jnp.sum(t.reshape(g, h, p), axis=1)
        o_ar[...] = d_ar + per_group(d_arx)
        o_ai[...] = d_ai + per_group(d_aix)
        o_dt[...] = d_dt + jnp.sum(per_group(d_dtx), axis=1, keepdims=True)
        o_br[...] = d_br
        o_bi[...] = d_bi

    return pl.pallas_call(
        body, name=name,
        out_shape=[_sds(prim[k].shape, F32) for k in (0, 1, 2, 6, 7)],
        compiler_params=pltpu.CompilerParams(vmem_limit_bytes=VMEM_LIMIT),
    )(*prim, dab_r, dab_i, dbb_r, dbb_i)


def _cfma(xr, xi, cr, ci, sr, si):
    return xr + (cr * sr - ci * si), xi + (cr * si + ci * sr)


def _scan_tables(pw_r, pw_i, reverse):
    gp = pw_r.shape[1] * pw_r.shape[2]
    pr, pi = pw_r.reshape(SUBLANES, gp), pw_i.reshape(SUBLANES, gp)
    if reverse:
        pi = -pi
    row = jnp.arange(SUBLANES)[:, None]
    tabs = []
    for d in (1, 2, 4):
        keep = (row < SUBLANES - d) if reverse else (row >= d)
        tabs += [jnp.where(keep, pr[d - 1][None, :], 0.0), jnp.where(keep, pi[d - 1][None, :], 0.0)]
    tabs += [pr[::-1], pi[::-1]] if reverse else [pr, pi]
    return jnp.concatenate(tabs, axis=0)


def _scan_fwd(name, tab, proj3, u_block, bbd, cdt):
    bsz, seq, _ = proj3.shape
    c, w = bbd.shape
    gp = w // 2
    tt = _pick(seq, TILE["scan_t"], 16)
    nblk = tt // SUBLANES
    cw = _pick(gp, TILE["scan_w"], LANES)

    def body(tab_ref, u_ref, bbd_ref, cdt_ref, xs_ref, y_ref, carry_ref, bu_ref):
        @pl.when(pl.program_id(1) == 0)
        def _():
            carry_ref[...] = jnp.zeros_like(carry_ref)

        bu_ref[0] = _dot(u_ref[0].astype(BF16), bbd_ref[...])

        for ch in range(gp // cw):
            re, im = pl.ds(ch * cw, cw), pl.ds(gp + ch * cw, cw)

            def blk(r, carry, re=re, im=im):
                tabs = [tab_ref[pl.ds(SUBLANES * k, SUBLANES), re] for k in range(8)]
                rows = pl.ds(pl.multiple_of(r * SUBLANES, SUBLANES), SUBLANES)
                xr, xi = bu_ref[0, rows, re], bu_ref[0, rows, im]
                for j, d in enumerate((1, 2, 4)):
                    xr, xi = _cfma(xr, xi, tabs[2 * j], tabs[2 * j + 1], pltpu.roll(xr, d, 0), pltpu.roll(xi, d, 0))
                xr, xi = _cfma(xr, xi, tabs[6], tabs[7], carry[0], carry[1])
                xs_ref[0, rows, re] = xr
                xs_ref[0, rows, im] = xi
                last = SUBLANES - 1
                return (jnp.broadcast_to(xr[last:, :], xr.shape), jnp.broadcast_to(xi[last:, :], xi.shape))

            cr, ci = lax.fori_loop(0, nblk, blk, (carry_ref[:, re], carry_ref[:, im]))
            carry_ref[:, re] = cr
            carry_ref[:, im] = ci

        y_ref[0] = _dot_nt(xs_ref[0].astype(BF16), cdt_ref[...])

    whole = lambda arr: pl.BlockSpec(arr.shape, lambda b, t: (0, 0))
    return pl.pallas_call(
        body, name=name, grid=(bsz, seq // tt),
        in_specs=[whole(tab), pl.BlockSpec((1, tt, c), lambda b, t: (b, t, u_block)), whole(bbd), whole(cdt)],
        out_specs=[pl.BlockSpec((1, tt, w), lambda b, t: (b, t, 0)), pl.BlockSpec((1, tt, c), lambda b, t: (b, t, 0))],
        out_shape=[_sds((bsz, seq, w), F32), _sds((bsz, seq, c), F32)],
        scratch_shapes=[pltpu.VMEM((SUBLANES, w), F32), pltpu.VMEM((1, tt, w), F32)],
        compiler_params=_cparams("arbitrary", "arbitrary"),
    )(tab, proj3, bbd, cdt)


def _scan_bwd(name, tab, dy3, xs3, du_skip3, bbd, cdt):
    bsz, seq, w = xs3.shape
    c = bbd.shape[0]
    gp = w // 2
    tt = _pick(seq, TILE["scan_t"], 16)
    nblk = tt // SUBLANES
    cw = _pick(gp, TILE["scan_w"], LANES)
    nt = seq // tt

    def body(tab_ref, dy_ref, xs_ref, halo_ref, skip_ref, bbd_ref, cdt_ref, lam_ref, du_ref, dar_ref, dai_ref,
             carry_ref, g_ref):
        t = pl.program_id(1)

        @pl.when(t == 0)
        def _():
            carry_ref[...] = jnp.zeros_like(carry_ref)

        @pl.when((pl.program_id(0) == 0) & (t == 0))
        def _():
            dar_ref[...] = jnp.zeros_like(dar_ref)
            dai_ref[...] = jnp.zeros_like(dai_ref)

        g_ref[0] = _dot(dy_ref[0], cdt_ref[...])

        has_prev = (t < nt - 1).astype(F32)
        row0 = lax.broadcasted_iota(jnp.int32, (SUBLANES, cw), 0) == 0
        last = SUBLANES - 1

        for ch in range(gp // cw):
            re, im = pl.ds(ch * cw, cw), pl.ds(gp + ch * cw, cw)

            def step(rows, xm1r, xm1i, state, re=re, im=im):
                tabs = [tab_ref[pl.ds(SUBLANES * k, SUBLANES), re] for k in range(8)]
                cr, ci, accr, acci = state
                lr, li = g_ref[0, rows, re], g_ref[0, rows, im]
                for j, d in enumerate((1, 2, 4)):
                    lr, li = _cfma(lr, li, tabs[2 * j], tabs[2 * j + 1],
                                   pltpu.roll(lr, SUBLANES - d, 0), pltpu.roll(li, SUBLANES - d, 0))
                lr, li = _cfma(lr, li, tabs[6], tabs[7], cr, ci)
                lam_ref[0, rows, re] = lr
                lam_ref[0, rows, im] = li
                xr, xi = xs_ref[0, rows, re], xs_ref[0, rows, im]
                xpr = jnp.where(row0, jnp.broadcast_to(xm1r[last:, :], xr.shape), pltpu.roll(xr, 1, 0))
                xpi = jnp.where(row0, jnp.broadcast_to(xm1i[last:, :], xi.shape), pltpu.roll(xi, 1, 0))
                accr = accr + (lr * xpr + li * xpi)
                acci = acci + (li * xpr - lr * xpi)
                return (jnp.broadcast_to(lr[:1, :], lr.shape), jnp.broadcast_to(li[:1, :], li.shape), accr, acci)

            def blk(k, state, re=re, im=im, step=step):
                r = nblk - 1 - k
                rows = pl.ds(pl.multiple_of(r * SUBLANES, SUBLANES), SUBLANES)
                prev = pl.ds(pl.multiple_of((r - 1) * SUBLANES, SUBLANES), SUBLANES)
                return step(rows, xs_ref[0, prev, re], xs_ref[0, prev, im], state)

            zero = jnp.zeros((SUBLANES, cw), F32)
            state = lax.fori_loop(0, nblk - 1, blk, (carry_ref[:, re], carry_ref[:, im], zero, zero))
            cr, ci, accr, acci = step(pl.ds(0, SUBLANES), has_prev * halo_ref[0, :, re], has_prev * halo_ref[0, :, im], state)
            carry_ref[:, re] = cr
            carry_ref[:, im] = ci
            dar_ref[:, re] += accr
            dai_ref[:, re] += acci

        du_ref[0] = (_dot_nt(lam_ref[0].astype(BF16), bbd_ref[...]) + skip_ref[0]).astype(BF16)

    tile = pl.BlockSpec((1, tt, w), lambda b, t: (b, nt - 1 - t, 0))
    thin = pl.BlockSpec((1, tt, c), lambda b, t: (b, nt - 1 - t, 0))
    halo = pl.BlockSpec((1, SUBLANES, w), lambda b, t: (b, jnp.maximum((nt - 1 - t) * nblk - 1, 0), 0))
    acc = pl.BlockSpec((SUBLANES, gp), lambda b, t: (0, 0))
    whole = lambda arr: pl.BlockSpec(arr.shape, lambda b, t: (0, 0))
    return pl.pallas_call(
        body, name=name, grid=(bsz, nt),
        in_specs=[whole(tab), thin, tile, halo, thin, whole(bbd), whole(cdt)],
        out_specs=[tile, thin, acc, acc],
        out_shape=[_sds(xs3.shape, F32), _sds((bsz, seq, c), BF16), _sds((SUBLANES, gp), F32), _sds((SUBLANES, gp), F32)],
        scratch_shapes=[pltpu.VMEM((SUBLANES, w), F32), pltpu.VMEM((1, tt, w), F32)],
        compiler_params=_cparams("arbitrary", "arbitrary"),
    )(tab, dy3, xs3, xs3, du_skip3, bbd, cdt)


def _gelu_parts(y):
    inner = _GELU_K * (y + _GELU_C * y * y * y)
    t = jnp.tanh(inner)
    return 0.5 * y * (1.0 + t), t


def _ssm_out_fwd(name, cx, proj, u_block, d_skip, glu_w, glu_b, out_g):
    n, c = cx.shape
    tm = _pick(n, TILE["row"], 16)

    def body(cx_ref, u_ref, d_ref, gw_ref, gb_ref, og_ref, y_ref, o_ref):
        y = cx_ref[...] + d_ref[...] * u_ref[...]
        y_ref[...] = y
        gy, _ = _gelu_parts(y)
        z = _dot(gy.astype(BF16), gw_ref[...]) + gb_ref[...]
        _, sh = _rms_stats(gy * _sigmoid(z))
        o_ref[...] = (sh * og_ref[...]).astype(BF16)

    vec = pl.BlockSpec((1, c), lambda i: (0, 0))
    row = pl.BlockSpec((tm, c), lambda i: (i, 0))
    return pl.pallas_call(
        body, name=name, grid=(n // tm,),
        in_specs=[row, pl.BlockSpec((tm, c), lambda i: (i, u_block)), vec, pl.BlockSpec(glu_w.shape, lambda i: (0, 0)),
                  vec, vec],
        out_specs=[row, row],
        out_shape=[_sds((n, c), F32), _sds((n, c), BF16)],
        compiler_params=_cparams("parallel"),
    )(cx, proj, d_skip, glu_w, glu_b, out_g)


def _ssm_out_bwd(name, dmix, d_block, y, proj, u_block, d_skip, glu_w, glu_b, out_g):
    n, c = y.shape
    tm = _pick(n, TILE["row"], 16)

    def body(d_ref, y_ref, u_ref, dk_ref, gw_ref, gb_ref, og_ref, dy_ref, du_ref, dgw_ref, dgb_ref, dog_ref, dd_ref):
        @pl.when(pl.program_id(0) == 0)
        def _():
            for r in (dgw_ref, dgb_ref, dog_ref, dd_ref):
                r[...] = jnp.zeros_like(r)

        yv = y_ref[...]
        gy, th = _gelu_parts(yv)
        gy16 = gy.astype(BF16)
        sz = _sigmoid(_dot(gy16, gw_ref[...]) + gb_ref[...])
        r, sh = _rms_stats(gy * sz)
        dout = d_ref[...]
        dog_ref[...] += jnp.sum(dout * sh, axis=0, keepdims=True)
        dsh = dout * og_ref[...]
        ds = r * (dsh - sh * jnp.mean(dsh * sh, axis=-1, keepdims=True))
        dz = ds * gy * sz * (1.0 - sz)
        dz16 = dz.astype(BF16)
        dgb_ref[...] += jnp.sum(dz, axis=0, keepdims=True)
        dgw_ref[...] += _dot_tn(gy16, dz16)
        dgy = ds * sz + _dot_nt(dz16, gw_ref[...])
        dgelu = 0.5 * (1.0 + th) + 0.5 * yv * (1.0 - th * th) * (_GELU_K * (1.0 + 3.0 * _GELU_C * yv * yv))
        dy = dgy * dgelu
        dy_ref[...] = dy.astype(BF16)
        du_ref[...] = dy * dk_ref[...]
        dd_ref[...] += jnp.sum(dy * u_ref[...], axis=0, keepdims=True)

    vec = pl.BlockSpec((1, c), lambda i: (0, 0))
    row = pl.BlockSpec((tm, c), lambda i: (i, 0))
    mat = pl.BlockSpec(glu_w.shape, lambda i: (0, 0))
    return pl.pallas_call(
        body, name=name, grid=(n // tm,),
        in_specs=[pl.BlockSpec((tm, c), lambda i: (i, d_block)), row, pl.BlockSpec((tm, c), lambda i: (i, u_block)),
                  vec, mat, vec, vec],
        out_specs=[row, row, mat, vec, vec, vec],
        out_shape=[_sds((n, c), BF16), _sds((n, c), F32), _sds(glu_w.shape, F32)] + [_sds((1, c), F32)] * 3,
        compiler_params=_cparams("arbitrary"),
    )(dmix, y, proj, d_skip, glu_w, glu_b, out_g)


def _mesh_pos():
    return tuple(lax.axis_index(a) for a in MESH_AXES)


def _other_chips(x, y):
    return [(1 - x, y), (x, 1 - y), (1 - x, 1 - y)]


def _remote(src, dst, send_sem, recv_sem, dev):
    return pltpu.make_async_remote_copy(src_ref=src, dst_ref=dst, send_sem=send_sem, recv_sem=recv_sem,
                                        device_id=dev, device_id_type=pl.DeviceIdType.MESH)


def _hbm_call(name, body, operands, out_shapes, scratch):
    hbm = pl.BlockSpec(memory_space=pltpu.HBM)
    return pl.pallas_call(body, name=name, in_specs=[hbm] * len(operands), out_specs=[hbm] * len(out_shapes),
                          out_shape=out_shapes, scratch_shapes=scratch)(*operands)


def _all_gather(name, blocks):
    nop = len(blocks)

    def body(*refs):
        x_refs, o_refs = refs[:nop], refs[nop:2 * nop]
        send_sems, recv_sems, local_sems = refs[2 * nop:]
        x, y, c = _mesh_pos()
        me, sibling = (x, y, c), (x, y, 1 - c)
        chips = _other_chips(x, y)

        def copy(i, k, block_of, to, src=None):
            dst = o_refs[i].at[4 * block_of[0] + 2 * block_of[1] + block_of[2]]
            return _remote(dst if src is None else src, dst, send_sems.at[i, k], recv_sems.at[i, k], to)

        own = [pltpu.make_async_copy(x_refs[i], o_refs[i].at[4 * x + 2 * y + c], local_sems.at[i]) for i in range(nop)]
        for cp in own:
            cp.start()
        first = []
        for i in range(nop):
            first.append(copy(i, 0, me, sibling, src=x_refs[i]))
            first += [copy(i, 1 + j, me, (*chip, c), src=x_refs[i]) for j, chip in enumerate(chips)]
        for cp in first:
            cp.start()
        passed = []
        for i in range(nop):
            for j, chip in enumerate(chips):
                copy(i, 1 + j, (*chip, c), me).wait_recv()
                passed.append(copy(i, 4 + j, (*chip, c), sibling))
                passed[-1].start()
        for i in range(nop):
            copy(i, 0, sibling, me).wait_recv()
            for j, chip in enumerate(chips):
                copy(i, 4 + j, (*chip, 1 - c), me).wait_recv()
        for cp in first + passed:
            cp.wait_send()
        for cp in own:
            cp.wait()

    return _hbm_call(name, body, blocks, [_sds((N_DEV,) + b.shape, b.dtype) for b in blocks],
                     [pltpu.SemaphoreType.DMA((nop, N_DEV - 1)), pltpu.SemaphoreType.DMA((nop, N_DEV - 1)),
                      pltpu.SemaphoreType.DMA((nop,))])


def _exchange_sibling(name, grads):
    nop = len(grads)

    def body(*refs):
        x_refs, o_refs = refs[:nop], refs[nop:2 * nop]
        send_sems, recv_sems = refs[2 * nop:]
        x, y, c = _mesh_pos()
        copies = [_remote(x_refs[i].at[2 * q + (1 - c)], o_refs[i].at[q], send_sems.at[i, q], recv_sems.at[i, q],
                          (x, y, 1 - c)) for i in range(nop) for q in range(N_DEV // 2)]
        for cp in copies:
            cp.start()
        for cp in copies:
            cp.wait_recv()
        for cp in copies:
            cp.wait_send()

    return _hbm_call(name, body, grads, [_sds((N_DEV // 2,) + g.shape[1:], g.dtype) for g in grads],
                     [pltpu.SemaphoreType.DMA((nop, N_DEV // 2)), pltpu.SemaphoreType.DMA((nop, N_DEV // 2))])


def _pair_sum(name, grad, other):
    nchip, _, r, c = grad.shape
    tr = _pick(r, max(SUBLANES, TILE["sum_bytes"] // (8 * c)), SUBLANES)

    def body(g_ref, o_ref, s_ref):
        mine = jnp.where(lax.axis_index("c") == 0, g_ref[0, 0], g_ref[0, 1])
        s_ref[0] = (mine + o_ref[0]).astype(s_ref.dtype)

    return pl.pallas_call(
        body, name=name, grid=(nchip, r // tr),
        in_specs=[pl.BlockSpec((1, 2, tr, c), lambda q, t: (q, 0, t, 0)), pl.BlockSpec((1, tr, c), lambda q, t: (q, t, 0))],
        out_specs=pl.BlockSpec((1, tr, c), lambda q, t: (q, t, 0)),
        out_shape=_sds((nchip, r, c), BF16),
        compiler_params=_cparams("parallel", "parallel"),
    )(grad, other)


def _exchange_chips(name, sums):
    nop = len(sums)

    def body(*refs):
        x_refs, o_refs = refs[:nop], refs[nop:2 * nop]
        send_sems, recv_sems, local_sems = refs[2 * nop:]
        x, y, c = _mesh_pos()
        mine = 2 * x + y
        own = [pltpu.make_async_copy(x_refs[i].at[mine], o_refs[i].at[mine], local_sems.at[i]) for i in range(nop)]
        for cp in own:
            cp.start()
        sends, recvs = [], []
        for i in range(nop):
            for j, (px, py) in enumerate(_other_chips(x, y)):
                theirs = 2 * px + py
                sends.append(_remote(x_refs[i].at[theirs], o_refs[i].at[mine], send_sems.at[i, j], recv_sems.at[i, j],
                                     (px, py, c)))
                recvs.append(_remote(x_refs[i].at[mine], o_refs[i].at[theirs], send_sems.at[i, j], recv_sems.at[i, j],
                                     (px, py, c)))
        for cp in sends:
            cp.start()
        for cp in recvs:
            cp.wait_recv()
        for cp in sends:
            cp.wait_send()
        for cp in own:
            cp.wait()

    return _hbm_call(name, body, sums, [_sds(s.shape, s.dtype) for s in sums],
                     [pltpu.SemaphoreType.DMA((nop, 3)), pltpu.SemaphoreType.DMA((nop, 3)), pltpu.SemaphoreType.DMA((nop,))])


def _part_rows(npart, r, c):
    return _pick(r, max(SUBLANES, TILE["sum_bytes"] // (4 * npart * c)), SUBLANES)


def _sum_parts(name, parts):
    npart, r, c = parts.shape
    tr = _part_rows(npart, r, c)

    def body(p_ref, o_ref):
        g = p_ref[0].astype(F32)
        for k in range(1, npart):
            g = g + p_ref[k].astype(F32)
        o_ref[...] = g

    return pl.pallas_call(
        body, name=name, grid=(r // tr,),
        in_specs=[pl.BlockSpec((npart, tr, c), lambda i: (0, i, 0))],
        out_specs=pl.BlockSpec((tr, c), lambda i: (i, 0)),
        out_shape=_sds((r, c), F32),
        compiler_params=_cparams("parallel"),
    )(parts)


def _adamw(name, parts, w, m, v):
    npart, r, c = parts.shape
    lead = len(w.shape) - 2
    tr = _part_rows(npart, r, c)
    c1 = 1.0 - ADAM_B1 ** ADAM_STEP
    c2 = 1.0 - ADAM_B2 ** ADAM_STEP
    at = (0,) * lead + (slice(None), slice(None))

    def body(p_ref, w_ref, m_ref, v_ref, g_ref, d_ref, nm_ref, nv_ref):
        g = p_ref[0].astype(F32)
        for k in range(1, npart):
            g = g + p_ref[k].astype(F32)
        nm = ADAM_B1 * m_ref[at] + (1.0 - ADAM_B1) * g
        nv = ADAM_B2 * v_ref[at] + (1.0 - ADAM_B2) * (g * g)
        g_ref[at] = g
        nm_ref[at] = nm
        nv_ref[at] = nv
        d_ref[at] = -ADAM_LR * ((nm / c1) / (jnp.sqrt(nv / c2) + ADAM_EPS) + ADAM_WD * w_ref[at])

    row = pl.BlockSpec((1,) * lead + (tr, c), lambda i: (0,) * lead + (i, 0))
    return pl.pallas_call(
        body, name=name, grid=(r // tr,),
        in_specs=[pl.BlockSpec((npart, tr, c), lambda i: (0, i, 0)), row, row, row],
        out_specs=[row] * 4,
        out_shape=[_sds(w.shape, F32)] * 4,
        compiler_params=_cparams("parallel"),
    )(parts, w, m, v)


def _pack(pieces, row_mult, lead=()):
    nl = len(lead)
    flat, spans, off = [], [], 0
    for p in pieces:
        p = p.reshape(lead + (-1,))
        size = p.shape[-1]
        padded = -(-size // PACK_W) * PACK_W
        flat.append(jnp.pad(p, [(0, 0)] * nl + [(0, padded - size)]))
        spans.append((off, size))
        off += padded
    rows = -(-(off // PACK_W) // row_mult) * row_mult
    if rows * PACK_W > off:
        flat.append(jnp.zeros(lead + (rows * PACK_W - off,), flat[0].dtype))
    return jnp.concatenate(flat, axis=-1).reshape(lead + (rows, PACK_W)), spans


def _unpack(buf, spans, shapes, lead=0):
    flat = buf.reshape(buf.shape[:lead] + (-1,))
    return [flat[..., o:o + s].reshape(buf.shape[:lead] + tuple(shape)) for (o, s), shape in zip(spans, shapes)]


def _block_diag(rows_gh, groups):
    gh, p = rows_gh.shape
    own = (jnp.arange(gh)[:, None] // (gh // groups) == jnp.arange(groups)[None, :]).astype(rows_gh.dtype)
    return (own[:, :, None] * rows_gh[:, None, :]).reshape(gh, groups * p)


def _block_diag_take(dense, groups):
    gh = dense.shape[0]
    p = dense.shape[1] // groups
    own = (jnp.arange(gh)[:, None] // (gh // groups) == jnp.arange(groups)[None, :]).astype(dense.dtype)
    return jnp.sum(dense.reshape(gh, groups, p) * own[:, :, None], axis=1)


BIG = ("ffn1_w1", "ffn1_w3", "ffn1_w2", "w_in", "ssm_glu_w", "w_out", "ffn2_w1", "ffn2_w3", "ffn2_w2")
COL_SHARDED = ("ffn1_w1", "ffn1_w3", "w_in", "ffn2_w1", "ffn2_w3", "conv_w")
SMALL = ("norm_ffn1", "norm_mix", "conv_b", "conv_ln_g", "conv_ln_b", "conv_out_g", "ssm_A_re", "ssm_A_im",
         "ssm_log_dt", "ssm_B_re", "ssm_B_im", "ssm_C_re", "ssm_C_im", "ssm_D", "ssm_glu_b", "ssm_out_g",
         "norm_ffn2", "norm_final")
WEIGHTS = ("norm_ffn1", "ffn1_w1", "ffn1_w3", "ffn1_w2", "norm_mix", "w_in", "conv_w", "conv_b", "conv_ln_g",
           "conv_ln_b", "conv_out_g", "ssm_A_re", "ssm_A_im", "ssm_log_dt", "ssm_B_re", "ssm_B_im", "ssm_C_re",
           "ssm_C_im", "ssm_D", "ssm_glu_w", "ssm_glu_b", "ssm_out_g", "w_out", "norm_ffn2", "ffn2_w1", "ffn2_w3",
           "ffn2_w2", "norm_final")


def _ffn_forward(tag, x, g, w1, w3, w2):
    out, a, b, h = _ffn_fwd(tag + "_fwd", x, g, w1, w3, w2)
    return out, (a, b, h)


def _ffn_backward(tag, dxo, x, g, w1, w3, w2, saved):
    a, b, h = saved
    da, db, hid, dxh = _ffn_bwd_hidden(tag + "_bwd_hidden", dxo, a, b, w2)
    f = a.shape[1]
    dx, dg = _dx_rms_bwd(tag + "_bwd_dx", [(da, f, 0, w1, f, 0), (db, f, 0, w3, f, 0)], dxo, x, g)
    return dx, dg, _mm_tn(tag + "_dw1", da, h), _mm_tn(tag + "_dw3", db, h), _mm_tn(tag + "_dw2", hid, dxh)


def kernel(x, norm_ffn1, ffn1_w1, ffn1_w3, ffn1_w2, norm_mix, w_in, conv_w, conv_b, conv_ln_g, conv_ln_b, conv_out_g, ssm_A_re, ssm_A_im, ssm_log_dt, ssm_B_re, ssm_B_im, ssm_C_re, ssm_C_im, ssm_D, ssm_glu_w, ssm_glu_b, ssm_out_g, w_out, norm_ffn2, ffn2_w1, ffn2_w3, ffn2_w2, norm_final, loss_target, m_norm_ffn1, m_ffn1_w1, m_ffn1_w3, m_ffn1_w2, m_norm_mix, m_w_in, m_conv_w, m_conv_b, m_conv_ln_g, m_conv_ln_b, m_conv_out_g, m_ssm_A_re, m_ssm_A_im, m_ssm_log_dt, m_ssm_B_re, m_ssm_B_im, m_ssm_C_re, m_ssm_C_im, m_ssm_D, m_ssm_glu_w, m_ssm_glu_b, m_ssm_out_g, m_w_out, m_norm_ffn2, m_ffn2_w1, m_ffn2_w3, m_ffn2_w2, m_norm_final, v_norm_ffn1, v_ffn1_w1, v_ffn1_w3, v_ffn1_w2, v_norm_mix, v_w_in, v_conv_w, v_conv_b, v_conv_ln_g, v_conv_ln_b, v_conv_out_g, v_ssm_A_re, v_ssm_A_im, v_ssm_log_dt, v_ssm_B_re, v_ssm_B_im, v_ssm_C_re, v_ssm_C_im, v_ssm_D, v_ssm_glu_w, v_ssm_glu_b, v_ssm_out_g, v_w_out, v_norm_ffn2, v_ffn2_w1, v_ffn2_w3, v_ffn2_w2, v_norm_final):
    args = dict(locals())
    wt = {n: args[n] for n in WEIGHTS}
    mom = {n: args["m_" + n] for n in WEIGHTS}
    var = {n: args["v_" + n] for n in WEIGHTS}

    bsz, seq, d = x.shape
    n = bsz * seq
    c = conv_b.shape[-1]
    groups = c // SSM_GROUP
    gp = groups * SSM_STATE
    u_b = 2

    shards = [(wt[k][0].T if k in COL_SHARDED else wt[k][0]).astype(BF16) for k in BIG] + [wt["conv_w"][0]]
    gathered = _all_gather("gather_weights", shards)
    full = {k: g.reshape(-1, g.shape[-1]) for k, g in zip(BIG, gathered)}
    conv_w_full = gathered[-1].transpose(1, 0, 2).reshape(CONV_WIDTH, c)
    conv_w_pad = jnp.pad(conv_w_full, ((0, CONV_HALO - CONV_WIDTH), (0, 0)))

    vec = lambda k: wt[k].reshape(1, -1)
    g_ffn1, g_mix, g_ffn2, g_fin = vec("norm_ffn1"), vec("norm_mix"), vec("norm_ffn2"), vec("norm_final")
    cb, lng, lnb, cog = vec("conv_b"), vec("conv_ln_g"), vec("conv_ln_b"), vec("conv_out_g")
    d_skip, glu_b, sog = vec("ssm_D"), vec("ssm_glu_b"), vec("ssm_out_g")

    a_re, a_im = wt["ssm_A_re"][0], wt["ssm_A_im"][0]
    log_dt = wt["ssm_log_dt"][0].reshape(groups, 1)
    bt_re = wt["ssm_B_re"][0].transpose(0, 2, 1).reshape(groups * SSM_GROUP, SSM_STATE)
    bt_im = wt["ssm_B_im"][0].transpose(0, 2, 1).reshape(groups * SSM_GROUP, SSM_STATE)
    c_re = wt["ssm_C_re"][0].reshape(groups * SSM_GROUP, SSM_STATE)
    c_im = wt["ssm_C_im"][0].reshape(groups * SSM_GROUP, SSM_STATE)
    per_chan = lambda t: jnp.repeat(t, SSM_GROUP, axis=0)
    ssm_prim = (a_re, a_im, log_dt, per_chan(a_re), per_chan(a_im), per_chan(jnp.broadcast_to(log_dt, a_re.shape)),
                bt_re, bt_im)
    pw_r, pw_i, bb_r, bb_i = _ssm_prep("ssm_prep", ssm_prim)
    tab_f = _scan_tables(pw_r, pw_i, False)
    tab_b = _scan_tables(pw_r, pw_i, True)
    bbd = jnp.concatenate([_block_diag(bb_r, groups), _block_diag(bb_i, groups)], axis=1).astype(BF16)
    cdt = jnp.concatenate([_block_diag(c_re, groups), -_block_diag(c_im, groups)], axis=1).astype(BF16)

    x0 = x.reshape(n, d)
    x1, ffn1_saved = _ffn_forward("ffn1", x0, g_ffn1, full["ffn1_w1"], full["ffn1_w3"], full["ffn1_w2"])
    (proj,), h2 = _rms_mm("mix_in", x1, g_mix, [full["w_in"]], F32)
    proj3 = proj.reshape(bsz, seq, 3 * c)
    an3, cv3 = _conv_fwd("conv_fwd", proj3, conv_w_pad, cb, lng, lnb, cog)
    an = an3.reshape(n, c)
    xs3, cx3 = _scan_fwd("scan_fwd", tab_f, proj3, u_b, bbd, cdt)
    xs = xs3.reshape(n, 2 * gp)
    y, sn = _ssm_out_fwd("ssm_out_fwd", cx3.reshape(n, c), proj, u_b, d_skip, full["ssm_glu_w"], glu_b, sog)
    w_o = full["w_out"]
    x2 = _row_mm("mix_out", [(an, c, 0, w_o, c, 0, False), (sn, c, 0, w_o, c, 1, False)], d, F32, add=x1)
    x3, ffn2_saved = _ffn_forward("ffn2", x2, g_ffn2, full["ffn2_w1"], full["ffn2_w3"], full["ffn2_w2"])
    dx3, loss_tile, d_gfin = _loss_head("loss_head", x3, g_fin, loss_target.reshape(n, d))
    loss = lax.psum(loss_tile[0, 0], MESH_AXES)

    grads = {}
    dx2, grads["norm_ffn2"], grads["ffn2_w1"], grads["ffn2_w3"], grads["ffn2_w2"] = _ffn_backward(
        "ffn2", dx3, x2, g_ffn2, full["ffn2_w1"], full["ffn2_w3"], full["ffn2_w2"], ffn2_saved)

    dmix = _row_mm("mix_out_bwd", [(dx2, d, 0, w_o, 2 * c, 0, True)], 2 * c, F32)
    grads["w_out"] = jnp.concatenate([_mm_tn("dw_out_a", an, dx2), _mm_tn("dw_out_s", sn, dx2)], axis=0)

    dy, du_skip, grads["ssm_glu_w"], grads["ssm_glu_b"], grads["ssm_out_g"], grads["ssm_D"] = _ssm_out_bwd(
        "ssm_out_bwd", dmix, 1, y, proj, u_b, d_skip, full["ssm_glu_w"], glu_b, sog)
    lam3, du3, dab_r, dab_i = _scan_bwd("scan_bwd", tab_b, dy.reshape(bsz, seq, c), xs3,
                                        du_skip.reshape(bsz, seq, c), bbd, cdt)
    lam, du = lam3.reshape(n, 2 * gp), du3.reshape(n, c)
    d_bbd = _mm_tn("ssm_dbb", proj, lam, a_cols=(u_b * c, c))
    d_cdt = _mm_tn("ssm_dc", dy, xs)
    d_are, d_aim, d_ldt, d_btr, d_bti = _ssm_param_grads(
        "ssm_param_grads", ssm_prim,
        dab_r.reshape(SUBLANES, groups, SSM_STATE), dab_i.reshape(SUBLANES, groups, SSM_STATE),
        _block_diag_take(d_bbd[:, :gp], groups), _block_diag_take(d_bbd[:, gp:], groups))
    grads["ssm_A_re"], grads["ssm_A_im"], grads["ssm_log_dt"] = d_are, d_aim, d_ldt
    grads["ssm_B_re"] = d_btr.reshape(groups, SSM_GROUP, SSM_STATE).transpose(0, 2, 1)
    grads["ssm_B_im"] = d_bti.reshape(groups, SSM_GROUP, SSM_STATE).transpose(0, 2, 1)
    grads["ssm_C_re"] = _block_diag_take(d_cdt[:, :gp], groups)
    grads["ssm_C_im"] = -_block_diag_take(d_cdt[:, gp:], groups)

    dconv3, d_cw, grads["conv_b"], grads["conv_ln_g"], grads["conv_ln_b"], grads["conv_out_g"] = _conv_bwd(
        "conv_bwd", dmix.reshape(bsz, seq, 2 * c), proj3, cv3, conv_w_pad, lng, lnb, cog)
    dconv = dconv3.reshape(n, 2 * c)
    grads["conv_w"] = d_cw[:CONV_WIDTH]
    grads["w_in"] = jnp.concatenate([_mm_tn("dw_in_conv", dconv, h2), _mm_tn("dw_in_ssm", du, h2)], axis=0)
    w_i = full["w_in"]
    dx1, grads["norm_mix"] = _dx_rms_bwd("mix_in_bwd", [(dconv, 2 * c, 0, w_i, 2 * c, 0), (du, c, 0, w_i, c, 2)], dx2, x1, g_mix)

    dx0, grads["norm_ffn1"], grads["ffn1_w1"], grads["ffn1_w3"], grads["ffn1_w2"] = _ffn_backward(
        "ffn1", dx1, x0, g_ffn1, full["ffn1_w1"], full["ffn1_w3"], full["ffn1_w2"], ffn1_saved)
    grads["norm_final"] = d_gfin

    send = [grads[k].reshape((N_DEV, -1) + grads[k].shape[1:]) for k in BIG]
    from_core = _exchange_sibling("exchange_grads_core", send)
    sums = [_pair_sum("pair_sum_" + k, s.reshape((N_DEV // 2, 2) + s.shape[1:]), o)
            for k, s, o in zip(BIG, send, from_core)]
    from_chips = _exchange_chips("exchange_grads_chip", sums)
    res = {}
    for k, parts in zip(BIG, from_chips):
        if k in COL_SHARDED:
            swap = lambda t: jnp.swapaxes(t, -1, -2)
            res[k] = [swap(t) for t in _adamw("adamw_" + k, parts, swap(wt[k]), swap(mom[k]), swap(var[k]))]
        else:
            res[k] = _adamw("adamw_" + k, parts, wt[k], mom[k], var[k])

    small_names = SMALL + ("conv_w",)
    no_state = jnp.zeros_like(grads["conv_w"])
    part, spans = _pack([grads[k] for k in small_names], SUBLANES)
    (all_parts,) = _all_gather("gather_small_grads", [part])
    w_pk, _ = _pack([wt[k] for k in SMALL] + [no_state], SUBLANES)
    m_pk, _ = _pack([mom[k] for k in SMALL] + [no_state], SUBLANES)
    v_pk, _ = _pack([var[k] for k in SMALL] + [no_state], SUBLANES)
    small_out = _adamw("adamw_replicated", all_parts, w_pk, m_pk, v_pk)
    small_shapes = [wt[k].shape for k in SMALL] + [grads["conv_w"].shape]
    small_res = [dict(zip(small_names, _unpack(o, spans, small_shapes))) for o in small_out]
    x_pos, y_pos, c_pos = (lax.axis_index(a) for a in MESH_AXES)
    cw_cols = c // N_DEV
    own_cw = lax.dynamic_slice_in_dim(small_res[0]["conv_w"], (4 * x_pos + 2 * y_pos + c_pos) * cw_cols, cw_cols, axis=1)
    res["conv_w"] = _adamw("adamw_conv_w", own_cw[None], wt["conv_w"], mom["conv_w"], var["conv_w"])

    outs = [loss, dx0.reshape(bsz, seq, d)]
    for kind in range(4):
        outs += [res[k][kind] if k in res else small_res[kind][k] for k in WEIGHTS]
    return tuple(outs)
```

```python
import collections
import functools
import math

import jax
import jax.numpy as jnp
from jax import lax
from jax.experimental import pallas as pl
from jax.experimental.pallas import tpu as pltpu

F32 = jnp.float32
BF16 = jnp.bfloat16

EPS = 1e-6
FFN_RES = 0.5
CONV_WIDTH = 31
CONV_HALO = 32
SSM_GROUP = 16
SSM_STATE = 64
ADAM_LR, ADAM_B1, ADAM_B2, ADAM_EPS, ADAM_WD, ADAM_STEP = 0.001, 0.9, 0.999, 1e-08, 0.01, 10

N_DEV = 8
MESH_AXES = ("x", "y", "c")
SUBLANES = 8
LANES = 128
PACK_W = 1024
V7X_VMEM_BYTES = 64 * 2**20
VMEM_LIMIT = V7X_VMEM_BYTES - 8 * 2**20

TILE = dict(row=256, mm_bytes=8 * 2**20, up_m=1024, up_n=256, conv_t=512, scan_t=256, scan_w=512,
            sum_bytes=4 * 2**20)

_GELU_K = math.sqrt(2.0 / math.pi)
_GELU_C = 0.044715


def _pick(n, target, mult):
    best = None
    for t in range(mult, min(n, target) + 1, mult):
        if n % t == 0:
            best = t
    return n if best is None else best


def _cparams(*sem):
    return pltpu.CompilerParams(dimension_semantics=sem, vmem_limit_bytes=VMEM_LIMIT)


def _sds(shape, dtype):
    return jax.ShapeDtypeStruct(shape, dtype)


def _call(name, body, grid, in_specs, out_specs, out_shape, operands, sem, scratch=(), exchange=None):
    if exchange is None:
        res = pl.pallas_call(body, name=name, grid=grid, in_specs=list(in_specs), out_specs=list(out_specs),
                             out_shape=list(out_shape), scratch_shapes=list(scratch),
                             compiler_params=_cparams(*sem))(*operands)
        return list(res), None
    n_in, n_out, n_scr = len(in_specs), len(out_specs), len(scratch)
    n_xin, n_xout = len(exchange.operands), len(exchange.out_shapes)
    hbm = pl.BlockSpec(memory_space=pltpu.HBM)

    def with_exchange(*refs):
        cuts, pos = [], 0
        for size in (n_in, n_xin, n_out, n_xout, n_scr):
            cuts.append(refs[pos:pos + size])
            pos += size
        ins, x_in, outs, x_out, scr = cuts
        sems = refs[pos:]
        ids = [pl.program_id(axis) for axis in range(len(grid))]
        first = functools.reduce(lambda p, q: p & q, [i == 0 for i in ids])
        last = functools.reduce(lambda p, q: p & q, [i == g - 1 for i, g in zip(ids, grid)])

        @pl.when(first)
        def _():
            exchange.start(x_in, x_out, sems)

        body(*ins, *outs, *scr)

        @pl.when(last)
        def _():
            exchange.finish(x_in, x_out, sems)

    res = pl.pallas_call(
        with_exchange, name=name, grid=grid, in_specs=list(in_specs) + [hbm] * n_xin,
        out_specs=list(out_specs) + [hbm] * n_xout, out_shape=list(out_shape) + list(exchange.out_shapes),
        scratch_shapes=list(scratch) + list(exchange.scratch),
        compiler_params=_cparams(*["arbitrary"] * len(grid)))(*operands, *exchange.operands)
    return list(res[:n_out]), list(res[n_out:])


def _dot(a, b):
    return jnp.dot(a, b, preferred_element_type=F32)


def _dot_nt(a, b):
    return lax.dot_general(a, b, (((1,), (1,)), ((), ())), preferred_element_type=F32)


def _dot_tn(a, b):
    return lax.dot_general(a, b, (((0,), (0,)), ((), ())), preferred_element_type=F32)


def _sigmoid(x):
    return 0.5 * jnp.tanh(0.5 * x) + 0.5


def _rms_stats(x):
    r = lax.rsqrt(jnp.mean(x * x, axis=-1, keepdims=True) + EPS)
    return r, x * r


def _rms_bwd(x, g, dy):
    r, xh = _rms_stats(x)
    dxh = dy * g
    dx = r * (dxh - xh * jnp.mean(dxh * xh, axis=-1, keepdims=True))
    return dx, jnp.sum(dy * xh, axis=0, keepdims=True)


def _rms_mm(name, x, g, ws, out_dtype):
    n, d = x.shape
    f = ws[0].shape[0]
    nw = len(ws)
    tm, tn = _pick(n, TILE["up_m"], 16), _pick(f, TILE["up_n"], LANES)

    def body(x_ref, g_ref, *refs):
        w_refs, o_refs, h_ref = refs[:nw], refs[nw:2 * nw], refs[2 * nw]

        @pl.when(pl.program_id(1) == 0)
        def _():
            _, xh = _rms_stats(x_ref[...])
            h_ref[...] = (xh * g_ref[...]).astype(BF16)

        h = h_ref[...]
        for w_ref, o_ref in zip(w_refs, o_refs):
            o_ref[...] = _dot_nt(h, w_ref[...]).astype(o_ref.dtype)

    outs = pl.pallas_call(
        body, name=name, grid=(n // tm, f // tn),
        in_specs=[pl.BlockSpec((tm, d), lambda i, j: (i, 0)), pl.BlockSpec((1, d), lambda i, j: (0, 0))]
        + [pl.BlockSpec((tn, d), lambda i, j: (j, 0))] * nw,
        out_specs=[pl.BlockSpec((tm, tn), lambda i, j: (i, j))] * nw + [pl.BlockSpec((tm, d), lambda i, j: (i, 0))],
        out_shape=[_sds((n, f), out_dtype)] * nw + [_sds((n, d), BF16)],
        compiler_params=_cparams("parallel", "arbitrary"),
    )(x, g, *ws)
    return outs[:nw], outs[nw]


def _row_chunks(rows):
    step = _pick(rows, TILE["row"], SUBLANES)
    return [pl.ds(r0, step) for r0 in range(0, rows, step)]


def _ffn_fwd(name, x, g, w1t, w3t, w2, exchange=None):
    n, d = x.shape
    f = w2.shape[0]
    tm, tn = _pick(n, TILE["up_m"], 16), _pick(f, TILE["up_n"], LANES)
    nj = f // tn

    def body(x_ref, g_ref, w1_ref, w3_ref, w2_ref, o_ref, a_ref, b_ref, h_ref, acc_ref):
        j = pl.program_id(1)

        @pl.when(j == 0)
        def _():
            for rows in _row_chunks(tm):
                _, xh = _rms_stats(x_ref[rows, :])
                h_ref[rows, :] = (xh * g_ref[...]).astype(BF16)
            acc_ref[...] = jnp.zeros_like(acc_ref)

        h = h_ref[...]
        av, bv = _dot_nt(h, w1_ref[...]), _dot_nt(h, w3_ref[...])
        a_ref[...] = av.astype(BF16)
        b_ref[...] = bv.astype(BF16)
        acc_ref[...] += _dot((av * _sigmoid(av) * bv).astype(BF16), w2_ref[...])

        @pl.when(j == nj - 1)
        def _():
            o_ref[...] = x_ref[...] + FFN_RES * acc_ref[...]

    row = pl.BlockSpec((tm, d), lambda i, j: (i, 0))
    tile = pl.BlockSpec((tm, tn), lambda i, j: (i, j))
    wblk = pl.BlockSpec((tn, d), lambda i, j: (j, 0))
    return _call(
        name, body, (n // tm, nj),
        [row, pl.BlockSpec((1, d), lambda i, j: (0, 0)), wblk, wblk, wblk], [row, tile, tile, row],
        [_sds((n, d), F32), _sds((n, f), BF16), _sds((n, f), BF16), _sds((n, d), BF16)],
        (x, g, w1t, w3t, w2), ("parallel", "arbitrary"), scratch=[pltpu.VMEM((tm, d), F32)], exchange=exchange)


def _ffn_bwd_hidden(name, dxo, a, b, w2, exchange=None):
    n, d = dxo.shape
    f = a.shape[1]
    tm, tn = _pick(n, TILE["up_m"], 16), _pick(f, TILE["up_n"], LANES)

    def body(dx_ref, a_ref, b_ref, w_ref, da_ref, db_ref, hid_ref, dxh_ref):
        @pl.when(pl.program_id(1) == 0)
        def _():
            dxh_ref[...] = (FFN_RES * dx_ref[...]).astype(BF16)

        dhid = _dot_nt(dxh_ref[...], w_ref[...])
        av, bv = a_ref[...].astype(F32), b_ref[...].astype(F32)
        sig = _sigmoid(av)
        silu = av * sig
        da_ref[...] = (dhid * bv * (sig * (1.0 + av * (1.0 - sig)))).astype(BF16)
        db_ref[...] = (dhid * silu).astype(BF16)
        hid_ref[...] = (silu * bv).astype(BF16)

    tile = pl.BlockSpec((tm, tn), lambda i, j: (i, j))
    return _call(
        name, body, (n // tm, f // tn),
        [pl.BlockSpec((tm, d), lambda i, j: (i, 0)), tile, tile, pl.BlockSpec((tn, d), lambda i, j: (j, 0))],
        [tile, tile, tile, pl.BlockSpec((tm, d), lambda i, j: (i, 0))],
        [_sds((n, f), BF16)] * 3 + [_sds((n, d), BF16)], (dxo, a, b, w2), ("parallel", "arbitrary"), exchange=exchange)


def _loss_head(name, x, g, target):
    n, d = x.shape
    tm = _pick(n, TILE["row"], SUBLANES)

    def body(x_ref, g_ref, t_ref, dx_ref, loss_ref, dg_ref):
        @pl.when(pl.program_id(0) == 0)
        def _():
            loss_ref[...] = jnp.zeros_like(loss_ref)
            dg_ref[...] = jnp.zeros_like(dg_ref)

        xv, gv = x_ref[...], g_ref[...]
        r, xh = _rms_stats(xv)
        err = xh * gv - t_ref[...]
        loss_ref[...] += 0.5 * jnp.sum(jnp.mean(err * err, axis=-1, keepdims=True))
        dy = err * (1.0 / d)
        dxh = dy * gv
        dx_ref[...] = r * (dxh - xh * jnp.mean(dxh * xh, axis=-1, keepdims=True))
        dg_ref[...] += jnp.sum(dy * xh, axis=0, keepdims=True)

    return pl.pallas_call(
        body, name=name, grid=(n // tm,),
        in_specs=[pl.BlockSpec((tm, d), lambda i: (i, 0)), pl.BlockSpec((1, d), lambda i: (0, 0)),
                  pl.BlockSpec((tm, d), lambda i: (i, 0))],
        out_specs=[pl.BlockSpec((tm, d), lambda i: (i, 0)), pl.BlockSpec((SUBLANES, LANES), lambda i: (0, 0)),
                   pl.BlockSpec((1, d), lambda i: (0, 0))],
        out_shape=[_sds((n, d), F32), _sds((SUBLANES, LANES), F32), _sds((1, d), F32)],
        compiler_params=_cparams("arbitrary"),
    )(x, g, target)


def _dx_rms_bwd(name, pairs, dxo, x, g, exchange=None):
    n, dm = x.shape
    tm = _pick(n, TILE["row"], 16)
    npair = len(pairs)

    def body(*refs):
        d_refs, w_refs = refs[:npair], refs[npair:2 * npair]
        dxo_ref, x_ref, g_ref, dx_ref, dg_ref = refs[2 * npair:]

        @pl.when(pl.program_id(0) == 0)
        def _():
            dg_ref[...] = jnp.zeros_like(dg_ref)

        dh = None
        for d_ref, w_ref in zip(d_refs, w_refs):
            t = _dot(d_ref[...].astype(BF16), w_ref[...])
            dh = t if dh is None else dh + t
        dx, dg = _rms_bwd(x_ref[...], g_ref[...], dh)
        dx_ref[...] = dxo_ref[...] + dx
        dg_ref[...] += dg

    row = pl.BlockSpec((tm, dm), lambda i: (i, 0))
    d_specs = [pl.BlockSpec((tm, p[1]), functools.partial(lambda i, cb: (i, cb), cb=p[2])) for p in pairs]
    w_specs = [pl.BlockSpec((p[4], dm), functools.partial(lambda i, rb: (rb, 0), rb=p[5])) for p in pairs]
    return _call(
        name, body, (n // tm,), d_specs + w_specs + [row, row, pl.BlockSpec((1, dm), lambda i: (0, 0))],
        [row, pl.BlockSpec((1, dm), lambda i: (0, 0))], [_sds((n, dm), F32), _sds((1, dm), F32)],
        (*[p[0] for p in pairs], *[p[3] for p in pairs], dxo, x, g), ("arbitrary",), exchange=exchange)


def _mm_tn(name, a, b, a_cols=None, b_cols=None):
    n = a.shape[0]
    a0, ma = a_cols if a_cols else (0, a.shape[1])
    b0, mb = b_cols if b_cols else (0, b.shape[1])
    assert a0 % ma == 0 and b0 % mb == 0
    ab, bb = a0 // ma, b0 // mb
    tk = _pick(n, TILE["mm_bytes"] // (ma * a.dtype.itemsize + mb * b.dtype.itemsize), 16)

    def body(a_ref, b_ref, o_ref):
        @pl.when(pl.program_id(0) == 0)
        def _():
            o_ref[...] = jnp.zeros_like(o_ref)

        o_ref[...] += _dot_tn(a_ref[...].astype(BF16), b_ref[...].astype(BF16))

    return pl.pallas_call(
        body, name=name, grid=(n // tk,),
        in_specs=[pl.BlockSpec((tk, ma), lambda k: (k, ab)), pl.BlockSpec((tk, mb), lambda k: (k, bb))],
        out_specs=pl.BlockSpec((ma, mb), lambda k: (0, 0)),
        out_shape=_sds((ma, mb), F32),
        compiler_params=_cparams("arbitrary"),
    )(a, b)


def _row_mm(name, pairs, out_w, out_dtype, add=None):
    n = pairs[0][0].shape[0]
    tm = _pick(n, TILE["row"], 16)
    npair = len(pairs)

    def body(*refs):
        a_refs, w_refs = refs[:npair], refs[npair:2 * npair]
        add_ref = refs[2 * npair] if add is not None else None
        o_ref = refs[-1]
        acc = None
        for a_ref, w_ref, p in zip(a_refs, w_refs, pairs):
            av = a_ref[...].astype(BF16)
            t = _dot_nt(av, w_ref[...]) if p[6] else _dot(av, w_ref[...])
            acc = t if acc is None else acc + t
        if add_ref is not None:
            acc = acc + add_ref[...].astype(F32)
        o_ref[...] = acc.astype(o_ref.dtype)

    a_specs = [pl.BlockSpec((tm, p[1]), functools.partial(lambda i, cb: (i, cb), cb=p[2])) for p in pairs]
    w_specs = [pl.BlockSpec((p[4], p[3].shape[1]), functools.partial(lambda i, rb: (rb, 0), rb=p[5])) for p in pairs]
    add_specs = [pl.BlockSpec((tm, out_w), lambda i: (i, 0))] if add is not None else []
    return pl.pallas_call(
        body, name=name, grid=(n // tm,),
        in_specs=a_specs + w_specs + add_specs,
        out_specs=pl.BlockSpec((tm, out_w), lambda i: (i, 0)),
        out_shape=_sds((n, out_w), out_dtype),
        compiler_params=_cparams("parallel"),
    )(*[p[0] for p in pairs], *[p[3] for p in pairs], *([add] if add is not None else []))


def _conv_post(c, ln_g, ln_b, out_g):
    mu = jnp.mean(c, axis=-1, keepdims=True)
    xc = c - mu
    rstd = lax.rsqrt(jnp.mean(xc * xc, axis=-1, keepdims=True) + EPS)
    nrm = xc * rstd
    l = nrm * ln_g + ln_b
    sig = _sigmoid(l)
    s = l * sig
    r, sh = _rms_stats(s)
    return sh * out_g, (rstd, nrm, l, sig, r, sh)


def _conv_taps(a_ref, w_ref, first, rows):
    acc = None
    for k in range(CONV_WIDTH):
        t = w_ref[k:k + 1, :] * a_ref[pl.ds(first + k, rows), :]
        acc = t if acc is None else acc + t
    return acc


def _conv_post_bwd(cv, dout, ln_g, ln_b, out_g):
    _, (rstd, nrm, l, sig, r, sh) = _conv_post(cv, ln_g, ln_b, out_g)
    dsh = dout * out_g
    ds = r * (dsh - sh * jnp.mean(dsh * sh, axis=-1, keepdims=True))
    dl = ds * (sig * (1.0 + l * (1.0 - sig)))
    dn = dl * ln_g
    dc = rstd * (dn - jnp.mean(dn, axis=-1, keepdims=True) - nrm * jnp.mean(dn * nrm, axis=-1, keepdims=True))
    col_sum = lambda t: jnp.sum(t, axis=0, keepdims=True)
    return dc, col_sum(dout * sh), col_sum(dl * nrm), col_sum(dl)


def _conv_fwd(name, proj3, conv_w, conv_b, ln_g, ln_b, out_g):
    bsz, seq, _ = proj3.shape
    c = conv_w.shape[1]
    tt = _pick(seq, TILE["conv_t"], CONV_HALO)
    hb = tt // CONV_HALO
    first = CONV_HALO - (CONV_WIDTH - 1)

    def body(v_ref, g_ref, vp_ref, gp_ref, w_ref, cb_ref, lg_ref, lb_ref, og_ref, o_ref, cv_ref, a_ref):
        keep = (pl.program_id(1) > 0).astype(F32)
        a_ref[pl.ds(0, CONV_HALO), :] = keep * vp_ref[0] * _sigmoid(gp_ref[0])
        a_ref[pl.ds(CONV_HALO, tt), :] = v_ref[0] * _sigmoid(g_ref[0])
        cv = _conv_taps(a_ref, w_ref, first, tt) + cb_ref[...]
        cv_ref[0] = cv
        out, _ = _conv_post(cv, lg_ref[...], lb_ref[...], og_ref[...])
        o_ref[0] = out.astype(BF16)

    vec = pl.BlockSpec((1, c), lambda b, i: (0, 0))
    prev = lambda col: pl.BlockSpec((1, CONV_HALO, c), lambda b, i: (b, jnp.maximum(i * hb - 1, 0), col))
    tile = pl.BlockSpec((1, tt, c), lambda b, i: (b, i, 0))
    return pl.pallas_call(
        body, name=name, grid=(bsz, seq // tt),
        in_specs=[tile, pl.BlockSpec((1, tt, c), lambda b, i: (b, i, 1)),
                  prev(0), prev(1), pl.BlockSpec(conv_w.shape, lambda b, i: (0, 0)), vec, vec, vec, vec],
        out_specs=[tile, tile],
        out_shape=[_sds((bsz, seq, c), BF16), _sds((bsz, seq, c), F32)],
        scratch_shapes=[pltpu.VMEM((CONV_HALO + tt, c), F32)],
        compiler_params=_cparams("parallel", "arbitrary"),
    )(proj3, proj3, proj3, proj3, conv_w, conv_b, ln_g, ln_b, out_g)


def _conv_bwd(name, dmix3, proj3, cv3, conv_w, ln_g, ln_b, out_g):
    bsz, seq, _ = proj3.shape
    c = conv_w.shape[1]
    tt = _pick(seq, TILE["conv_t"], CONV_HALO)
    hb = tt // CONV_HALO
    nt = seq // tt
    last_hb = seq // CONV_HALO - 1
    ext = tt + CONV_HALO
    first = CONV_HALO - (CONV_WIDTH - 1)

    def body(v_ref, g_ref, vp_ref, gp_ref, cv_ref, cvn_ref, d_ref, dn_ref, w_ref, lg_ref, lb_ref, og_ref,
             o_ref, dw_ref, dcb_ref, dlg_ref, dlb_ref, dog_ref, a_ref, dc_ref):
        i = pl.program_id(1)

        @pl.when((pl.program_id(0) == 0) & (i == 0))
        def _():
            for r in (dw_ref, dcb_ref, dlg_ref, dlb_ref, dog_ref):
                r[...] = jnp.zeros_like(r)

        keep_prev = (i > 0).astype(F32)
        keep_next = (i < nt - 1).astype(F32)
        sig_g = _sigmoid(g_ref[0])
        a_ref[pl.ds(0, CONV_HALO), :] = keep_prev * vp_ref[0] * _sigmoid(gp_ref[0])
        a_ref[pl.ds(CONV_HALO, tt), :] = v_ref[0] * sig_g

        lg, lb, og = lg_ref[...], lb_ref[...], og_ref[...]
        dc_own, d_og, d_lg, d_lb = _conv_post_bwd(cv_ref[0], d_ref[0], lg, lb, og)
        dc_next, _, _, _ = _conv_post_bwd(cvn_ref[0], keep_next * dn_ref[0], lg, lb, og)
        dog_ref[...] += d_og
        dlg_ref[...] += d_lg
        dlb_ref[...] += d_lb
        dcb_ref[...] += jnp.sum(dc_own, axis=0, keepdims=True)
        dc_ref[pl.ds(0, tt), :] = dc_own
        dc_ref[pl.ds(tt, CONV_HALO), :] = dc_next

        da = None
        for k in range(CONV_WIDTH):
            t = w_ref[k:k + 1, :] * dc_ref[pl.ds(CONV_WIDTH - 1 - k, tt), :]
            da = t if da is None else da + t
            dw_ref[k:k + 1, :] += jnp.sum(dc_own * a_ref[pl.ds(first + k, tt), :], axis=0, keepdims=True)
        val = v_ref[0]
        o_ref[0] = jnp.concatenate([da * sig_g, da * val * sig_g * (1.0 - sig_g)], axis=-1).astype(BF16)

    vec = pl.BlockSpec((1, c), lambda b, i: (0, 0))
    cur = lambda col: pl.BlockSpec((1, tt, c), lambda b, i: (b, i, col))
    prev = lambda col: pl.BlockSpec((1, CONV_HALO, c), lambda b, i: (b, jnp.maximum(i * hb - 1, 0), col))
    nxt = lambda col: pl.BlockSpec((1, CONV_HALO, c), lambda b, i: (b, jnp.minimum((i + 1) * hb, last_hb), col))
    wspec = pl.BlockSpec(conv_w.shape, lambda b, i: (0, 0))
    return pl.pallas_call(
        body, name=name, grid=(bsz, nt),
        in_specs=[cur(0), cur(1), prev(0), prev(1), cur(0), nxt(0), cur(0), nxt(0), wspec, vec, vec, vec],
        out_specs=[pl.BlockSpec((1, tt, 2 * c), lambda b, i: (b, i, 0)), wspec, vec, vec, vec, vec],
        out_shape=[_sds((bsz, seq, 2 * c), BF16), _sds(conv_w.shape, F32)] + [_sds((1, c), F32)] * 4,
        scratch_shapes=[pltpu.VMEM((CONV_HALO + tt, c), F32), pltpu.VMEM((ext, c), F32)],
        compiler_params=_cparams("arbitrary", "arbitrary"),
    )(proj3, proj3, proj3, proj3, cv3, cv3, dmix3, dmix3, conv_w, ln_g, ln_b, out_g)


def _ssm_discretise(a_re, a_im, log_dt):
    dt = jnp.exp(log_dt)
    zr, zi = a_re * dt, a_im * dt
    mag = jnp.exp(zr)
    ar, ai = mag * jnp.cos(zi), mag * jnp.sin(zi)
    den = a_re * a_re + a_im * a_im
    nr = ar - 1.0
    return ar, ai, (nr * a_re + ai * a_im) / den, (ai * a_re - nr * a_im) / den


def _ssm_system(a_re, a_im, log_dt, a_re_x, a_im_x, log_dt_x, bt_re, bt_im):
    ar, ai, _, _ = _ssm_discretise(a_re, a_im, log_dt)
    _, _, cr, ci = _ssm_discretise(a_re_x, a_im_x, log_dt_x)
    return ar, ai, cr * bt_re - ci * bt_im, cr * bt_im + ci * bt_re


def _ssm_prep(name, prim):
    g, p = prim[0].shape

    def body(*refs):
        pwr_ref, pwi_ref, bbr_ref, bbi_ref = refs[8:]
        ar, ai, bbr, bbi = _ssm_system(*[r[...] for r in refs[:8]])
        bbr_ref[...] = bbr
        bbi_ref[...] = bbi
        pr, pi = ar, ai
        for k in range(SUBLANES):
            pwr_ref[k] = pr
            pwi_ref[k] = pi
            pr, pi = pr * ar - pi * ai, pr * ai + pi * ar

    return pl.pallas_call(
        body, name=name,
        out_shape=[_sds((SUBLANES, g, p), F32)] * 2 + [_sds(prim[6].shape, F32)] * 2,
        compiler_params=pltpu.CompilerParams(vmem_limit_bytes=VMEM_LIMIT),
    )(*prim)


def _ssm_param_grads(name, prim, dab_r, dab_i, dbb_r, dbb_i):
    g, p = prim[0].shape
    h = prim[6].shape[0] // g

    def body(*refs):
        dar_ref, dai_ref, dbr_ref, dbi_ref = refs[8:12]
        o_ar, o_ai, o_dt, o_br, o_bi = refs[12:]
        _, vjp = jax.vjp(_ssm_system, *[r[...] for r in refs[:8]])
        ct = (jnp.sum(dar_ref[...], axis=0), jnp.sum(dai_ref[...], axis=0), dbr_ref[...], dbi_ref[...])
        d_ar, d_ai, d_dt, d_arx, d_aix, d_dtx, d_br, d_bi = vjp(ct)
        per_group = lambda t: jnp.sum(t.reshape(g, h, p), axis=1)
        o_ar[...] = d_ar + per_group(d_arx)
        o_ai[...] = d_ai + per_group(d_aix)
        o_dt[...] = d_dt + jnp.sum(per_group(d_dtx), axis=1, keepdims=True)
        o_br[...] = d_br
        o_bi[...] = d_bi

    return pl.pallas_call(
        body, name=name,
        out_shape=[_sds(prim[k].shape, F32) for k in (0, 1, 2, 6, 7)],
        compiler_params=pltpu.CompilerParams(vmem_limit_bytes=VMEM_LIMIT),
    )(*prim, dab_r, dab_i, dbb_r, dbb_i)


def _cfma(xr, xi, cr, ci, sr, si):
    return xr + (cr * sr - ci * si), xi + (cr * si + ci * sr)


def _scan_tables(pw_r, pw_i, reverse):
    gp = pw_r.shape[1] * pw_r.shape[2]
    pr, pi = pw_r.reshape(SUBLANES, gp), pw_i.reshape(SUBLANES, gp)
    if reverse:
        pi = -pi
    row = jnp.arange(SUBLANES)[:, None]
    tabs = []
    for d in (1, 2, 4):
        keep = (row < SUBLANES - d) if reverse else (row >= d)
        tabs += [jnp.where(keep, pr[d - 1][None, :], 0.0), jnp.where(keep, pi[d - 1][None, :], 0.0)]
    tabs += [pr[::-1], pi[::-1]] if reverse else [pr, pi]
    return jnp.concatenate(tabs, axis=0)


def _scan_fwd(name, tab, proj3, u_block, bbd, cdt):
    bsz, seq, _ = proj3.shape
    c, w = bbd.shape
    gp = w // 2
    tt = _pick(seq, TILE["scan_t"], 16)
    nblk = tt // SUBLANES
    cw = _pick(gp, TILE["scan_w"], LANES)

    def body(tab_ref, u_ref, bbd_ref, cdt_ref, xs_ref, y_ref, carry_ref, bu_ref):
        @pl.when(pl.program_id(1) == 0)
        def _():
            carry_ref[...] = jnp.zeros_like(carry_ref)

        bu_ref[0] = _dot(u_ref[0].astype(BF16), bbd_ref[...])

        for ch in range(gp // cw):
            re, im = pl.ds(ch * cw, cw), pl.ds(gp + ch * cw, cw)

            def blk(r, carry, re=re, im=im):
                tabs = [tab_ref[pl.ds(SUBLANES * k, SUBLANES), re] for k in range(8)]
                rows = pl.ds(pl.multiple_of(r * SUBLANES, SUBLANES), SUBLANES)
                xr, xi = bu_ref[0, rows, re], bu_ref[0, rows, im]
                for j, d in enumerate((1, 2, 4)):
                    xr, xi = _cfma(xr, xi, tabs[2 * j], tabs[2 * j + 1], pltpu.roll(xr, d, 0), pltpu.roll(xi, d, 0))
                xr, xi = _cfma(xr, xi, tabs[6], tabs[7], carry[0], carry[1])
                xs_ref[0, rows, re] = xr
                xs_ref[0, rows, im] = xi
                last = SUBLANES - 1
                return (jnp.broadcast_to(xr[last:, :], xr.shape), jnp.broadcast_to(xi[last:, :], xi.shape))

            cr, ci = lax.fori_loop(0, nblk, blk, (carry_ref[:, re], carry_ref[:, im]))
            carry_ref[:, re] = cr
            carry_ref[:, im] = ci

        y_ref[0] = _dot_nt(xs_ref[0].astype(BF16), cdt_ref[...])

    whole = lambda arr: pl.BlockSpec(arr.shape, lambda b, t: (0, 0))
    return pl.pallas_call(
        body, name=name, grid=(bsz, seq // tt),
        in_specs=[whole(tab), pl.BlockSpec((1, tt, c), lambda b, t: (b, t, u_block)), whole(bbd), whole(cdt)],
        out_specs=[pl.BlockSpec((1, tt, w), lambda b, t: (b, t, 0)), pl.BlockSpec((1, tt, c), lambda b, t: (b, t, 0))],
        out_shape=[_sds((bsz, seq, w), F32), _sds((bsz, seq, c), F32)],
        scratch_shapes=[pltpu.VMEM((SUBLANES, w), F32), pltpu.VMEM((1, tt, w), F32)],
        compiler_params=_cparams("arbitrary", "arbitrary"),
    )(tab, proj3, bbd, cdt)


def _scan_bwd(name, tab, dy3, xs3, du_skip3, bbd, cdt, exchange=None):
    bsz, seq, w = xs3.shape
    c = bbd.shape[0]
    gp = w // 2
    tt = _pick(seq, TILE["scan_t"], 16)
    nblk = tt // SUBLANES
    cw = _pick(gp, TILE["scan_w"], LANES)
    nt = seq // tt

    def body(tab_ref, dy_ref, xs_ref, halo_ref, skip_ref, bbd_ref, cdt_ref, lam_ref, du_ref, dar_ref, dai_ref,
             carry_ref, g_ref):
        t = pl.program_id(1)

        @pl.when(t == 0)
        def _():
            carry_ref[...] = jnp.zeros_like(carry_ref)

        @pl.when((pl.program_id(0) == 0) & (t == 0))
        def _():
            dar_ref[...] = jnp.zeros_like(dar_ref)
            dai_ref[...] = jnp.zeros_like(dai_ref)

        g_ref[0] = _dot(dy_ref[0], cdt_ref[...])

        has_prev = (t < nt - 1).astype(F32)
        row0 = lax.broadcasted_iota(jnp.int32, (SUBLANES, cw), 0) == 0
        last = SUBLANES - 1

        for ch in range(gp // cw):
            re, im = pl.ds(ch * cw, cw), pl.ds(gp + ch * cw, cw)

            def step(rows, xm1r, xm1i, state, re=re, im=im):
                tabs = [tab_ref[pl.ds(SUBLANES * k, SUBLANES), re] for k in range(8)]
                cr, ci, accr, acci = state
                lr, li = g_ref[0, rows, re], g_ref[0, rows, im]
                for j, d in enumerate((1, 2, 4)):
                    lr, li = _cfma(lr, li, tabs[2 * j], tabs[2 * j + 1],
                                   pltpu.roll(lr, SUBLANES - d, 0), pltpu.roll(li, SUBLANES - d, 0))
                lr, li = _cfma(lr, li, tabs[6], tabs[7], cr, ci)
                lam_ref[0, rows, re] = lr
                lam_ref[0, rows, im] = li
                xr, xi = xs_ref[0, rows, re], xs_ref[0, rows, im]
                xpr = jnp.where(row0, jnp.broadcast_to(xm1r[last:, :], xr.shape), pltpu.roll(xr, 1, 0))
                xpi = jnp.where(row0, jnp.broadcast_to(xm1i[last:, :], xi.shape), pltpu.roll(xi, 1, 0))
                accr = accr + (lr * xpr + li * xpi)
                acci = acci + (li * xpr - lr * xpi)
                return (jnp.broadcast_to(lr[:1, :], lr.shape), jnp.broadcast_to(li[:1, :], li.shape), accr, acci)

            def blk(k, state, re=re, im=im, step=step):
                r = nblk - 1 - k
                rows = pl.ds(pl.multiple_of(r * SUBLANES, SUBLANES), SUBLANES)
                prev = pl.ds(pl.multiple_of((r - 1) * SUBLANES, SUBLANES), SUBLANES)
                return step(rows, xs_ref[0, prev, re], xs_ref[0, prev, im], state)

            zero = jnp.zeros((SUBLANES, cw), F32)
            state = lax.fori_loop(0, nblk - 1, blk, (carry_ref[:, re], carry_ref[:, im], zero, zero))
            cr, ci, accr, acci = step(pl.ds(0, SUBLANES), has_prev * halo_ref[0, :, re], has_prev * halo_ref[0, :, im], state)
            carry_ref[:, re] = cr
            carry_ref[:, im] = ci
            dar_ref[:, re] += accr
            dai_ref[:, re] += acci

        du_ref[0] = (_dot_nt(lam_ref[0].astype(BF16), bbd_ref[...]) + skip_ref[0]).astype(BF16)

    tile = pl.BlockSpec((1, tt, w), lambda b, t: (b, nt - 1 - t, 0))
    thin = pl.BlockSpec((1, tt, c), lambda b, t: (b, nt - 1 - t, 0))
    halo = pl.BlockSpec((1, SUBLANES, w), lambda b, t: (b, jnp.maximum((nt - 1 - t) * nblk - 1, 0), 0))
    acc = pl.BlockSpec((SUBLANES, gp), lambda b, t: (0, 0))
    whole = lambda arr: pl.BlockSpec(arr.shape, lambda b, t: (0, 0))
    return _call(
        name, body, (bsz, nt), [whole(tab), thin, tile, halo, thin, whole(bbd), whole(cdt)], [tile, thin, acc, acc],
        [_sds(xs3.shape, F32), _sds((bsz, seq, c), BF16), _sds((SUBLANES, gp), F32), _sds((SUBLANES, gp), F32)],
        (tab, dy3, xs3, xs3, du_skip3, bbd, cdt), ("arbitrary", "arbitrary"),
        scratch=[pltpu.VMEM((SUBLANES, w), F32), pltpu.VMEM((1, tt, w), F32)], exchange=exchange)


def _gelu_parts(y):
    inner = _GELU_K * (y + _GELU_C * y * y * y)
    t = jnp.tanh(inner)
    return 0.5 * y * (1.0 + t), t


def _ssm_out_fwd(name, cx, proj, u_block, d_skip, glu_w, glu_b, out_g):
    n, c = cx.shape
    tm = _pick(n, TILE["row"], 16)

    def body(cx_ref, u_ref, d_ref, gw_ref, gb_ref, og_ref, y_ref, o_ref):
        y = cx_ref[...] + d_ref[...] * u_ref[...]
        y_ref[...] = y
        gy, _ = _gelu_parts(y)
        z = _dot(gy.astype(BF16), gw_ref[...]) + gb_ref[...]
        _, sh = _rms_stats(gy * _sigmoid(z))
        o_ref[...] = (sh * og_ref[...]).astype(BF16)

    vec = pl.BlockSpec((1, c), lambda i: (0, 0))
    row = pl.BlockSpec((tm, c), lambda i: (i, 0))
    return pl.pallas_call(
        body, name=name, grid=(n // tm,),
        in_specs=[row, pl.BlockSpec((tm, c), lambda i: (i, u_block)), vec, pl.BlockSpec(glu_w.shape, lambda i: (0, 0)),
                  vec, vec],
        out_specs=[row, row],
        out_shape=[_sds((n, c), F32), _sds((n, c), BF16)],
        compiler_params=_cparams("parallel"),
    )(cx, proj, d_skip, glu_w, glu_b, out_g)


def _ssm_out_bwd(name, dmix, d_block, y, proj, u_block, d_skip, glu_w, glu_b, out_g):
    n, c = y.shape
    tm = _pick(n, TILE["row"], 16)

    def body(d_ref, y_ref, u_ref, dk_ref, gw_ref, gb_ref, og_ref, dy_ref, du_ref, dgw_ref, dgb_ref, dog_ref, dd_ref):
        @pl.when(pl.program_id(0) == 0)
        def _():
            for r in (dgw_ref, dgb_ref, dog_ref, dd_ref):
                r[...] = jnp.zeros_like(r)

        yv = y_ref[...]
        gy, th = _gelu_parts(yv)
        gy16 = gy.astype(BF16)
        sz = _sigmoid(_dot(gy16, gw_ref[...]) + gb_ref[...])
        r, sh = _rms_stats(gy * sz)
        dout = d_ref[...]
        dog_ref[...] += jnp.sum(dout * sh, axis=0, keepdims=True)
        dsh = dout * og_ref[...]
        ds = r * (dsh - sh * jnp.mean(dsh * sh, axis=-1, keepdims=True))
        dz = ds * gy * sz * (1.0 - sz)
        dz16 = dz.astype(BF16)
        dgb_ref[...] += jnp.sum(dz, axis=0, keepdims=True)
        dgw_ref[...] += _dot_tn(gy16, dz16)
        dgy = ds * sz + _dot_nt(dz16, gw_ref[...])
        dgelu = 0.5 * (1.0 + th) + 0.5 * yv * (1.0 - th * th) * (_GELU_K * (1.0 + 3.0 * _GELU_C * yv * yv))
        dy = dgy * dgelu
        dy_ref[...] = dy.astype(BF16)
        du_ref[...] = dy * dk_ref[...]
        dd_ref[...] += jnp.sum(dy * u_ref[...], axis=0, keepdims=True)

    vec = pl.BlockSpec((1, c), lambda i: (0, 0))
    row = pl.BlockSpec((tm, c), lambda i: (i, 0))
    mat = pl.BlockSpec(glu_w.shape, lambda i: (0, 0))
    return pl.pallas_call(
        body, name=name, grid=(n // tm,),
        in_specs=[pl.BlockSpec((tm, c), lambda i: (i, d_block)), row, pl.BlockSpec((tm, c), lambda i: (i, u_block)),
                  vec, mat, vec, vec],
        out_specs=[row, row, mat, vec, vec, vec],
        out_shape=[_sds((n, c), BF16), _sds((n, c), F32), _sds(glu_w.shape, F32)] + [_sds((1, c), F32)] * 3,
        compiler_params=_cparams("arbitrary"),
    )(dmix, y, proj, d_skip, glu_w, glu_b, out_g)


def _mesh_pos():
    return tuple(lax.axis_index(a) for a in MESH_AXES)


def _other_chips(x, y):
    return [(1 - x, y), (x, 1 - y), (1 - x, 1 - y)]


def _remote(src, dst, send_sem, recv_sem, dev):
    return pltpu.make_async_remote_copy(src_ref=src, dst_ref=dst, send_sem=send_sem, recv_sem=recv_sem,
                                        device_id=dev, device_id_type=pl.DeviceIdType.MESH)


def _hbm_call(name, body, operands, out_shapes, scratch):
    hbm = pl.BlockSpec(memory_space=pltpu.HBM)
    return pl.pallas_call(body, name=name, in_specs=[hbm] * len(operands), out_specs=[hbm] * len(out_shapes),
                          out_shape=out_shapes, scratch_shapes=scratch)(*operands)


_Exchange = collections.namedtuple("_Exchange", "operands out_shapes scratch start finish")


def _run_exchange(name, plan):
    nin, nout = len(plan.operands), len(plan.out_shapes)

    def body(*refs):
        parts = refs[:nin], refs[nin:nin + nout], refs[nin + nout:]
        plan.start(*parts)
        plan.finish(*parts)

    return _hbm_call(name, body, plan.operands, plan.out_shapes, plan.scratch)


def _gather_plan(blocks):
    nop = len(blocks)

    def copies(x_refs, o_refs, sems):
        send_sems, recv_sems, local_sems = sems
        x, y, c = _mesh_pos()
        me, sibling = (x, y, c), (x, y, 1 - c)
        chips = _other_chips(x, y)

        def copy(i, k, block_of, to, src=None):
            dst = o_refs[i].at[4 * block_of[0] + 2 * block_of[1] + block_of[2]]
            return _remote(dst if src is None else src, dst, send_sems.at[i, k], recv_sems.at[i, k], to)

        own = [pltpu.make_async_copy(x_refs[i], o_refs[i].at[4 * x + 2 * y + c], local_sems.at[i]) for i in range(nop)]
        first = []
        for i in range(nop):
            first.append(copy(i, 0, me, sibling, src=x_refs[i]))
            first += [copy(i, 1 + j, me, (*chip, c), src=x_refs[i]) for j, chip in enumerate(chips)]
        return copy, own, first, me, sibling, chips, c

    def start(x_refs, o_refs, sems):
        _, own, first, *_ = copies(x_refs, o_refs, sems)
        for cp in own + first:
            cp.start()

    def finish(x_refs, o_refs, sems):
        copy, own, first, me, sibling, chips, c = copies(x_refs, o_refs, sems)
        passed = []
        for i in range(nop):
            for j, chip in enumerate(chips):
                copy(i, 1 + j, (*chip, c), me).wait_recv()
                passed.append(copy(i, 4 + j, (*chip, c), sibling))
                passed[-1].start()
        for i in range(nop):
            copy(i, 0, sibling, me).wait_recv()
            for j, chip in enumerate(chips):
                copy(i, 4 + j, (*chip, 1 - c), me).wait_recv()
        for cp in first + passed:
            cp.wait_send()
        for cp in own:
            cp.wait()

    return _Exchange(list(blocks), [_sds((N_DEV,) + b.shape, b.dtype) for b in blocks],
                     [pltpu.SemaphoreType.DMA((nop, N_DEV - 1)), pltpu.SemaphoreType.DMA((nop, N_DEV - 1)),
                      pltpu.SemaphoreType.DMA((nop,))], start, finish)


def _exchange_sibling(name, grads):
    nop = len(grads)

    def body(*refs):
        x_refs, o_refs = refs[:nop], refs[nop:2 * nop]
        send_sems, recv_sems = refs[2 * nop:]
        x, y, c = _mesh_pos()
        copies = [_remote(x_refs[i].at[2 * q + (1 - c)], o_refs[i].at[q], send_sems.at[i, q], recv_sems.at[i, q],
                          (x, y, 1 - c)) for i in range(nop) for q in range(N_DEV // 2)]
        for cp in copies:
            cp.start()
        for cp in copies:
            cp.wait_recv()
        for cp in copies:
            cp.wait_send()

    return _hbm_call(name, body, grads, [_sds((N_DEV // 2,) + g.shape[1:], g.dtype) for g in grads],
                     [pltpu.SemaphoreType.DMA((nop, N_DEV // 2)), pltpu.SemaphoreType.DMA((nop, N_DEV // 2))])


def _pair_sum(name, grad, other):
    nchip, _, r, c = grad.shape
    tr = _pick(r, max(SUBLANES, TILE["sum_bytes"] // (8 * c)), SUBLANES)

    def body(g_ref, o_ref, s_ref):
        mine = jnp.where(lax.axis_index("c") == 0, g_ref[0, 0], g_ref[0, 1])
        s_ref[0] = (mine + o_ref[0]).astype(s_ref.dtype)

    return pl.pallas_call(
        body, name=name, grid=(nchip, r // tr),
        in_specs=[pl.BlockSpec((1, 2, tr, c), lambda q, t: (q, 0, t, 0)), pl.BlockSpec((1, tr, c), lambda q, t: (q, t, 0))],
        out_specs=pl.BlockSpec((1, tr, c), lambda q, t: (q, t, 0)),
        out_shape=_sds((nchip, r, c), BF16),
        compiler_params=_cparams("parallel", "parallel"),
    )(grad, other)


def _chip_exchange_plan(sums):
    nop = len(sums)

    def copies(x_refs, o_refs, sems, arriving):
        send_sems, recv_sems, local_sems = sems
        x, y, c = _mesh_pos()
        mine = 2 * x + y
        out = []
        for i in range(nop):
            for j, (px, py) in enumerate(_other_chips(x, y)):
                theirs = 2 * px + py
                src, dst = (mine, theirs) if arriving else (theirs, mine)
                out.append(_remote(x_refs[i].at[src], o_refs[i].at[dst], send_sems.at[i, j], recv_sems.at[i, j],
                                   (px, py, c)))
        if not arriving:
            out += [pltpu.make_async_copy(x_refs[i].at[mine], o_refs[i].at[mine], local_sems.at[i]) for i in range(nop)]
        return out

    def start(x_refs, o_refs, sems):
        for cp in copies(x_refs, o_refs, sems, False):
            cp.start()

    def finish(x_refs, o_refs, sems):
        for cp in copies(x_refs, o_refs, sems, True):
            cp.wait_recv()
        mine = copies(x_refs, o_refs, sems, False)
        for cp in mine[:3 * nop]:
            cp.wait_send()
        for cp in mine[3 * nop:]:
            cp.wait()

    return _Exchange(list(sums), [_sds(s.shape, s.dtype) for s in sums],
                     [pltpu.SemaphoreType.DMA((nop, 3)), pltpu.SemaphoreType.DMA((nop, 3)), pltpu.SemaphoreType.DMA((nop,))],
                     start, finish)


def _part_rows(npart, r, c):
    return _pick(r, max(SUBLANES, TILE["sum_bytes"] // (4 * npart * c)), SUBLANES)


def _adamw(name, parts, w, m, v):
    npart, r, c = parts.shape
    lead = len(w.shape) - 2
    tr = _part_rows(npart, r, c)
    c1 = 1.0 - ADAM_B1 ** ADAM_STEP
    c2 = 1.0 - ADAM_B2 ** ADAM_STEP
    at = (0,) * lead + (slice(None), slice(None))

    def body(p_ref, w_ref, m_ref, v_ref, g_ref, d_ref, nm_ref, nv_ref):
        g = p_ref[0].astype(F32)
        for k in range(1, npart):
            g = g + p_ref[k].astype(F32)
        nm = ADAM_B1 * m_ref[at] + (1.0 - ADAM_B1) * g
        nv = ADAM_B2 * v_ref[at] + (1.0 - ADAM_B2) * (g * g)
        g_ref[at] = g
        nm_ref[at] = nm
        nv_ref[at] = nv
        d_ref[at] = -ADAM_LR * ((nm / c1) / (jnp.sqrt(nv / c2) + ADAM_EPS) + ADAM_WD * w_ref[at])

    row = pl.BlockSpec((1,) * lead + (tr, c), lambda i: (0,) * lead + (i, 0))
    return pl.pallas_call(
        body, name=name, grid=(r // tr,),
        in_specs=[pl.BlockSpec((npart, tr, c), lambda i: (0, i, 0)), row, row, row],
        out_specs=[row] * 4,
        out_shape=[_sds(w.shape, F32)] * 4,
        compiler_params=_cparams("parallel"),
    )(parts, w, m, v)


def _pack(pieces, row_mult, lead=()):
    nl = len(lead)
    flat, spans, off = [], [], 0
    for p in pieces:
        p = p.reshape(lead + (-1,))
        size = p.shape[-1]
        padded = -(-size // PACK_W) * PACK_W
        flat.append(jnp.pad(p, [(0, 0)] * nl + [(0, padded - size)]))
        spans.append((off, size))
        off += padded
    rows = -(-(off // PACK_W) // row_mult) * row_mult
    if rows * PACK_W > off:
        flat.append(jnp.zeros(lead + (rows * PACK_W - off,), flat[0].dtype))
    return jnp.concatenate(flat, axis=-1).reshape(lead + (rows, PACK_W)), spans


def _unpack(buf, spans, shapes, lead=0):
    flat = buf.reshape(buf.shape[:lead] + (-1,))
    return [flat[..., o:o + s].reshape(buf.shape[:lead] + tuple(shape)) for (o, s), shape in zip(spans, shapes)]


def _block_diag(rows_gh, groups):
    gh, p = rows_gh.shape
    own = (jnp.arange(gh)[:, None] // (gh // groups) == jnp.arange(groups)[None, :]).astype(rows_gh.dtype)
    return (own[:, :, None] * rows_gh[:, None, :]).reshape(gh, groups * p)


def _block_diag_take(dense, groups):
    gh = dense.shape[0]
    p = dense.shape[1] // groups
    own = (jnp.arange(gh)[:, None] // (gh // groups) == jnp.arange(groups)[None, :]).astype(dense.dtype)
    return jnp.sum(dense.reshape(gh, groups, p) * own[:, :, None], axis=1)


FFN1 = ("ffn1_w1", "ffn1_w3", "ffn1_w2")
MIXER = ("w_in", "ssm_glu_w", "w_out")
FFN2 = ("ffn2_w1", "ffn2_w3", "ffn2_w2")
BIG = FFN1 + MIXER + FFN2
COL_SHARDED = ("ffn1_w1", "ffn1_w3", "w_in", "ffn2_w1", "ffn2_w3", "conv_w")
SMALL = ("norm_ffn1", "norm_mix", "conv_b", "conv_ln_g", "conv_ln_b", "conv_out_g", "ssm_A_re", "ssm_A_im",
         "ssm_log_dt", "ssm_B_re", "ssm_B_im", "ssm_C_re", "ssm_C_im", "ssm_D", "ssm_glu_b", "ssm_out_g",
         "norm_ffn2", "norm_final")
WEIGHTS = ("norm_ffn1", "ffn1_w1", "ffn1_w3", "ffn1_w2", "norm_mix", "w_in", "conv_w", "conv_b", "conv_ln_g",
           "conv_ln_b", "conv_out_g", "ssm_A_re", "ssm_A_im", "ssm_log_dt", "ssm_B_re", "ssm_B_im", "ssm_C_re",
           "ssm_C_im", "ssm_D", "ssm_glu_w", "ssm_glu_b", "ssm_out_g", "w_out", "norm_ffn2", "ffn2_w1", "ffn2_w3",
           "ffn2_w2", "norm_final")


def _ffn_backward(tag, dxo, x, g, w1, w3, w2, saved, exchange=None, reduce_plan=None):
    a, b, h = saved
    (da, db, hid, dxh), got = _ffn_bwd_hidden(tag + "_bwd_hidden", dxo, a, b, w2, exchange=exchange)
    dws = [_mm_tn(tag + "_dw1", da, h), _mm_tn(tag + "_dw3", db, h), _mm_tn(tag + "_dw2", hid, dxh)]
    f = a.shape[1]
    (dx, dg), reduced = _dx_rms_bwd(tag + "_bwd_dx", [(da, f, 0, w1, f, 0), (db, f, 0, w3, f, 0)], dxo, x, g,
                                    exchange=reduce_plan(dws) if reduce_plan else None)
    return (dx, dg, dws), got, reduced


def _reduce_in_chip(tag, names, grads):
    send = [g.reshape((N_DEV, -1) + g.shape[1:]) for g in grads]
    from_core = _exchange_sibling("exchange_core_" + tag, send)
    return _chip_exchange_plan([_pair_sum("pair_sum_" + k, s.reshape((N_DEV // 2, 2) + s.shape[1:]), o)
                                for k, s, o in zip(names, send, from_core)])


def kernel(x, norm_ffn1, ffn1_w1, ffn1_w3, ffn1_w2, norm_mix, w_in, conv_w, conv_b, conv_ln_g, conv_ln_b, conv_out_g, ssm_A_re, ssm_A_im, ssm_log_dt, ssm_B_re, ssm_B_im, ssm_C_re, ssm_C_im, ssm_D, ssm_glu_w, ssm_glu_b, ssm_out_g, w_out, norm_ffn2, ffn2_w1, ffn2_w3, ffn2_w2, norm_final, loss_target, m_norm_ffn1, m_ffn1_w1, m_ffn1_w3, m_ffn1_w2, m_norm_mix, m_w_in, m_conv_w, m_conv_b, m_conv_ln_g, m_conv_ln_b, m_conv_out_g, m_ssm_A_re, m_ssm_A_im, m_ssm_log_dt, m_ssm_B_re, m_ssm_B_im, m_ssm_C_re, m_ssm_C_im, m_ssm_D, m_ssm_glu_w, m_ssm_glu_b, m_ssm_out_g, m_w_out, m_norm_ffn2, m_ffn2_w1, m_ffn2_w3, m_ffn2_w2, m_norm_final, v_norm_ffn1, v_ffn1_w1, v_ffn1_w3, v_ffn1_w2, v_norm_mix, v_w_in, v_conv_w, v_conv_b, v_conv_ln_g, v_conv_ln_b, v_conv_out_g, v_ssm_A_re, v_ssm_A_im, v_ssm_log_dt, v_ssm_B_re, v_ssm_B_im, v_ssm_C_re, v_ssm_C_im, v_ssm_D, v_ssm_glu_w, v_ssm_glu_b, v_ssm_out_g, v_w_out, v_norm_ffn2, v_ffn2_w1, v_ffn2_w3, v_ffn2_w2, v_norm_final):
    args = dict(locals())
    wt = {n: args[n] for n in WEIGHTS}
    mom = {n: args["m_" + n] for n in WEIGHTS}
    var = {n: args["v_" + n] for n in WEIGHTS}

    bsz, seq, d = x.shape
    n = bsz * seq
    c = conv_b.shape[-1]
    groups = c // SSM_GROUP
    gp = groups * SSM_STATE
    u_b = 2

    shard = {k: (wt[k][0].T if k in COL_SHARDED else wt[k][0]).astype(BF16) for k in BIG}
    gathered = _run_exchange("gather_weights_ffn1", _gather_plan([shard[k] for k in FFN1]))
    full = {k: g.reshape(-1, g.shape[-1]) for k, g in zip(FFN1, gathered)}
    gather_rest = _gather_plan([shard[k] for k in MIXER + FFN2] + [wt["conv_w"][0]])

    vec = lambda k: wt[k].reshape(1, -1)
    g_ffn1, g_mix, g_ffn2, g_fin = vec("norm_ffn1"), vec("norm_mix"), vec("norm_ffn2"), vec("norm_final")
    cb, lng, lnb, cog = vec("conv_b"), vec("conv_ln_g"), vec("conv_ln_b"), vec("conv_out_g")
    d_skip, glu_b, sog = vec("ssm_D"), vec("ssm_glu_b"), vec("ssm_out_g")

    a_re, a_im = wt["ssm_A_re"][0], wt["ssm_A_im"][0]
    log_dt = wt["ssm_log_dt"][0].reshape(groups, 1)
    bt_re = wt["ssm_B_re"][0].transpose(0, 2, 1).reshape(groups * SSM_GROUP, SSM_STATE)
    bt_im = wt["ssm_B_im"][0].transpose(0, 2, 1).reshape(groups * SSM_GROUP, SSM_STATE)
    c_re = wt["ssm_C_re"][0].reshape(groups * SSM_GROUP, SSM_STATE)
    c_im = wt["ssm_C_im"][0].reshape(groups * SSM_GROUP, SSM_STATE)
    per_chan = lambda t: jnp.repeat(t, SSM_GROUP, axis=0)
    ssm_prim = (a_re, a_im, log_dt, per_chan(a_re), per_chan(a_im), per_chan(jnp.broadcast_to(log_dt, a_re.shape)),
                bt_re, bt_im)
    pw_r, pw_i, bb_r, bb_i = _ssm_prep("ssm_prep", ssm_prim)
    tab_f = _scan_tables(pw_r, pw_i, False)
    tab_b = _scan_tables(pw_r, pw_i, True)
    bbd = jnp.concatenate([_block_diag(bb_r, groups), _block_diag(bb_i, groups)], axis=1).astype(BF16)
    cdt = jnp.concatenate([_block_diag(c_re, groups), -_block_diag(c_im, groups)], axis=1).astype(BF16)

    x0 = x.reshape(n, d)
    (x1, *ffn1_saved), gathered = _ffn_fwd("ffn1_fwd", x0, g_ffn1, full["ffn1_w1"], full["ffn1_w3"], full["ffn1_w2"],
                                           exchange=gather_rest)
    full.update({k: g.reshape(-1, g.shape[-1]) for k, g in zip(MIXER + FFN2, gathered)})
    conv_w_full = gathered[-1].transpose(1, 0, 2).reshape(CONV_WIDTH, c)
    conv_w_pad = jnp.pad(conv_w_full, ((0, CONV_HALO - CONV_WIDTH), (0, 0)))
    (proj,), h2 = _rms_mm("mix_in", x1, g_mix, [full["w_in"]], F32)
    proj3 = proj.reshape(bsz, seq, 3 * c)
    an3, cv3 = _conv_fwd("conv_fwd", proj3, conv_w_pad, cb, lng, lnb, cog)
    an = an3.reshape(n, c)
    xs3, cx3 = _scan_fwd("scan_fwd", tab_f, proj3, u_b, bbd, cdt)
    xs = xs3.reshape(n, 2 * gp)
    y, sn = _ssm_out_fwd("ssm_out_fwd", cx3.reshape(n, c), proj, u_b, d_skip, full["ssm_glu_w"], glu_b, sog)
    w_o = full["w_out"]
    x2 = _row_mm("mix_out", [(an, c, 0, w_o, c, 0, False), (sn, c, 0, w_o, c, 1, False)], d, F32, add=x1)
    (x3, *ffn2_saved), _ = _ffn_fwd("ffn2_fwd", x2, g_ffn2, full["ffn2_w1"], full["ffn2_w3"], full["ffn2_w2"])
    dx3, loss_tile, d_gfin = _loss_head("loss_head", x3, g_fin, loss_target.reshape(n, d))
    loss = lax.psum(loss_tile[0, 0], MESH_AXES)

    grads, from_chips = {}, {}
    (dx2, grads["norm_ffn2"], dws), _, _ = _ffn_backward(
        "ffn2", dx3, x2, g_ffn2, full["ffn2_w1"], full["ffn2_w3"], full["ffn2_w2"], ffn2_saved)
    reduce_ffn2 = _reduce_in_chip("ffn2", FFN2, dws)

    dmix = _row_mm("mix_out_bwd", [(dx2, d, 0, w_o, 2 * c, 0, True)], 2 * c, F32)
    grads["w_out"] = jnp.concatenate([_mm_tn("dw_out_a", an, dx2), _mm_tn("dw_out_s", sn, dx2)], axis=0)

    dy, du_skip, grads["ssm_glu_w"], grads["ssm_glu_b"], grads["ssm_out_g"], grads["ssm_D"] = _ssm_out_bwd(
        "ssm_out_bwd", dmix, 1, y, proj, u_b, d_skip, full["ssm_glu_w"], glu_b, sog)
    (lam3, du3, dab_r, dab_i), got = _scan_bwd("scan_bwd", tab_b, dy.reshape(bsz, seq, c), xs3,
                                               du_skip.reshape(bsz, seq, c), bbd, cdt, exchange=reduce_ffn2)
    from_chips.update(zip(FFN2, got))
    lam, du = lam3.reshape(n, 2 * gp), du3.reshape(n, c)
    d_bbd = _mm_tn("ssm_dbb", proj, lam, a_cols=(u_b * c, c))
    d_cdt = _mm_tn("ssm_dc", dy, xs)
    d_are, d_aim, d_ldt, d_btr, d_bti = _ssm_param_grads(
        "ssm_param_grads", ssm_prim,
        dab_r.reshape(SUBLANES, groups, SSM_STATE), dab_i.reshape(SUBLANES, groups, SSM_STATE),
        _block_diag_take(d_bbd[:, :gp], groups), _block_diag_take(d_bbd[:, gp:], groups))
    grads["ssm_A_re"], grads["ssm_A_im"], grads["ssm_log_dt"] = d_are, d_aim, d_ldt
    grads["ssm_B_re"] = d_btr.reshape(groups, SSM_GROUP, SSM_STATE).transpose(0, 2, 1)
    grads["ssm_B_im"] = d_bti.reshape(groups, SSM_GROUP, SSM_STATE).transpose(0, 2, 1)
    grads["ssm_C_re"] = _block_diag_take(d_cdt[:, :gp], groups)
    grads["ssm_C_im"] = -_block_diag_take(d_cdt[:, gp:], groups)

    dconv3, d_cw, grads["conv_b"], grads["conv_ln_g"], grads["conv_ln_b"], grads["conv_out_g"] = _conv_bwd(
        "conv_bwd", dmix.reshape(bsz, seq, 2 * c), proj3, cv3, conv_w_pad, lng, lnb, cog)
    dconv = dconv3.reshape(n, 2 * c)
    grads["conv_w"] = d_cw[:CONV_WIDTH]
    grads["w_in"] = jnp.concatenate([_mm_tn("dw_in_conv", dconv, h2), _mm_tn("dw_in_ssm", du, h2)], axis=0)
    w_i = full["w_in"]
    (dx1, grads["norm_mix"]), _ = _dx_rms_bwd("mix_in_bwd", [(dconv, 2 * c, 0, w_i, 2 * c, 0), (du, c, 0, w_i, c, 2)],
                                              dx2, x1, g_mix)
    reduce_mixer = _reduce_in_chip("mixer", MIXER, [grads[k] for k in MIXER])

    (dx0, grads["norm_ffn1"], _), got, reduced = _ffn_backward(
        "ffn1", dx1, x0, g_ffn1, full["ffn1_w1"], full["ffn1_w3"], full["ffn1_w2"], ffn1_saved,
        exchange=reduce_mixer, reduce_plan=functools.partial(_reduce_in_chip, "ffn1", FFN1))
    from_chips.update(zip(MIXER, got))
    from_chips.update(zip(FFN1, reduced))
    grads["norm_final"] = d_gfin

    res = {}
    for k in BIG:
        parts = from_chips[k]
        if k in COL_SHARDED:
            swap = lambda t: jnp.swapaxes(t, -1, -2)
            res[k] = [swap(t) for t in _adamw("adamw_" + k, parts, swap(wt[k]), swap(mom[k]), swap(var[k]))]
        else:
            res[k] = _adamw("adamw_" + k, parts, wt[k], mom[k], var[k])

    small_names = SMALL + ("conv_w",)
    no_state = jnp.zeros_like(grads["conv_w"])
    part, spans = _pack([grads[k] for k in small_names], SUBLANES)
    (all_parts,) = _run_exchange("gather_small_grads", _gather_plan([part]))
    w_pk, _ = _pack([wt[k] for k in SMALL] + [no_state], SUBLANES)
    m_pk, _ = _pack([mom[k] for k in SMALL] + [no_state], SUBLANES)
    v_pk, _ = _pack([var[k] for k in SMALL] + [no_state], SUBLANES)
    small_out = _adamw("adamw_replicated", all_parts, w_pk, m_pk, v_pk)
    small_shapes = [wt[k].shape for k in SMALL] + [grads["conv_w"].shape]
    small_res = [dict(zip(small_names, _unpack(o, spans, small_shapes))) for o in small_out]
    x_pos, y_pos, c_pos = (lax.axis_index(a) for a in MESH_AXES)
    cw_cols = c // N_DEV
    own_cw = lax.dynamic_slice_in_dim(small_res[0]["conv_w"], (4 * x_pos + 2 * y_pos + c_pos) * cw_cols, cw_cols, axis=1)
    res["conv_w"] = _adamw("adamw_conv_w", own_cw[None], wt["conv_w"], mom["conv_w"], var["conv_w"])

    outs = [loss, dx0.reshape(bsz, seq, d)]
    for kind in range(4):
        outs += [res[k][kind] if k in res else small_res[kind][k] for k in WEIGHTS]
    return tuple(outs)
```

```python
import collections
import functools
import math

import jax
import jax.numpy as jnp
from jax import lax
from jax.experimental import pallas as pl
from jax.experimental.pallas import tpu as pltpu

F32 = jnp.float32
BF16 = jnp.bfloat16

EPS = 1e-6
FFN_RES = 0.5
CONV_WIDTH = 31
CONV_HALO = 32
SSM_GROUP = 16
SSM_STATE = 64
ADAM_LR, ADAM_B1, ADAM_B2, ADAM_EPS, ADAM_WD, ADAM_STEP = 0.001, 0.9, 0.999, 1e-08, 0.01, 10

N_DEV = 8
MESH_AXES = ("x", "y", "c")
SUBLANES = 8
LANES = 128
PACK_W = 1024
V7X_VMEM_BYTES = 64 * 2**20
VMEM_LIMIT = V7X_VMEM_BYTES - 8 * 2**20

TILE = dict(row=256, mm_bytes=8 * 2**20, up_m=1024, up_n=256, wide_n=2048, conv_t=512, scan_t=256, scan_w=512,
            sum_bytes=4 * 2**20)

_GELU_K = math.sqrt(2.0 / math.pi)
_GELU_C = 0.044715


def _pick(n, target, mult):
    best = None
    for t in range(mult, min(n, target) + 1, mult):
        if n % t == 0:
            best = t
    return n if best is None else best


def _cparams(*sem):
    return pltpu.CompilerParams(dimension_semantics=sem, vmem_limit_bytes=VMEM_LIMIT)


def _sds(shape, dtype):
    return jax.ShapeDtypeStruct(shape, dtype)


def _call(name, body, grid, in_specs, out_specs, out_shape, operands, sem, scratch=(), exchange=None):
    if exchange is None:
        res = pl.pallas_call(body, name=name, grid=grid, in_specs=list(in_specs), out_specs=list(out_specs),
                             out_shape=list(out_shape), scratch_shapes=list(scratch),
                             compiler_params=_cparams(*sem))(*operands)
        return list(res), None
    n_in, n_out, n_scr = len(in_specs), len(out_specs), len(scratch)
    n_xin, n_xout = len(exchange.operands), len(exchange.out_shapes)
    hbm = pl.BlockSpec(memory_space=pltpu.HBM)

    def with_exchange(*refs):
        cuts, pos = [], 0
        for size in (n_in, n_xin, n_out, n_xout, n_scr):
            cuts.append(refs[pos:pos + size])
            pos += size
        ins, x_in, outs, x_out, scr = cuts
        sems = refs[pos:]
        ids = [pl.program_id(axis) for axis in range(len(grid))]
        first = functools.reduce(lambda p, q: p & q, [i == 0 for i in ids])
        last = functools.reduce(lambda p, q: p & q, [i == g - 1 for i, g in zip(ids, grid)])

        @pl.when(first)
        def _():
            exchange.start(x_in, x_out, sems)

        body(*ins, *outs, *scr)

        @pl.when(last)
        def _():
            exchange.finish(x_in, x_out, sems)

    res = pl.pallas_call(
        with_exchange, name=name, grid=grid, in_specs=list(in_specs) + [hbm] * n_xin,
        out_specs=list(out_specs) + [hbm] * n_xout, out_shape=list(out_shape) + list(exchange.out_shapes),
        scratch_shapes=list(scratch) + list(exchange.scratch),
        compiler_params=_cparams(*["arbitrary"] * len(grid)))(*operands, *exchange.operands)
    return list(res[:n_out]), list(res[n_out:])


def _dot(a, b):
    return jnp.dot(a, b, preferred_element_type=F32)


def _dot_nt(a, b):
    return lax.dot_general(a, b, (((1,), (1,)), ((), ())), preferred_element_type=F32)


def _dot_tn(a, b):
    return lax.dot_general(a, b, (((0,), (0,)), ((), ())), preferred_element_type=F32)


def _sigmoid(x):
    return 0.5 * jnp.tanh(0.5 * x) + 0.5


def _rms_stats(x):
    r = lax.rsqrt(jnp.mean(x * x, axis=-1, keepdims=True) + EPS)
    return r, x * r


def _rms_bwd(x, g, dy):
    r, xh = _rms_stats(x)
    dxh = dy * g
    dx = r * (dxh - xh * jnp.mean(dxh * xh, axis=-1, keepdims=True))
    return dx, jnp.sum(dy * xh, axis=0, keepdims=True)


def _rms_mm(name, x, g, ws, out_dtype):
    n, d = x.shape
    f = ws[0].shape[0]
    nw = len(ws)
    tm, tn = _pick(n, TILE["up_m"], 16), _pick(f, TILE["wide_n"], LANES)

    def body(x_ref, g_ref, *refs):
        w_refs, o_refs, h_ref = refs[:nw], refs[nw:2 * nw], refs[2 * nw]

        @pl.when(pl.program_id(1) == 0)
        def _():
            _, xh = _rms_stats(x_ref[...])
            h_ref[...] = (xh * g_ref[...]).astype(BF16)

        h = h_ref[...]
        for w_ref, o_ref in zip(w_refs, o_refs):
            o_ref[...] = _dot_nt(h, w_ref[...]).astype(o_ref.dtype)

    outs = pl.pallas_call(
        body, name=name, grid=(n // tm, f // tn),
        in_specs=[pl.BlockSpec((tm, d), lambda i, j: (i, 0)), pl.BlockSpec((1, d), lambda i, j: (0, 0))]
        + [pl.BlockSpec((tn, d), lambda i, j: (j, 0))] * nw,
        out_specs=[pl.BlockSpec((tm, tn), lambda i, j: (i, j))] * nw + [pl.BlockSpec((tm, d), lambda i, j: (i, 0))],
        out_shape=[_sds((n, f), out_dtype)] * nw + [_sds((n, d), BF16)],
        compiler_params=_cparams("parallel", "arbitrary"),
    )(x, g, *ws)
    return outs[:nw], outs[nw]


def _row_chunks(rows):
    step = _pick(rows, TILE["row"], SUBLANES)
    return [pl.ds(r0, step) for r0 in range(0, rows, step)]


def _ffn_fwd(name, x, g, w1t, w3t, w2, exchange=None):
    n, d = x.shape
    f = w2.shape[0]
    tm, tn = _pick(n, TILE["up_m"], 16), _pick(f, TILE["up_n"], LANES)
    nj = f // tn

    def body(x_ref, g_ref, w1_ref, w3_ref, w2_ref, o_ref, a_ref, b_ref, h_ref, acc_ref):
        j = pl.program_id(1)

        @pl.when(j == 0)
        def _():
            for rows in _row_chunks(tm):
                _, xh = _rms_stats(x_ref[rows, :])
                h_ref[rows, :] = (xh * g_ref[...]).astype(BF16)
            acc_ref[...] = jnp.zeros_like(acc_ref)

        h = h_ref[...]
        av, bv = _dot_nt(h, w1_ref[...]), _dot_nt(h, w3_ref[...])
        a_ref[...] = av.astype(BF16)
        b_ref[...] = bv.astype(BF16)
        acc_ref[...] += _dot((av * _sigmoid(av) * bv).astype(BF16), w2_ref[...])

        @pl.when(j == nj - 1)
        def _():
            o_ref[...] = x_ref[...] + FFN_RES * acc_ref[...]

    row = pl.BlockSpec((tm, d), lambda i, j: (i, 0))
    tile = pl.BlockSpec((tm, tn), lambda i, j: (i, j))
    wblk = pl.BlockSpec((tn, d), lambda i, j: (j, 0))
    return _call(
        name, body, (n // tm, nj),
        [row, pl.BlockSpec((1, d), lambda i, j: (0, 0)), wblk, wblk, wblk], [row, tile, tile, row],
        [_sds((n, d), F32), _sds((n, f), BF16), _sds((n, f), BF16), _sds((n, d), BF16)],
        (x, g, w1t, w3t, w2), ("parallel", "arbitrary"), scratch=[pltpu.VMEM((tm, d), F32)], exchange=exchange)


def _ffn_bwd_hidden(name, dxo, a, b, w2, exchange=None):
    n, d = dxo.shape
    f = a.shape[1]
    tm, tn = _pick(n, TILE["up_m"], 16), _pick(f, TILE["up_n"], LANES)

    def body(dx_ref, a_ref, b_ref, w_ref, da_ref, db_ref, hid_ref, dxh_ref):
        @pl.when(pl.program_id(1) == 0)
        def _():
            dxh_ref[...] = (FFN_RES * dx_ref[...]).astype(BF16)

        dhid = _dot_nt(dxh_ref[...], w_ref[...])
        av, bv = a_ref[...].astype(F32), b_ref[...].astype(F32)
        sig = _sigmoid(av)
        silu = av * sig
        da_ref[...] = (dhid * bv * (sig * (1.0 + av * (1.0 - sig)))).astype(BF16)
        db_ref[...] = (dhid * silu).astype(BF16)
        hid_ref[...] = (silu * bv).astype(BF16)

    tile = pl.BlockSpec((tm, tn), lambda i, j: (i, j))
    return _call(
        name, body, (n // tm, f // tn),
        [pl.BlockSpec((tm, d), lambda i, j: (i, 0)), tile, tile, pl.BlockSpec((tn, d), lambda i, j: (j, 0))],
        [tile, tile, tile, pl.BlockSpec((tm, d), lambda i, j: (i, 0))],
        [_sds((n, f), BF16)] * 3 + [_sds((n, d), BF16)], (dxo, a, b, w2), ("parallel", "arbitrary"), exchange=exchange)


def _loss_head(name, x, g, target):
    n, d = x.shape
    tm = _pick(n, TILE["row"], SUBLANES)

    def body(x_ref, g_ref, t_ref, dx_ref, loss_ref, dg_ref):
        @pl.when(pl.program_id(0) == 0)
        def _():
            loss_ref[...] = jnp.zeros_like(loss_ref)
            dg_ref[...] = jnp.zeros_like(dg_ref)

        xv, gv = x_ref[...], g_ref[...]
        r, xh = _rms_stats(xv)
        err = xh * gv - t_ref[...]
        loss_ref[...] += 0.5 * jnp.sum(jnp.mean(err * err, axis=-1, keepdims=True))
        dy = err * (1.0 / d)
        dxh = dy * gv
        dx_ref[...] = r * (dxh - xh * jnp.mean(dxh * xh, axis=-1, keepdims=True))
        dg_ref[...] += jnp.sum(dy * xh, axis=0, keepdims=True)

    return pl.pallas_call(
        body, name=name, grid=(n // tm,),
        in_specs=[pl.BlockSpec((tm, d), lambda i: (i, 0)), pl.BlockSpec((1, d), lambda i: (0, 0)),
                  pl.BlockSpec((tm, d), lambda i: (i, 0))],
        out_specs=[pl.BlockSpec((tm, d), lambda i: (i, 0)), pl.BlockSpec((SUBLANES, LANES), lambda i: (0, 0)),
                   pl.BlockSpec((1, d), lambda i: (0, 0))],
        out_shape=[_sds((n, d), F32), _sds((SUBLANES, LANES), F32), _sds((1, d), F32)],
        compiler_params=_cparams("arbitrary"),
    )(x, g, target)


def _dx_rms_bwd(name, pairs, dxo, x, g, exchange=None):
    n, dm = x.shape
    tm = _pick(n, TILE["row"], 16)
    npair = len(pairs)

    def body(*refs):
        d_refs, w_refs = refs[:npair], refs[npair:2 * npair]
        dxo_ref, x_ref, g_ref, dx_ref, dg_ref = refs[2 * npair:]

        @pl.when(pl.program_id(0) == 0)
        def _():
            dg_ref[...] = jnp.zeros_like(dg_ref)

        dh = None
        for d_ref, w_ref in zip(d_refs, w_refs):
            t = _dot(d_ref[...].astype(BF16), w_ref[...])
            dh = t if dh is None else dh + t
        dx, dg = _rms_bwd(x_ref[...], g_ref[...], dh)
        dx_ref[...] = dxo_ref[...] + dx
        dg_ref[...] += dg

    row = pl.BlockSpec((tm, dm), lambda i: (i, 0))
    d_specs = [pl.BlockSpec((tm, p[1]), functools.partial(lambda i, cb: (i, cb), cb=p[2])) for p in pairs]
    w_specs = [pl.BlockSpec((p[4], dm), functools.partial(lambda i, rb: (rb, 0), rb=p[5])) for p in pairs]
    return _call(
        name, body, (n // tm,), d_specs + w_specs + [row, row, pl.BlockSpec((1, dm), lambda i: (0, 0))],
        [row, pl.BlockSpec((1, dm), lambda i: (0, 0))], [_sds((n, dm), F32), _sds((1, dm), F32)],
        (*[p[0] for p in pairs], *[p[3] for p in pairs], dxo, x, g), ("arbitrary",), exchange=exchange)


def _mm_tn(name, a, b, a_cols=None, b_cols=None):
    n = a.shape[0]
    a0, ma = a_cols if a_cols else (0, a.shape[1])
    b0, mb = b_cols if b_cols else (0, b.shape[1])
    assert a0 % ma == 0 and b0 % mb == 0
    ab, bb = a0 // ma, b0 // mb
    tk = _pick(n, TILE["mm_bytes"] // (ma * a.dtype.itemsize + mb * b.dtype.itemsize), 16)

    def body(a_ref, b_ref, o_ref):
        @pl.when(pl.program_id(0) == 0)
        def _():
            o_ref[...] = jnp.zeros_like(o_ref)

        o_ref[...] += _dot_tn(a_ref[...].astype(BF16), b_ref[...].astype(BF16))

    return pl.pallas_call(
        body, name=name, grid=(n // tk,),
        in_specs=[pl.BlockSpec((tk, ma), lambda k: (k, ab)), pl.BlockSpec((tk, mb), lambda k: (k, bb))],
        out_specs=pl.BlockSpec((ma, mb), lambda k: (0, 0)),
        out_shape=_sds((ma, mb), F32),
        compiler_params=_cparams("arbitrary"),
    )(a, b)


def _row_mm(name, pairs, out_w, out_dtype, add=None):
    n = pairs[0][0].shape[0]
    tm = _pick(n, TILE["row"], 16)
    npair = len(pairs)

    def body(*refs):
        a_refs, w_refs = refs[:npair], refs[npair:2 * npair]
        add_ref = refs[2 * npair] if add is not None else None
        o_ref = refs[-1]
        acc = None
        for a_ref, w_ref, p in zip(a_refs, w_refs, pairs):
            av = a_ref[...].astype(BF16)
            t = _dot_nt(av, w_ref[...]) if p[6] else _dot(av, w_ref[...])
            acc = t if acc is None else acc + t
        if add_ref is not None:
            acc = acc + add_ref[...].astype(F32)
        o_ref[...] = acc.astype(o_ref.dtype)

    a_specs = [pl.BlockSpec((tm, p[1]), functools.partial(lambda i, cb: (i, cb), cb=p[2])) for p in pairs]
    w_specs = [pl.BlockSpec((p[4], p[3].shape[1]), functools.partial(lambda i, rb: (rb, 0), rb=p[5])) for p in pairs]
    add_specs = [pl.BlockSpec((tm, out_w), lambda i: (i, 0))] if add is not None else []
    return pl.pallas_call(
        body, name=name, grid=(n // tm,),
        in_specs=a_specs + w_specs + add_specs,
        out_specs=pl.BlockSpec((tm, out_w), lambda i: (i, 0)),
        out_shape=_sds((n, out_w), out_dtype),
        compiler_params=_cparams("parallel"),
    )(*[p[0] for p in pairs], *[p[3] for p in pairs], *([add] if add is not None else []))


def _conv_post(c, ln_g, ln_b, out_g):
    mu = jnp.mean(c, axis=-1, keepdims=True)
    xc = c - mu
    rstd = lax.rsqrt(jnp.mean(xc * xc, axis=-1, keepdims=True) + EPS)
    nrm = xc * rstd
    l = nrm * ln_g + ln_b
    sig = _sigmoid(l)
    s = l * sig
    r, sh = _rms_stats(s)
    return sh * out_g, (rstd, nrm, l, sig, r, sh)


def _tap_groups(first):
    groups = []
    for r in range(SUBLANES):
        taps = [(s - r, s - first) for s in range(first, first + CONV_WIDTH) if s % SUBLANES == r]
        if taps:
            groups.append((r, taps))
    return groups


def _conv_taps(a_ref, w_ref, b_ref, first, rows, flip=False):
    acc = None
    for r, taps in _tap_groups(first):
        ext = rows if r == 0 else rows + SUBLANES
        part = None
        for base, k in taps:
            kk = CONV_WIDTH - 1 - k if flip else k
            t = w_ref[kk:kk + 1, :] * a_ref[pl.ds(base, ext), :]
            part = t if part is None else part + t
        if r:
            b_ref[...] = part
            part = b_ref[pl.ds(r, rows), :]
        acc = part if acc is None else acc + part
    return acc


def _conv_post_bwd(cv, dout, ln_g, ln_b, out_g):
    _, (rstd, nrm, l, sig, r, sh) = _conv_post(cv, ln_g, ln_b, out_g)
    dsh = dout * out_g
    ds = r * (dsh - sh * jnp.mean(dsh * sh, axis=-1, keepdims=True))
    dl = ds * (sig * (1.0 + l * (1.0 - sig)))
    dn = dl * ln_g
    dc = rstd * (dn - jnp.mean(dn, axis=-1, keepdims=True) - nrm * jnp.mean(dn * nrm, axis=-1, keepdims=True))
    col_sum = lambda t: jnp.sum(t, axis=0, keepdims=True)
    return dc, col_sum(dout * sh), col_sum(dl * nrm), col_sum(dl)


def _conv_fwd(name, proj3, conv_w, conv_b, ln_g, ln_b, out_g):
    bsz, seq, _ = proj3.shape
    c = conv_w.shape[1]
    tt = _pick(seq, TILE["conv_t"], CONV_HALO)
    hb = tt // CONV_HALO
    first = CONV_HALO - (CONV_WIDTH - 1)

    def body(v_ref, g_ref, vp_ref, gp_ref, w_ref, cb_ref, lg_ref, lb_ref, og_ref, o_ref, cv_ref, a_ref, b_ref):
        keep = (pl.program_id(1) > 0).astype(F32)
        a_ref[pl.ds(0, CONV_HALO), :] = keep * vp_ref[0] * _sigmoid(gp_ref[0])
        a_ref[pl.ds(CONV_HALO, tt), :] = v_ref[0] * _sigmoid(g_ref[0])
        cv = _conv_taps(a_ref, w_ref, b_ref, first, tt) + cb_ref[...]
        cv_ref[0] = cv
        out, _ = _conv_post(cv, lg_ref[...], lb_ref[...], og_ref[...])
        o_ref[0] = out.astype(BF16)

    vec = pl.BlockSpec((1, c), lambda b, i: (0, 0))
    prev = lambda col: pl.BlockSpec((1, CONV_HALO, c), lambda b, i: (b, jnp.maximum(i * hb - 1, 0), col))
    tile = pl.BlockSpec((1, tt, c), lambda b, i: (b, i, 0))
    return pl.pallas_call(
        body, name=name, grid=(bsz, seq // tt),
        in_specs=[tile, pl.BlockSpec((1, tt, c), lambda b, i: (b, i, 1)),
                  prev(0), prev(1), pl.BlockSpec(conv_w.shape, lambda b, i: (0, 0)), vec, vec, vec, vec],
        out_specs=[tile, tile],
        out_shape=[_sds((bsz, seq, c), BF16), _sds((bsz, seq, c), F32)],
        scratch_shapes=[pltpu.VMEM((CONV_HALO + tt, c), F32), pltpu.VMEM((tt + SUBLANES, c), F32)],
        compiler_params=_cparams("parallel", "arbitrary"),
    )(proj3, proj3, proj3, proj3, conv_w, conv_b, ln_g, ln_b, out_g)


def _conv_bwd(name, dmix3, proj3, cv3, conv_w, ln_g, ln_b, out_g):
    bsz, seq, _ = proj3.shape
    c = conv_w.shape[1]
    tt = _pick(seq, TILE["conv_t"], CONV_HALO)
    hb = tt // CONV_HALO
    nt = seq // tt
    last_hb = seq // CONV_HALO - 1
    ext = tt + CONV_HALO
    first = CONV_HALO - (CONV_WIDTH - 1)

    def body(v_ref, g_ref, vp_ref, gp_ref, cv_ref, cvn_ref, d_ref, dn_ref, w_ref, lg_ref, lb_ref, og_ref,
             o_ref, dw_ref, dcb_ref, dlg_ref, dlb_ref, dog_ref, a_ref, dc_ref, b_ref, ds_ref):
        i = pl.program_id(1)

        @pl.when((pl.program_id(0) == 0) & (i == 0))
        def _():
            for r in (dw_ref, dcb_ref, dlg_ref, dlb_ref, dog_ref):
                r[...] = jnp.zeros_like(r)

        keep_prev = (i > 0).astype(F32)
        keep_next = (i < nt - 1).astype(F32)
        sig_g = _sigmoid(g_ref[0])
        a_ref[pl.ds(0, CONV_HALO), :] = keep_prev * vp_ref[0] * _sigmoid(gp_ref[0])
        a_ref[pl.ds(CONV_HALO, tt), :] = v_ref[0] * sig_g

        lg, lb, og = lg_ref[...], lb_ref[...], og_ref[...]
        dc_own, d_og, d_lg, d_lb = _conv_post_bwd(cv_ref[0], d_ref[0], lg, lb, og)
        dc_next, _, _, _ = _conv_post_bwd(cvn_ref[0], keep_next * dn_ref[0], lg, lb, og)
        dog_ref[...] += d_og
        dlg_ref[...] += d_lg
        dlb_ref[...] += d_lb
        dcb_ref[...] += jnp.sum(dc_own, axis=0, keepdims=True)
        dc_ref[pl.ds(0, tt), :] = dc_own
        dc_ref[pl.ds(tt, CONV_HALO), :] = dc_next

        da = _conv_taps(dc_ref, w_ref, b_ref, 0, tt, flip=True)

        for r, taps in _tap_groups(first):
            if r:
                ds_ref[pl.ds(0, SUBLANES), :] = jnp.zeros((SUBLANES, c), F32)
                ds_ref[pl.ds(tt, SUBLANES), :] = jnp.zeros((SUBLANES, c), F32)
                ds_ref[pl.ds(r, tt), :] = dc_own
            for base, k in taps:
                prod = (ds_ref[...] * a_ref[pl.ds(base, tt + SUBLANES), :]) if r else (dc_own * a_ref[pl.ds(base, tt), :])
                dw_ref[k:k + 1, :] += jnp.sum(prod, axis=0, keepdims=True)
        val = v_ref[0]
        o_ref[0] = jnp.concatenate([da * sig_g, da * val * sig_g * (1.0 - sig_g)], axis=-1).astype(BF16)

    vec = pl.BlockSpec((1, c), lambda b, i: (0, 0))
    cur = lambda col: pl.BlockSpec((1, tt, c), lambda b, i: (b, i, col))
    prev = lambda col: pl.BlockSpec((1, CONV_HALO, c), lambda b, i: (b, jnp.maximum(i * hb - 1, 0), col))
    nxt = lambda col: pl.BlockSpec((1, CONV_HALO, c), lambda b, i: (b, jnp.minimum((i + 1) * hb, last_hb), col))
    wspec = pl.BlockSpec(conv_w.shape, lambda b, i: (0, 0))
    return pl.pallas_call(
        body, name=name, grid=(bsz, nt),
        in_specs=[cur(0), cur(1), prev(0), prev(1), cur(0), nxt(0), cur(0), nxt(0), wspec, vec, vec, vec],
        out_specs=[pl.BlockSpec((1, tt, 2 * c), lambda b, i: (b, i, 0)), wspec, vec, vec, vec, vec],
        out_shape=[_sds((bsz, seq, 2 * c), BF16), _sds(conv_w.shape, F32)] + [_sds((1, c), F32)] * 4,
        scratch_shapes=[pltpu.VMEM((CONV_HALO + tt, c), F32), pltpu.VMEM((ext, c), F32),
                        pltpu.VMEM((tt + SUBLANES, c), F32), pltpu.VMEM((tt + SUBLANES, c), F32)],
        compiler_params=_cparams("arbitrary", "arbitrary"),
    )(proj3, proj3, proj3, proj3, cv3, cv3, dmix3, dmix3, conv_w, ln_g, ln_b, out_g)


def _ssm_discretise(a_re, a_im, log_dt):
    dt = jnp.exp(log_dt)
    zr, zi = a_re * dt, a_im * dt
    mag = jnp.exp(zr)
    ar, ai = mag * jnp.cos(zi), mag * jnp.sin(zi)
    den = a_re * a_re + a_im * a_im
    nr = ar - 1.0
    return ar, ai, (nr * a_re + ai * a_im) / den, (ai * a_re - nr * a_im) / den


def _ssm_system(a_re, a_im, log_dt, a_re_x, a_im_x, log_dt_x, bt_re, bt_im):
    ar, ai, _, _ = _ssm_discretise(a_re, a_im, log_dt)
    _, _, cr, ci = _ssm_discretise(a_re_x, a_im_x, log_dt_x)
    return ar, ai, cr * bt_re - ci * bt_im, cr * bt_im + ci * bt_re


def _ssm_prep(name, prim):
    g, p = prim[0].shape

    def body(*refs):
        pwr_ref, pwi_ref, bbr_ref, bbi_ref = refs[8:]
        ar, ai, bbr, bbi = _ssm_system(*[r[...] for r in refs[:8]])
        bbr_ref[...] = bbr
        bbi_ref[...] = bbi
        pr, pi = ar, ai
        for k in range(SUBLANES):
            pwr_ref[k] = pr
            pwi_ref[k] = pi
            pr, pi = pr * ar - pi * ai, pr * ai + pi * ar

    return pl.pallas_call(
        body, name=name,
        out_shape=[_sds((SUBLANES, g, p), F32)] * 2 + [_sds(prim[6].shape, F32)] * 2,
        compiler_params=pltpu.CompilerParams(vmem_limit_bytes=VMEM_LIMIT),
    )(*prim)


def _ssm_param_grads(name, prim, dab_r, dab_i, dbb_r, dbb_i):
    g, p = prim[0].shape
    h = prim[6].shape[0] // g

    def body(*refs):
        dar_ref, dai_ref, dbr_ref, dbi_ref = refs[8:12]
        o_ar, o_ai, o_dt, o_br, o_bi = refs[12:]
        _, vjp = jax.vjp(_ssm_system, *[r[...] for r in refs[:8]])
        ct = (jnp.sum(dar_ref[...], axis=0), jnp.sum(dai_ref[...], axis=0), dbr_ref[...], dbi_ref[...])
        d_ar, d_ai, d_dt, d_arx, d_aix, d_dtx, d_br, d_bi = vjp(ct)
        per_group = lambda t: jnp.sum(t.reshape(g, h, p), axis=1)
        o_ar[...] = d_ar + per_group(d_arx)
        o_ai[...] = d_ai + per_group(d_aix)
        o_dt[...] = d_dt + jnp.sum(per_group(d_dtx), axis=1, keepdims=True)
        o_br[...] = d_br
        o_bi[...] = d_bi

    return pl.pallas_call(
        body, name=name,
        out_shape=[_sds(prim[k].shape, F32) for k in (0, 1, 2, 6, 7)],
        compiler_params=pltpu.CompilerParams(vmem_limit_bytes=VMEM_LIMIT),
    )(*prim, dab_r, dab_i, dbb_r, dbb_i)


def _cfma(xr, xi, cr, ci, sr, si):
    return xr + (cr * sr - ci * si), xi + (cr * si + ci * sr)


def _scan_tables(pw_r, pw_i, reverse):
    gp = pw_r.shape[1] * pw_r.shape[2]
    pr, pi = pw_r.reshape(SUBLANES, gp), pw_i.reshape(SUBLANES, gp)
    if reverse:
        pi = -pi
    row = jnp.arange(SUBLANES)[:, None]
    tabs = []
    for d in (1, 2, 4):
        keep = (row < SUBLANES - d) if reverse else (row >= d)
        tabs += [jnp.where(keep, pr[d - 1][None, :], 0.0), jnp.where(keep, pi[d - 1][None, :], 0.0)]
    tabs += [pr[::-1], pi[::-1]] if reverse else [pr, pi]
    return jnp.concatenate(tabs, axis=0)


MXU_DEPTH = 256


def _bands(c, gp):
    bw = min(c, MXU_DEPTH)
    return c // bw, bw, gp * bw // c


def _band_expand(rows16, w_ref, put, c, gp):
    nb, bw, sw = _bands(c, gp)
    for s in range(nb):
        band = rows16[:, s * bw:(s + 1) * bw]
        for half in (0, gp):
            cols = pl.ds(half + s * sw, sw)
            put(cols, _dot(band, w_ref[pl.ds(s * bw, bw), cols]))


def _band_contract(get16, w_ref, c, gp):
    nb, bw, sw = _bands(c, gp)
    out = []
    for s in range(nb):
        acc = None
        for half in (0, gp):
            cols = pl.ds(half + s * sw, sw)
            t = _dot_nt(get16(cols), w_ref[pl.ds(s * bw, bw), cols])
            acc = t if acc is None else acc + t
        out.append(acc)
    return out[0] if nb == 1 else jnp.concatenate(out, axis=1)


def _band_wgrad(name, a, a_block, c, b):
    n = a.shape[0]
    gp = b.shape[1] // 2
    nb, bw, sw = _bands(c, gp)
    tk = _pick(n, TILE["mm_bytes"] // (c * a.dtype.itemsize + 2 * gp * b.dtype.itemsize), 16)

    def body(a_ref, b_ref, o_ref):
        @pl.when(pl.program_id(0) == 0)
        def _():
            o_ref[...] = jnp.zeros_like(o_ref)

        for s in range(nb):
            band = a_ref[:, s * bw:(s + 1) * bw].astype(BF16)
            for h, half in enumerate((0, gp)):
                o_ref[pl.ds(s * bw, bw), pl.ds(h * sw, sw)] += _dot_tn(
                    band, b_ref[:, pl.ds(half + s * sw, sw)].astype(BF16))

    return pl.pallas_call(
        body, name=name, grid=(n // tk,),
        in_specs=[pl.BlockSpec((tk, c), lambda k: (k, a_block)), pl.BlockSpec((tk, 2 * gp), lambda k: (k, 0))],
        out_specs=pl.BlockSpec((c, 2 * sw), lambda k: (0, 0)),
        out_shape=_sds((c, 2 * sw), F32),
        compiler_params=_cparams("arbitrary"),
    )(a, b)


def _band_diag_take(comp, half, c, gp):
    nb, bw, sw = _bands(c, gp)
    return jnp.concatenate([_block_diag_take(comp[s * bw:(s + 1) * bw, half * sw:(half + 1) * sw], bw // SSM_GROUP)
                            for s in range(nb)], axis=0)


def _scan_fwd(name, tab, proj3, u_block, bbd, cdt):
    bsz, seq, _ = proj3.shape
    c, w = bbd.shape
    gp = w // 2
    tt = _pick(seq, TILE["scan_t"], 16)
    nblk = tt // SUBLANES
    cw = _pick(gp, TILE["scan_w"], LANES)

    def body(tab_ref, u_ref, bbd_ref, cdt_ref, xs_ref, y_ref, carry_ref, bu_ref):
        @pl.when(pl.program_id(1) == 0)
        def _():
            carry_ref[...] = jnp.zeros_like(carry_ref)

        def put_bu(cols, val):
            bu_ref[0, :, cols] = val

        _band_expand(u_ref[0].astype(BF16), bbd_ref, put_bu, c, gp)

        for ch in range(gp // cw):
            re, im = pl.ds(ch * cw, cw), pl.ds(gp + ch * cw, cw)

            def blk(r, carry, re=re, im=im):
                tabs = [tab_ref[pl.ds(SUBLANES * k, SUBLANES), re] for k in range(8)]
                rows = pl.ds(pl.multiple_of(r * SUBLANES, SUBLANES), SUBLANES)
                xr, xi = bu_ref[0, rows, re], bu_ref[0, rows, im]
                for j, d in enumerate((1, 2, 4)):
                    xr, xi = _cfma(xr, xi, tabs[2 * j], tabs[2 * j + 1], pltpu.roll(xr, d, 0), pltpu.roll(xi, d, 0))
                xr, xi = _cfma(xr, xi, tabs[6], tabs[7], carry[0], carry[1])
                xs_ref[0, rows, re] = xr
                xs_ref[0, rows, im] = xi
                last = SUBLANES - 1
                return (jnp.broadcast_to(xr[last:, :], xr.shape), jnp.broadcast_to(xi[last:, :], xi.shape))

            cr, ci = lax.fori_loop(0, nblk, blk, (carry_ref[:, re], carry_ref[:, im]))
            carry_ref[:, re] = cr
            carry_ref[:, im] = ci

        y_ref[0] = _band_contract(lambda cols: xs_ref[0, :, cols].astype(BF16), cdt_ref, c, gp)

    whole = lambda arr: pl.BlockSpec(arr.shape, lambda b, t: (0, 0))
    return pl.pallas_call(
        body, name=name, grid=(bsz, seq // tt),
        in_specs=[whole(tab), pl.BlockSpec((1, tt, c), lambda b, t: (b, t, u_block)), whole(bbd), whole(cdt)],
        out_specs=[pl.BlockSpec((1, tt, w), lambda b, t: (b, t, 0)), pl.BlockSpec((1, tt, c), lambda b, t: (b, t, 0))],
        out_shape=[_sds((bsz, seq, w), F32), _sds((bsz, seq, c), F32)],
        scratch_shapes=[pltpu.VMEM((SUBLANES, w), F32), pltpu.VMEM((1, tt, w), F32)],
        compiler_params=_cparams("arbitrary", "arbitrary"),
    )(tab, proj3, bbd, cdt)


def _scan_bwd(name, tab, dy3, xs3, du_skip3, bbd, cdt, exchange=None):
    bsz, seq, w = xs3.shape
    c = bbd.shape[0]
    gp = w // 2
    tt = _pick(seq, TILE["scan_t"], 16)
    nblk = tt // SUBLANES
    cw = _pick(gp, TILE["scan_w"], LANES)
    nt = seq // tt

    def body(tab_ref, dy_ref, xs_ref, halo_ref, skip_ref, bbd_ref, cdt_ref, lam_ref, du_ref, dar_ref, dai_ref,
             carry_ref, g_ref):
        t = pl.program_id(1)

        @pl.when(t == 0)
        def _():
            carry_ref[...] = jnp.zeros_like(carry_ref)

        @pl.when((pl.program_id(0) == 0) & (t == 0))
        def _():
            dar_ref[...] = jnp.zeros_like(dar_ref)
            dai_ref[...] = jnp.zeros_like(dai_ref)

        def put_g(cols, val):
            g_ref[0, :, cols] = val

        _band_expand(dy_ref[0], cdt_ref, put_g, c, gp)

        has_prev = (t < nt - 1).astype(F32)
        row0 = lax.broadcasted_iota(jnp.int32, (SUBLANES, cw), 0) == 0
        last = SUBLANES - 1

        for ch in range(gp // cw):
            re, im = pl.ds(ch * cw, cw), pl.ds(gp + ch * cw, cw)

            def step(rows, xm1r, xm1i, state, re=re, im=im):
                tabs = [tab_ref[pl.ds(SUBLANES * k, SUBLANES), re] for k in range(8)]
                cr, ci, accr, acci = state
                lr, li = g_ref[0, rows, re], g_ref[0, rows, im]
                for j, d in enumerate((1, 2, 4)):
                    lr, li = _cfma(lr, li, tabs[2 * j], tabs[2 * j + 1],
                                   pltpu.roll(lr, SUBLANES - d, 0), pltpu.roll(li, SUBLANES - d, 0))
                lr, li = _cfma(lr, li, tabs[6], tabs[7], cr, ci)
                lam_ref[0, rows, re] = lr
                lam_ref[0, rows, im] = li
                xr, xi = xs_ref[0, rows, re], xs_ref[0, rows, im]
                xpr = jnp.where(row0, jnp.broadcast_to(xm1r[last:, :], xr.shape), pltpu.roll(xr, 1, 0))
                xpi = jnp.where(row0, jnp.broadcast_to(xm1i[last:, :], xi.shape), pltpu.roll(xi, 1, 0))
                accr = accr + (lr * xpr + li * xpi)
                acci = acci + (li * xpr - lr * xpi)
                return (jnp.broadcast_to(lr[:1, :], lr.shape), jnp.broadcast_to(li[:1, :], li.shape), accr, acci)

            def blk(k, state, re=re, im=im, step=step):
                r = nblk - 1 - k
                rows = pl.ds(pl.multiple_of(r * SUBLANES, SUBLANES), SUBLANES)
                prev = pl.ds(pl.multiple_of((r - 1) * SUBLANES, SUBLANES), SUBLANES)
                return step(rows, xs_ref[0, prev, re], xs_ref[0, prev, im], state)

            zero = jnp.zeros((SUBLANES, cw), F32)
            state = lax.fori_loop(0, nblk - 1, blk, (carry_ref[:, re], carry_ref[:, im], zero, zero))
            cr, ci, accr, acci = step(pl.ds(0, SUBLANES), has_prev * halo_ref[0, :, re], has_prev * halo_ref[0, :, im], state)
            carry_ref[:, re] = cr
            carry_ref[:, im] = ci
            dar_ref[:, re] += accr
            dai_ref[:, re] += acci

        du = _band_contract(lambda cols: lam_ref[0, :, cols].astype(BF16), bbd_ref, c, gp)
        du_ref[0] = (du + skip_ref[0]).astype(BF16)

    tile = pl.BlockSpec((1, tt, w), lambda b, t: (b, nt - 1 - t, 0))
    thin = pl.BlockSpec((1, tt, c), lambda b, t: (b, nt - 1 - t, 0))
    halo = pl.BlockSpec((1, SUBLANES, w), lambda b, t: (b, jnp.maximum((nt - 1 - t) * nblk - 1, 0), 0))
    acc = pl.BlockSpec((SUBLANES, gp), lambda b, t: (0, 0))
    whole = lambda arr: pl.BlockSpec(arr.shape, lambda b, t: (0, 0))
    return _call(
        name, body, (bsz, nt), [whole(tab), thin, tile, halo, thin, whole(bbd), whole(cdt)], [tile, thin, acc, acc],
        [_sds(xs3.shape, F32), _sds((bsz, seq, c), BF16), _sds((SUBLANES, gp), F32), _sds((SUBLANES, gp), F32)],
        (tab, dy3, xs3, xs3, du_skip3, bbd, cdt), ("arbitrary", "arbitrary"),
        scratch=[pltpu.VMEM((SUBLANES, w), F32), pltpu.VMEM((1, tt, w), F32)], exchange=exchange)


def _gelu_parts(y):
    inner = _GELU_K * (y + _GELU_C * y * y * y)
    t = jnp.tanh(inner)
    return 0.5 * y * (1.0 + t), t


def _ssm_out_fwd(name, cx, proj, u_block, d_skip, glu_w, glu_b, out_g):
    n, c = cx.shape
    tm = _pick(n, TILE["row"], 16)

    def body(cx_ref, u_ref, d_ref, gw_ref, gb_ref, og_ref, y_ref, o_ref):
        y = cx_ref[...] + d_ref[...] * u_ref[...]
        y_ref[...] = y
        gy, _ = _gelu_parts(y)
        z = _dot(gy.astype(BF16), gw_ref[...]) + gb_ref[...]
        _, sh = _rms_stats(gy * _sigmoid(z))
        o_ref[...] = (sh * og_ref[...]).astype(BF16)

    vec = pl.BlockSpec((1, c), lambda i: (0, 0))
    row = pl.BlockSpec((tm, c), lambda i: (i, 0))
    return pl.pallas_call(
        body, name=name, grid=(n // tm,),
        in_specs=[row, pl.BlockSpec((tm, c), lambda i: (i, u_block)), vec, pl.BlockSpec(glu_w.shape, lambda i: (0, 0)),
                  vec, vec],
        out_specs=[row, row],
        out_shape=[_sds((n, c), F32), _sds((n, c), BF16)],
        compiler_params=_cparams("parallel"),
    )(cx, proj, d_skip, glu_w, glu_b, out_g)


def _ssm_out_bwd(name, dmix, d_block, y, proj, u_block, d_skip, glu_w, glu_b, out_g):
    n, c = y.shape
    tm = _pick(n, TILE["row"], 16)

    def body(d_ref, y_ref, u_ref, dk_ref, gw_ref, gb_ref, og_ref, dy_ref, du_ref, dgw_ref, dgb_ref, dog_ref, dd_ref):
        @pl.when(pl.program_id(0) == 0)
        def _():
            for r in (dgw_ref, dgb_ref, dog_ref, dd_ref):
                r[...] = jnp.zeros_like(r)

        yv = y_ref[...]
        gy, th = _gelu_parts(yv)
        gy16 = gy.astype(BF16)
        sz = _sigmoid(_dot(gy16, gw_ref[...]) + gb_ref[...])
        r, sh = _rms_stats(gy * sz)
        dout = d_ref[...]
        dog_ref[...] += jnp.sum(dout * sh, axis=0, keepdims=True)
        dsh = dout * og_ref[...]
        ds = r * (dsh - sh * jnp.mean(dsh * sh, axis=-1, keepdims=True))
        dz = ds * gy * sz * (1.0 - sz)
        dz16 = dz.astype(BF16)
        dgb_ref[...] += jnp.sum(dz, axis=0, keepdims=True)
        dgw_ref[...] += _dot_tn(gy16, dz16)
        dgy = ds * sz + _dot_nt(dz16, gw_ref[...])
        dgelu = 0.5 * (1.0 + th) + 0.5 * yv * (1.0 - th * th) * (_GELU_K * (1.0 + 3.0 * _GELU_C * yv * yv))
        dy = dgy * dgelu
        dy_ref[...] = dy.astype(BF16)
        du_ref[...] = dy * dk_ref[...]
        dd_ref[...] += jnp.sum(dy * u_ref[...], axis=0, keepdims=True)

    vec = pl.BlockSpec((1, c), lambda i: (0, 0))
    row = pl.BlockSpec((tm, c), lambda i: (i, 0))
    mat = pl.BlockSpec(glu_w.shape, lambda i: (0, 0))
    return pl.pallas_call(
        body, name=name, grid=(n // tm,),
        in_specs=[pl.BlockSpec((tm, c), lambda i: (i, d_block)), row, pl.BlockSpec((tm, c), lambda i: (i, u_block)),
                  vec, mat, vec, vec],
        out_specs=[row, row, mat, vec, vec, vec],
        out_shape=[_sds((n, c), BF16), _sds((n, c), F32), _sds(glu_w.shape, F32)] + [_sds((1, c), F32)] * 3,
        compiler_params=_cparams("arbitrary"),
    )(dmix, y, proj, d_skip, glu_w, glu_b, out_g)


def _mesh_pos():
    return tuple(lax.axis_index(a) for a in MESH_AXES)


def _other_chips(x, y):
    return [(1 - x, y), (x, 1 - y), (1 - x, 1 - y)]


def _remote(src, dst, send_sem, recv_sem, dev):
    return pltpu.make_async_remote_copy(src_ref=src, dst_ref=dst, send_sem=send_sem, recv_sem=recv_sem,
                                        device_id=dev, device_id_type=pl.DeviceIdType.MESH)


def _hbm_call(name, body, operands, out_shapes, scratch):
    hbm = pl.BlockSpec(memory_space=pltpu.HBM)
    return pl.pallas_call(body, name=name, in_specs=[hbm] * len(operands), out_specs=[hbm] * len(out_shapes),
                          out_shape=out_shapes, scratch_shapes=scratch)(*operands)


_Exchange = collections.namedtuple("_Exchange", "operands out_shapes scratch start finish")


def _run_exchange(name, plan):
    nin, nout = len(plan.operands), len(plan.out_shapes)

    def body(*refs):
        parts = refs[:nin], refs[nin:nin + nout], refs[nin + nout:]
        plan.start(*parts)
        plan.finish(*parts)

    return _hbm_call(name, body, plan.operands, plan.out_shapes, plan.scratch)


def _gather_plan(blocks):
    nop = len(blocks)

    def copies(x_refs, o_refs, sems):
        send_sems, recv_sems, local_sems = sems
        x, y, c = _mesh_pos()
        me, sibling = (x, y, c), (x, y, 1 - c)
        chips = _other_chips(x, y)

        def copy(i, k, block_of, to, src=None):
            dst = o_refs[i].at[4 * block_of[0] + 2 * block_of[1] + block_of[2]]
            return _remote(dst if src is None else src, dst, send_sems.at[i, k], recv_sems.at[i, k], to)

        own = [pltpu.make_async_copy(x_refs[i], o_refs[i].at[4 * x + 2 * y + c], local_sems.at[i]) for i in range(nop)]
        first = []
        for i in range(nop):
            first.append(copy(i, 0, me, sibling, src=x_refs[i]))
            first += [copy(i, 1 + j, me, (*chip, c), src=x_refs[i]) for j, chip in enumerate(chips)]
        return copy, own, first, me, sibling, chips, c

    def start(x_refs, o_refs, sems):
        _, own, first, *_ = copies(x_refs, o_refs, sems)
        for cp in own + first:
            cp.start()

    def finish(x_refs, o_refs, sems):
        copy, own, first, me, sibling, chips, c = copies(x_refs, o_refs, sems)
        passed = []
        for i in range(nop):
            for j, chip in enumerate(chips):
                copy(i, 1 + j, (*chip, c), me).wait_recv()
                passed.append(copy(i, 4 + j, (*chip, c), sibling))
                passed[-1].start()
        for i in range(nop):
            copy(i, 0, sibling, me).wait_recv()
            for j, chip in enumerate(chips):
                copy(i, 4 + j, (*chip, 1 - c), me).wait_recv()
        for cp in first + passed:
            cp.wait_send()
        for cp in own:
            cp.wait()

    return _Exchange(list(blocks), [_sds((N_DEV,) + b.shape, b.dtype) for b in blocks],
                     [pltpu.SemaphoreType.DMA((nop, N_DEV - 1)), pltpu.SemaphoreType.DMA((nop, N_DEV - 1)),
                      pltpu.SemaphoreType.DMA((nop,))], start, finish)


def _exchange_sibling(name, grads):
    nop = len(grads)

    def body(*refs):
        x_refs, o_refs = refs[:nop], refs[nop:2 * nop]
        send_sems, recv_sems = refs[2 * nop:]
        x, y, c = _mesh_pos()
        copies = [_remote(x_refs[i].at[2 * q + (1 - c)], o_refs[i].at[q], send_sems.at[i, q], recv_sems.at[i, q],
                          (x, y, 1 - c)) for i in range(nop) for q in range(N_DEV // 2)]
        for cp in copies:
            cp.start()
        for cp in copies:
            cp.wait_recv()
        for cp in copies:
            cp.wait_send()

    return _hbm_call(name, body, grads, [_sds((N_DEV // 2,) + g.shape[1:], g.dtype) for g in grads],
                     [pltpu.SemaphoreType.DMA((nop, N_DEV // 2)), pltpu.SemaphoreType.DMA((nop, N_DEV // 2))])


def _pair_sum(name, grad, other):
    nchip, _, r, c = grad.shape
    tr = _pick(r, max(SUBLANES, TILE["sum_bytes"] // (8 * c)), SUBLANES)

    def body(g_ref, o_ref, s_ref):
        mine = jnp.where(lax.axis_index("c") == 0, g_ref[0, 0], g_ref[0, 1])
        s_ref[0] = (mine + o_ref[0]).astype(s_ref.dtype)

    return pl.pallas_call(
        body, name=name, grid=(nchip, r // tr),
        in_specs=[pl.BlockSpec((1, 2, tr, c), lambda q, t: (q, 0, t, 0)), pl.BlockSpec((1, tr, c), lambda q, t: (q, t, 0))],
        out_specs=pl.BlockSpec((1, tr, c), lambda q, t: (q, t, 0)),
        out_shape=_sds((nchip, r, c), BF16),
        compiler_params=_cparams("parallel", "parallel"),
    )(grad, other)


def _chip_exchange_plan(sums):
    nop = len(sums)

    def copies(x_refs, o_refs, sems, arriving):
        send_sems, recv_sems, local_sems = sems
        x, y, c = _mesh_pos()
        mine = 2 * x + y
        out = []
        for i in range(nop):
            for j, (px, py) in enumerate(_other_chips(x, y)):
                theirs = 2 * px + py
                src, dst = (mine, theirs) if arriving else (theirs, mine)
                out.append(_remote(x_refs[i].at[src], o_refs[i].at[dst], send_sems.at[i, j], recv_sems.at[i, j],
                                   (px, py, c)))
        if not arriving:
            out += [pltpu.make_async_copy(x_refs[i].at[mine], o_refs[i].at[mine], local_sems.at[i]) for i in range(nop)]
        return out

    def start(x_refs, o_refs, sems):
        for cp in copies(x_refs, o_refs, sems, False):
            cp.start()

    def finish(x_refs, o_refs, sems):
        for cp in copies(x_refs, o_refs, sems, True):
            cp.wait_recv()
        mine = copies(x_refs, o_refs, sems, False)
        for cp in mine[:3 * nop]:
            cp.wait_send()
        for cp in mine[3 * nop:]:
            cp.wait()

    return _Exchange(list(sums), [_sds(s.shape, s.dtype) for s in sums],
                     [pltpu.SemaphoreType.DMA((nop, 3)), pltpu.SemaphoreType.DMA((nop, 3)), pltpu.SemaphoreType.DMA((nop,))],
                     start, finish)


def _part_rows(npart, r, c):
    return _pick(r, max(SUBLANES, TILE["sum_bytes"] // (4 * npart * c)), SUBLANES)


def _adamw(name, parts, w, m, v):
    npart, r, c = parts.shape
    lead = len(w.shape) - 2
    tr = _part_rows(npart, r, c)
    c1 = 1.0 - ADAM_B1 ** ADAM_STEP
    c2 = 1.0 - ADAM_B2 ** ADAM_STEP
    at = (0,) * lead + (slice(None), slice(None))

    def body(p_ref, w_ref, m_ref, v_ref, g_ref, d_ref, nm_ref, nv_ref):
        g = p_ref[0].astype(F32)
        for k in range(1, npart):
            g = g + p_ref[k].astype(F32)
        nm = ADAM_B1 * m_ref[at] + (1.0 - ADAM_B1) * g
        nv = ADAM_B2 * v_ref[at] + (1.0 - ADAM_B2) * (g * g)
        g_ref[at] = g
        nm_ref[at] = nm
        nv_ref[at] = nv
        d_ref[at] = -ADAM_LR * ((nm / c1) / (jnp.sqrt(nv / c2) + ADAM_EPS) + ADAM_WD * w_ref[at])

    row = pl.BlockSpec((1,) * lead + (tr, c), lambda i: (0,) * lead + (i, 0))
    return pl.pallas_call(
        body, name=name, grid=(r // tr,),
        in_specs=[pl.BlockSpec((npart, tr, c), lambda i: (0, i, 0)), row, row, row],
        out_specs=[row] * 4,
        out_shape=[_sds(w.shape, F32)] * 4,
        compiler_params=_cparams("parallel"),
    )(parts, w, m, v)


def _pack(pieces, row_mult, lead=()):
    nl = len(lead)
    flat, spans, off = [], [], 0
    for p in pieces:
        p = p.reshape(lead + (-1,))
        size = p.shape[-1]
        padded = -(-size // PACK_W) * PACK_W
        flat.append(jnp.pad(p, [(0, 0)] * nl + [(0, padded - size)]))
        spans.append((off, size))
        off += padded
    rows = -(-(off // PACK_W) // row_mult) * row_mult
    if rows * PACK_W > off:
        flat.append(jnp.zeros(lead + (rows * PACK_W - off,), flat[0].dtype))
    return jnp.concatenate(flat, axis=-1).reshape(lead + (rows, PACK_W)), spans


def _unpack(buf, spans, shapes, lead=0):
    flat = buf.reshape(buf.shape[:lead] + (-1,))
    return [flat[..., o:o + s].reshape(buf.shape[:lead] + tuple(shape)) for (o, s), shape in zip(spans, shapes)]


def _block_diag(rows_gh, groups):
    gh, p = rows_gh.shape
    own = (jnp.arange(gh)[:, None] // (gh // groups) == jnp.arange(groups)[None, :]).astype(rows_gh.dtype)
    return (own[:, :, None] * rows_gh[:, None, :]).reshape(gh, groups * p)


def _block_diag_take(dense, groups):
    gh = dense.shape[0]
    p = dense.shape[1] // groups
    own = (jnp.arange(gh)[:, None] // (gh // groups) == jnp.arange(groups)[None, :]).astype(dense.dtype)
    return jnp.sum(dense.reshape(gh, groups, p) * own[:, :, None], axis=1)


FFN1 = ("ffn1_w1", "ffn1_w3", "ffn1_w2")
MIXER = ("w_in", "ssm_glu_w", "w_out")
FFN2 = ("ffn2_w1", "ffn2_w3", "ffn2_w2")
BIG = FFN1 + MIXER + FFN2
COL_SHARDED = ("ffn1_w1", "ffn1_w3", "w_in", "ffn2_w1", "ffn2_w3", "conv_w")
SMALL = ("norm_ffn1", "norm_mix", "conv_b", "conv_ln_g", "conv_ln_b", "conv_out_g", "ssm_A_re", "ssm_A_im",
         "ssm_log_dt", "ssm_B_re", "ssm_B_im", "ssm_C_re", "ssm_C_im", "ssm_D", "ssm_glu_b", "ssm_out_g",
         "norm_ffn2", "norm_final")
WEIGHTS = ("norm_ffn1", "ffn1_w1", "ffn1_w3", "ffn1_w2", "norm_mix", "w_in", "conv_w", "conv_b", "conv_ln_g",
           "conv_ln_b", "conv_out_g", "ssm_A_re", "ssm_A_im", "ssm_log_dt", "ssm_B_re", "ssm_B_im", "ssm_C_re",
           "ssm_C_im", "ssm_D", "ssm_glu_w", "ssm_glu_b", "ssm_out_g", "w_out", "norm_ffn2", "ffn2_w1", "ffn2_w3",
           "ffn2_w2", "norm_final")


def _ffn_backward(tag, dxo, x, g, w1, w3, w2, saved, exchange=None, reduce_plan=None):
    a, b, h = saved
    (da, db, hid, dxh), got = _ffn_bwd_hidden(tag + "_bwd_hidden", dxo, a, b, w2, exchange=exchange)
    dws = [_mm_tn(tag + "_dw1", da, h), _mm_tn(tag + "_dw3", db, h), _mm_tn(tag + "_dw2", hid, dxh)]
    f = a.shape[1]
    (dx, dg), reduced = _dx_rms_bwd(tag + "_bwd_dx", [(da, f, 0, w1, f, 0), (db, f, 0, w3, f, 0)], dxo, x, g,
                                    exchange=reduce_plan(dws) if reduce_plan else None)
    return (dx, dg, dws), got, reduced


def _reduce_in_chip(tag, names, grads):
    send = [g.reshape((N_DEV, -1) + g.shape[1:]) for g in grads]
    from_core = _exchange_sibling("exchange_core_" + tag, send)
    return _chip_exchange_plan([_pair_sum("pair_sum_" + k, s.reshape((N_DEV // 2, 2) + s.shape[1:]), o)
                                for k, s, o in zip(names, send, from_core)])


def kernel(x, norm_ffn1, ffn1_w1, ffn1_w3, ffn1_w2, norm_mix, w_in, conv_w, conv_b, conv_ln_g, conv_ln_b, conv_out_g, ssm_A_re, ssm_A_im, ssm_log_dt, ssm_B_re, ssm_B_im, ssm_C_re, ssm_C_im, ssm_D, ssm_glu_w, ssm_glu_b, ssm_out_g, w_out, norm_ffn2, ffn2_w1, ffn2_w3, ffn2_w2, norm_final, loss_target, m_norm_ffn1, m_ffn1_w1, m_ffn1_w3, m_ffn1_w2, m_norm_mix, m_w_in, m_conv_w, m_conv_b, m_conv_ln_g, m_conv_ln_b, m_conv_out_g, m_ssm_A_re, m_ssm_A_im, m_ssm_log_dt, m_ssm_B_re, m_ssm_B_im, m_ssm_C_re, m_ssm_C_im, m_ssm_D, m_ssm_glu_w, m_ssm_glu_b, m_ssm_out_g, m_w_out, m_norm_ffn2, m_ffn2_w1, m_ffn2_w3, m_ffn2_w2, m_norm_final, v_norm_ffn1, v_ffn1_w1, v_ffn1_w3, v_ffn1_w2, v_norm_mix, v_w_in, v_conv_w, v_conv_b, v_conv_ln_g, v_conv_ln_b, v_conv_out_g, v_ssm_A_re, v_ssm_A_im, v_ssm_log_dt, v_ssm_B_re, v_ssm_B_im, v_ssm_C_re, v_ssm_C_im, v_ssm_D, v_ssm_glu_w, v_ssm_glu_b, v_ssm_out_g, v_w_out, v_norm_ffn2, v_ffn2_w1, v_ffn2_w3, v_ffn2_w2, v_norm_final):
    args = dict(locals())
    wt = {n: args[n] for n in WEIGHTS}
    mom = {n: args["m_" + n] for n in WEIGHTS}
    var = {n: args["v_" + n] for n in WEIGHTS}

    bsz, seq, d = x.shape
    n = bsz * seq
    c = conv_b.shape[-1]
    groups = c // SSM_GROUP
    gp = groups * SSM_STATE
    u_b = 2

    shard = {k: (wt[k][0].T if k in COL_SHARDED else wt[k][0]).astype(BF16) for k in BIG}
    gathered = _run_exchange("gather_weights_ffn1", _gather_plan([shard[k] for k in FFN1]))
    full = {k: g.reshape(-1, g.shape[-1]) for k, g in zip(FFN1, gathered)}
    gather_rest = _gather_plan([shard[k] for k in MIXER + FFN2] + [wt["conv_w"][0]])

    vec = lambda k: wt[k].reshape(1, -1)
    g_ffn1, g_mix, g_ffn2, g_fin = vec("norm_ffn1"), vec("norm_mix"), vec("norm_ffn2"), vec("norm_final")
    cb, lng, lnb, cog = vec("conv_b"), vec("conv_ln_g"), vec("conv_ln_b"), vec("conv_out_g")
    d_skip, glu_b, sog = vec("ssm_D"), vec("ssm_glu_b"), vec("ssm_out_g")

    a_re, a_im = wt["ssm_A_re"][0], wt["ssm_A_im"][0]
    log_dt = wt["ssm_log_dt"][0].reshape(groups, 1)
    bt_re = wt["ssm_B_re"][0].transpose(0, 2, 1).reshape(groups * SSM_GROUP, SSM_STATE)
    bt_im = wt["ssm_B_im"][0].transpose(0, 2, 1).reshape(groups * SSM_GROUP, SSM_STATE)
    c_re = wt["ssm_C_re"][0].reshape(groups * SSM_GROUP, SSM_STATE)
    c_im = wt["ssm_C_im"][0].reshape(groups * SSM_GROUP, SSM_STATE)
    per_chan = lambda t: jnp.repeat(t, SSM_GROUP, axis=0)
    ssm_prim = (a_re, a_im, log_dt, per_chan(a_re), per_chan(a_im), per_chan(jnp.broadcast_to(log_dt, a_re.shape)),
                bt_re, bt_im)
    pw_r, pw_i, bb_r, bb_i = _ssm_prep("ssm_prep", ssm_prim)
    tab_f = _scan_tables(pw_r, pw_i, False)
    tab_b = _scan_tables(pw_r, pw_i, True)
    bbd = jnp.concatenate([_block_diag(bb_r, groups), _block_diag(bb_i, groups)], axis=1).astype(BF16)
    cdt = jnp.concatenate([_block_diag(c_re, groups), -_block_diag(c_im, groups)], axis=1).astype(BF16)

    x0 = x.reshape(n, d)
    (x1, *ffn1_saved), gathered = _ffn_fwd("ffn1_fwd", x0, g_ffn1, full["ffn1_w1"], full["ffn1_w3"], full["ffn1_w2"],
                                           exchange=gather_rest)
    full.update({k: g.reshape(-1, g.shape[-1]) for k, g in zip(MIXER + FFN2, gathered)})
    conv_w_full = gathered[-1].transpose(1, 0, 2).reshape(CONV_WIDTH, c)
    conv_w_pad = jnp.pad(conv_w_full, ((0, CONV_HALO - CONV_WIDTH), (0, 0)))
    (proj,), h2 = _rms_mm("mix_in", x1, g_mix, [full["w_in"]], F32)
    proj3 = proj.reshape(bsz, seq, 3 * c)
    an3, cv3 = _conv_fwd("conv_fwd", proj3, conv_w_pad, cb, lng, lnb, cog)
    an = an3.reshape(n, c)
    xs3, cx3 = _scan_fwd("scan_fwd", tab_f, proj3, u_b, bbd, cdt)
    xs = xs3.reshape(n, 2 * gp)
    y, sn = _ssm_out_fwd("ssm_out_fwd", cx3.reshape(n, c), proj, u_b, d_skip, full["ssm_glu_w"], glu_b, sog)
    w_o = full["w_out"]
    x2 = _row_mm("mix_out", [(an, c, 0, w_o, c, 0, False), (sn, c, 0, w_o, c, 1, False)], d, F32, add=x1)
    (x3, *ffn2_saved), _ = _ffn_fwd("ffn2_fwd", x2, g_ffn2, full["ffn2_w1"], full["ffn2_w3"], full["ffn2_w2"])
    dx3, loss_tile, d_gfin = _loss_head("loss_head", x3, g_fin, loss_target.reshape(n, d))
    loss = lax.psum(loss_tile[0, 0], MESH_AXES)

    grads, from_chips = {}, {}
    (dx2, grads["norm_ffn2"], dws), _, _ = _ffn_backward(
        "ffn2", dx3, x2, g_ffn2, full["ffn2_w1"], full["ffn2_w3"], full["ffn2_w2"], ffn2_saved)
    reduce_ffn2 = _reduce_in_chip("ffn2", FFN2, dws)

    dmix = _row_mm("mix_out_bwd", [(dx2, d, 0, w_o, 2 * c, 0, True)], 2 * c, F32)
    grads["w_out"] = jnp.concatenate([_mm_tn("dw_out_a", an, dx2), _mm_tn("dw_out_s", sn, dx2)], axis=0)

    dy, du_skip, grads["ssm_glu_w"], grads["ssm_glu_b"], grads["ssm_out_g"], grads["ssm_D"] = _ssm_out_bwd(
        "ssm_out_bwd", dmix, 1, y, proj, u_b, d_skip, full["ssm_glu_w"], glu_b, sog)
    (lam3, du3, dab_r, dab_i), got = _scan_bwd("scan_bwd", tab_b, dy.reshape(bsz, seq, c), xs3,
                                               du_skip.reshape(bsz, seq, c), bbd, cdt, exchange=reduce_ffn2)
    from_chips.update(zip(FFN2, got))
    lam, du = lam3.reshape(n, 2 * gp), du3.reshape(n, c)
    d_bbd = _band_wgrad("ssm_dbb", proj, u_b, c, lam)
    d_cdt = _band_wgrad("ssm_dc", dy, 0, c, xs)
    d_are, d_aim, d_ldt, d_btr, d_bti = _ssm_param_grads(
        "ssm_param_grads", ssm_prim,
        dab_r.reshape(SUBLANES, groups, SSM_STATE), dab_i.reshape(SUBLANES, groups, SSM_STATE),
        _band_diag_take(d_bbd, 0, c, gp), _band_diag_take(d_bbd, 1, c, gp))
    grads["ssm_A_re"], grads["ssm_A_im"], grads["ssm_log_dt"] = d_are, d_aim, d_ldt
    grads["ssm_B_re"] = d_btr.reshape(groups, SSM_GROUP, SSM_STATE).transpose(0, 2, 1)
    grads["ssm_B_im"] = d_bti.reshape(groups, SSM_GROUP, SSM_STATE).transpose(0, 2, 1)
    grads["ssm_C_re"] = _band_diag_take(d_cdt, 0, c, gp)
    grads["ssm_C_im"] = -_band_diag_take(d_cdt, 1, c, gp)

    dconv3, d_cw, grads["conv_b"], grads["conv_ln_g"], grads["conv_ln_b"], grads["conv_out_g"] = _conv_bwd(
        "conv_bwd", dmix.reshape(bsz, seq, 2 * c), proj3, cv3, conv_w_pad, lng, lnb, cog)
    dconv = dconv3.reshape(n, 2 * c)
    grads["conv_w"] = d_cw[:CONV_WIDTH]
    grads["w_in"] = jnp.concatenate([_mm_tn("dw_in_conv", dconv, h2), _mm_tn("dw_in_ssm", du, h2)], axis=0)
    w_i = full["w_in"]
    (dx1, grads["norm_mix"]), _ = _dx_rms_bwd("mix_in_bwd", [(dconv, 2 * c, 0, w_i, 2 * c, 0), (du, c, 0, w_i, c, 2)],
                                              dx2, x1, g_mix)
    reduce_mixer = _reduce_in_chip("mixer", MIXER, [grads[k] for k in MIXER])

    (dx0, grads["norm_ffn1"], _), got, reduced = _ffn_backward(
        "ffn1", dx1, x0, g_ffn1, full["ffn1_w1"], full["ffn1_w3"], full["ffn1_w2"], ffn1_saved,
        exchange=reduce_mixer, reduce_plan=functools.partial(_reduce_in_chip, "ffn1", FFN1))
    from_chips.update(zip(MIXER, got))
    from_chips.update(zip(FFN1, reduced))
    grads["norm_final"] = d_gfin

    res = {}
    for k in BIG:
        parts = from_chips[k]
        if k in COL_SHARDED:
            swap = lambda t: jnp.swapaxes(t, -1, -2)
            res[k] = [swap(t) for t in _adamw("adamw_" + k, parts, swap(wt[k]), swap(mom[k]), swap(var[k]))]
        else:
            res[k] = _adamw("adamw_" + k, parts, wt[k], mom[k], var[k])

    small_names = SMALL + ("conv_w",)
    no_state = jnp.zeros_like(grads["conv_w"])
    part, spans = _pack([grads[k] for k in small_names], SUBLANES)
    (all_parts,) = _run_exchange("gather_small_grads", _gather_plan([part]))
    w_pk, _ = _pack([wt[k] for k in SMALL] + [no_state], SUBLANES)
    m_pk, _ = _pack([mom[k] for k in SMALL] + [no_state], SUBLANES)
    v_pk, _ = _pack([var[k] for k in SMALL] + [no_state], SUBLANES)
    small_out = _adamw("adamw_replicated", all_parts, w_pk, m_pk, v_pk)
    small_shapes = [wt[k].shape for k in SMALL] + [grads["conv_w"].shape]
    small_res = [dict(zip(small_names, _unpack(o, spans, small_shapes))) for o in small_out]
    x_pos, y_pos, c_pos = (lax.axis_index(a) for a in MESH_AXES)
    cw_cols = c // N_DEV
    own_cw = lax.dynamic_slice_in_dim(small_res[0]["conv_w"], (4 * x_pos + 2 * y_pos + c_pos) * cw_cols, cw_cols, axis=1)
    res["conv_w"] = _adamw("adamw_conv_w", own_cw[None], wt["conv_w"], mom["conv_w"], var["conv_w"])

    outs = [loss, dx0.reshape(bsz, seq, d)]
    for kind in range(4):
        outs += [res[k][kind] if k in res else small_res[kind][k] for k in WEIGHTS]
    return tuple(outs)
```

```python
import collections
import functools
import math

import jax
import jax.numpy as jnp
from jax import lax
from jax.experimental import pallas as pl
from jax.experimental.pallas import tpu as pltpu

F32 = jnp.float32
BF16 = jnp.bfloat16

EPS = 1e-6
FFN_RES = 0.5
CONV_WIDTH = 31
CONV_HALO = 32
SSM_GROUP = 16
SSM_STATE = 64
ADAM_LR, ADAM_B1, ADAM_B2, ADAM_EPS, ADAM_WD, ADAM_STEP = 0.001, 0.9, 0.999, 1e-08, 0.01, 10

N_DEV = 8
MESH_AXES = ("x", "y", "c")
SUBLANES = 8
LANES = 128
PACK_W = 1024
V7X_VMEM_BYTES = 64 * 2**20
VMEM_LIMIT = V7X_VMEM_BYTES - 8 * 2**20

TILE = dict(row=256, mm_bytes=8 * 2**20, up_m=1024, up_n=256, wide_n=2048, conv_t=512, scan_t=256, scan_w=512,
            sum_bytes=4 * 2**20)

_GELU_K = math.sqrt(2.0 / math.pi)
_GELU_C = 0.044715


def _pick(n, target, mult):
    best = None
    for t in range(mult, min(n, target) + 1, mult):
        if n % t == 0:
            best = t
    return n if best is None else best


def _cparams(*sem):
    return pltpu.CompilerParams(dimension_semantics=sem, vmem_limit_bytes=VMEM_LIMIT)


def _sds(shape, dtype):
    return jax.ShapeDtypeStruct(shape, dtype)


def _call(name, body, grid, in_specs, out_specs, out_shape, operands, sem, scratch=(), exchange=None):
    if exchange is None:
        res = pl.pallas_call(body, name=name, grid=grid, in_specs=list(in_specs), out_specs=list(out_specs),
                             out_shape=list(out_shape), scratch_shapes=list(scratch),
                             compiler_params=_cparams(*sem))(*operands)
        return list(res), None
    n_in, n_out, n_scr = len(in_specs), len(out_specs), len(scratch)
    n_xin, n_xout = len(exchange.operands), len(exchange.out_shapes)
    hbm = pl.BlockSpec(memory_space=pltpu.HBM)

    def with_exchange(*refs):
        cuts, pos = [], 0
        for size in (n_in, n_xin, n_out, n_xout, n_scr):
            cuts.append(refs[pos:pos + size])
            pos += size
        ins, x_in, outs, x_out, scr = cuts
        sems = refs[pos:]
        ids = [pl.program_id(axis) for axis in range(len(grid))]
        first = functools.reduce(lambda p, q: p & q, [i == 0 for i in ids])
        last = functools.reduce(lambda p, q: p & q, [i == g - 1 for i, g in zip(ids, grid)])

        @pl.when(first)
        def _():
            exchange.start(x_in, x_out, sems)

        body(*ins, *outs, *scr)

        @pl.when(last)
        def _():
            exchange.finish(x_in, x_out, sems)

    res = pl.pallas_call(
        with_exchange, name=name, grid=grid, in_specs=list(in_specs) + [hbm] * n_xin,
        out_specs=list(out_specs) + [hbm] * n_xout, out_shape=list(out_shape) + list(exchange.out_shapes),
        scratch_shapes=list(scratch) + list(exchange.scratch),
        compiler_params=_cparams(*["arbitrary"] * len(grid)))(*operands, *exchange.operands)
    return list(res[:n_out]), list(res[n_out:])


def _dot(a, b):
    return jnp.dot(a, b, preferred_element_type=F32)


def _dot_nt(a, b):
    return lax.dot_general(a, b, (((1,), (1,)), ((), ())), preferred_element_type=F32)


def _dot_tn(a, b):
    return lax.dot_general(a, b, (((0,), (0,)), ((), ())), preferred_element_type=F32)


def _sigmoid(x):
    return 0.5 * jnp.tanh(0.5 * x) + 0.5


def _rms_stats(x):
    r = lax.rsqrt(jnp.mean(x * x, axis=-1, keepdims=True) + EPS)
    return r, x * r


def _rms_bwd(x, g, dy):
    r, xh = _rms_stats(x)
    dxh = dy * g
    dx = r * (dxh - xh * jnp.mean(dxh * xh, axis=-1, keepdims=True))
    return dx, jnp.sum(dy * xh, axis=0, keepdims=True)


def _rms_mm(name, x, g, ws, out_dtype):
    n, d = x.shape
    f = ws[0].shape[0]
    nw = len(ws)
    tm, tn = _pick(n, TILE["up_m"], 16), _pick(f, TILE["wide_n"], LANES)

    def body(x_ref, g_ref, *refs):
        w_refs, o_refs, h_ref = refs[:nw], refs[nw:2 * nw], refs[2 * nw]

        @pl.when(pl.program_id(1) == 0)
        def _():
            _, xh = _rms_stats(x_ref[...])
            h_ref[...] = (xh * g_ref[...]).astype(BF16)

        h = h_ref[...]
        for w_ref, o_ref in zip(w_refs, o_refs):
            o_ref[...] = _dot_nt(h, w_ref[...]).astype(o_ref.dtype)

    outs = pl.pallas_call(
        body, name=name, grid=(n // tm, f // tn),
        in_specs=[pl.BlockSpec((tm, d), lambda i, j: (i, 0)), pl.BlockSpec((1, d), lambda i, j: (0, 0))]
        + [pl.BlockSpec((tn, d), lambda i, j: (j, 0))] * nw,
        out_specs=[pl.BlockSpec((tm, tn), lambda i, j: (i, j))] * nw + [pl.BlockSpec((tm, d), lambda i, j: (i, 0))],
        out_shape=[_sds((n, f), out_dtype)] * nw + [_sds((n, d), BF16)],
        compiler_params=_cparams("parallel", "arbitrary"),
    )(x, g, *ws)
    return outs[:nw], outs[nw]


def _ffn_fwd(name, x, g, w1t, w3t, w2, exchange=None):
    n, d = x.shape
    f = w2.shape[0]
    tm, tn = _pick(n, TILE["row"], 16), _pick(f, TILE["up_n"], LANES)

    def body(x_ref, g_ref, w1_ref, w3_ref, w2_ref, o_ref, a_ref, b_ref, h_ref):
        xv = x_ref[...]
        _, xh = _rms_stats(xv)
        h = (xh * g_ref[...]).astype(BF16)
        h_ref[...] = h
        acc = None
        for c0 in range(0, f, tn):
            cols = pl.ds(c0, tn)
            av, bv = _dot_nt(h, w1_ref[cols, :]), _dot_nt(h, w3_ref[cols, :])
            a_ref[:, cols] = av.astype(BF16)
            b_ref[:, cols] = bv.astype(BF16)
            t = _dot((av * _sigmoid(av) * bv).astype(BF16), w2_ref[cols, :])
            acc = t if acc is None else acc + t
        o_ref[...] = xv + FFN_RES * acc

    row = pl.BlockSpec((tm, d), lambda i: (i, 0))
    wide = pl.BlockSpec((tm, f), lambda i: (i, 0))
    held = pl.BlockSpec((f, d), lambda i: (0, 0), pipeline_mode=pl.Buffered(1))
    return _call(
        name, body, (n // tm,), [row, pl.BlockSpec((1, d), lambda i: (0, 0)), held, held, held], [row, wide, wide, row],
        [_sds((n, d), F32), _sds((n, f), BF16), _sds((n, f), BF16), _sds((n, d), BF16)],
        (x, g, w1t, w3t, w2), ("parallel",), exchange=exchange)


def _ffn_bwd_hidden(name, dxo, a, b, w2, exchange=None):
    n, d = dxo.shape
    f = a.shape[1]
    tm, tn = _pick(n, TILE["row"], 16), _pick(f, TILE["up_n"], LANES)

    def body(dx_ref, a_ref, b_ref, w_ref, da_ref, db_ref, hid_ref, dxh_ref):
        dxh = (FFN_RES * dx_ref[...]).astype(BF16)
        dxh_ref[...] = dxh
        for c0 in range(0, f, tn):
            cols = pl.ds(c0, tn)
            dhid = _dot_nt(dxh, w_ref[cols, :])
            av, bv = a_ref[:, cols].astype(F32), b_ref[:, cols].astype(F32)
            sig = _sigmoid(av)
            silu = av * sig
            da_ref[:, cols] = (dhid * bv * (sig * (1.0 + av * (1.0 - sig)))).astype(BF16)
            db_ref[:, cols] = (dhid * silu).astype(BF16)
            hid_ref[:, cols] = (silu * bv).astype(BF16)

    wide = pl.BlockSpec((tm, f), lambda i: (i, 0))
    row = pl.BlockSpec((tm, d), lambda i: (i, 0))
    return _call(
        name, body, (n // tm,), [row, wide, wide, pl.BlockSpec((f, d), lambda i: (0, 0))], [wide, wide, wide, row],
        [_sds((n, f), BF16)] * 3 + [_sds((n, d), BF16)], (dxo, a, b, w2), ("parallel",), exchange=exchange)


def _loss_head(name, x, g, target):
    n, d = x.shape
    tm = _pick(n, TILE["row"], SUBLANES)

    def body(x_ref, g_ref, t_ref, dx_ref, loss_ref, dg_ref):
        @pl.when(pl.program_id(0) == 0)
        def _():
            loss_ref[...] = jnp.zeros_like(loss_ref)
            dg_ref[...] = jnp.zeros_like(dg_ref)

        xv, gv = x_ref[...], g_ref[...]
        r, xh = _rms_stats(xv)
        err = xh * gv - t_ref[...]
        loss_ref[...] += 0.5 * jnp.sum(jnp.mean(err * err, axis=-1, keepdims=True))
        dy = err * (1.0 / d)
        dxh = dy * gv
        dx_ref[...] = r * (dxh - xh * jnp.mean(dxh * xh, axis=-1, keepdims=True))
        dg_ref[...] += jnp.sum(dy * xh, axis=0, keepdims=True)

    return pl.pallas_call(
        body, name=name, grid=(n // tm,),
        in_specs=[pl.BlockSpec((tm, d), lambda i: (i, 0)), pl.BlockSpec((1, d), lambda i: (0, 0)),
                  pl.BlockSpec((tm, d), lambda i: (i, 0))],
        out_specs=[pl.BlockSpec((tm, d), lambda i: (i, 0)), pl.BlockSpec((SUBLANES, LANES), lambda i: (0, 0)),
                   pl.BlockSpec((1, d), lambda i: (0, 0))],
        out_shape=[_sds((n, d), F32), _sds((SUBLANES, LANES), F32), _sds((1, d), F32)],
        compiler_params=_cparams("arbitrary"),
    )(x, g, target)


def _dx_rms_bwd(name, pairs, dxo, x, g, exchange=None):
    n, dm = x.shape
    tm = _pick(n, TILE["row"], 16)
    npair = len(pairs)

    def body(*refs):
        d_refs, w_refs = refs[:npair], refs[npair:2 * npair]
        dxo_ref, x_ref, g_ref, dx_ref, dg_ref = refs[2 * npair:]

        @pl.when(pl.program_id(0) == 0)
        def _():
            dg_ref[...] = jnp.zeros_like(dg_ref)

        dh = None
        for d_ref, w_ref in zip(d_refs, w_refs):
            t = _dot(d_ref[...].astype(BF16), w_ref[...])
            dh = t if dh is None else dh + t
        dx, dg = _rms_bwd(x_ref[...], g_ref[...], dh)
        dx_ref[...] = dxo_ref[...] + dx
        dg_ref[...] += dg

    row = pl.BlockSpec((tm, dm), lambda i: (i, 0))
    d_specs = [pl.BlockSpec((tm, p[1]), functools.partial(lambda i, cb: (i, cb), cb=p[2])) for p in pairs]
    w_specs = [pl.BlockSpec((p[4], dm), functools.partial(lambda i, rb: (rb, 0), rb=p[5])) for p in pairs]
    return _call(
        name, body, (n // tm,), d_specs + w_specs + [row, row, pl.BlockSpec((1, dm), lambda i: (0, 0))],
        [row, pl.BlockSpec((1, dm), lambda i: (0, 0))], [_sds((n, dm), F32), _sds((1, dm), F32)],
        (*[p[0] for p in pairs], *[p[3] for p in pairs], dxo, x, g), ("arbitrary",), exchange=exchange)


def _mm_tn(name, a, b, a_cols=None, b_cols=None):
    n = a.shape[0]
    a0, ma = a_cols if a_cols else (0, a.shape[1])
    b0, mb = b_cols if b_cols else (0, b.shape[1])
    assert a0 % ma == 0 and b0 % mb == 0
    ab, bb = a0 // ma, b0 // mb
    tk = _pick(n, TILE["mm_bytes"] // (ma * a.dtype.itemsize + mb * b.dtype.itemsize), 16)

    def body(a_ref, b_ref, o_ref):
        @pl.when(pl.program_id(0) == 0)
        def _():
            o_ref[...] = jnp.zeros_like(o_ref)

        o_ref[...] += _dot_tn(a_ref[...].astype(BF16), b_ref[...].astype(BF16))

    return pl.pallas_call(
        body, name=name, grid=(n // tk,),
        in_specs=[pl.BlockSpec((tk, ma), lambda k: (k, ab)), pl.BlockSpec((tk, mb), lambda k: (k, bb))],
        out_specs=pl.BlockSpec((ma, mb), lambda k: (0, 0)),
        out_shape=_sds((ma, mb), F32),
        compiler_params=_cparams("arbitrary"),
    )(a, b)


def _row_mm(name, pairs, out_w, out_dtype, add=None):
    n = pairs[0][0].shape[0]
    tm = _pick(n, TILE["row"], 16)
    npair = len(pairs)

    def body(*refs):
        a_refs, w_refs = refs[:npair], refs[npair:2 * npair]
        add_ref = refs[2 * npair] if add is not None else None
        o_ref = refs[-1]
        acc = None
        for a_ref, w_ref, p in zip(a_refs, w_refs, pairs):
            av = a_ref[...].astype(BF16)
            t = _dot_nt(av, w_ref[...]) if p[6] else _dot(av, w_ref[...])
            acc = t if acc is None else acc + t
        if add_ref is not None:
            acc = acc + add_ref[...].astype(F32)
        o_ref[...] = acc.astype(o_ref.dtype)

    a_specs = [pl.BlockSpec((tm, p[1]), functools.partial(lambda i, cb: (i, cb), cb=p[2])) for p in pairs]
    w_specs = [pl.BlockSpec((p[4], p[3].shape[1]), functools.partial(lambda i, rb: (rb, 0), rb=p[5])) for p in pairs]
    add_specs = [pl.BlockSpec((tm, out_w), lambda i: (i, 0))] if add is not None else []
    return pl.pallas_call(
        body, name=name, grid=(n // tm,),
        in_specs=a_specs + w_specs + add_specs,
        out_specs=pl.BlockSpec((tm, out_w), lambda i: (i, 0)),
        out_shape=_sds((n, out_w), out_dtype),
        compiler_params=_cparams("parallel"),
    )(*[p[0] for p in pairs], *[p[3] for p in pairs], *([add] if add is not None else []))


def _conv_post(c, ln_g, ln_b, out_g):
    mu = jnp.mean(c, axis=-1, keepdims=True)
    xc = c - mu
    rstd = lax.rsqrt(jnp.mean(xc * xc, axis=-1, keepdims=True) + EPS)
    nrm = xc * rstd
    l = nrm * ln_g + ln_b
    sig = _sigmoid(l)
    s = l * sig
    r, sh = _rms_stats(s)
    return sh * out_g, (rstd, nrm, l, sig, r, sh)


def _tap_groups(first):
    groups = []
    for r in range(SUBLANES):
        taps = [(s - r, s - first) for s in range(first, first + CONV_WIDTH) if s % SUBLANES == r]
        if taps:
            groups.append((r, taps))
    return groups


def _conv_taps(a_ref, w_ref, b_ref, first, rows, flip=False):
    acc = None
    for r, taps in _tap_groups(first):
        ext = rows if r == 0 else rows + SUBLANES
        part = None
        for base, k in taps:
            kk = CONV_WIDTH - 1 - k if flip else k
            t = w_ref[kk:kk + 1, :] * a_ref[pl.ds(base, ext), :]
            part = t if part is None else part + t
        if r:
            b_ref[...] = part
            part = b_ref[pl.ds(r, rows), :]
        acc = part if acc is None else acc + part
    return acc


def _conv_post_bwd(cv, dout, ln_g, ln_b, out_g):
    _, (rstd, nrm, l, sig, r, sh) = _conv_post(cv, ln_g, ln_b, out_g)
    dsh = dout * out_g
    ds = r * (dsh - sh * jnp.mean(dsh * sh, axis=-1, keepdims=True))
    dl = ds * (sig * (1.0 + l * (1.0 - sig)))
    dn = dl * ln_g
    dc = rstd * (dn - jnp.mean(dn, axis=-1, keepdims=True) - nrm * jnp.mean(dn * nrm, axis=-1, keepdims=True))
    col_sum = lambda t: jnp.sum(t, axis=0, keepdims=True)
    return dc, col_sum(dout * sh), col_sum(dl * nrm), col_sum(dl)


def _conv_fwd(name, proj3, conv_w, conv_b, ln_g, ln_b, out_g):
    bsz, seq, _ = proj3.shape
    c = conv_w.shape[1]
    tt = _pick(seq, TILE["conv_t"], CONV_HALO)
    hb = tt // CONV_HALO
    first = CONV_HALO - (CONV_WIDTH - 1)

    def body(v_ref, g_ref, vp_ref, gp_ref, w_ref, cb_ref, lg_ref, lb_ref, og_ref, o_ref, cv_ref, a_ref, b_ref):
        keep = (pl.program_id(1) > 0).astype(F32)
        a_ref[pl.ds(0, CONV_HALO), :] = keep * vp_ref[0] * _sigmoid(gp_ref[0])
        a_ref[pl.ds(CONV_HALO, tt), :] = v_ref[0] * _sigmoid(g_ref[0])
        cv = _conv_taps(a_ref, w_ref, b_ref, first, tt) + cb_ref[...]
        cv_ref[0] = cv
        out, _ = _conv_post(cv, lg_ref[...], lb_ref[...], og_ref[...])
        o_ref[0] = out.astype(BF16)

    vec = pl.BlockSpec((1, c), lambda b, i: (0, 0))
    prev = lambda col: pl.BlockSpec((1, CONV_HALO, c), lambda b, i: (b, jnp.maximum(i * hb - 1, 0), col))
    tile = pl.BlockSpec((1, tt, c), lambda b, i: (b, i, 0))
    return pl.pallas_call(
        body, name=name, grid=(bsz, seq // tt),
        in_specs=[tile, pl.BlockSpec((1, tt, c), lambda b, i: (b, i, 1)),
                  prev(0), prev(1), pl.BlockSpec(conv_w.shape, lambda b, i: (0, 0)), vec, vec, vec, vec],
        out_specs=[tile, tile],
        out_shape=[_sds((bsz, seq, c), BF16), _sds((bsz, seq, c), F32)],
        scratch_shapes=[pltpu.VMEM((CONV_HALO + tt, c), F32), pltpu.VMEM((tt + SUBLANES, c), F32)],
        compiler_params=_cparams("parallel", "arbitrary"),
    )(proj3, proj3, proj3, proj3, conv_w, conv_b, ln_g, ln_b, out_g)


def _conv_bwd(name, dmix3, proj3, cv3, conv_w, ln_g, ln_b, out_g):
    bsz, seq, _ = proj3.shape
    c = conv_w.shape[1]
    tt = _pick(seq, TILE["conv_t"], CONV_HALO)
    hb = tt // CONV_HALO
    nt = seq // tt
    last_hb = seq // CONV_HALO - 1
    ext = tt + CONV_HALO
    first = CONV_HALO - (CONV_WIDTH - 1)

    def body(v_ref, g_ref, vp_ref, gp_ref, cv_ref, cvn_ref, d_ref, dn_ref, w_ref, lg_ref, lb_ref, og_ref,
             o_ref, dw_ref, dcb_ref, dlg_ref, dlb_ref, dog_ref, a_ref, dc_ref, b_ref, ds_ref):
        i = pl.program_id(1)

        @pl.when((pl.program_id(0) == 0) & (i == 0))
        def _():
            for r in (dw_ref, dcb_ref, dlg_ref, dlb_ref, dog_ref):
                r[...] = jnp.zeros_like(r)

        keep_prev = (i > 0).astype(F32)
        keep_next = (i < nt - 1).astype(F32)
        sig_g = _sigmoid(g_ref[0])
        a_ref[pl.ds(0, CONV_HALO), :] = keep_prev * vp_ref[0] * _sigmoid(gp_ref[0])
        a_ref[pl.ds(CONV_HALO, tt), :] = v_ref[0] * sig_g

        lg, lb, og = lg_ref[...], lb_ref[...], og_ref[...]
        dc_own, d_og, d_lg, d_lb = _conv_post_bwd(cv_ref[0], d_ref[0], lg, lb, og)
        dc_next, _, _, _ = _conv_post_bwd(cvn_ref[0], keep_next * dn_ref[0], lg, lb, og)
        dog_ref[...] += d_og
        dlg_ref[...] += d_lg
        dlb_ref[...] += d_lb
        dcb_ref[...] += jnp.sum(dc_own, axis=0, keepdims=True)
        dc_ref[pl.ds(0, tt), :] = dc_own
        dc_ref[pl.ds(tt, CONV_HALO), :] = dc_next

        da = _conv_taps(dc_ref, w_ref, b_ref, 0, tt, flip=True)

        for r, taps in _tap_groups(first):
            if r:
                ds_ref[pl.ds(0, SUBLANES), :] = jnp.zeros((SUBLANES, c), F32)
                ds_ref[pl.ds(tt, SUBLANES), :] = jnp.zeros((SUBLANES, c), F32)
                ds_ref[pl.ds(r, tt), :] = dc_own
            for base, k in taps:
                prod = (ds_ref[...] * a_ref[pl.ds(base, tt + SUBLANES), :]) if r else (dc_own * a_ref[pl.ds(base, tt), :])
                dw_ref[k:k + 1, :] += jnp.sum(prod, axis=0, keepdims=True)
        val = v_ref[0]
        o_ref[0] = jnp.concatenate([da * sig_g, da * val * sig_g * (1.0 - sig_g)], axis=-1).astype(BF16)

    vec = pl.BlockSpec((1, c), lambda b, i: (0, 0))
    cur = lambda col: pl.BlockSpec((1, tt, c), lambda b, i: (b, i, col))
    prev = lambda col: pl.BlockSpec((1, CONV_HALO, c), lambda b, i: (b, jnp.maximum(i * hb - 1, 0), col))
    nxt = lambda col: pl.BlockSpec((1, CONV_HALO, c), lambda b, i: (b, jnp.minimum((i + 1) * hb, last_hb), col))
    wspec = pl.BlockSpec(conv_w.shape, lambda b, i: (0, 0))
    return pl.pallas_call(
        body, name=name, grid=(bsz, nt),
        in_specs=[cur(0), cur(1), prev(0), prev(1), cur(0), nxt(0), cur(0), nxt(0), wspec, vec, vec, vec],
        out_specs=[pl.BlockSpec((1, tt, 2 * c), lambda b, i: (b, i, 0)), wspec, vec, vec, vec, vec],
        out_shape=[_sds((bsz, seq, 2 * c), BF16), _sds(conv_w.shape, F32)] + [_sds((1, c), F32)] * 4,
        scratch_shapes=[pltpu.VMEM((CONV_HALO + tt, c), F32), pltpu.VMEM((ext, c), F32),
                        pltpu.VMEM((tt + SUBLANES, c), F32), pltpu.VMEM((tt + SUBLANES, c), F32)],
        compiler_params=_cparams("arbitrary", "arbitrary"),
    )(proj3, proj3, proj3, proj3, cv3, cv3, dmix3, dmix3, conv_w, ln_g, ln_b, out_g)


def _ssm_discretise(a_re, a_im, log_dt):
    dt = jnp.exp(log_dt)
    zr, zi = a_re * dt, a_im * dt
    mag = jnp.exp(zr)
    ar, ai = mag * jnp.cos(zi), mag * jnp.sin(zi)
    den = a_re * a_re + a_im * a_im
    nr = ar - 1.0
    return ar, ai, (nr * a_re + ai * a_im) / den, (ai * a_re - nr * a_im) / den


def _ssm_system(a_re, a_im, log_dt, a_re_x, a_im_x, log_dt_x, bt_re, bt_im):
    ar, ai, _, _ = _ssm_discretise(a_re, a_im, log_dt)
    _, _, cr, ci = _ssm_discretise(a_re_x, a_im_x, log_dt_x)
    return ar, ai, cr * bt_re - ci * bt_im, cr * bt_im + ci * bt_re


def _ssm_prep(name, prim):
    g, p = prim[0].shape

    def body(*refs):
        pwr_ref, pwi_ref, bbr_ref, bbi_ref = refs[8:]
        ar, ai, bbr, bbi = _ssm_system(*[r[...] for r in refs[:8]])
        bbr_ref[...] = bbr
        bbi_ref[...] = bbi
        pr, pi = ar, ai
        for k in range(SUBLANES):
            pwr_ref[k] = pr
            pwi_ref[k] = pi
            pr, pi = pr * ar - pi * ai, pr * ai + pi * ar

    return pl.pallas_call(
        body, name=name,
        out_shape=[_sds((SUBLANES, g, p), F32)] * 2 + [_sds(prim[6].shape, F32)] * 2,
        compiler_params=pltpu.CompilerParams(vmem_limit_bytes=VMEM_LIMIT),
    )(*prim)


def _ssm_param_grads(name, prim, dab_r, dab_i, dbb_r, dbb_i):
    g, p = prim[0].shape
    h = prim[6].shape[0] // g

    def body(*refs):
        dar_ref, dai_ref, dbr_ref, dbi_ref = refs[8:12]
        o_ar, o_ai, o_dt, o_br, o_bi = refs[12:]
        _, vjp = jax.vjp(_ssm_system, *[r[...] for r in refs[:8]])
        ct = (jnp.sum(dar_ref[...], axis=0), jnp.sum(dai_ref[...], axis=0), dbr_ref[...], dbi_ref[...])
        d_ar, d_ai, d_dt, d_arx, d_aix, d_dtx, d_br, d_bi = vjp(ct)
        per_group = lambda t: jnp.sum(t.reshape(g, h, p), axis=1)
        o_ar[...] = d_ar + per_group(d_arx)
        o_ai[...] = d_ai + per_group(d_aix)
        o_dt[...] = d_dt + jnp.sum(per_group(d_dtx), axis=1, keepdims=True)
        o_br[...] = d_br
        o_bi[...] = d_bi

    return pl.pallas_call(
        body, name=name,
        out_shape=[_sds(prim[k].shape, F32) for k in (0, 1, 2, 6, 7)],
        compiler_params=pltpu.CompilerParams(vmem_limit_bytes=VMEM_LIMIT),
    )(*prim, dab_r, dab_i, dbb_r, dbb_i)


def _cfma(xr, xi, cr, ci, sr, si):
    return xr + (cr * sr - ci * si), xi + (cr * si + ci * sr)


def _scan_tables(pw_r, pw_i, reverse):
    gp = pw_r.shape[1] * pw_r.shape[2]
    pr, pi = pw_r.reshape(SUBLANES, gp), pw_i.reshape(SUBLANES, gp)
    if reverse:
        pi = -pi
    row = jnp.arange(SUBLANES)[:, None]
    tabs = []
    for d in (1, 2, 4):
        keep = (row < SUBLANES - d) if reverse else (row >= d)
        tabs += [jnp.where(keep, pr[d - 1][None, :], 0.0), jnp.where(keep, pi[d - 1][None, :], 0.0)]
    tabs += [pr[::-1], pi[::-1]] if reverse else [pr, pi]
    return jnp.concatenate(tabs, axis=0)


MXU_DEPTH = 256


def _bands(c, gp):
    bw = min(c, MXU_DEPTH)
    return c // bw, bw, gp * bw // c


def _band_expand(rows16, w_ref, put, c, gp):
    nb, bw, sw = _bands(c, gp)
    for s in range(nb):
        band = rows16[:, s * bw:(s + 1) * bw]
        for half in (0, gp):
            cols = pl.ds(half + s * sw, sw)
            put(cols, _dot(band, w_ref[pl.ds(s * bw, bw), cols]))


def _band_contract(get16, w_ref, c, gp):
    nb, bw, sw = _bands(c, gp)
    out = []
    for s in range(nb):
        acc = None
        for half in (0, gp):
            cols = pl.ds(half + s * sw, sw)
            t = _dot_nt(get16(cols), w_ref[pl.ds(s * bw, bw), cols])
            acc = t if acc is None else acc + t
        out.append(acc)
    return out[0] if nb == 1 else jnp.concatenate(out, axis=1)


def _band_wgrad(name, a, a_block, c, b):
    n = a.shape[0]
    gp = b.shape[1] // 2
    nb, bw, sw = _bands(c, gp)
    tk = _pick(n, TILE["mm_bytes"] // (c * a.dtype.itemsize + 2 * gp * b.dtype.itemsize), 16)

    def body(a_ref, b_ref, o_ref):
        @pl.when(pl.program_id(0) == 0)
        def _():
            o_ref[...] = jnp.zeros_like(o_ref)

        for s in range(nb):
            band = a_ref[:, s * bw:(s + 1) * bw].astype(BF16)
            for h, half in enumerate((0, gp)):
                o_ref[pl.ds(s * bw, bw), pl.ds(h * sw, sw)] += _dot_tn(
                    band, b_ref[:, pl.ds(half + s * sw, sw)].astype(BF16))

    return pl.pallas_call(
        body, name=name, grid=(n // tk,),
        in_specs=[pl.BlockSpec((tk, c), lambda k: (k, a_block)), pl.BlockSpec((tk, 2 * gp), lambda k: (k, 0))],
        out_specs=pl.BlockSpec((c, 2 * sw), lambda k: (0, 0)),
        out_shape=_sds((c, 2 * sw), F32),
        compiler_params=_cparams("arbitrary"),
    )(a, b)


def _band_diag_take(comp, half, c, gp):
    nb, bw, sw = _bands(c, gp)
    return jnp.concatenate([_block_diag_take(comp[s * bw:(s + 1) * bw, half * sw:(half + 1) * sw], bw // SSM_GROUP)
                            for s in range(nb)], axis=0)


def _scan_fwd(name, tab, proj3, u_block, bbd, cdt):
    bsz, seq, _ = proj3.shape
    c, w = bbd.shape
    gp = w // 2
    tt = _pick(seq, TILE["scan_t"], 16)
    nblk = tt // SUBLANES
    cw = _pick(gp, TILE["scan_w"], LANES)

    def body(tab_ref, u_ref, bbd_ref, cdt_ref, xs_ref, y_ref, carry_ref, bu_ref):
        @pl.when(pl.program_id(1) == 0)
        def _():
            carry_ref[...] = jnp.zeros_like(carry_ref)

        def put_bu(cols, val):
            bu_ref[0, :, cols] = val

        _band_expand(u_ref[0].astype(BF16), bbd_ref, put_bu, c, gp)

        for ch in range(gp // cw):
            re, im = pl.ds(ch * cw, cw), pl.ds(gp + ch * cw, cw)

            def blk(r, carry, re=re, im=im):
                tabs = [tab_ref[pl.ds(SUBLANES * k, SUBLANES), re] for k in range(8)]
                rows = pl.ds(pl.multiple_of(r * SUBLANES, SUBLANES), SUBLANES)
                xr, xi = bu_ref[0, rows, re], bu_ref[0, rows, im]
                for j, d in enumerate((1, 2, 4)):
                    xr, xi = _cfma(xr, xi, tabs[2 * j], tabs[2 * j + 1], pltpu.roll(xr, d, 0), pltpu.roll(xi, d, 0))
                xr, xi = _cfma(xr, xi, tabs[6], tabs[7], carry[0], carry[1])
                xs_ref[0, rows, re] = xr
                xs_ref[0, rows, im] = xi
                last = SUBLANES - 1
                return (jnp.broadcast_to(xr[last:, :], xr.shape), jnp.broadcast_to(xi[last:, :], xi.shape))

            cr, ci = lax.fori_loop(0, nblk, blk, (carry_ref[:, re], carry_ref[:, im]))
            carry_ref[:, re] = cr
            carry_ref[:, im] = ci

        y_ref[0] = _band_contract(lambda cols: xs_ref[0, :, cols].astype(BF16), cdt_ref, c, gp)

    whole = lambda arr: pl.BlockSpec(arr.shape, lambda b, t: (0, 0))
    return pl.pallas_call(
        body, name=name, grid=(bsz, seq // tt),
        in_specs=[whole(tab), pl.BlockSpec((1, tt, c), lambda b, t: (b, t, u_block)), whole(bbd), whole(cdt)],
        out_specs=[pl.BlockSpec((1, tt, w), lambda b, t: (b, t, 0)), pl.BlockSpec((1, tt, c), lambda b, t: (b, t, 0))],
        out_shape=[_sds((bsz, seq, w), F32), _sds((bsz, seq, c), F32)],
        scratch_shapes=[pltpu.VMEM((SUBLANES, w), F32), pltpu.VMEM((1, tt, w), F32)],
        compiler_params=_cparams("arbitrary", "arbitrary"),
    )(tab, proj3, bbd, cdt)


def _scan_bwd(name, tab, dy3, xs3, du_skip3, bbd, cdt, exchange=None):
    bsz, seq, w = xs3.shape
    c = bbd.shape[0]
    gp = w // 2
    tt = _pick(seq, TILE["scan_t"], 16)
    nblk = tt // SUBLANES
    cw = _pick(gp, TILE["scan_w"], LANES)
    nt = seq // tt

    def body(tab_ref, dy_ref, xs_ref, halo_ref, skip_ref, bbd_ref, cdt_ref, lam_ref, du_ref, dar_ref, dai_ref,
             carry_ref, g_ref):
        t = pl.program_id(1)

        @pl.when(t == 0)
        def _():
            carry_ref[...] = jnp.zeros_like(carry_ref)

        @pl.when((pl.program_id(0) == 0) & (t == 0))
        def _():
            dar_ref[...] = jnp.zeros_like(dar_ref)
            dai_ref[...] = jnp.zeros_like(dai_ref)

        def put_g(cols, val):
            g_ref[0, :, cols] = val

        _band_expand(dy_ref[0], cdt_ref, put_g, c, gp)

        has_prev = (t < nt - 1).astype(F32)
        row0 = lax.broadcasted_iota(jnp.int32, (SUBLANES, cw), 0) == 0
        last = SUBLANES - 1

        for ch in range(gp // cw):
            re, im = pl.ds(ch * cw, cw), pl.ds(gp + ch * cw, cw)

            def step(rows, xm1r, xm1i, state, re=re, im=im):
                tabs = [tab_ref[pl.ds(SUBLANES * k, SUBLANES), re] for k in range(8)]
                cr, ci, accr, acci = state
                lr, li = g_ref[0, rows, re], g_ref[0, rows, im]
                for j, d in enumerate((1, 2, 4)):
                    lr, li = _cfma(lr, li, tabs[2 * j], tabs[2 * j + 1],
                                   pltpu.roll(lr, SUBLANES - d, 0), pltpu.roll(li, SUBLANES - d, 0))
                lr, li = _cfma(lr, li, tabs[6], tabs[7], cr, ci)
                lam_ref[0, rows, re] = lr
                lam_ref[0, rows, im] = li
                xr, xi = xs_ref[0, rows, re], xs_ref[0, rows, im]
                xpr = jnp.where(row0, jnp.broadcast_to(xm1r[last:, :], xr.shape), pltpu.roll(xr, 1, 0))
                xpi = jnp.where(row0, jnp.broadcast_to(xm1i[last:, :], xi.shape), pltpu.roll(xi, 1, 0))
                accr = accr + (lr * xpr + li * xpi)
                acci = acci + (li * xpr - lr * xpi)
                return (jnp.broadcast_to(lr[:1, :], lr.shape), jnp.broadcast_to(li[:1, :], li.shape), accr, acci)

            def blk(k, state, re=re, im=im, step=step):
                r = nblk - 1 - k
                rows = pl.ds(pl.multiple_of(r * SUBLANES, SUBLANES), SUBLANES)
                prev = pl.ds(pl.multiple_of((r - 1) * SUBLANES, SUBLANES), SUBLANES)
                return step(rows, xs_ref[0, prev, re], xs_ref[0, prev, im], state)

            zero = jnp.zeros((SUBLANES, cw), F32)
            state = lax.fori_loop(0, nblk - 1, blk, (carry_ref[:, re], carry_ref[:, im], zero, zero))
            cr, ci, accr, acci = step(pl.ds(0, SUBLANES), has_prev * halo_ref[0, :, re], has_prev * halo_ref[0, :, im], state)
            carry_ref[:, re] = cr
            carry_ref[:, im] = ci
            dar_ref[:, re] += accr
            dai_ref[:, re] += acci

        du = _band_contract(lambda cols: lam_ref[0, :, cols].astype(BF16), bbd_ref, c, gp)
        du_ref[0] = (du + skip_ref[0]).astype(BF16)

    tile = pl.BlockSpec((1, tt, w), lambda b, t: (b, nt - 1 - t, 0))
    thin = pl.BlockSpec((1, tt, c), lambda b, t: (b, nt - 1 - t, 0))
    halo = pl.BlockSpec((1, SUBLANES, w), lambda b, t: (b, jnp.maximum((nt - 1 - t) * nblk - 1, 0), 0))
    acc = pl.BlockSpec((SUBLANES, gp), lambda b, t: (0, 0))
    whole = lambda arr: pl.BlockSpec(arr.shape, lambda b, t: (0, 0))
    return _call(
        name, body, (bsz, nt), [whole(tab), thin, tile, halo, thin, whole(bbd), whole(cdt)], [tile, thin, acc, acc],
        [_sds(xs3.shape, F32), _sds((bsz, seq, c), BF16), _sds((SUBLANES, gp), F32), _sds((SUBLANES, gp), F32)],
        (tab, dy3, xs3, xs3, du_skip3, bbd, cdt), ("arbitrary", "arbitrary"),
        scratch=[pltpu.VMEM((SUBLANES, w), F32), pltpu.VMEM((1, tt, w), F32)], exchange=exchange)


def _gelu_parts(y):
    inner = _GELU_K * (y + _GELU_C * y * y * y)
    t = jnp.tanh(inner)
    return 0.5 * y * (1.0 + t), t


def _ssm_out_fwd(name, cx, proj, u_block, d_skip, glu_w, glu_b, out_g):
    n, c = cx.shape
    tm = _pick(n, TILE["row"], 16)

    def body(cx_ref, u_ref, d_ref, gw_ref, gb_ref, og_ref, y_ref, o_ref):
        y = cx_ref[...] + d_ref[...] * u_ref[...]
        y_ref[...] = y
        gy, _ = _gelu_parts(y)
        z = _dot(gy.astype(BF16), gw_ref[...]) + gb_ref[...]
        _, sh = _rms_stats(gy * _sigmoid(z))
        o_ref[...] = (sh * og_ref[...]).astype(BF16)

    vec = pl.BlockSpec((1, c), lambda i: (0, 0))
    row = pl.BlockSpec((tm, c), lambda i: (i, 0))
    return pl.pallas_call(
        body, name=name, grid=(n // tm,),
        in_specs=[row, pl.BlockSpec((tm, c), lambda i: (i, u_block)), vec, pl.BlockSpec(glu_w.shape, lambda i: (0, 0)),
                  vec, vec],
        out_specs=[row, row],
        out_shape=[_sds((n, c), F32), _sds((n, c), BF16)],
        compiler_params=_cparams("parallel"),
    )(cx, proj, d_skip, glu_w, glu_b, out_g)


def _ssm_out_bwd(name, dmix, d_block, y, proj, u_block, d_skip, glu_w, glu_b, out_g):
    n, c = y.shape
    tm = _pick(n, TILE["row"], 16)

    def body(d_ref, y_ref, u_ref, dk_ref, gw_ref, gb_ref, og_ref, dy_ref, du_ref, dgw_ref, dgb_ref, dog_ref, dd_ref):
        @pl.when(pl.program_id(0) == 0)
        def _():
            for r in (dgw_ref, dgb_ref, dog_ref, dd_ref):
                r[...] = jnp.zeros_like(r)

        yv = y_ref[...]
        gy, th = _gelu_parts(yv)
        gy16 = gy.astype(BF16)
        sz = _sigmoid(_dot(gy16, gw_ref[...]) + gb_ref[...])
        r, sh = _rms_stats(gy * sz)
        dout = d_ref[...]
        dog_ref[...] += jnp.sum(dout * sh, axis=0, keepdims=True)
        dsh = dout * og_ref[...]
        ds = r * (dsh - sh * jnp.mean(dsh * sh, axis=-1, keepdims=True))
        dz = ds * gy * sz * (1.0 - sz)
        dz16 = dz.astype(BF16)
        dgb_ref[...] += jnp.sum(dz, axis=0, keepdims=True)
        dgw_ref[...] += _dot_tn(gy16, dz16)
        dgy = ds * sz + _dot_nt(dz16, gw_ref[...])
        dgelu = 0.5 * (1.0 + th) + 0.5 * yv * (1.0 - th * th) * (_GELU_K * (1.0 + 3.0 * _GELU_C * yv * yv))
        dy = dgy * dgelu
        dy_ref[...] = dy.astype(BF16)
        du_ref[...] = dy * dk_ref[...]
        dd_ref[...] += jnp.sum(dy * u_ref[...], axis=0, keepdims=True)

    vec = pl.BlockSpec((1, c), lambda i: (0, 0))
    row = pl.BlockSpec((tm, c), lambda i: (i, 0))
    mat = pl.BlockSpec(glu_w.shape, lambda i: (0, 0))
    return pl.pallas_call(
        body, name=name, grid=(n // tm,),
        in_specs=[pl.BlockSpec((tm, c), lambda i: (i, d_block)), row, pl.BlockSpec((tm, c), lambda i: (i, u_block)),
                  vec, mat, vec, vec],
        out_specs=[row, row, mat, vec, vec, vec],
        out_shape=[_sds((n, c), BF16), _sds((n, c), F32), _sds(glu_w.shape, F32)] + [_sds((1, c), F32)] * 3,
        compiler_params=_cparams("arbitrary"),
    )(dmix, y, proj, d_skip, glu_w, glu_b, out_g)


def _mesh_pos():
    return tuple(lax.axis_index(a) for a in MESH_AXES)


def _other_chips(x, y):
    return [(1 - x, y), (x, 1 - y), (1 - x, 1 - y)]


def _remote(src, dst, send_sem, recv_sem, dev):
    return pltpu.make_async_remote_copy(src_ref=src, dst_ref=dst, send_sem=send_sem, recv_sem=recv_sem,
                                        device_id=dev, device_id_type=pl.DeviceIdType.MESH)


def _hbm_call(name, body, operands, out_shapes, scratch):
    hbm = pl.BlockSpec(memory_space=pltpu.HBM)
    return pl.pallas_call(body, name=name, in_specs=[hbm] * len(operands), out_specs=[hbm] * len(out_shapes),
                          out_shape=out_shapes, scratch_shapes=scratch)(*operands)


_Exchange = collections.namedtuple("_Exchange", "operands out_shapes scratch start finish")


def _run_exchange(name, plan):
    nin, nout = len(plan.operands), len(plan.out_shapes)

    def body(*refs):
        parts = refs[:nin], refs[nin:nin + nout], refs[nin + nout:]
        plan.start(*parts)
        plan.finish(*parts)

    return _hbm_call(name, body, plan.operands, plan.out_shapes, plan.scratch)


def _gather_plan(blocks):
    nop = len(blocks)

    def copies(x_refs, o_refs, sems):
        send_sems, recv_sems, local_sems = sems
        x, y, c = _mesh_pos()
        me, sibling = (x, y, c), (x, y, 1 - c)
        chips = _other_chips(x, y)

        def copy(i, k, block_of, to, src=None):
            dst = o_refs[i].at[4 * block_of[0] + 2 * block_of[1] + block_of[2]]
            return _remote(dst if src is None else src, dst, send_sems.at[i, k], recv_sems.at[i, k], to)

        own = [pltpu.make_async_copy(x_refs[i], o_refs[i].at[4 * x + 2 * y + c], local_sems.at[i]) for i in range(nop)]
        first = []
        for i in range(nop):
            first.append(copy(i, 0, me, sibling, src=x_refs[i]))
            first += [copy(i, 1 + j, me, (*chip, c), src=x_refs[i]) for j, chip in enumerate(chips)]
        return copy, own, first, me, sibling, chips, c

    def start(x_refs, o_refs, sems):
        _, own, first, *_ = copies(x_refs, o_refs, sems)
        for cp in own + first:
            cp.start()

    def finish(x_refs, o_refs, sems):
        copy, own, first, me, sibling, chips, c = copies(x_refs, o_refs, sems)
        passed = []
        for i in range(nop):
            for j, chip in enumerate(chips):
                copy(i, 1 + j, (*chip, c), me).wait_recv()
                passed.append(copy(i, 4 + j, (*chip, c), sibling))
                passed[-1].start()
        for i in range(nop):
            copy(i, 0, sibling, me).wait_recv()
            for j, chip in enumerate(chips):
                copy(i, 4 + j, (*chip, 1 - c), me).wait_recv()
        for cp in first + passed:
            cp.wait_send()
        for cp in own:
            cp.wait()

    return _Exchange(list(blocks), [_sds((N_DEV,) + b.shape, b.dtype) for b in blocks],
                     [pltpu.SemaphoreType.DMA((nop, N_DEV - 1)), pltpu.SemaphoreType.DMA((nop, N_DEV - 1)),
                      pltpu.SemaphoreType.DMA((nop,))], start, finish)


def _exchange_sibling(name, grads):
    nop = len(grads)

    def body(*refs):
        x_refs, o_refs = refs[:nop], refs[nop:2 * nop]
        send_sems, recv_sems = refs[2 * nop:]
        x, y, c = _mesh_pos()
        copies = [_remote(x_refs[i].at[2 * q + (1 - c)], o_refs[i].at[q], send_sems.at[i, q], recv_sems.at[i, q],
                          (x, y, 1 - c)) for i in range(nop) for q in range(N_DEV // 2)]
        for cp in copies:
            cp.start()
        for cp in copies:
            cp.wait_recv()
        for cp in copies:
            cp.wait_send()

    return _hbm_call(name, body, grads, [_sds((N_DEV // 2,) + g.shape[1:], g.dtype) for g in grads],
                     [pltpu.SemaphoreType.DMA((nop, N_DEV // 2)), pltpu.SemaphoreType.DMA((nop, N_DEV // 2))])


def _pair_sum(name, grad, other):
    nchip, _, r, c = grad.shape
    tr = _pick(r, max(SUBLANES, TILE["sum_bytes"] // (8 * c)), SUBLANES)

    def body(g_ref, o_ref, s_ref):
        mine = jnp.where(lax.axis_index("c") == 0, g_ref[0, 0], g_ref[0, 1])
        s_ref[0] = (mine + o_ref[0]).astype(s_ref.dtype)

    return pl.pallas_call(
        body, name=name, grid=(nchip, r // tr),
        in_specs=[pl.BlockSpec((1, 2, tr, c), lambda q, t: (q, 0, t, 0)), pl.BlockSpec((1, tr, c), lambda q, t: (q, t, 0))],
        out_specs=pl.BlockSpec((1, tr, c), lambda q, t: (q, t, 0)),
        out_shape=_sds((nchip, r, c), BF16),
        compiler_params=_cparams("parallel", "parallel"),
    )(grad, other)


def _chip_exchange_plan(sums):
    nop = len(sums)

    def copies(x_refs, o_refs, sems, arriving):
        send_sems, recv_sems, local_sems = sems
        x, y, c = _mesh_pos()
        mine = 2 * x + y
        out = []
        for i in range(nop):
            for j, (px, py) in enumerate(_other_chips(x, y)):
                theirs = 2 * px + py
                src, dst = (mine, theirs) if arriving else (theirs, mine)
                out.append(_remote(x_refs[i].at[src], o_refs[i].at[dst], send_sems.at[i, j], recv_sems.at[i, j],
                                   (px, py, c)))
        if not arriving:
            out += [pltpu.make_async_copy(x_refs[i].at[mine], o_refs[i].at[mine], local_sems.at[i]) for i in range(nop)]
        return out

    def start(x_refs, o_refs, sems):
        for cp in copies(x_refs, o_refs, sems, False):
            cp.start()

    def finish(x_refs, o_refs, sems):
        for cp in copies(x_refs, o_refs, sems, True):
            cp.wait_recv()
        mine = copies(x_refs, o_refs, sems, False)
        for cp in mine[:3 * nop]:
            cp.wait_send()
        for cp in mine[3 * nop:]:
            cp.wait()

    return _Exchange(list(sums), [_sds(s.shape, s.dtype) for s in sums],
                     [pltpu.SemaphoreType.DMA((nop, 3)), pltpu.SemaphoreType.DMA((nop, 3)), pltpu.SemaphoreType.DMA((nop,))],
                     start, finish)


def _part_rows(npart, r, c):
    return _pick(r, max(SUBLANES, TILE["sum_bytes"] // (4 * npart * c)), SUBLANES)


def _adamw(name, parts, w, m, v):
    npart, r, c = parts.shape
    lead = len(w.shape) - 2
    tr = _part_rows(npart, r, c)
    c1 = 1.0 - ADAM_B1 ** ADAM_STEP
    c2 = 1.0 - ADAM_B2 ** ADAM_STEP
    at = (0,) * lead + (slice(None), slice(None))

    def body(p_ref, w_ref, m_ref, v_ref, g_ref, d_ref, nm_ref, nv_ref):
        g = p_ref[0].astype(F32)
        for k in range(1, npart):
            g = g + p_ref[k].astype(F32)
        nm = ADAM_B1 * m_ref[at] + (1.0 - ADAM_B1) * g
        nv = ADAM_B2 * v_ref[at] + (1.0 - ADAM_B2) * (g * g)
        g_ref[at] = g
        nm_ref[at] = nm
        nv_ref[at] = nv
        d_ref[at] = -ADAM_LR * ((nm / c1) / (jnp.sqrt(nv / c2) + ADAM_EPS) + ADAM_WD * w_ref[at])

    row = pl.BlockSpec((1,) * lead + (tr, c), lambda i: (0,) * lead + (i, 0))
    return pl.pallas_call(
        body, name=name, grid=(r // tr,),
        in_specs=[pl.BlockSpec((npart, tr, c), lambda i: (0, i, 0)), row, row, row],
        out_specs=[row] * 4,
        out_shape=[_sds(w.shape, F32)] * 4,
        compiler_params=_cparams("parallel"),
    )(parts, w, m, v)


def _pack(pieces, row_mult, lead=()):
    nl = len(lead)
    flat, spans, off = [], [], 0
    for p in pieces:
        p = p.reshape(lead + (-1,))
        size = p.shape[-1]
        padded = -(-size // PACK_W) * PACK_W
        flat.append(jnp.pad(p, [(0, 0)] * nl + [(0, padded - size)]))
        spans.append((off, size))
        off += padded
    rows = -(-(off // PACK_W) // row_mult) * row_mult
    if rows * PACK_W > off:
        flat.append(jnp.zeros(lead + (rows * PACK_W - off,), flat[0].dtype))
    return jnp.concatenate(flat, axis=-1).reshape(lead + (rows, PACK_W)), spans


def _unpack(buf, spans, shapes, lead=0):
    flat = buf.reshape(buf.shape[:lead] + (-1,))
    return [flat[..., o:o + s].reshape(buf.shape[:lead] + tuple(shape)) for (o, s), shape in zip(spans, shapes)]


def _block_diag(rows_gh, groups):
    gh, p = rows_gh.shape
    own = (jnp.arange(gh)[:, None] // (gh // groups) == jnp.arange(groups)[None, :]).astype(rows_gh.dtype)
    return (own[:, :, None] * rows_gh[:, None, :]).reshape(gh, groups * p)


def _block_diag_take(dense, groups):
    gh = dense.shape[0]
    p = dense.shape[1] // groups
    own = (jnp.arange(gh)[:, None] // (gh // groups) == jnp.arange(groups)[None, :]).astype(dense.dtype)
    return jnp.sum(dense.reshape(gh, groups, p) * own[:, :, None], axis=1)


FFN1 = ("ffn1_w1", "ffn1_w3", "ffn1_w2")
MIXER = ("w_in", "ssm_glu_w", "w_out")
FFN2 = ("ffn2_w1", "ffn2_w3", "ffn2_w2")
BIG = FFN1 + MIXER + FFN2
COL_SHARDED = ("ffn1_w1", "ffn1_w3", "w_in", "ffn2_w1", "ffn2_w3", "conv_w")
SMALL = ("norm_ffn1", "norm_mix", "conv_b", "conv_ln_g", "conv_ln_b", "conv_out_g", "ssm_A_re", "ssm_A_im",
         "ssm_log_dt", "ssm_B_re", "ssm_B_im", "ssm_C_re", "ssm_C_im", "ssm_D", "ssm_glu_b", "ssm_out_g",
         "norm_ffn2", "norm_final")
WEIGHTS = ("norm_ffn1", "ffn1_w1", "ffn1_w3", "ffn1_w2", "norm_mix", "w_in", "conv_w", "conv_b", "conv_ln_g",
           "conv_ln_b", "conv_out_g", "ssm_A_re", "ssm_A_im", "ssm_log_dt", "ssm_B_re", "ssm_B_im", "ssm_C_re",
           "ssm_C_im", "ssm_D", "ssm_glu_w", "ssm_glu_b", "ssm_out_g", "w_out", "norm_ffn2", "ffn2_w1", "ffn2_w3",
           "ffn2_w2", "norm_final")


def _ffn_backward(tag, dxo, x, g, w1, w3, w2, saved, exchange=None, reduce_plan=None):
    a, b, h = saved
    (da, db, hid, dxh), got = _ffn_bwd_hidden(tag + "_bwd_hidden", dxo, a, b, w2, exchange=exchange)
    dws = [_mm_tn(tag + "_dw1", da, h), _mm_tn(tag + "_dw3", db, h), _mm_tn(tag + "_dw2", hid, dxh)]
    f = a.shape[1]
    (dx, dg), reduced = _dx_rms_bwd(tag + "_bwd_dx", [(da, f, 0, w1, f, 0), (db, f, 0, w3, f, 0)], dxo, x, g,
                                    exchange=reduce_plan(dws) if reduce_plan else None)
    return (dx, dg, dws), got, reduced


def _reduce_in_chip(tag, names, grads):
    send = [g.reshape((N_DEV, -1) + g.shape[1:]) for g in grads]
    from_core = _exchange_sibling("exchange_core_" + tag, send)
    return _chip_exchange_plan([_pair_sum("pair_sum_" + k, s.reshape((N_DEV // 2, 2) + s.shape[1:]), o)
                                for k, s, o in zip(names, send, from_core)])


def kernel(x, norm_ffn1, ffn1_w1, ffn1_w3, ffn1_w2, norm_mix, w_in, conv_w, conv_b, conv_ln_g, conv_ln_b, conv_out_g, ssm_A_re, ssm_A_im, ssm_log_dt, ssm_B_re, ssm_B_im, ssm_C_re, ssm_C_im, ssm_D, ssm_glu_w, ssm_glu_b, ssm_out_g, w_out, norm_ffn2, ffn2_w1, ffn2_w3, ffn2_w2, norm_final, loss_target, m_norm_ffn1, m_ffn1_w1, m_ffn1_w3, m_ffn1_w2, m_norm_mix, m_w_in, m_conv_w, m_conv_b, m_conv_ln_g, m_conv_ln_b, m_conv_out_g, m_ssm_A_re, m_ssm_A_im, m_ssm_log_dt, m_ssm_B_re, m_ssm_B_im, m_ssm_C_re, m_ssm_C_im, m_ssm_D, m_ssm_glu_w, m_ssm_glu_b, m_ssm_out_g, m_w_out, m_norm_ffn2, m_ffn2_w1, m_ffn2_w3, m_ffn2_w2, m_norm_final, v_norm_ffn1, v_ffn1_w1, v_ffn1_w3, v_ffn1_w2, v_norm_mix, v_w_in, v_conv_w, v_conv_b, v_conv_ln_g, v_conv_ln_b, v_conv_out_g, v_ssm_A_re, v_ssm_A_im, v_ssm_log_dt, v_ssm_B_re, v_ssm_B_im, v_ssm_C_re, v_ssm_C_im, v_ssm_D, v_ssm_glu_w, v_ssm_glu_b, v_ssm_out_g, v_w_out, v_norm_ffn2, v_ffn2_w1, v_ffn2_w3, v_ffn2_w2, v_norm_final):
    args = dict(locals())
    wt = {n: args[n] for n in WEIGHTS}
    mom = {n: args["m_" + n] for n in WEIGHTS}
    var = {n: args["v_" + n] for n in WEIGHTS}

    bsz, seq, d = x.shape
    n = bsz * seq
    c = conv_b.shape[-1]
    groups = c // SSM_GROUP
    gp = groups * SSM_STATE
    u_b = 2

    shard = {k: (wt[k][0].T if k in COL_SHARDED else wt[k][0]).astype(BF16) for k in BIG}
    gathered = _run_exchange("gather_weights_ffn1", _gather_plan([shard[k] for k in FFN1]))
    full = {k: g.reshape(-1, g.shape[-1]) for k, g in zip(FFN1, gathered)}
    gather_rest = _gather_plan([shard[k] for k in MIXER + FFN2] + [wt["conv_w"][0]])

    vec = lambda k: wt[k].reshape(1, -1)
    g_ffn1, g_mix, g_ffn2, g_fin = vec("norm_ffn1"), vec("norm_mix"), vec("norm_ffn2"), vec("norm_final")
    cb, lng, lnb, cog = vec("conv_b"), vec("conv_ln_g"), vec("conv_ln_b"), vec("conv_out_g")
    d_skip, glu_b, sog = vec("ssm_D"), vec("ssm_glu_b"), vec("ssm_out_g")

    a_re, a_im = wt["ssm_A_re"][0], wt["ssm_A_im"][0]
    log_dt = wt["ssm_log_dt"][0].reshape(groups, 1)
    bt_re = wt["ssm_B_re"][0].transpose(0, 2, 1).reshape(groups * SSM_GROUP, SSM_STATE)
    bt_im = wt["ssm_B_im"][0].transpose(0, 2, 1).reshape(groups * SSM_GROUP, SSM_STATE)
    c_re = wt["ssm_C_re"][0].reshape(groups * SSM_GROUP, SSM_STATE)
    c_im = wt["ssm_C_im"][0].reshape(groups * SSM_GROUP, SSM_STATE)
    per_chan = lambda t: jnp.repeat(t, SSM_GROUP, axis=0)
    ssm_prim = (a_re, a_im, log_dt, per_chan(a_re), per_chan(a_im), per_chan(jnp.broadcast_to(log_dt, a_re.shape)),
                bt_re, bt_im)
    pw_r, pw_i, bb_r, bb_i = _ssm_prep("ssm_prep", ssm_prim)
    tab_f = _scan_tables(pw_r, pw_i, False)
    tab_b = _scan_tables(pw_r, pw_i, True)
    bbd = jnp.concatenate([_block_diag(bb_r, groups), _block_diag(bb_i, groups)], axis=1).astype(BF16)
    cdt = jnp.concatenate([_block_diag(c_re, groups), -_block_diag(c_im, groups)], axis=1).astype(BF16)

    x0 = x.reshape(n, d)
    (x1, *ffn1_saved), gathered = _ffn_fwd("ffn1_fwd", x0, g_ffn1, full["ffn1_w1"], full["ffn1_w3"], full["ffn1_w2"],
                                           exchange=gather_rest)
    full.update({k: g.reshape(-1, g.shape[-1]) for k, g in zip(MIXER + FFN2, gathered)})
    conv_w_full = gathered[-1].transpose(1, 0, 2).reshape(CONV_WIDTH, c)
    conv_w_pad = jnp.pad(conv_w_full, ((0, CONV_HALO - CONV_WIDTH), (0, 0)))
    (proj,), h2 = _rms_mm("mix_in", x1, g_mix, [full["w_in"]], F32)
    proj3 = proj.reshape(bsz, seq, 3 * c)
    an3, cv3 = _conv_fwd("conv_fwd", proj3, conv_w_pad, cb, lng, lnb, cog)
    an = an3.reshape(n, c)
    xs3, cx3 = _scan_fwd("scan_fwd", tab_f, proj3, u_b, bbd, cdt)
    xs = xs3.reshape(n, 2 * gp)
    y, sn = _ssm_out_fwd("ssm_out_fwd", cx3.reshape(n, c), proj, u_b, d_skip, full["ssm_glu_w"], glu_b, sog)
    w_o = full["w_out"]
    x2 = _row_mm("mix_out", [(an, c, 0, w_o, c, 0, False), (sn, c, 0, w_o, c, 1, False)], d, F32, add=x1)
    (x3, *ffn2_saved), _ = _ffn_fwd("ffn2_fwd", x2, g_ffn2, full["ffn2_w1"], full["ffn2_w3"], full["ffn2_w2"])
    dx3, loss_tile, d_gfin = _loss_head("loss_head", x3, g_fin, loss_target.reshape(n, d))
    loss = lax.psum(loss_tile[0, 0], MESH_AXES)

    grads, from_chips = {}, {}
    (dx2, grads["norm_ffn2"], dws), _, _ = _ffn_backward(
        "ffn2", dx3, x2, g_ffn2, full["ffn2_w1"], full["ffn2_w3"], full["ffn2_w2"], ffn2_saved)
    reduce_ffn2 = _reduce_in_chip("ffn2", FFN2, dws)

    dmix = _row_mm("mix_out_bwd", [(dx2, d, 0, w_o, 2 * c, 0, True)], 2 * c, F32)
    grads["w_out"] = jnp.concatenate([_mm_tn("dw_out_a", an, dx2), _mm_tn("dw_out_s", sn, dx2)], axis=0)

    dy, du_skip, grads["ssm_glu_w"], grads["ssm_glu_b"], grads["ssm_out_g"], grads["ssm_D"] = _ssm_out_bwd(
        "ssm_out_bwd", dmix, 1, y, proj, u_b, d_skip, full["ssm_glu_w"], glu_b, sog)
    (lam3, du3, dab_r, dab_i), got = _scan_bwd("scan_bwd", tab_b, dy.reshape(bsz, seq, c), xs3,
                                               du_skip.reshape(bsz, seq, c), bbd, cdt, exchange=reduce_ffn2)
    from_chips.update(zip(FFN2, got))
    lam, du = lam3.reshape(n, 2 * gp), du3.reshape(n, c)
    d_bbd = _band_wgrad("ssm_dbb", proj, u_b, c, lam)
    d_cdt = _band_wgrad("ssm_dc", dy, 0, c, xs)
    d_are, d_aim, d_ldt, d_btr, d_bti = _ssm_param_grads(
        "ssm_param_grads", ssm_prim,
        dab_r.reshape(SUBLANES, groups, SSM_STATE), dab_i.reshape(SUBLANES, groups, SSM_STATE),
        _band_diag_take(d_bbd, 0, c, gp), _band_diag_take(d_bbd, 1, c, gp))
    grads["ssm_A_re"], grads["ssm_A_im"], grads["ssm_log_dt"] = d_are, d_aim, d_ldt
    grads["ssm_B_re"] = d_btr.reshape(groups, SSM_GROUP, SSM_STATE).transpose(0, 2, 1)
    grads["ssm_B_im"] = d_bti.reshape(groups, SSM_GROUP, SSM_STATE).transpose(0, 2, 1)
    grads["ssm_C_re"] = _band_diag_take(d_cdt, 0, c, gp)
    grads["ssm_C_im"] = -_band_diag_take(d_cdt, 1, c, gp)

    dconv3, d_cw, grads["conv_b"], grads["conv_ln_g"], grads["conv_ln_b"], grads["conv_out_g"] = _conv_bwd(
        "conv_bwd", dmix.reshape(bsz, seq, 2 * c), proj3, cv3, conv_w_pad, lng, lnb, cog)
    dconv = dconv3.reshape(n, 2 * c)
    grads["conv_w"] = d_cw[:CONV_WIDTH]
    grads["w_in"] = jnp.concatenate([_mm_tn("dw_in_conv", dconv, h2), _mm_tn("dw_in_ssm", du, h2)], axis=0)
    w_i = full["w_in"]
    (dx1, grads["norm_mix"]), _ = _dx_rms_bwd("mix_in_bwd", [(dconv, 2 * c, 0, w_i, 2 * c, 0), (du, c, 0, w_i, c, 2)],
                                              dx2, x1, g_mix)
    reduce_mixer = _reduce_in_chip("mixer", MIXER, [grads[k] for k in MIXER])

    (dx0, grads["norm_ffn1"], _), got, reduced = _ffn_backward(
        "ffn1", dx1, x0, g_ffn1, full["ffn1_w1"], full["ffn1_w3"], full["ffn1_w2"], ffn1_saved,
        exchange=reduce_mixer, reduce_plan=functools.partial(_reduce_in_chip, "ffn1", FFN1))
    from_chips.update(zip(MIXER, got))
    from_chips.update(zip(FFN1, reduced))
    grads["norm_final"] = d_gfin

    res = {}
    for k in BIG:
        parts = from_chips[k]
        if k in COL_SHARDED:
            swap = lambda t: jnp.swapaxes(t, -1, -2)
            res[k] = [swap(t) for t in _adamw("adamw_" + k, parts, swap(wt[k]), swap(mom[k]), swap(var[k]))]
        else:
            res[k] = _adamw("adamw_" + k, parts, wt[k], mom[k], var[k])

    small_names = SMALL + ("conv_w",)
    no_state = jnp.zeros_like(grads["conv_w"])
    part, spans = _pack([grads[k] for k in small_names], SUBLANES)
    (all_parts,) = _run_exchange("gather_small_grads", _gather_plan([part]))
    w_pk, _ = _pack([wt[k] for k in SMALL] + [no_state], SUBLANES)
    m_pk, _ = _pack([mom[k] for k in SMALL] + [no_state], SUBLANES)
    v_pk, _ = _pack([var[k] for k in SMALL] + [no_state], SUBLANES)
    small_out = _adamw("adamw_replicated", all_parts, w_pk, m_pk, v_pk)
    small_shapes = [wt[k].shape for k in SMALL] + [grads["conv_w"].shape]
    small_res = [dict(zip(small_names, _unpack(o, spans, small_shapes))) for o in small_out]
    x_pos, y_pos, c_pos = (lax.axis_index(a) for a in MESH_AXES)
    cw_cols = c // N_DEV
    own_cw = lax.dynamic_slice_in_dim(small_res[0]["conv_w"], (4 * x_pos + 2 * y_pos + c_pos) * cw_cols, cw_cols, axis=1)
    res["conv_w"] = _adamw("adamw_conv_w", own_cw[None], wt["conv_w"], mom["conv_w"], var["conv_w"])

    outs = [loss, dx0.reshape(bsz, seq, d)]
    for kind in range(4):
        outs += [res[k][kind] if k in res else small_res[kind][k] for k in WEIGHTS]
    return tuple(outs)
```

```python
import collections
import functools
import math

import jax
import jax.numpy as jnp
from jax import lax
from jax.experimental import pallas as pl
from jax.experimental.pallas import tpu as pltpu

F32 = jnp.float32
BF16 = jnp.bfloat16

EPS = 1e-6
FFN_RES = 0.5
CONV_WIDTH = 31
CONV_HALO = 32
SSM_GROUP = 16
SSM_STATE = 64
ADAM_LR, ADAM_B1, ADAM_B2, ADAM_EPS, ADAM_WD, ADAM_STEP = 0.001, 0.9, 0.999, 1e-08, 0.01, 10

N_DEV = 8
MESH_AXES = ("x", "y", "c")
SUBLANES = 8
LANES = 128
V7X_VMEM_BYTES = 64 * 2**20
VMEM_LIMIT = V7X_VMEM_BYTES - 8 * 2**20

TILE = dict(row=256, mm_bytes=8 * 2**20, up_m=1024, up_n=256, wide_n=2048, conv_t=512, scan_t=256, scan_w=512,
            sum_bytes=4 * 2**20)

_GELU_K = math.sqrt(2.0 / math.pi)
_GELU_C = 0.044715


def _pick(n, target, mult):
    best = None
    for t in range(mult, min(n, target) + 1, mult):
        if n % t == 0:
            best = t
    return n if best is None else best


def _cparams(*sem):
    return pltpu.CompilerParams(dimension_semantics=sem, vmem_limit_bytes=VMEM_LIMIT)


def _sds(shape, dtype):
    return jax.ShapeDtypeStruct(shape, dtype)


def _call(name, body, grid, in_specs, out_specs, out_shape, operands, sem, scratch=(), exchange=None):
    if exchange is None:
        res = pl.pallas_call(body, name=name, grid=grid, in_specs=list(in_specs), out_specs=list(out_specs),
                             out_shape=list(out_shape), scratch_shapes=list(scratch),
                             compiler_params=_cparams(*sem))(*operands)
        return list(res), None
    n_in, n_out, n_scr = len(in_specs), len(out_specs), len(scratch)
    n_xin, n_xout = len(exchange.operands), len(exchange.out_shapes)
    hbm = pl.BlockSpec(memory_space=pltpu.HBM)

    def with_exchange(*refs):
        cuts, pos = [], 0
        for size in (n_in, n_xin, n_out, n_xout, n_scr):
            cuts.append(refs[pos:pos + size])
            pos += size
        ins, x_in, outs, x_out, scr = cuts
        sems = refs[pos:]
        ids = [pl.program_id(axis) for axis in range(len(grid))]
        first = functools.reduce(lambda p, q: p & q, [i == 0 for i in ids])
        last = functools.reduce(lambda p, q: p & q, [i == g - 1 for i, g in zip(ids, grid)])

        @pl.when(first)
        def _():
            exchange.start(x_in, x_out, sems)

        body(*ins, *outs, *scr)

        @pl.when(last)
        def _():
            exchange.finish(x_in, x_out, sems)

    res = pl.pallas_call(
        with_exchange, name=name, grid=grid, in_specs=list(in_specs) + [hbm] * n_xin,
        out_specs=list(out_specs) + [hbm] * n_xout, out_shape=list(out_shape) + list(exchange.out_shapes),
        scratch_shapes=list(scratch) + list(exchange.scratch),
        compiler_params=_cparams(*["arbitrary"] * len(grid)))(*operands, *exchange.operands)
    return list(res[:n_out]), list(res[n_out:])


def _dot(a, b):
    return jnp.dot(a, b, preferred_element_type=F32)


def _dot_nt(a, b):
    return lax.dot_general(a, b, (((1,), (1,)), ((), ())), preferred_element_type=F32)


def _dot_tn(a, b):
    return lax.dot_general(a, b, (((0,), (0,)), ((), ())), preferred_element_type=F32)


def _sigmoid(x):
    return 0.5 * jnp.tanh(0.5 * x) + 0.5


def _rms_stats(x):
    r = lax.rsqrt(jnp.mean(x * x, axis=-1, keepdims=True) + EPS)
    return r, x * r


def _rms_bwd(x, g, dy):
    r, xh = _rms_stats(x)
    dxh = dy * g
    dx = r * (dxh - xh * jnp.mean(dxh * xh, axis=-1, keepdims=True))
    return dx, jnp.sum(dy * xh, axis=0, keepdims=True)


def _rms_mm(name, x, g, ws, out_dtype):
    n, d = x.shape
    f = ws[0].shape[0]
    nw = len(ws)
    tm, tn = _pick(n, TILE["up_m"], 16), _pick(f, TILE["wide_n"], LANES)

    def body(x_ref, g_ref, *refs):
        w_refs, o_refs, h_ref = refs[:nw], refs[nw:2 * nw], refs[2 * nw]

        @pl.when(pl.program_id(1) == 0)
        def _():
            _, xh = _rms_stats(x_ref[...])
            h_ref[...] = (xh * g_ref[...]).astype(BF16)

        h = h_ref[...]
        for w_ref, o_ref in zip(w_refs, o_refs):
            o_ref[...] = _dot_nt(h, w_ref[...]).astype(o_ref.dtype)

    outs = pl.pallas_call(
        body, name=name, grid=(n // tm, f // tn),
        in_specs=[pl.BlockSpec((tm, d), lambda i, j: (i, 0)), pl.BlockSpec((1, d), lambda i, j: (0, 0))]
        + [pl.BlockSpec((tn, d), lambda i, j: (j, 0))] * nw,
        out_specs=[pl.BlockSpec((tm, tn), lambda i, j: (i, j))] * nw + [pl.BlockSpec((tm, d), lambda i, j: (i, 0))],
        out_shape=[_sds((n, f), out_dtype)] * nw + [_sds((n, d), BF16)],
        compiler_params=_cparams("parallel", "arbitrary"),
    )(x, g, *ws)
    return outs[:nw], outs[nw]


def _ffn_fwd(name, x, g, w1t, w3t, w2, exchange=None):
    n, d = x.shape
    f = w2.shape[0]
    tm, tn = _pick(n, TILE["row"], 16), _pick(f, TILE["up_n"], LANES)

    def body(x_ref, g_ref, w1_ref, w3_ref, w2_ref, o_ref, a_ref, b_ref, h_ref):
        xv = x_ref[...]
        _, xh = _rms_stats(xv)
        h = (xh * g_ref[...]).astype(BF16)
        h_ref[...] = h
        acc = None
        for c0 in range(0, f, tn):
            cols = pl.ds(c0, tn)
            av, bv = _dot_nt(h, w1_ref[cols, :]), _dot_nt(h, w3_ref[cols, :])
            a_ref[:, cols] = av.astype(BF16)
            b_ref[:, cols] = bv.astype(BF16)
            t = _dot((av * _sigmoid(av) * bv).astype(BF16), w2_ref[cols, :])
            acc = t if acc is None else acc + t
        o_ref[...] = xv + FFN_RES * acc

    row = pl.BlockSpec((tm, d), lambda i: (i, 0))
    wide = pl.BlockSpec((tm, f), lambda i: (i, 0))
    held = pl.BlockSpec((f, d), lambda i: (0, 0), pipeline_mode=pl.Buffered(1))
    return _call(
        name, body, (n // tm,), [row, pl.BlockSpec((1, d), lambda i: (0, 0)), held, held, held], [row, wide, wide, row],
        [_sds((n, d), F32), _sds((n, f), BF16), _sds((n, f), BF16), _sds((n, d), BF16)],
        (x, g, w1t, w3t, w2), ("parallel",), exchange=exchange)


def _ffn_bwd_hidden(name, dxo, a, b, w2, exchange=None):
    n, d = dxo.shape
    f = a.shape[1]
    tm, tn = _pick(n, TILE["row"], 16), _pick(f, TILE["up_n"], LANES)

    def body(dx_ref, a_ref, b_ref, w_ref, da_ref, db_ref, hid_ref, dxh_ref):
        dxh = (FFN_RES * dx_ref[...]).astype(BF16)
        dxh_ref[...] = dxh
        for c0 in range(0, f, tn):
            cols = pl.ds(c0, tn)
            dhid = _dot_nt(dxh, w_ref[cols, :])
            av, bv = a_ref[:, cols].astype(F32), b_ref[:, cols].astype(F32)
            sig = _sigmoid(av)
            silu = av * sig
            da_ref[:, cols] = (dhid * bv * (sig * (1.0 + av * (1.0 - sig)))).astype(BF16)
            db_ref[:, cols] = (dhid * silu).astype(BF16)
            hid_ref[:, cols] = (silu * bv).astype(BF16)

    wide = pl.BlockSpec((tm, f), lambda i: (i, 0))
    row = pl.BlockSpec((tm, d), lambda i: (i, 0))
    return _call(
        name, body, (n // tm,), [row, wide, wide, pl.BlockSpec((f, d), lambda i: (0, 0))], [wide, wide, wide, row],
        [_sds((n, f), BF16)] * 3 + [_sds((n, d), BF16)], (dxo, a, b, w2), ("parallel",), exchange=exchange)


def _loss_head(name, x, g, target):
    n, d = x.shape
    tm = _pick(n, TILE["row"], SUBLANES)

    def body(x_ref, g_ref, t_ref, dx_ref, loss_ref, dg_ref):
        @pl.when(pl.program_id(0) == 0)
        def _():
            loss_ref[...] = jnp.zeros_like(loss_ref)
            dg_ref[...] = jnp.zeros_like(dg_ref)

        xv, gv = x_ref[...], g_ref[...]
        r, xh = _rms_stats(xv)
        err = xh * gv - t_ref[...]
        loss_ref[...] += 0.5 * jnp.sum(jnp.mean(err * err, axis=-1, keepdims=True))
        dy = err * (1.0 / d)
        dxh = dy * gv
        dx_ref[...] = r * (dxh - xh * jnp.mean(dxh * xh, axis=-1, keepdims=True))
        dg_ref[...] += jnp.sum(dy * xh, axis=0, keepdims=True)

    return pl.pallas_call(
        body, name=name, grid=(n // tm,),
        in_specs=[pl.BlockSpec((tm, d), lambda i: (i, 0)), pl.BlockSpec((1, d), lambda i: (0, 0)),
                  pl.BlockSpec((tm, d), lambda i: (i, 0))],
        out_specs=[pl.BlockSpec((tm, d), lambda i: (i, 0)), pl.BlockSpec((SUBLANES, LANES), lambda i: (0, 0)),
                   pl.BlockSpec((1, d), lambda i: (0, 0))],
        out_shape=[_sds((n, d), F32), _sds((SUBLANES, LANES), F32), _sds((1, d), F32)],
        compiler_params=_cparams("arbitrary"),
    )(x, g, target)


def _dx_rms_bwd(name, pairs, dxo, x, g, exchange=None):
    n, dm = x.shape
    tm = _pick(n, TILE["row"], 16)
    npair = len(pairs)

    def body(*refs):
        d_refs, w_refs = refs[:npair], refs[npair:2 * npair]
        dxo_ref, x_ref, g_ref, dx_ref, dg_ref = refs[2 * npair:]

        @pl.when(pl.program_id(0) == 0)
        def _():
            dg_ref[...] = jnp.zeros_like(dg_ref)

        dh = None
        for d_ref, w_ref in zip(d_refs, w_refs):
            t = _dot(d_ref[...].astype(BF16), w_ref[...])
            dh = t if dh is None else dh + t
        dx, dg = _rms_bwd(x_ref[...], g_ref[...], dh)
        dx_ref[...] = dxo_ref[...] + dx
        dg_ref[...] += dg

    row = pl.BlockSpec((tm, dm), lambda i: (i, 0))
    d_specs = [pl.BlockSpec((tm, p[1]), functools.partial(lambda i, cb: (i, cb), cb=p[2])) for p in pairs]
    w_specs = [pl.BlockSpec((p[4], dm), functools.partial(lambda i, rb: (rb, 0), rb=p[5])) for p in pairs]
    return _call(
        name, body, (n // tm,), d_specs + w_specs + [row, row, pl.BlockSpec((1, dm), lambda i: (0, 0))],
        [row, pl.BlockSpec((1, dm), lambda i: (0, 0))], [_sds((n, dm), F32), _sds((1, dm), F32)],
        (*[p[0] for p in pairs], *[p[3] for p in pairs], dxo, x, g), ("arbitrary",), exchange=exchange)


def _mm_tn(name, a, b, a_cols=None, b_cols=None):
    n = a.shape[0]
    a0, ma = a_cols if a_cols else (0, a.shape[1])
    b0, mb = b_cols if b_cols else (0, b.shape[1])
    assert a0 % ma == 0 and b0 % mb == 0
    ab, bb = a0 // ma, b0 // mb
    tk = _pick(n, TILE["mm_bytes"] // (ma * a.dtype.itemsize + mb * b.dtype.itemsize), 16)

    def body(a_ref, b_ref, o_ref):
        @pl.when(pl.program_id(0) == 0)
        def _():
            o_ref[...] = jnp.zeros_like(o_ref)

        o_ref[...] += _dot_tn(a_ref[...].astype(BF16), b_ref[...].astype(BF16))

    return pl.pallas_call(
        body, name=name, grid=(n // tk,),
        in_specs=[pl.BlockSpec((tk, ma), lambda k: (k, ab)), pl.BlockSpec((tk, mb), lambda k: (k, bb))],
        out_specs=pl.BlockSpec((ma, mb), lambda k: (0, 0)),
        out_shape=_sds((ma, mb), F32),
        compiler_params=_cparams("arbitrary"),
    )(a, b)


def _row_mm(name, pairs, out_w, out_dtype, add=None, exchange=None):
    n = pairs[0][0].shape[0]
    tm = _pick(n, TILE["row"], 16)
    npair = len(pairs)

    def body(*refs):
        a_refs, w_refs = refs[:npair], refs[npair:2 * npair]
        add_ref = refs[2 * npair] if add is not None else None
        o_ref = refs[-1]
        acc = None
        for a_ref, w_ref, p in zip(a_refs, w_refs, pairs):
            av = a_ref[...].astype(BF16)
            t = _dot_nt(av, w_ref[...]) if p[6] else _dot(av, w_ref[...])
            acc = t if acc is None else acc + t
        if add_ref is not None:
            acc = acc + add_ref[...].astype(F32)
        o_ref[...] = acc.astype(o_ref.dtype)

    a_specs = [pl.BlockSpec((tm, p[1]), functools.partial(lambda i, cb: (i, cb), cb=p[2])) for p in pairs]
    w_specs = [pl.BlockSpec((p[4], p[3].shape[1]), functools.partial(lambda i, rb: (rb, 0), rb=p[5])) for p in pairs]
    add_specs = [pl.BlockSpec((tm, out_w), lambda i: (i, 0))] if add is not None else []
    (out,), got = _call(
        name, body, (n // tm,), a_specs + w_specs + add_specs, [pl.BlockSpec((tm, out_w), lambda i: (i, 0))],
        [_sds((n, out_w), out_dtype)],
        (*[p[0] for p in pairs], *[p[3] for p in pairs], *([add] if add is not None else [])), ("parallel",),
        exchange=exchange)
    return out, got


def _conv_post(c, ln_g, ln_b, out_g):
    mu = jnp.mean(c, axis=-1, keepdims=True)
    xc = c - mu
    rstd = lax.rsqrt(jnp.mean(xc * xc, axis=-1, keepdims=True) + EPS)
    nrm = xc * rstd
    l = nrm * ln_g + ln_b
    sig = _sigmoid(l)
    s = l * sig
    r, sh = _rms_stats(s)
    return sh * out_g, (rstd, nrm, l, sig, r, sh)


def _tap_groups(first):
    groups = []
    for r in range(SUBLANES):
        taps = [(s - r, s - first) for s in range(first, first + CONV_WIDTH) if s % SUBLANES == r]
        if taps:
            groups.append((r, taps))
    return groups


def _conv_taps(a_ref, w_ref, b_ref, first, rows, flip=False):
    acc = None
    for r, taps in _tap_groups(first):
        ext = rows if r == 0 else rows + SUBLANES
        part = None
        for base, k in taps:
            kk = CONV_WIDTH - 1 - k if flip else k
            t = w_ref[kk:kk + 1, :] * a_ref[pl.ds(base, ext), :]
            part = t if part is None else part + t
        if r:
            b_ref[...] = part
            part = b_ref[pl.ds(r, rows), :]
        acc = part if acc is None else acc + part
    return acc


def _conv_post_bwd(cv, dout, ln_g, ln_b, out_g):
    _, (rstd, nrm, l, sig, r, sh) = _conv_post(cv, ln_g, ln_b, out_g)
    dsh = dout * out_g
    ds = r * (dsh - sh * jnp.mean(dsh * sh, axis=-1, keepdims=True))
    dl = ds * (sig * (1.0 + l * (1.0 - sig)))
    dn = dl * ln_g
    dc = rstd * (dn - jnp.mean(dn, axis=-1, keepdims=True) - nrm * jnp.mean(dn * nrm, axis=-1, keepdims=True))
    col_sum = lambda t: jnp.sum(t, axis=0, keepdims=True)
    return dc, col_sum(dout * sh), col_sum(dl * nrm), col_sum(dl)


def _conv_fwd(name, proj3, conv_w, conv_b, ln_g, ln_b, out_g):
    bsz, seq, _ = proj3.shape
    c = conv_w.shape[1]
    tt = _pick(seq, TILE["conv_t"], CONV_HALO)
    hb = tt // CONV_HALO
    first = CONV_HALO - (CONV_WIDTH - 1)

    def body(v_ref, g_ref, vp_ref, gp_ref, w_ref, cb_ref, lg_ref, lb_ref, og_ref, o_ref, cv_ref, a_ref, b_ref):
        keep = (pl.program_id(1) > 0).astype(F32)
        a_ref[pl.ds(0, CONV_HALO), :] = keep * vp_ref[0] * _sigmoid(gp_ref[0])
        a_ref[pl.ds(CONV_HALO, tt), :] = v_ref[0] * _sigmoid(g_ref[0])
        cv = _conv_taps(a_ref, w_ref, b_ref, first, tt) + cb_ref[...]
        cv_ref[0] = cv
        out, _ = _conv_post(cv, lg_ref[...], lb_ref[...], og_ref[...])
        o_ref[0] = out.astype(BF16)

    vec = pl.BlockSpec((1, c), lambda b, i: (0, 0))
    prev = lambda col: pl.BlockSpec((1, CONV_HALO, c), lambda b, i: (b, jnp.maximum(i * hb - 1, 0), col))
    tile = pl.BlockSpec((1, tt, c), lambda b, i: (b, i, 0))
    return pl.pallas_call(
        body, name=name, grid=(bsz, seq // tt),
        in_specs=[tile, pl.BlockSpec((1, tt, c), lambda b, i: (b, i, 1)),
                  prev(0), prev(1), pl.BlockSpec(conv_w.shape, lambda b, i: (0, 0)), vec, vec, vec, vec],
        out_specs=[tile, tile],
        out_shape=[_sds((bsz, seq, c), BF16), _sds((bsz, seq, c), F32)],
        scratch_shapes=[pltpu.VMEM((CONV_HALO + tt, c), F32), pltpu.VMEM((tt + SUBLANES, c), F32)],
        compiler_params=_cparams("parallel", "arbitrary"),
    )(proj3, proj3, proj3, proj3, conv_w, conv_b, ln_g, ln_b, out_g)


def _conv_bwd(name, dmix3, proj3, cv3, conv_w, ln_g, ln_b, out_g):
    bsz, seq, _ = proj3.shape
    c = conv_w.shape[1]
    tt = _pick(seq, TILE["conv_t"], CONV_HALO)
    hb = tt // CONV_HALO
    nt = seq // tt
    last_hb = seq // CONV_HALO - 1
    ext = tt + CONV_HALO
    first = CONV_HALO - (CONV_WIDTH - 1)

    def body(v_ref, g_ref, vp_ref, gp_ref, cv_ref, cvn_ref, d_ref, dn_ref, w_ref, lg_ref, lb_ref, og_ref,
             o_ref, dw_ref, dcb_ref, dlg_ref, dlb_ref, dog_ref, a_ref, dc_ref, b_ref, ds_ref):
        i = pl.program_id(1)

        @pl.when((pl.program_id(0) == 0) & (i == 0))
        def _():
            for r in (dw_ref, dcb_ref, dlg_ref, dlb_ref, dog_ref):
                r[...] = jnp.zeros_like(r)

        keep_prev = (i > 0).astype(F32)
        keep_next = (i < nt - 1).astype(F32)
        sig_g = _sigmoid(g_ref[0])
        a_ref[pl.ds(0, CONV_HALO), :] = keep_prev * vp_ref[0] * _sigmoid(gp_ref[0])
        a_ref[pl.ds(CONV_HALO, tt), :] = v_ref[0] * sig_g

        lg, lb, og = lg_ref[...], lb_ref[...], og_ref[...]
        dc_own, d_og, d_lg, d_lb = _conv_post_bwd(cv_ref[0], d_ref[0], lg, lb, og)
        dc_next, _, _, _ = _conv_post_bwd(cvn_ref[0], keep_next * dn_ref[0], lg, lb, og)
        dog_ref[...] += d_og
        dlg_ref[...] += d_lg
        dlb_ref[...] += d_lb
        dcb_ref[...] += jnp.sum(dc_own, axis=0, keepdims=True)
        dc_ref[pl.ds(0, tt), :] = dc_own
        dc_ref[pl.ds(tt, CONV_HALO), :] = dc_next

        da = _conv_taps(dc_ref, w_ref, b_ref, 0, tt, flip=True)

        for r, taps in _tap_groups(first):
            if r:
                ds_ref[pl.ds(0, SUBLANES), :] = jnp.zeros((SUBLANES, c), F32)
                ds_ref[pl.ds(tt, SUBLANES), :] = jnp.zeros((SUBLANES, c), F32)
                ds_ref[pl.ds(r, tt), :] = dc_own
            for base, k in taps:
                prod = (ds_ref[...] * a_ref[pl.ds(base, tt + SUBLANES), :]) if r else (dc_own * a_ref[pl.ds(base, tt), :])
                dw_ref[k:k + 1, :] += jnp.sum(prod, axis=0, keepdims=True)
        val = v_ref[0]
        o_ref[0] = jnp.concatenate([da * sig_g, da * val * sig_g * (1.0 - sig_g)], axis=-1).astype(BF16)

    vec = pl.BlockSpec((1, c), lambda b, i: (0, 0))
    cur = lambda col: pl.BlockSpec((1, tt, c), lambda b, i: (b, i, col))
    prev = lambda col: pl.BlockSpec((1, CONV_HALO, c), lambda b, i: (b, jnp.maximum(i * hb - 1, 0), col))
    nxt = lambda col: pl.BlockSpec((1, CONV_HALO, c), lambda b, i: (b, jnp.minimum((i + 1) * hb, last_hb), col))
    wspec = pl.BlockSpec(conv_w.shape, lambda b, i: (0, 0))
    return pl.pallas_call(
        body, name=name, grid=(bsz, nt),
        in_specs=[cur(0), cur(1), prev(0), prev(1), cur(0), nxt(0), cur(0), nxt(0), wspec, vec, vec, vec],
        out_specs=[pl.BlockSpec((1, tt, 2 * c), lambda b, i: (b, i, 0)), wspec, vec, vec, vec, vec],
        out_shape=[_sds((bsz, seq, 2 * c), BF16), _sds(conv_w.shape, F32)] + [_sds((1, c), F32)] * 4,
        scratch_shapes=[pltpu.VMEM((CONV_HALO + tt, c), F32), pltpu.VMEM((ext, c), F32),
                        pltpu.VMEM((tt + SUBLANES, c), F32), pltpu.VMEM((tt + SUBLANES, c), F32)],
        compiler_params=_cparams("arbitrary", "arbitrary"),
    )(proj3, proj3, proj3, proj3, cv3, cv3, dmix3, dmix3, conv_w, ln_g, ln_b, out_g)


def _ssm_discretise(a_re, a_im, log_dt):
    dt = jnp.exp(log_dt)
    zr, zi = a_re * dt, a_im * dt
    mag = jnp.exp(zr)
    ar, ai = mag * jnp.cos(zi), mag * jnp.sin(zi)
    den = a_re * a_re + a_im * a_im
    nr = ar - 1.0
    return ar, ai, (nr * a_re + ai * a_im) / den, (ai * a_re - nr * a_im) / den


def _ssm_system(a_re, a_im, log_dt, a_re_x, a_im_x, log_dt_x, bt_re, bt_im):
    ar, ai, _, _ = _ssm_discretise(a_re, a_im, log_dt)
    _, _, cr, ci = _ssm_discretise(a_re_x, a_im_x, log_dt_x)
    return ar, ai, cr * bt_re - ci * bt_im, cr * bt_im + ci * bt_re


def _ssm_prep(name, prim):
    g, p = prim[0].shape

    def body(*refs):
        pwr_ref, pwi_ref, bbr_ref, bbi_ref = refs[8:]
        ar, ai, bbr, bbi = _ssm_system(*[r[...] for r in refs[:8]])
        bbr_ref[...] = bbr
        bbi_ref[...] = bbi
        pr, pi = ar, ai
        for k in range(SUBLANES):
            pwr_ref[k] = pr
            pwi_ref[k] = pi
            pr, pi = pr * ar - pi * ai, pr * ai + pi * ar

    return pl.pallas_call(
        body, name=name,
        out_shape=[_sds((SUBLANES, g, p), F32)] * 2 + [_sds(prim[6].shape, F32)] * 2,
        compiler_params=pltpu.CompilerParams(vmem_limit_bytes=VMEM_LIMIT),
    )(*prim)


def _ssm_param_grads(name, prim, dab_r, dab_i, dbb_r, dbb_i):
    g, p = prim[0].shape
    h = prim[6].shape[0] // g

    def body(*refs):
        dar_ref, dai_ref, dbr_ref, dbi_ref = refs[8:12]
        o_ar, o_ai, o_dt, o_br, o_bi = refs[12:]
        _, vjp = jax.vjp(_ssm_system, *[r[...] for r in refs[:8]])
        ct = (jnp.sum(dar_ref[...], axis=0), jnp.sum(dai_ref[...], axis=0), dbr_ref[...], dbi_ref[...])
        d_ar, d_ai, d_dt, d_arx, d_aix, d_dtx, d_br, d_bi = vjp(ct)
        per_group = lambda t: jnp.sum(t.reshape(g, h, p), axis=1)
        o_ar[...] = d_ar + per_group(d_arx)
        o_ai[...] = d_ai + per_group(d_aix)
        o_dt[...] = d_dt + jnp.sum(per_group(d_dtx), axis=1, keepdims=True)
        o_br[...] = d_br
        o_bi[...] = d_bi

    return pl.pallas_call(
        body, name=name,
        out_shape=[_sds(prim[k].shape, F32) for k in (0, 1, 2, 6, 7)],
        compiler_params=pltpu.CompilerParams(vmem_limit_bytes=VMEM_LIMIT),
    )(*prim, dab_r, dab_i, dbb_r, dbb_i)


def _cfma(xr, xi, cr, ci, sr, si):
    return xr + (cr * sr - ci * si), xi + (cr * si + ci * sr)


def _scan_tables(pw_r, pw_i, reverse):
    gp = pw_r.shape[1] * pw_r.shape[2]
    pr, pi = pw_r.reshape(SUBLANES, gp), pw_i.reshape(SUBLANES, gp)
    if reverse:
        pi = -pi
    row = jnp.arange(SUBLANES)[:, None]
    tabs = []
    for d in (1, 2, 4):
        keep = (row < SUBLANES - d) if reverse else (row >= d)
        tabs += [jnp.where(keep, pr[d - 1][None, :], 0.0), jnp.where(keep, pi[d - 1][None, :], 0.0)]
    tabs += [pr[::-1], pi[::-1]] if reverse else [pr, pi]
    return jnp.concatenate(tabs, axis=0)


MXU_DEPTH = 256


def _bands(c, gp):
    bw = min(c, MXU_DEPTH)
    return c // bw, bw, gp * bw // c


def _band_expand(rows16, w_ref, put, c, gp):
    nb, bw, sw = _bands(c, gp)
    for s in range(nb):
        band = rows16[:, s * bw:(s + 1) * bw]
        for half in (0, gp):
            cols = pl.ds(half + s * sw, sw)
            put(cols, _dot(band, w_ref[pl.ds(s * bw, bw), cols]))


def _band_contract(get16, w_ref, c, gp):
    nb, bw, sw = _bands(c, gp)
    out = []
    for s in range(nb):
        acc = None
        for half in (0, gp):
            cols = pl.ds(half + s * sw, sw)
            t = _dot_nt(get16(cols), w_ref[pl.ds(s * bw, bw), cols])
            acc = t if acc is None else acc + t
        out.append(acc)
    return out[0] if nb == 1 else jnp.concatenate(out, axis=1)


def _band_wgrad(name, a, a_block, c, b):
    n = a.shape[0]
    gp = b.shape[1] // 2
    nb, bw, sw = _bands(c, gp)
    tk = _pick(n, TILE["mm_bytes"] // (c * a.dtype.itemsize + 2 * gp * b.dtype.itemsize), 16)

    def body(a_ref, b_ref, o_ref):
        @pl.when(pl.program_id(0) == 0)
        def _():
            o_ref[...] = jnp.zeros_like(o_ref)

        for s in range(nb):
            band = a_ref[:, s * bw:(s + 1) * bw].astype(BF16)
            for h, half in enumerate((0, gp)):
                o_ref[pl.ds(s * bw, bw), pl.ds(h * sw, sw)] += _dot_tn(
                    band, b_ref[:, pl.ds(half + s * sw, sw)].astype(BF16))

    return pl.pallas_call(
        body, name=name, grid=(n // tk,),
        in_specs=[pl.BlockSpec((tk, c), lambda k: (k, a_block)), pl.BlockSpec((tk, 2 * gp), lambda k: (k, 0))],
        out_specs=pl.BlockSpec((c, 2 * sw), lambda k: (0, 0)),
        out_shape=_sds((c, 2 * sw), F32),
        compiler_params=_cparams("arbitrary"),
    )(a, b)


def _band_diag_take(comp, half, c, gp):
    nb, bw, sw = _bands(c, gp)
    return jnp.concatenate([_block_diag_take(comp[s * bw:(s + 1) * bw, half * sw:(half + 1) * sw], bw // SSM_GROUP)
                            for s in range(nb)], axis=0)


def _scan_fwd(name, tab, proj3, u_block, bbd, cdt):
    bsz, seq, _ = proj3.shape
    c, w = bbd.shape
    gp = w // 2
    tt = _pick(seq, TILE["scan_t"], 16)
    nblk = tt // SUBLANES
    cw = _pick(gp, TILE["scan_w"], LANES)

    def body(tab_ref, u_ref, bbd_ref, cdt_ref, xs_ref, y_ref, carry_ref, bu_ref):
        @pl.when(pl.program_id(1) == 0)
        def _():
            carry_ref[...] = jnp.zeros_like(carry_ref)

        def put_bu(cols, val):
            bu_ref[0, :, cols] = val

        _band_expand(u_ref[0].astype(BF16), bbd_ref, put_bu, c, gp)

        for ch in range(gp // cw):
            re, im = pl.ds(ch * cw, cw), pl.ds(gp + ch * cw, cw)

            def blk(r, carry, re=re, im=im):
                tabs = [tab_ref[pl.ds(SUBLANES * k, SUBLANES), re] for k in range(8)]
                rows = pl.ds(pl.multiple_of(r * SUBLANES, SUBLANES), SUBLANES)
                xr, xi = bu_ref[0, rows, re], bu_ref[0, rows, im]
                for j, d in enumerate((1, 2, 4)):
                    xr, xi = _cfma(xr, xi, tabs[2 * j], tabs[2 * j + 1], pltpu.roll(xr, d, 0), pltpu.roll(xi, d, 0))
                xr, xi = _cfma(xr, xi, tabs[6], tabs[7], carry[0], carry[1])
                xs_ref[0, rows, re] = xr
                xs_ref[0, rows, im] = xi
                last = SUBLANES - 1
                return (jnp.broadcast_to(xr[last:, :], xr.shape), jnp.broadcast_to(xi[last:, :], xi.shape))

            cr, ci = lax.fori_loop(0, nblk, blk, (carry_ref[:, re], carry_ref[:, im]))
            carry_ref[:, re] = cr
            carry_ref[:, im] = ci

        y_ref[0] = _band_contract(lambda cols: xs_ref[0, :, cols].astype(BF16), cdt_ref, c, gp)

    whole = lambda arr: pl.BlockSpec(arr.shape, lambda b, t: (0, 0))
    return pl.pallas_call(
        body, name=name, grid=(bsz, seq // tt),
        in_specs=[whole(tab), pl.BlockSpec((1, tt, c), lambda b, t: (b, t, u_block)), whole(bbd), whole(cdt)],
        out_specs=[pl.BlockSpec((1, tt, w), lambda b, t: (b, t, 0)), pl.BlockSpec((1, tt, c), lambda b, t: (b, t, 0))],
        out_shape=[_sds((bsz, seq, w), F32), _sds((bsz, seq, c), F32)],
        scratch_shapes=[pltpu.VMEM((SUBLANES, w), F32), pltpu.VMEM((1, tt, w), F32)],
        compiler_params=_cparams("arbitrary", "arbitrary"),
    )(tab, proj3, bbd, cdt)


def _scan_bwd(name, tab, dy3, xs3, du_skip3, bbd, cdt, exchange=None):
    bsz, seq, w = xs3.shape
    c = bbd.shape[0]
    gp = w // 2
    tt = _pick(seq, TILE["scan_t"], 16)
    nblk = tt // SUBLANES
    cw = _pick(gp, TILE["scan_w"], LANES)
    nt = seq // tt

    def body(tab_ref, dy_ref, xs_ref, halo_ref, skip_ref, bbd_ref, cdt_ref, lam_ref, du_ref, dar_ref, dai_ref,
             carry_ref, g_ref):
        t = pl.program_id(1)

        @pl.when(t == 0)
        def _():
            carry_ref[...] = jnp.zeros_like(carry_ref)

        @pl.when((pl.program_id(0) == 0) & (t == 0))
        def _():
            dar_ref[...] = jnp.zeros_like(dar_ref)
            dai_ref[...] = jnp.zeros_like(dai_ref)

        def put_g(cols, val):
            g_ref[0, :, cols] = val

        _band_expand(dy_ref[0], cdt_ref, put_g, c, gp)

        has_prev = (t < nt - 1).astype(F32)
        row0 = lax.broadcasted_iota(jnp.int32, (SUBLANES, cw), 0) == 0
        last = SUBLANES - 1

        for ch in range(gp // cw):
            re, im = pl.ds(ch * cw, cw), pl.ds(gp + ch * cw, cw)

            def step(rows, xm1r, xm1i, state, re=re, im=im):
                tabs = [tab_ref[pl.ds(SUBLANES * k, SUBLANES), re] for k in range(8)]
                cr, ci, accr, acci = state
                lr, li = g_ref[0, rows, re], g_ref[0, rows, im]
                for j, d in enumerate((1, 2, 4)):
                    lr, li = _cfma(lr, li, tabs[2 * j], tabs[2 * j + 1],
                                   pltpu.roll(lr, SUBLANES - d, 0), pltpu.roll(li, SUBLANES - d, 0))
                lr, li = _cfma(lr, li, tabs[6], tabs[7], cr, ci)
                lam_ref[0, rows, re] = lr
                lam_ref[0, rows, im] = li
                xr, xi = xs_ref[0, rows, re], xs_ref[0, rows, im]
                xpr = jnp.where(row0, jnp.broadcast_to(xm1r[last:, :], xr.shape), pltpu.roll(xr, 1, 0))
                xpi = jnp.where(row0, jnp.broadcast_to(xm1i[last:, :], xi.shape), pltpu.roll(xi, 1, 0))
                accr = accr + (lr * xpr + li * xpi)
                acci = acci + (li * xpr - lr * xpi)
                return (jnp.broadcast_to(lr[:1, :], lr.shape), jnp.broadcast_to(li[:1, :], li.shape), accr, acci)

            def blk(k, state, re=re, im=im, step=step):
                r = nblk - 1 - k
                rows = pl.ds(pl.multiple_of(r * SUBLANES, SUBLANES), SUBLANES)
                prev = pl.ds(pl.multiple_of((r - 1) * SUBLANES, SUBLANES), SUBLANES)
                return step(rows, xs_ref[0, prev, re], xs_ref[0, prev, im], state)

            zero = jnp.zeros((SUBLANES, cw), F32)
            state = lax.fori_loop(0, nblk - 1, blk, (carry_ref[:, re], carry_ref[:, im], zero, zero))
            cr, ci, accr, acci = step(pl.ds(0, SUBLANES), has_prev * halo_ref[0, :, re], has_prev * halo_ref[0, :, im], state)
            carry_ref[:, re] = cr
            carry_ref[:, im] = ci
            dar_ref[:, re] += accr
            dai_ref[:, re] += acci

        du = _band_contract(lambda cols: lam_ref[0, :, cols].astype(BF16), bbd_ref, c, gp)
        du_ref[0] = (du + skip_ref[0]).astype(BF16)

    tile = pl.BlockSpec((1, tt, w), lambda b, t: (b, nt - 1 - t, 0))
    thin = pl.BlockSpec((1, tt, c), lambda b, t: (b, nt - 1 - t, 0))
    halo = pl.BlockSpec((1, SUBLANES, w), lambda b, t: (b, jnp.maximum((nt - 1 - t) * nblk - 1, 0), 0))
    acc = pl.BlockSpec((SUBLANES, gp), lambda b, t: (0, 0))
    whole = lambda arr: pl.BlockSpec(arr.shape, lambda b, t: (0, 0))
    return _call(
        name, body, (bsz, nt), [whole(tab), thin, tile, halo, thin, whole(bbd), whole(cdt)], [tile, thin, acc, acc],
        [_sds(xs3.shape, F32), _sds((bsz, seq, c), BF16), _sds((SUBLANES, gp), F32), _sds((SUBLANES, gp), F32)],
        (tab, dy3, xs3, xs3, du_skip3, bbd, cdt), ("arbitrary", "arbitrary"),
        scratch=[pltpu.VMEM((SUBLANES, w), F32), pltpu.VMEM((1, tt, w), F32)], exchange=exchange)


def _gelu_parts(y):
    inner = _GELU_K * (y + _GELU_C * y * y * y)
    t = jnp.tanh(inner)
    return 0.5 * y * (1.0 + t), t


def _ssm_out_fwd(name, cx, proj, u_block, d_skip, glu_w, glu_b, out_g):
    n, c = cx.shape
    tm = _pick(n, TILE["row"], 16)

    def body(cx_ref, u_ref, d_ref, gw_ref, gb_ref, og_ref, y_ref, o_ref):
        y = cx_ref[...] + d_ref[...] * u_ref[...]
        y_ref[...] = y
        gy, _ = _gelu_parts(y)
        z = _dot(gy.astype(BF16), gw_ref[...]) + gb_ref[...]
        _, sh = _rms_stats(gy * _sigmoid(z))
        o_ref[...] = (sh * og_ref[...]).astype(BF16)

    vec = pl.BlockSpec((1, c), lambda i: (0, 0))
    row = pl.BlockSpec((tm, c), lambda i: (i, 0))
    return pl.pallas_call(
        body, name=name, grid=(n // tm,),
        in_specs=[row, pl.BlockSpec((tm, c), lambda i: (i, u_block)), vec, pl.BlockSpec(glu_w.shape, lambda i: (0, 0)),
                  vec, vec],
        out_specs=[row, row],
        out_shape=[_sds((n, c), F32), _sds((n, c), BF16)],
        compiler_params=_cparams("parallel"),
    )(cx, proj, d_skip, glu_w, glu_b, out_g)


def _ssm_out_bwd(name, dmix, d_block, y, proj, u_block, d_skip, glu_w, glu_b, out_g):
    n, c = y.shape
    tm = _pick(n, TILE["row"], 16)

    def body(d_ref, y_ref, u_ref, dk_ref, gw_ref, gb_ref, og_ref, dy_ref, du_ref, dgw_ref, dgb_ref, dog_ref, dd_ref):
        @pl.when(pl.program_id(0) == 0)
        def _():
            for r in (dgw_ref, dgb_ref, dog_ref, dd_ref):
                r[...] = jnp.zeros_like(r)

        yv = y_ref[...]
        gy, th = _gelu_parts(yv)
        gy16 = gy.astype(BF16)
        sz = _sigmoid(_dot(gy16, gw_ref[...]) + gb_ref[...])
        r, sh = _rms_stats(gy * sz)
        dout = d_ref[...]
        dog_ref[...] += jnp.sum(dout * sh, axis=0, keepdims=True)
        dsh = dout * og_ref[...]
        ds = r * (dsh - sh * jnp.mean(dsh * sh, axis=-1, keepdims=True))
        dz = ds * gy * sz * (1.0 - sz)
        dz16 = dz.astype(BF16)
        dgb_ref[...] += jnp.sum(dz, axis=0, keepdims=True)
        dgw_ref[...] += _dot_tn(gy16, dz16)
        dgy = ds * sz + _dot_nt(dz16, gw_ref[...])
        dgelu = 0.5 * (1.0 + th) + 0.5 * yv * (1.0 - th * th) * (_GELU_K * (1.0 + 3.0 * _GELU_C * yv * yv))
        dy = dgy * dgelu
        dy_ref[...] = dy.astype(BF16)
        du_ref[...] = dy * dk_ref[...]
        dd_ref[...] += jnp.sum(dy * u_ref[...], axis=0, keepdims=True)

    vec = pl.BlockSpec((1, c), lambda i: (0, 0))
    row = pl.BlockSpec((tm, c), lambda i: (i, 0))
    mat = pl.BlockSpec(glu_w.shape, lambda i: (0, 0))
    return pl.pallas_call(
        body, name=name, grid=(n // tm,),
        in_specs=[pl.BlockSpec((tm, c), lambda i: (i, d_block)), row, pl.BlockSpec((tm, c), lambda i: (i, u_block)),
                  vec, mat, vec, vec],
        out_specs=[row, row, mat, vec, vec, vec],
        out_shape=[_sds((n, c), BF16), _sds((n, c), F32), _sds(glu_w.shape, F32)] + [_sds((1, c), F32)] * 3,
        compiler_params=_cparams("arbitrary"),
    )(dmix, y, proj, d_skip, glu_w, glu_b, out_g)


def _mesh_pos():
    return tuple(lax.axis_index(a) for a in MESH_AXES)


def _other_chips(x, y):
    return [(1 - x, y), (x, 1 - y), (1 - x, 1 - y)]


def _remote(src, dst, send_sem, recv_sem, dev):
    return pltpu.make_async_remote_copy(src_ref=src, dst_ref=dst, send_sem=send_sem, recv_sem=recv_sem,
                                        device_id=dev, device_id_type=pl.DeviceIdType.MESH)


def _hbm_call(name, body, operands, out_shapes, scratch):
    hbm = pl.BlockSpec(memory_space=pltpu.HBM)
    return pl.pallas_call(body, name=name, in_specs=[hbm] * len(operands), out_specs=[hbm] * len(out_shapes),
                          out_shape=out_shapes, scratch_shapes=scratch)(*operands)


_Exchange = collections.namedtuple("_Exchange", "operands out_shapes scratch start finish")


def _run_exchange(name, plan):
    nin, nout = len(plan.operands), len(plan.out_shapes)

    def body(*refs):
        parts = refs[:nin], refs[nin:nin + nout], refs[nin + nout:]
        plan.start(*parts)
        plan.finish(*parts)

    return _hbm_call(name, body, plan.operands, plan.out_shapes, plan.scratch)


def _gather_plan(blocks):
    nop = len(blocks)

    def copies(x_refs, o_refs, sems):
        send_sems, recv_sems, local_sems = sems
        x, y, c = _mesh_pos()
        me, sibling = (x, y, c), (x, y, 1 - c)
        chips = _other_chips(x, y)

        def copy(i, k, block_of, to, src=None):
            dst = o_refs[i].at[4 * block_of[0] + 2 * block_of[1] + block_of[2]]
            return _remote(dst if src is None else src, dst, send_sems.at[i, k], recv_sems.at[i, k], to)

        own = [pltpu.make_async_copy(x_refs[i], o_refs[i].at[4 * x + 2 * y + c], local_sems.at[i]) for i in range(nop)]
        first = []
        for i in range(nop):
            first.append(copy(i, 0, me, sibling, src=x_refs[i]))
            first += [copy(i, 1 + j, me, (*chip, c), src=x_refs[i]) for j, chip in enumerate(chips)]
        return copy, own, first, me, sibling, chips, c

    def start(x_refs, o_refs, sems):
        _, own, first, *_ = copies(x_refs, o_refs, sems)
        for cp in own + first:
            cp.start()

    def finish(x_refs, o_refs, sems):
        copy, own, first, me, sibling, chips, c = copies(x_refs, o_refs, sems)
        passed = []
        for i in range(nop):
            for j, chip in enumerate(chips):
                copy(i, 1 + j, (*chip, c), me).wait_recv()
                passed.append(copy(i, 4 + j, (*chip, c), sibling))
                passed[-1].start()
        for i in range(nop):
            copy(i, 0, sibling, me).wait_recv()
            for j, chip in enumerate(chips):
                copy(i, 4 + j, (*chip, 1 - c), me).wait_recv()
        for cp in first + passed:
            cp.wait_send()
        for cp in own:
            cp.wait()

    return _Exchange(list(blocks), [_sds((N_DEV,) + b.shape, b.dtype) for b in blocks],
                     [pltpu.SemaphoreType.DMA((nop, N_DEV - 1)), pltpu.SemaphoreType.DMA((nop, N_DEV - 1)),
                      pltpu.SemaphoreType.DMA((nop,))], start, finish)


def _core_exchange_plan(grads):
    nop = len(grads)

    def copies(x_refs, o_refs, sems):
        send_sems, recv_sems = sems
        x, y, c = _mesh_pos()
        return [_remote(x_refs[i].at[2 * q + (1 - c)], o_refs[i].at[q], send_sems.at[i, q], recv_sems.at[i, q],
                        (x, y, 1 - c)) for i in range(nop) for q in range(N_DEV // 2)]

    def start(x_refs, o_refs, sems):
        for cp in copies(x_refs, o_refs, sems):
            cp.start()

    def finish(x_refs, o_refs, sems):
        for cp in copies(x_refs, o_refs, sems):
            cp.wait()

    return _Exchange(list(grads), [_sds((N_DEV // 2,) + g.shape[1:], g.dtype) for g in grads],
                     [pltpu.SemaphoreType.DMA((nop, N_DEV // 2)), pltpu.SemaphoreType.DMA((nop, N_DEV // 2))],
                     start, finish)


def _pair_sum(name, grad, other):
    nchip, _, r, c = grad.shape
    tr = _pick(r, max(SUBLANES, TILE["sum_bytes"] // (4 * c)), SUBLANES)
    core = lax.axis_index("c").astype(jnp.int32).reshape(1)

    def body(core_ref, g_ref, o_ref, s_ref):
        s_ref[0] = (g_ref[0, 0] + o_ref[0]).astype(s_ref.dtype)

    tile = pl.BlockSpec((1, tr, c), lambda q, t, core_ref: (q, t, 0))
    return pl.pallas_call(
        body, name=name,
        grid_spec=pltpu.PrefetchScalarGridSpec(
            num_scalar_prefetch=1, grid=(nchip, r // tr),
            in_specs=[pl.BlockSpec((1, 1, tr, c), lambda q, t, core_ref: (q, core_ref[0], t, 0)), tile],
            out_specs=tile),
        out_shape=_sds((nchip, r, c), BF16),
        compiler_params=_cparams("parallel", "parallel"),
    )(core, grad, other)


def _chip_exchange_plan(sums):
    nop = len(sums)

    def copies(x_refs, o_refs, sems, arriving):
        send_sems, recv_sems, local_sems = sems
        x, y, c = _mesh_pos()
        mine = 2 * x + y
        out = []
        for i in range(nop):
            for j, (px, py) in enumerate(_other_chips(x, y)):
                theirs = 2 * px + py
                src, dst = (mine, theirs) if arriving else (theirs, mine)
                out.append(_remote(x_refs[i].at[src], o_refs[i].at[dst], send_sems.at[i, j], recv_sems.at[i, j],
                                   (px, py, c)))
        if not arriving:
            out += [pltpu.make_async_copy(x_refs[i].at[mine], o_refs[i].at[mine], local_sems.at[i]) for i in range(nop)]
        return out

    def start(x_refs, o_refs, sems):
        for cp in copies(x_refs, o_refs, sems, False):
            cp.start()

    def finish(x_refs, o_refs, sems):
        for cp in copies(x_refs, o_refs, sems, True):
            cp.wait_recv()
        mine = copies(x_refs, o_refs, sems, False)
        for cp in mine[:3 * nop]:
            cp.wait_send()
        for cp in mine[3 * nop:]:
            cp.wait()

    return _Exchange(list(sums), [_sds(s.shape, s.dtype) for s in sums],
                     [pltpu.SemaphoreType.DMA((nop, 3)), pltpu.SemaphoreType.DMA((nop, 3)), pltpu.SemaphoreType.DMA((nop,))],
                     start, finish)


def _part_rows(npart, r, c):
    return _pick(r, max(SUBLANES, TILE["sum_bytes"] // (4 * npart * c)), SUBLANES)


def _sum_slots(p_ref):
    g = p_ref[0].astype(F32)
    for k in range(1, p_ref.shape[0]):
        g = g + p_ref[k].astype(F32)
    return g


def _adamw_step(g, w, m, v):
    c1 = 1.0 - ADAM_B1 ** ADAM_STEP
    c2 = 1.0 - ADAM_B2 ** ADAM_STEP
    nm = ADAM_B1 * m + (1.0 - ADAM_B1) * g
    nv = ADAM_B2 * v + (1.0 - ADAM_B2) * (g * g)
    return -ADAM_LR * ((nm / c1) / (jnp.sqrt(nv / c2) + ADAM_EPS) + ADAM_WD * w), nm, nv


def _adamw_small(name, parts, ws, ms, vs):
    nparam, nall = len(ws), len(parts)

    def body(*refs):
        p_refs = refs[:nall]
        w_refs, m_refs, v_refs = (refs[nall + k * nparam:nall + (k + 1) * nparam] for k in range(3))
        outs = refs[nall + 3 * nparam:]
        for p in range(nall):
            g = _sum_slots(p_refs[p])
            if p < nparam:
                delta, nm, nv = _adamw_step(g, w_refs[p][...], m_refs[p][...], v_refs[p][...])
                for o_ref, val in zip(outs[4 * p:4 * p + 4], (g, delta, nm, nv)):
                    o_ref[...] = val
            else:
                outs[4 * nparam + p - nparam][...] = g

    shapes = [_sds(w.shape, F32) for w in ws for _ in range(4)] + [_sds(p.shape[1:], F32) for p in parts[nparam:]]
    res = pl.pallas_call(body, name=name, out_shape=shapes,
                         compiler_params=pltpu.CompilerParams(vmem_limit_bytes=VMEM_LIMIT))(*parts, *ws, *ms, *vs)
    return [res[4 * p:4 * p + 4] for p in range(nparam)] + [[r] for r in res[4 * nparam:]]


def _adamw(name, parts, w, m, v):
    npart, r, c = parts.shape
    lead = len(w.shape) - 2
    tr = _part_rows(npart, r, c)
    at = (0,) * lead + (slice(None), slice(None))

    def body(p_ref, w_ref, m_ref, v_ref, g_ref, d_ref, nm_ref, nv_ref):
        g = _sum_slots(p_ref)
        delta, nm, nv = _adamw_step(g, w_ref[at], m_ref[at], v_ref[at])
        g_ref[at] = g
        nm_ref[at] = nm
        nv_ref[at] = nv
        d_ref[at] = delta

    row = pl.BlockSpec((1,) * lead + (tr, c), lambda i: (0,) * lead + (i, 0))
    return pl.pallas_call(
        body, name=name, grid=(r // tr,),
        in_specs=[pl.BlockSpec((npart, tr, c), lambda i: (0, i, 0)), row, row, row],
        out_specs=[row] * 4,
        out_shape=[_sds(w.shape, F32)] * 4,
        compiler_params=_cparams("parallel"),
    )(parts, w, m, v)


def _block_diag(rows_gh, groups):
    gh, p = rows_gh.shape
    own = (jnp.arange(gh)[:, None] // (gh // groups) == jnp.arange(groups)[None, :]).astype(rows_gh.dtype)
    return (own[:, :, None] * rows_gh[:, None, :]).reshape(gh, groups * p)


def _block_diag_take(dense, groups):
    gh = dense.shape[0]
    p = dense.shape[1] // groups
    own = (jnp.arange(gh)[:, None] // (gh // groups) == jnp.arange(groups)[None, :]).astype(dense.dtype)
    return jnp.sum(dense.reshape(gh, groups, p) * own[:, :, None], axis=1)


FFN1 = ("ffn1_w1", "ffn1_w3", "ffn1_w2")
MIXER = ("w_in", "ssm_glu_w", "w_out")
FFN2 = ("ffn2_w1", "ffn2_w3", "ffn2_w2")
BIG = FFN1 + MIXER + FFN2
COL_SHARDED = ("ffn1_w1", "ffn1_w3", "w_in", "ffn2_w1", "ffn2_w3", "conv_w")
SMALL = ("norm_ffn1", "norm_mix", "conv_b", "conv_ln_g", "conv_ln_b", "conv_out_g", "ssm_A_re", "ssm_A_im",
         "ssm_log_dt", "ssm_B_re", "ssm_B_im", "ssm_C_re", "ssm_C_im", "ssm_D", "ssm_glu_b", "ssm_out_g",
         "norm_ffn2", "norm_final")
WEIGHTS = ("norm_ffn1", "ffn1_w1", "ffn1_w3", "ffn1_w2", "norm_mix", "w_in", "conv_w", "conv_b", "conv_ln_g",
           "conv_ln_b", "conv_out_g", "ssm_A_re", "ssm_A_im", "ssm_log_dt", "ssm_B_re", "ssm_B_im", "ssm_C_re",
           "ssm_C_im", "ssm_D", "ssm_glu_w", "ssm_glu_b", "ssm_out_g", "w_out", "norm_ffn2", "ffn2_w1", "ffn2_w3",
           "ffn2_w2", "norm_final")


def _ffn_backward(tag, dxo, x, g, w1, w3, w2, saved, exchange=None, reduce_plan=None):
    a, b, h = saved
    (da, db, hid, dxh), got = _ffn_bwd_hidden(tag + "_bwd_hidden", dxo, a, b, w2, exchange=exchange)
    dws = [_mm_tn(tag + "_dw1", da, h), _mm_tn(tag + "_dw3", db, h), _mm_tn(tag + "_dw2", hid, dxh)]
    f = a.shape[1]
    (dx, dg), reduced = _dx_rms_bwd(tag + "_bwd_dx", [(da, f, 0, w1, f, 0), (db, f, 0, w3, f, 0)], dxo, x, g,
                                    exchange=reduce_plan(dws) if reduce_plan else None)
    return (dx, dg, dws), got, reduced


def _reduce_in_chip(names, grads):
    send = [g.reshape((N_DEV, -1) + g.shape[1:]) for g in grads]

    def then(from_core):
        return _chip_exchange_plan([_pair_sum("pair_sum_" + k, s.reshape((N_DEV // 2, 2) + s.shape[1:]), o)
                                    for k, s, o in zip(names, send, from_core)])

    return _core_exchange_plan(send), then


def kernel(x, norm_ffn1, ffn1_w1, ffn1_w3, ffn1_w2, norm_mix, w_in, conv_w, conv_b, conv_ln_g, conv_ln_b, conv_out_g, ssm_A_re, ssm_A_im, ssm_log_dt, ssm_B_re, ssm_B_im, ssm_C_re, ssm_C_im, ssm_D, ssm_glu_w, ssm_glu_b, ssm_out_g, w_out, norm_ffn2, ffn2_w1, ffn2_w3, ffn2_w2, norm_final, loss_target, m_norm_ffn1, m_ffn1_w1, m_ffn1_w3, m_ffn1_w2, m_norm_mix, m_w_in, m_conv_w, m_conv_b, m_conv_ln_g, m_conv_ln_b, m_conv_out_g, m_ssm_A_re, m_ssm_A_im, m_ssm_log_dt, m_ssm_B_re, m_ssm_B_im, m_ssm_C_re, m_ssm_C_im, m_ssm_D, m_ssm_glu_w, m_ssm_glu_b, m_ssm_out_g, m_w_out, m_norm_ffn2, m_ffn2_w1, m_ffn2_w3, m_ffn2_w2, m_norm_final, v_norm_ffn1, v_ffn1_w1, v_ffn1_w3, v_ffn1_w2, v_norm_mix, v_w_in, v_conv_w, v_conv_b, v_conv_ln_g, v_conv_ln_b, v_conv_out_g, v_ssm_A_re, v_ssm_A_im, v_ssm_log_dt, v_ssm_B_re, v_ssm_B_im, v_ssm_C_re, v_ssm_C_im, v_ssm_D, v_ssm_glu_w, v_ssm_glu_b, v_ssm_out_g, v_w_out, v_norm_ffn2, v_ffn2_w1, v_ffn2_w3, v_ffn2_w2, v_norm_final):
    args = dict(locals())
    wt = {n: args[n] for n in WEIGHTS}
    mom = {n: args["m_" + n] for n in WEIGHTS}
    var = {n: args["v_" + n] for n in WEIGHTS}

    bsz, seq, d = x.shape
    n = bsz * seq
    c = conv_b.shape[-1]
    groups = c // SSM_GROUP
    gp = groups * SSM_STATE
    u_b = 2

    shard = {k: (wt[k][0].T if k in COL_SHARDED else wt[k][0]).astype(BF16) for k in BIG}
    gathered = _run_exchange("gather_weights_ffn1", _gather_plan([shard[k] for k in FFN1]))
    full = {k: g.reshape(-1, g.shape[-1]) for k, g in zip(FFN1, gathered)}
    gather_rest = _gather_plan([shard[k] for k in MIXER + FFN2] + [wt["conv_w"][0]])

    vec = lambda k: wt[k].reshape(1, -1)
    g_ffn1, g_mix, g_ffn2, g_fin = vec("norm_ffn1"), vec("norm_mix"), vec("norm_ffn2"), vec("norm_final")
    cb, lng, lnb, cog = vec("conv_b"), vec("conv_ln_g"), vec("conv_ln_b"), vec("conv_out_g")
    d_skip, glu_b, sog = vec("ssm_D"), vec("ssm_glu_b"), vec("ssm_out_g")

    a_re, a_im = wt["ssm_A_re"][0], wt["ssm_A_im"][0]
    log_dt = wt["ssm_log_dt"][0].reshape(groups, 1)
    bt_re = wt["ssm_B_re"][0].transpose(0, 2, 1).reshape(groups * SSM_GROUP, SSM_STATE)
    bt_im = wt["ssm_B_im"][0].transpose(0, 2, 1).reshape(groups * SSM_GROUP, SSM_STATE)
    c_re = wt["ssm_C_re"][0].reshape(groups * SSM_GROUP, SSM_STATE)
    c_im = wt["ssm_C_im"][0].reshape(groups * SSM_GROUP, SSM_STATE)
    per_chan = lambda t: jnp.repeat(t, SSM_GROUP, axis=0)
    ssm_prim = (a_re, a_im, log_dt, per_chan(a_re), per_chan(a_im), per_chan(jnp.broadcast_to(log_dt, a_re.shape)),
                bt_re, bt_im)
    pw_r, pw_i, bb_r, bb_i = _ssm_prep("ssm_prep", ssm_prim)
    tab_f = _scan_tables(pw_r, pw_i, False)
    tab_b = _scan_tables(pw_r, pw_i, True)
    bbd = jnp.concatenate([_block_diag(bb_r, groups), _block_diag(bb_i, groups)], axis=1).astype(BF16)
    cdt = jnp.concatenate([_block_diag(c_re, groups), -_block_diag(c_im, groups)], axis=1).astype(BF16)

    x0 = x.reshape(n, d)
    (x1, *ffn1_saved), gathered = _ffn_fwd("ffn1_fwd", x0, g_ffn1, full["ffn1_w1"], full["ffn1_w3"], full["ffn1_w2"],
                                           exchange=gather_rest)
    full.update({k: g.reshape(-1, g.shape[-1]) for k, g in zip(MIXER + FFN2, gathered)})
    conv_w_full = gathered[-1].transpose(1, 0, 2).reshape(CONV_WIDTH, c)
    conv_w_pad = jnp.pad(conv_w_full, ((0, CONV_HALO - CONV_WIDTH), (0, 0)))
    (proj,), h2 = _rms_mm("mix_in", x1, g_mix, [full["w_in"]], F32)
    proj3 = proj.reshape(bsz, seq, 3 * c)
    an3, cv3 = _conv_fwd("conv_fwd", proj3, conv_w_pad, cb, lng, lnb, cog)
    an = an3.reshape(n, c)
    xs3, cx3 = _scan_fwd("scan_fwd", tab_f, proj3, u_b, bbd, cdt)
    xs = xs3.reshape(n, 2 * gp)
    y, sn = _ssm_out_fwd("ssm_out_fwd", cx3.reshape(n, c), proj, u_b, d_skip, full["ssm_glu_w"], glu_b, sog)
    w_o = full["w_out"]
    x2, _ = _row_mm("mix_out", [(an, c, 0, w_o, c, 0, False), (sn, c, 0, w_o, c, 1, False)], d, F32, add=x1)
    (x3, *ffn2_saved), _ = _ffn_fwd("ffn2_fwd", x2, g_ffn2, full["ffn2_w1"], full["ffn2_w3"], full["ffn2_w2"])
    dx3, loss_tile, d_gfin = _loss_head("loss_head", x3, g_fin, loss_target.reshape(n, d))
    loss = lax.psum(loss_tile[0, 0], MESH_AXES)

    grads, from_chips = {}, {}
    (dx2, grads["norm_ffn2"], dws), _, _ = _ffn_backward(
        "ffn2", dx3, x2, g_ffn2, full["ffn2_w1"], full["ffn2_w3"], full["ffn2_w2"], ffn2_saved)
    in_chip, across_chips = _reduce_in_chip(FFN2, dws)

    dmix, got = _row_mm("mix_out_bwd", [(dx2, d, 0, w_o, 2 * c, 0, True)], 2 * c, F32, exchange=in_chip)
    reduce_ffn2 = across_chips(got)
    grads["w_out"] = jnp.concatenate([_mm_tn("dw_out_a", an, dx2), _mm_tn("dw_out_s", sn, dx2)], axis=0)

    dy, du_skip, grads["ssm_glu_w"], grads["ssm_glu_b"], grads["ssm_out_g"], grads["ssm_D"] = _ssm_out_bwd(
        "ssm_out_bwd", dmix, 1, y, proj, u_b, d_skip, full["ssm_glu_w"], glu_b, sog)
    (lam3, du3, dab_r, dab_i), got = _scan_bwd("scan_bwd", tab_b, dy.reshape(bsz, seq, c), xs3,
                                               du_skip.reshape(bsz, seq, c), bbd, cdt, exchange=reduce_ffn2)
    from_chips.update(zip(FFN2, got))
    lam, du = lam3.reshape(n, 2 * gp), du3.reshape(n, c)
    d_bbd = _band_wgrad("ssm_dbb", proj, u_b, c, lam)
    d_cdt = _band_wgrad("ssm_dc", dy, 0, c, xs)
    d_are, d_aim, d_ldt, d_btr, d_bti = _ssm_param_grads(
        "ssm_param_grads", ssm_prim,
        dab_r.reshape(SUBLANES, groups, SSM_STATE), dab_i.reshape(SUBLANES, groups, SSM_STATE),
        _band_diag_take(d_bbd, 0, c, gp), _band_diag_take(d_bbd, 1, c, gp))
    grads["ssm_A_re"], grads["ssm_A_im"], grads["ssm_log_dt"] = d_are, d_aim, d_ldt
    grads["ssm_B_re"], grads["ssm_B_im"] = d_btr, d_bti
    grads["ssm_C_re"] = _band_diag_take(d_cdt, 0, c, gp)
    grads["ssm_C_im"] = -_band_diag_take(d_cdt, 1, c, gp)

    dconv3, d_cw, grads["conv_b"], grads["conv_ln_g"], grads["conv_ln_b"], grads["conv_out_g"] = _conv_bwd(
        "conv_bwd", dmix.reshape(bsz, seq, 2 * c), proj3, cv3, conv_w_pad, lng, lnb, cog)
    dconv = dconv3.reshape(n, 2 * c)
    grads["conv_w"] = d_cw[:CONV_WIDTH]
    grads["w_in"] = jnp.concatenate([_mm_tn("dw_in_conv", dconv, h2), _mm_tn("dw_in_ssm", du, h2)], axis=0)
    w_i = full["w_in"]
    in_chip, across_chips = _reduce_in_chip(MIXER, [grads[k] for k in MIXER])
    (dx1, grads["norm_mix"]), got = _dx_rms_bwd("mix_in_bwd", [(dconv, 2 * c, 0, w_i, 2 * c, 0), (du, c, 0, w_i, c, 2)],
                                                dx2, x1, g_mix, exchange=in_chip)
    reduce_mixer = across_chips(got)

    def reduce_ffn1(dws):
        in_chip, across_chips = _reduce_in_chip(FFN1, dws)
        return across_chips(_run_exchange("exchange_core_ffn1", in_chip))

    (dx0, grads["norm_ffn1"], _), got, reduced = _ffn_backward(
        "ffn1", dx1, x0, g_ffn1, full["ffn1_w1"], full["ffn1_w3"], full["ffn1_w2"], ffn1_saved,
        exchange=reduce_mixer, reduce_plan=reduce_ffn1)
    from_chips.update(zip(MIXER, got))
    from_chips.update(zip(FFN1, reduced))
    grads["norm_final"] = d_gfin

    res = {}
    for k in BIG:
        parts = from_chips[k]
        if k in COL_SHARDED:
            swap = lambda t: jnp.swapaxes(t, -1, -2)
            res[k] = [swap(t) for t in _adamw("adamw_" + k, parts, swap(wt[k]), swap(mom[k]), swap(var[k]))]
        else:
            res[k] = _adamw("adamw_" + k, parts, wt[k], mom[k], var[k])

    def as_2d(k, t):
        if k in ("ssm_B_re", "ssm_B_im"):
            return t[0].transpose(0, 2, 1).reshape(-1, SSM_STATE)
        if k in ("ssm_C_re", "ssm_C_im"):
            return t[0].reshape(-1, SSM_STATE)
        if k in ("ssm_A_re", "ssm_A_im"):
            return t[0]
        return t.reshape(-1, 1) if k == "ssm_log_dt" else t.reshape(1, -1)

    def as_param(k, t):
        if k in ("ssm_B_re", "ssm_B_im"):
            t = t.reshape(groups, SSM_GROUP, SSM_STATE).transpose(0, 2, 1)
        return t.reshape(wt[k].shape)

    gathered = _run_exchange("gather_small_grads", _gather_plan([grads[k] for k in SMALL] + [grads["conv_w"]]))
    updated = _adamw_small("adamw_replicated", gathered, *[[as_2d(k, src[k]) for k in SMALL] for src in (wt, mom, var)])
    res.update({k: [as_param(k, t) for t in upd] for k, upd in zip(SMALL, updated)})
    (conv_w_grad,) = updated[-1]
    x_pos, y_pos, c_pos = (lax.axis_index(a) for a in MESH_AXES)
    cw_cols = c // N_DEV
    own_cw = lax.dynamic_slice_in_dim(conv_w_grad, (4 * x_pos + 2 * y_pos + c_pos) * cw_cols, cw_cols, axis=1)
    res["conv_w"] = _adamw("adamw_conv_w", own_cw[None], wt["conv_w"], mom["conv_w"], var["conv_w"])

    outs = [loss, dx0.reshape(bsz, seq, d)]
    for kind in range(4):
        outs += [res[k][kind] for k in WEIGHTS]
    return tuple(outs)
```

```python
import collections
import functools
import math

import jax
import jax.numpy as jnp
from jax import lax
from jax.experimental import pallas as pl
from jax.experimental.pallas import tpu as pltpu

F32 = jnp.float32
BF16 = jnp.bfloat16

EPS = 1e-6
FFN_RES = 0.5
CONV_WIDTH = 31
CONV_HALO = 32
SSM_GROUP = 16
SSM_STATE = 64
ADAM_LR, ADAM_B1, ADAM_B2, ADAM_EPS, ADAM_WD, ADAM_STEP = 0.001, 0.9, 0.999, 1e-08, 0.01, 10

N_DEV = 8
MESH_AXES = ("x", "y", "c")
SUBLANES = 8
LANES = 128
V7X_VMEM_BYTES = 64 * 2**20
VMEM_LIMIT = V7X_VMEM_BYTES - 8 * 2**20

TILE = dict(row=256, ffn_m=512, mm_bytes=8 * 2**20, up_m=1024, up_n=256, wide_n=2048, conv_t=512, scan_t=256, scan_w=512,
            sum_bytes=4 * 2**20)

_GELU_K = math.sqrt(2.0 / math.pi)
_GELU_C = 0.044715


def _pick(n, target, mult):
    best = None
    for t in range(mult, min(n, target) + 1, mult):
        if n % t == 0:
            best = t
    return n if best is None else best


def _cparams(*sem):
    return pltpu.CompilerParams(dimension_semantics=sem, vmem_limit_bytes=VMEM_LIMIT)


def _sds(shape, dtype):
    return jax.ShapeDtypeStruct(shape, dtype)


def _call(name, body, grid, in_specs, out_specs, out_shape, operands, sem, scratch=(), exchange=None):
    if exchange is None:
        res = pl.pallas_call(body, name=name, grid=grid, in_specs=list(in_specs), out_specs=list(out_specs),
                             out_shape=list(out_shape), scratch_shapes=list(scratch),
                             compiler_params=_cparams(*sem))(*operands)
        return list(res), None
    n_in, n_out, n_scr = len(in_specs), len(out_specs), len(scratch)
    n_xin, n_xout = len(exchange.operands), len(exchange.out_shapes)
    hbm = pl.BlockSpec(memory_space=pltpu.HBM)

    def with_exchange(*refs):
        cuts, pos = [], 0
        for size in (n_in, n_xin, n_out, n_xout, n_scr):
            cuts.append(refs[pos:pos + size])
            pos += size
        ins, x_in, outs, x_out, scr = cuts
        sems = refs[pos:]
        ids = [pl.program_id(axis) for axis in range(len(grid))]
        first = functools.reduce(lambda p, q: p & q, [i == 0 for i in ids])
        last = functools.reduce(lambda p, q: p & q, [i == g - 1 for i, g in zip(ids, grid)])

        @pl.when(first)
        def _():
            exchange.start(x_in, x_out, sems)

        body(*ins, *outs, *scr)

        @pl.when(last)
        def _():
            exchange.finish(x_in, x_out, sems)

    res = pl.pallas_call(
        with_exchange, name=name, grid=grid, in_specs=list(in_specs) + [hbm] * n_xin,
        out_specs=list(out_specs) + [hbm] * n_xout, out_shape=list(out_shape) + list(exchange.out_shapes),
        scratch_shapes=list(scratch) + list(exchange.scratch),
        compiler_params=_cparams(*["arbitrary"] * len(grid)))(*operands, *exchange.operands)
    return list(res[:n_out]), list(res[n_out:])


def _dot(a, b):
    return jnp.dot(a, b, preferred_element_type=F32)


def _dot_nt(a, b):
    return lax.dot_general(a, b, (((1,), (1,)), ((), ())), preferred_element_type=F32)


def _dot_tn(a, b):
    return lax.dot_general(a, b, (((0,), (0,)), ((), ())), preferred_element_type=F32)


def _sigmoid(x):
    return 0.5 * jnp.tanh(0.5 * x) + 0.5


def _rms_stats(x):
    r = lax.rsqrt(jnp.mean(x * x, axis=-1, keepdims=True) + EPS)
    return r, x * r


def _rms_bwd(x, g, dy):
    r, xh = _rms_stats(x)
    dxh = dy * g
    dx = r * (dxh - xh * jnp.mean(dxh * xh, axis=-1, keepdims=True))
    return dx, jnp.sum(dy * xh, axis=0, keepdims=True)


def _rms_mm(name, x, g, ws, out_dtype):
    n, d = x.shape
    f = ws[0].shape[0]
    nw = len(ws)
    tm, tn = _pick(n, TILE["up_m"], 16), _pick(f, TILE["wide_n"], LANES)

    def body(x_ref, g_ref, *refs):
        w_refs, o_refs, h_ref = refs[:nw], refs[nw:2 * nw], refs[2 * nw]

        @pl.when(pl.program_id(1) == 0)
        def _():
            _, xh = _rms_stats(x_ref[...])
            h_ref[...] = (xh * g_ref[...]).astype(BF16)

        h = h_ref[...]
        for w_ref, o_ref in zip(w_refs, o_refs):
            o_ref[...] = _dot_nt(h, w_ref[...]).astype(o_ref.dtype)

    outs = pl.pallas_call(
        body, name=name, grid=(n // tm, f // tn),
        in_specs=[pl.BlockSpec((tm, d), lambda i, j: (i, 0)), pl.BlockSpec((1, d), lambda i, j: (0, 0))]
        + [pl.BlockSpec((tn, d), lambda i, j: (j, 0))] * nw,
        out_specs=[pl.BlockSpec((tm, tn), lambda i, j: (i, j))] * nw + [pl.BlockSpec((tm, d), lambda i, j: (i, 0))],
        out_shape=[_sds((n, f), out_dtype)] * nw + [_sds((n, d), BF16)],
        compiler_params=_cparams("parallel", "arbitrary"),
    )(x, g, *ws)
    return outs[:nw], outs[nw]


def _ffn_fwd(name, x, g, w1t, w3t, w2, exchange=None, head=None):
    n, d = x.shape
    f = w2.shape[0]
    tm, tn = _pick(n, TILE["ffn_m"], 16), _pick(f, TILE["up_n"], LANES)

    def body(x_ref, g_ref, w1_ref, w3_ref, w2_ref, *refs):
        (gf_ref, t_ref), refs = (refs[:2], refs[2:]) if head else ((None, None), refs)
        o_ref, a_ref, b_ref, h_ref = refs[:4]
        xv = x_ref[...]
        _, xh = _rms_stats(xv)
        h = (xh * g_ref[...]).astype(BF16)
        h_ref[...] = h
        acc = None
        for c0 in range(0, f, tn):
            cols = pl.ds(c0, tn)
            av, bv = _dot_nt(h, w1_ref[cols, :]), _dot_nt(h, w3_ref[cols, :])
            a_ref[:, cols] = av.astype(BF16)
            b_ref[:, cols] = bv.astype(BF16)
            t = _dot((av * _sigmoid(av) * bv).astype(BF16), w2_ref[cols, :])
            acc = t if acc is None else acc + t
        out = xv + FFN_RES * acc
        if head is None:
            o_ref[...] = out
        else:
            loss_ref, dg_ref = refs[4:]

            @pl.when(pl.program_id(0) == 0)
            def _():
                loss_ref[...] = jnp.zeros_like(loss_ref)
                dg_ref[...] = jnp.zeros_like(dg_ref)

            dx, loss, dg = _loss_head_rows(out, gf_ref[...], t_ref[...])
            o_ref[...] = dx
            loss_ref[...] += loss
            dg_ref[...] += dg

    row = pl.BlockSpec((tm, d), lambda i: (i, 0))
    wide = pl.BlockSpec((tm, f), lambda i: (i, 0))
    vec = pl.BlockSpec((1, d), lambda i: (0, 0))
    held = pl.BlockSpec((f, d), lambda i: (0, 0), pipeline_mode=pl.Buffered(1))
    extra_in, extra_out, extra_shape = ([vec, row], [pl.BlockSpec((SUBLANES, LANES), lambda i: (0, 0)), vec],
                                        [_sds((SUBLANES, LANES), F32), _sds((1, d), F32)]) if head else ([], [], [])
    return _call(
        name, body, (n // tm,), [row, vec, held, held, held] + extra_in, [row, wide, wide, row] + extra_out,
        [_sds((n, d), F32), _sds((n, f), BF16), _sds((n, f), BF16), _sds((n, d), BF16)] + extra_shape,
        (x, g, w1t, w3t, w2) + (tuple(head) if head else ()), ("arbitrary",) if head else ("parallel",),
        exchange=exchange)


def _ffn_bwd_hidden(name, dxo, a, b, w2, exchange=None):
    n, d = dxo.shape
    f = a.shape[1]
    tm, tn = _pick(n, TILE["row"], 16), _pick(f, TILE["up_n"], LANES)

    def body(dx_ref, a_ref, b_ref, w_ref, da_ref, db_ref, hid_ref, dxh_ref):
        dxh = (FFN_RES * dx_ref[...]).astype(BF16)
        dxh_ref[...] = dxh
        for c0 in range(0, f, tn):
            cols = pl.ds(c0, tn)
            dhid = _dot_nt(dxh, w_ref[cols, :])
            av, bv = a_ref[:, cols].astype(F32), b_ref[:, cols].astype(F32)
            sig = _sigmoid(av)
            silu = av * sig
            da_ref[:, cols] = (dhid * bv * (sig * (1.0 + av * (1.0 - sig)))).astype(BF16)
            db_ref[:, cols] = (dhid * silu).astype(BF16)
            hid_ref[:, cols] = (silu * bv).astype(BF16)

    wide = pl.BlockSpec((tm, f), lambda i: (i, 0))
    row = pl.BlockSpec((tm, d), lambda i: (i, 0))
    return _call(
        name, body, (n // tm,), [row, wide, wide, pl.BlockSpec((f, d), lambda i: (0, 0))], [wide, wide, wide, row],
        [_sds((n, f), BF16)] * 3 + [_sds((n, d), BF16)], (dxo, a, b, w2), ("parallel",), exchange=exchange)


def _loss_head_rows(x, g, target):
    r, xh = _rms_stats(x)
    err = xh * g - target
    dy = err * (1.0 / x.shape[-1])
    dxh = dy * g
    dx = r * (dxh - xh * jnp.mean(dxh * xh, axis=-1, keepdims=True))
    return dx, 0.5 * jnp.sum(jnp.mean(err * err, axis=-1, keepdims=True)), jnp.sum(dy * xh, axis=0, keepdims=True)


def _dx_rms_bwd(name, pairs, dxo, x, g, exchange=None):
    n, dm = x.shape
    tm = _pick(n, TILE["ffn_m"], 16)
    npair = len(pairs)

    def body(*refs):
        d_refs, w_refs = refs[:npair], refs[npair:2 * npair]
        dxo_ref, x_ref, g_ref, dx_ref, dg_ref = refs[2 * npair:]

        @pl.when(pl.program_id(0) == 0)
        def _():
            dg_ref[...] = jnp.zeros_like(dg_ref)

        dh = None
        for d_ref, w_ref in zip(d_refs, w_refs):
            t = _dot(d_ref[...].astype(BF16), w_ref[...])
            dh = t if dh is None else dh + t
        dx, dg = _rms_bwd(x_ref[...], g_ref[...], dh)
        dx_ref[...] = dxo_ref[...] + dx
        dg_ref[...] += dg

    row = pl.BlockSpec((tm, dm), lambda i: (i, 0))
    d_specs = [pl.BlockSpec((tm, p[1]), functools.partial(lambda i, cb: (i, cb), cb=p[2])) for p in pairs]
    w_specs = [pl.BlockSpec((p[4], dm), functools.partial(lambda i, rb: (rb, 0), rb=p[5]), pipeline_mode=pl.Buffered(1))
               for p in pairs]
    return _call(
        name, body, (n // tm,), d_specs + w_specs + [row, row, pl.BlockSpec((1, dm), lambda i: (0, 0))],
        [row, pl.BlockSpec((1, dm), lambda i: (0, 0))], [_sds((n, dm), F32), _sds((1, dm), F32)],
        (*[p[0] for p in pairs], *[p[3] for p in pairs], dxo, x, g), ("arbitrary",), exchange=exchange)


def _mm_tn(name, a, b, a_cols=None, b_cols=None, exchange=None):
    n = a.shape[0]
    a0, ma = a_cols if a_cols else (0, a.shape[1])
    b0, mb = b_cols if b_cols else (0, b.shape[1])
    assert a0 % ma == 0 and b0 % mb == 0
    ab, bb = a0 // ma, b0 // mb
    tk = _pick(n, TILE["mm_bytes"] // (ma * a.dtype.itemsize + mb * b.dtype.itemsize), 16)

    def body(a_ref, b_ref, o_ref):
        @pl.when(pl.program_id(0) == 0)
        def _():
            o_ref[...] = jnp.zeros_like(o_ref)

        o_ref[...] += _dot_tn(a_ref[...].astype(BF16), b_ref[...].astype(BF16))

    (out,), got = _call(
        name, body, (n // tk,), [pl.BlockSpec((tk, ma), lambda k: (k, ab)), pl.BlockSpec((tk, mb), lambda k: (k, bb))],
        [pl.BlockSpec((ma, mb), lambda k: (0, 0))], [_sds((ma, mb), F32)], (a, b), ("arbitrary",), exchange=exchange)
    return out if exchange is None else (out, got)


def _row_mm(name, pairs, out_w, out_dtype, add=None, exchange=None):
    n = pairs[0][0].shape[0]
    tm = _pick(n, TILE["row"], 16)
    npair = len(pairs)

    def body(*refs):
        a_refs, w_refs = refs[:npair], refs[npair:2 * npair]
        add_ref = refs[2 * npair] if add is not None else None
        o_ref = refs[-1]
        acc = None
        for a_ref, w_ref, p in zip(a_refs, w_refs, pairs):
            av = a_ref[...].astype(BF16)
            t = _dot_nt(av, w_ref[...]) if p[6] else _dot(av, w_ref[...])
            acc = t if acc is None else acc + t
        if add_ref is not None:
            acc = acc + add_ref[...].astype(F32)
        o_ref[...] = acc.astype(o_ref.dtype)

    a_specs = [pl.BlockSpec((tm, p[1]), functools.partial(lambda i, cb: (i, cb), cb=p[2])) for p in pairs]
    w_specs = [pl.BlockSpec((p[4], p[3].shape[1]), functools.partial(lambda i, rb: (rb, 0), rb=p[5])) for p in pairs]
    add_specs = [pl.BlockSpec((tm, out_w), lambda i: (i, 0))] if add is not None else []
    (out,), got = _call(
        name, body, (n // tm,), a_specs + w_specs + add_specs, [pl.BlockSpec((tm, out_w), lambda i: (i, 0))],
        [_sds((n, out_w), out_dtype)],
        (*[p[0] for p in pairs], *[p[3] for p in pairs], *([add] if add is not None else [])), ("parallel",),
        exchange=exchange)
    return out, got


def _conv_post(c, ln_g, ln_b, out_g):
    mu = jnp.mean(c, axis=-1, keepdims=True)
    xc = c - mu
    rstd = lax.rsqrt(jnp.mean(xc * xc, axis=-1, keepdims=True) + EPS)
    nrm = xc * rstd
    l = nrm * ln_g + ln_b
    sig = _sigmoid(l)
    s = l * sig
    r, sh = _rms_stats(s)
    return sh * out_g, (rstd, nrm, l, sig, r, sh)


def _tap_groups(first):
    groups = []
    for r in range(SUBLANES):
        taps = [(s - r, s - first) for s in range(first, first + CONV_WIDTH) if s % SUBLANES == r]
        if taps:
            groups.append((r, taps))
    return groups


def _conv_taps(a_ref, w_ref, b_ref, first, rows, flip=False):
    acc = None
    for r, taps in _tap_groups(first):
        ext = rows if r == 0 else rows + SUBLANES
        part = None
        for base, k in taps:
            kk = CONV_WIDTH - 1 - k if flip else k
            t = w_ref[kk:kk + 1, :] * a_ref[pl.ds(base, ext), :]
            part = t if part is None else part + t
        if r:
            b_ref[...] = part
            part = b_ref[pl.ds(r, rows), :]
        acc = part if acc is None else acc + part
    return acc


def _conv_post_bwd(cv, dout, ln_g, ln_b, out_g):
    _, (rstd, nrm, l, sig, r, sh) = _conv_post(cv, ln_g, ln_b, out_g)
    dsh = dout * out_g
    ds = r * (dsh - sh * jnp.mean(dsh * sh, axis=-1, keepdims=True))
    dl = ds * (sig * (1.0 + l * (1.0 - sig)))
    dn = dl * ln_g
    dc = rstd * (dn - jnp.mean(dn, axis=-1, keepdims=True) - nrm * jnp.mean(dn * nrm, axis=-1, keepdims=True))
    col_sum = lambda t: jnp.sum(t, axis=0, keepdims=True)
    return dc, col_sum(dout * sh), col_sum(dl * nrm), col_sum(dl)


def _conv_fwd(name, proj3, conv_w, conv_b, ln_g, ln_b, out_g):
    bsz, seq, _ = proj3.shape
    c = conv_w.shape[1]
    tt = _pick(seq, TILE["conv_t"], CONV_HALO)
    hb = tt // CONV_HALO
    first = CONV_HALO - (CONV_WIDTH - 1)

    def body(v_ref, g_ref, vp_ref, gp_ref, w_ref, cb_ref, lg_ref, lb_ref, og_ref, o_ref, cv_ref, a_ref, b_ref):
        keep = (pl.program_id(1) > 0).astype(F32)
        a_ref[pl.ds(0, CONV_HALO), :] = keep * vp_ref[0] * _sigmoid(gp_ref[0])
        a_ref[pl.ds(CONV_HALO, tt), :] = v_ref[0] * _sigmoid(g_ref[0])
        cv = _conv_taps(a_ref, w_ref, b_ref, first, tt) + cb_ref[...]
        cv_ref[0] = cv
        out, _ = _conv_post(cv, lg_ref[...], lb_ref[...], og_ref[...])
        o_ref[0] = out.astype(BF16)

    vec = pl.BlockSpec((1, c), lambda b, i: (0, 0))
    prev = lambda col: pl.BlockSpec((1, CONV_HALO, c), lambda b, i: (b, jnp.maximum(i * hb - 1, 0), col))
    tile = pl.BlockSpec((1, tt, c), lambda b, i: (b, i, 0))
    return pl.pallas_call(
        body, name=name, grid=(bsz, seq // tt),
        in_specs=[tile, pl.BlockSpec((1, tt, c), lambda b, i: (b, i, 1)),
                  prev(0), prev(1), pl.BlockSpec(conv_w.shape, lambda b, i: (0, 0)), vec, vec, vec, vec],
        out_specs=[tile, tile],
        out_shape=[_sds((bsz, seq, c), BF16), _sds((bsz, seq, c), F32)],
        scratch_shapes=[pltpu.VMEM((CONV_HALO + tt, c), F32), pltpu.VMEM((tt + SUBLANES, c), F32)],
        compiler_params=_cparams("parallel", "arbitrary"),
    )(proj3, proj3, proj3, proj3, conv_w, conv_b, ln_g, ln_b, out_g)


def _conv_bwd(name, dmix3, proj3, cv3, conv_w, ln_g, ln_b, out_g):
    bsz, seq, _ = proj3.shape
    c = conv_w.shape[1]
    tt = _pick(seq, TILE["conv_t"], CONV_HALO)
    hb = tt // CONV_HALO
    nt = seq // tt
    last_hb = seq // CONV_HALO - 1
    ext = tt + CONV_HALO
    first = CONV_HALO - (CONV_WIDTH - 1)

    def body(v_ref, g_ref, vp_ref, gp_ref, cv_ref, cvn_ref, d_ref, dn_ref, w_ref, lg_ref, lb_ref, og_ref,
             o_ref, dw_ref, dcb_ref, dlg_ref, dlb_ref, dog_ref, a_ref, dc_ref, b_ref, ds_ref):
        i = pl.program_id(1)

        @pl.when((pl.program_id(0) == 0) & (i == 0))
        def _():
            for r in (dw_ref, dcb_ref, dlg_ref, dlb_ref, dog_ref):
                r[...] = jnp.zeros_like(r)

        keep_prev = (i > 0).astype(F32)
        keep_next = (i < nt - 1).astype(F32)
        sig_g = _sigmoid(g_ref[0])
        a_ref[pl.ds(0, CONV_HALO), :] = keep_prev * vp_ref[0] * _sigmoid(gp_ref[0])
        a_ref[pl.ds(CONV_HALO, tt), :] = v_ref[0] * sig_g

        lg, lb, og = lg_ref[...], lb_ref[...], og_ref[...]
        dc_own, d_og, d_lg, d_lb = _conv_post_bwd(cv_ref[0], d_ref[0], lg, lb, og)
        dc_next, _, _, _ = _conv_post_bwd(cvn_ref[0], keep_next * dn_ref[0], lg, lb, og)
        dog_ref[...] += d_og
        dlg_ref[...] += d_lg
        dlb_ref[...] += d_lb
        dcb_ref[...] += jnp.sum(dc_own, axis=0, keepdims=True)
        dc_ref[pl.ds(0, tt), :] = dc_own
        dc_ref[pl.ds(tt, CONV_HALO), :] = dc_next

        da = _conv_taps(dc_ref, w_ref, b_ref, 0, tt, flip=True)

        for r, taps in _tap_groups(first):
            if r:
                ds_ref[pl.ds(0, SUBLANES), :] = jnp.zeros((SUBLANES, c), F32)
                ds_ref[pl.ds(tt, SUBLANES), :] = jnp.zeros((SUBLANES, c), F32)
                ds_ref[pl.ds(r, tt), :] = dc_own
            for base, k in taps:
                prod = (ds_ref[...] * a_ref[pl.ds(base, tt + SUBLANES), :]) if r else (dc_own * a_ref[pl.ds(base, tt), :])
                dw_ref[k:k + 1, :] += jnp.sum(prod, axis=0, keepdims=True)
        val = v_ref[0]
        o_ref[0] = jnp.concatenate([da * sig_g, da * val * sig_g * (1.0 - sig_g)], axis=-1).astype(BF16)

    vec = pl.BlockSpec((1, c), lambda b, i: (0, 0))
    cur = lambda col: pl.BlockSpec((1, tt, c), lambda b, i: (b, i, col))
    prev = lambda col: pl.BlockSpec((1, CONV_HALO, c), lambda b, i: (b, jnp.maximum(i * hb - 1, 0), col))
    nxt = lambda col: pl.BlockSpec((1, CONV_HALO, c), lambda b, i: (b, jnp.minimum((i + 1) * hb, last_hb), col))
    wspec = pl.BlockSpec(conv_w.shape, lambda b, i: (0, 0))
    return pl.pallas_call(
        body, name=name, grid=(bsz, nt),
        in_specs=[cur(0), cur(1), prev(0), prev(1), cur(0), nxt(0), cur(0), nxt(0), wspec, vec, vec, vec],
        out_specs=[pl.BlockSpec((1, tt, 2 * c), lambda b, i: (b, i, 0)), wspec, vec, vec, vec, vec],
        out_shape=[_sds((bsz, seq, 2 * c), BF16), _sds(conv_w.shape, F32)] + [_sds((1, c), F32)] * 4,
        scratch_shapes=[pltpu.VMEM((CONV_HALO + tt, c), F32), pltpu.VMEM((ext, c), F32),
                        pltpu.VMEM((tt + SUBLANES, c), F32), pltpu.VMEM((tt + SUBLANES, c), F32)],
        compiler_params=_cparams("arbitrary", "arbitrary"),
    )(proj3, proj3, proj3, proj3, cv3, cv3, dmix3, dmix3, conv_w, ln_g, ln_b, out_g)


def _ssm_discretise(a_re, a_im, log_dt):
    dt = jnp.exp(log_dt)
    zr, zi = a_re * dt, a_im * dt
    mag = jnp.exp(zr)
    ar, ai = mag * jnp.cos(zi), mag * jnp.sin(zi)
    den = a_re * a_re + a_im * a_im
    nr = ar - 1.0
    return ar, ai, (nr * a_re + ai * a_im) / den, (ai * a_re - nr * a_im) / den


def _ssm_system(a_re, a_im, log_dt, a_re_x, a_im_x, log_dt_x, bt_re, bt_im):
    ar, ai, _, _ = _ssm_discretise(a_re, a_im, log_dt)
    _, _, cr, ci = _ssm_discretise(a_re_x, a_im_x, log_dt_x)
    return ar, ai, cr * bt_re - ci * bt_im, cr * bt_im + ci * bt_re


def _ssm_prep(name, prim):
    g, p = prim[0].shape

    def body(*refs):
        pwr_ref, pwi_ref, bbr_ref, bbi_ref = refs[8:]
        ar, ai, bbr, bbi = _ssm_system(*[r[...] for r in refs[:8]])
        bbr_ref[...] = bbr
        bbi_ref[...] = bbi
        pr, pi = ar, ai
        for k in range(SUBLANES):
            pwr_ref[k] = pr
            pwi_ref[k] = pi
            pr, pi = pr * ar - pi * ai, pr * ai + pi * ar

    return pl.pallas_call(
        body, name=name,
        out_shape=[_sds((SUBLANES, g, p), F32)] * 2 + [_sds(prim[6].shape, F32)] * 2,
        compiler_params=pltpu.CompilerParams(vmem_limit_bytes=VMEM_LIMIT),
    )(*prim)


def _ssm_param_grads(name, prim, dab_r, dab_i, dbb_r, dbb_i):
    g, p = prim[0].shape
    h = prim[6].shape[0] // g

    def body(*refs):
        dar_ref, dai_ref, dbr_ref, dbi_ref = refs[8:12]
        o_ar, o_ai, o_dt, o_br, o_bi = refs[12:]
        _, vjp = jax.vjp(_ssm_system, *[r[...] for r in refs[:8]])
        ct = (jnp.sum(dar_ref[...], axis=0), jnp.sum(dai_ref[...], axis=0), dbr_ref[...], dbi_ref[...])
        d_ar, d_ai, d_dt, d_arx, d_aix, d_dtx, d_br, d_bi = vjp(ct)
        per_group = lambda t: jnp.sum(t.reshape(g, h, p), axis=1)
        o_ar[...] = d_ar + per_group(d_arx)
        o_ai[...] = d_ai + per_group(d_aix)
        o_dt[...] = d_dt + jnp.sum(per_group(d_dtx), axis=1, keepdims=True)
        o_br[...] = d_br
        o_bi[...] = d_bi

    return pl.pallas_call(
        body, name=name,
        out_shape=[_sds(prim[k].shape, F32) for k in (0, 1, 2, 6, 7)],
        compiler_params=pltpu.CompilerParams(vmem_limit_bytes=VMEM_LIMIT),
    )(*prim, dab_r, dab_i, dbb_r, dbb_i)


def _cfma(xr, xi, cr, ci, sr, si):
    return xr + (cr * sr - ci * si), xi + (cr * si + ci * sr)


def _scan_tables(pw_r, pw_i, reverse):
    gp = pw_r.shape[1] * pw_r.shape[2]
    pr, pi = pw_r.reshape(SUBLANES, gp), pw_i.reshape(SUBLANES, gp)
    if reverse:
        pi = -pi
    row = jnp.arange(SUBLANES)[:, None]
    tabs = []
    for d in (1, 2, 4):
        keep = (row < SUBLANES - d) if reverse else (row >= d)
        tabs += [jnp.where(keep, pr[d - 1][None, :], 0.0), jnp.where(keep, pi[d - 1][None, :], 0.0)]
    tabs += [pr[::-1], pi[::-1]] if reverse else [pr, pi]
    return jnp.concatenate(tabs, axis=0)


MXU_DEPTH = 256


def _bands(c, gp):
    bw = min(c, MXU_DEPTH)
    return c // bw, bw, gp * bw // c


def _band_expand(rows16, w_ref, put, c, gp):
    nb, bw, sw = _bands(c, gp)
    for s in range(nb):
        band = rows16[:, s * bw:(s + 1) * bw]
        for half in (0, gp):
            cols = pl.ds(half + s * sw, sw)
            put(cols, _dot(band, w_ref[pl.ds(s * bw, bw), cols]))


def _band_contract(get16, w_ref, c, gp):
    nb, bw, sw = _bands(c, gp)
    out = []
    for s in range(nb):
        acc = None
        for half in (0, gp):
            cols = pl.ds(half + s * sw, sw)
            t = _dot_nt(get16(cols), w_ref[pl.ds(s * bw, bw), cols])
            acc = t if acc is None else acc + t
        out.append(acc)
    return out[0] if nb == 1 else jnp.concatenate(out, axis=1)


def _band_wgrad(name, a, a_block, c, b):
    n = a.shape[0]
    gp = b.shape[1] // 2
    nb, bw, sw = _bands(c, gp)
    tk = _pick(n, TILE["mm_bytes"] // (c * a.dtype.itemsize + 2 * gp * b.dtype.itemsize), 16)

    def body(a_ref, b_ref, o_ref):
        @pl.when(pl.program_id(0) == 0)
        def _():
            o_ref[...] = jnp.zeros_like(o_ref)

        for s in range(nb):
            band = a_ref[:, s * bw:(s + 1) * bw].astype(BF16)
            for h, half in enumerate((0, gp)):
                o_ref[pl.ds(s * bw, bw), pl.ds(h * sw, sw)] += _dot_tn(
                    band, b_ref[:, pl.ds(half + s * sw, sw)].astype(BF16))

    return pl.pallas_call(
        body, name=name, grid=(n // tk,),
        in_specs=[pl.BlockSpec((tk, c), lambda k: (k, a_block)), pl.BlockSpec((tk, 2 * gp), lambda k: (k, 0))],
        out_specs=pl.BlockSpec((c, 2 * sw), lambda k: (0, 0)),
        out_shape=_sds((c, 2 * sw), F32),
        compiler_params=_cparams("arbitrary"),
    )(a, b)


def _band_diag_take(comp, half, c, gp):
    nb, bw, sw = _bands(c, gp)
    return jnp.concatenate([_block_diag_take(comp[s * bw:(s + 1) * bw, half * sw:(half + 1) * sw], bw // SSM_GROUP)
                            for s in range(nb)], axis=0)


def _scan_fwd(name, tab, proj3, u_block, bbd, cdt):
    bsz, seq, _ = proj3.shape
    c, w = bbd.shape
    gp = w // 2
    tt = _pick(seq, TILE["scan_t"], 16)
    nblk = tt // SUBLANES
    cw = _pick(gp, TILE["scan_w"], LANES)

    def body(tab_ref, u_ref, bbd_ref, cdt_ref, xs_ref, y_ref, carry_ref, bu_ref):
        @pl.when(pl.program_id(1) == 0)
        def _():
            carry_ref[...] = jnp.zeros_like(carry_ref)

        def put_bu(cols, val):
            bu_ref[0, :, cols] = val

        _band_expand(u_ref[0].astype(BF16), bbd_ref, put_bu, c, gp)

        for ch in range(gp // cw):
            re, im = pl.ds(ch * cw, cw), pl.ds(gp + ch * cw, cw)

            def blk(r, carry, re=re, im=im):
                tabs = [tab_ref[pl.ds(SUBLANES * k, SUBLANES), re] for k in range(8)]
                rows = pl.ds(pl.multiple_of(r * SUBLANES, SUBLANES), SUBLANES)
                xr, xi = bu_ref[0, rows, re], bu_ref[0, rows, im]
                for j, d in enumerate((1, 2, 4)):
                    xr, xi = _cfma(xr, xi, tabs[2 * j], tabs[2 * j + 1], pltpu.roll(xr, d, 0), pltpu.roll(xi, d, 0))
                xr, xi = _cfma(xr, xi, tabs[6], tabs[7], carry[0], carry[1])
                xs_ref[0, rows, re] = xr
                xs_ref[0, rows, im] = xi
                last = SUBLANES - 1
                return (jnp.broadcast_to(xr[last:, :], xr.shape), jnp.broadcast_to(xi[last:, :], xi.shape))

            cr, ci = lax.fori_loop(0, nblk, blk, (carry_ref[:, re], carry_ref[:, im]))
            carry_ref[:, re] = cr
            carry_ref[:, im] = ci

        y_ref[0] = _band_contract(lambda cols: xs_ref[0, :, cols].astype(BF16), cdt_ref, c, gp)

    whole = lambda arr: pl.BlockSpec(arr.shape, lambda b, t: (0, 0))
    return pl.pallas_call(
        body, name=name, grid=(bsz, seq // tt),
        in_specs=[whole(tab), pl.BlockSpec((1, tt, c), lambda b, t: (b, t, u_block)), whole(bbd), whole(cdt)],
        out_specs=[pl.BlockSpec((1, tt, w), lambda b, t: (b, t, 0)), pl.BlockSpec((1, tt, c), lambda b, t: (b, t, 0))],
        out_shape=[_sds((bsz, seq, w), F32), _sds((bsz, seq, c), F32)],
        scratch_shapes=[pltpu.VMEM((SUBLANES, w), F32), pltpu.VMEM((1, tt, w), F32)],
        compiler_params=_cparams("arbitrary", "arbitrary"),
    )(tab, proj3, bbd, cdt)


def _scan_bwd(name, tab, dy3, xs3, du_skip3, bbd, cdt, exchange=None):
    bsz, seq, w = xs3.shape
    c = bbd.shape[0]
    gp = w // 2
    tt = _pick(seq, TILE["scan_t"], 16)
    nblk = tt // SUBLANES
    cw = _pick(gp, TILE["scan_w"], LANES)
    nt = seq // tt

    def body(tab_ref, dy_ref, xs_ref, halo_ref, skip_ref, bbd_ref, cdt_ref, lam_ref, du_ref, dar_ref, dai_ref,
             carry_ref, g_ref):
        t = pl.program_id(1)

        @pl.when(t == 0)
        def _():
            carry_ref[...] = jnp.zeros_like(carry_ref)

        @pl.when((pl.program_id(0) == 0) & (t == 0))
        def _():
            dar_ref[...] = jnp.zeros_like(dar_ref)
            dai_ref[...] = jnp.zeros_like(dai_ref)

        def put_g(cols, val):
            g_ref[0, :, cols] = val

        _band_expand(dy_ref[0], cdt_ref, put_g, c, gp)

        has_prev = (t < nt - 1).astype(F32)
        row0 = lax.broadcasted_iota(jnp.int32, (SUBLANES, cw), 0) == 0
        last = SUBLANES - 1

        for ch in range(gp // cw):
            re, im = pl.ds(ch * cw, cw), pl.ds(gp + ch * cw, cw)

            def step(rows, xm1r, xm1i, state, re=re, im=im):
                tabs = [tab_ref[pl.ds(SUBLANES * k, SUBLANES), re] for k in range(8)]
                cr, ci, accr, acci = state
                lr, li = g_ref[0, rows, re], g_ref[0, rows, im]
                for j, d in enumerate((1, 2, 4)):
                    lr, li = _cfma(lr, li, tabs[2 * j], tabs[2 * j + 1],
                                   pltpu.roll(lr, SUBLANES - d, 0), pltpu.roll(li, SUBLANES - d, 0))
                lr, li = _cfma(lr, li, tabs[6], tabs[7], cr, ci)
                lam_ref[0, rows, re] = lr
                lam_ref[0, rows, im] = li
                xr, xi = xs_ref[0, rows, re], xs_ref[0, rows, im]
                xpr = jnp.where(row0, jnp.broadcast_to(xm1r[last:, :], xr.shape), pltpu.roll(xr, 1, 0))
                xpi = jnp.where(row0, jnp.broadcast_to(xm1i[last:, :], xi.shape), pltpu.roll(xi, 1, 0))
                accr = accr + (lr * xpr + li * xpi)
                acci = acci + (li * xpr - lr * xpi)
                return (jnp.broadcast_to(lr[:1, :], lr.shape), jnp.broadcast_to(li[:1, :], li.shape), accr, acci)

            def blk(k, state, re=re, im=im, step=step):
                r = nblk - 1 - k
                rows = pl.ds(pl.multiple_of(r * SUBLANES, SUBLANES), SUBLANES)
                prev = pl.ds(pl.multiple_of((r - 1) * SUBLANES, SUBLANES), SUBLANES)
                return step(rows, xs_ref[0, prev, re], xs_ref[0, prev, im], state)

            zero = jnp.zeros((SUBLANES, cw), F32)
            state = lax.fori_loop(0, nblk - 1, blk, (carry_ref[:, re], carry_ref[:, im], zero, zero))
            cr, ci, accr, acci = step(pl.ds(0, SUBLANES), has_prev * halo_ref[0, :, re], has_prev * halo_ref[0, :, im], state)
            carry_ref[:, re] = cr
            carry_ref[:, im] = ci
            dar_ref[:, re] += accr
            dai_ref[:, re] += acci

        du = _band_contract(lambda cols: lam_ref[0, :, cols].astype(BF16), bbd_ref, c, gp)
        du_ref[0] = (du + skip_ref[0]).astype(BF16)

    tile = pl.BlockSpec((1, tt, w), lambda b, t: (b, nt - 1 - t, 0))
    thin = pl.BlockSpec((1, tt, c), lambda b, t: (b, nt - 1 - t, 0))
    halo = pl.BlockSpec((1, SUBLANES, w), lambda b, t: (b, jnp.maximum((nt - 1 - t) * nblk - 1, 0), 0))
    acc = pl.BlockSpec((SUBLANES, gp), lambda b, t: (0, 0))
    whole = lambda arr: pl.BlockSpec(arr.shape, lambda b, t: (0, 0))
    return _call(
        name, body, (bsz, nt), [whole(tab), thin, tile, halo, thin, whole(bbd), whole(cdt)], [tile, thin, acc, acc],
        [_sds(xs3.shape, F32), _sds((bsz, seq, c), BF16), _sds((SUBLANES, gp), F32), _sds((SUBLANES, gp), F32)],
        (tab, dy3, xs3, xs3, du_skip3, bbd, cdt), ("arbitrary", "arbitrary"),
        scratch=[pltpu.VMEM((SUBLANES, w), F32), pltpu.VMEM((1, tt, w), F32)], exchange=exchange)


def _gelu_parts(y):
    inner = _GELU_K * (y + _GELU_C * y * y * y)
    t = jnp.tanh(inner)
    return 0.5 * y * (1.0 + t), t


def _ssm_out_fwd(name, cx, proj, u_block, d_skip, glu_w, glu_b, out_g):
    n, c = cx.shape
    tm = _pick(n, TILE["row"], 16)

    def body(cx_ref, u_ref, d_ref, gw_ref, gb_ref, og_ref, y_ref, o_ref):
        y = cx_ref[...] + d_ref[...] * u_ref[...]
        y_ref[...] = y
        gy, _ = _gelu_parts(y)
        z = _dot(gy.astype(BF16), gw_ref[...]) + gb_ref[...]
        _, sh = _rms_stats(gy * _sigmoid(z))
        o_ref[...] = (sh * og_ref[...]).astype(BF16)

    vec = pl.BlockSpec((1, c), lambda i: (0, 0))
    row = pl.BlockSpec((tm, c), lambda i: (i, 0))
    return pl.pallas_call(
        body, name=name, grid=(n // tm,),
        in_specs=[row, pl.BlockSpec((tm, c), lambda i: (i, u_block)), vec, pl.BlockSpec(glu_w.shape, lambda i: (0, 0)),
                  vec, vec],
        out_specs=[row, row],
        out_shape=[_sds((n, c), F32), _sds((n, c), BF16)],
        compiler_params=_cparams("parallel"),
    )(cx, proj, d_skip, glu_w, glu_b, out_g)


def _ssm_out_bwd(name, dmix, d_block, y, proj, u_block, d_skip, glu_w, glu_b, out_g):
    n, c = y.shape
    tm = _pick(n, TILE["row"], 16)

    def body(d_ref, y_ref, u_ref, dk_ref, gw_ref, gb_ref, og_ref, dy_ref, du_ref, dgw_ref, dgb_ref, dog_ref, dd_ref):
        @pl.when(pl.program_id(0) == 0)
        def _():
            for r in (dgw_ref, dgb_ref, dog_ref, dd_ref):
                r[...] = jnp.zeros_like(r)

        yv = y_ref[...]
        gy, th = _gelu_parts(yv)
        gy16 = gy.astype(BF16)
        sz = _sigmoid(_dot(gy16, gw_ref[...]) + gb_ref[...])
        r, sh = _rms_stats(gy * sz)
        dout = d_ref[...]
        dog_ref[...] += jnp.sum(dout * sh, axis=0, keepdims=True)
        dsh = dout * og_ref[...]
        ds = r * (dsh - sh * jnp.mean(dsh * sh, axis=-1, keepdims=True))
        dz = ds * gy * sz * (1.0 - sz)
        dz16 = dz.astype(BF16)
        dgb_ref[...] += jnp.sum(dz, axis=0, keepdims=True)
        dgw_ref[...] += _dot_tn(gy16, dz16)
        dgy = ds * sz + _dot_nt(dz16, gw_ref[...])
        dgelu = 0.5 * (1.0 + th) + 0.5 * yv * (1.0 - th * th) * (_GELU_K * (1.0 + 3.0 * _GELU_C * yv * yv))
        dy = dgy * dgelu
        dy_ref[...] = dy.astype(BF16)
        du_ref[...] = dy * dk_ref[...]
        dd_ref[...] += jnp.sum(dy * u_ref[...], axis=0, keepdims=True)

    vec = pl.BlockSpec((1, c), lambda i: (0, 0))
    row = pl.BlockSpec((tm, c), lambda i: (i, 0))
    mat = pl.BlockSpec(glu_w.shape, lambda i: (0, 0))
    return pl.pallas_call(
        body, name=name, grid=(n // tm,),
        in_specs=[pl.BlockSpec((tm, c), lambda i: (i, d_block)), row, pl.BlockSpec((tm, c), lambda i: (i, u_block)),
                  vec, mat, vec, vec],
        out_specs=[row, row, mat, vec, vec, vec],
        out_shape=[_sds((n, c), BF16), _sds((n, c), F32), _sds(glu_w.shape, F32)] + [_sds((1, c), F32)] * 3,
        compiler_params=_cparams("arbitrary"),
    )(dmix, y, proj, d_skip, glu_w, glu_b, out_g)


def _mesh_pos():
    return tuple(lax.axis_index(a) for a in MESH_AXES)


def _other_chips(x, y):
    return [(1 - x, y), (x, 1 - y), (1 - x, 1 - y)]


def _remote(src, dst, send_sem, recv_sem, dev):
    return pltpu.make_async_remote_copy(src_ref=src, dst_ref=dst, send_sem=send_sem, recv_sem=recv_sem,
                                        device_id=dev, device_id_type=pl.DeviceIdType.MESH)


def _hbm_call(name, body, operands, out_shapes, scratch):
    hbm = pl.BlockSpec(memory_space=pltpu.HBM)
    return pl.pallas_call(body, name=name, in_specs=[hbm] * len(operands), out_specs=[hbm] * len(out_shapes),
                          out_shape=out_shapes, scratch_shapes=scratch)(*operands)


_Exchange = collections.namedtuple("_Exchange", "operands out_shapes scratch start finish")


def _run_exchange(name, plan):
    nin, nout = len(plan.operands), len(plan.out_shapes)

    def body(*refs):
        parts = refs[:nin], refs[nin:nin + nout], refs[nin + nout:]
        plan.start(*parts)
        plan.finish(*parts)

    return _hbm_call(name, body, plan.operands, plan.out_shapes, plan.scratch)


def _gather_plan(blocks):
    nop = len(blocks)

    def copies(x_refs, o_refs, sems):
        send_sems, recv_sems, local_sems = sems
        x, y, c = _mesh_pos()
        me, sibling = (x, y, c), (x, y, 1 - c)
        chips = _other_chips(x, y)

        def copy(i, k, block_of, to, src=None):
            dst = o_refs[i].at[4 * block_of[0] + 2 * block_of[1] + block_of[2]]
            return _remote(dst if src is None else src, dst, send_sems.at[i, k], recv_sems.at[i, k], to)

        own = [pltpu.make_async_copy(x_refs[i], o_refs[i].at[4 * x + 2 * y + c], local_sems.at[i]) for i in range(nop)]
        first = []
        for i in range(nop):
            first.append(copy(i, 0, me, sibling, src=x_refs[i]))
            first += [copy(i, 1 + j, me, (*chip, c), src=x_refs[i]) for j, chip in enumerate(chips)]
        return copy, own, first, me, sibling, chips, c

    def start(x_refs, o_refs, sems):
        _, own, first, *_ = copies(x_refs, o_refs, sems)
        for cp in own + first:
            cp.start()

    def finish(x_refs, o_refs, sems):
        copy, own, first, me, sibling, chips, c = copies(x_refs, o_refs, sems)
        passed = []
        for i in range(nop):
            for j, chip in enumerate(chips):
                copy(i, 1 + j, (*chip, c), me).wait_recv()
                passed.append(copy(i, 4 + j, (*chip, c), sibling))
                passed[-1].start()
        for i in range(nop):
            copy(i, 0, sibling, me).wait_recv()
            for j, chip in enumerate(chips):
                copy(i, 4 + j, (*chip, 1 - c), me).wait_recv()
        for cp in first + passed:
            cp.wait_send()
        for cp in own:
            cp.wait()

    return _Exchange(list(blocks), [_sds((N_DEV,) + b.shape, b.dtype) for b in blocks],
                     [pltpu.SemaphoreType.DMA((nop, N_DEV - 1)), pltpu.SemaphoreType.DMA((nop, N_DEV - 1)),
                      pltpu.SemaphoreType.DMA((nop,))], start, finish)


def _core_exchange_plan(grads):
    nop = len(grads)

    def copies(x_refs, o_refs, sems):
        send_sems, recv_sems = sems
        x, y, c = _mesh_pos()
        return [_remote(x_refs[i].at[2 * q + (1 - c)], o_refs[i].at[q], send_sems.at[i, q], recv_sems.at[i, q],
                        (x, y, 1 - c)) for i in range(nop) for q in range(N_DEV // 2)]

    def start(x_refs, o_refs, sems):
        for cp in copies(x_refs, o_refs, sems):
            cp.start()

    def finish(x_refs, o_refs, sems):
        for cp in copies(x_refs, o_refs, sems):
            cp.wait()

    return _Exchange(list(grads), [_sds((N_DEV // 2,) + g.shape[1:], g.dtype) for g in grads],
                     [pltpu.SemaphoreType.DMA((nop, N_DEV // 2)), pltpu.SemaphoreType.DMA((nop, N_DEV // 2))],
                     start, finish)


def _pair_sum(name, grad, other):
    nchip, _, r, c = grad.shape
    tr = _pick(r, max(SUBLANES, TILE["sum_bytes"] // (4 * c)), SUBLANES)
    core = lax.axis_index("c").astype(jnp.int32).reshape(1)

    def body(core_ref, g_ref, o_ref, s_ref):
        s_ref[0] = (g_ref[0, 0] + o_ref[0]).astype(s_ref.dtype)

    tile = pl.BlockSpec((1, tr, c), lambda q, t, core_ref: (q, t, 0))
    return pl.pallas_call(
        body, name=name,
        grid_spec=pltpu.PrefetchScalarGridSpec(
            num_scalar_prefetch=1, grid=(nchip, r // tr),
            in_specs=[pl.BlockSpec((1, 1, tr, c), lambda q, t, core_ref: (q, core_ref[0], t, 0)), tile],
            out_specs=tile),
        out_shape=_sds((nchip, r, c), BF16),
        compiler_params=_cparams("parallel", "parallel"),
    )(core, grad, other)


def _chip_exchange_plan(sums):
    nop = len(sums)

    def copies(x_refs, o_refs, sems, arriving):
        send_sems, recv_sems, local_sems = sems
        x, y, c = _mesh_pos()
        mine = 2 * x + y
        out = []
        for i in range(nop):
            for j, (px, py) in enumerate(_other_chips(x, y)):
                theirs = 2 * px + py
                src, dst = (mine, theirs) if arriving else (theirs, mine)
                out.append(_remote(x_refs[i].at[src], o_refs[i].at[dst], send_sems.at[i, j], recv_sems.at[i, j],
                                   (px, py, c)))
        if not arriving:
            out += [pltpu.make_async_copy(x_refs[i].at[mine], o_refs[i].at[mine], local_sems.at[i]) for i in range(nop)]
        return out

    def start(x_refs, o_refs, sems):
        for cp in copies(x_refs, o_refs, sems, False):
            cp.start()

    def finish(x_refs, o_refs, sems):
        for cp in copies(x_refs, o_refs, sems, True):
            cp.wait_recv()
        mine = copies(x_refs, o_refs, sems, False)
        for cp in mine[:3 * nop]:
            cp.wait_send()
        for cp in mine[3 * nop:]:
            cp.wait()

    return _Exchange(list(sums), [_sds(s.shape, s.dtype) for s in sums],
                     [pltpu.SemaphoreType.DMA((nop, 3)), pltpu.SemaphoreType.DMA((nop, 3)), pltpu.SemaphoreType.DMA((nop,))],
                     start, finish)


def _part_rows(npart, r, c):
    return _pick(r, max(SUBLANES, TILE["sum_bytes"] // (4 * npart * c)), SUBLANES)


def _sum_slots(p_ref):
    g = p_ref[0].astype(F32)
    for k in range(1, p_ref.shape[0]):
        g = g + p_ref[k].astype(F32)
    return g


def _adamw_step(g, w, m, v):
    c1 = 1.0 - ADAM_B1 ** ADAM_STEP
    c2 = 1.0 - ADAM_B2 ** ADAM_STEP
    nm = ADAM_B1 * m + (1.0 - ADAM_B1) * g
    nv = ADAM_B2 * v + (1.0 - ADAM_B2) * (g * g)
    return -ADAM_LR * ((nm / c1) / (jnp.sqrt(nv / c2) + ADAM_EPS) + ADAM_WD * w), nm, nv


def _adamw_small(name, parts, ws, ms, vs):
    nparam, nall = len(ws), len(parts)

    def body(*refs):
        p_refs = refs[:nall]
        w_refs, m_refs, v_refs = (refs[nall + k * nparam:nall + (k + 1) * nparam] for k in range(3))
        outs = refs[nall + 3 * nparam:]
        for p in range(nall):
            g = _sum_slots(p_refs[p])
            if p < nparam:
                delta, nm, nv = _adamw_step(g, w_refs[p][...], m_refs[p][...], v_refs[p][...])
                for o_ref, val in zip(outs[4 * p:4 * p + 4], (g, delta, nm, nv)):
                    o_ref[...] = val
            else:
                outs[4 * nparam + p - nparam][...] = g

    shapes = [_sds(w.shape, F32) for w in ws for _ in range(4)] + [_sds(p.shape[1:], F32) for p in parts[nparam:]]
    res = pl.pallas_call(body, name=name, out_shape=shapes,
                         compiler_params=pltpu.CompilerParams(vmem_limit_bytes=VMEM_LIMIT))(*parts, *ws, *ms, *vs)
    return [res[4 * p:4 * p + 4] for p in range(nparam)] + [[r] for r in res[4 * nparam:]]


def _adamw(name, parts, w, m, v):
    npart, r, c = parts.shape
    lead = len(w.shape) - 2
    tr = _part_rows(npart, r, c)
    at = (0,) * lead + (slice(None), slice(None))

    def body(p_ref, w_ref, m_ref, v_ref, g_ref, d_ref, nm_ref, nv_ref):
        g = _sum_slots(p_ref)
        delta, nm, nv = _adamw_step(g, w_ref[at], m_ref[at], v_ref[at])
        g_ref[at] = g
        nm_ref[at] = nm
        nv_ref[at] = nv
        d_ref[at] = delta

    row = pl.BlockSpec((1,) * lead + (tr, c), lambda i: (0,) * lead + (i, 0))
    return pl.pallas_call(
        body, name=name, grid=(r // tr,),
        in_specs=[pl.BlockSpec((npart, tr, c), lambda i: (0, i, 0)), row, row, row],
        out_specs=[row] * 4,
        out_shape=[_sds(w.shape, F32)] * 4,
        compiler_params=_cparams("parallel"),
    )(parts, w, m, v)


def _block_diag(rows_gh, groups):
    gh, p = rows_gh.shape
    own = (jnp.arange(gh)[:, None] // (gh // groups) == jnp.arange(groups)[None, :]).astype(rows_gh.dtype)
    return (own[:, :, None] * rows_gh[:, None, :]).reshape(gh, groups * p)


def _block_diag_take(dense, groups):
    gh = dense.shape[0]
    p = dense.shape[1] // groups
    own = (jnp.arange(gh)[:, None] // (gh // groups) == jnp.arange(groups)[None, :]).astype(dense.dtype)
    return jnp.sum(dense.reshape(gh, groups, p) * own[:, :, None], axis=1)


FFN1 = ("ffn1_w1", "ffn1_w3", "ffn1_w2")
MIXER = ("w_in", "ssm_glu_w", "w_out")
FFN2 = ("ffn2_w1", "ffn2_w3", "ffn2_w2")
BIG = FFN1 + MIXER + FFN2
COL_SHARDED = ("ffn1_w1", "ffn1_w3", "w_in", "ffn2_w1", "ffn2_w3", "conv_w")
SMALL = ("norm_ffn1", "norm_mix", "conv_b", "conv_ln_g", "conv_ln_b", "conv_out_g", "ssm_A_re", "ssm_A_im",
         "ssm_log_dt", "ssm_B_re", "ssm_B_im", "ssm_C_re", "ssm_C_im", "ssm_D", "ssm_glu_b", "ssm_out_g",
         "norm_ffn2", "norm_final")
WEIGHTS = ("norm_ffn1", "ffn1_w1", "ffn1_w3", "ffn1_w2", "norm_mix", "w_in", "conv_w", "conv_b", "conv_ln_g",
           "conv_ln_b", "conv_out_g", "ssm_A_re", "ssm_A_im", "ssm_log_dt", "ssm_B_re", "ssm_B_im", "ssm_C_re",
           "ssm_C_im", "ssm_D", "ssm_glu_w", "ssm_glu_b", "ssm_out_g", "w_out", "norm_ffn2", "ffn2_w1", "ffn2_w3",
           "ffn2_w2", "norm_final")


def _ffn_backward(tag, dxo, x, g, w1, w3, w2, saved, exchange=None, dw_exchange=None, reduce_plan=None):
    a, b, h = saved
    (da, db, hid, dxh), got = _ffn_bwd_hidden(tag + "_bwd_hidden", dxo, a, b, w2, exchange=exchange)
    dw1, dw_got = _mm_tn(tag + "_dw1", da, h, exchange=dw_exchange) if dw_exchange else (_mm_tn(tag + "_dw1", da, h), None)
    dws = [dw1, _mm_tn(tag + "_dw3", db, h), _mm_tn(tag + "_dw2", hid, dxh)]
    f = a.shape[1]
    (dx, dg), reduced = _dx_rms_bwd(tag + "_bwd_dx", [(da, f, 0, w1, f, 0), (db, f, 0, w3, f, 0)], dxo, x, g,
                                    exchange=reduce_plan(dws) if reduce_plan else None)
    return (dx, dg, dws), got, dw_got, reduced


def _reduce_in_chip(names, grads):
    send = [g.reshape((N_DEV, -1) + g.shape[1:]) for g in grads]

    def then(from_core):
        return _chip_exchange_plan([_pair_sum("pair_sum_" + k, s.reshape((N_DEV // 2, 2) + s.shape[1:]), o)
                                    for k, s, o in zip(names, send, from_core)])

    return _core_exchange_plan(send), then


def kernel(x, norm_ffn1, ffn1_w1, ffn1_w3, ffn1_w2, norm_mix, w_in, conv_w, conv_b, conv_ln_g, conv_ln_b, conv_out_g, ssm_A_re, ssm_A_im, ssm_log_dt, ssm_B_re, ssm_B_im, ssm_C_re, ssm_C_im, ssm_D, ssm_glu_w, ssm_glu_b, ssm_out_g, w_out, norm_ffn2, ffn2_w1, ffn2_w3, ffn2_w2, norm_final, loss_target, m_norm_ffn1, m_ffn1_w1, m_ffn1_w3, m_ffn1_w2, m_norm_mix, m_w_in, m_conv_w, m_conv_b, m_conv_ln_g, m_conv_ln_b, m_conv_out_g, m_ssm_A_re, m_ssm_A_im, m_ssm_log_dt, m_ssm_B_re, m_ssm_B_im, m_ssm_C_re, m_ssm_C_im, m_ssm_D, m_ssm_glu_w, m_ssm_glu_b, m_ssm_out_g, m_w_out, m_norm_ffn2, m_ffn2_w1, m_ffn2_w3, m_ffn2_w2, m_norm_final, v_norm_ffn1, v_ffn1_w1, v_ffn1_w3, v_ffn1_w2, v_norm_mix, v_w_in, v_conv_w, v_conv_b, v_conv_ln_g, v_conv_ln_b, v_conv_out_g, v_ssm_A_re, v_ssm_A_im, v_ssm_log_dt, v_ssm_B_re, v_ssm_B_im, v_ssm_C_re, v_ssm_C_im, v_ssm_D, v_ssm_glu_w, v_ssm_glu_b, v_ssm_out_g, v_w_out, v_norm_ffn2, v_ffn2_w1, v_ffn2_w3, v_ffn2_w2, v_norm_final):
    args = dict(locals())
    wt = {n: args[n] for n in WEIGHTS}
    mom = {n: args["m_" + n] for n in WEIGHTS}
    var = {n: args["v_" + n] for n in WEIGHTS}

    bsz, seq, d = x.shape
    n = bsz * seq
    c = conv_b.shape[-1]
    groups = c // SSM_GROUP
    gp = groups * SSM_STATE
    u_b = 2

    shard = {k: (wt[k][0].T if k in COL_SHARDED else wt[k][0]).astype(BF16) for k in BIG}
    gathered = _run_exchange("gather_weights_ffn1", _gather_plan([shard[k] for k in FFN1]))
    full = {k: g.reshape(-1, g.shape[-1]) for k, g in zip(FFN1, gathered)}
    gather_rest = _gather_plan([shard[k] for k in MIXER + FFN2] + [wt["conv_w"][0]])

    vec = lambda k: wt[k].reshape(1, -1)
    g_ffn1, g_mix, g_ffn2, g_fin = vec("norm_ffn1"), vec("norm_mix"), vec("norm_ffn2"), vec("norm_final")
    cb, lng, lnb, cog = vec("conv_b"), vec("conv_ln_g"), vec("conv_ln_b"), vec("conv_out_g")
    d_skip, glu_b, sog = vec("ssm_D"), vec("ssm_glu_b"), vec("ssm_out_g")

    a_re, a_im = wt["ssm_A_re"][0], wt["ssm_A_im"][0]
    log_dt = wt["ssm_log_dt"][0].reshape(groups, 1)
    bt_re = wt["ssm_B_re"][0].transpose(0, 2, 1).reshape(groups * SSM_GROUP, SSM_STATE)
    bt_im = wt["ssm_B_im"][0].transpose(0, 2, 1).reshape(groups * SSM_GROUP, SSM_STATE)
    c_re = wt["ssm_C_re"][0].reshape(groups * SSM_GROUP, SSM_STATE)
    c_im = wt["ssm_C_im"][0].reshape(groups * SSM_GROUP, SSM_STATE)
    per_chan = lambda t: jnp.repeat(t, SSM_GROUP, axis=0)
    ssm_prim = (a_re, a_im, log_dt, per_chan(a_re), per_chan(a_im), per_chan(jnp.broadcast_to(log_dt, a_re.shape)),
                bt_re, bt_im)
    pw_r, pw_i, bb_r, bb_i = _ssm_prep("ssm_prep", ssm_prim)
    tab_f = _scan_tables(pw_r, pw_i, False)
    tab_b = _scan_tables(pw_r, pw_i, True)
    bbd = jnp.concatenate([_block_diag(bb_r, groups), _block_diag(bb_i, groups)], axis=1).astype(BF16)
    cdt = jnp.concatenate([_block_diag(c_re, groups), -_block_diag(c_im, groups)], axis=1).astype(BF16)

    x0 = x.reshape(n, d)
    (x1, *ffn1_saved), gathered = _ffn_fwd("ffn1_fwd", x0, g_ffn1, full["ffn1_w1"], full["ffn1_w3"], full["ffn1_w2"],
                                           exchange=gather_rest)
    full.update({k: g.reshape(-1, g.shape[-1]) for k, g in zip(MIXER + FFN2, gathered)})
    conv_w_full = gathered[-1].transpose(1, 0, 2).reshape(CONV_WIDTH, c)
    conv_w_pad = jnp.pad(conv_w_full, ((0, CONV_HALO - CONV_WIDTH), (0, 0)))
    (proj,), h2 = _rms_mm("mix_in", x1, g_mix, [full["w_in"]], F32)
    proj3 = proj.reshape(bsz, seq, 3 * c)
    an3, cv3 = _conv_fwd("conv_fwd", proj3, conv_w_pad, cb, lng, lnb, cog)
    an = an3.reshape(n, c)
    xs3, cx3 = _scan_fwd("scan_fwd", tab_f, proj3, u_b, bbd, cdt)
    xs = xs3.reshape(n, 2 * gp)
    y, sn = _ssm_out_fwd("ssm_out_fwd", cx3.reshape(n, c), proj, u_b, d_skip, full["ssm_glu_w"], glu_b, sog)
    w_o = full["w_out"]
    x2, _ = _row_mm("mix_out", [(an, c, 0, w_o, c, 0, False), (sn, c, 0, w_o, c, 1, False)], d, F32, add=x1)
    (dx3, *ffn2_saved, loss_tile, d_gfin), _ = _ffn_fwd(
        "ffn2_fwd", x2, g_ffn2, full["ffn2_w1"], full["ffn2_w3"], full["ffn2_w2"],
        head=(g_fin, loss_target.reshape(n, d)))
    loss = lax.psum(loss_tile[0, 0], MESH_AXES)

    grads, from_chips = {}, {}
    (dx2, grads["norm_ffn2"], dws), _, _, _ = _ffn_backward(
        "ffn2", dx3, x2, g_ffn2, full["ffn2_w1"], full["ffn2_w3"], full["ffn2_w2"], ffn2_saved)
    in_chip, across_chips = _reduce_in_chip(FFN2, dws)

    dmix, got = _row_mm("mix_out_bwd", [(dx2, d, 0, w_o, 2 * c, 0, True)], 2 * c, F32, exchange=in_chip)
    reduce_ffn2 = across_chips(got)
    grads["w_out"] = jnp.concatenate([_mm_tn("dw_out_a", an, dx2), _mm_tn("dw_out_s", sn, dx2)], axis=0)

    dy, du_skip, grads["ssm_glu_w"], grads["ssm_glu_b"], grads["ssm_out_g"], grads["ssm_D"] = _ssm_out_bwd(
        "ssm_out_bwd", dmix, 1, y, proj, u_b, d_skip, full["ssm_glu_w"], glu_b, sog)
    (lam3, du3, dab_r, dab_i), got = _scan_bwd("scan_bwd", tab_b, dy.reshape(bsz, seq, c), xs3,
                                               du_skip.reshape(bsz, seq, c), bbd, cdt, exchange=reduce_ffn2)
    from_chips.update(zip(FFN2, got))
    lam, du = lam3.reshape(n, 2 * gp), du3.reshape(n, c)
    d_bbd = _band_wgrad("ssm_dbb", proj, u_b, c, lam)
    d_cdt = _band_wgrad("ssm_dc", dy, 0, c, xs)
    d_are, d_aim, d_ldt, d_btr, d_bti = _ssm_param_grads(
        "ssm_param_grads", ssm_prim,
        dab_r.reshape(SUBLANES, groups, SSM_STATE), dab_i.reshape(SUBLANES, groups, SSM_STATE),
        _band_diag_take(d_bbd, 0, c, gp), _band_diag_take(d_bbd, 1, c, gp))
    grads["ssm_A_re"], grads["ssm_A_im"], grads["ssm_log_dt"] = d_are, d_aim, d_ldt
    grads["ssm_B_re"], grads["ssm_B_im"] = d_btr, d_bti
    grads["ssm_C_re"] = _band_diag_take(d_cdt, 0, c, gp)
    grads["ssm_C_im"] = -_band_diag_take(d_cdt, 1, c, gp)

    dconv3, d_cw, grads["conv_b"], grads["conv_ln_g"], grads["conv_ln_b"], grads["conv_out_g"] = _conv_bwd(
        "conv_bwd", dmix.reshape(bsz, seq, 2 * c), proj3, cv3, conv_w_pad, lng, lnb, cog)
    dconv = dconv3.reshape(n, 2 * c)
    grads["conv_w"] = d_cw[:CONV_WIDTH]
    grads["w_in"] = jnp.concatenate([_mm_tn("dw_in_conv", dconv, h2), _mm_tn("dw_in_ssm", du, h2)], axis=0)
    w_i = full["w_in"]
    in_chip, across_chips = _reduce_in_chip(MIXER, [grads[k] for k in MIXER])
    (dx1, grads["norm_mix"]), got = _dx_rms_bwd("mix_in_bwd", [(dconv, 2 * c, 0, w_i, 2 * c, 0), (du, c, 0, w_i, c, 2)],
                                                dx2, x1, g_mix, exchange=in_chip)
    reduce_mixer = across_chips(got)

    def reduce_ffn1(dws):
        in_chip, across_chips = _reduce_in_chip(FFN1, dws)
        return across_chips(_run_exchange("exchange_core_ffn1", in_chip))

    grads["norm_final"] = d_gfin
    early = tuple(k for k in SMALL if k != "norm_ffn1")
    gather_small = _gather_plan([grads[k] for k in early] + [grads["conv_w"]])

    (dx0, grads["norm_ffn1"], _), got, small_parts, reduced = _ffn_backward(
        "ffn1", dx1, x0, g_ffn1, full["ffn1_w1"], full["ffn1_w3"], full["ffn1_w2"], ffn1_saved,
        exchange=reduce_mixer, dw_exchange=gather_small, reduce_plan=reduce_ffn1)
    from_chips.update(zip(MIXER, got))
    from_chips.update(zip(FFN1, reduced))

    res = {}
    for k in BIG:
        parts = from_chips[k]
        if k in COL_SHARDED:
            swap = lambda t: jnp.swapaxes(t, -1, -2)
            res[k] = [swap(t) for t in _adamw("adamw_" + k, parts, swap(wt[k]), swap(mom[k]), swap(var[k]))]
        else:
            res[k] = _adamw("adamw_" + k, parts, wt[k], mom[k], var[k])

    def as_2d(k, t):
        if k in ("ssm_B_re", "ssm_B_im"):
            return t[0].transpose(0, 2, 1).reshape(-1, SSM_STATE)
        if k in ("ssm_C_re", "ssm_C_im"):
            return t[0].reshape(-1, SSM_STATE)
        if k in ("ssm_A_re", "ssm_A_im"):
            return t[0]
        return t.reshape(-1, 1) if k == "ssm_log_dt" else t.reshape(1, -1)

    def as_param(k, t):
        if k in ("ssm_B_re", "ssm_B_im"):
            t = t.reshape(groups, SSM_GROUP, SSM_STATE).transpose(0, 2, 1)
        return t.reshape(wt[k].shape)

    (last_part,) = _run_exchange("gather_norm_ffn1_grad", _gather_plan([grads["norm_ffn1"]]))
    order = ("norm_ffn1",) + early
    updated = _adamw_small("adamw_replicated", [last_part] + small_parts,
                           *[[as_2d(k, src[k]) for k in order] for src in (wt, mom, var)])
    res.update({k: [as_param(k, t) for t in upd] for k, upd in zip(order, updated)})
    (conv_w_grad,) = updated[-1]
    x_pos, y_pos, c_pos = (lax.axis_index(a) for a in MESH_AXES)
    cw_cols = c // N_DEV
    own_cw = lax.dynamic_slice_in_dim(conv_w_grad, (4 * x_pos + 2 * y_pos + c_pos) * cw_cols, cw_cols, axis=1)
    res["conv_w"] = _adamw("adamw_conv_w", own_cw[None], wt["conv_w"], mom["conv_w"], var["conv_w"])

    outs = [loss, dx0.reshape(bsz, seq, d)]
    for kind in range(4):
        outs += [res[k][kind] for k in WEIGHTS]
    return tuple(outs)
```

```python
import collections
import functools
import math

import jax
import jax.numpy as jnp
from jax import lax
from jax.experimental import pallas as pl
from jax.experimental.pallas import tpu as pltpu

F32 = jnp.float32
BF16 = jnp.bfloat16

EPS = 1e-6
FFN_RES = 0.5
CONV_WIDTH = 31
CONV_HALO = 32
SSM_GROUP = 16
SSM_STATE = 64
ADAM_LR, ADAM_B1, ADAM_B2, ADAM_EPS, ADAM_WD, ADAM_STEP = 0.001, 0.9, 0.999, 1e-08, 0.01, 10

N_DEV = 8
MESH_AXES = ("x", "y", "c")
SUBLANES = 8
LANES = 128
V7X_VMEM_BYTES = 64 * 2**20
VMEM_LIMIT = V7X_VMEM_BYTES - 8 * 2**20

TILE = dict(row=256, ffn_m=512, mm_bytes=8 * 2**20, up_m=1024, up_n=256, wide_n=2048, conv_t=512, scan_t=256, scan_w=512,
            sum_bytes=4 * 2**20)

_GELU_K = math.sqrt(2.0 / math.pi)
_GELU_C = 0.044715


def _pick(n, target, mult):
    best = None
    for t in range(mult, min(n, target) + 1, mult):
        if n % t == 0:
            best = t
    return n if best is None else best


def _cparams(*sem):
    return pltpu.CompilerParams(dimension_semantics=sem, vmem_limit_bytes=VMEM_LIMIT)


def _sds(shape, dtype):
    return jax.ShapeDtypeStruct(shape, dtype)


def _call(name, body, grid, in_specs, out_specs, out_shape, operands, sem, scratch=(), exchange=None):
    if exchange is None:
        res = pl.pallas_call(body, name=name, grid=grid, in_specs=list(in_specs), out_specs=list(out_specs),
                             out_shape=list(out_shape), scratch_shapes=list(scratch),
                             compiler_params=_cparams(*sem))(*operands)
        return list(res), None
    n_in, n_out, n_scr = len(in_specs), len(out_specs), len(scratch)
    n_xin, n_xout = len(exchange.operands), len(exchange.out_shapes)
    hbm = pl.BlockSpec(memory_space=pltpu.HBM)

    def with_exchange(*refs):
        cuts, pos = [], 0
        for size in (n_in, n_xin, n_out, n_xout, n_scr):
            cuts.append(refs[pos:pos + size])
            pos += size
        ins, x_in, outs, x_out, scr = cuts
        sems = refs[pos:]
        ids = [pl.program_id(axis) for axis in range(len(grid))]
        first = functools.reduce(lambda p, q: p & q, [i == 0 for i in ids])
        last = functools.reduce(lambda p, q: p & q, [i == g - 1 for i, g in zip(ids, grid)])

        @pl.when(first)
        def _():
            exchange.start(x_in, x_out, sems)

        body(*ins, *outs, *scr)

        @pl.when(last)
        def _():
            exchange.finish(x_in, x_out, sems)

    res = pl.pallas_call(
        with_exchange, name=name, grid=grid, in_specs=list(in_specs) + [hbm] * n_xin,
        out_specs=list(out_specs) + [hbm] * n_xout, out_shape=list(out_shape) + list(exchange.out_shapes),
        scratch_shapes=list(scratch) + list(exchange.scratch),
        compiler_params=_cparams(*["arbitrary"] * len(grid)))(*operands, *exchange.operands)
    return list(res[:n_out]), list(res[n_out:])


def _dot(a, b):
    return jnp.dot(a, b, preferred_element_type=F32)


def _dot_nt(a, b):
    return lax.dot_general(a, b, (((1,), (1,)), ((), ())), preferred_element_type=F32)


def _dot_tn(a, b):
    return lax.dot_general(a, b, (((0,), (0,)), ((), ())), preferred_element_type=F32)


def _sigmoid(x):
    return 0.5 * jnp.tanh(0.5 * x) + 0.5


def _rms_stats(x):
    r = lax.rsqrt(jnp.mean(x * x, axis=-1, keepdims=True) + EPS)
    return r, x * r


def _rms_bwd(x, g, dy):
    r, xh = _rms_stats(x)
    dxh = dy * g
    dx = r * (dxh - xh * jnp.mean(dxh * xh, axis=-1, keepdims=True))
    return dx, jnp.sum(dy * xh, axis=0, keepdims=True)


def _rms_mm(name, x, g, ws, out_dtype):
    n, d = x.shape
    f = ws[0].shape[0]
    nw = len(ws)
    tm, tn = _pick(n, TILE["up_m"], 16), _pick(f, TILE["wide_n"], LANES)

    def body(x_ref, g_ref, *refs):
        w_refs, o_refs, h_ref = refs[:nw], refs[nw:2 * nw], refs[2 * nw]

        @pl.when(pl.program_id(1) == 0)
        def _():
            _, xh = _rms_stats(x_ref[...])
            h_ref[...] = (xh * g_ref[...]).astype(BF16)

        h = h_ref[...]
        for w_ref, o_ref in zip(w_refs, o_refs):
            o_ref[...] = _dot_nt(h, w_ref[...]).astype(o_ref.dtype)

    outs = pl.pallas_call(
        body, name=name, grid=(n // tm, f // tn),
        in_specs=[pl.BlockSpec((tm, d), lambda i, j: (i, 0)), pl.BlockSpec((1, d), lambda i, j: (0, 0))]
        + [pl.BlockSpec((tn, d), lambda i, j: (j, 0))] * nw,
        out_specs=[pl.BlockSpec((tm, tn), lambda i, j: (i, j))] * nw + [pl.BlockSpec((tm, d), lambda i, j: (i, 0))],
        out_shape=[_sds((n, f), out_dtype)] * nw + [_sds((n, d), BF16)],
        compiler_params=_cparams("parallel", "arbitrary"),
    )(x, g, *ws)
    return outs[:nw], outs[nw]


def _ffn_fwd(name, x, g, w1t, w3t, w2, exchange=None, head=None):
    n, d = x.shape
    f = w2.shape[0]
    tm, tn = _pick(n, TILE["ffn_m"], 16), _pick(f, TILE["up_n"], LANES)

    def body(x_ref, g_ref, w1_ref, w3_ref, w2_ref, *refs):
        (gf_ref, t_ref), refs = (refs[:2], refs[2:]) if head else ((None, None), refs)
        o_ref, a_ref, b_ref, h_ref = refs[:4]
        xv = x_ref[...]
        _, xh = _rms_stats(xv)
        h = (xh * g_ref[...]).astype(BF16)
        h_ref[...] = h
        acc = None
        for c0 in range(0, f, tn):
            cols = pl.ds(c0, tn)
            av, bv = _dot_nt(h, w1_ref[cols, :]), _dot_nt(h, w3_ref[cols, :])
            a_ref[:, cols] = av.astype(BF16)
            b_ref[:, cols] = bv.astype(BF16)
            t = _dot((av * _sigmoid(av) * bv).astype(BF16), w2_ref[cols, :])
            acc = t if acc is None else acc + t
        out = xv + FFN_RES * acc
        if head is None:
            o_ref[...] = out
        else:
            loss_ref, dg_ref = refs[4:]

            @pl.when(pl.program_id(0) == 0)
            def _():
                loss_ref[...] = jnp.zeros_like(loss_ref)
                dg_ref[...] = jnp.zeros_like(dg_ref)

            dx, loss, dg = _loss_head_rows(out, gf_ref[...], t_ref[...])
            o_ref[...] = dx
            loss_ref[...] += loss
            dg_ref[...] += dg

    row = pl.BlockSpec((tm, d), lambda i: (i, 0))
    wide = pl.BlockSpec((tm, f), lambda i: (i, 0))
    vec = pl.BlockSpec((1, d), lambda i: (0, 0))
    held = pl.BlockSpec((f, d), lambda i: (0, 0), pipeline_mode=pl.Buffered(1))
    extra_in, extra_out, extra_shape = ([vec, row], [pl.BlockSpec((SUBLANES, LANES), lambda i: (0, 0)), vec],
                                        [_sds((SUBLANES, LANES), F32), _sds((1, d), F32)]) if head else ([], [], [])
    return _call(
        name, body, (n // tm,), [row, vec, held, held, held] + extra_in, [row, wide, wide, row] + extra_out,
        [_sds((n, d), F32), _sds((n, f), BF16), _sds((n, f), BF16), _sds((n, d), BF16)] + extra_shape,
        (x, g, w1t, w3t, w2) + (tuple(head) if head else ()), ("arbitrary",) if head else ("parallel",),
        exchange=exchange)


def _ffn_bwd_hidden(name, dxo, a, b, w2, exchange=None):
    n, d = dxo.shape
    f = a.shape[1]
    tm, tn = _pick(n, TILE["row"], 16), _pick(f, TILE["up_n"], LANES)

    def body(dx_ref, a_ref, b_ref, w_ref, da_ref, db_ref, hid_ref, dxh_ref):
        dxh = (FFN_RES * dx_ref[...]).astype(BF16)
        dxh_ref[...] = dxh
        for c0 in range(0, f, tn):
            cols = pl.ds(c0, tn)
            dhid = _dot_nt(dxh, w_ref[cols, :])
            av, bv = a_ref[:, cols].astype(F32), b_ref[:, cols].astype(F32)
            sig = _sigmoid(av)
            silu = av * sig
            da_ref[:, cols] = (dhid * bv * (sig * (1.0 + av * (1.0 - sig)))).astype(BF16)
            db_ref[:, cols] = (dhid * silu).astype(BF16)
            hid_ref[:, cols] = (silu * bv).astype(BF16)

    wide = pl.BlockSpec((tm, f), lambda i: (i, 0))
    row = pl.BlockSpec((tm, d), lambda i: (i, 0))
    return _call(
        name, body, (n // tm,), [row, wide, wide, pl.BlockSpec((f, d), lambda i: (0, 0))], [wide, wide, wide, row],
        [_sds((n, f), BF16)] * 3 + [_sds((n, d), BF16)], (dxo, a, b, w2), ("parallel",), exchange=exchange)


def _loss_head_rows(x, g, target):
    r, xh = _rms_stats(x)
    err = xh * g - target
    dy = err * (1.0 / x.shape[-1])
    dxh = dy * g
    dx = r * (dxh - xh * jnp.mean(dxh * xh, axis=-1, keepdims=True))
    return dx, 0.5 * jnp.sum(jnp.mean(err * err, axis=-1, keepdims=True)), jnp.sum(dy * xh, axis=0, keepdims=True)


def _dx_rms_bwd(name, pairs, dxo, x, g, exchange=None):
    n, dm = x.shape
    tm = _pick(n, TILE["ffn_m"], 16)
    npair = len(pairs)

    def body(*refs):
        d_refs, w_refs = refs[:npair], refs[npair:2 * npair]
        dxo_ref, x_ref, g_ref, dx_ref, dg_ref = refs[2 * npair:]

        @pl.when(pl.program_id(0) == 0)
        def _():
            dg_ref[...] = jnp.zeros_like(dg_ref)

        dh = None
        for d_ref, w_ref in zip(d_refs, w_refs):
            t = _dot(d_ref[...].astype(BF16), w_ref[...])
            dh = t if dh is None else dh + t
        dx, dg = _rms_bwd(x_ref[...], g_ref[...], dh)
        dx_ref[...] = dxo_ref[...] + dx
        dg_ref[...] += dg

    row = pl.BlockSpec((tm, dm), lambda i: (i, 0))
    d_specs = [pl.BlockSpec((tm, p[1]), functools.partial(lambda i, cb: (i, cb), cb=p[2])) for p in pairs]
    w_specs = [pl.BlockSpec((p[4], dm), functools.partial(lambda i, rb: (rb, 0), rb=p[5]), pipeline_mode=pl.Buffered(1))
               for p in pairs]
    return _call(
        name, body, (n // tm,), d_specs + w_specs + [row, row, pl.BlockSpec((1, dm), lambda i: (0, 0))],
        [row, pl.BlockSpec((1, dm), lambda i: (0, 0))], [_sds((n, dm), F32), _sds((1, dm), F32)],
        (*[p[0] for p in pairs], *[p[3] for p in pairs], dxo, x, g), ("arbitrary",), exchange=exchange)


def _mm_tn(name, a, b, a_cols=None, b_cols=None, exchange=None):
    n = a.shape[0]
    a0, ma = a_cols if a_cols else (0, a.shape[1])
    b0, mb = b_cols if b_cols else (0, b.shape[1])
    assert a0 % ma == 0 and b0 % mb == 0
    ab, bb = a0 // ma, b0 // mb
    tk = _pick(n, TILE["mm_bytes"] // (ma * a.dtype.itemsize + mb * b.dtype.itemsize), 16)

    def body(a_ref, b_ref, o_ref):
        @pl.when(pl.program_id(0) == 0)
        def _():
            o_ref[...] = jnp.zeros_like(o_ref)

        o_ref[...] += _dot_tn(a_ref[...].astype(BF16), b_ref[...].astype(BF16))

    (out,), got = _call(
        name, body, (n // tk,), [pl.BlockSpec((tk, ma), lambda k: (k, ab)), pl.BlockSpec((tk, mb), lambda k: (k, bb))],
        [pl.BlockSpec((ma, mb), lambda k: (0, 0))], [_sds((ma, mb), F32)], (a, b), ("arbitrary",), exchange=exchange)
    return out if exchange is None else (out, got)


def _row_mm(name, pairs, out_w, out_dtype, add=None, exchange=None):
    n = pairs[0][0].shape[0]
    tm = _pick(n, TILE["row"], 16)
    npair = len(pairs)

    def body(*refs):
        a_refs, w_refs = refs[:npair], refs[npair:2 * npair]
        add_ref = refs[2 * npair] if add is not None else None
        o_ref = refs[-1]
        acc = None
        for a_ref, w_ref, p in zip(a_refs, w_refs, pairs):
            av = a_ref[...].astype(BF16)
            t = _dot_nt(av, w_ref[...]) if p[6] else _dot(av, w_ref[...])
            acc = t if acc is None else acc + t
        if add_ref is not None:
            acc = acc + add_ref[...].astype(F32)
        o_ref[...] = acc.astype(o_ref.dtype)

    a_specs = [pl.BlockSpec((tm, p[1]), functools.partial(lambda i, cb: (i, cb), cb=p[2])) for p in pairs]
    w_specs = [pl.BlockSpec((p[4], p[3].shape[1]), functools.partial(lambda i, rb: (rb, 0), rb=p[5])) for p in pairs]
    add_specs = [pl.BlockSpec((tm, out_w), lambda i: (i, 0))] if add is not None else []
    (out,), got = _call(
        name, body, (n // tm,), a_specs + w_specs + add_specs, [pl.BlockSpec((tm, out_w), lambda i: (i, 0))],
        [_sds((n, out_w), out_dtype)],
        (*[p[0] for p in pairs], *[p[3] for p in pairs], *([add] if add is not None else [])), ("parallel",),
        exchange=exchange)
    return out, got


def _conv_post(c, ln_g, ln_b, out_g):
    mu = jnp.mean(c, axis=-1, keepdims=True)
    xc = c - mu
    rstd = lax.rsqrt(jnp.mean(xc * xc, axis=-1, keepdims=True) + EPS)
    nrm = xc * rstd
    l = nrm * ln_g + ln_b
    sig = _sigmoid(l)
    s = l * sig
    r, sh = _rms_stats(s)
    return sh * out_g, (rstd, nrm, l, sig, r, sh)


def _tap_groups(first):
    groups = []
    for r in range(SUBLANES):
        taps = [(s - r, s - first) for s in range(first, first + CONV_WIDTH) if s % SUBLANES == r]
        if taps:
            groups.append((r, taps))
    return groups


def _shift_rows(lo, hi, s):
    row = lax.broadcasted_iota(jnp.int32, lo.shape, 0)
    return pltpu.roll(jnp.where(row >= s, lo, hi), SUBLANES - s, 0)


def _conv_taps(a_ref, w_ref, b_ref, put, first, rows, flip=False, bias=None):
    groups = _tap_groups(first)
    c = a_ref.shape[-1]
    for slot, (r, taps) in enumerate(groups):
        weights = [jnp.broadcast_to(w_ref[kk:kk + 1, :], (SUBLANES, c))
                   for kk in ((CONV_WIDTH - 1 - k if flip else k) for _, k in taps)]

        def blk(i, carry, slot=slot, taps=taps, weights=weights):
            off = pl.multiple_of(i * SUBLANES, SUBLANES)
            acc = None
            for (base, _), wv in zip(taps, weights):
                t = wv * a_ref[pl.ds(off + base, SUBLANES), :]
                acc = t if acc is None else acc + t
            b_ref[slot, pl.ds(off, SUBLANES), :] = acc
            return carry

        lax.fori_loop(0, (rows if r == 0 else rows + SUBLANES) // SUBLANES, blk, 0)

    def combine(i, carry):
        off = pl.multiple_of(i * SUBLANES, SUBLANES)
        out = bias
        for slot, (r, _) in enumerate(groups):
            part = b_ref[slot, pl.ds(off, SUBLANES), :]
            if r:
                part = _shift_rows(part, b_ref[slot, pl.ds(off + SUBLANES, SUBLANES), :], r)
            out = part if out is None else out + part
        put(off, out)
        return carry

    lax.fori_loop(0, rows // SUBLANES, combine, 0)


def _conv_post_bwd(cv, dout, ln_g, ln_b, out_g):
    _, (rstd, nrm, l, sig, r, sh) = _conv_post(cv, ln_g, ln_b, out_g)
    dsh = dout * out_g
    ds = r * (dsh - sh * jnp.mean(dsh * sh, axis=-1, keepdims=True))
    dl = ds * (sig * (1.0 + l * (1.0 - sig)))
    dn = dl * ln_g
    dc = rstd * (dn - jnp.mean(dn, axis=-1, keepdims=True) - nrm * jnp.mean(dn * nrm, axis=-1, keepdims=True))
    col_sum = lambda t: jnp.sum(t, axis=0, keepdims=True)
    return dc, col_sum(dout * sh), col_sum(dl * nrm), col_sum(dl)


def _conv_fwd(name, proj3, conv_w, conv_b, ln_g, ln_b, out_g):
    bsz, seq, _ = proj3.shape
    c = conv_w.shape[1]
    tt = _pick(seq, TILE["conv_t"], CONV_HALO)
    hb = tt // CONV_HALO
    first = CONV_HALO - (CONV_WIDTH - 1)

    def body(v_ref, g_ref, vp_ref, gp_ref, w_ref, cb_ref, lg_ref, lb_ref, og_ref, o_ref, cv_ref, a_ref, b_ref):
        keep = (pl.program_id(1) > 0).astype(F32)
        a_ref[pl.ds(0, CONV_HALO), :] = keep * vp_ref[0] * _sigmoid(gp_ref[0])
        a_ref[pl.ds(CONV_HALO, tt), :] = v_ref[0] * _sigmoid(g_ref[0])
        def put_cv(off, rows8):
            cv_ref[0, pl.ds(off, SUBLANES), :] = rows8

        _conv_taps(a_ref, w_ref, b_ref, put_cv, first, tt, bias=cb_ref[...])
        out, _ = _conv_post(cv_ref[0], lg_ref[...], lb_ref[...], og_ref[...])
        o_ref[0] = out.astype(BF16)

    vec = pl.BlockSpec((1, c), lambda b, i: (0, 0))
    prev = lambda col: pl.BlockSpec((1, CONV_HALO, c), lambda b, i: (b, jnp.maximum(i * hb - 1, 0), col))
    tile = pl.BlockSpec((1, tt, c), lambda b, i: (b, i, 0))
    return pl.pallas_call(
        body, name=name, grid=(bsz, seq // tt),
        in_specs=[tile, pl.BlockSpec((1, tt, c), lambda b, i: (b, i, 1)),
                  prev(0), prev(1), pl.BlockSpec(conv_w.shape, lambda b, i: (0, 0)), vec, vec, vec, vec],
        out_specs=[tile, tile],
        out_shape=[_sds((bsz, seq, c), BF16), _sds((bsz, seq, c), F32)],
        scratch_shapes=[pltpu.VMEM((CONV_HALO + tt, c), F32), pltpu.VMEM((SUBLANES, tt + SUBLANES, c), F32)],
        compiler_params=_cparams("parallel", "arbitrary"),
    )(proj3, proj3, proj3, proj3, conv_w, conv_b, ln_g, ln_b, out_g)


def _conv_bwd(name, dmix3, proj3, cv3, conv_w, ln_g, ln_b, out_g):
    bsz, seq, _ = proj3.shape
    c = conv_w.shape[1]
    tt = _pick(seq, TILE["conv_t"], CONV_HALO)
    hb = tt // CONV_HALO
    nt = seq // tt
    last_hb = seq // CONV_HALO - 1
    ext = tt + CONV_HALO
    first = CONV_HALO - (CONV_WIDTH - 1)

    def body(v_ref, g_ref, vp_ref, gp_ref, cv_ref, cvn_ref, d_ref, dn_ref, w_ref, lg_ref, lb_ref, og_ref,
             o_ref, dw_ref, dcb_ref, dlg_ref, dlb_ref, dog_ref, a_ref, dc_ref, b_ref, dcz_ref, da_ref):
        i = pl.program_id(1)

        @pl.when((pl.program_id(0) == 0) & (i == 0))
        def _():
            for r in (dw_ref, dcb_ref, dlg_ref, dlb_ref, dog_ref):
                r[...] = jnp.zeros_like(r)

        keep_prev = (i > 0).astype(F32)
        keep_next = (i < nt - 1).astype(F32)
        sig_g = _sigmoid(g_ref[0])
        a_ref[pl.ds(0, CONV_HALO), :] = keep_prev * vp_ref[0] * _sigmoid(gp_ref[0])
        a_ref[pl.ds(CONV_HALO, tt), :] = v_ref[0] * sig_g

        lg, lb, og = lg_ref[...], lb_ref[...], og_ref[...]
        dc_own, d_og, d_lg, d_lb = _conv_post_bwd(cv_ref[0], d_ref[0], lg, lb, og)
        dc_next, _, _, _ = _conv_post_bwd(cvn_ref[0], keep_next * dn_ref[0], lg, lb, og)
        dog_ref[...] += d_og
        dlg_ref[...] += d_lg
        dlb_ref[...] += d_lb
        dcb_ref[...] += jnp.sum(dc_own, axis=0, keepdims=True)
        dc_ref[pl.ds(0, tt), :] = dc_own
        dc_ref[pl.ds(tt, CONV_HALO), :] = dc_next
        dcz_ref[pl.ds(0, SUBLANES), :] = jnp.zeros((SUBLANES, c), F32)
        dcz_ref[pl.ds(SUBLANES, tt), :] = dc_own
        dcz_ref[pl.ds(SUBLANES + tt, SUBLANES), :] = jnp.zeros((SUBLANES, c), F32)

        def put_da(off, rows8):
            da_ref[pl.ds(off, SUBLANES), :] = rows8

        _conv_taps(dc_ref, w_ref, b_ref, put_da, 0, tt, flip=True)

        for r, taps in _tap_groups(first):
            def blk(j, accs, r=r, taps=taps):
                off = pl.multiple_of(j * SUBLANES, SUBLANES)
                ds = dcz_ref[pl.ds(off + SUBLANES, SUBLANES), :]
                if r:
                    ds = _shift_rows(dcz_ref[pl.ds(off, SUBLANES), :], ds, SUBLANES - r)
                return tuple(acc + ds * a_ref[pl.ds(off + base, SUBLANES), :] for acc, (base, _) in zip(accs, taps))

            zero = jnp.zeros((SUBLANES, c), F32)
            accs = lax.fori_loop(0, tt // SUBLANES + (1 if r else 0), blk, tuple(zero for _ in taps))
            for acc, (_, k) in zip(accs, taps):
                dw_ref[k:k + 1, :] += jnp.sum(acc, axis=0, keepdims=True)
        val, da = v_ref[0], da_ref[...]
        o_ref[0] = jnp.concatenate([da * sig_g, da * val * sig_g * (1.0 - sig_g)], axis=-1).astype(BF16)

    vec = pl.BlockSpec((1, c), lambda b, i: (0, 0))
    cur = lambda col: pl.BlockSpec((1, tt, c), lambda b, i: (b, i, col))
    prev = lambda col: pl.BlockSpec((1, CONV_HALO, c), lambda b, i: (b, jnp.maximum(i * hb - 1, 0), col))
    nxt = lambda col: pl.BlockSpec((1, CONV_HALO, c), lambda b, i: (b, jnp.minimum((i + 1) * hb, last_hb), col))
    wspec = pl.BlockSpec(conv_w.shape, lambda b, i: (0, 0))
    return pl.pallas_call(
        body, name=name, grid=(bsz, nt),
        in_specs=[cur(0), cur(1), prev(0), prev(1), cur(0), nxt(0), cur(0), nxt(0), wspec, vec, vec, vec],
        out_specs=[pl.BlockSpec((1, tt, 2 * c), lambda b, i: (b, i, 0)), wspec, vec, vec, vec, vec],
        out_shape=[_sds((bsz, seq, 2 * c), BF16), _sds(conv_w.shape, F32)] + [_sds((1, c), F32)] * 4,
        scratch_shapes=[pltpu.VMEM((CONV_HALO + tt, c), F32), pltpu.VMEM((ext, c), F32),
                        pltpu.VMEM((SUBLANES, tt + SUBLANES, c), F32), pltpu.VMEM((tt + 2 * SUBLANES, c), F32),
                        pltpu.VMEM((tt, c), F32)],
        compiler_params=_cparams("arbitrary", "arbitrary"),
    )(proj3, proj3, proj3, proj3, cv3, cv3, dmix3, dmix3, conv_w, ln_g, ln_b, out_g)


def _ssm_discretise(a_re, a_im, log_dt):
    dt = jnp.exp(log_dt)
    zr, zi = a_re * dt, a_im * dt
    mag = jnp.exp(zr)
    ar, ai = mag * jnp.cos(zi), mag * jnp.sin(zi)
    den = a_re * a_re + a_im * a_im
    nr = ar - 1.0
    return ar, ai, (nr * a_re + ai * a_im) / den, (ai * a_re - nr * a_im) / den


def _ssm_system(a_re, a_im, log_dt, a_re_x, a_im_x, log_dt_x, bt_re, bt_im):
    ar, ai, _, _ = _ssm_discretise(a_re, a_im, log_dt)
    _, _, cr, ci = _ssm_discretise(a_re_x, a_im_x, log_dt_x)
    return ar, ai, cr * bt_re - ci * bt_im, cr * bt_im + ci * bt_re


def _ssm_prep(name, prim):
    g, p = prim[0].shape

    def body(*refs):
        pwr_ref, pwi_ref, bbr_ref, bbi_ref = refs[8:]
        ar, ai, bbr, bbi = _ssm_system(*[r[...] for r in refs[:8]])
        bbr_ref[...] = bbr
        bbi_ref[...] = bbi
        pr, pi = ar, ai
        for k in range(SUBLANES):
            pwr_ref[k] = pr
            pwi_ref[k] = pi
            pr, pi = pr * ar - pi * ai, pr * ai + pi * ar

    return pl.pallas_call(
        body, name=name,
        out_shape=[_sds((SUBLANES, g, p), F32)] * 2 + [_sds(prim[6].shape, F32)] * 2,
        compiler_params=pltpu.CompilerParams(vmem_limit_bytes=VMEM_LIMIT),
    )(*prim)


def _ssm_param_grads(name, prim, dab_r, dab_i, dbb_r, dbb_i):
    g, p = prim[0].shape
    h = prim[6].shape[0] // g

    def body(*refs):
        dar_ref, dai_ref, dbr_ref, dbi_ref = refs[8:12]
        o_ar, o_ai, o_dt, o_br, o_bi = refs[12:]
        _, vjp = jax.vjp(_ssm_system, *[r[...] for r in refs[:8]])
        ct = (jnp.sum(dar_ref[...], axis=0), jnp.sum(dai_ref[...], axis=0), dbr_ref[...], dbi_ref[...])
        d_ar, d_ai, d_dt, d_arx, d_aix, d_dtx, d_br, d_bi = vjp(ct)
        per_group = lambda t: jnp.sum(t.reshape(g, h, p), axis=1)
        o_ar[...] = d_ar + per_group(d_arx)
        o_ai[...] = d_ai + per_group(d_aix)
        o_dt[...] = d_dt + jnp.sum(per_group(d_dtx), axis=1, keepdims=True)
        o_br[...] = d_br
        o_bi[...] = d_bi

    return pl.pallas_call(
        body, name=name,
        out_shape=[_sds(prim[k].shape, F32) for k in (0, 1, 2, 6, 7)],
        compiler_params=pltpu.CompilerParams(vmem_limit_bytes=VMEM_LIMIT),
    )(*prim, dab_r, dab_i, dbb_r, dbb_i)


def _cfma(xr, xi, cr, ci, sr, si):
    return xr + (cr * sr - ci * si), xi + (cr * si + ci * sr)


def _scan_tables(pw_r, pw_i, reverse):
    gp = pw_r.shape[1] * pw_r.shape[2]
    pr, pi = pw_r.reshape(SUBLANES, gp), pw_i.reshape(SUBLANES, gp)
    if reverse:
        pi = -pi
    row = jnp.arange(SUBLANES)[:, None]
    tabs = []
    for d in (1, 2, 4):
        keep = (row < SUBLANES - d) if reverse else (row >= d)
        tabs += [jnp.where(keep, pr[d - 1][None, :], 0.0), jnp.where(keep, pi[d - 1][None, :], 0.0)]
    tabs += [pr[::-1], pi[::-1]] if reverse else [pr, pi]
    return jnp.concatenate(tabs, axis=0)


MXU_DEPTH = 256


def _bands(c, gp):
    bw = min(c, MXU_DEPTH)
    return c // bw, bw, gp * bw // c


def _band_expand(rows16, w_ref, put, c, gp):
    nb, bw, sw = _bands(c, gp)
    for s in range(nb):
        band = rows16[:, s * bw:(s + 1) * bw]
        for half in (0, gp):
            cols = pl.ds(half + s * sw, sw)
            put(cols, _dot(band, w_ref[pl.ds(s * bw, bw), cols]))


def _band_contract(get16, w_ref, c, gp):
    nb, bw, sw = _bands(c, gp)
    out = []
    for s in range(nb):
        acc = None
        for half in (0, gp):
            cols = pl.ds(half + s * sw, sw)
            t = _dot_nt(get16(cols), w_ref[pl.ds(s * bw, bw), cols])
            acc = t if acc is None else acc + t
        out.append(acc)
    return out[0] if nb == 1 else jnp.concatenate(out, axis=1)


def _band_wgrad(name, a, a_block, c, b):
    n = a.shape[0]
    gp = b.shape[1] // 2
    nb, bw, sw = _bands(c, gp)
    tk = _pick(n, TILE["mm_bytes"] // (c * a.dtype.itemsize + 2 * gp * b.dtype.itemsize), 16)

    def body(a_ref, b_ref, o_ref):
        @pl.when(pl.program_id(0) == 0)
        def _():
            o_ref[...] = jnp.zeros_like(o_ref)

        for s in range(nb):
            band = a_ref[:, s * bw:(s + 1) * bw].astype(BF16)
            for h, half in enumerate((0, gp)):
                o_ref[pl.ds(s * bw, bw), pl.ds(h * sw, sw)] += _dot_tn(
                    band, b_ref[:, pl.ds(half + s * sw, sw)].astype(BF16))

    return pl.pallas_call(
        body, name=name, grid=(n // tk,),
        in_specs=[pl.BlockSpec((tk, c), lambda k: (k, a_block)), pl.BlockSpec((tk, 2 * gp), lambda k: (k, 0))],
        out_specs=pl.BlockSpec((c, 2 * sw), lambda k: (0, 0)),
        out_shape=_sds((c, 2 * sw), F32),
        compiler_params=_cparams("arbitrary"),
    )(a, b)


def _band_diag_take(comp, half, c, gp):
    nb, bw, sw = _bands(c, gp)
    return jnp.concatenate([_block_diag_take(comp[s * bw:(s + 1) * bw, half * sw:(half + 1) * sw], bw // SSM_GROUP)
                            for s in range(nb)], axis=0)


def _scan_fwd(name, tab, proj3, u_block, bbd, cdt):
    bsz, seq, _ = proj3.shape
    c, w = bbd.shape
    gp = w // 2
    tt = _pick(seq, TILE["scan_t"], 16)
    nblk = tt // SUBLANES
    cw = _pick(gp, TILE["scan_w"], LANES)

    def body(tab_ref, u_ref, bbd_ref, cdt_ref, xs_ref, y_ref, carry_ref, bu_ref):
        @pl.when(pl.program_id(1) == 0)
        def _():
            carry_ref[...] = jnp.zeros_like(carry_ref)

        def put_bu(cols, val):
            bu_ref[0, :, cols] = val

        _band_expand(u_ref[0].astype(BF16), bbd_ref, put_bu, c, gp)

        for ch in range(gp // cw):
            re, im = pl.ds(ch * cw, cw), pl.ds(gp + ch * cw, cw)

            def blk(r, carry, re=re, im=im):
                tabs = [tab_ref[pl.ds(SUBLANES * k, SUBLANES), re] for k in range(8)]
                rows = pl.ds(pl.multiple_of(r * SUBLANES, SUBLANES), SUBLANES)
                xr, xi = bu_ref[0, rows, re], bu_ref[0, rows, im]
                for j, d in enumerate((1, 2, 4)):
                    xr, xi = _cfma(xr, xi, tabs[2 * j], tabs[2 * j + 1], pltpu.roll(xr, d, 0), pltpu.roll(xi, d, 0))
                xr, xi = _cfma(xr, xi, tabs[6], tabs[7], carry[0], carry[1])
                xs_ref[0, rows, re] = xr
                xs_ref[0, rows, im] = xi
                last = SUBLANES - 1
                return (jnp.broadcast_to(xr[last:, :], xr.shape), jnp.broadcast_to(xi[last:, :], xi.shape))

            cr, ci = lax.fori_loop(0, nblk, blk, (carry_ref[:, re], carry_ref[:, im]), unroll=2)
            carry_ref[:, re] = cr
            carry_ref[:, im] = ci

        y_ref[0] = _band_contract(lambda cols: xs_ref[0, :, cols].astype(BF16), cdt_ref, c, gp)

    whole = lambda arr: pl.BlockSpec(arr.shape, lambda b, t: (0, 0))
    return pl.pallas_call(
        body, name=name, grid=(bsz, seq // tt),
        in_specs=[whole(tab), pl.BlockSpec((1, tt, c), lambda b, t: (b, t, u_block)), whole(bbd), whole(cdt)],
        out_specs=[pl.BlockSpec((1, tt, w), lambda b, t: (b, t, 0)), pl.BlockSpec((1, tt, c), lambda b, t: (b, t, 0))],
        out_shape=[_sds((bsz, seq, w), F32), _sds((bsz, seq, c), F32)],
        scratch_shapes=[pltpu.VMEM((SUBLANES, w), F32), pltpu.VMEM((1, tt, w), F32)],
        compiler_params=_cparams("arbitrary", "arbitrary"),
    )(tab, proj3, bbd, cdt)


def _scan_bwd(name, tab, dy3, xs3, du_skip3, bbd, cdt, exchange=None):
    bsz, seq, w = xs3.shape
    c = bbd.shape[0]
    gp = w // 2
    tt = _pick(seq, TILE["scan_t"], 16)
    nblk = tt // SUBLANES
    cw = _pick(gp, TILE["scan_w"], LANES)
    nt = seq // tt

    def body(tab_ref, dy_ref, xs_ref, halo_ref, skip_ref, bbd_ref, cdt_ref, lam_ref, du_ref, dar_ref, dai_ref,
             carry_ref, g_ref):
        t = pl.program_id(1)

        @pl.when(t == 0)
        def _():
            carry_ref[...] = jnp.zeros_like(carry_ref)

        @pl.when((pl.program_id(0) == 0) & (t == 0))
        def _():
            dar_ref[...] = jnp.zeros_like(dar_ref)
            dai_ref[...] = jnp.zeros_like(dai_ref)

        def put_g(cols, val):
            g_ref[0, :, cols] = val

        _band_expand(dy_ref[0], cdt_ref, put_g, c, gp)

        has_prev = (t < nt - 1).astype(F32)
        row0 = lax.broadcasted_iota(jnp.int32, (SUBLANES, cw), 0) == 0
        last = SUBLANES - 1

        for ch in range(gp // cw):
            re, im = pl.ds(ch * cw, cw), pl.ds(gp + ch * cw, cw)

            def step(rows, xm1r, xm1i, state, re=re, im=im):
                tabs = [tab_ref[pl.ds(SUBLANES * k, SUBLANES), re] for k in range(8)]
                cr, ci, accr, acci = state
                lr, li = g_ref[0, rows, re], g_ref[0, rows, im]
                for j, d in enumerate((1, 2, 4)):
                    lr, li = _cfma(lr, li, tabs[2 * j], tabs[2 * j + 1],
                                   pltpu.roll(lr, SUBLANES - d, 0), pltpu.roll(li, SUBLANES - d, 0))
                lr, li = _cfma(lr, li, tabs[6], tabs[7], cr, ci)
                lam_ref[0, rows, re] = lr
                lam_ref[0, rows, im] = li
                xr, xi = xs_ref[0, rows, re], xs_ref[0, rows, im]
                xpr = jnp.where(row0, jnp.broadcast_to(xm1r[last:, :], xr.shape), pltpu.roll(xr, 1, 0))
                xpi = jnp.where(row0, jnp.broadcast_to(xm1i[last:, :], xi.shape), pltpu.roll(xi, 1, 0))
                accr = accr + (lr * xpr + li * xpi)
                acci = acci + (li * xpr - lr * xpi)
                return (jnp.broadcast_to(lr[:1, :], lr.shape), jnp.broadcast_to(li[:1, :], li.shape), accr, acci)

            def blk(k, state, re=re, im=im, step=step):
                r = nblk - 1 - k
                rows = pl.ds(pl.multiple_of(r * SUBLANES, SUBLANES), SUBLANES)
                prev = pl.ds(pl.multiple_of((r - 1) * SUBLANES, SUBLANES), SUBLANES)
                return step(rows, xs_ref[0, prev, re], xs_ref[0, prev, im], state)

            zero = jnp.zeros((SUBLANES, cw), F32)
            state = lax.fori_loop(0, nblk - 1, blk, (carry_ref[:, re], carry_ref[:, im], zero, zero))
            cr, ci, accr, acci = step(pl.ds(0, SUBLANES), has_prev * halo_ref[0, :, re], has_prev * halo_ref[0, :, im], state)
            carry_ref[:, re] = cr
            carry_ref[:, im] = ci
            dar_ref[:, re] += accr
            dai_ref[:, re] += acci

        du = _band_contract(lambda cols: lam_ref[0, :, cols].astype(BF16), bbd_ref, c, gp)
        du_ref[0] = (du + skip_ref[0]).astype(BF16)

    tile = pl.BlockSpec((1, tt, w), lambda b, t: (b, nt - 1 - t, 0))
    thin = pl.BlockSpec((1, tt, c), lambda b, t: (b, nt - 1 - t, 0))
    halo = pl.BlockSpec((1, SUBLANES, w), lambda b, t: (b, jnp.maximum((nt - 1 - t) * nblk - 1, 0), 0))
    acc = pl.BlockSpec((SUBLANES, gp), lambda b, t: (0, 0))
    whole = lambda arr: pl.BlockSpec(arr.shape, lambda b, t: (0, 0))
    return _call(
        name, body, (bsz, nt), [whole(tab), thin, tile, halo, thin, whole(bbd), whole(cdt)], [tile, thin, acc, acc],
        [_sds(xs3.shape, F32), _sds((bsz, seq, c), BF16), _sds((SUBLANES, gp), F32), _sds((SUBLANES, gp), F32)],
        (tab, dy3, xs3, xs3, du_skip3, bbd, cdt), ("arbitrary", "arbitrary"),
        scratch=[pltpu.VMEM((SUBLANES, w), F32), pltpu.VMEM((1, tt, w), F32)], exchange=exchange)


def _gelu_parts(y):
    inner = _GELU_K * (y + _GELU_C * y * y * y)
    t = jnp.tanh(inner)
    return 0.5 * y * (1.0 + t), t


def _ssm_out_fwd(name, cx, proj, u_block, d_skip, glu_w, glu_b, out_g):
    n, c = cx.shape
    tm = _pick(n, TILE["row"], 16)

    def body(cx_ref, u_ref, d_ref, gw_ref, gb_ref, og_ref, y_ref, o_ref):
        y = cx_ref[...] + d_ref[...] * u_ref[...]
        y_ref[...] = y
        gy, _ = _gelu_parts(y)
        z = _dot(gy.astype(BF16), gw_ref[...]) + gb_ref[...]
        _, sh = _rms_stats(gy * _sigmoid(z))
        o_ref[...] = (sh * og_ref[...]).astype(BF16)

    vec = pl.BlockSpec((1, c), lambda i: (0, 0))
    row = pl.BlockSpec((tm, c), lambda i: (i, 0))
    return pl.pallas_call(
        body, name=name, grid=(n // tm,),
        in_specs=[row, pl.BlockSpec((tm, c), lambda i: (i, u_block)), vec, pl.BlockSpec(glu_w.shape, lambda i: (0, 0)),
                  vec, vec],
        out_specs=[row, row],
        out_shape=[_sds((n, c), F32), _sds((n, c), BF16)],
        compiler_params=_cparams("parallel"),
    )(cx, proj, d_skip, glu_w, glu_b, out_g)


def _ssm_out_bwd(name, dmix, d_block, y, proj, u_block, d_skip, glu_w, glu_b, out_g):
    n, c = y.shape
    tm = _pick(n, TILE["row"], 16)

    def body(d_ref, y_ref, u_ref, dk_ref, gw_ref, gb_ref, og_ref, dy_ref, du_ref, dgw_ref, dgb_ref, dog_ref, dd_ref):
        @pl.when(pl.program_id(0) == 0)
        def _():
            for r in (dgw_ref, dgb_ref, dog_ref, dd_ref):
                r[...] = jnp.zeros_like(r)

        yv = y_ref[...]
        gy, th = _gelu_parts(yv)
        gy16 = gy.astype(BF16)
        sz = _sigmoid(_dot(gy16, gw_ref[...]) + gb_ref[...])
        r, sh = _rms_stats(gy * sz)
        dout = d_ref[...]
        dog_ref[...] += jnp.sum(dout * sh, axis=0, keepdims=True)
        dsh = dout * og_ref[...]
        ds = r * (dsh - sh * jnp.mean(dsh * sh, axis=-1, keepdims=True))
        dz = ds * gy * sz * (1.0 - sz)
        dz16 = dz.astype(BF16)
        dgb_ref[...] += jnp.sum(dz, axis=0, keepdims=True)
        dgw_ref[...] += _dot_tn(gy16, dz16)
        dgy = ds * sz + _dot_nt(dz16, gw_ref[...])
        dgelu = 0.5 * (1.0 + th) + 0.5 * yv * (1.0 - th * th) * (_GELU_K * (1.0 + 3.0 * _GELU_C * yv * yv))
        dy = dgy * dgelu
        dy_ref[...] = dy.astype(BF16)
        du_ref[...] = dy * dk_ref[...]
        dd_ref[...] += jnp.sum(dy * u_ref[...], axis=0, keepdims=True)

    vec = pl.BlockSpec((1, c), lambda i: (0, 0))
    row = pl.BlockSpec((tm, c), lambda i: (i, 0))
    mat = pl.BlockSpec(glu_w.shape, lambda i: (0, 0))
    return pl.pallas_call(
        body, name=name, grid=(n // tm,),
        in_specs=[pl.BlockSpec((tm, c), lambda i: (i, d_block)), row, pl.BlockSpec((tm, c), lambda i: (i, u_block)),
                  vec, mat, vec, vec],
        out_specs=[row, row, mat, vec, vec, vec],
        out_shape=[_sds((n, c), BF16), _sds((n, c), F32), _sds(glu_w.shape, F32)] + [_sds((1, c), F32)] * 3,
        compiler_params=_cparams("arbitrary"),
    )(dmix, y, proj, d_skip, glu_w, glu_b, out_g)


def _mesh_pos():
    return tuple(lax.axis_index(a) for a in MESH_AXES)


def _other_chips(x, y):
    return [(1 - x, y), (x, 1 - y), (1 - x, 1 - y)]


def _remote(src, dst, send_sem, recv_sem, dev):
    return pltpu.make_async_remote_copy(src_ref=src, dst_ref=dst, send_sem=send_sem, recv_sem=recv_sem,
                                        device_id=dev, device_id_type=pl.DeviceIdType.MESH)


def _hbm_call(name, body, operands, out_shapes, scratch):
    hbm = pl.BlockSpec(memory_space=pltpu.HBM)
    return pl.pallas_call(body, name=name, in_specs=[hbm] * len(operands), out_specs=[hbm] * len(out_shapes),
                          out_shape=out_shapes, scratch_shapes=scratch)(*operands)


_Exchange = collections.namedtuple("_Exchange", "operands out_shapes scratch start finish")


def _run_exchange(name, plan):
    nin, nout = len(plan.operands), len(plan.out_shapes)

    def body(*refs):
        parts = refs[:nin], refs[nin:nin + nout], refs[nin + nout:]
        plan.start(*parts)
        plan.finish(*parts)

    return _hbm_call(name, body, plan.operands, plan.out_shapes, plan.scratch)


def _gather_plan(blocks):
    nop = len(blocks)

    def copies(x_refs, o_refs, sems):
        send_sems, recv_sems, local_sems = sems
        x, y, c = _mesh_pos()
        me, sibling = (x, y, c), (x, y, 1 - c)
        chips = _other_chips(x, y)

        def copy(i, k, block_of, to, src=None):
            dst = o_refs[i].at[4 * block_of[0] + 2 * block_of[1] + block_of[2]]
            return _remote(dst if src is None else src, dst, send_sems.at[i, k], recv_sems.at[i, k], to)

        own = [pltpu.make_async_copy(x_refs[i], o_refs[i].at[4 * x + 2 * y + c], local_sems.at[i]) for i in range(nop)]
        first = []
        for i in range(nop):
            first.append(copy(i, 0, me, sibling, src=x_refs[i]))
            first += [copy(i, 1 + j, me, (*chip, c), src=x_refs[i]) for j, chip in enumerate(chips)]
        return copy, own, first, me, sibling, chips, c

    def start(x_refs, o_refs, sems):
        _, own, first, *_ = copies(x_refs, o_refs, sems)
        for cp in own + first:
            cp.start()

    def finish(x_refs, o_refs, sems):
        copy, own, first, me, sibling, chips, c = copies(x_refs, o_refs, sems)
        passed = []
        for i in range(nop):
            for j, chip in enumerate(chips):
                copy(i, 1 + j, (*chip, c), me).wait_recv()
                passed.append(copy(i, 4 + j, (*chip, c), sibling))
                passed[-1].start()
        for i in range(nop):
            copy(i, 0, sibling, me).wait_recv()
            for j, chip in enumerate(chips):
                copy(i, 4 + j, (*chip, 1 - c), me).wait_recv()
        for cp in first + passed:
            cp.wait_send()
        for cp in own:
            cp.wait()

    return _Exchange(list(blocks), [_sds((N_DEV,) + b.shape, b.dtype) for b in blocks],
                     [pltpu.SemaphoreType.DMA((nop, N_DEV - 1)), pltpu.SemaphoreType.DMA((nop, N_DEV - 1)),
                      pltpu.SemaphoreType.DMA((nop,))], start, finish)


def _core_exchange_plan(grads):
    nop = len(grads)

    def copies(x_refs, o_refs, sems):
        send_sems, recv_sems = sems
        x, y, c = _mesh_pos()
        return [_remote(x_refs[i].at[2 * q + (1 - c)], o_refs[i].at[q], send_sems.at[i, q], recv_sems.at[i, q],
                        (x, y, 1 - c)) for i in range(nop) for q in range(N_DEV // 2)]

    def start(x_refs, o_refs, sems):
        for cp in copies(x_refs, o_refs, sems):
            cp.start()

    def finish(x_refs, o_refs, sems):
        for cp in copies(x_refs, o_refs, sems):
            cp.wait()

    return _Exchange(list(grads), [_sds((N_DEV // 2,) + g.shape[1:], g.dtype) for g in grads],
                     [pltpu.SemaphoreType.DMA((nop, N_DEV // 2)), pltpu.SemaphoreType.DMA((nop, N_DEV // 2))],
                     start, finish)


def _pair_sum(name, grad, other):
    nchip, _, r, c = grad.shape
    tr = _pick(r, max(SUBLANES, TILE["sum_bytes"] // (4 * c)), SUBLANES)
    core = lax.axis_index("c").astype(jnp.int32).reshape(1)

    def body(core_ref, g_ref, o_ref, s_ref):
        s_ref[0] = (g_ref[0, 0] + o_ref[0]).astype(s_ref.dtype)

    tile = pl.BlockSpec((1, tr, c), lambda q, t, core_ref: (q, t, 0))
    return pl.pallas_call(
        body, name=name,
        grid_spec=pltpu.PrefetchScalarGridSpec(
            num_scalar_prefetch=1, grid=(nchip, r // tr),
            in_specs=[pl.BlockSpec((1, 1, tr, c), lambda q, t, core_ref: (q, core_ref[0], t, 0)), tile],
            out_specs=tile),
        out_shape=_sds((nchip, r, c), BF16),
        compiler_params=_cparams("parallel", "parallel"),
    )(core, grad, other)


def _chip_exchange_plan(sums):
    nop = len(sums)

    def copies(x_refs, o_refs, sems, arriving):
        send_sems, recv_sems, local_sems = sems
        x, y, c = _mesh_pos()
        mine = 2 * x + y
        out = []
        for i in range(nop):
            for j, (px, py) in enumerate(_other_chips(x, y)):
                theirs = 2 * px + py
                src, dst = (mine, theirs) if arriving else (theirs, mine)
                out.append(_remote(x_refs[i].at[src], o_refs[i].at[dst], send_sems.at[i, j], recv_sems.at[i, j],
                                   (px, py, c)))
        if not arriving:
            out += [pltpu.make_async_copy(x_refs[i].at[mine], o_refs[i].at[mine], local_sems.at[i]) for i in range(nop)]
        return out

    def start(x_refs, o_refs, sems):
        for cp in copies(x_refs, o_refs, sems, False):
            cp.start()

    def finish(x_refs, o_refs, sems):
        for cp in copies(x_refs, o_refs, sems, True):
            cp.wait_recv()
        mine = copies(x_refs, o_refs, sems, False)
        for cp in mine[:3 * nop]:
            cp.wait_send()
        for cp in mine[3 * nop:]:
            cp.wait()

    return _Exchange(list(sums), [_sds(s.shape, s.dtype) for s in sums],
                     [pltpu.SemaphoreType.DMA((nop, 3)), pltpu.SemaphoreType.DMA((nop, 3)), pltpu.SemaphoreType.DMA((nop,))],
                     start, finish)


def _part_rows(npart, r, c):
    return _pick(r, max(SUBLANES, TILE["sum_bytes"] // (4 * npart * c)), SUBLANES)


def _sum_slots(p_ref):
    g = p_ref[0].astype(F32)
    for k in range(1, p_ref.shape[0]):
        g = g + p_ref[k].astype(F32)
    return g


def _adamw_step(g, w, m, v):
    c1 = 1.0 - ADAM_B1 ** ADAM_STEP
    c2 = 1.0 - ADAM_B2 ** ADAM_STEP
    nm = ADAM_B1 * m + (1.0 - ADAM_B1) * g
    nv = ADAM_B2 * v + (1.0 - ADAM_B2) * (g * g)
    return -ADAM_LR * ((nm / c1) / (jnp.sqrt(nv / c2) + ADAM_EPS) + ADAM_WD * w), nm, nv


def _adamw_small(name, parts, ws, ms, vs):
    nparam, nall = len(ws), len(parts)

    def body(*refs):
        p_refs = refs[:nall]
        w_refs, m_refs, v_refs = (refs[nall + k * nparam:nall + (k + 1) * nparam] for k in range(3))
        outs = refs[nall + 3 * nparam:]
        for p in range(nall):
            g = _sum_slots(p_refs[p])
            if p < nparam:
                delta, nm, nv = _adamw_step(g, w_refs[p][...], m_refs[p][...], v_refs[p][...])
                for o_ref, val in zip(outs[4 * p:4 * p + 4], (g, delta, nm, nv)):
                    o_ref[...] = val
            else:
                outs[4 * nparam + p - nparam][...] = g

    shapes = [_sds(w.shape, F32) for w in ws for _ in range(4)] + [_sds(p.shape[1:], F32) for p in parts[nparam:]]
    res = pl.pallas_call(body, name=name, out_shape=shapes,
                         compiler_params=pltpu.CompilerParams(vmem_limit_bytes=VMEM_LIMIT))(*parts, *ws, *ms, *vs)
    return [res[4 * p:4 * p + 4] for p in range(nparam)] + [[r] for r in res[4 * nparam:]]


def _adamw(name, parts, w, m, v):
    npart, r, c = parts.shape
    lead = len(w.shape) - 2
    tr = _part_rows(npart, r, c)
    at = (0,) * lead + (slice(None), slice(None))

    def body(p_ref, w_ref, m_ref, v_ref, g_ref, d_ref, nm_ref, nv_ref):
        g = _sum_slots(p_ref)
        delta, nm, nv = _adamw_step(g, w_ref[at], m_ref[at], v_ref[at])
        g_ref[at] = g
        nm_ref[at] = nm
        nv_ref[at] = nv
        d_ref[at] = delta

    row = pl.BlockSpec((1,) * lead + (tr, c), lambda i: (0,) * lead + (i, 0))
    return pl.pallas_call(
        body, name=name, grid=(r // tr,),
        in_specs=[pl.BlockSpec((npart, tr, c), lambda i: (0, i, 0)), row, row, row],
        out_specs=[row] * 4,
        out_shape=[_sds(w.shape, F32)] * 4,
        compiler_params=_cparams("parallel"),
    )(parts, w, m, v)


def _block_diag(rows_gh, groups):
    gh, p = rows_gh.shape
    own = (jnp.arange(gh)[:, None] // (gh // groups) == jnp.arange(groups)[None, :]).astype(rows_gh.dtype)
    return (own[:, :, None] * rows_gh[:, None, :]).reshape(gh, groups * p)


def _block_diag_take(dense, groups):
    gh = dense.shape[0]
    p = dense.shape[1] // groups
    own = (jnp.arange(gh)[:, None] // (gh // groups) == jnp.arange(groups)[None, :]).astype(dense.dtype)
    return jnp.sum(dense.reshape(gh, groups, p) * own[:, :, None], axis=1)


FFN1 = ("ffn1_w1", "ffn1_w3", "ffn1_w2")
MIXER = ("w_in", "ssm_glu_w", "w_out")
FFN2 = ("ffn2_w1", "ffn2_w3", "ffn2_w2")
BIG = FFN1 + MIXER + FFN2
COL_SHARDED = ("ffn1_w1", "ffn1_w3", "w_in", "ffn2_w1", "ffn2_w3", "conv_w")
SMALL = ("norm_ffn1", "norm_mix", "conv_b", "conv_ln_g", "conv_ln_b", "conv_out_g", "ssm_A_re", "ssm_A_im",
         "ssm_log_dt", "ssm_B_re", "ssm_B_im", "ssm_C_re", "ssm_C_im", "ssm_D", "ssm_glu_b", "ssm_out_g",
         "norm_ffn2", "norm_final")
WEIGHTS = ("norm_ffn1", "ffn1_w1", "ffn1_w3", "ffn1_w2", "norm_mix", "w_in", "conv_w", "conv_b", "conv_ln_g",
           "conv_ln_b", "conv_out_g", "ssm_A_re", "ssm_A_im", "ssm_log_dt", "ssm_B_re", "ssm_B_im", "ssm_C_re",
           "ssm_C_im", "ssm_D", "ssm_glu_w", "ssm_glu_b", "ssm_out_g", "w_out", "norm_ffn2", "ffn2_w1", "ffn2_w3",
           "ffn2_w2", "norm_final")


def _ffn_backward(tag, dxo, x, g, w1, w3, w2, saved, exchange=None, dw_exchange=None, reduce_plan=None):
    a, b, h = saved
    (da, db, hid, dxh), got = _ffn_bwd_hidden(tag + "_bwd_hidden", dxo, a, b, w2, exchange=exchange)
    dw1, dw_got = _mm_tn(tag + "_dw1", da, h, exchange=dw_exchange) if dw_exchange else (_mm_tn(tag + "_dw1", da, h), None)
    dws = [dw1, _mm_tn(tag + "_dw3", db, h), _mm_tn(tag + "_dw2", hid, dxh)]
    f = a.shape[1]
    (dx, dg), reduced = _dx_rms_bwd(tag + "_bwd_dx", [(da, f, 0, w1, f, 0), (db, f, 0, w3, f, 0)], dxo, x, g,
                                    exchange=reduce_plan(dws) if reduce_plan else None)
    return (dx, dg, dws), got, dw_got, reduced


def _reduce_in_chip(names, grads):
    send = [g.reshape((N_DEV, -1) + g.shape[1:]) for g in grads]

    def then(from_core):
        return _chip_exchange_plan([_pair_sum("pair_sum_" + k, s.reshape((N_DEV // 2, 2) + s.shape[1:]), o)
                                    for k, s, o in zip(names, send, from_core)])

    return _core_exchange_plan(send), then


def kernel(x, norm_ffn1, ffn1_w1, ffn1_w3, ffn1_w2, norm_mix, w_in, conv_w, conv_b, conv_ln_g, conv_ln_b, conv_out_g, ssm_A_re, ssm_A_im, ssm_log_dt, ssm_B_re, ssm_B_im, ssm_C_re, ssm_C_im, ssm_D, ssm_glu_w, ssm_glu_b, ssm_out_g, w_out, norm_ffn2, ffn2_w1, ffn2_w3, ffn2_w2, norm_final, loss_target, m_norm_ffn1, m_ffn1_w1, m_ffn1_w3, m_ffn1_w2, m_norm_mix, m_w_in, m_conv_w, m_conv_b, m_conv_ln_g, m_conv_ln_b, m_conv_out_g, m_ssm_A_re, m_ssm_A_im, m_ssm_log_dt, m_ssm_B_re, m_ssm_B_im, m_ssm_C_re, m_ssm_C_im, m_ssm_D, m_ssm_glu_w, m_ssm_glu_b, m_ssm_out_g, m_w_out, m_norm_ffn2, m_ffn2_w1, m_ffn2_w3, m_ffn2_w2, m_norm_final, v_norm_ffn1, v_ffn1_w1, v_ffn1_w3, v_ffn1_w2, v_norm_mix, v_w_in, v_conv_w, v_conv_b, v_conv_ln_g, v_conv_ln_b, v_conv_out_g, v_ssm_A_re, v_ssm_A_im, v_ssm_log_dt, v_ssm_B_re, v_ssm_B_im, v_ssm_C_re, v_ssm_C_im, v_ssm_D, v_ssm_glu_w, v_ssm_glu_b, v_ssm_out_g, v_w_out, v_norm_ffn2, v_ffn2_w1, v_ffn2_w3, v_ffn2_w2, v_norm_final):
    args = dict(locals())
    wt = {n: args[n] for n in WEIGHTS}
    mom = {n: args["m_" + n] for n in WEIGHTS}
    var = {n: args["v_" + n] for n in WEIGHTS}

    bsz, seq, d = x.shape
    n = bsz * seq
    c = conv_b.shape[-1]
    groups = c // SSM_GROUP
    gp = groups * SSM_STATE
    u_b = 2

    shard = {k: (wt[k][0].T if k in COL_SHARDED else wt[k][0]).astype(BF16) for k in BIG}
    gathered = _run_exchange("gather_weights_ffn1", _gather_plan([shard[k] for k in FFN1]))
    full = {k: g.reshape(-1, g.shape[-1]) for k, g in zip(FFN1, gathered)}
    gather_rest = _gather_plan([shard[k] for k in MIXER + FFN2] + [wt["conv_w"][0]])

    vec = lambda k: wt[k].reshape(1, -1)
    g_ffn1, g_mix, g_ffn2, g_fin = vec("norm_ffn1"), vec("norm_mix"), vec("norm_ffn2"), vec("norm_final")
    cb, lng, lnb, cog = vec("conv_b"), vec("conv_ln_g"), vec("conv_ln_b"), vec("conv_out_g")
    d_skip, glu_b, sog = vec("ssm_D"), vec("ssm_glu_b"), vec("ssm_out_g")

    a_re, a_im = wt["ssm_A_re"][0], wt["ssm_A_im"][0]
    log_dt = wt["ssm_log_dt"][0].reshape(groups, 1)
    bt_re = wt["ssm_B_re"][0].transpose(0, 2, 1).reshape(groups * SSM_GROUP, SSM_STATE)
    bt_im = wt["ssm_B_im"][0].transpose(0, 2, 1).reshape(groups * SSM_GROUP, SSM_STATE)
    c_re = wt["ssm_C_re"][0].reshape(groups * SSM_GROUP, SSM_STATE)
    c_im = wt["ssm_C_im"][0].reshape(groups * SSM_GROUP, SSM_STATE)
    per_chan = lambda t: jnp.repeat(t, SSM_GROUP, axis=0)
    ssm_prim = (a_re, a_im, log_dt, per_chan(a_re), per_chan(a_im), per_chan(jnp.broadcast_to(log_dt, a_re.shape)),
                bt_re, bt_im)
    pw_r, pw_i, bb_r, bb_i = _ssm_prep("ssm_prep", ssm_prim)
    tab_f = _scan_tables(pw_r, pw_i, False)
    tab_b = _scan_tables(pw_r, pw_i, True)
    bbd = jnp.concatenate([_block_diag(bb_r, groups), _block_diag(bb_i, groups)], axis=1).astype(BF16)
    cdt = jnp.concatenate([_block_diag(c_re, groups), -_block_diag(c_im, groups)], axis=1).astype(BF16)

    x0 = x.reshape(n, d)
    (x1, *ffn1_saved), gathered = _ffn_fwd("ffn1_fwd", x0, g_ffn1, full["ffn1_w1"], full["ffn1_w3"], full["ffn1_w2"],
                                           exchange=gather_rest)
    full.update({k: g.reshape(-1, g.shape[-1]) for k, g in zip(MIXER + FFN2, gathered)})
    conv_w_full = gathered[-1].transpose(1, 0, 2).reshape(CONV_WIDTH, c)
    conv_w_pad = jnp.pad(conv_w_full, ((0, CONV_HALO - CONV_WIDTH), (0, 0)))
    (proj,), h2 = _rms_mm("mix_in", x1, g_mix, [full["w_in"]], F32)
    proj3 = proj.reshape(bsz, seq, 3 * c)
    an3, cv3 = _conv_fwd("conv_fwd", proj3, conv_w_pad, cb, lng, lnb, cog)
    an = an3.reshape(n, c)
    xs3, cx3 = _scan_fwd("scan_fwd", tab_f, proj3, u_b, bbd, cdt)
    xs = xs3.reshape(n, 2 * gp)
    y, sn = _ssm_out_fwd("ssm_out_fwd", cx3.reshape(n, c), proj, u_b, d_skip, full["ssm_glu_w"], glu_b, sog)
    w_o = full["w_out"]
    x2, _ = _row_mm("mix_out", [(an, c, 0, w_o, c, 0, False), (sn, c, 0, w_o, c, 1, False)], d, F32, add=x1)
    (dx3, *ffn2_saved, loss_tile, d_gfin), _ = _ffn_fwd(
        "ffn2_fwd", x2, g_ffn2, full["ffn2_w1"], full["ffn2_w3"], full["ffn2_w2"],
        head=(g_fin, loss_target.reshape(n, d)))
    loss = lax.psum(loss_tile[0, 0], MESH_AXES)

    grads, from_chips = {}, {}
    (dx2, grads["norm_ffn2"], dws), _, _, _ = _ffn_backward(
        "ffn2", dx3, x2, g_ffn2, full["ffn2_w1"], full["ffn2_w3"], full["ffn2_w2"], ffn2_saved)
    in_chip, across_chips = _reduce_in_chip(FFN2, dws)

    dmix, got = _row_mm("mix_out_bwd", [(dx2, d, 0, w_o, 2 * c, 0, True)], 2 * c, F32, exchange=in_chip)
    reduce_ffn2 = across_chips(got)
    grads["w_out"] = jnp.concatenate([_mm_tn("dw_out_a", an, dx2), _mm_tn("dw_out_s", sn, dx2)], axis=0)

    dy, du_skip, grads["ssm_glu_w"], grads["ssm_glu_b"], grads["ssm_out_g"], grads["ssm_D"] = _ssm_out_bwd(
        "ssm_out_bwd", dmix, 1, y, proj, u_b, d_skip, full["ssm_glu_w"], glu_b, sog)
    (lam3, du3, dab_r, dab_i), got = _scan_bwd("scan_bwd", tab_b, dy.reshape(bsz, seq, c), xs3,
                                               du_skip.reshape(bsz, seq, c), bbd, cdt, exchange=reduce_ffn2)
    from_chips.update(zip(FFN2, got))
    lam, du = lam3.reshape(n, 2 * gp), du3.reshape(n, c)
    d_bbd = _band_wgrad("ssm_dbb", proj, u_b, c, lam)
    d_cdt = _band_wgrad("ssm_dc", dy, 0, c, xs)
    d_are, d_aim, d_ldt, d_btr, d_bti = _ssm_param_grads(
        "ssm_param_grads", ssm_prim,
        dab_r.reshape(SUBLANES, groups, SSM_STATE), dab_i.reshape(SUBLANES, groups, SSM_STATE),
        _band_diag_take(d_bbd, 0, c, gp), _band_diag_take(d_bbd, 1, c, gp))
    grads["ssm_A_re"], grads["ssm_A_im"], grads["ssm_log_dt"] = d_are, d_aim, d_ldt
    grads["ssm_B_re"], grads["ssm_B_im"] = d_btr, d_bti
    grads["ssm_C_re"] = _band_diag_take(d_cdt, 0, c, gp)
    grads["ssm_C_im"] = -_band_diag_take(d_cdt, 1, c, gp)

    dconv3, d_cw, grads["conv_b"], grads["conv_ln_g"], grads["conv_ln_b"], grads["conv_out_g"] = _conv_bwd(
        "conv_bwd", dmix.reshape(bsz, seq, 2 * c), proj3, cv3, conv_w_pad, lng, lnb, cog)
    dconv = dconv3.reshape(n, 2 * c)
    grads["conv_w"] = d_cw[:CONV_WIDTH]
    grads["w_in"] = jnp.concatenate([_mm_tn("dw_in_conv", dconv, h2), _mm_tn("dw_in_ssm", du, h2)], axis=0)
    w_i = full["w_in"]
    in_chip, across_chips = _reduce_in_chip(MIXER, [grads[k] for k in MIXER])
    (dx1, grads["norm_mix"]), got = _dx_rms_bwd("mix_in_bwd", [(dconv, 2 * c, 0, w_i, 2 * c, 0), (du, c, 0, w_i, c, 2)],
                                                dx2, x1, g_mix, exchange=in_chip)
    reduce_mixer = across_chips(got)

    def reduce_ffn1(dws):
        in_chip, across_chips = _reduce_in_chip(FFN1, dws)
        return across_chips(_run_exchange("exchange_core_ffn1", in_chip))

    grads["norm_final"] = d_gfin
    early = tuple(k for k in SMALL if k != "norm_ffn1")
    gather_small = _gather_plan([grads[k] for k in early] + [grads["conv_w"]])

    (dx0, grads["norm_ffn1"], _), got, small_parts, reduced = _ffn_backward(
        "ffn1", dx1, x0, g_ffn1, full["ffn1_w1"], full["ffn1_w3"], full["ffn1_w2"], ffn1_saved,
        exchange=reduce_mixer, dw_exchange=gather_small, reduce_plan=reduce_ffn1)
    from_chips.update(zip(MIXER, got))
    from_chips.update(zip(FFN1, reduced))

    res = {}
    for k in BIG:
        parts = from_chips[k]
        if k in COL_SHARDED:
            swap = lambda t: jnp.swapaxes(t, -1, -2)
            res[k] = [swap(t) for t in _adamw("adamw_" + k, parts, swap(wt[k]), swap(mom[k]), swap(var[k]))]
        else:
            res[k] = _adamw("adamw_" + k, parts, wt[k], mom[k], var[k])

    def as_2d(k, t):
        if k in ("ssm_B_re", "ssm_B_im"):
            return t[0].transpose(0, 2, 1).reshape(-1, SSM_STATE)
        if k in ("ssm_C_re", "ssm_C_im"):
            return t[0].reshape(-1, SSM_STATE)
        if k in ("ssm_A_re", "ssm_A_im"):
            return t[0]
        return t.reshape(-1, 1) if k == "ssm_log_dt" else t.reshape(1, -1)

    def as_param(k, t):
        if k in ("ssm_B_re", "ssm_B_im"):
            t = t.reshape(groups, SSM_GROUP, SSM_STATE).transpose(0, 2, 1)
        return t.reshape(wt[k].shape)

    (last_part,) = _run_exchange("gather_norm_ffn1_grad", _gather_plan([grads["norm_ffn1"]]))
    order = ("norm_ffn1",) + early
    updated = _adamw_small("adamw_replicated", [last_part] + small_parts,
                           *[[as_2d(k, src[k]) for k in order] for src in (wt, mom, var)])
    res.update({k: [as_param(k, t) for t in upd] for k, upd in zip(order, updated)})
    (conv_w_grad,) = updated[-1]
    x_pos, y_pos, c_pos = (lax.axis_index(a) for a in MESH_AXES)
    cw_cols = c // N_DEV
    own_cw = lax.dynamic_slice_in_dim(conv_w_grad, (4 * x_pos + 2 * y_pos + c_pos) * cw_cols, cw_cols, axis=1)
    res["conv_w"] = _adamw("adamw_conv_w", own_cw[None], wt["conv_w"], mom["conv_w"], var["conv_w"])

    outs = [loss, dx0.reshape(bsz, seq, d)]
    for kind in range(4):
        outs += [res[k][kind] for k in WEIGHTS]
    return tuple(outs)
```

```python
import collections
import functools
import math

import jax
import jax.numpy as jnp
from jax import lax
from jax.experimental import pallas as pl
from jax.experimental.pallas import tpu as pltpu

F32 = jnp.float32
BF16 = jnp.bfloat16

EPS = 1e-6
FFN_RES = 0.5
CONV_WIDTH = 31
CONV_HALO = 32
SSM_GROUP = 16
SSM_STATE = 64
ADAM_LR, ADAM_B1, ADAM_B2, ADAM_EPS, ADAM_WD, ADAM_STEP = 0.001, 0.9, 0.999, 1e-08, 0.01, 10

N_DEV = 8
MESH_AXES = ("x", "y", "c")
SUBLANES = 8
LANES = 128
V7X_VMEM_BYTES = 64 * 2**20
VMEM_LIMIT = V7X_VMEM_BYTES - 8 * 2**20

TILE = dict(row=256, ffn_m=512, mm_bytes=8 * 2**20, up_m=1024, up_n=256, wide_n=2048, conv_t=512, scan_t=256, scan_w=512,
            sum_bytes=4 * 2**20)

_GELU_K = math.sqrt(2.0 / math.pi)
_GELU_C = 0.044715


def _pick(n, target, mult):
    best = None
    for t in range(mult, min(n, target) + 1, mult):
        if n % t == 0:
            best = t
    return n if best is None else best


def _cparams(*sem):
    return pltpu.CompilerParams(dimension_semantics=sem, vmem_limit_bytes=VMEM_LIMIT)


def _sds(shape, dtype):
    return jax.ShapeDtypeStruct(shape, dtype)


def _call(name, body, grid, in_specs, out_specs, out_shape, operands, sem, scratch=(), exchange=None):
    if exchange is None:
        res = pl.pallas_call(body, name=name, grid=grid, in_specs=list(in_specs), out_specs=list(out_specs),
                             out_shape=list(out_shape), scratch_shapes=list(scratch),
                             compiler_params=_cparams(*sem))(*operands)
        return list(res), None
    n_in, n_out, n_scr = len(in_specs), len(out_specs), len(scratch)
    n_xin, n_xout = len(exchange.operands), len(exchange.out_shapes)
    hbm = pl.BlockSpec(memory_space=pltpu.HBM)

    def with_exchange(*refs):
        cuts, pos = [], 0
        for size in (n_in, n_xin, n_out, n_xout, n_scr):
            cuts.append(refs[pos:pos + size])
            pos += size
        ins, x_in, outs, x_out, scr = cuts
        sems = refs[pos:]
        ids = [pl.program_id(axis) for axis in range(len(grid))]
        first = functools.reduce(lambda p, q: p & q, [i == 0 for i in ids])
        last = functools.reduce(lambda p, q: p & q, [i == g - 1 for i, g in zip(ids, grid)])

        @pl.when(first)
        def _():
            exchange.start(x_in, x_out, sems)

        body(*ins, *outs, *scr)

        @pl.when(last)
        def _():
            exchange.finish(x_in, x_out, sems)

    res = pl.pallas_call(
        with_exchange, name=name, grid=grid, in_specs=list(in_specs) + [hbm] * n_xin,
        out_specs=list(out_specs) + [hbm] * n_xout, out_shape=list(out_shape) + list(exchange.out_shapes),
        scratch_shapes=list(scratch) + list(exchange.scratch),
        compiler_params=_cparams(*["arbitrary"] * len(grid)))(*operands, *exchange.operands)
    return list(res[:n_out]), list(res[n_out:])


def _dot(a, b):
    return jnp.dot(a, b, preferred_element_type=F32)


def _dot_nt(a, b):
    return lax.dot_general(a, b, (((1,), (1,)), ((), ())), preferred_element_type=F32)


def _dot_tn(a, b):
    return lax.dot_general(a, b, (((0,), (0,)), ((), ())), preferred_element_type=F32)


def _sigmoid(x):
    return 0.5 * jnp.tanh(0.5 * x) + 0.5


def _rms_stats(x):
    r = lax.rsqrt(jnp.mean(x * x, axis=-1, keepdims=True) + EPS)
    return r, x * r


def _rms_bwd(x, g, dy):
    r, xh = _rms_stats(x)
    dxh = dy * g
    dx = r * (dxh - xh * jnp.mean(dxh * xh, axis=-1, keepdims=True))
    return dx, jnp.sum(dy * xh, axis=0, keepdims=True)


def _rms_mm(name, x, g, ws, out_dtype):
    n, d = x.shape
    f = ws[0].shape[0]
    nw = len(ws)
    tm, tn = _pick(n, TILE["up_m"], 16), _pick(f, TILE["wide_n"], LANES)

    def body(x_ref, g_ref, *refs):
        w_refs, o_refs, h_ref = refs[:nw], refs[nw:2 * nw], refs[2 * nw]

        @pl.when(pl.program_id(1) == 0)
        def _():
            _, xh = _rms_stats(x_ref[...])
            h_ref[...] = (xh * g_ref[...]).astype(BF16)

        h = h_ref[...]
        for w_ref, o_ref in zip(w_refs, o_refs):
            o_ref[...] = _dot_nt(h, w_ref[...]).astype(o_ref.dtype)

    outs = pl.pallas_call(
        body, name=name, grid=(n // tm, f // tn),
        in_specs=[pl.BlockSpec((tm, d), lambda i, j: (i, 0)), pl.BlockSpec((1, d), lambda i, j: (0, 0))]
        + [pl.BlockSpec((tn, d), lambda i, j: (j, 0))] * nw,
        out_specs=[pl.BlockSpec((tm, tn), lambda i, j: (i, j))] * nw + [pl.BlockSpec((tm, d), lambda i, j: (i, 0))],
        out_shape=[_sds((n, f), out_dtype)] * nw + [_sds((n, d), BF16)],
        compiler_params=_cparams("parallel", "arbitrary"),
    )(x, g, *ws)
    return outs[:nw], outs[nw]


def _ffn_fwd(name, x, g, w1t, w3t, w2, exchange=None, head=None):
    n, d = x.shape
    f = w2.shape[0]
    tm, tn = _pick(n, TILE["ffn_m"], 16), _pick(f, TILE["up_n"], LANES)

    def body(x_ref, g_ref, w1_ref, w3_ref, w2_ref, *refs):
        (gf_ref, t_ref), refs = (refs[:2], refs[2:]) if head else ((None, None), refs)
        o_ref, a_ref, b_ref, h_ref = refs[:4]
        xv = x_ref[...]
        _, xh = _rms_stats(xv)
        h = (xh * g_ref[...]).astype(BF16)
        h_ref[...] = h
        acc = None
        for c0 in range(0, f, tn):
            cols = pl.ds(c0, tn)
            av, bv = _dot_nt(h, w1_ref[cols, :]), _dot_nt(h, w3_ref[cols, :])
            a_ref[:, cols] = av.astype(BF16)
            b_ref[:, cols] = bv.astype(BF16)
            t = _dot((av * _sigmoid(av) * bv).astype(BF16), w2_ref[cols, :])
            acc = t if acc is None else acc + t
        out = xv + FFN_RES * acc
        if head is None:
            o_ref[...] = out
        else:
            loss_ref, dg_ref = refs[4:]

            @pl.when(pl.program_id(0) == 0)
            def _():
                loss_ref[...] = jnp.zeros_like(loss_ref)
                dg_ref[...] = jnp.zeros_like(dg_ref)

            dx, loss, dg = _loss_head_rows(out, gf_ref[...], t_ref[...])
            o_ref[...] = dx
            loss_ref[...] += loss
            dg_ref[...] += dg

    row = pl.BlockSpec((tm, d), lambda i: (i, 0))
    wide = pl.BlockSpec((tm, f), lambda i: (i, 0))
    vec = pl.BlockSpec((1, d), lambda i: (0, 0))
    held = pl.BlockSpec((f, d), lambda i: (0, 0), pipeline_mode=pl.Buffered(1))
    extra_in, extra_out, extra_shape = ([vec, row], [pl.BlockSpec((SUBLANES, LANES), lambda i: (0, 0)), vec],
                                        [_sds((SUBLANES, LANES), F32), _sds((1, d), F32)]) if head else ([], [], [])
    return _call(
        name, body, (n // tm,), [row, vec, held, held, held] + extra_in, [row, wide, wide, row] + extra_out,
        [_sds((n, d), F32), _sds((n, f), BF16), _sds((n, f), BF16), _sds((n, d), BF16)] + extra_shape,
        (x, g, w1t, w3t, w2) + (tuple(head) if head else ()), ("arbitrary",) if head else ("parallel",),
        exchange=exchange)


def _ffn_bwd_hidden(name, dxo, a, b, w2, exchange=None):
    n, d = dxo.shape
    f = a.shape[1]
    tm, tn = _pick(n, TILE["row"], 16), _pick(f, TILE["up_n"], LANES)

    def body(dx_ref, a_ref, b_ref, w_ref, da_ref, db_ref, hid_ref, dxh_ref):
        dxh = (FFN_RES * dx_ref[...]).astype(BF16)
        dxh_ref[...] = dxh
        for c0 in range(0, f, tn):
            cols = pl.ds(c0, tn)
            dhid = _dot_nt(dxh, w_ref[cols, :])
            av, bv = a_ref[:, cols].astype(F32), b_ref[:, cols].astype(F32)
            sig = _sigmoid(av)
            silu = av * sig
            da_ref[:, cols] = (dhid * bv * (sig * (1.0 + av * (1.0 - sig)))).astype(BF16)
            db_ref[:, cols] = (dhid * silu).astype(BF16)
            hid_ref[:, cols] = (silu * bv).astype(BF16)

    wide = pl.BlockSpec((tm, f), lambda i: (i, 0))
    row = pl.BlockSpec((tm, d), lambda i: (i, 0))
    return _call(
        name, body, (n // tm,), [row, wide, wide, pl.BlockSpec((f, d), lambda i: (0, 0))], [wide, wide, wide, row],
        [_sds((n, f), BF16)] * 3 + [_sds((n, d), BF16)], (dxo, a, b, w2), ("parallel",), exchange=exchange)


def _loss_head_rows(x, g, target):
    r, xh = _rms_stats(x)
    err = xh * g - target
    dy = err * (1.0 / x.shape[-1])
    dxh = dy * g
    dx = r * (dxh - xh * jnp.mean(dxh * xh, axis=-1, keepdims=True))
    return dx, 0.5 * jnp.sum(jnp.mean(err * err, axis=-1, keepdims=True)), jnp.sum(dy * xh, axis=0, keepdims=True)


def _dx_rms_bwd(name, pairs, dxo, x, g, exchange=None):
    n, dm = x.shape
    tm = _pick(n, TILE["ffn_m"], 16)
    npair = len(pairs)

    def body(*refs):
        d_refs, w_refs = refs[:npair], refs[npair:2 * npair]
        dxo_ref, x_ref, g_ref, dx_ref, dg_ref = refs[2 * npair:]

        @pl.when(pl.program_id(0) == 0)
        def _():
            dg_ref[...] = jnp.zeros_like(dg_ref)

        dh = None
        for d_ref, w_ref in zip(d_refs, w_refs):
            t = _dot(d_ref[...].astype(BF16), w_ref[...])
            dh = t if dh is None else dh + t
        dx, dg = _rms_bwd(x_ref[...], g_ref[...], dh)
        dx_ref[...] = dxo_ref[...] + dx
        dg_ref[...] += dg

    row = pl.BlockSpec((tm, dm), lambda i: (i, 0))
    d_specs = [pl.BlockSpec((tm, p[1]), functools.partial(lambda i, cb: (i, cb), cb=p[2])) for p in pairs]
    w_specs = [pl.BlockSpec((p[4], dm), functools.partial(lambda i, rb: (rb, 0), rb=p[5]), pipeline_mode=pl.Buffered(1))
               for p in pairs]
    return _call(
        name, body, (n // tm,), d_specs + w_specs + [row, row, pl.BlockSpec((1, dm), lambda i: (0, 0))],
        [row, pl.BlockSpec((1, dm), lambda i: (0, 0))], [_sds((n, dm), F32), _sds((1, dm), F32)],
        (*[p[0] for p in pairs], *[p[3] for p in pairs], dxo, x, g), ("arbitrary",), exchange=exchange)


def _mm_tn(name, a, b, a_cols=None, b_cols=None, exchange=None):
    n = a.shape[0]
    a0, ma = a_cols if a_cols else (0, a.shape[1])
    b0, mb = b_cols if b_cols else (0, b.shape[1])
    assert a0 % ma == 0 and b0 % mb == 0
    ab, bb = a0 // ma, b0 // mb
    tk = _pick(n, TILE["mm_bytes"] // (ma * a.dtype.itemsize + mb * b.dtype.itemsize), 16)

    def body(a_ref, b_ref, o_ref):
        @pl.when(pl.program_id(0) == 0)
        def _():
            o_ref[...] = jnp.zeros_like(o_ref)

        o_ref[...] += _dot_tn(a_ref[...].astype(BF16), b_ref[...].astype(BF16))

    (out,), got = _call(
        name, body, (n // tk,), [pl.BlockSpec((tk, ma), lambda k: (k, ab)), pl.BlockSpec((tk, mb), lambda k: (k, bb))],
        [pl.BlockSpec((ma, mb), lambda k: (0, 0))], [_sds((ma, mb), F32)], (a, b), ("arbitrary",), exchange=exchange)
    return out if exchange is None else (out, got)


def _row_mm(name, pairs, out_w, out_dtype, add=None, exchange=None):
    n = pairs[0][0].shape[0]
    tm = _pick(n, TILE["row"], 16)
    npair = len(pairs)

    def body(*refs):
        a_refs, w_refs = refs[:npair], refs[npair:2 * npair]
        add_ref = refs[2 * npair] if add is not None else None
        o_ref = refs[-1]
        acc = None
        for a_ref, w_ref, p in zip(a_refs, w_refs, pairs):
            av = a_ref[...].astype(BF16)
            t = _dot_nt(av, w_ref[...]) if p[6] else _dot(av, w_ref[...])
            acc = t if acc is None else acc + t
        if add_ref is not None:
            acc = acc + add_ref[...].astype(F32)
        o_ref[...] = acc.astype(o_ref.dtype)

    a_specs = [pl.BlockSpec((tm, p[1]), functools.partial(lambda i, cb: (i, cb), cb=p[2])) for p in pairs]
    w_specs = [pl.BlockSpec((p[4], p[3].shape[1]), functools.partial(lambda i, rb: (rb, 0), rb=p[5])) for p in pairs]
    add_specs = [pl.BlockSpec((tm, out_w), lambda i: (i, 0))] if add is not None else []
    (out,), got = _call(
        name, body, (n // tm,), a_specs + w_specs + add_specs, [pl.BlockSpec((tm, out_w), lambda i: (i, 0))],
        [_sds((n, out_w), out_dtype)],
        (*[p[0] for p in pairs], *[p[3] for p in pairs], *([add] if add is not None else [])), ("parallel",),
        exchange=exchange)
    return out, got


def _conv_post(c, ln_g, ln_b, out_g):
    mu = jnp.mean(c, axis=-1, keepdims=True)
    xc = c - mu
    rstd = lax.rsqrt(jnp.mean(xc * xc, axis=-1, keepdims=True) + EPS)
    nrm = xc * rstd
    l = nrm * ln_g + ln_b
    sig = _sigmoid(l)
    s = l * sig
    r, sh = _rms_stats(s)
    return sh * out_g, (rstd, nrm, l, sig, r, sh)


def _tap_groups(first):
    groups = []
    for r in range(SUBLANES):
        taps = [(s - r, s - first) for s in range(first, first + CONV_WIDTH) if s % SUBLANES == r]
        if taps:
            groups.append((r, taps))
    return groups


def _shift_rows(lo, hi, s):
    row = lax.broadcasted_iota(jnp.int32, lo.shape, 0)
    return pltpu.roll(jnp.where(row >= s, lo, hi), SUBLANES - s, 0)


def _conv_taps(a_ref, w_ref, b_ref, first, rows, flip=False):
    acc = None
    for r, taps in _tap_groups(first):
        ext = rows if r == 0 else rows + SUBLANES
        part = None
        for base, k in taps:
            kk = CONV_WIDTH - 1 - k if flip else k
            t = w_ref[kk:kk + 1, :] * a_ref[pl.ds(base, ext), :]
            part = t if part is None else part + t
        if r:
            b_ref[...] = part
            part = b_ref[pl.ds(r, rows), :]
        acc = part if acc is None else acc + part
    return acc


def _conv_post_bwd(cv, dout, ln_g, ln_b, out_g):
    _, (rstd, nrm, l, sig, r, sh) = _conv_post(cv, ln_g, ln_b, out_g)
    dsh = dout * out_g
    ds = r * (dsh - sh * jnp.mean(dsh * sh, axis=-1, keepdims=True))
    dl = ds * (sig * (1.0 + l * (1.0 - sig)))
    dn = dl * ln_g
    dc = rstd * (dn - jnp.mean(dn, axis=-1, keepdims=True) - nrm * jnp.mean(dn * nrm, axis=-1, keepdims=True))
    col_sum = lambda t: jnp.sum(t, axis=0, keepdims=True)
    return dc, col_sum(dout * sh), col_sum(dl * nrm), col_sum(dl)


def _conv_fwd(name, proj3, conv_w, conv_b, ln_g, ln_b, out_g):
    bsz, seq, _ = proj3.shape
    c = conv_w.shape[1]
    tt = _pick(seq, TILE["conv_t"], CONV_HALO)
    hb = tt // CONV_HALO
    first = CONV_HALO - (CONV_WIDTH - 1)

    def body(v_ref, g_ref, vp_ref, gp_ref, w_ref, cb_ref, lg_ref, lb_ref, og_ref, o_ref, cv_ref, a_ref, b_ref):
        keep = (pl.program_id(1) > 0).astype(F32)
        a_ref[pl.ds(0, CONV_HALO), :] = keep * vp_ref[0] * _sigmoid(gp_ref[0])
        a_ref[pl.ds(CONV_HALO, tt), :] = v_ref[0] * _sigmoid(g_ref[0])
        cv = _conv_taps(a_ref, w_ref, b_ref, first, tt) + cb_ref[...]
        cv_ref[0] = cv
        out, _ = _conv_post(cv, lg_ref[...], lb_ref[...], og_ref[...])
        o_ref[0] = out.astype(BF16)

    vec = pl.BlockSpec((1, c), lambda b, i: (0, 0))
    prev = lambda col: pl.BlockSpec((1, CONV_HALO, c), lambda b, i: (b, jnp.maximum(i * hb - 1, 0), col))
    tile = pl.BlockSpec((1, tt, c), lambda b, i: (b, i, 0))
    return pl.pallas_call(
        body, name=name, grid=(bsz, seq // tt),
        in_specs=[tile, pl.BlockSpec((1, tt, c), lambda b, i: (b, i, 1)),
                  prev(0), prev(1), pl.BlockSpec(conv_w.shape, lambda b, i: (0, 0)), vec, vec, vec, vec],
        out_specs=[tile, tile],
        out_shape=[_sds((bsz, seq, c), BF16), _sds((bsz, seq, c), F32)],
        scratch_shapes=[pltpu.VMEM((CONV_HALO + tt, c), F32), pltpu.VMEM((tt + SUBLANES, c), F32)],
        compiler_params=_cparams("parallel", "arbitrary"),
    )(proj3, proj3, proj3, proj3, conv_w, conv_b, ln_g, ln_b, out_g)


def _conv_bwd(name, dmix3, proj3, cv3, conv_w, ln_g, ln_b, out_g):
    bsz, seq, _ = proj3.shape
    c = conv_w.shape[1]
    tt = _pick(seq, TILE["conv_t"], CONV_HALO)
    hb = tt // CONV_HALO
    nt = seq // tt
    last_hb = seq // CONV_HALO - 1
    ext = tt + CONV_HALO
    first = CONV_HALO - (CONV_WIDTH - 1)

    def body(v_ref, g_ref, vp_ref, gp_ref, cv_ref, cvn_ref, d_ref, dn_ref, w_ref, lg_ref, lb_ref, og_ref,
             o_ref, dw_ref, dcb_ref, dlg_ref, dlb_ref, dog_ref, a_ref, dc_ref, b_ref, dcz_ref):
        i = pl.program_id(1)

        @pl.when((pl.program_id(0) == 0) & (i == 0))
        def _():
            for r in (dw_ref, dcb_ref, dlg_ref, dlb_ref, dog_ref):
                r[...] = jnp.zeros_like(r)

        keep_prev = (i > 0).astype(F32)
        keep_next = (i < nt - 1).astype(F32)
        sig_g = _sigmoid(g_ref[0])
        a_ref[pl.ds(0, CONV_HALO), :] = keep_prev * vp_ref[0] * _sigmoid(gp_ref[0])
        a_ref[pl.ds(CONV_HALO, tt), :] = v_ref[0] * sig_g

        lg, lb, og = lg_ref[...], lb_ref[...], og_ref[...]
        dc_own, d_og, d_lg, d_lb = _conv_post_bwd(cv_ref[0], d_ref[0], lg, lb, og)
        dc_next, _, _, _ = _conv_post_bwd(cvn_ref[0], keep_next * dn_ref[0], lg, lb, og)
        dog_ref[...] += d_og
        dlg_ref[...] += d_lg
        dlb_ref[...] += d_lb
        dcb_ref[...] += jnp.sum(dc_own, axis=0, keepdims=True)
        dc_ref[pl.ds(0, tt), :] = dc_own
        dc_ref[pl.ds(tt, CONV_HALO), :] = dc_next
        dcz_ref[pl.ds(0, SUBLANES), :] = jnp.zeros((SUBLANES, c), F32)
        dcz_ref[pl.ds(SUBLANES, tt), :] = dc_own
        dcz_ref[pl.ds(SUBLANES + tt, SUBLANES), :] = jnp.zeros((SUBLANES, c), F32)

        da = _conv_taps(dc_ref, w_ref, b_ref, 0, tt, flip=True)

        for r, taps in _tap_groups(first):
            def blk(j, accs, r=r, taps=taps):
                off = pl.multiple_of(j * SUBLANES, SUBLANES)
                ds = dcz_ref[pl.ds(off + SUBLANES, SUBLANES), :]
                if r:
                    ds = _shift_rows(dcz_ref[pl.ds(off, SUBLANES), :], ds, SUBLANES - r)
                return tuple(acc + ds * a_ref[pl.ds(off + base, SUBLANES), :] for acc, (base, _) in zip(accs, taps))

            zero = jnp.zeros((SUBLANES, c), F32)
            accs = lax.fori_loop(0, tt // SUBLANES + (1 if r else 0), blk, tuple(zero for _ in taps))
            for acc, (_, k) in zip(accs, taps):
                dw_ref[k:k + 1, :] += jnp.sum(acc, axis=0, keepdims=True)
        val = v_ref[0]
        o_ref[0] = jnp.concatenate([da * sig_g, da * val * sig_g * (1.0 - sig_g)], axis=-1).astype(BF16)

    vec = pl.BlockSpec((1, c), lambda b, i: (0, 0))
    cur = lambda col: pl.BlockSpec((1, tt, c), lambda b, i: (b, i, col))
    prev = lambda col: pl.BlockSpec((1, CONV_HALO, c), lambda b, i: (b, jnp.maximum(i * hb - 1, 0), col))
    nxt = lambda col: pl.BlockSpec((1, CONV_HALO, c), lambda b, i: (b, jnp.minimum((i + 1) * hb, last_hb), col))
    wspec = pl.BlockSpec(conv_w.shape, lambda b, i: (0, 0))
    return pl.pallas_call(
        body, name=name, grid=(bsz, nt),
        in_specs=[cur(0), cur(1), prev(0), prev(1), cur(0), nxt(0), cur(0), nxt(0), wspec, vec, vec, vec],
        out_specs=[pl.BlockSpec((1, tt, 2 * c), lambda b, i: (b, i, 0)), wspec, vec, vec, vec, vec],
        out_shape=[_sds((bsz, seq, 2 * c), BF16), _sds(conv_w.shape, F32)] + [_sds((1, c), F32)] * 4,
        scratch_shapes=[pltpu.VMEM((CONV_HALO + tt, c), F32), pltpu.VMEM((ext, c), F32),
                        pltpu.VMEM((tt + SUBLANES, c), F32), pltpu.VMEM((tt + 2 * SUBLANES, c), F32)],
        compiler_params=_cparams("arbitrary", "arbitrary"),
    )(proj3, proj3, proj3, proj3, cv3, cv3, dmix3, dmix3, conv_w, ln_g, ln_b, out_g)


def _ssm_discretise(a_re, a_im, log_dt):
    dt = jnp.exp(log_dt)
    zr, zi = a_re * dt, a_im * dt
    mag = jnp.exp(zr)
    ar, ai = mag * jnp.cos(zi), mag * jnp.sin(zi)
    den = a_re * a_re + a_im * a_im
    nr = ar - 1.0
    return ar, ai, (nr * a_re + ai * a_im) / den, (ai * a_re - nr * a_im) / den


def _ssm_system(a_re, a_im, log_dt, a_re_x, a_im_x, log_dt_x, bt_re, bt_im):
    ar, ai, _, _ = _ssm_discretise(a_re, a_im, log_dt)
    _, _, cr, ci = _ssm_discretise(a_re_x, a_im_x, log_dt_x)
    return ar, ai, cr * bt_re - ci * bt_im, cr * bt_im + ci * bt_re


def _ssm_prep(name, prim):
    g, p = prim[0].shape

    def body(*refs):
        pwr_ref, pwi_ref, bbr_ref, bbi_ref = refs[8:]
        ar, ai, bbr, bbi = _ssm_system(*[r[...] for r in refs[:8]])
        bbr_ref[...] = bbr
        bbi_ref[...] = bbi
        pr, pi = ar, ai
        for k in range(SUBLANES):
            pwr_ref[k] = pr
            pwi_ref[k] = pi
            pr, pi = pr * ar - pi * ai, pr * ai + pi * ar

    return pl.pallas_call(
        body, name=name,
        out_shape=[_sds((SUBLANES, g, p), F32)] * 2 + [_sds(prim[6].shape, F32)] * 2,
        compiler_params=pltpu.CompilerParams(vmem_limit_bytes=VMEM_LIMIT),
    )(*prim)


def _ssm_param_grads(name, prim, dab_r, dab_i, dbb_r, dbb_i):
    g, p = prim[0].shape
    h = prim[6].shape[0] // g

    def body(*refs):
        dar_ref, dai_ref, dbr_ref, dbi_ref = refs[8:12]
        o_ar, o_ai, o_dt, o_br, o_bi = refs[12:]
        _, vjp = jax.vjp(_ssm_system, *[r[...] for r in refs[:8]])
        ct = (jnp.sum(dar_ref[...], axis=0), jnp.sum(dai_ref[...], axis=0), dbr_ref[...], dbi_ref[...])
        d_ar, d_ai, d_dt, d_arx, d_aix, d_dtx, d_br, d_bi = vjp(ct)
        per_group = lambda t: jnp.sum(t.reshape(g, h, p), axis=1)
        o_ar[...] = d_ar + per_group(d_arx)
        o_ai[...] = d_ai + per_group(d_aix)
        o_dt[...] = d_dt + jnp.sum(per_group(d_dtx), axis=1, keepdims=True)
        o_br[...] = d_br
        o_bi[...] = d_bi

    return pl.pallas_call(
        body, name=name,
        out_shape=[_sds(prim[k].shape, F32) for k in (0, 1, 2, 6, 7)],
        compiler_params=pltpu.CompilerParams(vmem_limit_bytes=VMEM_LIMIT),
    )(*prim, dab_r, dab_i, dbb_r, dbb_i)


def _cfma(xr, xi, cr, ci, sr, si):
    return xr + (cr * sr - ci * si), xi + (cr * si + ci * sr)


def _scan_tables(pw_r, pw_i, reverse):
    gp = pw_r.shape[1] * pw_r.shape[2]
    pr, pi = pw_r.reshape(SUBLANES, gp), pw_i.reshape(SUBLANES, gp)
    if reverse:
        pi = -pi
    row = jnp.arange(SUBLANES)[:, None]
    tabs = []
    for d in (1, 2, 4):
        keep = (row < SUBLANES - d) if reverse else (row >= d)
        tabs += [jnp.where(keep, pr[d - 1][None, :], 0.0), jnp.where(keep, pi[d - 1][None, :], 0.0)]
    tabs += [pr[::-1], pi[::-1]] if reverse else [pr, pi]
    return jnp.concatenate(tabs, axis=0)


MXU_DEPTH = 256


def _bands(c, gp):
    bw = min(c, MXU_DEPTH)
    return c // bw, bw, gp * bw // c


def _band_expand(rows16, w_ref, put, c, gp):
    nb, bw, sw = _bands(c, gp)
    for s in range(nb):
        band = rows16[:, s * bw:(s + 1) * bw]
        for half in (0, gp):
            cols = pl.ds(half + s * sw, sw)
            put(cols, _dot(band, w_ref[pl.ds(s * bw, bw), cols]))


def _band_contract(get16, w_ref, c, gp):
    nb, bw, sw = _bands(c, gp)
    out = []
    for s in range(nb):
        acc = None
        for half in (0, gp):
            cols = pl.ds(half + s * sw, sw)
            t = _dot_nt(get16(cols), w_ref[pl.ds(s * bw, bw), cols])
            acc = t if acc is None else acc + t
        out.append(acc)
    return out[0] if nb == 1 else jnp.concatenate(out, axis=1)


def _band_wgrad(name, a, a_block, c, b):
    n = a.shape[0]
    gp = b.shape[1] // 2
    nb, bw, sw = _bands(c, gp)
    tk = _pick(n, TILE["mm_bytes"] // (c * a.dtype.itemsize + 2 * gp * b.dtype.itemsize), 16)

    def body(a_ref, b_ref, o_ref):
        @pl.when(pl.program_id(0) == 0)
        def _():
            o_ref[...] = jnp.zeros_like(o_ref)

        for s in range(nb):
            band = a_ref[:, s * bw:(s + 1) * bw].astype(BF16)
            for h, half in enumerate((0, gp)):
                o_ref[pl.ds(s * bw, bw), pl.ds(h * sw, sw)] += _dot_tn(
                    band, b_ref[:, pl.ds(half + s * sw, sw)].astype(BF16))

    return pl.pallas_call(
        body, name=name, grid=(n // tk,),
        in_specs=[pl.BlockSpec((tk, c), lambda k: (k, a_block)), pl.BlockSpec((tk, 2 * gp), lambda k: (k, 0))],
        out_specs=pl.BlockSpec((c, 2 * sw), lambda k: (0, 0)),
        out_shape=_sds((c, 2 * sw), F32),
        compiler_params=_cparams("arbitrary"),
    )(a, b)


def _band_diag_take(comp, half, c, gp):
    nb, bw, sw = _bands(c, gp)
    return jnp.concatenate([_block_diag_take(comp[s * bw:(s + 1) * bw, half * sw:(half + 1) * sw], bw // SSM_GROUP)
                            for s in range(nb)], axis=0)


def _scan_fwd(name, tab, proj3, u_block, bbd, cdt):
    bsz, seq, _ = proj3.shape
    c, w = bbd.shape
    gp = w // 2
    tt = _pick(seq, TILE["scan_t"], 16)
    nblk = tt // SUBLANES
    cw = _pick(gp, TILE["scan_w"], LANES)

    def body(tab_ref, u_ref, bbd_ref, cdt_ref, xs_ref, xs16_ref, y_ref, carry_ref, bu_ref):
        @pl.when(pl.program_id(1) == 0)
        def _():
            carry_ref[...] = jnp.zeros_like(carry_ref)

        def put_bu(cols, val):
            bu_ref[0, :, cols] = val

        _band_expand(u_ref[0].astype(BF16), bbd_ref, put_bu, c, gp)

        for ch in range(gp // cw):
            re, im = pl.ds(ch * cw, cw), pl.ds(gp + ch * cw, cw)

            def blk(r, carry, re=re, im=im):
                tabs = [tab_ref[pl.ds(SUBLANES * k, SUBLANES), re] for k in range(8)]
                rows = pl.ds(pl.multiple_of(r * SUBLANES, SUBLANES), SUBLANES)
                xr, xi = bu_ref[0, rows, re], bu_ref[0, rows, im]
                for j, d in enumerate((1, 2, 4)):
                    xr, xi = _cfma(xr, xi, tabs[2 * j], tabs[2 * j + 1], pltpu.roll(xr, d, 0), pltpu.roll(xi, d, 0))
                xr, xi = _cfma(xr, xi, tabs[6], tabs[7], carry[0], carry[1])
                xs_ref[0, rows, re] = xr
                xs_ref[0, rows, im] = xi
                last = SUBLANES - 1
                return (jnp.broadcast_to(xr[last:, :], xr.shape), jnp.broadcast_to(xi[last:, :], xi.shape))

            cr, ci = lax.fori_loop(0, nblk, blk, (carry_ref[:, re], carry_ref[:, im]))
            carry_ref[:, re] = cr
            carry_ref[:, im] = ci

        xs16_ref[0] = xs_ref[0].astype(BF16)
        y_ref[0] = _band_contract(lambda cols: xs16_ref[0, :, cols], cdt_ref, c, gp)

    whole = lambda arr: pl.BlockSpec(arr.shape, lambda b, t: (0, 0))
    wide = pl.BlockSpec((1, tt, w), lambda b, t: (b, t, 0))
    return pl.pallas_call(
        body, name=name, grid=(bsz, seq // tt),
        in_specs=[whole(tab), pl.BlockSpec((1, tt, c), lambda b, t: (b, t, u_block)), whole(bbd), whole(cdt)],
        out_specs=[wide, wide, pl.BlockSpec((1, tt, c), lambda b, t: (b, t, 0))],
        out_shape=[_sds((bsz, seq, w), F32), _sds((bsz, seq, w), BF16), _sds((bsz, seq, c), F32)],
        scratch_shapes=[pltpu.VMEM((SUBLANES, w), F32), pltpu.VMEM((1, tt, w), F32)],
        compiler_params=_cparams("arbitrary", "arbitrary"),
    )(tab, proj3, bbd, cdt)


def _scan_bwd(name, tab, dy3, xs3, du_skip3, bbd, cdt, exchange=None):
    bsz, seq, w = xs3.shape
    c = bbd.shape[0]
    gp = w // 2
    tt = _pick(seq, TILE["scan_t"], 16)
    nblk = tt // SUBLANES
    cw = _pick(gp, TILE["scan_w"], LANES)
    nt = seq // tt

    def body(tab_ref, dy_ref, xs_ref, halo_ref, skip_ref, bbd_ref, cdt_ref, lam16_ref, du_ref, dar_ref, dai_ref,
             carry_ref, g_ref, lam_ref):
        t = pl.program_id(1)

        @pl.when(t == 0)
        def _():
            carry_ref[...] = jnp.zeros_like(carry_ref)

        @pl.when((pl.program_id(0) == 0) & (t == 0))
        def _():
            dar_ref[...] = jnp.zeros_like(dar_ref)
            dai_ref[...] = jnp.zeros_like(dai_ref)

        def put_g(cols, val):
            g_ref[0, :, cols] = val

        _band_expand(dy_ref[0], cdt_ref, put_g, c, gp)

        has_prev = (t < nt - 1).astype(F32)
        row0 = lax.broadcasted_iota(jnp.int32, (SUBLANES, cw), 0) == 0
        last = SUBLANES - 1

        for ch in range(gp // cw):
            re, im = pl.ds(ch * cw, cw), pl.ds(gp + ch * cw, cw)

            def step(rows, xm1r, xm1i, state, re=re, im=im):
                tabs = [tab_ref[pl.ds(SUBLANES * k, SUBLANES), re] for k in range(8)]
                cr, ci, accr, acci = state
                lr, li = g_ref[0, rows, re], g_ref[0, rows, im]
                for j, d in enumerate((1, 2, 4)):
                    lr, li = _cfma(lr, li, tabs[2 * j], tabs[2 * j + 1],
                                   pltpu.roll(lr, SUBLANES - d, 0), pltpu.roll(li, SUBLANES - d, 0))
                lr, li = _cfma(lr, li, tabs[6], tabs[7], cr, ci)
                lam_ref[0, rows, re] = lr
                lam_ref[0, rows, im] = li
                xr, xi = xs_ref[0, rows, re], xs_ref[0, rows, im]
                xpr = jnp.where(row0, jnp.broadcast_to(xm1r[last:, :], xr.shape), pltpu.roll(xr, 1, 0))
                xpi = jnp.where(row0, jnp.broadcast_to(xm1i[last:, :], xi.shape), pltpu.roll(xi, 1, 0))
                accr = accr + (lr * xpr + li * xpi)
                acci = acci + (li * xpr - lr * xpi)
                return (jnp.broadcast_to(lr[:1, :], lr.shape), jnp.broadcast_to(li[:1, :], li.shape), accr, acci)

            def blk(k, state, re=re, im=im, step=step):
                r = nblk - 1 - k
                rows = pl.ds(pl.multiple_of(r * SUBLANES, SUBLANES), SUBLANES)
                prev = pl.ds(pl.multiple_of((r - 1) * SUBLANES, SUBLANES), SUBLANES)
                return step(rows, xs_ref[0, prev, re], xs_ref[0, prev, im], state)

            zero = jnp.zeros((SUBLANES, cw), F32)
            state = lax.fori_loop(0, nblk - 1, blk, (carry_ref[:, re], carry_ref[:, im], zero, zero))
            cr, ci, accr, acci = step(pl.ds(0, SUBLANES), has_prev * halo_ref[0, :, re], has_prev * halo_ref[0, :, im], state)
            carry_ref[:, re] = cr
            carry_ref[:, im] = ci
            dar_ref[:, re] += accr
            dai_ref[:, re] += acci

        lam16_ref[0] = lam_ref[0].astype(BF16)
        du = _band_contract(lambda cols: lam16_ref[0, :, cols], bbd_ref, c, gp)
        du_ref[0] = (du + skip_ref[0]).astype(BF16)

    tile = pl.BlockSpec((1, tt, w), lambda b, t: (b, nt - 1 - t, 0))
    thin = pl.BlockSpec((1, tt, c), lambda b, t: (b, nt - 1 - t, 0))
    halo = pl.BlockSpec((1, SUBLANES, w), lambda b, t: (b, jnp.maximum((nt - 1 - t) * nblk - 1, 0), 0))
    acc = pl.BlockSpec((SUBLANES, gp), lambda b, t: (0, 0))
    whole = lambda arr: pl.BlockSpec(arr.shape, lambda b, t: (0, 0))
    return _call(
        name, body, (bsz, nt), [whole(tab), thin, tile, halo, thin, whole(bbd), whole(cdt)], [tile, thin, acc, acc],
        [_sds(xs3.shape, BF16), _sds((bsz, seq, c), BF16), _sds((SUBLANES, gp), F32), _sds((SUBLANES, gp), F32)],
        (tab, dy3, xs3, xs3, du_skip3, bbd, cdt), ("arbitrary", "arbitrary"),
        scratch=[pltpu.VMEM((SUBLANES, w), F32), pltpu.VMEM((1, tt, w), F32), pltpu.VMEM((1, tt, w), F32)],
        exchange=exchange)


def _gelu_parts(y):
    inner = _GELU_K * (y + _GELU_C * y * y * y)
    t = jnp.tanh(inner)
    return 0.5 * y * (1.0 + t), t


def _ssm_out_fwd(name, cx, proj, u_block, d_skip, glu_w, glu_b, out_g):
    n, c = cx.shape
    tm = _pick(n, TILE["row"], 16)

    def body(cx_ref, u_ref, d_ref, gw_ref, gb_ref, og_ref, y_ref, o_ref):
        y = cx_ref[...] + d_ref[...] * u_ref[...]
        y_ref[...] = y
        gy, _ = _gelu_parts(y)
        z = _dot(gy.astype(BF16), gw_ref[...]) + gb_ref[...]
        _, sh = _rms_stats(gy * _sigmoid(z))
        o_ref[...] = (sh * og_ref[...]).astype(BF16)

    vec = pl.BlockSpec((1, c), lambda i: (0, 0))
    row = pl.BlockSpec((tm, c), lambda i: (i, 0))
    return pl.pallas_call(
        body, name=name, grid=(n // tm,),
        in_specs=[row, pl.BlockSpec((tm, c), lambda i: (i, u_block)), vec, pl.BlockSpec(glu_w.shape, lambda i: (0, 0)),
                  vec, vec],
        out_specs=[row, row],
        out_shape=[_sds((n, c), F32), _sds((n, c), BF16)],
        compiler_params=_cparams("parallel"),
    )(cx, proj, d_skip, glu_w, glu_b, out_g)


def _ssm_out_bwd(name, dmix, d_block, y, proj, u_block, d_skip, glu_w, glu_b, out_g):
    n, c = y.shape
    tm = _pick(n, TILE["row"], 16)

    def body(d_ref, y_ref, u_ref, dk_ref, gw_ref, gb_ref, og_ref, dy_ref, du_ref, dgw_ref, dgb_ref, dog_ref, dd_ref):
        @pl.when(pl.program_id(0) == 0)
        def _():
            for r in (dgw_ref, dgb_ref, dog_ref, dd_ref):
                r[...] = jnp.zeros_like(r)

        yv = y_ref[...]
        gy, th = _gelu_parts(yv)
        gy16 = gy.astype(BF16)
        sz = _sigmoid(_dot(gy16, gw_ref[...]) + gb_ref[...])
        r, sh = _rms_stats(gy * sz)
        dout = d_ref[...]
        dog_ref[...] += jnp.sum(dout * sh, axis=0, keepdims=True)
        dsh = dout * og_ref[...]
        ds = r * (dsh - sh * jnp.mean(dsh * sh, axis=-1, keepdims=True))
        dz = ds * gy * sz * (1.0 - sz)
        dz16 = dz.astype(BF16)
        dgb_ref[...] += jnp.sum(dz, axis=0, keepdims=True)
        dgw_ref[...] += _dot_tn(gy16, dz16)
        dgy = ds * sz + _dot_nt(dz16, gw_ref[...])
        dgelu = 0.5 * (1.0 + th) + 0.5 * yv * (1.0 - th * th) * (_GELU_K * (1.0 + 3.0 * _GELU_C * yv * yv))
        dy = dgy * dgelu
        dy_ref[...] = dy.astype(BF16)
        du_ref[...] = dy * dk_ref[...]
        dd_ref[...] += jnp.sum(dy * u_ref[...], axis=0, keepdims=True)

    vec = pl.BlockSpec((1, c), lambda i: (0, 0))
    row = pl.BlockSpec((tm, c), lambda i: (i, 0))
    mat = pl.BlockSpec(glu_w.shape, lambda i: (0, 0))
    return pl.pallas_call(
        body, name=name, grid=(n // tm,),
        in_specs=[pl.BlockSpec((tm, c), lambda i: (i, d_block)), row, pl.BlockSpec((tm, c), lambda i: (i, u_block)),
                  vec, mat, vec, vec],
        out_specs=[row, row, mat, vec, vec, vec],
        out_shape=[_sds((n, c), BF16), _sds((n, c), F32), _sds(glu_w.shape, F32)] + [_sds((1, c), F32)] * 3,
        compiler_params=_cparams("arbitrary"),
    )(dmix, y, proj, d_skip, glu_w, glu_b, out_g)


def _mesh_pos():
    return tuple(lax.axis_index(a) for a in MESH_AXES)


def _other_chips(x, y):
    return [(1 - x, y), (x, 1 - y), (1 - x, 1 - y)]


def _remote(src, dst, send_sem, recv_sem, dev):
    return pltpu.make_async_remote_copy(src_ref=src, dst_ref=dst, send_sem=send_sem, recv_sem=recv_sem,
                                        device_id=dev, device_id_type=pl.DeviceIdType.MESH)


def _hbm_call(name, body, operands, out_shapes, scratch):
    hbm = pl.BlockSpec(memory_space=pltpu.HBM)
    return pl.pallas_call(body, name=name, in_specs=[hbm] * len(operands), out_specs=[hbm] * len(out_shapes),
                          out_shape=out_shapes, scratch_shapes=scratch)(*operands)


_Exchange = collections.namedtuple("_Exchange", "operands out_shapes scratch start finish")


def _run_exchange(name, plan):
    nin, nout = len(plan.operands), len(plan.out_shapes)

    def body(*refs):
        parts = refs[:nin], refs[nin:nin + nout], refs[nin + nout:]
        plan.start(*parts)
        plan.finish(*parts)

    return _hbm_call(name, body, plan.operands, plan.out_shapes, plan.scratch)


def _gather_plan(blocks):
    nop = len(blocks)

    def copies(x_refs, o_refs, sems):
        send_sems, recv_sems, local_sems = sems
        x, y, c = _mesh_pos()
        me, sibling = (x, y, c), (x, y, 1 - c)
        chips = _other_chips(x, y)

        def copy(i, k, block_of, to, src=None):
            dst = o_refs[i].at[4 * block_of[0] + 2 * block_of[1] + block_of[2]]
            return _remote(dst if src is None else src, dst, send_sems.at[i, k], recv_sems.at[i, k], to)

        own = [pltpu.make_async_copy(x_refs[i], o_refs[i].at[4 * x + 2 * y + c], local_sems.at[i]) for i in range(nop)]
        first = []
        for i in range(nop):
            first.append(copy(i, 0, me, sibling, src=x_refs[i]))
            first += [copy(i, 1 + j, me, (*chip, c), src=x_refs[i]) for j, chip in enumerate(chips)]
        return copy, own, first, me, sibling, chips, c

    def start(x_refs, o_refs, sems):
        _, own, first, *_ = copies(x_refs, o_refs, sems)
        for cp in own + first:
            cp.start()

    def finish(x_refs, o_refs, sems):
        copy, own, first, me, sibling, chips, c = copies(x_refs, o_refs, sems)
        passed = []
        for i in range(nop):
            for j, chip in enumerate(chips):
                copy(i, 1 + j, (*chip, c), me).wait_recv()
                passed.append(copy(i, 4 + j, (*chip, c), sibling))
                passed[-1].start()
        for i in range(nop):
            copy(i, 0, sibling, me).wait_recv()
            for j, chip in enumerate(chips):
                copy(i, 4 + j, (*chip, 1 - c), me).wait_recv()
        for cp in first + passed:
            cp.wait_send()
        for cp in own:
            cp.wait()

    return _Exchange(list(blocks), [_sds((N_DEV,) + b.shape, b.dtype) for b in blocks],
                     [pltpu.SemaphoreType.DMA((nop, N_DEV - 1)), pltpu.SemaphoreType.DMA((nop, N_DEV - 1)),
                      pltpu.SemaphoreType.DMA((nop,))], start, finish)


def _core_exchange_plan(grads):
    nop = len(grads)

    def copies(x_refs, o_refs, sems):
        send_sems, recv_sems = sems
        x, y, c = _mesh_pos()
        return [_remote(x_refs[i].at[2 * q + (1 - c)], o_refs[i].at[q], send_sems.at[i, q], recv_sems.at[i, q],
                        (x, y, 1 - c)) for i in range(nop) for q in range(N_DEV // 2)]

    def start(x_refs, o_refs, sems):
        for cp in copies(x_refs, o_refs, sems):
            cp.start()

    def finish(x_refs, o_refs, sems):
        for cp in copies(x_refs, o_refs, sems):
            cp.wait()

    return _Exchange(list(grads), [_sds((N_DEV // 2,) + g.shape[1:], g.dtype) for g in grads],
                     [pltpu.SemaphoreType.DMA((nop, N_DEV // 2)), pltpu.SemaphoreType.DMA((nop, N_DEV // 2))],
                     start, finish)


def _pair_sum(name, grad, other):
    nchip, _, r, c = grad.shape
    tr = _pick(r, max(SUBLANES, TILE["sum_bytes"] // (4 * c)), SUBLANES)
    core = lax.axis_index("c").astype(jnp.int32).reshape(1)

    def body(core_ref, g_ref, o_ref, s_ref):
        s_ref[0] = (g_ref[0, 0] + o_ref[0]).astype(s_ref.dtype)

    tile = pl.BlockSpec((1, tr, c), lambda q, t, core_ref: (q, t, 0))
    return pl.pallas_call(
        body, name=name,
        grid_spec=pltpu.PrefetchScalarGridSpec(
            num_scalar_prefetch=1, grid=(nchip, r // tr),
            in_specs=[pl.BlockSpec((1, 1, tr, c), lambda q, t, core_ref: (q, core_ref[0], t, 0)), tile],
            out_specs=tile),
        out_shape=_sds((nchip, r, c), BF16),
        compiler_params=_cparams("parallel", "parallel"),
    )(core, grad, other)


def _chip_exchange_plan(sums):
    nop = len(sums)

    def copies(x_refs, o_refs, sems, arriving):
        send_sems, recv_sems, local_sems = sems
        x, y, c = _mesh_pos()
        mine = 2 * x + y
        out = []
        for i in range(nop):
            for j, (px, py) in enumerate(_other_chips(x, y)):
                theirs = 2 * px + py
                src, dst = (mine, theirs) if arriving else (theirs, mine)
                out.append(_remote(x_refs[i].at[src], o_refs[i].at[dst], send_sems.at[i, j], recv_sems.at[i, j],
                                   (px, py, c)))
        if not arriving:
            out += [pltpu.make_async_copy(x_refs[i].at[mine], o_refs[i].at[mine], local_sems.at[i]) for i in range(nop)]
        return out

    def start(x_refs, o_refs, sems):
        for cp in copies(x_refs, o_refs, sems, False):
            cp.start()

    def finish(x_refs, o_refs, sems):
        for cp in copies(x_refs, o_refs, sems, True):
            cp.wait_recv()
        mine = copies(x_refs, o_refs, sems, False)
        for cp in mine[:3 * nop]:
            cp.wait_send()
        for cp in mine[3 * nop:]:
            cp.wait()

    return _Exchange(list(sums), [_sds(s.shape, s.dtype) for s in sums],
                     [pltpu.SemaphoreType.DMA((nop, 3)), pltpu.SemaphoreType.DMA((nop, 3)), pltpu.SemaphoreType.DMA((nop,))],
                     start, finish)


def _part_rows(npart, r, c):
    return _pick(r, max(SUBLANES, TILE["sum_bytes"] // (4 * npart * c)), SUBLANES)


def _sum_slots(p_ref):
    g = p_ref[0].astype(F32)
    for k in range(1, p_ref.shape[0]):
        g = g + p_ref[k].astype(F32)
    return g


def _adamw_step(g, w, m, v):
    c1 = 1.0 - ADAM_B1 ** ADAM_STEP
    c2 = 1.0 - ADAM_B2 ** ADAM_STEP
    nm = ADAM_B1 * m + (1.0 - ADAM_B1) * g
    nv = ADAM_B2 * v + (1.0 - ADAM_B2) * (g * g)
    return -ADAM_LR * ((nm / c1) / (jnp.sqrt(nv / c2) + ADAM_EPS) + ADAM_WD * w), nm, nv


def _adamw_small(name, parts, ws, ms, vs):
    nparam, nall = len(ws), len(parts)

    def body(*refs):
        p_refs = refs[:nall]
        w_refs, m_refs, v_refs = (refs[nall + k * nparam:nall + (k + 1) * nparam] for k in range(3))
        outs = refs[nall + 3 * nparam:]
        for p in range(nall):
            g = _sum_slots(p_refs[p])
            if p < nparam:
                delta, nm, nv = _adamw_step(g, w_refs[p][...], m_refs[p][...], v_refs[p][...])
                for o_ref, val in zip(outs[4 * p:4 * p + 4], (g, delta, nm, nv)):
                    o_ref[...] = val
            else:
                outs[4 * nparam + p - nparam][...] = g

    shapes = [_sds(w.shape, F32) for w in ws for _ in range(4)] + [_sds(p.shape[1:], F32) for p in parts[nparam:]]
    res = pl.pallas_call(body, name=name, out_shape=shapes,
                         compiler_params=pltpu.CompilerParams(vmem_limit_bytes=VMEM_LIMIT))(*parts, *ws, *ms, *vs)
    return [res[4 * p:4 * p + 4] for p in range(nparam)] + [[r] for r in res[4 * nparam:]]


def _adamw(name, parts, w, m, v):
    npart, r, c = parts.shape
    lead = len(w.shape) - 2
    tr = _part_rows(npart, r, c)
    at = (0,) * lead + (slice(None), slice(None))

    def body(p_ref, w_ref, m_ref, v_ref, g_ref, d_ref, nm_ref, nv_ref):
        g = _sum_slots(p_ref)
        delta, nm, nv = _adamw_step(g, w_ref[at], m_ref[at], v_ref[at])
        g_ref[at] = g
        nm_ref[at] = nm
        nv_ref[at] = nv
        d_ref[at] = delta

    row = pl.BlockSpec((1,) * lead + (tr, c), lambda i: (0,) * lead + (i, 0))
    return pl.pallas_call(
        body, name=name, grid=(r // tr,),
        in_specs=[pl.BlockSpec((npart, tr, c), lambda i: (0, i, 0)), row, row, row],
        out_specs=[row] * 4,
        out_shape=[_sds(w.shape, F32)] * 4,
        compiler_params=_cparams("parallel"),
    )(parts, w, m, v)


def _block_diag(rows_gh, groups):
    gh, p = rows_gh.shape
    own = (jnp.arange(gh)[:, None] // (gh // groups) == jnp.arange(groups)[None, :]).astype(rows_gh.dtype)
    return (own[:, :, None] * rows_gh[:, None, :]).reshape(gh, groups * p)


def _block_diag_take(dense, groups):
    gh = dense.shape[0]
    p = dense.shape[1] // groups
    own = (jnp.arange(gh)[:, None] // (gh // groups) == jnp.arange(groups)[None, :]).astype(dense.dtype)
    return jnp.sum(dense.reshape(gh, groups, p) * own[:, :, None], axis=1)


FFN1 = ("ffn1_w1", "ffn1_w3", "ffn1_w2")
MIXER = ("w_in", "ssm_glu_w", "w_out")
FFN2 = ("ffn2_w1", "ffn2_w3", "ffn2_w2")
BIG = FFN1 + MIXER + FFN2
COL_SHARDED = ("ffn1_w1", "ffn1_w3", "w_in", "ffn2_w1", "ffn2_w3", "conv_w")
SMALL = ("norm_ffn1", "norm_mix", "conv_b", "conv_ln_g", "conv_ln_b", "conv_out_g", "ssm_A_re", "ssm_A_im",
         "ssm_log_dt", "ssm_B_re", "ssm_B_im", "ssm_C_re", "ssm_C_im", "ssm_D", "ssm_glu_b", "ssm_out_g",
         "norm_ffn2", "norm_final")
WEIGHTS = ("norm_ffn1", "ffn1_w1", "ffn1_w3", "ffn1_w2", "norm_mix", "w_in", "conv_w", "conv_b", "conv_ln_g",
           "conv_ln_b", "conv_out_g", "ssm_A_re", "ssm_A_im", "ssm_log_dt", "ssm_B_re", "ssm_B_im", "ssm_C_re",
           "ssm_C_im", "ssm_D", "ssm_glu_w", "ssm_glu_b", "ssm_out_g", "w_out", "norm_ffn2", "ffn2_w1", "ffn2_w3",
           "ffn2_w2", "norm_final")


def _ffn_backward(tag, dxo, x, g, w1, w3, w2, saved, exchange=None, dw_exchange=None, reduce_plan=None):
    a, b, h = saved
    (da, db, hid, dxh), got = _ffn_bwd_hidden(tag + "_bwd_hidden", dxo, a, b, w2, exchange=exchange)
    dw1, dw_got = _mm_tn(tag + "_dw1", da, h, exchange=dw_exchange) if dw_exchange else (_mm_tn(tag + "_dw1", da, h), None)
    dws = [dw1, _mm_tn(tag + "_dw3", db, h), _mm_tn(tag + "_dw2", hid, dxh)]
    f = a.shape[1]
    (dx, dg), reduced = _dx_rms_bwd(tag + "_bwd_dx", [(da, f, 0, w1, f, 0), (db, f, 0, w3, f, 0)], dxo, x, g,
                                    exchange=reduce_plan(dws) if reduce_plan else None)
    return (dx, dg, dws), got, dw_got, reduced


def _reduce_in_chip(names, grads):
    send = [g.reshape((N_DEV, -1) + g.shape[1:]) for g in grads]

    def then(from_core):
        return _chip_exchange_plan([_pair_sum("pair_sum_" + k, s.reshape((N_DEV // 2, 2) + s.shape[1:]), o)
                                    for k, s, o in zip(names, send, from_core)])

    return _core_exchange_plan(send), then


def kernel(x, norm_ffn1, ffn1_w1, ffn1_w3, ffn1_w2, norm_mix, w_in, conv_w, conv_b, conv_ln_g, conv_ln_b, conv_out_g, ssm_A_re, ssm_A_im, ssm_log_dt, ssm_B_re, ssm_B_im, ssm_C_re, ssm_C_im, ssm_D, ssm_glu_w, ssm_glu_b, ssm_out_g, w_out, norm_ffn2, ffn2_w1, ffn2_w3, ffn2_w2, norm_final, loss_target, m_norm_ffn1, m_ffn1_w1, m_ffn1_w3, m_ffn1_w2, m_norm_mix, m_w_in, m_conv_w, m_conv_b, m_conv_ln_g, m_conv_ln_b, m_conv_out_g, m_ssm_A_re, m_ssm_A_im, m_ssm_log_dt, m_ssm_B_re, m_ssm_B_im, m_ssm_C_re, m_ssm_C_im, m_ssm_D, m_ssm_glu_w, m_ssm_glu_b, m_ssm_out_g, m_w_out, m_norm_ffn2, m_ffn2_w1, m_ffn2_w3, m_ffn2_w2, m_norm_final, v_norm_ffn1, v_ffn1_w1, v_ffn1_w3, v_ffn1_w2, v_norm_mix, v_w_in, v_conv_w, v_conv_b, v_conv_ln_g, v_conv_ln_b, v_conv_out_g, v_ssm_A_re, v_ssm_A_im, v_ssm_log_dt, v_ssm_B_re, v_ssm_B_im, v_ssm_C_re, v_ssm_C_im, v_ssm_D, v_ssm_glu_w, v_ssm_glu_b, v_ssm_out_g, v_w_out, v_norm_ffn2, v_ffn2_w1, v_ffn2_w3, v_ffn2_w2, v_norm_final):
    args = dict(locals())
    wt = {n: args[n] for n in WEIGHTS}
    mom = {n: args["m_" + n] for n in WEIGHTS}
    var = {n: args["v_" + n] for n in WEIGHTS}

    bsz, seq, d = x.shape
    n = bsz * seq
    c = conv_b.shape[-1]
    groups = c // SSM_GROUP
    gp = groups * SSM_STATE
    u_b = 2

    shard = {k: (wt[k][0].T if k in COL_SHARDED else wt[k][0]).astype(BF16) for k in BIG}
    gathered = _run_exchange("gather_weights_ffn1", _gather_plan([shard[k] for k in FFN1]))
    full = {k: g.reshape(-1, g.shape[-1]) for k, g in zip(FFN1, gathered)}
    gather_rest = _gather_plan([shard[k] for k in MIXER + FFN2] + [wt["conv_w"][0]])

    vec = lambda k: wt[k].reshape(1, -1)
    g_ffn1, g_mix, g_ffn2, g_fin = vec("norm_ffn1"), vec("norm_mix"), vec("norm_ffn2"), vec("norm_final")
    cb, lng, lnb, cog = vec("conv_b"), vec("conv_ln_g"), vec("conv_ln_b"), vec("conv_out_g")
    d_skip, glu_b, sog = vec("ssm_D"), vec("ssm_glu_b"), vec("ssm_out_g")

    a_re, a_im = wt["ssm_A_re"][0], wt["ssm_A_im"][0]
    log_dt = wt["ssm_log_dt"][0].reshape(groups, 1)
    bt_re = wt["ssm_B_re"][0].transpose(0, 2, 1).reshape(groups * SSM_GROUP, SSM_STATE)
    bt_im = wt["ssm_B_im"][0].transpose(0, 2, 1).reshape(groups * SSM_GROUP, SSM_STATE)
    c_re = wt["ssm_C_re"][0].reshape(groups * SSM_GROUP, SSM_STATE)
    c_im = wt["ssm_C_im"][0].reshape(groups * SSM_GROUP, SSM_STATE)
    per_chan = lambda t: jnp.repeat(t, SSM_GROUP, axis=0)
    ssm_prim = (a_re, a_im, log_dt, per_chan(a_re), per_chan(a_im), per_chan(jnp.broadcast_to(log_dt, a_re.shape)),
                bt_re, bt_im)
    pw_r, pw_i, bb_r, bb_i = _ssm_prep("ssm_prep", ssm_prim)
    tab_f = _scan_tables(pw_r, pw_i, False)
    tab_b = _scan_tables(pw_r, pw_i, True)
    bbd = jnp.concatenate([_block_diag(bb_r, groups), _block_diag(bb_i, groups)], axis=1).astype(BF16)
    cdt = jnp.concatenate([_block_diag(c_re, groups), -_block_diag(c_im, groups)], axis=1).astype(BF16)

    x0 = x.reshape(n, d)
    (x1, *ffn1_saved), gathered = _ffn_fwd("ffn1_fwd", x0, g_ffn1, full["ffn1_w1"], full["ffn1_w3"], full["ffn1_w2"],
                                           exchange=gather_rest)
    full.update({k: g.reshape(-1, g.shape[-1]) for k, g in zip(MIXER + FFN2, gathered)})
    conv_w_full = gathered[-1].transpose(1, 0, 2).reshape(CONV_WIDTH, c)
    conv_w_pad = jnp.pad(conv_w_full, ((0, CONV_HALO - CONV_WIDTH), (0, 0)))
    (proj,), h2 = _rms_mm("mix_in", x1, g_mix, [full["w_in"]], F32)
    proj3 = proj.reshape(bsz, seq, 3 * c)
    an3, cv3 = _conv_fwd("conv_fwd", proj3, conv_w_pad, cb, lng, lnb, cog)
    an = an3.reshape(n, c)
    xs3, xs16, cx3 = _scan_fwd("scan_fwd", tab_f, proj3, u_b, bbd, cdt)
    y, sn = _ssm_out_fwd("ssm_out_fwd", cx3.reshape(n, c), proj, u_b, d_skip, full["ssm_glu_w"], glu_b, sog)
    w_o = full["w_out"]
    x2, _ = _row_mm("mix_out", [(an, c, 0, w_o, c, 0, False), (sn, c, 0, w_o, c, 1, False)], d, F32, add=x1)
    (dx3, *ffn2_saved, loss_tile, d_gfin), _ = _ffn_fwd(
        "ffn2_fwd", x2, g_ffn2, full["ffn2_w1"], full["ffn2_w3"], full["ffn2_w2"],
        head=(g_fin, loss_target.reshape(n, d)))
    loss = lax.psum(loss_tile[0, 0], MESH_AXES)

    grads, from_chips = {}, {}
    (dx2, grads["norm_ffn2"], dws), _, _, _ = _ffn_backward(
        "ffn2", dx3, x2, g_ffn2, full["ffn2_w1"], full["ffn2_w3"], full["ffn2_w2"], ffn2_saved)
    in_chip, across_chips = _reduce_in_chip(FFN2, dws)

    dmix, got = _row_mm("mix_out_bwd", [(dx2, d, 0, w_o, 2 * c, 0, True)], 2 * c, F32, exchange=in_chip)
    reduce_ffn2 = across_chips(got)
    grads["w_out"] = jnp.concatenate([_mm_tn("dw_out_a", an, dx2), _mm_tn("dw_out_s", sn, dx2)], axis=0)

    dy, du_skip, grads["ssm_glu_w"], grads["ssm_glu_b"], grads["ssm_out_g"], grads["ssm_D"] = _ssm_out_bwd(
        "ssm_out_bwd", dmix, 1, y, proj, u_b, d_skip, full["ssm_glu_w"], glu_b, sog)
    (lam3, du3, dab_r, dab_i), got = _scan_bwd("scan_bwd", tab_b, dy.reshape(bsz, seq, c), xs3,
                                               du_skip.reshape(bsz, seq, c), bbd, cdt, exchange=reduce_ffn2)
    from_chips.update(zip(FFN2, got))
    lam, du = lam3.reshape(n, 2 * gp), du3.reshape(n, c)
    d_bbd = _band_wgrad("ssm_dbb", proj, u_b, c, lam)
    d_cdt = _band_wgrad("ssm_dc", dy, 0, c, xs16.reshape(n, 2 * gp))
    d_are, d_aim, d_ldt, d_btr, d_bti = _ssm_param_grads(
        "ssm_param_grads", ssm_prim,
        dab_r.reshape(SUBLANES, groups, SSM_STATE), dab_i.reshape(SUBLANES, groups, SSM_STATE),
        _band_diag_take(d_bbd, 0, c, gp), _band_diag_take(d_bbd, 1, c, gp))
    grads["ssm_A_re"], grads["ssm_A_im"], grads["ssm_log_dt"] = d_are, d_aim, d_ldt
    grads["ssm_B_re"], grads["ssm_B_im"] = d_btr, d_bti
    grads["ssm_C_re"] = _band_diag_take(d_cdt, 0, c, gp)
    grads["ssm_C_im"] = -_band_diag_take(d_cdt, 1, c, gp)

    dconv3, d_cw, grads["conv_b"], grads["conv_ln_g"], grads["conv_ln_b"], grads["conv_out_g"] = _conv_bwd(
        "conv_bwd", dmix.reshape(bsz, seq, 2 * c), proj3, cv3, conv_w_pad, lng, lnb, cog)
    dconv = dconv3.reshape(n, 2 * c)
    grads["conv_w"] = d_cw[:CONV_WIDTH]
    grads["w_in"] = jnp.concatenate([_mm_tn("dw_in_conv", dconv, h2), _mm_tn("dw_in_ssm", du, h2)], axis=0)
    w_i = full["w_in"]
    in_chip, across_chips = _reduce_in_chip(MIXER, [grads[k] for k in MIXER])
    (dx1, grads["norm_mix"]), got = _dx_rms_bwd("mix_in_bwd", [(dconv, 2 * c, 0, w_i, 2 * c, 0), (du, c, 0, w_i, c, 2)],
                                                dx2, x1, g_mix, exchange=in_chip)
    reduce_mixer = across_chips(got)

    def reduce_ffn1(dws):
        in_chip, across_chips = _reduce_in_chip(FFN1, dws)
        return across_chips(_run_exchange("exchange_core_ffn1", in_chip))

    grads["norm_final"] = d_gfin
    early = tuple(k for k in SMALL if k != "norm_ffn1")
    gather_small = _gather_plan([grads[k] for k in early] + [grads["conv_w"]])

    (dx0, grads["norm_ffn1"], _), got, small_parts, reduced = _ffn_backward(
        "ffn1", dx1, x0, g_ffn1, full["ffn1_w1"], full["ffn1_w3"], full["ffn1_w2"], ffn1_saved,
        exchange=reduce_mixer, dw_exchange=gather_small, reduce_plan=reduce_ffn1)
    from_chips.update(zip(MIXER, got))
    from_chips.update(zip(FFN1, reduced))

    res = {}
    for k in BIG:
        parts = from_chips[k]
        if k in COL_SHARDED:
            swap = lambda t: jnp.swapaxes(t, -1, -2)
            res[k] = [swap(t) for t in _adamw("adamw_" + k, parts, swap(wt[k]), swap(mom[k]), swap(var[k]))]
        else:
            res[k] = _adamw("adamw_" + k, parts, wt[k], mom[k], var[k])

    def as_2d(k, t):
        if k in ("ssm_B_re", "ssm_B_im"):
            return t[0].transpose(0, 2, 1).reshape(-1, SSM_STATE)
        if k in ("ssm_C_re", "ssm_C_im"):
            return t[0].reshape(-1, SSM_STATE)
        if k in ("ssm_A_re", "ssm_A_im"):
            return t[0]
        return t.reshape(-1, 1) if k == "ssm_log_dt" else t.reshape(1, -1)

    def as_param(k, t):
        if k in ("ssm_B_re", "ssm_B_im"):
            t = t.reshape(groups, SSM_GROUP, SSM_STATE).transpose(0, 2, 1)
        return t.reshape(wt[k].shape)

    (last_part,) = _run_exchange("gather_norm_ffn1_grad", _gather_plan([grads["norm_ffn1"]]))
    order = ("norm_ffn1",) + early
    updated = _adamw_small("adamw_replicated", [last_part] + small_parts,
                           *[[as_2d(k, src[k]) for k in order] for src in (wt, mom, var)])
    res.update({k: [as_param(k, t) for t in upd] for k, upd in zip(order, updated)})
    (conv_w_grad,) = updated[-1]
    x_pos, y_pos, c_pos = (lax.axis_index(a) for a in MESH_AXES)
    cw_cols = c // N_DEV
    own_cw = lax.dynamic_slice_in_dim(conv_w_grad, (4 * x_pos + 2 * y_pos + c_pos) * cw_cols, cw_cols, axis=1)
    res["conv_w"] = _adamw("adamw_conv_w", own_cw[None], wt["conv_w"], mom["conv_w"], var["conv_w"])

    outs = [loss, dx0.reshape(bsz, seq, d)]
    for kind in range(4):
        outs += [res[k][kind] for k in WEIGHTS]
    return tuple(outs)
```

```python
import collections
import functools
import math

import jax
import jax.numpy as jnp
from jax import lax
from jax.experimental import pallas as pl
from jax.experimental.pallas import tpu as pltpu

F32 = jnp.float32
BF16 = jnp.bfloat16

EPS = 1e-6
FFN_RES = 0.5
CONV_WIDTH = 31
CONV_HALO = 32
SSM_GROUP = 16
SSM_STATE = 64
ADAM_LR, ADAM_B1, ADAM_B2, ADAM_EPS, ADAM_WD, ADAM_STEP = 0.001, 0.9, 0.999, 1e-08, 0.01, 10

N_DEV = 8
MESH_AXES = ("x", "y", "c")
SUBLANES = 8
LANES = 128
V7X_VMEM_BYTES = 64 * 2**20
VMEM_LIMIT = V7X_VMEM_BYTES - 8 * 2**20

TILE = dict(row=256, ffn_m=512, mm_bytes=8 * 2**20, up_m=1024, up_n=256, wide_n=2048, conv_t=512, scan_t=256, scan_w=512,
            sum_bytes=4 * 2**20)

_GELU_K = math.sqrt(2.0 / math.pi)
_GELU_C = 0.044715


def _pick(n, target, mult):
    best = None
    for t in range(mult, min(n, target) + 1, mult):
        if n % t == 0:
            best = t
    return n if best is None else best


def _cparams(*sem):
    return pltpu.CompilerParams(dimension_semantics=sem, vmem_limit_bytes=VMEM_LIMIT)


def _sds(shape, dtype):
    return jax.ShapeDtypeStruct(shape, dtype)


def _call(name, body, grid, in_specs, out_specs, out_shape, operands, sem, scratch=(), exchange=None):
    if exchange is None:
        res = pl.pallas_call(body, name=name, grid=grid, in_specs=list(in_specs), out_specs=list(out_specs),
                             out_shape=list(out_shape), scratch_shapes=list(scratch),
                             compiler_params=_cparams(*sem))(*operands)
        return list(res), None
    n_in, n_out, n_scr = len(in_specs), len(out_specs), len(scratch)
    n_xin, n_xout = len(exchange.operands), len(exchange.out_shapes)
    hbm = pl.BlockSpec(memory_space=pltpu.HBM)

    def with_exchange(*refs):
        cuts, pos = [], 0
        for size in (n_in, n_xin, n_out, n_xout, n_scr):
            cuts.append(refs[pos:pos + size])
            pos += size
        ins, x_in, outs, x_out, scr = cuts
        sems = refs[pos:]
        ids = [pl.program_id(axis) for axis in range(len(grid))]
        first = functools.reduce(lambda p, q: p & q, [i == 0 for i in ids])
        last = functools.reduce(lambda p, q: p & q, [i == g - 1 for i, g in zip(ids, grid)])

        @pl.when(first)
        def _():
            exchange.start(x_in, x_out, sems)

        body(*ins, *outs, *scr)

        @pl.when(last)
        def _():
            exchange.finish(x_in, x_out, sems)

    res = pl.pallas_call(
        with_exchange, name=name, grid=grid, in_specs=list(in_specs) + [hbm] * n_xin,
        out_specs=list(out_specs) + [hbm] * n_xout, out_shape=list(out_shape) + list(exchange.out_shapes),
        scratch_shapes=list(scratch) + list(exchange.scratch),
        compiler_params=_cparams(*["arbitrary"] * len(grid)))(*operands, *exchange.operands)
    return list(res[:n_out]), list(res[n_out:])


def _dot(a, b):
    return jnp.dot(a, b, preferred_element_type=F32)


def _dot_nt(a, b):
    return lax.dot_general(a, b, (((1,), (1,)), ((), ())), preferred_element_type=F32)


def _dot_tn(a, b):
    return lax.dot_general(a, b, (((0,), (0,)), ((), ())), preferred_element_type=F32)


def _sigmoid(x):
    return 0.5 * jnp.tanh(0.5 * x) + 0.5


def _rms_stats(x):
    r = lax.rsqrt(jnp.mean(x * x, axis=-1, keepdims=True) + EPS)
    return r, x * r


def _rms_bwd(x, g, dy):
    r, xh = _rms_stats(x)
    dxh = dy * g
    dx = r * (dxh - xh * jnp.mean(dxh * xh, axis=-1, keepdims=True))
    return dx, jnp.sum(dy * xh, axis=0, keepdims=True)


def _rms_mm(name, x, g, ws, out_dtype):
    n, d = x.shape
    f = ws[0].shape[0]
    nw = len(ws)
    tm, tn = _pick(n, TILE["up_m"], 16), _pick(f, TILE["wide_n"], LANES)

    def body(x_ref, g_ref, *refs):
        w_refs, o_refs, h_ref = refs[:nw], refs[nw:2 * nw], refs[2 * nw]

        @pl.when(pl.program_id(1) == 0)
        def _():
            _, xh = _rms_stats(x_ref[...])
            h_ref[...] = (xh * g_ref[...]).astype(BF16)

        h = h_ref[...]
        for w_ref, o_ref in zip(w_refs, o_refs):
            o_ref[...] = _dot_nt(h, w_ref[...]).astype(o_ref.dtype)

    outs = pl.pallas_call(
        body, name=name, grid=(n // tm, f // tn),
        in_specs=[pl.BlockSpec((tm, d), lambda i, j: (i, 0)), pl.BlockSpec((1, d), lambda i, j: (0, 0))]
        + [pl.BlockSpec((tn, d), lambda i, j: (j, 0))] * nw,
        out_specs=[pl.BlockSpec((tm, tn), lambda i, j: (i, j))] * nw + [pl.BlockSpec((tm, d), lambda i, j: (i, 0))],
        out_shape=[_sds((n, f), out_dtype)] * nw + [_sds((n, d), BF16)],
        compiler_params=_cparams("parallel", "arbitrary"),
    )(x, g, *ws)
    return outs[:nw], outs[nw]


def _ffn_fwd(name, x, g, w1t, w3t, w2, exchange=None, head=None):
    n, d = x.shape
    f = w2.shape[0]
    tm, tn = _pick(n, TILE["ffn_m"], 16), _pick(f, TILE["up_n"], LANES)

    def body(x_ref, g_ref, w1_ref, w3_ref, w2_ref, *refs):
        (gf_ref, t_ref), refs = (refs[:2], refs[2:]) if head else ((None, None), refs)
        o_ref, a_ref, b_ref, h_ref = refs[:4]
        xv = x_ref[...]
        _, xh = _rms_stats(xv)
        h = (xh * g_ref[...]).astype(BF16)
        h_ref[...] = h
        acc = None
        for c0 in range(0, f, tn):
            cols = pl.ds(c0, tn)
            av, bv = _dot_nt(h, w1_ref[cols, :]), _dot_nt(h, w3_ref[cols, :])
            a_ref[:, cols] = av.astype(BF16)
            b_ref[:, cols] = bv.astype(BF16)
            t = _dot((av * _sigmoid(av) * bv).astype(BF16), w2_ref[cols, :])
            acc = t if acc is None else acc + t
        out = xv + FFN_RES * acc
        if head is None:
            o_ref[...] = out
        else:
            loss_ref, dg_ref = refs[4:]

            @pl.when(pl.program_id(0) == 0)
            def _():
                loss_ref[...] = jnp.zeros_like(loss_ref)
                dg_ref[...] = jnp.zeros_like(dg_ref)

            dx, loss, dg = _loss_head_rows(out, gf_ref[...], t_ref[...])
            o_ref[...] = dx
            loss_ref[...] += loss
            dg_ref[...] += dg

    row = pl.BlockSpec((tm, d), lambda i: (i, 0))
    wide = pl.BlockSpec((tm, f), lambda i: (i, 0))
    vec = pl.BlockSpec((1, d), lambda i: (0, 0))
    held = pl.BlockSpec((f, d), lambda i: (0, 0), pipeline_mode=pl.Buffered(1))
    extra_in, extra_out, extra_shape = ([vec, row], [pl.BlockSpec((SUBLANES, LANES), lambda i: (0, 0)), vec],
                                        [_sds((SUBLANES, LANES), F32), _sds((1, d), F32)]) if head else ([], [], [])
    return _call(
        name, body, (n // tm,), [row, vec, held, held, held] + extra_in, [row, wide, wide, row] + extra_out,
        [_sds((n, d), F32), _sds((n, f), BF16), _sds((n, f), BF16), _sds((n, d), BF16)] + extra_shape,
        (x, g, w1t, w3t, w2) + (tuple(head) if head else ()), ("arbitrary",) if head else ("parallel",),
        exchange=exchange)


def _ffn_bwd_hidden(name, dxo, a, b, w2, exchange=None):
    n, d = dxo.shape
    f = a.shape[1]
    tm, tn = _pick(n, TILE["row"], 16), _pick(f, TILE["up_n"], LANES)

    def body(dx_ref, a_ref, b_ref, w_ref, da_ref, db_ref, hid_ref, dxh_ref):
        dxh = (FFN_RES * dx_ref[...]).astype(BF16)
        dxh_ref[...] = dxh
        for c0 in range(0, f, tn):
            cols = pl.ds(c0, tn)
            dhid = _dot_nt(dxh, w_ref[cols, :])
            av, bv = a_ref[:, cols].astype(F32), b_ref[:, cols].astype(F32)
            sig = _sigmoid(av)
            silu = av * sig
            da_ref[:, cols] = (dhid * bv * (sig * (1.0 + av - silu))).astype(BF16)
            db_ref[:, cols] = (dhid * silu).astype(BF16)
            hid_ref[:, cols] = (silu * bv).astype(BF16)

    wide = pl.BlockSpec((tm, f), lambda i: (i, 0))
    row = pl.BlockSpec((tm, d), lambda i: (i, 0))
    return _call(
        name, body, (n // tm,), [row, wide, wide, pl.BlockSpec((f, d), lambda i: (0, 0))], [wide, wide, wide, row],
        [_sds((n, f), BF16)] * 3 + [_sds((n, d), BF16)], (dxo, a, b, w2), ("parallel",), exchange=exchange)


def _loss_head_rows(x, g, target):
    r, xh = _rms_stats(x)
    err = xh * g - target
    dy = err * (1.0 / x.shape[-1])
    dxh = dy * g
    dx = r * (dxh - xh * jnp.mean(dxh * xh, axis=-1, keepdims=True))
    return dx, 0.5 * jnp.sum(jnp.mean(err * err, axis=-1, keepdims=True)), jnp.sum(dy * xh, axis=0, keepdims=True)


def _dx_rms_bwd(name, pairs, dxo, x, g, exchange=None):
    n, dm = x.shape
    tm = _pick(n, TILE["ffn_m"], 16)
    npair = len(pairs)

    def body(*refs):
        d_refs, w_refs = refs[:npair], refs[npair:2 * npair]
        dxo_ref, x_ref, g_ref, dx_ref, dg_ref = refs[2 * npair:]

        @pl.when(pl.program_id(0) == 0)
        def _():
            dg_ref[...] = jnp.zeros_like(dg_ref)

        dh = None
        for d_ref, w_ref in zip(d_refs, w_refs):
            t = _dot(d_ref[...].astype(BF16), w_ref[...])
            dh = t if dh is None else dh + t
        dx, dg = _rms_bwd(x_ref[...], g_ref[...], dh)
        dx_ref[...] = dxo_ref[...] + dx
        dg_ref[...] += dg

    row = pl.BlockSpec((tm, dm), lambda i: (i, 0))
    d_specs = [pl.BlockSpec((tm, p[1]), functools.partial(lambda i, cb: (i, cb), cb=p[2])) for p in pairs]
    w_specs = [pl.BlockSpec((p[4], dm), functools.partial(lambda i, rb: (rb, 0), rb=p[5]), pipeline_mode=pl.Buffered(1))
               for p in pairs]
    return _call(
        name, body, (n // tm,), d_specs + w_specs + [row, row, pl.BlockSpec((1, dm), lambda i: (0, 0))],
        [row, pl.BlockSpec((1, dm), lambda i: (0, 0))], [_sds((n, dm), F32), _sds((1, dm), F32)],
        (*[p[0] for p in pairs], *[p[3] for p in pairs], dxo, x, g), ("arbitrary",), exchange=exchange)


def _mm_tn(name, a, b, a_cols=None, b_cols=None, exchange=None):
    n = a.shape[0]
    a0, ma = a_cols if a_cols else (0, a.shape[1])
    b0, mb = b_cols if b_cols else (0, b.shape[1])
    assert a0 % ma == 0 and b0 % mb == 0
    ab, bb = a0 // ma, b0 // mb
    tk = _pick(n, TILE["mm_bytes"] // (ma * a.dtype.itemsize + mb * b.dtype.itemsize), 16)

    def body(a_ref, b_ref, o_ref):
        @pl.when(pl.program_id(0) == 0)
        def _():
            o_ref[...] = jnp.zeros_like(o_ref)

        o_ref[...] += _dot_tn(a_ref[...].astype(BF16), b_ref[...].astype(BF16))

    (out,), got = _call(
        name, body, (n // tk,), [pl.BlockSpec((tk, ma), lambda k: (k, ab)), pl.BlockSpec((tk, mb), lambda k: (k, bb))],
        [pl.BlockSpec((ma, mb), lambda k: (0, 0))], [_sds((ma, mb), F32)], (a, b), ("arbitrary",), exchange=exchange)
    return out if exchange is None else (out, got)


def _row_mm(name, pairs, out_w, out_dtype, add=None, exchange=None):
    n = pairs[0][0].shape[0]
    tm = _pick(n, TILE["row"], 16)
    npair = len(pairs)

    def body(*refs):
        a_refs, w_refs = refs[:npair], refs[npair:2 * npair]
        add_ref = refs[2 * npair] if add is not None else None
        o_ref = refs[-1]
        acc = None
        for a_ref, w_ref, p in zip(a_refs, w_refs, pairs):
            av = a_ref[...].astype(BF16)
            t = _dot_nt(av, w_ref[...]) if p[6] else _dot(av, w_ref[...])
            acc = t if acc is None else acc + t
        if add_ref is not None:
            acc = acc + add_ref[...].astype(F32)
        o_ref[...] = acc.astype(o_ref.dtype)

    a_specs = [pl.BlockSpec((tm, p[1]), functools.partial(lambda i, cb: (i, cb), cb=p[2])) for p in pairs]
    w_specs = [pl.BlockSpec((p[4], p[3].shape[1]), functools.partial(lambda i, rb: (rb, 0), rb=p[5])) for p in pairs]
    add_specs = [pl.BlockSpec((tm, out_w), lambda i: (i, 0))] if add is not None else []
    (out,), got = _call(
        name, body, (n // tm,), a_specs + w_specs + add_specs, [pl.BlockSpec((tm, out_w), lambda i: (i, 0))],
        [_sds((n, out_w), out_dtype)],
        (*[p[0] for p in pairs], *[p[3] for p in pairs], *([add] if add is not None else [])), ("parallel",),
        exchange=exchange)
    return out, got


def _conv_post(c, ln_g, ln_b, out_g):
    mu = jnp.mean(c, axis=-1, keepdims=True)
    xc = c - mu
    rstd = lax.rsqrt(jnp.mean(xc * xc, axis=-1, keepdims=True) + EPS)
    nrm = xc * rstd
    l = nrm * ln_g + ln_b
    sig = _sigmoid(l)
    s = l * sig
    r, sh = _rms_stats(s)
    return sh * out_g, (rstd, nrm, l, sig, r, sh)


def _tap_groups(first):
    groups = []
    for r in range(SUBLANES):
        taps = [(s - r, s - first) for s in range(first, first + CONV_WIDTH) if s % SUBLANES == r]
        if taps:
            groups.append((r, taps))
    return groups


def _conv_taps(a_ref, w_ref, b_ref, first, rows, flip=False):
    acc = None
    for r, taps in _tap_groups(first):
        ext = rows if r == 0 else rows + SUBLANES
        part = None
        for base, k in taps:
            kk = CONV_WIDTH - 1 - k if flip else k
            t = w_ref[kk:kk + 1, :] * a_ref[pl.ds(base, ext), :]
            part = t if part is None else part + t
        if r:
            b_ref[...] = part
            part = b_ref[pl.ds(r, rows), :]
        acc = part if acc is None else acc + part
    return acc


def _conv_post_bwd(cv, dout, ln_g, ln_b, out_g):
    _, (rstd, nrm, l, sig, r, sh) = _conv_post(cv, ln_g, ln_b, out_g)
    dsh = dout * out_g
    ds = r * (dsh - sh * jnp.mean(dsh * sh, axis=-1, keepdims=True))
    dl = ds * (sig * (1.0 + l * (1.0 - sig)))
    dn = dl * ln_g
    dc = rstd * (dn - jnp.mean(dn, axis=-1, keepdims=True) - nrm * jnp.mean(dn * nrm, axis=-1, keepdims=True))
    col_sum = lambda t: jnp.sum(t, axis=0, keepdims=True)
    return dc, col_sum(dout * sh), col_sum(dl * nrm), col_sum(dl)


def _conv_fwd(name, proj3, conv_w, conv_b, ln_g, ln_b, out_g):
    bsz, seq, _ = proj3.shape
    c = conv_w.shape[1]
    tt = _pick(seq, TILE["conv_t"], CONV_HALO)
    hb = tt // CONV_HALO
    first = CONV_HALO - (CONV_WIDTH - 1)

    def body(v_ref, g_ref, vp_ref, gp_ref, w_ref, cb_ref, lg_ref, lb_ref, og_ref, o_ref, cv_ref, a_ref, b_ref):
        keep = (pl.program_id(1) > 0).astype(F32)
        a_ref[pl.ds(0, CONV_HALO), :] = keep * vp_ref[0] * _sigmoid(gp_ref[0])
        a_ref[pl.ds(CONV_HALO, tt), :] = v_ref[0] * _sigmoid(g_ref[0])
        cv = _conv_taps(a_ref, w_ref, b_ref, first, tt) + cb_ref[...]
        cv_ref[0] = cv
        out, _ = _conv_post(cv, lg_ref[...], lb_ref[...], og_ref[...])
        o_ref[0] = out.astype(BF16)

    vec = pl.BlockSpec((1, c), lambda b, i: (0, 0))
    prev = lambda col: pl.BlockSpec((1, CONV_HALO, c), lambda b, i: (b, jnp.maximum(i * hb - 1, 0), col))
    tile = pl.BlockSpec((1, tt, c), lambda b, i: (b, i, 0))
    return pl.pallas_call(
        body, name=name, grid=(bsz, seq // tt),
        in_specs=[tile, pl.BlockSpec((1, tt, c), lambda b, i: (b, i, 1)),
                  prev(0), prev(1), pl.BlockSpec(conv_w.shape, lambda b, i: (0, 0)), vec, vec, vec, vec],
        out_specs=[tile, tile],
        out_shape=[_sds((bsz, seq, c), BF16), _sds((bsz, seq, c), F32)],
        scratch_shapes=[pltpu.VMEM((CONV_HALO + tt, c), F32), pltpu.VMEM((tt + SUBLANES, c), F32)],
        compiler_params=_cparams("parallel", "arbitrary"),
    )(proj3, proj3, proj3, proj3, conv_w, conv_b, ln_g, ln_b, out_g)


def _conv_bwd(name, dmix3, proj3, cv3, conv_w, ln_g, ln_b, out_g):
    bsz, seq, _ = proj3.shape
    c = conv_w.shape[1]
    tt = _pick(seq, TILE["conv_t"], CONV_HALO)
    hb = tt // CONV_HALO
    nt = seq // tt
    last_hb = seq // CONV_HALO - 1
    ext = tt + CONV_HALO
    first = CONV_HALO - (CONV_WIDTH - 1)

    def body(v_ref, g_ref, vp_ref, gp_ref, cv_ref, cvn_ref, d_ref, dn_ref, w_ref, lg_ref, lb_ref, og_ref,
             o_ref, dw_ref, dcb_ref, dlg_ref, dlb_ref, dog_ref, a_ref, dc_ref, b_ref, ds_ref):
        i = pl.program_id(1)

        @pl.when((pl.program_id(0) == 0) & (i == 0))
        def _():
            for r in (dw_ref, dcb_ref, dlg_ref, dlb_ref, dog_ref):
                r[...] = jnp.zeros_like(r)

        keep_prev = (i > 0).astype(F32)
        keep_next = (i < nt - 1).astype(F32)
        sig_g = _sigmoid(g_ref[0])
        a_ref[pl.ds(0, CONV_HALO), :] = keep_prev * vp_ref[0] * _sigmoid(gp_ref[0])
        a_ref[pl.ds(CONV_HALO, tt), :] = v_ref[0] * sig_g

        lg, lb, og = lg_ref[...], lb_ref[...], og_ref[...]
        dc_own, d_og, d_lg, d_lb = _conv_post_bwd(cv_ref[0], d_ref[0], lg, lb, og)
        dc_next, _, _, _ = _conv_post_bwd(cvn_ref[0], keep_next * dn_ref[0], lg, lb, og)
        dog_ref[...] += d_og
        dlg_ref[...] += d_lg
        dlb_ref[...] += d_lb
        dcb_ref[...] += jnp.sum(dc_own, axis=0, keepdims=True)
        dc_ref[pl.ds(0, tt), :] = dc_own
        dc_ref[pl.ds(tt, CONV_HALO), :] = dc_next

        da = _conv_taps(dc_ref, w_ref, b_ref, 0, tt, flip=True)

        for r, taps in _tap_groups(first):
            if r:
                ds_ref[pl.ds(0, SUBLANES), :] = jnp.zeros((SUBLANES, c), F32)
                ds_ref[pl.ds(tt, SUBLANES), :] = jnp.zeros((SUBLANES, c), F32)
                ds_ref[pl.ds(r, tt), :] = dc_own
            for base, k in taps:
                prod = (ds_ref[...] * a_ref[pl.ds(base, tt + SUBLANES), :]) if r else (dc_own * a_ref[pl.ds(base, tt), :])
                dw_ref[k:k + 1, :] += jnp.sum(prod, axis=0, keepdims=True)
        val = v_ref[0]
        o_ref[0] = jnp.concatenate([da * sig_g, da * val * sig_g * (1.0 - sig_g)], axis=-1).astype(BF16)

    vec = pl.BlockSpec((1, c), lambda b, i: (0, 0))
    cur = lambda col: pl.BlockSpec((1, tt, c), lambda b, i: (b, i, col))
    prev = lambda col: pl.BlockSpec((1, CONV_HALO, c), lambda b, i: (b, jnp.maximum(i * hb - 1, 0), col))
    nxt = lambda col: pl.BlockSpec((1, CONV_HALO, c), lambda b, i: (b, jnp.minimum((i + 1) * hb, last_hb), col))
    wspec = pl.BlockSpec(conv_w.shape, lambda b, i: (0, 0))
    return pl.pallas_call(
        body, name=name, grid=(bsz, nt),
        in_specs=[cur(0), cur(1), prev(0), prev(1), cur(0), nxt(0), cur(0), nxt(0), wspec, vec, vec, vec],
        out_specs=[pl.BlockSpec((1, tt, 2 * c), lambda b, i: (b, i, 0)), wspec, vec, vec, vec, vec],
        out_shape=[_sds((bsz, seq, 2 * c), BF16), _sds(conv_w.shape, F32)] + [_sds((1, c), F32)] * 4,
        scratch_shapes=[pltpu.VMEM((CONV_HALO + tt, c), F32), pltpu.VMEM((ext, c), F32),
                        pltpu.VMEM((tt + SUBLANES, c), F32), pltpu.VMEM((tt + SUBLANES, c), F32)],
        compiler_params=_cparams("arbitrary", "arbitrary"),
    )(proj3, proj3, proj3, proj3, cv3, cv3, dmix3, dmix3, conv_w, ln_g, ln_b, out_g)


def _ssm_discretise(a_re, a_im, log_dt):
    dt = jnp.exp(log_dt)
    zr, zi = a_re * dt, a_im * dt
    mag = jnp.exp(zr)
    ar, ai = mag * jnp.cos(zi), mag * jnp.sin(zi)
    den = a_re * a_re + a_im * a_im
    nr = ar - 1.0
    return ar, ai, (nr * a_re + ai * a_im) / den, (ai * a_re - nr * a_im) / den


def _ssm_system(a_re, a_im, log_dt, a_re_x, a_im_x, log_dt_x, bt_re, bt_im):
    ar, ai, _, _ = _ssm_discretise(a_re, a_im, log_dt)
    _, _, cr, ci = _ssm_discretise(a_re_x, a_im_x, log_dt_x)
    return ar, ai, cr * bt_re - ci * bt_im, cr * bt_im + ci * bt_re


def _ssm_prep(name, prim):
    g, p = prim[0].shape

    def body(*refs):
        pwr_ref, pwi_ref, bbr_ref, bbi_ref = refs[8:]
        ar, ai, bbr, bbi = _ssm_system(*[r[...] for r in refs[:8]])
        bbr_ref[...] = bbr
        bbi_ref[...] = bbi
        pr, pi = ar, ai
        for k in range(SUBLANES):
            pwr_ref[k] = pr
            pwi_ref[k] = pi
            pr, pi = pr * ar - pi * ai, pr * ai + pi * ar

    return pl.pallas_call(
        body, name=name,
        out_shape=[_sds((SUBLANES, g, p), F32)] * 2 + [_sds(prim[6].shape, F32)] * 2,
        compiler_params=pltpu.CompilerParams(vmem_limit_bytes=VMEM_LIMIT),
    )(*prim)


def _ssm_param_grads(name, prim, dab_r, dab_i, dbb_r, dbb_i):
    g, p = prim[0].shape
    h = prim[6].shape[0] // g

    def body(*refs):
        dar_ref, dai_ref, dbr_ref, dbi_ref = refs[8:12]
        o_ar, o_ai, o_dt, o_br, o_bi = refs[12:]
        _, vjp = jax.vjp(_ssm_system, *[r[...] for r in refs[:8]])
        ct = (jnp.sum(dar_ref[...], axis=0), jnp.sum(dai_ref[...], axis=0), dbr_ref[...], dbi_ref[...])
        d_ar, d_ai, d_dt, d_arx, d_aix, d_dtx, d_br, d_bi = vjp(ct)
        per_group = lambda t: jnp.sum(t.reshape(g, h, p), axis=1)
        o_ar[...] = d_ar + per_group(d_arx)
        o_ai[...] = d_ai + per_group(d_aix)
        o_dt[...] = d_dt + jnp.sum(per_group(d_dtx), axis=1, keepdims=True)
        o_br[...] = d_br
        o_bi[...] = d_bi

    return pl.pallas_call(
        body, name=name,
        out_shape=[_sds(prim[k].shape, F32) for k in (0, 1, 2, 6, 7)],
        compiler_params=pltpu.CompilerParams(vmem_limit_bytes=VMEM_LIMIT),
    )(*prim, dab_r, dab_i, dbb_r, dbb_i)


def _cfma(xr, xi, cr, ci, sr, si):
    return xr + (cr * sr - ci * si), xi + (cr * si + ci * sr)


def _scan_tables(pw_r, pw_i, reverse):
    gp = pw_r.shape[1] * pw_r.shape[2]
    pr, pi = pw_r.reshape(SUBLANES, gp), pw_i.reshape(SUBLANES, gp)
    if reverse:
        pi = -pi
    row = jnp.arange(SUBLANES)[:, None]
    tabs = []
    for d in (1, 2, 4):
        keep = (row < SUBLANES - d) if reverse else (row >= d)
        tabs += [jnp.where(keep, pr[d - 1][None, :], 0.0), jnp.where(keep, pi[d - 1][None, :], 0.0)]
    tabs += [pr[::-1], pi[::-1]] if reverse else [pr, pi]
    return jnp.concatenate(tabs, axis=0)


MXU_DEPTH = 256


def _bands(c, gp):
    bw = min(c, MXU_DEPTH)
    return c // bw, bw, gp * bw // c


def _band_expand(rows16, w_ref, put, c, gp):
    nb, bw, sw = _bands(c, gp)
    for s in range(nb):
        band = rows16[:, s * bw:(s + 1) * bw]
        for half in (0, gp):
            cols = pl.ds(half + s * sw, sw)
            put(cols, _dot(band, w_ref[pl.ds(s * bw, bw), cols]))


def _band_contract(get16, w_ref, c, gp):
    nb, bw, sw = _bands(c, gp)
    out = []
    for s in range(nb):
        acc = None
        for half in (0, gp):
            cols = pl.ds(half + s * sw, sw)
            t = _dot_nt(get16(cols), w_ref[pl.ds(s * bw, bw), cols])
            acc = t if acc is None else acc + t
        out.append(acc)
    return out[0] if nb == 1 else jnp.concatenate(out, axis=1)


def _band_wgrad(name, a, a_block, c, b):
    n = a.shape[0]
    gp = b.shape[1] // 2
    nb, bw, sw = _bands(c, gp)
    tk = _pick(n, TILE["mm_bytes"] // (c * a.dtype.itemsize + 2 * gp * b.dtype.itemsize), 16)

    def body(a_ref, b_ref, o_ref):
        @pl.when(pl.program_id(0) == 0)
        def _():
            o_ref[...] = jnp.zeros_like(o_ref)

        for s in range(nb):
            band = a_ref[:, s * bw:(s + 1) * bw].astype(BF16)
            for h, half in enumerate((0, gp)):
                o_ref[pl.ds(s * bw, bw), pl.ds(h * sw, sw)] += _dot_tn(
                    band, b_ref[:, pl.ds(half + s * sw, sw)].astype(BF16))

    return pl.pallas_call(
        body, name=name, grid=(n // tk,),
        in_specs=[pl.BlockSpec((tk, c), lambda k: (k, a_block)), pl.BlockSpec((tk, 2 * gp), lambda k: (k, 0))],
        out_specs=pl.BlockSpec((c, 2 * sw), lambda k: (0, 0)),
        out_shape=_sds((c, 2 * sw), F32),
        compiler_params=_cparams("arbitrary"),
    )(a, b)


def _band_diag_take(comp, half, c, gp):
    nb, bw, sw = _bands(c, gp)
    return jnp.concatenate([_block_diag_take(comp[s * bw:(s + 1) * bw, half * sw:(half + 1) * sw], bw // SSM_GROUP)
                            for s in range(nb)], axis=0)


def _scan_fwd(name, tab, proj3, u_block, bbd, cdt):
    bsz, seq, _ = proj3.shape
    c, w = bbd.shape
    gp = w // 2
    tt = _pick(seq, TILE["scan_t"], 16)
    nblk = tt // SUBLANES
    cw = _pick(gp, TILE["scan_w"], LANES)

    def body(tab_ref, u_ref, bbd_ref, cdt_ref, xs_ref, xs16_ref, y_ref, carry_ref, bu_ref):
        @pl.when(pl.program_id(1) == 0)
        def _():
            carry_ref[...] = jnp.zeros_like(carry_ref)

        def put_bu(cols, val):
            bu_ref[0, :, cols] = val

        _band_expand(u_ref[0].astype(BF16), bbd_ref, put_bu, c, gp)

        for ch in range(gp // cw):
            re, im = pl.ds(ch * cw, cw), pl.ds(gp + ch * cw, cw)

            def blk(r, carry, re=re, im=im):
                tabs = [tab_ref[pl.ds(SUBLANES * k, SUBLANES), re] for k in range(8)]
                rows = pl.ds(pl.multiple_of(r * SUBLANES, SUBLANES), SUBLANES)
                xr, xi = bu_ref[0, rows, re], bu_ref[0, rows, im]
                for j, d in enumerate((1, 2, 4)):
                    xr, xi = _cfma(xr, xi, tabs[2 * j], tabs[2 * j + 1], pltpu.roll(xr, d, 0), pltpu.roll(xi, d, 0))
                xr, xi = _cfma(xr, xi, tabs[6], tabs[7], carry[0], carry[1])
                xs_ref[0, rows, re] = xr
                xs_ref[0, rows, im] = xi
                last = SUBLANES - 1
                return (jnp.broadcast_to(xr[last:, :], xr.shape), jnp.broadcast_to(xi[last:, :], xi.shape))

            cr, ci = lax.fori_loop(0, nblk, blk, (carry_ref[:, re], carry_ref[:, im]))
            carry_ref[:, re] = cr
            carry_ref[:, im] = ci

        xs16_ref[0] = xs_ref[0].astype(BF16)
        y_ref[0] = _band_contract(lambda cols: xs16_ref[0, :, cols], cdt_ref, c, gp)

    whole = lambda arr: pl.BlockSpec(arr.shape, lambda b, t: (0, 0))
    wide = pl.BlockSpec((1, tt, w), lambda b, t: (b, t, 0))
    return pl.pallas_call(
        body, name=name, grid=(bsz, seq // tt),
        in_specs=[whole(tab), pl.BlockSpec((1, tt, c), lambda b, t: (b, t, u_block)), whole(bbd), whole(cdt)],
        out_specs=[wide, wide, pl.BlockSpec((1, tt, c), lambda b, t: (b, t, 0))],
        out_shape=[_sds((bsz, seq, w), F32), _sds((bsz, seq, w), BF16), _sds((bsz, seq, c), F32)],
        scratch_shapes=[pltpu.VMEM((SUBLANES, w), F32), pltpu.VMEM((1, tt, w), F32)],
        compiler_params=_cparams("arbitrary", "arbitrary"),
    )(tab, proj3, bbd, cdt)


def _scan_bwd(name, tab, dy3, xs3, du_skip3, bbd, cdt, exchange=None):
    bsz, seq, w = xs3.shape
    c = bbd.shape[0]
    gp = w // 2
    tt = _pick(seq, TILE["scan_t"], 16)
    nblk = tt // SUBLANES
    cw = _pick(gp, TILE["scan_w"], LANES)
    nt = seq // tt

    def body(tab_ref, dy_ref, xs_ref, halo_ref, skip_ref, bbd_ref, cdt_ref, lam16_ref, du_ref, dar_ref, dai_ref,
             carry_ref, g_ref, lam_ref):
        t = pl.program_id(1)

        @pl.when(t == 0)
        def _():
            carry_ref[...] = jnp.zeros_like(carry_ref)

        @pl.when((pl.program_id(0) == 0) & (t == 0))
        def _():
            dar_ref[...] = jnp.zeros_like(dar_ref)
            dai_ref[...] = jnp.zeros_like(dai_ref)

        def put_g(cols, val):
            g_ref[0, :, cols] = val

        _band_expand(dy_ref[0], cdt_ref, put_g, c, gp)

        has_prev = (t < nt - 1).astype(F32)
        row0 = lax.broadcasted_iota(jnp.int32, (SUBLANES, cw), 0) == 0
        last = SUBLANES - 1

        for ch in range(gp // cw):
            re, im = pl.ds(ch * cw, cw), pl.ds(gp + ch * cw, cw)

            def step(rows, xm1r, xm1i, state, re=re, im=im):
                tabs = [tab_ref[pl.ds(SUBLANES * k, SUBLANES), re] for k in range(8)]
                cr, ci, accr, acci = state
                lr, li = g_ref[0, rows, re], g_ref[0, rows, im]
                for j, d in enumerate((1, 2, 4)):
                    lr, li = _cfma(lr, li, tabs[2 * j], tabs[2 * j + 1],
                                   pltpu.roll(lr, SUBLANES - d, 0), pltpu.roll(li, SUBLANES - d, 0))
                lr, li = _cfma(lr, li, tabs[6], tabs[7], cr, ci)
                lam_ref[0, rows, re] = lr
                lam_ref[0, rows, im] = li
                xr, xi = xs_ref[0, rows, re], xs_ref[0, rows, im]
                xpr = jnp.where(row0, jnp.broadcast_to(xm1r[last:, :], xr.shape), pltpu.roll(xr, 1, 0))
                xpi = jnp.where(row0, jnp.broadcast_to(xm1i[last:, :], xi.shape), pltpu.roll(xi, 1, 0))
                accr = accr + (lr * xpr + li * xpi)
                acci = acci + (li * xpr - lr * xpi)
                return (jnp.broadcast_to(lr[:1, :], lr.shape), jnp.broadcast_to(li[:1, :], li.shape), accr, acci)

            def blk(k, state, re=re, im=im, step=step):
                r = nblk - 1 - k
                rows = pl.ds(pl.multiple_of(r * SUBLANES, SUBLANES), SUBLANES)
                prev = pl.ds(pl.multiple_of((r - 1) * SUBLANES, SUBLANES), SUBLANES)
                return step(rows, xs_ref[0, prev, re], xs_ref[0, prev, im], state)

            zero = jnp.zeros((SUBLANES, cw), F32)
            state = lax.fori_loop(0, nblk - 1, blk, (carry_ref[:, re], carry_ref[:, im], zero, zero))
            cr, ci, accr, acci = step(pl.ds(0, SUBLANES), has_prev * halo_ref[0, :, re], has_prev * halo_ref[0, :, im], state)
            carry_ref[:, re] = cr
            carry_ref[:, im] = ci
            dar_ref[:, re] += accr
            dai_ref[:, re] += acci

        lam16_ref[0] = lam_ref[0].astype(BF16)
        du = _band_contract(lambda cols: lam16_ref[0, :, cols], bbd_ref, c, gp)
        du_ref[0] = (du + skip_ref[0]).astype(BF16)

    tile = pl.BlockSpec((1, tt, w), lambda b, t: (b, nt - 1 - t, 0))
    thin = pl.BlockSpec((1, tt, c), lambda b, t: (b, nt - 1 - t, 0))
    halo = pl.BlockSpec((1, SUBLANES, w), lambda b, t: (b, jnp.maximum((nt - 1 - t) * nblk - 1, 0), 0))
    acc = pl.BlockSpec((SUBLANES, gp), lambda b, t: (0, 0))
    whole = lambda arr: pl.BlockSpec(arr.shape, lambda b, t: (0, 0))
    return _call(
        name, body, (bsz, nt), [whole(tab), thin, tile, halo, thin, whole(bbd), whole(cdt)], [tile, thin, acc, acc],
        [_sds(xs3.shape, BF16), _sds((bsz, seq, c), BF16), _sds((SUBLANES, gp), F32), _sds((SUBLANES, gp), F32)],
        (tab, dy3, xs3, xs3, du_skip3, bbd, cdt), ("arbitrary", "arbitrary"),
        scratch=[pltpu.VMEM((SUBLANES, w), F32), pltpu.VMEM((1, tt, w), F32), pltpu.VMEM((1, tt, w), F32)],
        exchange=exchange)


def _gelu_parts(y):
    inner = _GELU_K * (y + _GELU_C * y * y * y)
    t = jnp.tanh(inner)
    return 0.5 * y * (1.0 + t), t


def _ssm_out_fwd(name, cx, proj, u_block, d_skip, glu_w, glu_b, out_g):
    n, c = cx.shape
    tm = _pick(n, TILE["row"], 16)

    def body(cx_ref, u_ref, d_ref, gw_ref, gb_ref, og_ref, y_ref, o_ref):
        y = cx_ref[...] + d_ref[...] * u_ref[...]
        y_ref[...] = y
        gy, _ = _gelu_parts(y)
        z = _dot(gy.astype(BF16), gw_ref[...]) + gb_ref[...]
        _, sh = _rms_stats(gy * _sigmoid(z))
        o_ref[...] = (sh * og_ref[...]).astype(BF16)

    vec = pl.BlockSpec((1, c), lambda i: (0, 0))
    row = pl.BlockSpec((tm, c), lambda i: (i, 0))
    return pl.pallas_call(
        body, name=name, grid=(n // tm,),
        in_specs=[row, pl.BlockSpec((tm, c), lambda i: (i, u_block)), vec, pl.BlockSpec(glu_w.shape, lambda i: (0, 0)),
                  vec, vec],
        out_specs=[row, row],
        out_shape=[_sds((n, c), F32), _sds((n, c), BF16)],
        compiler_params=_cparams("parallel"),
    )(cx, proj, d_skip, glu_w, glu_b, out_g)


def _ssm_out_bwd(name, dmix, d_block, y, proj, u_block, d_skip, glu_w, glu_b, out_g):
    n, c = y.shape
    tm = _pick(n, TILE["row"], 16)

    def body(d_ref, y_ref, u_ref, dk_ref, gw_ref, gb_ref, og_ref, dy_ref, du_ref, dgw_ref, dgb_ref, dog_ref, dd_ref):
        @pl.when(pl.program_id(0) == 0)
        def _():
            for r in (dgw_ref, dgb_ref, dog_ref, dd_ref):
                r[...] = jnp.zeros_like(r)

        yv = y_ref[...]
        gy, th = _gelu_parts(yv)
        gy16 = gy.astype(BF16)
        sz = _sigmoid(_dot(gy16, gw_ref[...]) + gb_ref[...])
        r, sh = _rms_stats(gy * sz)
        dout = d_ref[...]
        dog_ref[...] += jnp.sum(dout * sh, axis=0, keepdims=True)
        dsh = dout * og_ref[...]
        ds = r * (dsh - sh * jnp.mean(dsh * sh, axis=-1, keepdims=True))
        dz = ds * gy * sz * (1.0 - sz)
        dz16 = dz.astype(BF16)
        dgb_ref[...] += jnp.sum(dz, axis=0, keepdims=True)
        dgw_ref[...] += _dot_tn(gy16, dz16)
        dgy = ds * sz + _dot_nt(dz16, gw_ref[...])
        dgelu = 0.5 * (1.0 + th) + 0.5 * yv * (1.0 - th * th) * (_GELU_K * (1.0 + 3.0 * _GELU_C * yv * yv))
        dy = dgy * dgelu
        dy_ref[...] = dy.astype(BF16)
        du_ref[...] = dy * dk_ref[...]
        dd_ref[...] += jnp.sum(dy * u_ref[...], axis=0, keepdims=True)

    vec = pl.BlockSpec((1, c), lambda i: (0, 0))
    row = pl.BlockSpec((tm, c), lambda i: (i, 0))
    mat = pl.BlockSpec(glu_w.shape, lambda i: (0, 0))
    return pl.pallas_call(
        body, name=name, grid=(n // tm,),
        in_specs=[pl.BlockSpec((tm, c), lambda i: (i, d_block)), row, pl.BlockSpec((tm, c), lambda i: (i, u_block)),
                  vec, mat, vec, vec],
        out_specs=[row, row, mat, vec, vec, vec],
        out_shape=[_sds((n, c), BF16), _sds((n, c), F32), _sds(glu_w.shape, F32)] + [_sds((1, c), F32)] * 3,
        compiler_params=_cparams("arbitrary"),
    )(dmix, y, proj, d_skip, glu_w, glu_b, out_g)


def _mesh_pos():
    return tuple(lax.axis_index(a) for a in MESH_AXES)


def _other_chips(x, y):
    return [(1 - x, y), (x, 1 - y), (1 - x, 1 - y)]


def _remote(src, dst, send_sem, recv_sem, dev):
    return pltpu.make_async_remote_copy(src_ref=src, dst_ref=dst, send_sem=send_sem, recv_sem=recv_sem,
                                        device_id=dev, device_id_type=pl.DeviceIdType.MESH)


def _hbm_call(name, body, operands, out_shapes, scratch):
    hbm = pl.BlockSpec(memory_space=pltpu.HBM)
    return pl.pallas_call(body, name=name, in_specs=[hbm] * len(operands), out_specs=[hbm] * len(out_shapes),
                          out_shape=out_shapes, scratch_shapes=scratch)(*operands)


_Exchange = collections.namedtuple("_Exchange", "operands out_shapes scratch start finish")


def _run_exchange(name, plan):
    nin, nout = len(plan.operands), len(plan.out_shapes)

    def body(*refs):
        parts = refs[:nin], refs[nin:nin + nout], refs[nin + nout:]
        plan.start(*parts)
        plan.finish(*parts)

    return _hbm_call(name, body, plan.operands, plan.out_shapes, plan.scratch)


def _gather_plan(blocks):
    nop = len(blocks)

    def copies(x_refs, o_refs, sems):
        send_sems, recv_sems, local_sems = sems
        x, y, c = _mesh_pos()
        me, sibling = (x, y, c), (x, y, 1 - c)
        chips = _other_chips(x, y)

        def copy(i, k, block_of, to, src=None):
            dst = o_refs[i].at[4 * block_of[0] + 2 * block_of[1] + block_of[2]]
            return _remote(dst if src is None else src, dst, send_sems.at[i, k], recv_sems.at[i, k], to)

        own = [pltpu.make_async_copy(x_refs[i], o_refs[i].at[4 * x + 2 * y + c], local_sems.at[i]) for i in range(nop)]
        first = []
        for i in range(nop):
            first.append(copy(i, 0, me, sibling, src=x_refs[i]))
            first += [copy(i, 1 + j, me, (*chip, c), src=x_refs[i]) for j, chip in enumerate(chips)]
        return copy, own, first, me, sibling, chips, c

    def start(x_refs, o_refs, sems):
        _, own, first, *_ = copies(x_refs, o_refs, sems)
        for cp in own + first:
            cp.start()

    def finish(x_refs, o_refs, sems):
        copy, own, first, me, sibling, chips, c = copies(x_refs, o_refs, sems)
        passed = []
        for i in range(nop):
            for j, chip in enumerate(chips):
                copy(i, 1 + j, (*chip, c), me).wait_recv()
                passed.append(copy(i, 4 + j, (*chip, c), sibling))
                passed[-1].start()
        for i in range(nop):
            copy(i, 0, sibling, me).wait_recv()
            for j, chip in enumerate(chips):
                copy(i, 4 + j, (*chip, 1 - c), me).wait_recv()
        for cp in first + passed:
            cp.wait_send()
        for cp in own:
            cp.wait()

    return _Exchange(list(blocks), [_sds((N_DEV,) + b.shape, b.dtype) for b in blocks],
                     [pltpu.SemaphoreType.DMA((nop, N_DEV - 1)), pltpu.SemaphoreType.DMA((nop, N_DEV - 1)),
                      pltpu.SemaphoreType.DMA((nop,))], start, finish)


def _core_exchange_plan(grads):
    nop = len(grads)

    def copies(x_refs, o_refs, sems):
        send_sems, recv_sems = sems
        x, y, c = _mesh_pos()
        return [_remote(x_refs[i].at[2 * q + (1 - c)], o_refs[i].at[q], send_sems.at[i, q], recv_sems.at[i, q],
                        (x, y, 1 - c)) for i in range(nop) for q in range(N_DEV // 2)]

    def start(x_refs, o_refs, sems):
        for cp in copies(x_refs, o_refs, sems):
            cp.start()

    def finish(x_refs, o_refs, sems):
        for cp in copies(x_refs, o_refs, sems):
            cp.wait()

    return _Exchange(list(grads), [_sds((N_DEV // 2,) + g.shape[1:], g.dtype) for g in grads],
                     [pltpu.SemaphoreType.DMA((nop, N_DEV // 2)), pltpu.SemaphoreType.DMA((nop, N_DEV // 2))],
                     start, finish)


def _pair_sum(name, grad, other):
    nchip, _, r, c = grad.shape
    tr = _pick(r, max(SUBLANES, TILE["sum_bytes"] // (4 * c)), SUBLANES)
    core = lax.axis_index("c").astype(jnp.int32).reshape(1)

    def body(core_ref, g_ref, o_ref, s_ref):
        s_ref[0] = (g_ref[0, 0] + o_ref[0]).astype(s_ref.dtype)

    tile = pl.BlockSpec((1, tr, c), lambda q, t, core_ref: (q, t, 0))
    return pl.pallas_call(
        body, name=name,
        grid_spec=pltpu.PrefetchScalarGridSpec(
            num_scalar_prefetch=1, grid=(nchip, r // tr),
            in_specs=[pl.BlockSpec((1, 1, tr, c), lambda q, t, core_ref: (q, core_ref[0], t, 0)), tile],
            out_specs=tile),
        out_shape=_sds((nchip, r, c), BF16),
        compiler_params=_cparams("parallel", "parallel"),
    )(core, grad, other)


def _chip_exchange_plan(sums):
    nop = len(sums)

    def copies(x_refs, o_refs, sems, arriving):
        send_sems, recv_sems, local_sems = sems
        x, y, c = _mesh_pos()
        mine = 2 * x + y
        out = []
        for i in range(nop):
            for j, (px, py) in enumerate(_other_chips(x, y)):
                theirs = 2 * px + py
                src, dst = (mine, theirs) if arriving else (theirs, mine)
                out.append(_remote(x_refs[i].at[src], o_refs[i].at[dst], send_sems.at[i, j], recv_sems.at[i, j],
                                   (px, py, c)))
        if not arriving:
            out += [pltpu.make_async_copy(x_refs[i].at[mine], o_refs[i].at[mine], local_sems.at[i]) for i in range(nop)]
        return out

    def start(x_refs, o_refs, sems):
        for cp in copies(x_refs, o_refs, sems, False):
            cp.start()

    def finish(x_refs, o_refs, sems):
        for cp in copies(x_refs, o_refs, sems, True):
            cp.wait_recv()
        mine = copies(x_refs, o_refs, sems, False)
        for cp in mine[:3 * nop]:
            cp.wait_send()
        for cp in mine[3 * nop:]:
            cp.wait()

    return _Exchange(list(sums), [_sds(s.shape, s.dtype) for s in sums],
                     [pltpu.SemaphoreType.DMA((nop, 3)), pltpu.SemaphoreType.DMA((nop, 3)), pltpu.SemaphoreType.DMA((nop,))],
                     start, finish)


def _part_rows(npart, r, c):
    return _pick(r, max(SUBLANES, TILE["sum_bytes"] // (4 * npart * c)), SUBLANES)


def _sum_slots(p_ref):
    g = p_ref[0].astype(F32)
    for k in range(1, p_ref.shape[0]):
        g = g + p_ref[k].astype(F32)
    return g


def _adamw_step(g, w, m, v):
    c1 = 1.0 - ADAM_B1 ** ADAM_STEP
    c2 = 1.0 - ADAM_B2 ** ADAM_STEP
    nm = ADAM_B1 * m + (1.0 - ADAM_B1) * g
    nv = ADAM_B2 * v + (1.0 - ADAM_B2) * (g * g)
    return -ADAM_LR * ((nm / c1) / (jnp.sqrt(nv / c2) + ADAM_EPS) + ADAM_WD * w), nm, nv


def _adamw_small(name, parts, ws, ms, vs):
    nparam, nall = len(ws), len(parts)

    def body(*refs):
        p_refs = refs[:nall]
        w_refs, m_refs, v_refs = (refs[nall + k * nparam:nall + (k + 1) * nparam] for k in range(3))
        outs = refs[nall + 3 * nparam:]
        for p in range(nall):
            g = _sum_slots(p_refs[p])
            if p < nparam:
                delta, nm, nv = _adamw_step(g, w_refs[p][...], m_refs[p][...], v_refs[p][...])
                for o_ref, val in zip(outs[4 * p:4 * p + 4], (g, delta, nm, nv)):
                    o_ref[...] = val
            else:
                outs[4 * nparam + p - nparam][...] = g

    shapes = [_sds(w.shape, F32) for w in ws for _ in range(4)] + [_sds(p.shape[1:], F32) for p in parts[nparam:]]
    res = pl.pallas_call(body, name=name, out_shape=shapes,
                         compiler_params=pltpu.CompilerParams(vmem_limit_bytes=VMEM_LIMIT))(*parts, *ws, *ms, *vs)
    return [res[4 * p:4 * p + 4] for p in range(nparam)] + [[r] for r in res[4 * nparam:]]


def _adamw(name, parts, w, m, v):
    npart, r, c = parts.shape
    lead = len(w.shape) - 2
    tr = _part_rows(npart, r, c)
    at = (0,) * lead + (slice(None), slice(None))

    def body(p_ref, w_ref, m_ref, v_ref, g_ref, d_ref, nm_ref, nv_ref):
        g = _sum_slots(p_ref)
        delta, nm, nv = _adamw_step(g, w_ref[at], m_ref[at], v_ref[at])
        g_ref[at] = g
        nm_ref[at] = nm
        nv_ref[at] = nv
        d_ref[at] = delta

    row = pl.BlockSpec((1,) * lead + (tr, c), lambda i: (0,) * lead + (i, 0))
    return pl.pallas_call(
        body, name=name, grid=(r // tr,),
        in_specs=[pl.BlockSpec((npart, tr, c), lambda i: (0, i, 0)), row, row, row],
        out_specs=[row] * 4,
        out_shape=[_sds(w.shape, F32)] * 4,
        compiler_params=_cparams("parallel"),
    )(parts, w, m, v)


def _block_diag(rows_gh, groups):
    gh, p = rows_gh.shape
    own = (jnp.arange(gh)[:, None] // (gh // groups) == jnp.arange(groups)[None, :]).astype(rows_gh.dtype)
    return (own[:, :, None] * rows_gh[:, None, :]).reshape(gh, groups * p)


def _block_diag_take(dense, groups):
    gh = dense.shape[0]
    p = dense.shape[1] // groups
    own = (jnp.arange(gh)[:, None] // (gh // groups) == jnp.arange(groups)[None, :]).astype(dense.dtype)
    return jnp.sum(dense.reshape(gh, groups, p) * own[:, :, None], axis=1)


FFN1 = ("ffn1_w1", "ffn1_w3", "ffn1_w2")
MIXER = ("w_in", "ssm_glu_w", "w_out")
FFN2 = ("ffn2_w1", "ffn2_w3", "ffn2_w2")
BIG = FFN1 + MIXER + FFN2
COL_SHARDED = ("ffn1_w1", "ffn1_w3", "w_in", "ffn2_w1", "ffn2_w3", "conv_w")
SMALL = ("norm_ffn1", "norm_mix", "conv_b", "conv_ln_g", "conv_ln_b", "conv_out_g", "ssm_A_re", "ssm_A_im",
         "ssm_log_dt", "ssm_B_re", "ssm_B_im", "ssm_C_re", "ssm_C_im", "ssm_D", "ssm_glu_b", "ssm_out_g",
         "norm_ffn2", "norm_final")
WEIGHTS = ("norm_ffn1", "ffn1_w1", "ffn1_w3", "ffn1_w2", "norm_mix", "w_in", "conv_w", "conv_b", "conv_ln_g",
           "conv_ln_b", "conv_out_g", "ssm_A_re", "ssm_A_im", "ssm_log_dt", "ssm_B_re", "ssm_B_im", "ssm_C_re",
           "ssm_C_im", "ssm_D", "ssm_glu_w", "ssm_glu_b", "ssm_out_g", "w_out", "norm_ffn2", "ffn2_w1", "ffn2_w3",
           "ffn2_w2", "norm_final")


def _ffn_backward(tag, dxo, x, g, w1, w3, w2, saved, exchange=None, dw_exchange=None, reduce_names=None):
    a, b, h = saved
    (da, db, hid, dxh), got = _ffn_bwd_hidden(tag + "_bwd_hidden", dxo, a, b, w2, exchange=exchange)
    dw1, dw_got = _mm_tn(tag + "_dw1", da, h, exchange=dw_exchange) if dw_exchange else (_mm_tn(tag + "_dw1", da, h), None)
    dw3 = _mm_tn(tag + "_dw3", db, h)
    across = None
    if reduce_names:
        send = [_row_blocks(dw1), _row_blocks(dw3)]
        dw2, from_core = _mm_tn(tag + "_dw2", hid, dxh, exchange=_core_exchange_plan(send))
        send.append(_row_blocks(dw2))
        from_core += _run_exchange("exchange_core_" + tag, _core_exchange_plan(send[2:]))
        across = _across_chips(reduce_names, send, from_core)
    else:
        dw2 = _mm_tn(tag + "_dw2", hid, dxh)
    f = a.shape[1]
    (dx, dg), reduced = _dx_rms_bwd(tag + "_bwd_dx", [(da, f, 0, w1, f, 0), (db, f, 0, w3, f, 0)], dxo, x, g,
                                    exchange=across)
    return (dx, dg, [dw1, dw3, dw2]), got, dw_got, reduced


def _row_blocks(grad):
    return grad.reshape((N_DEV, -1) + grad.shape[1:])


def _across_chips(names, send, from_core):
    return _chip_exchange_plan([_pair_sum("pair_sum_" + k, s.reshape((N_DEV // 2, 2) + s.shape[1:]), o)
                                for k, s, o in zip(names, send, from_core)])


def _reduce_in_chip(names, grads):
    send = [_row_blocks(g) for g in grads]
    return _core_exchange_plan(send), functools.partial(_across_chips, names, send)


def kernel(x, norm_ffn1, ffn1_w1, ffn1_w3, ffn1_w2, norm_mix, w_in, conv_w, conv_b, conv_ln_g, conv_ln_b, conv_out_g, ssm_A_re, ssm_A_im, ssm_log_dt, ssm_B_re, ssm_B_im, ssm_C_re, ssm_C_im, ssm_D, ssm_glu_w, ssm_glu_b, ssm_out_g, w_out, norm_ffn2, ffn2_w1, ffn2_w3, ffn2_w2, norm_final, loss_target, m_norm_ffn1, m_ffn1_w1, m_ffn1_w3, m_ffn1_w2, m_norm_mix, m_w_in, m_conv_w, m_conv_b, m_conv_ln_g, m_conv_ln_b, m_conv_out_g, m_ssm_A_re, m_ssm_A_im, m_ssm_log_dt, m_ssm_B_re, m_ssm_B_im, m_ssm_C_re, m_ssm_C_im, m_ssm_D, m_ssm_glu_w, m_ssm_glu_b, m_ssm_out_g, m_w_out, m_norm_ffn2, m_ffn2_w1, m_ffn2_w3, m_ffn2_w2, m_norm_final, v_norm_ffn1, v_ffn1_w1, v_ffn1_w3, v_ffn1_w2, v_norm_mix, v_w_in, v_conv_w, v_conv_b, v_conv_ln_g, v_conv_ln_b, v_conv_out_g, v_ssm_A_re, v_ssm_A_im, v_ssm_log_dt, v_ssm_B_re, v_ssm_B_im, v_ssm_C_re, v_ssm_C_im, v_ssm_D, v_ssm_glu_w, v_ssm_glu_b, v_ssm_out_g, v_w_out, v_norm_ffn2, v_ffn2_w1, v_ffn2_w3, v_ffn2_w2, v_norm_final):
    args = dict(locals())
    wt = {n: args[n] for n in WEIGHTS}
    mom = {n: args["m_" + n] for n in WEIGHTS}
    var = {n: args["v_" + n] for n in WEIGHTS}

    bsz, seq, d = x.shape
    n = bsz * seq
    c = conv_b.shape[-1]
    groups = c // SSM_GROUP
    gp = groups * SSM_STATE
    u_b = 2

    shard = {k: (wt[k][0].T if k in COL_SHARDED else wt[k][0]).astype(BF16) for k in BIG}
    gathered = _run_exchange("gather_weights_ffn1", _gather_plan([shard[k] for k in FFN1]))
    full = {k: g.reshape(-1, g.shape[-1]) for k, g in zip(FFN1, gathered)}
    gather_rest = _gather_plan([shard[k] for k in MIXER + FFN2] + [wt["conv_w"][0]])

    vec = lambda k: wt[k].reshape(1, -1)
    g_ffn1, g_mix, g_ffn2, g_fin = vec("norm_ffn1"), vec("norm_mix"), vec("norm_ffn2"), vec("norm_final")
    cb, lng, lnb, cog = vec("conv_b"), vec("conv_ln_g"), vec("conv_ln_b"), vec("conv_out_g")
    d_skip, glu_b, sog = vec("ssm_D"), vec("ssm_glu_b"), vec("ssm_out_g")

    a_re, a_im = wt["ssm_A_re"][0], wt["ssm_A_im"][0]
    log_dt = wt["ssm_log_dt"][0].reshape(groups, 1)
    bt_re = wt["ssm_B_re"][0].transpose(0, 2, 1).reshape(groups * SSM_GROUP, SSM_STATE)
    bt_im = wt["ssm_B_im"][0].transpose(0, 2, 1).reshape(groups * SSM_GROUP, SSM_STATE)
    c_re = wt["ssm_C_re"][0].reshape(groups * SSM_GROUP, SSM_STATE)
    c_im = wt["ssm_C_im"][0].reshape(groups * SSM_GROUP, SSM_STATE)
    per_chan = lambda t: jnp.repeat(t, SSM_GROUP, axis=0)
    ssm_prim = (a_re, a_im, log_dt, per_chan(a_re), per_chan(a_im), per_chan(jnp.broadcast_to(log_dt, a_re.shape)),
                bt_re, bt_im)
    pw_r, pw_i, bb_r, bb_i = _ssm_prep("ssm_prep", ssm_prim)
    tab_f = _scan_tables(pw_r, pw_i, False)
    tab_b = _scan_tables(pw_r, pw_i, True)
    bbd = jnp.concatenate([_block_diag(bb_r, groups), _block_diag(bb_i, groups)], axis=1).astype(BF16)
    cdt = jnp.concatenate([_block_diag(c_re, groups), -_block_diag(c_im, groups)], axis=1).astype(BF16)

    x0 = x.reshape(n, d)
    (x1, *ffn1_saved), gathered = _ffn_fwd("ffn1_fwd", x0, g_ffn1, full["ffn1_w1"], full["ffn1_w3"], full["ffn1_w2"],
                                           exchange=gather_rest)
    full.update({k: g.reshape(-1, g.shape[-1]) for k, g in zip(MIXER + FFN2, gathered)})
    conv_w_full = gathered[-1].transpose(1, 0, 2).reshape(CONV_WIDTH, c)
    conv_w_pad = jnp.pad(conv_w_full, ((0, CONV_HALO - CONV_WIDTH), (0, 0)))
    (proj,), h2 = _rms_mm("mix_in", x1, g_mix, [full["w_in"]], F32)
    proj3 = proj.reshape(bsz, seq, 3 * c)
    an3, cv3 = _conv_fwd("conv_fwd", proj3, conv_w_pad, cb, lng, lnb, cog)
    an = an3.reshape(n, c)
    xs3, xs16, cx3 = _scan_fwd("scan_fwd", tab_f, proj3, u_b, bbd, cdt)
    y, sn = _ssm_out_fwd("ssm_out_fwd", cx3.reshape(n, c), proj, u_b, d_skip, full["ssm_glu_w"], glu_b, sog)
    w_o = full["w_out"]
    x2, _ = _row_mm("mix_out", [(an, c, 0, w_o, c, 0, False), (sn, c, 0, w_o, c, 1, False)], d, F32, add=x1)
    (dx3, *ffn2_saved, loss_tile, d_gfin), _ = _ffn_fwd(
        "ffn2_fwd", x2, g_ffn2, full["ffn2_w1"], full["ffn2_w3"], full["ffn2_w2"],
        head=(g_fin, loss_target.reshape(n, d)))
    loss = lax.psum(loss_tile[0, 0], MESH_AXES)

    grads, from_chips = {}, {}
    (dx2, grads["norm_ffn2"], dws), _, _, _ = _ffn_backward(
        "ffn2", dx3, x2, g_ffn2, full["ffn2_w1"], full["ffn2_w3"], full["ffn2_w2"], ffn2_saved)
    in_chip, across_chips = _reduce_in_chip(FFN2, dws)

    dmix, got = _row_mm("mix_out_bwd", [(dx2, d, 0, w_o, 2 * c, 0, True)], 2 * c, F32, exchange=in_chip)
    reduce_ffn2 = across_chips(got)
    grads["w_out"] = jnp.concatenate([_mm_tn("dw_out_a", an, dx2), _mm_tn("dw_out_s", sn, dx2)], axis=0)

    dy, du_skip, grads["ssm_glu_w"], grads["ssm_glu_b"], grads["ssm_out_g"], grads["ssm_D"] = _ssm_out_bwd(
        "ssm_out_bwd", dmix, 1, y, proj, u_b, d_skip, full["ssm_glu_w"], glu_b, sog)
    (lam3, du3, dab_r, dab_i), got = _scan_bwd("scan_bwd", tab_b, dy.reshape(bsz, seq, c), xs3,
                                               du_skip.reshape(bsz, seq, c), bbd, cdt, exchange=reduce_ffn2)
    from_chips.update(zip(FFN2, got))
    lam, du = lam3.reshape(n, 2 * gp), du3.reshape(n, c)
    d_bbd = _band_wgrad("ssm_dbb", proj, u_b, c, lam)
    d_cdt = _band_wgrad("ssm_dc", dy, 0, c, xs16.reshape(n, 2 * gp))
    d_are, d_aim, d_ldt, d_btr, d_bti = _ssm_param_grads(
        "ssm_param_grads", ssm_prim,
        dab_r.reshape(SUBLANES, groups, SSM_STATE), dab_i.reshape(SUBLANES, groups, SSM_STATE),
        _band_diag_take(d_bbd, 0, c, gp), _band_diag_take(d_bbd, 1, c, gp))
    grads["ssm_A_re"], grads["ssm_A_im"], grads["ssm_log_dt"] = d_are, d_aim, d_ldt
    grads["ssm_B_re"], grads["ssm_B_im"] = d_btr, d_bti
    grads["ssm_C_re"] = _band_diag_take(d_cdt, 0, c, gp)
    grads["ssm_C_im"] = -_band_diag_take(d_cdt, 1, c, gp)

    dconv3, d_cw, grads["conv_b"], grads["conv_ln_g"], grads["conv_ln_b"], grads["conv_out_g"] = _conv_bwd(
        "conv_bwd", dmix.reshape(bsz, seq, 2 * c), proj3, cv3, conv_w_pad, lng, lnb, cog)
    dconv = dconv3.reshape(n, 2 * c)
    grads["conv_w"] = d_cw[:CONV_WIDTH]
    grads["w_in"] = jnp.concatenate([_mm_tn("dw_in_conv", dconv, h2), _mm_tn("dw_in_ssm", du, h2)], axis=0)
    w_i = full["w_in"]
    in_chip, across_chips = _reduce_in_chip(MIXER, [grads[k] for k in MIXER])
    (dx1, grads["norm_mix"]), got = _dx_rms_bwd("mix_in_bwd", [(dconv, 2 * c, 0, w_i, 2 * c, 0), (du, c, 0, w_i, c, 2)],
                                                dx2, x1, g_mix, exchange=in_chip)
    reduce_mixer = across_chips(got)

    grads["norm_final"] = d_gfin
    early = tuple(k for k in SMALL if k != "norm_ffn1")
    gather_small = _gather_plan([grads[k] for k in early] + [grads["conv_w"]])

    (dx0, grads["norm_ffn1"], _), got, small_parts, reduced = _ffn_backward(
        "ffn1", dx1, x0, g_ffn1, full["ffn1_w1"], full["ffn1_w3"], full["ffn1_w2"], ffn1_saved,
        exchange=reduce_mixer, dw_exchange=gather_small, reduce_names=FFN1)
    from_chips.update(zip(MIXER, got))
    from_chips.update(zip(FFN1, reduced))

    res = {}
    for k in BIG:
        parts = from_chips[k]
        if k in COL_SHARDED:
            swap = lambda t: jnp.swapaxes(t, -1, -2)
            res[k] = [swap(t) for t in _adamw("adamw_" + k, parts, swap(wt[k]), swap(mom[k]), swap(var[k]))]
        else:
            res[k] = _adamw("adamw_" + k, parts, wt[k], mom[k], var[k])

    def as_2d(k, t):
        if k in ("ssm_B_re", "ssm_B_im"):
            return t[0].transpose(0, 2, 1).reshape(-1, SSM_STATE)
        if k in ("ssm_C_re", "ssm_C_im"):
            return t[0].reshape(-1, SSM_STATE)
        if k in ("ssm_A_re", "ssm_A_im"):
            return t[0]
        return t.reshape(-1, 1) if k == "ssm_log_dt" else t.reshape(1, -1)

    def as_param(k, t):
        if k in ("ssm_B_re", "ssm_B_im"):
            t = t.reshape(groups, SSM_GROUP, SSM_STATE).transpose(0, 2, 1)
        return t.reshape(wt[k].shape)

    (last_part,) = _run_exchange("gather_norm_ffn1_grad", _gather_plan([grads["norm_ffn1"]]))
    order = ("norm_ffn1",) + early
    updated = _adamw_small("adamw_replicated", [last_part] + small_parts,
                           *[[as_2d(k, src[k]) for k in order] for src in (wt, mom, var)])
    res.update({k: [as_param(k, t) for t in upd] for k, upd in zip(order, updated)})
    (conv_w_grad,) = updated[-1]
    x_pos, y_pos, c_pos = (lax.axis_index(a) for a in MESH_AXES)
    cw_cols = c // N_DEV
    own_cw = lax.dynamic_slice_in_dim(conv_w_grad, (4 * x_pos + 2 * y_pos + c_pos) * cw_cols, cw_cols, axis=1)
    res["conv_w"] = _adamw("adamw_conv_w", own_cw[None], wt["conv_w"], mom["conv_w"], var["conv_w"])

    outs = [loss, dx0.reshape(bsz, seq, d)]
    for kind in range(4):
        outs += [res[k][kind] for k in WEIGHTS]
    return tuple(outs)
```

```python
import collections
import functools
import math

import jax
import jax.numpy as jnp
from jax import lax
from jax.experimental import pallas as pl
from jax.experimental.pallas import tpu as pltpu

F32 = jnp.float32
BF16 = jnp.bfloat16

EPS = 1e-6
FFN_RES = 0.5
CONV_WIDTH = 31
CONV_HALO = 32
SSM_GROUP = 16
SSM_STATE = 64
ADAM_LR, ADAM_B1, ADAM_B2, ADAM_EPS, ADAM_WD, ADAM_STEP = 0.001, 0.9, 0.999, 1e-08, 0.01, 10

N_DEV = 8
MESH_AXES = ("x", "y", "c")
SUBLANES = 8
LANES = 128
V7X_VMEM_BYTES = 64 * 2**20
VMEM_LIMIT = V7X_VMEM_BYTES - 8 * 2**20

TILE = dict(row=512, hid_m=256, ffn_m=512, mm_bytes=8 * 2**20, up_m=1024, up_n=256, wide_n=2048, conv_t=512,
            scan_fwd_t=512, scan_t=256, scan_w=512, sum_bytes=4 * 2**20)

_GELU_K = math.sqrt(2.0 / math.pi)
_GELU_C = 0.044715


def _pick(n, target, mult):
    best = None
    for t in range(mult, min(n, target) + 1, mult):
        if n % t == 0:
            best = t
    return n if best is None else best


def _cparams(*sem):
    return pltpu.CompilerParams(dimension_semantics=sem, vmem_limit_bytes=VMEM_LIMIT)


def _sds(shape, dtype):
    return jax.ShapeDtypeStruct(shape, dtype)


def _call(name, body, grid, in_specs, out_specs, out_shape, operands, sem, scratch=(), exchange=None):
    if exchange is None:
        res = pl.pallas_call(body, name=name, grid=grid, in_specs=list(in_specs), out_specs=list(out_specs),
                             out_shape=list(out_shape), scratch_shapes=list(scratch),
                             compiler_params=_cparams(*sem))(*operands)
        return list(res), None
    n_in, n_out, n_scr = len(in_specs), len(out_specs), len(scratch)
    n_xin, n_xout = len(exchange.operands), len(exchange.out_shapes)
    hbm = pl.BlockSpec(memory_space=pltpu.HBM)

    def with_exchange(*refs):
        cuts, pos = [], 0
        for size in (n_in, n_xin, n_out, n_xout, n_scr):
            cuts.append(refs[pos:pos + size])
            pos += size
        ins, x_in, outs, x_out, scr = cuts
        sems = refs[pos:]
        ids = [pl.program_id(axis) for axis in range(len(grid))]
        first = functools.reduce(lambda p, q: p & q, [i == 0 for i in ids])
        last = functools.reduce(lambda p, q: p & q, [i == g - 1 for i, g in zip(ids, grid)])

        @pl.when(first)
        def _():
            exchange.start(x_in, x_out, sems)

        body(*ins, *outs, *scr)

        @pl.when(last)
        def _():
            exchange.finish(x_in, x_out, sems)

    res = pl.pallas_call(
        with_exchange, name=name, grid=grid, in_specs=list(in_specs) + [hbm] * n_xin,
        out_specs=list(out_specs) + [hbm] * n_xout, out_shape=list(out_shape) + list(exchange.out_shapes),
        scratch_shapes=list(scratch) + list(exchange.scratch),
        compiler_params=_cparams(*["arbitrary"] * len(grid)))(*operands, *exchange.operands)
    return list(res[:n_out]), list(res[n_out:])


def _dot(a, b):
    return jnp.dot(a, b, preferred_element_type=F32)


def _dot_nt(a, b):
    return lax.dot_general(a, b, (((1,), (1,)), ((), ())), preferred_element_type=F32)


def _dot_tn(a, b):
    return lax.dot_general(a, b, (((0,), (0,)), ((), ())), preferred_element_type=F32)


def _sigmoid(x):
    return 0.5 * jnp.tanh(0.5 * x) + 0.5


def _rms_stats(x):
    r = lax.rsqrt(jnp.mean(x * x, axis=-1, keepdims=True) + EPS)
    return r, x * r


def _rms_bwd(x, g, dy):
    r, xh = _rms_stats(x)
    dxh = dy * g
    dx = r * (dxh - xh * jnp.mean(dxh * xh, axis=-1, keepdims=True))
    return dx, jnp.sum(dy * xh, axis=0, keepdims=True)


def _rms_mm(name, x, g, ws, out_dtype):
    n, d = x.shape
    f = ws[0].shape[0]
    nw = len(ws)
    tm, tn = _pick(n, TILE["up_m"], 16), _pick(f, TILE["wide_n"], LANES)

    def body(x_ref, g_ref, *refs):
        w_refs, o_refs, h_ref = refs[:nw], refs[nw:2 * nw], refs[2 * nw]

        @pl.when(pl.program_id(1) == 0)
        def _():
            _, xh = _rms_stats(x_ref[...])
            h_ref[...] = (xh * g_ref[...]).astype(BF16)

        h = h_ref[...]
        for w_ref, o_ref in zip(w_refs, o_refs):
            o_ref[...] = _dot_nt(h, w_ref[...]).astype(o_ref.dtype)

    outs = pl.pallas_call(
        body, name=name, grid=(n // tm, f // tn),
        in_specs=[pl.BlockSpec((tm, d), lambda i, j: (i, 0)), pl.BlockSpec((1, d), lambda i, j: (0, 0))]
        + [pl.BlockSpec((tn, d), lambda i, j: (j, 0))] * nw,
        out_specs=[pl.BlockSpec((tm, tn), lambda i, j: (i, j))] * nw + [pl.BlockSpec((tm, d), lambda i, j: (i, 0))],
        out_shape=[_sds((n, f), out_dtype)] * nw + [_sds((n, d), BF16)],
        compiler_params=_cparams("parallel", "arbitrary"),
    )(x, g, *ws)
    return outs[:nw], outs[nw]


def _ffn_fwd(name, x, g, w1t, w3t, w2, exchange=None, head=None):
    n, d = x.shape
    f = w2.shape[0]
    tm, tn = _pick(n, TILE["ffn_m"], 16), _pick(f, TILE["up_n"], LANES)

    def body(x_ref, g_ref, w1_ref, w3_ref, w2_ref, *refs):
        (gf_ref, t_ref), refs = (refs[:2], refs[2:]) if head else ((None, None), refs)
        o_ref, a_ref, b_ref, h_ref = refs[:4]
        xv = x_ref[...]
        _, xh = _rms_stats(xv)
        h = (xh * g_ref[...]).astype(BF16)
        h_ref[...] = h
        acc = None
        for c0 in range(0, f, tn):
            cols = pl.ds(c0, tn)
            av, bv = _dot_nt(h, w1_ref[cols, :]), _dot_nt(h, w3_ref[cols, :])
            a_ref[:, cols] = av.astype(BF16)
            b_ref[:, cols] = bv.astype(BF16)
            t = _dot((av * _sigmoid(av) * bv).astype(BF16), w2_ref[cols, :])
            acc = t if acc is None else acc + t
        out = xv + FFN_RES * acc
        if head is None:
            o_ref[...] = out
        else:
            loss_ref, dg_ref = refs[4:]

            @pl.when(pl.program_id(0) == 0)
            def _():
                loss_ref[...] = jnp.zeros_like(loss_ref)
                dg_ref[...] = jnp.zeros_like(dg_ref)

            dx, loss, dg = _loss_head_rows(out, gf_ref[...], t_ref[...])
            o_ref[...] = dx
            loss_ref[...] += loss
            dg_ref[...] += dg

    row = pl.BlockSpec((tm, d), lambda i: (i, 0))
    wide = pl.BlockSpec((tm, f), lambda i: (i, 0))
    vec = pl.BlockSpec((1, d), lambda i: (0, 0))
    held = pl.BlockSpec((f, d), lambda i: (0, 0), pipeline_mode=pl.Buffered(1))
    extra_in, extra_out, extra_shape = ([vec, row], [pl.BlockSpec((SUBLANES, LANES), lambda i: (0, 0)), vec],
                                        [_sds((SUBLANES, LANES), F32), _sds((1, d), F32)]) if head else ([], [], [])
    return _call(
        name, body, (n // tm,), [row, vec, held, held, held] + extra_in, [row, wide, wide, row] + extra_out,
        [_sds((n, d), F32), _sds((n, f), BF16), _sds((n, f), BF16), _sds((n, d), BF16)] + extra_shape,
        (x, g, w1t, w3t, w2) + (tuple(head) if head else ()), ("arbitrary",) if head else ("parallel",),
        exchange=exchange)


def _ffn_bwd_hidden(name, dxo, a, b, w2, exchange=None):
    n, d = dxo.shape
    f = a.shape[1]
    tm, tn = _pick(n, TILE["hid_m"], 16), _pick(f, TILE["up_n"], LANES)

    def body(dx_ref, a_ref, b_ref, w_ref, da_ref, db_ref, hid_ref, dxh_ref):
        dxh = (FFN_RES * dx_ref[...]).astype(BF16)
        dxh_ref[...] = dxh
        for c0 in range(0, f, tn):
            cols = pl.ds(c0, tn)
            dhid = _dot_nt(dxh, w_ref[cols, :])
            av, bv = a_ref[:, cols].astype(F32), b_ref[:, cols].astype(F32)
            sig = _sigmoid(av)
            silu = av * sig
            da_ref[:, cols] = (dhid * bv * (sig * (1.0 + av - silu))).astype(BF16)
            db_ref[:, cols] = (dhid * silu).astype(BF16)
            hid_ref[:, cols] = (silu * bv).astype(BF16)

    wide = pl.BlockSpec((tm, f), lambda i: (i, 0))
    row = pl.BlockSpec((tm, d), lambda i: (i, 0))
    return _call(
        name, body, (n // tm,), [row, wide, wide, pl.BlockSpec((f, d), lambda i: (0, 0))], [wide, wide, wide, row],
        [_sds((n, f), BF16)] * 3 + [_sds((n, d), BF16)], (dxo, a, b, w2), ("parallel",), exchange=exchange)


def _loss_head_rows(x, g, target):
    r, xh = _rms_stats(x)
    err = xh * g - target
    dy = err * (1.0 / x.shape[-1])
    dxh = dy * g
    dx = r * (dxh - xh * jnp.mean(dxh * xh, axis=-1, keepdims=True))
    return dx, 0.5 * jnp.sum(jnp.mean(err * err, axis=-1, keepdims=True)), jnp.sum(dy * xh, axis=0, keepdims=True)


def _dx_rms_bwd(name, pairs, dxo, x, g, exchange=None):
    n, dm = x.shape
    tm = _pick(n, TILE["ffn_m"], 16)
    npair = len(pairs)

    def body(*refs):
        d_refs, w_refs = refs[:npair], refs[npair:2 * npair]
        dxo_ref, x_ref, g_ref, dx_ref, dg_ref = refs[2 * npair:]

        @pl.when(pl.program_id(0) == 0)
        def _():
            dg_ref[...] = jnp.zeros_like(dg_ref)

        dh = None
        for d_ref, w_ref in zip(d_refs, w_refs):
            t = _dot(d_ref[...].astype(BF16), w_ref[...])
            dh = t if dh is None else dh + t
        dx, dg = _rms_bwd(x_ref[...], g_ref[...], dh)
        dx_ref[...] = dxo_ref[...] + dx
        dg_ref[...] += dg

    row = pl.BlockSpec((tm, dm), lambda i: (i, 0))
    d_specs = [pl.BlockSpec((tm, p[1]), functools.partial(lambda i, cb: (i, cb), cb=p[2])) for p in pairs]
    w_specs = [pl.BlockSpec((p[4], dm), functools.partial(lambda i, rb: (rb, 0), rb=p[5]), pipeline_mode=pl.Buffered(1))
               for p in pairs]
    return _call(
        name, body, (n // tm,), d_specs + w_specs + [row, row, pl.BlockSpec((1, dm), lambda i: (0, 0))],
        [row, pl.BlockSpec((1, dm), lambda i: (0, 0))], [_sds((n, dm), F32), _sds((1, dm), F32)],
        (*[p[0] for p in pairs], *[p[3] for p in pairs], dxo, x, g), ("arbitrary",), exchange=exchange)


def _mm_tn(name, a, b, a_cols=None, b_cols=None, exchange=None):
    n = a.shape[0]
    a0, ma = a_cols if a_cols else (0, a.shape[1])
    b0, mb = b_cols if b_cols else (0, b.shape[1])
    assert a0 % ma == 0 and b0 % mb == 0
    ab, bb = a0 // ma, b0 // mb
    tk = _pick(n, TILE["mm_bytes"] // (ma * a.dtype.itemsize + mb * b.dtype.itemsize), 16)

    def body(a_ref, b_ref, o_ref):
        @pl.when(pl.program_id(0) == 0)
        def _():
            o_ref[...] = jnp.zeros_like(o_ref)

        o_ref[...] += _dot_tn(a_ref[...].astype(BF16), b_ref[...].astype(BF16))

    (out,), got = _call(
        name, body, (n // tk,), [pl.BlockSpec((tk, ma), lambda k: (k, ab)), pl.BlockSpec((tk, mb), lambda k: (k, bb))],
        [pl.BlockSpec((ma, mb), lambda k: (0, 0))], [_sds((ma, mb), F32)], (a, b), ("arbitrary",), exchange=exchange)
    return out if exchange is None else (out, got)


def _row_mm(name, pairs, out_w, out_dtype, add=None, exchange=None):
    n = pairs[0][0].shape[0]
    tm = _pick(n, TILE["row"], 16)
    npair = len(pairs)

    def body(*refs):
        a_refs, w_refs = refs[:npair], refs[npair:2 * npair]
        add_ref = refs[2 * npair] if add is not None else None
        o_ref = refs[-1]
        acc = None
        for a_ref, w_ref, p in zip(a_refs, w_refs, pairs):
            av = a_ref[...].astype(BF16)
            t = _dot_nt(av, w_ref[...]) if p[6] else _dot(av, w_ref[...])
            acc = t if acc is None else acc + t
        if add_ref is not None:
            acc = acc + add_ref[...].astype(F32)
        o_ref[...] = acc.astype(o_ref.dtype)

    a_specs = [pl.BlockSpec((tm, p[1]), functools.partial(lambda i, cb: (i, cb), cb=p[2])) for p in pairs]
    w_specs = [pl.BlockSpec((p[4], p[3].shape[1]), functools.partial(lambda i, rb: (rb, 0), rb=p[5])) for p in pairs]
    add_specs = [pl.BlockSpec((tm, out_w), lambda i: (i, 0))] if add is not None else []
    (out,), got = _call(
        name, body, (n // tm,), a_specs + w_specs + add_specs, [pl.BlockSpec((tm, out_w), lambda i: (i, 0))],
        [_sds((n, out_w), out_dtype)],
        (*[p[0] for p in pairs], *[p[3] for p in pairs], *([add] if add is not None else [])), ("parallel",),
        exchange=exchange)
    return out, got


def _conv_post(c, ln_g, ln_b, out_g):
    mu = jnp.mean(c, axis=-1, keepdims=True)
    xc = c - mu
    rstd = lax.rsqrt(jnp.mean(xc * xc, axis=-1, keepdims=True) + EPS)
    nrm = xc * rstd
    l = nrm * ln_g + ln_b
    sig = _sigmoid(l)
    s = l * sig
    r, sh = _rms_stats(s)
    return sh * out_g, (rstd, nrm, l, sig, r, sh)


def _tap_groups(first):
    groups = []
    for r in range(SUBLANES):
        taps = [(s - r, s - first) for s in range(first, first + CONV_WIDTH) if s % SUBLANES == r]
        if taps:
            groups.append((r, taps))
    return groups


def _conv_taps(a_ref, w_ref, b_ref, first, rows, flip=False):
    acc = None
    for r, taps in _tap_groups(first):
        ext = rows if r == 0 else rows + SUBLANES
        part = None
        for base, k in taps:
            kk = CONV_WIDTH - 1 - k if flip else k
            t = w_ref[kk:kk + 1, :] * a_ref[pl.ds(base, ext), :]
            part = t if part is None else part + t
        if r:
            b_ref[...] = part
            part = b_ref[pl.ds(r, rows), :]
        acc = part if acc is None else acc + part
    return acc


def _conv_post_bwd(cv, dout, ln_g, ln_b, out_g):
    _, (rstd, nrm, l, sig, r, sh) = _conv_post(cv, ln_g, ln_b, out_g)
    dsh = dout * out_g
    ds = r * (dsh - sh * jnp.mean(dsh * sh, axis=-1, keepdims=True))
    dl = ds * (sig * (1.0 + l * (1.0 - sig)))
    dn = dl * ln_g
    dc = rstd * (dn - jnp.mean(dn, axis=-1, keepdims=True) - nrm * jnp.mean(dn * nrm, axis=-1, keepdims=True))
    col_sum = lambda t: jnp.sum(t, axis=0, keepdims=True)
    return dc, col_sum(dout * sh), col_sum(dl * nrm), col_sum(dl)


def _conv_fwd(name, proj3, conv_w, conv_b, ln_g, ln_b, out_g):
    bsz, seq, _ = proj3.shape
    c = conv_w.shape[1]
    tt = _pick(seq, TILE["conv_t"], CONV_HALO)
    hb = tt // CONV_HALO
    first = CONV_HALO - (CONV_WIDTH - 1)

    def body(v_ref, g_ref, vp_ref, gp_ref, w_ref, cb_ref, lg_ref, lb_ref, og_ref, o_ref, cv_ref, a_ref, b_ref):
        keep = (pl.program_id(1) > 0).astype(F32)
        a_ref[pl.ds(0, CONV_HALO), :] = keep * vp_ref[0] * _sigmoid(gp_ref[0])
        a_ref[pl.ds(CONV_HALO, tt), :] = v_ref[0] * _sigmoid(g_ref[0])
        cv = _conv_taps(a_ref, w_ref, b_ref, first, tt) + cb_ref[...]
        cv_ref[0] = cv
        out, _ = _conv_post(cv, lg_ref[...], lb_ref[...], og_ref[...])
        o_ref[0] = out.astype(BF16)

    vec = pl.BlockSpec((1, c), lambda b, i: (0, 0))
    prev = lambda col: pl.BlockSpec((1, CONV_HALO, c), lambda b, i: (b, jnp.maximum(i * hb - 1, 0), col))
    tile = pl.BlockSpec((1, tt, c), lambda b, i: (b, i, 0))
    return pl.pallas_call(
        body, name=name, grid=(bsz, seq // tt),
        in_specs=[tile, pl.BlockSpec((1, tt, c), lambda b, i: (b, i, 1)),
                  prev(0), prev(1), pl.BlockSpec(conv_w.shape, lambda b, i: (0, 0)), vec, vec, vec, vec],
        out_specs=[tile, tile],
        out_shape=[_sds((bsz, seq, c), BF16), _sds((bsz, seq, c), F32)],
        scratch_shapes=[pltpu.VMEM((CONV_HALO + tt, c), F32), pltpu.VMEM((tt + SUBLANES, c), F32)],
        compiler_params=_cparams("parallel", "arbitrary"),
    )(proj3, proj3, proj3, proj3, conv_w, conv_b, ln_g, ln_b, out_g)


def _conv_bwd(name, dmix3, proj3, cv3, conv_w, ln_g, ln_b, out_g):
    bsz, seq, _ = proj3.shape
    c = conv_w.shape[1]
    tt = _pick(seq, TILE["conv_t"], CONV_HALO)
    hb = tt // CONV_HALO
    nt = seq // tt
    last_hb = seq // CONV_HALO - 1
    ext = tt + CONV_HALO
    first = CONV_HALO - (CONV_WIDTH - 1)

    def body(v_ref, g_ref, vp_ref, gp_ref, cv_ref, cvn_ref, d_ref, dn_ref, w_ref, lg_ref, lb_ref, og_ref,
             o_ref, dw_ref, dcb_ref, dlg_ref, dlb_ref, dog_ref, a_ref, dc_ref, b_ref, ds_ref):
        i = pl.program_id(1)

        @pl.when((pl.program_id(0) == 0) & (i == 0))
        def _():
            for r in (dw_ref, dcb_ref, dlg_ref, dlb_ref, dog_ref):
                r[...] = jnp.zeros_like(r)

        keep_prev = (i > 0).astype(F32)
        keep_next = (i < nt - 1).astype(F32)
        sig_g = _sigmoid(g_ref[0])
        a_ref[pl.ds(0, CONV_HALO), :] = keep_prev * vp_ref[0] * _sigmoid(gp_ref[0])
        a_ref[pl.ds(CONV_HALO, tt), :] = v_ref[0] * sig_g

        lg, lb, og = lg_ref[...], lb_ref[...], og_ref[...]
        dc_own, d_og, d_lg, d_lb = _conv_post_bwd(cv_ref[0], d_ref[0], lg, lb, og)
        dc_next, _, _, _ = _conv_post_bwd(cvn_ref[0], keep_next * dn_ref[0], lg, lb, og)
        dog_ref[...] += d_og
        dlg_ref[...] += d_lg
        dlb_ref[...] += d_lb
        dcb_ref[...] += jnp.sum(dc_own, axis=0, keepdims=True)
        dc_ref[pl.ds(0, tt), :] = dc_own
        dc_ref[pl.ds(tt, CONV_HALO), :] = dc_next

        da = _conv_taps(dc_ref, w_ref, b_ref, 0, tt, flip=True)

        for r, taps in _tap_groups(first):
            if r:
                ds_ref[pl.ds(0, SUBLANES), :] = jnp.zeros((SUBLANES, c), F32)
                ds_ref[pl.ds(tt, SUBLANES), :] = jnp.zeros((SUBLANES, c), F32)
                ds_ref[pl.ds(r, tt), :] = dc_own
            for base, k in taps:
                prod = (ds_ref[...] * a_ref[pl.ds(base, tt + SUBLANES), :]) if r else (dc_own * a_ref[pl.ds(base, tt), :])
                dw_ref[k:k + 1, :] += jnp.sum(prod, axis=0, keepdims=True)
        val = v_ref[0]
        o_ref[0] = jnp.concatenate([da * sig_g, da * val * sig_g * (1.0 - sig_g)], axis=-1).astype(BF16)

    vec = pl.BlockSpec((1, c), lambda b, i: (0, 0))
    cur = lambda col: pl.BlockSpec((1, tt, c), lambda b, i: (b, i, col))
    prev = lambda col: pl.BlockSpec((1, CONV_HALO, c), lambda b, i: (b, jnp.maximum(i * hb - 1, 0), col))
    nxt = lambda col: pl.BlockSpec((1, CONV_HALO, c), lambda b, i: (b, jnp.minimum((i + 1) * hb, last_hb), col))
    wspec = pl.BlockSpec(conv_w.shape, lambda b, i: (0, 0))
    return pl.pallas_call(
        body, name=name, grid=(bsz, nt),
        in_specs=[cur(0), cur(1), prev(0), prev(1), cur(0), nxt(0), cur(0), nxt(0), wspec, vec, vec, vec],
        out_specs=[pl.BlockSpec((1, tt, 2 * c), lambda b, i: (b, i, 0)), wspec, vec, vec, vec, vec],
        out_shape=[_sds((bsz, seq, 2 * c), BF16), _sds(conv_w.shape, F32)] + [_sds((1, c), F32)] * 4,
        scratch_shapes=[pltpu.VMEM((CONV_HALO + tt, c), F32), pltpu.VMEM((ext, c), F32),
                        pltpu.VMEM((tt + SUBLANES, c), F32), pltpu.VMEM((tt + SUBLANES, c), F32)],
        compiler_params=_cparams("arbitrary", "arbitrary"),
    )(proj3, proj3, proj3, proj3, cv3, cv3, dmix3, dmix3, conv_w, ln_g, ln_b, out_g)


def _ssm_discretise(a_re, a_im, log_dt):
    dt = jnp.exp(log_dt)
    zr, zi = a_re * dt, a_im * dt
    mag = jnp.exp(zr)
    ar, ai = mag * jnp.cos(zi), mag * jnp.sin(zi)
    den = a_re * a_re + a_im * a_im
    nr = ar - 1.0
    return ar, ai, (nr * a_re + ai * a_im) / den, (ai * a_re - nr * a_im) / den


def _ssm_system(a_re, a_im, log_dt, a_re_x, a_im_x, log_dt_x, bt_re, bt_im):
    ar, ai, _, _ = _ssm_discretise(a_re, a_im, log_dt)
    _, _, cr, ci = _ssm_discretise(a_re_x, a_im_x, log_dt_x)
    return ar, ai, cr * bt_re - ci * bt_im, cr * bt_im + ci * bt_re


def _ssm_prep(name, prim):
    g, p = prim[0].shape

    def body(*refs):
        pwr_ref, pwi_ref, bbr_ref, bbi_ref = refs[8:]
        ar, ai, bbr, bbi = _ssm_system(*[r[...] for r in refs[:8]])
        bbr_ref[...] = bbr
        bbi_ref[...] = bbi
        pr, pi = ar, ai
        for k in range(SUBLANES):
            pwr_ref[k] = pr
            pwi_ref[k] = pi
            pr, pi = pr * ar - pi * ai, pr * ai + pi * ar

    return pl.pallas_call(
        body, name=name,
        out_shape=[_sds((SUBLANES, g, p), F32)] * 2 + [_sds(prim[6].shape, F32)] * 2,
        compiler_params=pltpu.CompilerParams(vmem_limit_bytes=VMEM_LIMIT),
    )(*prim)


def _ssm_param_grads(name, prim, dab_r, dab_i, dbb_r, dbb_i):
    g, p = prim[0].shape
    h = prim[6].shape[0] // g

    def body(*refs):
        dar_ref, dai_ref, dbr_ref, dbi_ref = refs[8:12]
        o_ar, o_ai, o_dt, o_br, o_bi = refs[12:]
        _, vjp = jax.vjp(_ssm_system, *[r[...] for r in refs[:8]])
        ct = (jnp.sum(dar_ref[...], axis=0), jnp.sum(dai_ref[...], axis=0), dbr_ref[...], dbi_ref[...])
        d_ar, d_ai, d_dt, d_arx, d_aix, d_dtx, d_br, d_bi = vjp(ct)
        per_group = lambda t: jnp.sum(t.reshape(g, h, p), axis=1)
        o_ar[...] = d_ar + per_group(d_arx)
        o_ai[...] = d_ai + per_group(d_aix)
        o_dt[...] = d_dt + jnp.sum(per_group(d_dtx), axis=1, keepdims=True)
        o_br[...] = d_br
        o_bi[...] = d_bi

    return pl.pallas_call(
        body, name=name,
        out_shape=[_sds(prim[k].shape, F32) for k in (0, 1, 2, 6, 7)],
        compiler_params=pltpu.CompilerParams(vmem_limit_bytes=VMEM_LIMIT),
    )(*prim, dab_r, dab_i, dbb_r, dbb_i)


def _cfma(xr, xi, cr, ci, sr, si):
    return xr + (cr * sr - ci * si), xi + (cr * si + ci * sr)


def _scan_tables(pw_r, pw_i, reverse):
    gp = pw_r.shape[1] * pw_r.shape[2]
    pr, pi = pw_r.reshape(SUBLANES, gp), pw_i.reshape(SUBLANES, gp)
    if reverse:
        pi = -pi
    row = jnp.arange(SUBLANES)[:, None]
    tabs = []
    for d in (1, 2, 4):
        keep = (row < SUBLANES - d) if reverse else (row >= d)
        tabs += [jnp.where(keep, pr[d - 1][None, :], 0.0), jnp.where(keep, pi[d - 1][None, :], 0.0)]
    tabs += [pr[::-1], pi[::-1]] if reverse else [pr, pi]
    return jnp.concatenate(tabs, axis=0)


MXU_DEPTH = 256


def _bands(c, gp):
    bw = min(c, MXU_DEPTH)
    return c // bw, bw, gp * bw // c


def _band_expand(rows16, w_ref, put, c, gp):
    nb, bw, sw = _bands(c, gp)
    for s in range(nb):
        band = rows16[:, s * bw:(s + 1) * bw]
        for half in (0, gp):
            cols = pl.ds(half + s * sw, sw)
            put(cols, _dot(band, w_ref[pl.ds(s * bw, bw), cols]))


def _band_contract(get16, w_ref, c, gp):
    nb, bw, sw = _bands(c, gp)
    out = []
    for s in range(nb):
        acc = None
        for half in (0, gp):
            cols = pl.ds(half + s * sw, sw)
            t = _dot_nt(get16(cols), w_ref[pl.ds(s * bw, bw), cols])
            acc = t if acc is None else acc + t
        out.append(acc)
    return out[0] if nb == 1 else jnp.concatenate(out, axis=1)


def _band_wgrad(name, a, a_block, c, b):
    n = a.shape[0]
    gp = b.shape[1] // 2
    nb, bw, sw = _bands(c, gp)
    tk = _pick(n, TILE["mm_bytes"] // (c * a.dtype.itemsize + 2 * gp * b.dtype.itemsize), 16)

    def body(a_ref, b_ref, o_ref):
        @pl.when(pl.program_id(0) == 0)
        def _():
            o_ref[...] = jnp.zeros_like(o_ref)

        for s in range(nb):
            band = a_ref[:, s * bw:(s + 1) * bw].astype(BF16)
            for h, half in enumerate((0, gp)):
                o_ref[pl.ds(s * bw, bw), pl.ds(h * sw, sw)] += _dot_tn(
                    band, b_ref[:, pl.ds(half + s * sw, sw)].astype(BF16))

    return pl.pallas_call(
        body, name=name, grid=(n // tk,),
        in_specs=[pl.BlockSpec((tk, c), lambda k: (k, a_block)), pl.BlockSpec((tk, 2 * gp), lambda k: (k, 0))],
        out_specs=pl.BlockSpec((c, 2 * sw), lambda k: (0, 0)),
        out_shape=_sds((c, 2 * sw), F32),
        compiler_params=_cparams("arbitrary"),
    )(a, b)


def _band_diag_take(comp, half, c, gp):
    nb, bw, sw = _bands(c, gp)
    return jnp.concatenate([_block_diag_take(comp[s * bw:(s + 1) * bw, half * sw:(half + 1) * sw], bw // SSM_GROUP)
                            for s in range(nb)], axis=0)


def _scan_fwd(name, tab, proj3, u_block, bbd, cdt):
    bsz, seq, _ = proj3.shape
    c, w = bbd.shape
    gp = w // 2
    tt = _pick(seq, TILE["scan_fwd_t"], 16)
    nblk = tt // SUBLANES
    cw = _pick(gp, TILE["scan_w"], LANES)

    def body(tab_ref, u_ref, bbd_ref, cdt_ref, xs_ref, xs16_ref, y_ref, carry_ref, bu_ref):
        @pl.when(pl.program_id(1) == 0)
        def _():
            carry_ref[...] = jnp.zeros_like(carry_ref)

        def put_bu(cols, val):
            bu_ref[0, :, cols] = val

        _band_expand(u_ref[0].astype(BF16), bbd_ref, put_bu, c, gp)

        for ch in range(gp // cw):
            re, im = pl.ds(ch * cw, cw), pl.ds(gp + ch * cw, cw)

            def blk(r, carry, re=re, im=im):
                tabs = [tab_ref[pl.ds(SUBLANES * k, SUBLANES), re] for k in range(8)]
                rows = pl.ds(pl.multiple_of(r * SUBLANES, SUBLANES), SUBLANES)
                xr, xi = bu_ref[0, rows, re], bu_ref[0, rows, im]
                for j, d in enumerate((1, 2, 4)):
                    xr, xi = _cfma(xr, xi, tabs[2 * j], tabs[2 * j + 1], pltpu.roll(xr, d, 0), pltpu.roll(xi, d, 0))
                xr, xi = _cfma(xr, xi, tabs[6], tabs[7], carry[0], carry[1])
                xs_ref[0, rows, re] = xr
                xs_ref[0, rows, im] = xi
                last = SUBLANES - 1
                return (jnp.broadcast_to(xr[last:, :], xr.shape), jnp.broadcast_to(xi[last:, :], xi.shape))

            cr, ci = lax.fori_loop(0, nblk, blk, (carry_ref[:, re], carry_ref[:, im]))
            carry_ref[:, re] = cr
            carry_ref[:, im] = ci

        xs16_ref[0] = xs_ref[0].astype(BF16)
        y_ref[0] = _band_contract(lambda cols: xs16_ref[0, :, cols], cdt_ref, c, gp)

    whole = lambda arr: pl.BlockSpec(arr.shape, lambda b, t: (0, 0), pipeline_mode=pl.Buffered(1))
    wide = pl.BlockSpec((1, tt, w), lambda b, t: (b, t, 0))
    return pl.pallas_call(
        body, name=name, grid=(bsz, seq // tt),
        in_specs=[whole(tab), pl.BlockSpec((1, tt, c), lambda b, t: (b, t, u_block)), whole(bbd), whole(cdt)],
        out_specs=[wide, wide, pl.BlockSpec((1, tt, c), lambda b, t: (b, t, 0))],
        out_shape=[_sds((bsz, seq, w), F32), _sds((bsz, seq, w), BF16), _sds((bsz, seq, c), F32)],
        scratch_shapes=[pltpu.VMEM((SUBLANES, w), F32), pltpu.VMEM((1, tt, w), F32)],
        compiler_params=_cparams("arbitrary", "arbitrary"),
    )(tab, proj3, bbd, cdt)


def _scan_bwd(name, tab, dy3, xs3, du_skip3, bbd, cdt, exchange=None):
    bsz, seq, w = xs3.shape
    c = bbd.shape[0]
    gp = w // 2
    tt = _pick(seq, TILE["scan_t"], 16)
    nblk = tt // SUBLANES
    cw = _pick(gp, TILE["scan_w"], LANES)
    nt = seq // tt

    def body(tab_ref, dy_ref, xs_ref, halo_ref, skip_ref, bbd_ref, cdt_ref, lam16_ref, du_ref, dar_ref, dai_ref,
             carry_ref, g_ref, lam_ref):
        t = pl.program_id(1)

        @pl.when(t == 0)
        def _():
            carry_ref[...] = jnp.zeros_like(carry_ref)

        @pl.when((pl.program_id(0) == 0) & (t == 0))
        def _():
            dar_ref[...] = jnp.zeros_like(dar_ref)
            dai_ref[...] = jnp.zeros_like(dai_ref)

        def put_g(cols, val):
            g_ref[0, :, cols] = val

        _band_expand(dy_ref[0], cdt_ref, put_g, c, gp)

        has_prev = (t < nt - 1).astype(F32)
        row0 = lax.broadcasted_iota(jnp.int32, (SUBLANES, cw), 0) == 0
        last = SUBLANES - 1

        for ch in range(gp // cw):
            re, im = pl.ds(ch * cw, cw), pl.ds(gp + ch * cw, cw)

            def step(rows, xm1r, xm1i, state, re=re, im=im):
                tabs = [tab_ref[pl.ds(SUBLANES * k, SUBLANES), re] for k in range(8)]
                cr, ci, accr, acci = state
                lr, li = g_ref[0, rows, re], g_ref[0, rows, im]
                for j, d in enumerate((1, 2, 4)):
                    lr, li = _cfma(lr, li, tabs[2 * j], tabs[2 * j + 1],
                                   pltpu.roll(lr, SUBLANES - d, 0), pltpu.roll(li, SUBLANES - d, 0))
                lr, li = _cfma(lr, li, tabs[6], tabs[7], cr, ci)
                lam_ref[0, rows, re] = lr
                lam_ref[0, rows, im] = li
                xr, xi = xs_ref[0, rows, re], xs_ref[0, rows, im]
                xpr = jnp.where(row0, jnp.broadcast_to(xm1r[last:, :], xr.shape), pltpu.roll(xr, 1, 0))
                xpi = jnp.where(row0, jnp.broadcast_to(xm1i[last:, :], xi.shape), pltpu.roll(xi, 1, 0))
                accr = accr + (lr * xpr + li * xpi)
                acci = acci + (li * xpr - lr * xpi)
                return (jnp.broadcast_to(lr[:1, :], lr.shape), jnp.broadcast_to(li[:1, :], li.shape), accr, acci)

            def blk(k, state, re=re, im=im, step=step):
                r = nblk - 1 - k
                rows = pl.ds(pl.multiple_of(r * SUBLANES, SUBLANES), SUBLANES)
                prev = pl.ds(pl.multiple_of((r - 1) * SUBLANES, SUBLANES), SUBLANES)
                return step(rows, xs_ref[0, prev, re], xs_ref[0, prev, im], state)

            zero = jnp.zeros((SUBLANES, cw), F32)
            state = lax.fori_loop(0, nblk - 1, blk, (carry_ref[:, re], carry_ref[:, im], zero, zero))
            cr, ci, accr, acci = step(pl.ds(0, SUBLANES), has_prev * halo_ref[0, :, re], has_prev * halo_ref[0, :, im], state)
            carry_ref[:, re] = cr
            carry_ref[:, im] = ci
            dar_ref[:, re] += accr
            dai_ref[:, re] += acci

        lam16_ref[0] = lam_ref[0].astype(BF16)
        du = _band_contract(lambda cols: lam16_ref[0, :, cols], bbd_ref, c, gp)
        du_ref[0] = (du + skip_ref[0]).astype(BF16)

    tile = pl.BlockSpec((1, tt, w), lambda b, t: (b, nt - 1 - t, 0))
    thin = pl.BlockSpec((1, tt, c), lambda b, t: (b, nt - 1 - t, 0))
    halo = pl.BlockSpec((1, SUBLANES, w), lambda b, t: (b, jnp.maximum((nt - 1 - t) * nblk - 1, 0), 0))
    acc = pl.BlockSpec((SUBLANES, gp), lambda b, t: (0, 0))
    whole = lambda arr: pl.BlockSpec(arr.shape, lambda b, t: (0, 0), pipeline_mode=pl.Buffered(1))
    return _call(
        name, body, (bsz, nt), [whole(tab), thin, tile, halo, thin, whole(bbd), whole(cdt)], [tile, thin, acc, acc],
        [_sds(xs3.shape, BF16), _sds((bsz, seq, c), BF16), _sds((SUBLANES, gp), F32), _sds((SUBLANES, gp), F32)],
        (tab, dy3, xs3, xs3, du_skip3, bbd, cdt), ("arbitrary", "arbitrary"),
        scratch=[pltpu.VMEM((SUBLANES, w), F32), pltpu.VMEM((1, tt, w), F32), pltpu.VMEM((1, tt, w), F32)],
        exchange=exchange)


def _gelu_parts(y):
    inner = _GELU_K * (y + _GELU_C * y * y * y)
    t = jnp.tanh(inner)
    return 0.5 * y * (1.0 + t), t


def _ssm_out_fwd(name, cx, proj, u_block, d_skip, glu_w, glu_b, out_g, x, conv_out, w_out):
    n, c = cx.shape
    d = x.shape[1]
    tm = _pick(n, TILE["row"], 16)

    def body(cx_ref, u_ref, d_ref, gw_ref, gb_ref, og_ref, x_ref, a_ref, wo_ref, y_ref, o_ref, xo_ref):
        y = cx_ref[...] + d_ref[...] * u_ref[...]
        y_ref[...] = y
        gy, _ = _gelu_parts(y)
        z = _dot(gy.astype(BF16), gw_ref[...]) + gb_ref[...]
        _, sh = _rms_stats(gy * _sigmoid(z))
        out = (sh * og_ref[...]).astype(BF16)
        o_ref[...] = out
        xo_ref[...] = x_ref[...] + _dot(a_ref[...], wo_ref[pl.ds(0, c), :]) + _dot(out, wo_ref[pl.ds(c, c), :])

    vec = pl.BlockSpec((1, c), lambda i: (0, 0))
    row = pl.BlockSpec((tm, c), lambda i: (i, 0))
    wide = pl.BlockSpec((tm, d), lambda i: (i, 0))
    held = lambda arr: pl.BlockSpec(arr.shape, lambda i: (0, 0), pipeline_mode=pl.Buffered(1))
    return pl.pallas_call(
        body, name=name, grid=(n // tm,),
        in_specs=[row, pl.BlockSpec((tm, c), lambda i: (i, u_block)), vec, held(glu_w), vec, vec, wide, row, held(w_out)],
        out_specs=[row, row, wide],
        out_shape=[_sds((n, c), F32), _sds((n, c), BF16), _sds((n, d), F32)],
        compiler_params=_cparams("parallel"),
    )(cx, proj, d_skip, glu_w, glu_b, out_g, x, conv_out, w_out)


def _ssm_out_bwd(name, dmix, d_block, y, proj, u_block, d_skip, glu_w, glu_b, out_g):
    n, c = y.shape
    tm = _pick(n, TILE["row"], 16)

    def body(d_ref, y_ref, u_ref, dk_ref, gw_ref, gb_ref, og_ref, dy_ref, du_ref, dgw_ref, dgb_ref, dog_ref, dd_ref):
        @pl.when(pl.program_id(0) == 0)
        def _():
            for r in (dgw_ref, dgb_ref, dog_ref, dd_ref):
                r[...] = jnp.zeros_like(r)

        yv = y_ref[...]
        gy, th = _gelu_parts(yv)
        gy16 = gy.astype(BF16)
        sz = _sigmoid(_dot(gy16, gw_ref[...]) + gb_ref[...])
        r, sh = _rms_stats(gy * sz)
        dout = d_ref[...]
        dog_ref[...] += jnp.sum(dout * sh, axis=0, keepdims=True)
        dsh = dout * og_ref[...]
        ds = r * (dsh - sh * jnp.mean(dsh * sh, axis=-1, keepdims=True))
        dz = ds * gy * sz * (1.0 - sz)
        dz16 = dz.astype(BF16)
        dgb_ref[...] += jnp.sum(dz, axis=0, keepdims=True)
        dgw_ref[...] += _dot_tn(gy16, dz16)
        dgy = ds * sz + _dot_nt(dz16, gw_ref[...])
        dgelu = 0.5 * (1.0 + th) + 0.5 * yv * (1.0 - th * th) * (_GELU_K * (1.0 + 3.0 * _GELU_C * yv * yv))
        dy = dgy * dgelu
        dy_ref[...] = dy.astype(BF16)
        du_ref[...] = dy * dk_ref[...]
        dd_ref[...] += jnp.sum(dy * u_ref[...], axis=0, keepdims=True)

    vec = pl.BlockSpec((1, c), lambda i: (0, 0))
    row = pl.BlockSpec((tm, c), lambda i: (i, 0))
    mat = pl.BlockSpec(glu_w.shape, lambda i: (0, 0))
    return pl.pallas_call(
        body, name=name, grid=(n // tm,),
        in_specs=[pl.BlockSpec((tm, c), lambda i: (i, d_block)), row, pl.BlockSpec((tm, c), lambda i: (i, u_block)),
                  vec, mat, vec, vec],
        out_specs=[row, row, mat, vec, vec, vec],
        out_shape=[_sds((n, c), BF16), _sds((n, c), F32), _sds(glu_w.shape, F32)] + [_sds((1, c), F32)] * 3,
        compiler_params=_cparams("arbitrary"),
    )(dmix, y, proj, d_skip, glu_w, glu_b, out_g)


def _mesh_pos():
    return tuple(lax.axis_index(a) for a in MESH_AXES)


def _other_chips(x, y):
    return [(1 - x, y), (x, 1 - y), (1 - x, 1 - y)]


def _remote(src, dst, send_sem, recv_sem, dev):
    return pltpu.make_async_remote_copy(src_ref=src, dst_ref=dst, send_sem=send_sem, recv_sem=recv_sem,
                                        device_id=dev, device_id_type=pl.DeviceIdType.MESH)


def _hbm_call(name, body, operands, out_shapes, scratch):
    hbm = pl.BlockSpec(memory_space=pltpu.HBM)
    return pl.pallas_call(body, name=name, in_specs=[hbm] * len(operands), out_specs=[hbm] * len(out_shapes),
                          out_shape=out_shapes, scratch_shapes=scratch)(*operands)


_Exchange = collections.namedtuple("_Exchange", "operands out_shapes scratch start finish")


def _run_exchange(name, plan):
    nin, nout = len(plan.operands), len(plan.out_shapes)

    def body(*refs):
        parts = refs[:nin], refs[nin:nin + nout], refs[nin + nout:]
        plan.start(*parts)
        plan.finish(*parts)

    return _hbm_call(name, body, plan.operands, plan.out_shapes, plan.scratch)


def _gather_plan(blocks):
    nop = len(blocks)

    def copies(x_refs, o_refs, sems):
        send_sems, recv_sems, local_sems = sems
        x, y, c = _mesh_pos()
        me, sibling = (x, y, c), (x, y, 1 - c)
        chips = _other_chips(x, y)

        def copy(i, k, block_of, to, src=None):
            dst = o_refs[i].at[4 * block_of[0] + 2 * block_of[1] + block_of[2]]
            return _remote(dst if src is None else src, dst, send_sems.at[i, k], recv_sems.at[i, k], to)

        own = [pltpu.make_async_copy(x_refs[i], o_refs[i].at[4 * x + 2 * y + c], local_sems.at[i]) for i in range(nop)]
        first = []
        for i in range(nop):
            first.append(copy(i, 0, me, sibling, src=x_refs[i]))
            first += [copy(i, 1 + j, me, (*chip, c), src=x_refs[i]) for j, chip in enumerate(chips)]
        return copy, own, first, me, sibling, chips, c

    def start(x_refs, o_refs, sems):
        _, own, first, *_ = copies(x_refs, o_refs, sems)
        for cp in own + first:
            cp.start()

    def finish(x_refs, o_refs, sems):
        copy, own, first, me, sibling, chips, c = copies(x_refs, o_refs, sems)
        passed = []
        for i in range(nop):
            for j, chip in enumerate(chips):
                copy(i, 1 + j, (*chip, c), me).wait_recv()
                passed.append(copy(i, 4 + j, (*chip, c), sibling))
                passed[-1].start()
        for i in range(nop):
            copy(i, 0, sibling, me).wait_recv()
            for j, chip in enumerate(chips):
                copy(i, 4 + j, (*chip, 1 - c), me).wait_recv()
        for cp in first + passed:
            cp.wait_send()
        for cp in own:
            cp.wait()

    return _Exchange(list(blocks), [_sds((N_DEV,) + b.shape, b.dtype) for b in blocks],
                     [pltpu.SemaphoreType.DMA((nop, N_DEV - 1)), pltpu.SemaphoreType.DMA((nop, N_DEV - 1)),
                      pltpu.SemaphoreType.DMA((nop,))], start, finish)


def _core_exchange_plan(grads):
    nop = len(grads)

    def copies(x_refs, o_refs, sems):
        send_sems, recv_sems = sems
        x, y, c = _mesh_pos()
        return [_remote(x_refs[i].at[2 * q + (1 - c)], o_refs[i].at[q], send_sems.at[i, q], recv_sems.at[i, q],
                        (x, y, 1 - c)) for i in range(nop) for q in range(N_DEV // 2)]

    def start(x_refs, o_refs, sems):
        for cp in copies(x_refs, o_refs, sems):
            cp.start()

    def finish(x_refs, o_refs, sems):
        for cp in copies(x_refs, o_refs, sems):
            cp.wait()

    return _Exchange(list(grads), [_sds((N_DEV // 2,) + g.shape[1:], g.dtype) for g in grads],
                     [pltpu.SemaphoreType.DMA((nop, N_DEV // 2)), pltpu.SemaphoreType.DMA((nop, N_DEV // 2))],
                     start, finish)


def _pair_sum(name, grad, other):
    nchip, _, r, c = grad.shape
    tr = _pick(r, max(SUBLANES, TILE["sum_bytes"] // (4 * c)), SUBLANES)
    core = lax.axis_index("c").astype(jnp.int32).reshape(1)

    def body(core_ref, g_ref, o_ref, s_ref):
        s_ref[0] = (g_ref[0, 0] + o_ref[0]).astype(s_ref.dtype)

    tile = pl.BlockSpec((1, tr, c), lambda q, t, core_ref: (q, t, 0))
    return pl.pallas_call(
        body, name=name,
        grid_spec=pltpu.PrefetchScalarGridSpec(
            num_scalar_prefetch=1, grid=(nchip, r // tr),
            in_specs=[pl.BlockSpec((1, 1, tr, c), lambda q, t, core_ref: (q, core_ref[0], t, 0)), tile],
            out_specs=tile),
        out_shape=_sds((nchip, r, c), BF16),
        compiler_params=_cparams("parallel", "parallel"),
    )(core, grad, other)


def _chip_exchange_plan(sums):
    nop = len(sums)

    def copies(x_refs, o_refs, sems, arriving):
        send_sems, recv_sems, local_sems = sems
        x, y, c = _mesh_pos()
        mine = 2 * x + y
        out = []
        for i in range(nop):
            for j, (px, py) in enumerate(_other_chips(x, y)):
                theirs = 2 * px + py
                src, dst = (mine, theirs) if arriving else (theirs, mine)
                out.append(_remote(x_refs[i].at[src], o_refs[i].at[dst], send_sems.at[i, j], recv_sems.at[i, j],
                                   (px, py, c)))
        if not arriving:
            out += [pltpu.make_async_copy(x_refs[i].at[mine], o_refs[i].at[mine], local_sems.at[i]) for i in range(nop)]
        return out

    def start(x_refs, o_refs, sems):
        for cp in copies(x_refs, o_refs, sems, False):
            cp.start()

    def finish(x_refs, o_refs, sems):
        for cp in copies(x_refs, o_refs, sems, True):
            cp.wait_recv()
        mine = copies(x_refs, o_refs, sems, False)
        for cp in mine[:3 * nop]:
            cp.wait_send()
        for cp in mine[3 * nop:]:
            cp.wait()

    return _Exchange(list(sums), [_sds(s.shape, s.dtype) for s in sums],
                     [pltpu.SemaphoreType.DMA((nop, 3)), pltpu.SemaphoreType.DMA((nop, 3)), pltpu.SemaphoreType.DMA((nop,))],
                     start, finish)


def _part_rows(npart, r, c):
    return _pick(r, max(SUBLANES, TILE["sum_bytes"] // (4 * npart * c)), SUBLANES)


def _sum_slots(p_ref):
    g = p_ref[0].astype(F32)
    for k in range(1, p_ref.shape[0]):
        g = g + p_ref[k].astype(F32)
    return g


def _adamw_step(g, w, m, v):
    c1 = 1.0 - ADAM_B1 ** ADAM_STEP
    c2 = 1.0 - ADAM_B2 ** ADAM_STEP
    nm = ADAM_B1 * m + (1.0 - ADAM_B1) * g
    nv = ADAM_B2 * v + (1.0 - ADAM_B2) * (g * g)
    return -ADAM_LR * ((nm / c1) / (jnp.sqrt(nv / c2) + ADAM_EPS) + ADAM_WD * w), nm, nv


def _adamw_small(name, parts, ws, ms, vs):
    nparam, nall = len(ws), len(parts)

    def body(*refs):
        p_refs = refs[:nall]
        w_refs, m_refs, v_refs = (refs[nall + k * nparam:nall + (k + 1) * nparam] for k in range(3))
        outs = refs[nall + 3 * nparam:]
        for p in range(nall):
            g = _sum_slots(p_refs[p])
            if p < nparam:
                delta, nm, nv = _adamw_step(g, w_refs[p][...], m_refs[p][...], v_refs[p][...])
                for o_ref, val in zip(outs[4 * p:4 * p + 4], (g, delta, nm, nv)):
                    o_ref[...] = val
            else:
                outs[4 * nparam + p - nparam][...] = g

    shapes = [_sds(w.shape, F32) for w in ws for _ in range(4)] + [_sds(p.shape[1:], F32) for p in parts[nparam:]]
    res = pl.pallas_call(body, name=name, out_shape=shapes,
                         compiler_params=pltpu.CompilerParams(vmem_limit_bytes=VMEM_LIMIT))(*parts, *ws, *ms, *vs)
    return [res[4 * p:4 * p + 4] for p in range(nparam)] + [[r] for r in res[4 * nparam:]]


def _adamw(name, parts, w, m, v):
    npart, r, c = parts.shape
    lead = len(w.shape) - 2
    tr = _part_rows(npart, r, c)
    at = (0,) * lead + (slice(None), slice(None))

    def body(p_ref, w_ref, m_ref, v_ref, g_ref, d_ref, nm_ref, nv_ref):
        g = _sum_slots(p_ref)
        delta, nm, nv = _adamw_step(g, w_ref[at], m_ref[at], v_ref[at])
        g_ref[at] = g
        nm_ref[at] = nm
        nv_ref[at] = nv
        d_ref[at] = delta

    row = pl.BlockSpec((1,) * lead + (tr, c), lambda i: (0,) * lead + (i, 0))
    return pl.pallas_call(
        body, name=name, grid=(r // tr,),
        in_specs=[pl.BlockSpec((npart, tr, c), lambda i: (0, i, 0)), row, row, row],
        out_specs=[row] * 4,
        out_shape=[_sds(w.shape, F32)] * 4,
        compiler_params=_cparams("parallel"),
    )(parts, w, m, v)


def _block_diag(rows_gh, groups):
    gh, p = rows_gh.shape
    own = (jnp.arange(gh)[:, None] // (gh // groups) == jnp.arange(groups)[None, :]).astype(rows_gh.dtype)
    return (own[:, :, None] * rows_gh[:, None, :]).reshape(gh, groups * p)


def _block_diag_take(dense, groups):
    gh = dense.shape[0]
    p = dense.shape[1] // groups
    own = (jnp.arange(gh)[:, None] // (gh // groups) == jnp.arange(groups)[None, :]).astype(dense.dtype)
    return jnp.sum(dense.reshape(gh, groups, p) * own[:, :, None], axis=1)


FFN1 = ("ffn1_w1", "ffn1_w3", "ffn1_w2")
MIXER = ("w_in", "ssm_glu_w", "w_out")
FFN2 = ("ffn2_w1", "ffn2_w3", "ffn2_w2")
BIG = FFN1 + MIXER + FFN2
COL_SHARDED = ("ffn1_w1", "ffn1_w3", "w_in", "ffn2_w1", "ffn2_w3", "conv_w")
SMALL = ("norm_ffn1", "norm_mix", "conv_b", "conv_ln_g", "conv_ln_b", "conv_out_g", "ssm_A_re", "ssm_A_im",
         "ssm_log_dt", "ssm_B_re", "ssm_B_im", "ssm_C_re", "ssm_C_im", "ssm_D", "ssm_glu_b", "ssm_out_g",
         "norm_ffn2", "norm_final")
WEIGHTS = ("norm_ffn1", "ffn1_w1", "ffn1_w3", "ffn1_w2", "norm_mix", "w_in", "conv_w", "conv_b", "conv_ln_g",
           "conv_ln_b", "conv_out_g", "ssm_A_re", "ssm_A_im", "ssm_log_dt", "ssm_B_re", "ssm_B_im", "ssm_C_re",
           "ssm_C_im", "ssm_D", "ssm_glu_w", "ssm_glu_b", "ssm_out_g", "w_out", "norm_ffn2", "ffn2_w1", "ffn2_w3",
           "ffn2_w2", "norm_final")


def _ffn_backward(tag, dxo, x, g, w1, w3, w2, saved, exchange=None, dw_exchange=None, reduce_names=None):
    a, b, h = saved
    (da, db, hid, dxh), got = _ffn_bwd_hidden(tag + "_bwd_hidden", dxo, a, b, w2, exchange=exchange)
    dw1, dw_got = _mm_tn(tag + "_dw1", da, h, exchange=dw_exchange) if dw_exchange else (_mm_tn(tag + "_dw1", da, h), None)
    dw3 = _mm_tn(tag + "_dw3", db, h)
    across = None
    if reduce_names:
        send = [_row_blocks(dw1), _row_blocks(dw3)]
        dw2, from_core = _mm_tn(tag + "_dw2", hid, dxh, exchange=_core_exchange_plan(send))
        send.append(_row_blocks(dw2))
        from_core += _run_exchange("exchange_core_" + tag, _core_exchange_plan(send[2:]))
        across = _across_chips(reduce_names, send, from_core)
    else:
        dw2 = _mm_tn(tag + "_dw2", hid, dxh)
    f = a.shape[1]
    (dx, dg), reduced = _dx_rms_bwd(tag + "_bwd_dx", [(da, f, 0, w1, f, 0), (db, f, 0, w3, f, 0)], dxo, x, g,
                                    exchange=across)
    return (dx, dg, [dw1, dw3, dw2]), got, dw_got, reduced


def _row_blocks(grad):
    return grad.reshape((N_DEV, -1) + grad.shape[1:])


def _across_chips(names, send, from_core):
    return _chip_exchange_plan([_pair_sum("pair_sum_" + k, s.reshape((N_DEV // 2, 2) + s.shape[1:]), o)
                                for k, s, o in zip(names, send, from_core)])


def _reduce_in_chip(names, grads):
    send = [_row_blocks(g) for g in grads]
    return _core_exchange_plan(send), functools.partial(_across_chips, names, send)


def kernel(x, norm_ffn1, ffn1_w1, ffn1_w3, ffn1_w2, norm_mix, w_in, conv_w, conv_b, conv_ln_g, conv_ln_b, conv_out_g, ssm_A_re, ssm_A_im, ssm_log_dt, ssm_B_re, ssm_B_im, ssm_C_re, ssm_C_im, ssm_D, ssm_glu_w, ssm_glu_b, ssm_out_g, w_out, norm_ffn2, ffn2_w1, ffn2_w3, ffn2_w2, norm_final, loss_target, m_norm_ffn1, m_ffn1_w1, m_ffn1_w3, m_ffn1_w2, m_norm_mix, m_w_in, m_conv_w, m_conv_b, m_conv_ln_g, m_conv_ln_b, m_conv_out_g, m_ssm_A_re, m_ssm_A_im, m_ssm_log_dt, m_ssm_B_re, m_ssm_B_im, m_ssm_C_re, m_ssm_C_im, m_ssm_D, m_ssm_glu_w, m_ssm_glu_b, m_ssm_out_g, m_w_out, m_norm_ffn2, m_ffn2_w1, m_ffn2_w3, m_ffn2_w2, m_norm_final, v_norm_ffn1, v_ffn1_w1, v_ffn1_w3, v_ffn1_w2, v_norm_mix, v_w_in, v_conv_w, v_conv_b, v_conv_ln_g, v_conv_ln_b, v_conv_out_g, v_ssm_A_re, v_ssm_A_im, v_ssm_log_dt, v_ssm_B_re, v_ssm_B_im, v_ssm_C_re, v_ssm_C_im, v_ssm_D, v_ssm_glu_w, v_ssm_glu_b, v_ssm_out_g, v_w_out, v_norm_ffn2, v_ffn2_w1, v_ffn2_w3, v_ffn2_w2, v_norm_final):
    args = dict(locals())
    wt = {n: args[n] for n in WEIGHTS}
    mom = {n: args["m_" + n] for n in WEIGHTS}
    var = {n: args["v_" + n] for n in WEIGHTS}

    bsz, seq, d = x.shape
    n = bsz * seq
    c = conv_b.shape[-1]
    groups = c // SSM_GROUP
    gp = groups * SSM_STATE
    u_b = 2

    shard = {k: (wt[k][0].T if k in COL_SHARDED else wt[k][0]).astype(BF16) for k in BIG}
    gathered = _run_exchange("gather_weights_ffn1", _gather_plan([shard[k] for k in FFN1]))
    full = {k: g.reshape(-1, g.shape[-1]) for k, g in zip(FFN1, gathered)}
    gather_rest = _gather_plan([shard[k] for k in MIXER + FFN2] + [wt["conv_w"][0]])

    vec = lambda k: wt[k].reshape(1, -1)
    g_ffn1, g_mix, g_ffn2, g_fin = vec("norm_ffn1"), vec("norm_mix"), vec("norm_ffn2"), vec("norm_final")
    cb, lng, lnb, cog = vec("conv_b"), vec("conv_ln_g"), vec("conv_ln_b"), vec("conv_out_g")
    d_skip, glu_b, sog = vec("ssm_D"), vec("ssm_glu_b"), vec("ssm_out_g")

    a_re, a_im = wt["ssm_A_re"][0], wt["ssm_A_im"][0]
    log_dt = wt["ssm_log_dt"][0].reshape(groups, 1)
    bt_re = wt["ssm_B_re"][0].transpose(0, 2, 1).reshape(groups * SSM_GROUP, SSM_STATE)
    bt_im = wt["ssm_B_im"][0].transpose(0, 2, 1).reshape(groups * SSM_GROUP, SSM_STATE)
    c_re = wt["ssm_C_re"][0].reshape(groups * SSM_GROUP, SSM_STATE)
    c_im = wt["ssm_C_im"][0].reshape(groups * SSM_GROUP, SSM_STATE)
    per_chan = lambda t: jnp.repeat(t, SSM_GROUP, axis=0)
    ssm_prim = (a_re, a_im, log_dt, per_chan(a_re), per_chan(a_im), per_chan(jnp.broadcast_to(log_dt, a_re.shape)),
                bt_re, bt_im)
    pw_r, pw_i, bb_r, bb_i = _ssm_prep("ssm_prep", ssm_prim)
    tab_f = _scan_tables(pw_r, pw_i, False)
    tab_b = _scan_tables(pw_r, pw_i, True)
    bbd = jnp.concatenate([_block_diag(bb_r, groups), _block_diag(bb_i, groups)], axis=1).astype(BF16)
    cdt = jnp.concatenate([_block_diag(c_re, groups), -_block_diag(c_im, groups)], axis=1).astype(BF16)

    x0 = x.reshape(n, d)
    (x1, *ffn1_saved), gathered = _ffn_fwd("ffn1_fwd", x0, g_ffn1, full["ffn1_w1"], full["ffn1_w3"], full["ffn1_w2"],
                                           exchange=gather_rest)
    full.update({k: g.reshape(-1, g.shape[-1]) for k, g in zip(MIXER + FFN2, gathered)})
    conv_w_full = gathered[-1].transpose(1, 0, 2).reshape(CONV_WIDTH, c)
    conv_w_pad = jnp.pad(conv_w_full, ((0, CONV_HALO - CONV_WIDTH), (0, 0)))
    (proj,), h2 = _rms_mm("mix_in", x1, g_mix, [full["w_in"]], F32)
    proj3 = proj.reshape(bsz, seq, 3 * c)
    an3, cv3 = _conv_fwd("conv_fwd", proj3, conv_w_pad, cb, lng, lnb, cog)
    an = an3.reshape(n, c)
    xs3, xs16, cx3 = _scan_fwd("scan_fwd", tab_f, proj3, u_b, bbd, cdt)
    w_o = full["w_out"]
    y, sn, x2 = _ssm_out_fwd("ssm_out_fwd", cx3.reshape(n, c), proj, u_b, d_skip, full["ssm_glu_w"], glu_b, sog,
                             x1, an, w_o)
    (dx3, *ffn2_saved, loss_tile, d_gfin), _ = _ffn_fwd(
        "ffn2_fwd", x2, g_ffn2, full["ffn2_w1"], full["ffn2_w3"], full["ffn2_w2"],
        head=(g_fin, loss_target.reshape(n, d)))
    loss = lax.psum(loss_tile[0, 0], MESH_AXES)

    grads, from_chips = {}, {}
    (dx2, grads["norm_ffn2"], dws), _, _, _ = _ffn_backward(
        "ffn2", dx3, x2, g_ffn2, full["ffn2_w1"], full["ffn2_w3"], full["ffn2_w2"], ffn2_saved)
    in_chip, across_chips = _reduce_in_chip(FFN2, dws)

    dmix, got = _row_mm("mix_out_bwd", [(dx2, d, 0, w_o, 2 * c, 0, True)], 2 * c, F32, exchange=in_chip)
    reduce_ffn2 = across_chips(got)
    grads["w_out"] = jnp.concatenate([_mm_tn("dw_out_a", an, dx2), _mm_tn("dw_out_s", sn, dx2)], axis=0)

    dy, du_skip, grads["ssm_glu_w"], grads["ssm_glu_b"], grads["ssm_out_g"], grads["ssm_D"] = _ssm_out_bwd(
        "ssm_out_bwd", dmix, 1, y, proj, u_b, d_skip, full["ssm_glu_w"], glu_b, sog)
    (lam3, du3, dab_r, dab_i), got = _scan_bwd("scan_bwd", tab_b, dy.reshape(bsz, seq, c), xs3,
                                               du_skip.reshape(bsz, seq, c), bbd, cdt, exchange=reduce_ffn2)
    from_chips.update(zip(FFN2, got))
    lam, du = lam3.reshape(n, 2 * gp), du3.reshape(n, c)
    d_bbd = _band_wgrad("ssm_dbb", proj, u_b, c, lam)
    d_cdt = _band_wgrad("ssm_dc", dy, 0, c, xs16.reshape(n, 2 * gp))
    d_are, d_aim, d_ldt, d_btr, d_bti = _ssm_param_grads(
        "ssm_param_grads", ssm_prim,
        dab_r.reshape(SUBLANES, groups, SSM_STATE), dab_i.reshape(SUBLANES, groups, SSM_STATE),
        _band_diag_take(d_bbd, 0, c, gp), _band_diag_take(d_bbd, 1, c, gp))
    grads["ssm_A_re"], grads["ssm_A_im"], grads["ssm_log_dt"] = d_are, d_aim, d_ldt
    grads["ssm_B_re"], grads["ssm_B_im"] = d_btr, d_bti
    grads["ssm_C_re"] = _band_diag_take(d_cdt, 0, c, gp)
    grads["ssm_C_im"] = -_band_diag_take(d_cdt, 1, c, gp)

    dconv3, d_cw, grads["conv_b"], grads["conv_ln_g"], grads["conv_ln_b"], grads["conv_out_g"] = _conv_bwd(
        "conv_bwd", dmix.reshape(bsz, seq, 2 * c), proj3, cv3, conv_w_pad, lng, lnb, cog)
    dconv = dconv3.reshape(n, 2 * c)
    grads["conv_w"] = d_cw[:CONV_WIDTH]
    grads["w_in"] = jnp.concatenate([_mm_tn("dw_in_conv", dconv, h2), _mm_tn("dw_in_ssm", du, h2)], axis=0)
    w_i = full["w_in"]
    in_chip, across_chips = _reduce_in_chip(MIXER, [grads[k] for k in MIXER])
    (dx1, grads["norm_mix"]), got = _dx_rms_bwd("mix_in_bwd", [(dconv, 2 * c, 0, w_i, 2 * c, 0), (du, c, 0, w_i, c, 2)],
                                                dx2, x1, g_mix, exchange=in_chip)
    reduce_mixer = across_chips(got)

    grads["norm_final"] = d_gfin
    early = tuple(k for k in SMALL if k != "norm_ffn1")
    gather_small = _gather_plan([grads[k] for k in early] + [grads["conv_w"]])

    (dx0, grads["norm_ffn1"], _), got, small_parts, reduced = _ffn_backward(
        "ffn1", dx1, x0, g_ffn1, full["ffn1_w1"], full["ffn1_w3"], full["ffn1_w2"], ffn1_saved,
        exchange=reduce_mixer, dw_exchange=gather_small, reduce_names=FFN1)
    from_chips.update(zip(MIXER, got))
    from_chips.update(zip(FFN1, reduced))

    res = {}
    for k in BIG:
        parts = from_chips[k]
        if k in COL_SHARDED:
            swap = lambda t: jnp.swapaxes(t, -1, -2)
            res[k] = [swap(t) for t in _adamw("adamw_" + k, parts, swap(wt[k]), swap(mom[k]), swap(var[k]))]
        else:
            res[k] = _adamw("adamw_" + k, parts, wt[k], mom[k], var[k])

    def as_2d(k, t):
        if k in ("ssm_B_re", "ssm_B_im"):
            return t[0].transpose(0, 2, 1).reshape(-1, SSM_STATE)
        if k in ("ssm_C_re", "ssm_C_im"):
            return t[0].reshape(-1, SSM_STATE)
        if k in ("ssm_A_re", "ssm_A_im"):
            return t[0]
        return t.reshape(-1, 1) if k == "ssm_log_dt" else t.reshape(1, -1)

    def as_param(k, t):
        if k in ("ssm_B_re", "ssm_B_im"):
            t = t.reshape(groups, SSM_GROUP, SSM_STATE).transpose(0, 2, 1)
        return t.reshape(wt[k].shape)

    (last_part,) = _run_exchange("gather_norm_ffn1_grad", _gather_plan([grads["norm_ffn1"]]))
    order = ("norm_ffn1",) + early
    updated = _adamw_small("adamw_replicated", [last_part] + small_parts,
                           *[[as_2d(k, src[k]) for k in order] for src in (wt, mom, var)])
    res.update({k: [as_param(k, t) for t in upd] for k, upd in zip(order, updated)})
    (conv_w_grad,) = updated[-1]
    x_pos, y_pos, c_pos = (lax.axis_index(a) for a in MESH_AXES)
    cw_cols = c // N_DEV
    own_cw = lax.dynamic_slice_in_dim(conv_w_grad, (4 * x_pos + 2 * y_pos + c_pos) * cw_cols, cw_cols, axis=1)
    res["conv_w"] = _adamw("adamw_conv_w", own_cw[None], wt["conv_w"], mom["conv_w"], var["conv_w"])

    outs = [loss, dx0.reshape(bsz, seq, d)]
    for kind in range(4):
        outs += [res[k][kind] for k in WEIGHTS]
    return tuple(outs)
```

```python
import collections
import functools
import math

import jax
import jax.numpy as jnp
from jax import lax
from jax.experimental import pallas as pl
from jax.experimental.pallas import tpu as pltpu

F32 = jnp.float32
BF16 = jnp.bfloat16

EPS = 1e-6
FFN_RES = 0.5
CONV_WIDTH = 31
CONV_HALO = 32
SSM_GROUP = 16
SSM_STATE = 64
ADAM_LR, ADAM_B1, ADAM_B2, ADAM_EPS, ADAM_WD, ADAM_STEP = 0.001, 0.9, 0.999, 1e-08, 0.01, 10

N_DEV = 8
MESH_AXES = ("x", "y", "c")
SUBLANES = 8
LANES = 128
V7X_VMEM_BYTES = 64 * 2**20
VMEM_LIMIT = V7X_VMEM_BYTES - 8 * 2**20

TILE = dict(row=512, hid_m=512, ffn_m=512, mm_bytes=8 * 2**20, up_m=1024, up_n=256, wide_n=2048, conv_t=512,
            scan_fwd_t=512, scan_t=256, scan_w=512, sum_bytes=4 * 2**20)

_GELU_K = math.sqrt(2.0 / math.pi)
_GELU_C = 0.044715


def _pick(n, target, mult):
    best = None
    for t in range(mult, min(n, target) + 1, mult):
        if n % t == 0:
            best = t
    return n if best is None else best


def _cparams(*sem):
    return pltpu.CompilerParams(dimension_semantics=sem, vmem_limit_bytes=VMEM_LIMIT)


def _sds(shape, dtype):
    return jax.ShapeDtypeStruct(shape, dtype)


def _call(name, body, grid, in_specs, out_specs, out_shape, operands, sem, scratch=(), exchange=None):
    if exchange is None:
        res = pl.pallas_call(body, name=name, grid=grid, in_specs=list(in_specs), out_specs=list(out_specs),
                             out_shape=list(out_shape), scratch_shapes=list(scratch),
                             compiler_params=_cparams(*sem))(*operands)
        return list(res), None
    n_in, n_out, n_scr = len(in_specs), len(out_specs), len(scratch)
    n_xin, n_xout = len(exchange.operands), len(exchange.out_shapes)
    hbm = pl.BlockSpec(memory_space=pltpu.HBM)

    def with_exchange(*refs):
        cuts, pos = [], 0
        for size in (n_in, n_xin, n_out, n_xout, n_scr):
            cuts.append(refs[pos:pos + size])
            pos += size
        ins, x_in, outs, x_out, scr = cuts
        sems = refs[pos:]
        ids = [pl.program_id(axis) for axis in range(len(grid))]
        first = functools.reduce(lambda p, q: p & q, [i == 0 for i in ids])
        last = functools.reduce(lambda p, q: p & q, [i == g - 1 for i, g in zip(ids, grid)])

        @pl.when(first)
        def _():
            exchange.start(x_in, x_out, sems)

        body(*ins, *outs, *scr)

        @pl.when(last)
        def _():
            exchange.finish(x_in, x_out, sems)

    res = pl.pallas_call(
        with_exchange, name=name, grid=grid, in_specs=list(in_specs) + [hbm] * n_xin,
        out_specs=list(out_specs) + [hbm] * n_xout, out_shape=list(out_shape) + list(exchange.out_shapes),
        scratch_shapes=list(scratch) + list(exchange.scratch),
        compiler_params=_cparams(*["arbitrary"] * len(grid)))(*operands, *exchange.operands)
    return list(res[:n_out]), list(res[n_out:])


def _dot(a, b):
    return jnp.dot(a, b, preferred_element_type=F32)


def _dot_nt(a, b):
    return lax.dot_general(a, b, (((1,), (1,)), ((), ())), preferred_element_type=F32)


def _dot_tn(a, b):
    return lax.dot_general(a, b, (((0,), (0,)), ((), ())), preferred_element_type=F32)


def _sigmoid(x):
    return 0.5 * jnp.tanh(0.5 * x) + 0.5


def _rms_stats(x):
    r = lax.rsqrt(jnp.mean(x * x, axis=-1, keepdims=True) + EPS)
    return r, x * r


def _rms_bwd(x, g, dy):
    r, xh = _rms_stats(x)
    dxh = dy * g
    dx = r * (dxh - xh * jnp.mean(dxh * xh, axis=-1, keepdims=True))
    return dx, jnp.sum(dy * xh, axis=0, keepdims=True)


def _rms_mm(name, x, g, ws, out_dtype):
    n, d = x.shape
    f = ws[0].shape[0]
    nw = len(ws)
    tm, tn = _pick(n, TILE["up_m"], 16), _pick(f, TILE["wide_n"], LANES)

    def body(x_ref, g_ref, *refs):
        w_refs, o_refs, h_ref = refs[:nw], refs[nw:2 * nw], refs[2 * nw]

        @pl.when(pl.program_id(1) == 0)
        def _():
            _, xh = _rms_stats(x_ref[...])
            h_ref[...] = (xh * g_ref[...]).astype(BF16)

        h = h_ref[...]
        for w_ref, o_ref in zip(w_refs, o_refs):
            o_ref[...] = _dot_nt(h, w_ref[...]).astype(o_ref.dtype)

    outs = pl.pallas_call(
        body, name=name, grid=(n // tm, f // tn),
        in_specs=[pl.BlockSpec((tm, d), lambda i, j: (i, 0)), pl.BlockSpec((1, d), lambda i, j: (0, 0))]
        + [pl.BlockSpec((tn, d), lambda i, j: (j, 0))] * nw,
        out_specs=[pl.BlockSpec((tm, tn), lambda i, j: (i, j))] * nw + [pl.BlockSpec((tm, d), lambda i, j: (i, 0))],
        out_shape=[_sds((n, f), out_dtype)] * nw + [_sds((n, d), BF16)],
        compiler_params=_cparams("parallel", "arbitrary"),
    )(x, g, *ws)
    return outs[:nw], outs[nw]


def _ffn_fwd(name, x, g, w1t, w3t, w2, exchange=None, head=None):
    n, d = x.shape
    f = w2.shape[0]
    tm, tn = _pick(n, TILE["ffn_m"], 16), _pick(f, TILE["up_n"], LANES)

    def body(x_ref, g_ref, w1_ref, w3_ref, w2_ref, *refs):
        (gf_ref, t_ref), refs = (refs[:2], refs[2:]) if head else ((None, None), refs)
        o_ref, a_ref, b_ref, h_ref = refs[:4]
        xv = x_ref[...]
        _, xh = _rms_stats(xv)
        h = (xh * g_ref[...]).astype(BF16)
        h_ref[...] = h
        acc = None
        for c0 in range(0, f, tn):
            cols = pl.ds(c0, tn)
            av, bv = _dot_nt(h, w1_ref[cols, :]), _dot_nt(h, w3_ref[cols, :])
            a_ref[:, cols] = av.astype(BF16)
            b_ref[:, cols] = bv.astype(BF16)
            t = _dot((av * _sigmoid(av) * bv).astype(BF16), w2_ref[cols, :])
            acc = t if acc is None else acc + t
        out = xv + FFN_RES * acc
        if head is None:
            o_ref[...] = out
        else:
            loss_ref, dg_ref = refs[4:]

            @pl.when(pl.program_id(0) == 0)
            def _():
                loss_ref[...] = jnp.zeros_like(loss_ref)
                dg_ref[...] = jnp.zeros_like(dg_ref)

            dx, loss, dg = _loss_head_rows(out, gf_ref[...], t_ref[...])
            o_ref[...] = dx
            loss_ref[...] += loss
            dg_ref[...] += dg

    row = pl.BlockSpec((tm, d), lambda i: (i, 0))
    wide = pl.BlockSpec((tm, f), lambda i: (i, 0))
    vec = pl.BlockSpec((1, d), lambda i: (0, 0))
    held = pl.BlockSpec((f, d), lambda i: (0, 0), pipeline_mode=pl.Buffered(1))
    extra_in, extra_out, extra_shape = ([vec, row], [pl.BlockSpec((SUBLANES, LANES), lambda i: (0, 0)), vec],
                                        [_sds((SUBLANES, LANES), F32), _sds((1, d), F32)]) if head else ([], [], [])
    return _call(
        name, body, (n // tm,), [row, vec, held, held, held] + extra_in, [row, wide, wide, row] + extra_out,
        [_sds((n, d), F32), _sds((n, f), BF16), _sds((n, f), BF16), _sds((n, d), BF16)] + extra_shape,
        (x, g, w1t, w3t, w2) + (tuple(head) if head else ()), ("arbitrary",) if head else ("parallel",),
        exchange=exchange)


def _ffn_bwd_hidden(name, dxo, a, b, w2, exchange=None):
    n, d = dxo.shape
    f = a.shape[1]
    tm, tn = _pick(n, TILE["hid_m"], 16), _pick(f, TILE["up_n"], LANES)

    def body(dx_ref, a_ref, b_ref, w_ref, da_ref, db_ref, hid_ref, dxh_ref):
        dxh = (FFN_RES * dx_ref[...]).astype(BF16)
        dxh_ref[...] = dxh
        for c0 in range(0, f, tn):
            cols = pl.ds(c0, tn)
            dhid = _dot_nt(dxh, w_ref[cols, :])
            av, bv = a_ref[:, cols].astype(F32), b_ref[:, cols].astype(F32)
            sig = _sigmoid(av)
            silu = av * sig
            da_ref[:, cols] = (dhid * bv * (sig * (1.0 + av - silu))).astype(BF16)
            db_ref[:, cols] = (dhid * silu).astype(BF16)
            hid_ref[:, cols] = (silu * bv).astype(BF16)

    wide = pl.BlockSpec((tm, f), lambda i: (i, 0))
    row = pl.BlockSpec((tm, d), lambda i: (i, 0))
    return _call(
        name, body, (n // tm,),
        [row, wide, wide, pl.BlockSpec((f, d), lambda i: (0, 0), pipeline_mode=pl.Buffered(1))], [wide, wide, wide, row],
        [_sds((n, f), BF16)] * 3 + [_sds((n, d), BF16)], (dxo, a, b, w2), ("parallel",), exchange=exchange)


def _loss_head_rows(x, g, target):
    r, xh = _rms_stats(x)
    err = xh * g - target
    dy = err * (1.0 / x.shape[-1])
    dxh = dy * g
    dx = r * (dxh - xh * jnp.mean(dxh * xh, axis=-1, keepdims=True))
    return dx, 0.5 * jnp.sum(jnp.mean(err * err, axis=-1, keepdims=True)), jnp.sum(dy * xh, axis=0, keepdims=True)


def _dx_rms_bwd(name, pairs, dxo, x, g, exchange=None):
    n, dm = x.shape
    tm = _pick(n, TILE["ffn_m"], 16)
    npair = len(pairs)

    def body(*refs):
        d_refs, w_refs = refs[:npair], refs[npair:2 * npair]
        dxo_ref, x_ref, g_ref, dx_ref, dg_ref = refs[2 * npair:]

        @pl.when(pl.program_id(0) == 0)
        def _():
            dg_ref[...] = jnp.zeros_like(dg_ref)

        dh = None
        for d_ref, w_ref in zip(d_refs, w_refs):
            t = _dot(d_ref[...].astype(BF16), w_ref[...])
            dh = t if dh is None else dh + t
        dx, dg = _rms_bwd(x_ref[...], g_ref[...], dh)
        dx_ref[...] = dxo_ref[...] + dx
        dg_ref[...] += dg

    row = pl.BlockSpec((tm, dm), lambda i: (i, 0))
    d_specs = [pl.BlockSpec((tm, p[1]), functools.partial(lambda i, cb: (i, cb), cb=p[2])) for p in pairs]
    w_specs = [pl.BlockSpec((p[4], dm), functools.partial(lambda i, rb: (rb, 0), rb=p[5]), pipeline_mode=pl.Buffered(1))
               for p in pairs]
    return _call(
        name, body, (n // tm,), d_specs + w_specs + [row, row, pl.BlockSpec((1, dm), lambda i: (0, 0))],
        [row, pl.BlockSpec((1, dm), lambda i: (0, 0))], [_sds((n, dm), F32), _sds((1, dm), F32)],
        (*[p[0] for p in pairs], *[p[3] for p in pairs], dxo, x, g), ("arbitrary",), exchange=exchange)


def _mm_tn(name, a, b, exchange=None):
    parts = tuple(a) if isinstance(a, (tuple, list)) else (a,)
    n, mb = b.shape
    widths = [p.shape[1] for p in parts]
    tk = _pick(n, TILE["mm_bytes"] // (sum(p.shape[1] * p.dtype.itemsize for p in parts) + mb * b.dtype.itemsize), 16)

    def body(*refs):
        a_refs, b_ref, o_ref = refs[:-2], refs[-2], refs[-1]

        @pl.when(pl.program_id(0) == 0)
        def _():
            o_ref[...] = jnp.zeros_like(o_ref)

        bv = b_ref[...].astype(BF16)
        row0 = 0
        for a_ref, width in zip(a_refs, widths):
            o_ref[pl.ds(row0, width), :] += _dot_tn(a_ref[...].astype(BF16), bv)
            row0 += width

    (out,), got = _call(
        name, body, (n // tk,),
        [pl.BlockSpec((tk, w), lambda k: (k, 0)) for w in widths] + [pl.BlockSpec((tk, mb), lambda k: (k, 0))],
        [pl.BlockSpec((sum(widths), mb), lambda k: (0, 0))], [_sds((sum(widths), mb), F32)], (*parts, b), ("arbitrary",),
        exchange=exchange)
    return out if exchange is None else (out, got)


def _mm_nt(name, a, w, exchange=None):
    n, k = a.shape
    m = w.shape[0]
    tm = _pick(n, TILE["row"], 16)

    def body(a_ref, w_ref, o_ref):
        o_ref[...] = _dot_nt(a_ref[...].astype(BF16), w_ref[...])

    (out,), got = _call(
        name, body, (n // tm,),
        [pl.BlockSpec((tm, k), lambda i: (i, 0)), pl.BlockSpec((m, k), lambda i: (0, 0), pipeline_mode=pl.Buffered(1))],
        [pl.BlockSpec((tm, m), lambda i: (i, 0))], [_sds((n, m), F32)], (a, w), ("parallel",), exchange=exchange)
    return out, got


def _conv_post(c, ln_g, ln_b, out_g):
    mu = jnp.mean(c, axis=-1, keepdims=True)
    xc = c - mu
    rstd = lax.rsqrt(jnp.mean(xc * xc, axis=-1, keepdims=True) + EPS)
    nrm = xc * rstd
    l = nrm * ln_g + ln_b
    sig = _sigmoid(l)
    s = l * sig
    r, sh = _rms_stats(s)
    return sh * out_g, (rstd, nrm, l, sig, r, sh)


def _tap_groups(first):
    groups = []
    for r in range(SUBLANES):
        taps = [(s - r, s - first) for s in range(first, first + CONV_WIDTH) if s % SUBLANES == r]
        if taps:
            groups.append((r, taps))
    return groups


def _conv_taps(a_ref, w_ref, b_ref, first, rows, flip=False):
    acc = None
    for r, taps in _tap_groups(first):
        ext = rows if r == 0 else rows + SUBLANES
        part = None
        for base, k in taps:
            kk = CONV_WIDTH - 1 - k if flip else k
            t = w_ref[kk:kk + 1, :] * a_ref[pl.ds(base, ext), :]
            part = t if part is None else part + t
        if r:
            b_ref[...] = part
            part = b_ref[pl.ds(r, rows), :]
        acc = part if acc is None else acc + part
    return acc


def _conv_post_bwd(cv, dout, ln_g, ln_b, out_g):
    _, (rstd, nrm, l, sig, r, sh) = _conv_post(cv, ln_g, ln_b, out_g)
    dsh = dout * out_g
    ds = r * (dsh - sh * jnp.mean(dsh * sh, axis=-1, keepdims=True))
    dl = ds * (sig * (1.0 + l * (1.0 - sig)))
    dn = dl * ln_g
    dc = rstd * (dn - jnp.mean(dn, axis=-1, keepdims=True) - nrm * jnp.mean(dn * nrm, axis=-1, keepdims=True))
    col_sum = lambda t: jnp.sum(t, axis=0, keepdims=True)
    return dc, col_sum(dout * sh), col_sum(dl * nrm), col_sum(dl)


def _conv_fwd(name, proj3, conv_w, conv_b, ln_g, ln_b, out_g):
    bsz, seq, _ = proj3.shape
    c = conv_w.shape[1]
    tt = _pick(seq, TILE["conv_t"], CONV_HALO)
    hb = tt // CONV_HALO
    first = CONV_HALO - (CONV_WIDTH - 1)

    def body(v_ref, g_ref, vp_ref, gp_ref, w_ref, cb_ref, lg_ref, lb_ref, og_ref, o_ref, cv_ref, a_ref, b_ref):
        keep = (pl.program_id(1) > 0).astype(F32)
        a_ref[pl.ds(0, CONV_HALO), :] = keep * vp_ref[0] * _sigmoid(gp_ref[0])
        a_ref[pl.ds(CONV_HALO, tt), :] = v_ref[0] * _sigmoid(g_ref[0])
        cv = _conv_taps(a_ref, w_ref, b_ref, first, tt) + cb_ref[...]
        cv_ref[0] = cv
        out, _ = _conv_post(cv, lg_ref[...], lb_ref[...], og_ref[...])
        o_ref[0] = out.astype(BF16)

    vec = pl.BlockSpec((1, c), lambda b, i: (0, 0))
    prev = lambda col: pl.BlockSpec((1, CONV_HALO, c), lambda b, i: (b, jnp.maximum(i * hb - 1, 0), col))
    tile = pl.BlockSpec((1, tt, c), lambda b, i: (b, i, 0))
    return pl.pallas_call(
        body, name=name, grid=(bsz, seq // tt),
        in_specs=[tile, pl.BlockSpec((1, tt, c), lambda b, i: (b, i, 1)),
                  prev(0), prev(1), pl.BlockSpec(conv_w.shape, lambda b, i: (0, 0)), vec, vec, vec, vec],
        out_specs=[tile, tile],
        out_shape=[_sds((bsz, seq, c), BF16), _sds((bsz, seq, c), F32)],
        scratch_shapes=[pltpu.VMEM((CONV_HALO + tt, c), F32), pltpu.VMEM((tt + SUBLANES, c), F32)],
        compiler_params=_cparams("parallel", "arbitrary"),
    )(proj3, proj3, proj3, proj3, conv_w, conv_b, ln_g, ln_b, out_g)


def _conv_bwd(name, dmix3, proj3, cv3, conv_w, ln_g, ln_b, out_g):
    bsz, seq, _ = proj3.shape
    c = conv_w.shape[1]
    tt = _pick(seq, TILE["conv_t"], CONV_HALO)
    hb = tt // CONV_HALO
    nt = seq // tt
    last_hb = seq // CONV_HALO - 1
    ext = tt + CONV_HALO
    first = CONV_HALO - (CONV_WIDTH - 1)

    def body(v_ref, g_ref, vp_ref, gp_ref, cv_ref, cvn_ref, d_ref, dn_ref, w_ref, lg_ref, lb_ref, og_ref,
             o_ref, dw_ref, dcb_ref, dlg_ref, dlb_ref, dog_ref, a_ref, dc_ref, b_ref, ds_ref):
        i = pl.program_id(1)

        @pl.when((pl.program_id(0) == 0) & (i == 0))
        def _():
            for r in (dw_ref, dcb_ref, dlg_ref, dlb_ref, dog_ref):
                r[...] = jnp.zeros_like(r)

        keep_prev = (i > 0).astype(F32)
        keep_next = (i < nt - 1).astype(F32)
        sig_g = _sigmoid(g_ref[0])
        a_ref[pl.ds(0, CONV_HALO), :] = keep_prev * vp_ref[0] * _sigmoid(gp_ref[0])
        a_ref[pl.ds(CONV_HALO, tt), :] = v_ref[0] * sig_g

        lg, lb, og = lg_ref[...], lb_ref[...], og_ref[...]
        dc_own, d_og, d_lg, d_lb = _conv_post_bwd(cv_ref[0], d_ref[0], lg, lb, og)
        dc_next, _, _, _ = _conv_post_bwd(cvn_ref[0], keep_next * dn_ref[0], lg, lb, og)
        dog_ref[...] += d_og
        dlg_ref[...] += d_lg
        dlb_ref[...] += d_lb
        dcb_ref[...] += jnp.sum(dc_own, axis=0, keepdims=True)
        dc_ref[pl.ds(0, tt), :] = dc_own
        dc_ref[pl.ds(tt, CONV_HALO), :] = dc_next

        da = _conv_taps(dc_ref, w_ref, b_ref, 0, tt, flip=True)

        for r, taps in _tap_groups(first):
            if r:
                ds_ref[pl.ds(0, SUBLANES), :] = jnp.zeros((SUBLANES, c), F32)
                ds_ref[pl.ds(tt, SUBLANES), :] = jnp.zeros((SUBLANES, c), F32)
                ds_ref[pl.ds(r, tt), :] = dc_own
            for base, k in taps:
                prod = (ds_ref[...] * a_ref[pl.ds(base, tt + SUBLANES), :]) if r else (dc_own * a_ref[pl.ds(base, tt), :])
                dw_ref[k:k + 1, :] += jnp.sum(prod, axis=0, keepdims=True)
        val = v_ref[0]
        o_ref[0] = jnp.concatenate([da * sig_g, da * val * sig_g * (1.0 - sig_g)], axis=-1).astype(BF16)

    vec = pl.BlockSpec((1, c), lambda b, i: (0, 0))
    cur = lambda col: pl.BlockSpec((1, tt, c), lambda b, i: (b, i, col))
    prev = lambda col: pl.BlockSpec((1, CONV_HALO, c), lambda b, i: (b, jnp.maximum(i * hb - 1, 0), col))
    nxt = lambda col: pl.BlockSpec((1, CONV_HALO, c), lambda b, i: (b, jnp.minimum((i + 1) * hb, last_hb), col))
    wspec = pl.BlockSpec(conv_w.shape, lambda b, i: (0, 0))
    return pl.pallas_call(
        body, name=name, grid=(bsz, nt),
        in_specs=[cur(0), cur(1), prev(0), prev(1), cur(0), nxt(0), cur(0), nxt(0), wspec, vec, vec, vec],
        out_specs=[pl.BlockSpec((1, tt, 2 * c), lambda b, i: (b, i, 0)), wspec, vec, vec, vec, vec],
        out_shape=[_sds((bsz, seq, 2 * c), BF16), _sds(conv_w.shape, F32)] + [_sds((1, c), F32)] * 4,
        scratch_shapes=[pltpu.VMEM((CONV_HALO + tt, c), F32), pltpu.VMEM((ext, c), F32),
                        pltpu.VMEM((tt + SUBLANES, c), F32), pltpu.VMEM((tt + SUBLANES, c), F32)],
        compiler_params=_cparams("arbitrary", "arbitrary"),
    )(proj3, proj3, proj3, proj3, cv3, cv3, dmix3, dmix3, conv_w, ln_g, ln_b, out_g)


def _ssm_discretise(a_re, a_im, log_dt):
    dt = jnp.exp(log_dt)
    zr, zi = a_re * dt, a_im * dt
    mag = jnp.exp(zr)
    ar, ai = mag * jnp.cos(zi), mag * jnp.sin(zi)
    den = a_re * a_re + a_im * a_im
    nr = ar - 1.0
    return ar, ai, (nr * a_re + ai * a_im) / den, (ai * a_re - nr * a_im) / den


def _ssm_system(a_re, a_im, log_dt, a_re_x, a_im_x, log_dt_x, bt_re, bt_im):
    ar, ai, _, _ = _ssm_discretise(a_re, a_im, log_dt)
    _, _, cr, ci = _ssm_discretise(a_re_x, a_im_x, log_dt_x)
    return ar, ai, cr * bt_re - ci * bt_im, cr * bt_im + ci * bt_re


def _ssm_prep(name, prim):
    g, p = prim[0].shape

    def body(*refs):
        pwr_ref, pwi_ref, bbr_ref, bbi_ref = refs[8:]
        ar, ai, bbr, bbi = _ssm_system(*[r[...] for r in refs[:8]])
        bbr_ref[...] = bbr
        bbi_ref[...] = bbi
        pr, pi = ar, ai
        for k in range(SUBLANES):
            pwr_ref[k] = pr
            pwi_ref[k] = pi
            pr, pi = pr * ar - pi * ai, pr * ai + pi * ar

    return pl.pallas_call(
        body, name=name,
        out_shape=[_sds((SUBLANES, g, p), F32)] * 2 + [_sds(prim[6].shape, F32)] * 2,
        compiler_params=pltpu.CompilerParams(vmem_limit_bytes=VMEM_LIMIT),
    )(*prim)


def _ssm_param_grads(name, prim, dab_r, dab_i, dbb_r, dbb_i):
    g, p = prim[0].shape
    h = prim[6].shape[0] // g

    def body(*refs):
        dar_ref, dai_ref, dbr_ref, dbi_ref = refs[8:12]
        o_ar, o_ai, o_dt, o_br, o_bi = refs[12:]
        _, vjp = jax.vjp(_ssm_system, *[r[...] for r in refs[:8]])
        ct = (jnp.sum(dar_ref[...], axis=0), jnp.sum(dai_ref[...], axis=0), dbr_ref[...], dbi_ref[...])
        d_ar, d_ai, d_dt, d_arx, d_aix, d_dtx, d_br, d_bi = vjp(ct)
        per_group = lambda t: jnp.sum(t.reshape(g, h, p), axis=1)
        o_ar[...] = d_ar + per_group(d_arx)
        o_ai[...] = d_ai + per_group(d_aix)
        o_dt[...] = d_dt + jnp.sum(per_group(d_dtx), axis=1, keepdims=True)
        o_br[...] = d_br
        o_bi[...] = d_bi

    return pl.pallas_call(
        body, name=name,
        out_shape=[_sds(prim[k].shape, F32) for k in (0, 1, 2, 6, 7)],
        compiler_params=pltpu.CompilerParams(vmem_limit_bytes=VMEM_LIMIT),
    )(*prim, dab_r, dab_i, dbb_r, dbb_i)


def _cfma(xr, xi, cr, ci, sr, si):
    return xr + (cr * sr - ci * si), xi + (cr * si + ci * sr)


def _scan_tables(pw_r, pw_i, reverse):
    gp = pw_r.shape[1] * pw_r.shape[2]
    pr, pi = pw_r.reshape(SUBLANES, gp), pw_i.reshape(SUBLANES, gp)
    if reverse:
        pi = -pi
    row = jnp.arange(SUBLANES)[:, None]
    tabs = []
    for d in (1, 2, 4):
        keep = (row < SUBLANES - d) if reverse else (row >= d)
        tabs += [jnp.where(keep, pr[d - 1][None, :], 0.0), jnp.where(keep, pi[d - 1][None, :], 0.0)]
    tabs += [pr[::-1], pi[::-1]] if reverse else [pr, pi]
    return jnp.concatenate(tabs, axis=0)


MXU_DEPTH = 256


def _bands(c, gp):
    bw = min(c, MXU_DEPTH)
    return c // bw, bw, gp * bw // c


def _band_expand(rows16, w_ref, put, c, gp):
    nb, bw, sw = _bands(c, gp)
    for s in range(nb):
        band = rows16[:, s * bw:(s + 1) * bw]
        for half in (0, gp):
            cols = pl.ds(half + s * sw, sw)
            put(cols, _dot(band, w_ref[pl.ds(s * bw, bw), cols]))


def _band_contract(get16, w_ref, c, gp):
    nb, bw, sw = _bands(c, gp)
    out = []
    for s in range(nb):
        acc = None
        for half in (0, gp):
            cols = pl.ds(half + s * sw, sw)
            t = _dot_nt(get16(cols), w_ref[pl.ds(s * bw, bw), cols])
            acc = t if acc is None else acc + t
        out.append(acc)
    return out[0] if nb == 1 else jnp.concatenate(out, axis=1)


def _band_wgrad(name, a, a_block, c, b):
    n = a.shape[0]
    gp = b.shape[1] // 2
    nb, bw, sw = _bands(c, gp)
    tk = _pick(n, TILE["mm_bytes"] // (c * a.dtype.itemsize + 2 * gp * b.dtype.itemsize), 16)

    def body(a_ref, b_ref, o_ref):
        @pl.when(pl.program_id(0) == 0)
        def _():
            o_ref[...] = jnp.zeros_like(o_ref)

        for s in range(nb):
            band = a_ref[:, s * bw:(s + 1) * bw].astype(BF16)
            for h, half in enumerate((0, gp)):
                o_ref[pl.ds(s * bw, bw), pl.ds(h * sw, sw)] += _dot_tn(
                    band, b_ref[:, pl.ds(half + s * sw, sw)].astype(BF16))

    return pl.pallas_call(
        body, name=name, grid=(n // tk,),
        in_specs=[pl.BlockSpec((tk, c), lambda k: (k, a_block)), pl.BlockSpec((tk, 2 * gp), lambda k: (k, 0))],
        out_specs=pl.BlockSpec((c, 2 * sw), lambda k: (0, 0)),
        out_shape=_sds((c, 2 * sw), F32),
        compiler_params=_cparams("arbitrary"),
    )(a, b)


def _band_diag_take(comp, half, c, gp):
    nb, bw, sw = _bands(c, gp)
    return jnp.concatenate([_block_diag_take(comp[s * bw:(s + 1) * bw, half * sw:(half + 1) * sw], bw // SSM_GROUP)
                            for s in range(nb)], axis=0)


def _scan_fwd(name, tab, proj3, u_block, bbd, cdt):
    bsz, seq, _ = proj3.shape
    c, w = bbd.shape
    gp = w // 2
    tt = _pick(seq, TILE["scan_fwd_t"], 16)
    nblk = tt // SUBLANES
    cw = _pick(gp, TILE["scan_w"], LANES)

    def body(tab_ref, u_ref, bbd_ref, cdt_ref, xs_ref, xs16_ref, y_ref, carry_ref, bu_ref):
        @pl.when(pl.program_id(1) == 0)
        def _():
            carry_ref[...] = jnp.zeros_like(carry_ref)

        def put_bu(cols, val):
            bu_ref[0, :, cols] = val

        _band_expand(u_ref[0].astype(BF16), bbd_ref, put_bu, c, gp)

        for ch in range(gp // cw):
            re, im = pl.ds(ch * cw, cw), pl.ds(gp + ch * cw, cw)

            def blk(r, carry, re=re, im=im):
                tabs = [tab_ref[pl.ds(SUBLANES * k, SUBLANES), re] for k in range(8)]
                rows = pl.ds(pl.multiple_of(r * SUBLANES, SUBLANES), SUBLANES)
                xr, xi = bu_ref[0, rows, re], bu_ref[0, rows, im]
                for j, d in enumerate((1, 2, 4)):
                    xr, xi = _cfma(xr, xi, tabs[2 * j], tabs[2 * j + 1], pltpu.roll(xr, d, 0), pltpu.roll(xi, d, 0))
                xr, xi = _cfma(xr, xi, tabs[6], tabs[7], carry[0], carry[1])
                xs_ref[0, rows, re] = xr
                xs_ref[0, rows, im] = xi
                last = SUBLANES - 1
                return (jnp.broadcast_to(xr[last:, :], xr.shape), jnp.broadcast_to(xi[last:, :], xi.shape))

            cr, ci = lax.fori_loop(0, nblk, blk, (carry_ref[:, re], carry_ref[:, im]))
            carry_ref[:, re] = cr
            carry_ref[:, im] = ci

        xs16_ref[0] = xs_ref[0].astype(BF16)
        y_ref[0] = _band_contract(lambda cols: xs16_ref[0, :, cols], cdt_ref, c, gp)

    whole = lambda arr: pl.BlockSpec(arr.shape, lambda b, t: (0, 0), pipeline_mode=pl.Buffered(1))
    wide = pl.BlockSpec((1, tt, w), lambda b, t: (b, t, 0))
    return pl.pallas_call(
        body, name=name, grid=(bsz, seq // tt),
        in_specs=[whole(tab), pl.BlockSpec((1, tt, c), lambda b, t: (b, t, u_block)), whole(bbd), whole(cdt)],
        out_specs=[wide, wide, pl.BlockSpec((1, tt, c), lambda b, t: (b, t, 0))],
        out_shape=[_sds((bsz, seq, w), F32), _sds((bsz, seq, w), BF16), _sds((bsz, seq, c), F32)],
        scratch_shapes=[pltpu.VMEM((SUBLANES, w), F32), pltpu.VMEM((1, tt, w), F32)],
        compiler_params=_cparams("arbitrary", "arbitrary"),
    )(tab, proj3, bbd, cdt)


def _scan_bwd(name, tab, dy3, xs3, du_skip3, bbd, cdt, exchange=None):
    bsz, seq, w = xs3.shape
    c = bbd.shape[0]
    gp = w // 2
    tt = _pick(seq, TILE["scan_t"], 16)
    nblk = tt // SUBLANES
    cw = _pick(gp, TILE["scan_w"], LANES)
    nt = seq // tt

    def body(tab_ref, dy_ref, xs_ref, halo_ref, skip_ref, bbd_ref, cdt_ref, lam16_ref, du_ref, dar_ref, dai_ref,
             carry_ref, g_ref, lam_ref):
        t = pl.program_id(1)

        @pl.when(t == 0)
        def _():
            carry_ref[...] = jnp.zeros_like(carry_ref)

        @pl.when((pl.program_id(0) == 0) & (t == 0))
        def _():
            dar_ref[...] = jnp.zeros_like(dar_ref)
            dai_ref[...] = jnp.zeros_like(dai_ref)

        def put_g(cols, val):
            g_ref[0, :, cols] = val

        _band_expand(dy_ref[0], cdt_ref, put_g, c, gp)

        has_prev = (t < nt - 1).astype(F32)
        row0 = lax.broadcasted_iota(jnp.int32, (SUBLANES, cw), 0) == 0
        last = SUBLANES - 1

        for ch in range(gp // cw):
            re, im = pl.ds(ch * cw, cw), pl.ds(gp + ch * cw, cw)

            def step(rows, xm1r, xm1i, state, re=re, im=im):
                tabs = [tab_ref[pl.ds(SUBLANES * k, SUBLANES), re] for k in range(8)]
                cr, ci, accr, acci = state
                lr, li = g_ref[0, rows, re], g_ref[0, rows, im]
                for j, d in enumerate((1, 2, 4)):
                    lr, li = _cfma(lr, li, tabs[2 * j], tabs[2 * j + 1],
                                   pltpu.roll(lr, SUBLANES - d, 0), pltpu.roll(li, SUBLANES - d, 0))
                lr, li = _cfma(lr, li, tabs[6], tabs[7], cr, ci)
                lam_ref[0, rows, re] = lr
                lam_ref[0, rows, im] = li
                xr, xi = xs_ref[0, rows, re], xs_ref[0, rows, im]
                xpr = jnp.where(row0, jnp.broadcast_to(xm1r[last:, :], xr.shape), pltpu.roll(xr, 1, 0))
                xpi = jnp.where(row0, jnp.broadcast_to(xm1i[last:, :], xi.shape), pltpu.roll(xi, 1, 0))
                accr = accr + (lr * xpr + li * xpi)
                acci = acci + (li * xpr - lr * xpi)
                return (jnp.broadcast_to(lr[:1, :], lr.shape), jnp.broadcast_to(li[:1, :], li.shape), accr, acci)

            def blk(k, state, re=re, im=im, step=step):
                r = nblk - 1 - k
                rows = pl.ds(pl.multiple_of(r * SUBLANES, SUBLANES), SUBLANES)
                prev = pl.ds(pl.multiple_of((r - 1) * SUBLANES, SUBLANES), SUBLANES)
                return step(rows, xs_ref[0, prev, re], xs_ref[0, prev, im], state)

            zero = jnp.zeros((SUBLANES, cw), F32)
            state = lax.fori_loop(0, nblk - 1, blk, (carry_ref[:, re], carry_ref[:, im], zero, zero))
            cr, ci, accr, acci = step(pl.ds(0, SUBLANES), has_prev * halo_ref[0, :, re], has_prev * halo_ref[0, :, im], state)
            carry_ref[:, re] = cr
            carry_ref[:, im] = ci
            dar_ref[:, re] += accr
            dai_ref[:, re] += acci

        lam16_ref[0] = lam_ref[0].astype(BF16)
        du = _band_contract(lambda cols: lam16_ref[0, :, cols], bbd_ref, c, gp)
        du_ref[0] = (du + skip_ref[0]).astype(BF16)

    tile = pl.BlockSpec((1, tt, w), lambda b, t: (b, nt - 1 - t, 0))
    thin = pl.BlockSpec((1, tt, c), lambda b, t: (b, nt - 1 - t, 0))
    halo = pl.BlockSpec((1, SUBLANES, w), lambda b, t: (b, jnp.maximum((nt - 1 - t) * nblk - 1, 0), 0))
    acc = pl.BlockSpec((SUBLANES, gp), lambda b, t: (0, 0))
    whole = lambda arr: pl.BlockSpec(arr.shape, lambda b, t: (0, 0), pipeline_mode=pl.Buffered(1))
    return _call(
        name, body, (bsz, nt), [whole(tab), thin, tile, halo, thin, whole(bbd), whole(cdt)], [tile, thin, acc, acc],
        [_sds(xs3.shape, BF16), _sds((bsz, seq, c), BF16), _sds((SUBLANES, gp), F32), _sds((SUBLANES, gp), F32)],
        (tab, dy3, xs3, xs3, du_skip3, bbd, cdt), ("arbitrary", "arbitrary"),
        scratch=[pltpu.VMEM((SUBLANES, w), F32), pltpu.VMEM((1, tt, w), F32), pltpu.VMEM((1, tt, w), F32)],
        exchange=exchange)


def _gelu_parts(y):
    inner = _GELU_K * (y + _GELU_C * y * y * y)
    t = jnp.tanh(inner)
    return 0.5 * y * (1.0 + t), t


def _ssm_out_fwd(name, cx, proj, u_block, d_skip, glu_w, glu_b, out_g, x, conv_out, w_out):
    n, c = cx.shape
    d = x.shape[1]
    tm = _pick(n, TILE["row"], 16)

    def body(cx_ref, u_ref, d_ref, gw_ref, gb_ref, og_ref, x_ref, a_ref, wo_ref, y_ref, o_ref, xo_ref):
        y = cx_ref[...] + d_ref[...] * u_ref[...]
        y_ref[...] = y
        gy, _ = _gelu_parts(y)
        z = _dot(gy.astype(BF16), gw_ref[...]) + gb_ref[...]
        _, sh = _rms_stats(gy * _sigmoid(z))
        out = (sh * og_ref[...]).astype(BF16)
        o_ref[...] = out
        xo_ref[...] = x_ref[...] + _dot(a_ref[...], wo_ref[pl.ds(0, c), :]) + _dot(out, wo_ref[pl.ds(c, c), :])

    vec = pl.BlockSpec((1, c), lambda i: (0, 0))
    row = pl.BlockSpec((tm, c), lambda i: (i, 0))
    wide = pl.BlockSpec((tm, d), lambda i: (i, 0))
    held = lambda arr: pl.BlockSpec(arr.shape, lambda i: (0, 0), pipeline_mode=pl.Buffered(1))
    return pl.pallas_call(
        body, name=name, grid=(n // tm,),
        in_specs=[row, pl.BlockSpec((tm, c), lambda i: (i, u_block)), vec, held(glu_w), vec, vec, wide, row, held(w_out)],
        out_specs=[row, row, wide],
        out_shape=[_sds((n, c), F32), _sds((n, c), BF16), _sds((n, d), F32)],
        compiler_params=_cparams("parallel"),
    )(cx, proj, d_skip, glu_w, glu_b, out_g, x, conv_out, w_out)


def _ssm_out_bwd(name, dmix, d_block, y, proj, u_block, d_skip, glu_w, glu_b, out_g):
    n, c = y.shape
    tm = _pick(n, TILE["row"], 16)

    def body(d_ref, y_ref, u_ref, dk_ref, gw_ref, gb_ref, og_ref, dy_ref, du_ref, dgw_ref, dgb_ref, dog_ref, dd_ref):
        @pl.when(pl.program_id(0) == 0)
        def _():
            for r in (dgw_ref, dgb_ref, dog_ref, dd_ref):
                r[...] = jnp.zeros_like(r)

        yv = y_ref[...]
        gy, th = _gelu_parts(yv)
        gy16 = gy.astype(BF16)
        sz = _sigmoid(_dot(gy16, gw_ref[...]) + gb_ref[...])
        r, sh = _rms_stats(gy * sz)
        dout = d_ref[...]
        dog_ref[...] += jnp.sum(dout * sh, axis=0, keepdims=True)
        dsh = dout * og_ref[...]
        ds = r * (dsh - sh * jnp.mean(dsh * sh, axis=-1, keepdims=True))
        dz = ds * gy * sz * (1.0 - sz)
        dz16 = dz.astype(BF16)
        dgb_ref[...] += jnp.sum(dz, axis=0, keepdims=True)
        dgw_ref[...] += _dot_tn(gy16, dz16)
        dgy = ds * sz + _dot_nt(dz16, gw_ref[...])
        dgelu = 0.5 * (1.0 + th) + 0.5 * yv * (1.0 - th * th) * (_GELU_K * (1.0 + 3.0 * _GELU_C * yv * yv))
        dy = dgy * dgelu
        dy_ref[...] = dy.astype(BF16)
        du_ref[...] = dy * dk_ref[...]
        dd_ref[...] += jnp.sum(dy * u_ref[...], axis=0, keepdims=True)

    vec = pl.BlockSpec((1, c), lambda i: (0, 0))
    row = pl.BlockSpec((tm, c), lambda i: (i, 0))
    mat = pl.BlockSpec(glu_w.shape, lambda i: (0, 0))
    return pl.pallas_call(
        body, name=name, grid=(n // tm,),
        in_specs=[pl.BlockSpec((tm, c), lambda i: (i, d_block)), row, pl.BlockSpec((tm, c), lambda i: (i, u_block)),
                  vec, mat, vec, vec],
        out_specs=[row, row, mat, vec, vec, vec],
        out_shape=[_sds((n, c), BF16), _sds((n, c), F32), _sds(glu_w.shape, F32)] + [_sds((1, c), F32)] * 3,
        compiler_params=_cparams("arbitrary"),
    )(dmix, y, proj, d_skip, glu_w, glu_b, out_g)


def _mesh_pos():
    return tuple(lax.axis_index(a) for a in MESH_AXES)


def _other_chips(x, y):
    return [(1 - x, y), (x, 1 - y), (1 - x, 1 - y)]


def _remote(src, dst, send_sem, recv_sem, dev):
    return pltpu.make_async_remote_copy(src_ref=src, dst_ref=dst, send_sem=send_sem, recv_sem=recv_sem,
                                        device_id=dev, device_id_type=pl.DeviceIdType.MESH)


def _hbm_call(name, body, operands, out_shapes, scratch):
    hbm = pl.BlockSpec(memory_space=pltpu.HBM)
    return pl.pallas_call(body, name=name, in_specs=[hbm] * len(operands), out_specs=[hbm] * len(out_shapes),
                          out_shape=out_shapes, scratch_shapes=scratch)(*operands)


_Exchange = collections.namedtuple("_Exchange", "operands out_shapes scratch start finish")


def _run_exchange(name, plan):
    nin, nout = len(plan.operands), len(plan.out_shapes)

    def body(*refs):
        parts = refs[:nin], refs[nin:nin + nout], refs[nin + nout:]
        plan.start(*parts)
        plan.finish(*parts)

    return _hbm_call(name, body, plan.operands, plan.out_shapes, plan.scratch)


def _gather_plan(blocks):
    nop = len(blocks)

    def copies(x_refs, o_refs, sems):
        send_sems, recv_sems, local_sems = sems
        x, y, c = _mesh_pos()
        me, sibling = (x, y, c), (x, y, 1 - c)
        chips = _other_chips(x, y)

        def copy(i, k, block_of, to, src=None):
            dst = o_refs[i].at[4 * block_of[0] + 2 * block_of[1] + block_of[2]]
            return _remote(dst if src is None else src, dst, send_sems.at[i, k], recv_sems.at[i, k], to)

        own = [pltpu.make_async_copy(x_refs[i], o_refs[i].at[4 * x + 2 * y + c], local_sems.at[i]) for i in range(nop)]
        first = []
        for i in range(nop):
            first.append(copy(i, 0, me, sibling, src=x_refs[i]))
            first += [copy(i, 1 + j, me, (*chip, c), src=x_refs[i]) for j, chip in enumerate(chips)]
        return copy, own, first, me, sibling, chips, c

    def start(x_refs, o_refs, sems):
        _, own, first, *_ = copies(x_refs, o_refs, sems)
        for cp in own + first:
            cp.start()

    def finish(x_refs, o_refs, sems):
        copy, own, first, me, sibling, chips, c = copies(x_refs, o_refs, sems)
        passed = []
        for i in range(nop):
            for j, chip in enumerate(chips):
                copy(i, 1 + j, (*chip, c), me).wait_recv()
                passed.append(copy(i, 4 + j, (*chip, c), sibling))
                passed[-1].start()
        for i in range(nop):
            copy(i, 0, sibling, me).wait_recv()
            for j, chip in enumerate(chips):
                copy(i, 4 + j, (*chip, 1 - c), me).wait_recv()
        for cp in first + passed:
            cp.wait_send()
        for cp in own:
            cp.wait()

    return _Exchange(list(blocks), [_sds((N_DEV,) + b.shape, b.dtype) for b in blocks],
                     [pltpu.SemaphoreType.DMA((nop, N_DEV - 1)), pltpu.SemaphoreType.DMA((nop, N_DEV - 1)),
                      pltpu.SemaphoreType.DMA((nop,))], start, finish)


def _core_exchange_plan(grads):
    nop = len(grads)

    def copies(x_refs, o_refs, sems):
        send_sems, recv_sems = sems
        x, y, c = _mesh_pos()
        return [_remote(x_refs[i].at[2 * q + (1 - c)], o_refs[i].at[q], send_sems.at[i, q], recv_sems.at[i, q],
                        (x, y, 1 - c)) for i in range(nop) for q in range(N_DEV // 2)]

    def start(x_refs, o_refs, sems):
        for cp in copies(x_refs, o_refs, sems):
            cp.start()

    def finish(x_refs, o_refs, sems):
        for cp in copies(x_refs, o_refs, sems):
            cp.wait()

    return _Exchange(list(grads), [_sds((N_DEV // 2,) + g.shape[1:], g.dtype) for g in grads],
                     [pltpu.SemaphoreType.DMA((nop, N_DEV // 2)), pltpu.SemaphoreType.DMA((nop, N_DEV // 2))],
                     start, finish)


def _pair_sum(name, grad, other):
    nchip, _, r, c = grad.shape
    tr = _pick(r, max(SUBLANES, TILE["sum_bytes"] // (4 * c)), SUBLANES)
    core = lax.axis_index("c").astype(jnp.int32).reshape(1)

    def body(core_ref, g_ref, o_ref, s_ref):
        s_ref[0] = (g_ref[0, 0] + o_ref[0]).astype(s_ref.dtype)

    tile = pl.BlockSpec((1, tr, c), lambda q, t, core_ref: (q, t, 0))
    return pl.pallas_call(
        body, name=name,
        grid_spec=pltpu.PrefetchScalarGridSpec(
            num_scalar_prefetch=1, grid=(nchip, r // tr),
            in_specs=[pl.BlockSpec((1, 1, tr, c), lambda q, t, core_ref: (q, core_ref[0], t, 0)), tile],
            out_specs=tile),
        out_shape=_sds((nchip, r, c), BF16),
        compiler_params=_cparams("parallel", "parallel"),
    )(core, grad, other)


def _chip_exchange_plan(sums):
    nop = len(sums)

    def copies(x_refs, o_refs, sems, arriving):
        send_sems, recv_sems, local_sems = sems
        x, y, c = _mesh_pos()
        mine = 2 * x + y
        out = []
        for i in range(nop):
            for j, (px, py) in enumerate(_other_chips(x, y)):
                theirs = 2 * px + py
                src, dst = (mine, theirs) if arriving else (theirs, mine)
                out.append(_remote(x_refs[i].at[src], o_refs[i].at[dst], send_sems.at[i, j], recv_sems.at[i, j],
                                   (px, py, c)))
        if not arriving:
            out += [pltpu.make_async_copy(x_refs[i].at[mine], o_refs[i].at[mine], local_sems.at[i]) for i in range(nop)]
        return out

    def start(x_refs, o_refs, sems):
        for cp in copies(x_refs, o_refs, sems, False):
            cp.start()

    def finish(x_refs, o_refs, sems):
        for cp in copies(x_refs, o_refs, sems, True):
            cp.wait_recv()
        mine = copies(x_refs, o_refs, sems, False)
        for cp in mine[:3 * nop]:
            cp.wait_send()
        for cp in mine[3 * nop:]:
            cp.wait()

    return _Exchange(list(sums), [_sds(s.shape, s.dtype) for s in sums],
                     [pltpu.SemaphoreType.DMA((nop, 3)), pltpu.SemaphoreType.DMA((nop, 3)), pltpu.SemaphoreType.DMA((nop,))],
                     start, finish)


def _part_rows(npart, r, c):
    return _pick(r, max(SUBLANES, TILE["sum_bytes"] // (4 * npart * c)), SUBLANES)


def _sum_slots(p_ref):
    g = p_ref[0].astype(F32)
    for k in range(1, p_ref.shape[0]):
        g = g + p_ref[k].astype(F32)
    return g


def _adamw_step(g, w, m, v):
    c1 = 1.0 - ADAM_B1 ** ADAM_STEP
    c2 = 1.0 - ADAM_B2 ** ADAM_STEP
    nm = ADAM_B1 * m + (1.0 - ADAM_B1) * g
    nv = ADAM_B2 * v + (1.0 - ADAM_B2) * (g * g)
    return -ADAM_LR * ((nm / c1) / (jnp.sqrt(nv / c2) + ADAM_EPS) + ADAM_WD * w), nm, nv


def _adamw_small(name, parts, ws, ms, vs):
    nparam, nall = len(ws), len(parts)

    def body(*refs):
        p_refs = refs[:nall]
        w_refs, m_refs, v_refs = (refs[nall + k * nparam:nall + (k + 1) * nparam] for k in range(3))
        outs = refs[nall + 3 * nparam:]
        for p in range(nall):
            g = _sum_slots(p_refs[p])
            if p < nparam:
                delta, nm, nv = _adamw_step(g, w_refs[p][...], m_refs[p][...], v_refs[p][...])
                for o_ref, val in zip(outs[4 * p:4 * p + 4], (g, delta, nm, nv)):
                    o_ref[...] = val
            else:
                outs[4 * nparam + p - nparam][...] = g

    shapes = [_sds(w.shape, F32) for w in ws for _ in range(4)] + [_sds(p.shape[1:], F32) for p in parts[nparam:]]
    res = pl.pallas_call(body, name=name, out_shape=shapes,
                         compiler_params=pltpu.CompilerParams(vmem_limit_bytes=VMEM_LIMIT))(*parts, *ws, *ms, *vs)
    return [res[4 * p:4 * p + 4] for p in range(nparam)] + [[r] for r in res[4 * nparam:]]


def _adamw(name, parts, w, m, v):
    npart, r, c = parts.shape
    lead = len(w.shape) - 2
    tr = _part_rows(npart, r, c)
    at = (0,) * lead + (slice(None), slice(None))

    def body(p_ref, w_ref, m_ref, v_ref, g_ref, d_ref, nm_ref, nv_ref):
        g = _sum_slots(p_ref)
        delta, nm, nv = _adamw_step(g, w_ref[at], m_ref[at], v_ref[at])
        g_ref[at] = g
        nm_ref[at] = nm
        nv_ref[at] = nv
        d_ref[at] = delta

    row = pl.BlockSpec((1,) * lead + (tr, c), lambda i: (0,) * lead + (i, 0))
    return pl.pallas_call(
        body, name=name, grid=(r // tr,),
        in_specs=[pl.BlockSpec((npart, tr, c), lambda i: (0, i, 0)), row, row, row],
        out_specs=[row] * 4,
        out_shape=[_sds(w.shape, F32)] * 4,
        compiler_params=_cparams("parallel"),
    )(parts, w, m, v)


def _block_diag(rows_gh, groups):
    gh, p = rows_gh.shape
    own = (jnp.arange(gh)[:, None] // (gh // groups) == jnp.arange(groups)[None, :]).astype(rows_gh.dtype)
    return (own[:, :, None] * rows_gh[:, None, :]).reshape(gh, groups * p)


def _block_diag_take(dense, groups):
    gh = dense.shape[0]
    p = dense.shape[1] // groups
    own = (jnp.arange(gh)[:, None] // (gh // groups) == jnp.arange(groups)[None, :]).astype(dense.dtype)
    return jnp.sum(dense.reshape(gh, groups, p) * own[:, :, None], axis=1)


FFN1 = ("ffn1_w1", "ffn1_w3", "ffn1_w2")
MIXER = ("w_in", "ssm_glu_w", "w_out")
FFN2 = ("ffn2_w1", "ffn2_w3", "ffn2_w2")
BIG = FFN1 + MIXER + FFN2
COL_SHARDED = ("ffn1_w1", "ffn1_w3", "w_in", "ffn2_w1", "ffn2_w3", "conv_w")
SMALL = ("norm_ffn1", "norm_mix", "conv_b", "conv_ln_g", "conv_ln_b", "conv_out_g", "ssm_A_re", "ssm_A_im",
         "ssm_log_dt", "ssm_B_re", "ssm_B_im", "ssm_C_re", "ssm_C_im", "ssm_D", "ssm_glu_b", "ssm_out_g",
         "norm_ffn2", "norm_final")
WEIGHTS = ("norm_ffn1", "ffn1_w1", "ffn1_w3", "ffn1_w2", "norm_mix", "w_in", "conv_w", "conv_b", "conv_ln_g",
           "conv_ln_b", "conv_out_g", "ssm_A_re", "ssm_A_im", "ssm_log_dt", "ssm_B_re", "ssm_B_im", "ssm_C_re",
           "ssm_C_im", "ssm_D", "ssm_glu_w", "ssm_glu_b", "ssm_out_g", "w_out", "norm_ffn2", "ffn2_w1", "ffn2_w3",
           "ffn2_w2", "norm_final")


def _ffn_backward(tag, dxo, x, g, w1, w3, w2, saved, exchange=None, dw_exchange=None, reduce_names=None):
    a, b, h = saved
    (da, db, hid, dxh), got = _ffn_bwd_hidden(tag + "_bwd_hidden", dxo, a, b, w2, exchange=exchange)
    dw1, dw_got = _mm_tn(tag + "_dw1", da, h, exchange=dw_exchange) if dw_exchange else (_mm_tn(tag + "_dw1", da, h), None)
    dw3 = _mm_tn(tag + "_dw3", db, h)
    across = None
    if reduce_names:
        send = [_row_blocks(dw1), _row_blocks(dw3)]
        dw2, from_core = _mm_tn(tag + "_dw2", hid, dxh, exchange=_core_exchange_plan(send))
        send.append(_row_blocks(dw2))
        from_core += _run_exchange("exchange_core_" + tag, _core_exchange_plan(send[2:]))
        across = _across_chips(reduce_names, send, from_core)
    else:
        dw2 = _mm_tn(tag + "_dw2", hid, dxh)
    f = a.shape[1]
    (dx, dg), reduced = _dx_rms_bwd(tag + "_bwd_dx", [(da, f, 0, w1, f, 0), (db, f, 0, w3, f, 0)], dxo, x, g,
                                    exchange=across)
    return (dx, dg, [dw1, dw3, dw2]), got, dw_got, reduced


def _row_blocks(grad):
    return grad.reshape((N_DEV, -1) + grad.shape[1:])


def _across_chips(names, send, from_core):
    return _chip_exchange_plan([_pair_sum("pair_sum_" + k, s.reshape((N_DEV // 2, 2) + s.shape[1:]), o)
                                for k, s, o in zip(names, send, from_core)])


def _reduce_in_chip(names, grads):
    send = [_row_blocks(g) for g in grads]
    return _core_exchange_plan(send), functools.partial(_across_chips, names, send)


def kernel(x, norm_ffn1, ffn1_w1, ffn1_w3, ffn1_w2, norm_mix, w_in, conv_w, conv_b, conv_ln_g, conv_ln_b, conv_out_g, ssm_A_re, ssm_A_im, ssm_log_dt, ssm_B_re, ssm_B_im, ssm_C_re, ssm_C_im, ssm_D, ssm_glu_w, ssm_glu_b, ssm_out_g, w_out, norm_ffn2, ffn2_w1, ffn2_w3, ffn2_w2, norm_final, loss_target, m_norm_ffn1, m_ffn1_w1, m_ffn1_w3, m_ffn1_w2, m_norm_mix, m_w_in, m_conv_w, m_conv_b, m_conv_ln_g, m_conv_ln_b, m_conv_out_g, m_ssm_A_re, m_ssm_A_im, m_ssm_log_dt, m_ssm_B_re, m_ssm_B_im, m_ssm_C_re, m_ssm_C_im, m_ssm_D, m_ssm_glu_w, m_ssm_glu_b, m_ssm_out_g, m_w_out, m_norm_ffn2, m_ffn2_w1, m_ffn2_w3, m_ffn2_w2, m_norm_final, v_norm_ffn1, v_ffn1_w1, v_ffn1_w3, v_ffn1_w2, v_norm_mix, v_w_in, v_conv_w, v_conv_b, v_conv_ln_g, v_conv_ln_b, v_conv_out_g, v_ssm_A_re, v_ssm_A_im, v_ssm_log_dt, v_ssm_B_re, v_ssm_B_im, v_ssm_C_re, v_ssm_C_im, v_ssm_D, v_ssm_glu_w, v_ssm_glu_b, v_ssm_out_g, v_w_out, v_norm_ffn2, v_ffn2_w1, v_ffn2_w3, v_ffn2_w2, v_norm_final):
    args = dict(locals())
    wt = {n: args[n] for n in WEIGHTS}
    mom = {n: args["m_" + n] for n in WEIGHTS}
    var = {n: args["v_" + n] for n in WEIGHTS}

    bsz, seq, d = x.shape
    n = bsz * seq
    c = conv_b.shape[-1]
    groups = c // SSM_GROUP
    gp = groups * SSM_STATE
    u_b = 2

    shard = {k: (wt[k][0].T if k in COL_SHARDED else wt[k][0]).astype(BF16) for k in BIG}
    gathered = _run_exchange("gather_weights_ffn1", _gather_plan([shard[k] for k in FFN1]))
    full = {k: g.reshape(-1, g.shape[-1]) for k, g in zip(FFN1, gathered)}
    gather_rest = _gather_plan([shard[k] for k in MIXER + FFN2] + [wt["conv_w"][0]])

    vec = lambda k: wt[k].reshape(1, -1)
    g_ffn1, g_mix, g_ffn2, g_fin = vec("norm_ffn1"), vec("norm_mix"), vec("norm_ffn2"), vec("norm_final")
    cb, lng, lnb, cog = vec("conv_b"), vec("conv_ln_g"), vec("conv_ln_b"), vec("conv_out_g")
    d_skip, glu_b, sog = vec("ssm_D"), vec("ssm_glu_b"), vec("ssm_out_g")

    a_re, a_im = wt["ssm_A_re"][0], wt["ssm_A_im"][0]
    log_dt = wt["ssm_log_dt"][0].reshape(groups, 1)
    bt_re = wt["ssm_B_re"][0].transpose(0, 2, 1).reshape(groups * SSM_GROUP, SSM_STATE)
    bt_im = wt["ssm_B_im"][0].transpose(0, 2, 1).reshape(groups * SSM_GROUP, SSM_STATE)
    c_re = wt["ssm_C_re"][0].reshape(groups * SSM_GROUP, SSM_STATE)
    c_im = wt["ssm_C_im"][0].reshape(groups * SSM_GROUP, SSM_STATE)
    per_chan = lambda t: jnp.repeat(t, SSM_GROUP, axis=0)
    ssm_prim = (a_re, a_im, log_dt, per_chan(a_re), per_chan(a_im), per_chan(jnp.broadcast_to(log_dt, a_re.shape)),
                bt_re, bt_im)
    pw_r, pw_i, bb_r, bb_i = _ssm_prep("ssm_prep", ssm_prim)
    tab_f = _scan_tables(pw_r, pw_i, False)
    tab_b = _scan_tables(pw_r, pw_i, True)
    bbd = jnp.concatenate([_block_diag(bb_r, groups), _block_diag(bb_i, groups)], axis=1).astype(BF16)
    cdt = jnp.concatenate([_block_diag(c_re, groups), -_block_diag(c_im, groups)], axis=1).astype(BF16)

    x0 = x.reshape(n, d)
    (x1, *ffn1_saved), gathered = _ffn_fwd("ffn1_fwd", x0, g_ffn1, full["ffn1_w1"], full["ffn1_w3"], full["ffn1_w2"],
                                           exchange=gather_rest)
    full.update({k: g.reshape(-1, g.shape[-1]) for k, g in zip(MIXER + FFN2, gathered)})
    conv_w_full = gathered[-1].transpose(1, 0, 2).reshape(CONV_WIDTH, c)
    conv_w_pad = jnp.pad(conv_w_full, ((0, CONV_HALO - CONV_WIDTH), (0, 0)))
    (proj,), h2 = _rms_mm("mix_in", x1, g_mix, [full["w_in"]], F32)
    proj3 = proj.reshape(bsz, seq, 3 * c)
    an3, cv3 = _conv_fwd("conv_fwd", proj3, conv_w_pad, cb, lng, lnb, cog)
    an = an3.reshape(n, c)
    xs3, xs16, cx3 = _scan_fwd("scan_fwd", tab_f, proj3, u_b, bbd, cdt)
    w_o = full["w_out"]
    y, sn, x2 = _ssm_out_fwd("ssm_out_fwd", cx3.reshape(n, c), proj, u_b, d_skip, full["ssm_glu_w"], glu_b, sog,
                             x1, an, w_o)
    (dx3, *ffn2_saved, loss_tile, d_gfin), _ = _ffn_fwd(
        "ffn2_fwd", x2, g_ffn2, full["ffn2_w1"], full["ffn2_w3"], full["ffn2_w2"],
        head=(g_fin, loss_target.reshape(n, d)))
    loss = lax.psum(loss_tile[0, 0], MESH_AXES)

    grads, from_chips = {}, {}
    (dx2, grads["norm_ffn2"], dws), _, _, _ = _ffn_backward(
        "ffn2", dx3, x2, g_ffn2, full["ffn2_w1"], full["ffn2_w3"], full["ffn2_w2"], ffn2_saved)
    in_chip, across_chips = _reduce_in_chip(FFN2, dws)

    dmix, got = _mm_nt("mix_out_bwd", dx2, w_o, exchange=in_chip)
    reduce_ffn2 = across_chips(got)
    grads["w_out"] = _mm_tn("dw_out", (an, sn), dx2)

    dy, du_skip, grads["ssm_glu_w"], grads["ssm_glu_b"], grads["ssm_out_g"], grads["ssm_D"] = _ssm_out_bwd(
        "ssm_out_bwd", dmix, 1, y, proj, u_b, d_skip, full["ssm_glu_w"], glu_b, sog)
    (lam3, du3, dab_r, dab_i), got = _scan_bwd("scan_bwd", tab_b, dy.reshape(bsz, seq, c), xs3,
                                               du_skip.reshape(bsz, seq, c), bbd, cdt, exchange=reduce_ffn2)
    from_chips.update(zip(FFN2, got))
    lam, du = lam3.reshape(n, 2 * gp), du3.reshape(n, c)
    d_bbd = _band_wgrad("ssm_dbb", proj, u_b, c, lam)
    d_cdt = _band_wgrad("ssm_dc", dy, 0, c, xs16.reshape(n, 2 * gp))
    d_are, d_aim, d_ldt, d_btr, d_bti = _ssm_param_grads(
        "ssm_param_grads", ssm_prim,
        dab_r.reshape(SUBLANES, groups, SSM_STATE), dab_i.reshape(SUBLANES, groups, SSM_STATE),
        _band_diag_take(d_bbd, 0, c, gp), _band_diag_take(d_bbd, 1, c, gp))
    grads["ssm_A_re"], grads["ssm_A_im"], grads["ssm_log_dt"] = d_are, d_aim, d_ldt
    grads["ssm_B_re"], grads["ssm_B_im"] = d_btr, d_bti
    grads["ssm_C_re"] = _band_diag_take(d_cdt, 0, c, gp)
    grads["ssm_C_im"] = -_band_diag_take(d_cdt, 1, c, gp)

    dconv3, d_cw, grads["conv_b"], grads["conv_ln_g"], grads["conv_ln_b"], grads["conv_out_g"] = _conv_bwd(
        "conv_bwd", dmix.reshape(bsz, seq, 2 * c), proj3, cv3, conv_w_pad, lng, lnb, cog)
    dconv = dconv3.reshape(n, 2 * c)
    grads["conv_w"] = d_cw[:CONV_WIDTH]
    grads["w_in"] = _mm_tn("dw_in", (dconv, du), h2)
    w_i = full["w_in"]
    in_chip, across_chips = _reduce_in_chip(MIXER, [grads[k] for k in MIXER])
    (dx1, grads["norm_mix"]), got = _dx_rms_bwd("mix_in_bwd", [(dconv, 2 * c, 0, w_i, 2 * c, 0), (du, c, 0, w_i, c, 2)],
                                                dx2, x1, g_mix, exchange=in_chip)
    reduce_mixer = across_chips(got)

    grads["norm_final"] = d_gfin
    early = tuple(k for k in SMALL if k != "norm_ffn1")
    gather_small = _gather_plan([grads[k] for k in early] + [grads["conv_w"]])

    (dx0, grads["norm_ffn1"], _), got, small_parts, reduced = _ffn_backward(
        "ffn1", dx1, x0, g_ffn1, full["ffn1_w1"], full["ffn1_w3"], full["ffn1_w2"], ffn1_saved,
        exchange=reduce_mixer, dw_exchange=gather_small, reduce_names=FFN1)
    from_chips.update(zip(MIXER, got))
    from_chips.update(zip(FFN1, reduced))

    res = {}
    for k in BIG:
        parts = from_chips[k]
        if k in COL_SHARDED:
            swap = lambda t: jnp.swapaxes(t, -1, -2)
            res[k] = [swap(t) for t in _adamw("adamw_" + k, parts, swap(wt[k]), swap(mom[k]), swap(var[k]))]
        else:
            res[k] = _adamw("adamw_" + k, parts, wt[k], mom[k], var[k])

    def as_2d(k, t):
        if k in ("ssm_B_re", "ssm_B_im"):
            return t[0].transpose(0, 2, 1).reshape(-1, SSM_STATE)
        if k in ("ssm_C_re", "ssm_C_im"):
            return t[0].reshape(-1, SSM_STATE)
        if k in ("ssm_A_re", "ssm_A_im"):
            return t[0]
        return t.reshape(-1, 1) if k == "ssm_log_dt" else t.reshape(1, -1)

    def as_param(k, t):
        if k in ("ssm_B_re", "ssm_B_im"):
            t = t.reshape(groups, SSM_GROUP, SSM_STATE).transpose(0, 2, 1)
        return t.reshape(wt[k].shape)

    (last_part,) = _run_exchange("gather_norm_ffn1_grad", _gather_plan([grads["norm_ffn1"]]))
    order = ("norm_ffn1",) + early
    updated = _adamw_small("adamw_replicated", [last_part] + small_parts,
                           *[[as_2d(k, src[k]) for k in order] for src in (wt, mom, var)])
    res.update({k: [as_param(k, t) for t in upd] for k, upd in zip(order, updated)})
    (conv_w_grad,) = updated[-1]
    x_pos, y_pos, c_pos = (lax.axis_index(a) for a in MESH_AXES)
    cw_cols = c // N_DEV
    own_cw = lax.dynamic_slice_in_dim(conv_w_grad, (4 * x_pos + 2 * y_pos + c_pos) * cw_cols, cw_cols, axis=1)
    res["conv_w"] = _adamw("adamw_conv_w", own_cw[None], wt["conv_w"], mom["conv_w"], var["conv_w"])

    outs = [loss, dx0.reshape(bsz, seq, d)]
    for kind in range(4):
        outs += [res[k][kind] for k in WEIGHTS]
    return tuple(outs)
```

```python
import collections
import functools
import math

import jax
import jax.numpy as jnp
from jax import lax
from jax.experimental import pallas as pl
from jax.experimental.pallas import tpu as pltpu

F32 = jnp.float32
BF16 = jnp.bfloat16

EPS = 1e-6
FFN_RES = 0.5
CONV_WIDTH = 31
CONV_HALO = 32
SSM_GROUP = 16
SSM_STATE = 64
ADAM_LR, ADAM_B1, ADAM_B2, ADAM_EPS, ADAM_WD, ADAM_STEP = 0.001, 0.9, 0.999, 1e-08, 0.01, 10

N_DEV = 8
MESH_AXES = ("x", "y", "c")
SUBLANES = 8
LANES = 128
V7X_VMEM_BYTES = 64 * 2**20
VMEM_LIMIT = V7X_VMEM_BYTES - 8 * 2**20

TILE = dict(row=512, hid_m=512, ffn_m=512, mm_bytes=8 * 2**20, up_m=1024, up_n=256, wide_n=2048, conv_t=512,
            scan_fwd_t=512, scan_t=256, scan_w=512, sum_bytes=4 * 2**20)

_GELU_K = math.sqrt(2.0 / math.pi)
_GELU_C = 0.044715


def _pick(n, target, mult):
    best = None
    for t in range(mult, min(n, target) + 1, mult):
        if n % t == 0:
            best = t
    return n if best is None else best


def _cparams(*sem):
    return pltpu.CompilerParams(dimension_semantics=sem, vmem_limit_bytes=VMEM_LIMIT)


def _sds(shape, dtype):
    return jax.ShapeDtypeStruct(shape, dtype)


def _call(name, body, grid, in_specs, out_specs, out_shape, operands, sem, scratch=(), exchange=None):
    if exchange is None:
        res = pl.pallas_call(body, name=name, grid=grid, in_specs=list(in_specs), out_specs=list(out_specs),
                             out_shape=list(out_shape), scratch_shapes=list(scratch),
                             compiler_params=_cparams(*sem))(*operands)
        return list(res), None
    n_in, n_out, n_scr = len(in_specs), len(out_specs), len(scratch)
    n_xin, n_xout = len(exchange.operands), len(exchange.out_shapes)
    hbm = pl.BlockSpec(memory_space=pltpu.HBM)

    def with_exchange(*refs):
        cuts, pos = [], 0
        for size in (n_in, n_xin, n_out, n_xout, n_scr):
            cuts.append(refs[pos:pos + size])
            pos += size
        ins, x_in, outs, x_out, scr = cuts
        sems = refs[pos:]
        ids = [pl.program_id(axis) for axis in range(len(grid))]
        first = functools.reduce(lambda p, q: p & q, [i == 0 for i in ids])
        last = functools.reduce(lambda p, q: p & q, [i == g - 1 for i, g in zip(ids, grid)])

        @pl.when(first)
        def _():
            exchange.start(x_in, x_out, sems)

        body(*ins, *outs, *scr)

        @pl.when(last)
        def _():
            exchange.finish(x_in, x_out, sems)

    res = pl.pallas_call(
        with_exchange, name=name, grid=grid, in_specs=list(in_specs) + [hbm] * n_xin,
        out_specs=list(out_specs) + [hbm] * n_xout, out_shape=list(out_shape) + list(exchange.out_shapes),
        scratch_shapes=list(scratch) + list(exchange.scratch),
        compiler_params=_cparams(*["arbitrary"] * len(grid)))(*operands, *exchange.operands)
    return list(res[:n_out]), list(res[n_out:])


def _dot(a, b):
    return jnp.dot(a, b, preferred_element_type=F32)


def _dot_nt(a, b):
    return lax.dot_general(a, b, (((1,), (1,)), ((), ())), preferred_element_type=F32)


def _dot_tn(a, b):
    return lax.dot_general(a, b, (((0,), (0,)), ((), ())), preferred_element_type=F32)


def _sigmoid(x):
    return 0.5 * jnp.tanh(0.5 * x) + 0.5


def _rms_stats(x):
    r = lax.rsqrt(jnp.mean(x * x, axis=-1, keepdims=True) + EPS)
    return r, x * r


def _rms_bwd(x, g, dy):
    r, xh = _rms_stats(x)
    dxh = dy * g
    dx = r * (dxh - xh * jnp.mean(dxh * xh, axis=-1, keepdims=True))
    return dx, jnp.sum(dy * xh, axis=0, keepdims=True)


def _rms_mm(name, x, g, ws, out_dtype):
    n, d = x.shape
    f = ws[0].shape[0]
    nw = len(ws)
    tm, tn = _pick(n, TILE["up_m"], 16), _pick(f, TILE["wide_n"], LANES)

    def body(x_ref, g_ref, *refs):
        w_refs, o_refs, h_ref = refs[:nw], refs[nw:2 * nw], refs[2 * nw]

        @pl.when(pl.program_id(1) == 0)
        def _():
            _, xh = _rms_stats(x_ref[...])
            h_ref[...] = (xh * g_ref[...]).astype(BF16)

        h = h_ref[...]
        for w_ref, o_ref in zip(w_refs, o_refs):
            o_ref[...] = _dot_nt(h, w_ref[...]).astype(o_ref.dtype)

    outs = pl.pallas_call(
        body, name=name, grid=(n // tm, f // tn),
        in_specs=[pl.BlockSpec((tm, d), lambda i, j: (i, 0)), pl.BlockSpec((1, d), lambda i, j: (0, 0))]
        + [pl.BlockSpec((tn, d), lambda i, j: (j, 0))] * nw,
        out_specs=[pl.BlockSpec((tm, tn), lambda i, j: (i, j))] * nw + [pl.BlockSpec((tm, d), lambda i, j: (i, 0))],
        out_shape=[_sds((n, f), out_dtype)] * nw + [_sds((n, d), BF16)],
        compiler_params=_cparams("parallel", "arbitrary"),
    )(x, g, *ws)
    return outs[:nw], outs[nw]


def _ffn_fwd(name, x, g, w1t, w3t, w2, exchange=None, head=None):
    n, d = x.shape
    f = w2.shape[0]
    tm, tn = _pick(n, TILE["ffn_m"], 16), _pick(f, TILE["up_n"], LANES)

    def body(x_ref, g_ref, w1_ref, w3_ref, w2_ref, *refs):
        (gf_ref, t_ref), refs = (refs[:2], refs[2:]) if head else ((None, None), refs)
        o_ref, a_ref, b_ref, h_ref = refs[:4]
        xv = x_ref[...]
        _, xh = _rms_stats(xv)
        h = (xh * g_ref[...]).astype(BF16)
        h_ref[...] = h
        acc = None
        for c0 in range(0, f, tn):
            cols = pl.ds(c0, tn)
            av, bv = _dot_nt(h, w1_ref[cols, :]), _dot_nt(h, w3_ref[cols, :])
            a_ref[:, cols] = av.astype(BF16)
            b_ref[:, cols] = bv.astype(BF16)
            t = _dot((av * _sigmoid(av) * bv).astype(BF16), w2_ref[cols, :])
            acc = t if acc is None else acc + t
        out = xv + FFN_RES * acc
        if head is None:
            o_ref[...] = out
        else:
            loss_ref, dg_ref = refs[4:]

            @pl.when(pl.program_id(0) == 0)
            def _():
                loss_ref[...] = jnp.zeros_like(loss_ref)
                dg_ref[...] = jnp.zeros_like(dg_ref)

            dx, loss, dg = _loss_head_rows(out, gf_ref[...], t_ref[...])
            o_ref[...] = dx
            loss_ref[...] += loss
            dg_ref[...] += dg

    row = pl.BlockSpec((tm, d), lambda i: (i, 0))
    wide = pl.BlockSpec((tm, f), lambda i: (i, 0))
    vec = pl.BlockSpec((1, d), lambda i: (0, 0))
    held = pl.BlockSpec((f, d), lambda i: (0, 0), pipeline_mode=pl.Buffered(1))
    extra_in, extra_out, extra_shape = ([vec, row], [pl.BlockSpec((SUBLANES, LANES), lambda i: (0, 0)), vec],
                                        [_sds((SUBLANES, LANES), F32), _sds((1, d), F32)]) if head else ([], [], [])
    return _call(
        name, body, (n // tm,), [row, vec, held, held, held] + extra_in, [row, wide, wide, row] + extra_out,
        [_sds((n, d), F32), _sds((n, f), BF16), _sds((n, f), BF16), _sds((n, d), BF16)] + extra_shape,
        (x, g, w1t, w3t, w2) + (tuple(head) if head else ()), ("arbitrary",) if head else ("parallel",),
        exchange=exchange)


def _ffn_bwd_hidden(name, dxo, a, b, w2, exchange=None):
    n, d = dxo.shape
    f = a.shape[1]
    tm, tn = _pick(n, TILE["hid_m"], 16), _pick(f, TILE["up_n"], LANES)

    def body(dx_ref, a_ref, b_ref, w_ref, da_ref, db_ref, hid_ref, dxh_ref):
        dxh = (FFN_RES * dx_ref[...]).astype(BF16)
        dxh_ref[...] = dxh
        for c0 in range(0, f, tn):
            cols = pl.ds(c0, tn)
            dhid = _dot_nt(dxh, w_ref[cols, :])
            av, bv = a_ref[:, cols].astype(F32), b_ref[:, cols].astype(F32)
            sig = _sigmoid(av)
            silu = av * sig
            da_ref[:, cols] = (dhid * bv * (sig * (1.0 + av - silu))).astype(BF16)
            db_ref[:, cols] = (dhid * silu).astype(BF16)
            hid_ref[:, cols] = (silu * bv).astype(BF16)

    wide = pl.BlockSpec((tm, f), lambda i: (i, 0))
    row = pl.BlockSpec((tm, d), lambda i: (i, 0))
    return _call(
        name, body, (n // tm,),
        [row, wide, wide, pl.BlockSpec((f, d), lambda i: (0, 0), pipeline_mode=pl.Buffered(1))], [wide, wide, wide, row],
        [_sds((n, f), BF16)] * 3 + [_sds((n, d), BF16)], (dxo, a, b, w2), ("parallel",), exchange=exchange)


def _loss_head_rows(x, g, target):
    r, xh = _rms_stats(x)
    err = xh * g - target
    dy = err * (1.0 / x.shape[-1])
    dxh = dy * g
    dx = r * (dxh - xh * jnp.mean(dxh * xh, axis=-1, keepdims=True))
    return dx, 0.5 * jnp.sum(jnp.mean(err * err, axis=-1, keepdims=True)), jnp.sum(dy * xh, axis=0, keepdims=True)


def _dx_rms_bwd(name, pairs, dxo, x, g, exchange=None):
    n, dm = x.shape
    tm = _pick(n, TILE["ffn_m"], 16)
    npair = len(pairs)

    def body(*refs):
        d_refs, w_refs = refs[:npair], refs[npair:2 * npair]
        dxo_ref, x_ref, g_ref, dx_ref, dg_ref = refs[2 * npair:]

        @pl.when(pl.program_id(0) == 0)
        def _():
            dg_ref[...] = jnp.zeros_like(dg_ref)

        dh = None
        for d_ref, w_ref in zip(d_refs, w_refs):
            t = _dot(d_ref[...].astype(BF16), w_ref[...])
            dh = t if dh is None else dh + t
        dx, dg = _rms_bwd(x_ref[...], g_ref[...], dh)
        dx_ref[...] = dxo_ref[...] + dx
        dg_ref[...] += dg

    row = pl.BlockSpec((tm, dm), lambda i: (i, 0))
    d_specs = [pl.BlockSpec((tm, p[1]), functools.partial(lambda i, cb: (i, cb), cb=p[2])) for p in pairs]
    w_specs = [pl.BlockSpec((p[4], dm), functools.partial(lambda i, rb: (rb, 0), rb=p[5]), pipeline_mode=pl.Buffered(1))
               for p in pairs]
    return _call(
        name, body, (n // tm,), d_specs + w_specs + [row, row, pl.BlockSpec((1, dm), lambda i: (0, 0))],
        [row, pl.BlockSpec((1, dm), lambda i: (0, 0))], [_sds((n, dm), F32), _sds((1, dm), F32)],
        (*[p[0] for p in pairs], *[p[3] for p in pairs], dxo, x, g), ("arbitrary",), exchange=exchange)


def _mm_tn(name, a, b, exchange=None):
    parts = tuple(a) if isinstance(a, (tuple, list)) else (a,)
    n, mb = b.shape
    widths = [p.shape[1] for p in parts]
    tk = _pick(n, TILE["mm_bytes"] // (sum(p.shape[1] * p.dtype.itemsize for p in parts) + mb * b.dtype.itemsize), 16)

    def body(*refs):
        a_refs, b_ref, o_ref = refs[:-2], refs[-2], refs[-1]

        @pl.when(pl.program_id(0) == 0)
        def _():
            o_ref[...] = jnp.zeros_like(o_ref)

        bv = b_ref[...].astype(BF16)
        row0 = 0
        for a_ref, width in zip(a_refs, widths):
            o_ref[pl.ds(row0, width), :] += _dot_tn(a_ref[...].astype(BF16), bv)
            row0 += width

    (out,), got = _call(
        name, body, (n // tk,),
        [pl.BlockSpec((tk, w), lambda k: (k, 0)) for w in widths] + [pl.BlockSpec((tk, mb), lambda k: (k, 0))],
        [pl.BlockSpec((sum(widths), mb), lambda k: (0, 0))], [_sds((sum(widths), mb), F32)], (*parts, b), ("arbitrary",),
        exchange=exchange)
    return out if exchange is None else (out, got)


def _mm_nt(name, a, w, exchange=None):
    n, k = a.shape
    m = w.shape[0]
    tm = _pick(n, TILE["row"], 16)

    def body(a_ref, w_ref, o_ref):
        o_ref[...] = _dot_nt(a_ref[...].astype(BF16), w_ref[...])

    (out,), got = _call(
        name, body, (n // tm,),
        [pl.BlockSpec((tm, k), lambda i: (i, 0)), pl.BlockSpec((m, k), lambda i: (0, 0), pipeline_mode=pl.Buffered(1))],
        [pl.BlockSpec((tm, m), lambda i: (i, 0))], [_sds((n, m), F32)], (a, w), ("parallel",), exchange=exchange)
    return out, got


def _conv_post(c, ln_g, ln_b, out_g):
    mu = jnp.mean(c, axis=-1, keepdims=True)
    xc = c - mu
    rstd = lax.rsqrt(jnp.mean(xc * xc, axis=-1, keepdims=True) + EPS)
    nrm = xc * rstd
    l = nrm * ln_g + ln_b
    sig = _sigmoid(l)
    s = l * sig
    r, sh = _rms_stats(s)
    return sh * out_g, (rstd, nrm, l, sig, r, sh)


def _tap_groups(first):
    groups = []
    for r in range(SUBLANES):
        taps = [(s - r, s - first) for s in range(first, first + CONV_WIDTH) if s % SUBLANES == r]
        if taps:
            groups.append((r, taps))
    return groups


def _conv_taps(a_ref, w_ref, b_ref, first, rows, flip=False):
    acc = None
    for r, taps in _tap_groups(first):
        ext = rows if r == 0 else rows + SUBLANES
        part = None
        for base, k in taps:
            kk = CONV_WIDTH - 1 - k if flip else k
            t = w_ref[kk:kk + 1, :] * a_ref[pl.ds(base, ext), :]
            part = t if part is None else part + t
        if r:
            b_ref[...] = part
            part = b_ref[pl.ds(r, rows), :]
        acc = part if acc is None else acc + part
    return acc


def _conv_post_bwd(cv, dout, ln_g, ln_b, out_g):
    _, (rstd, nrm, l, sig, r, sh) = _conv_post(cv, ln_g, ln_b, out_g)
    dsh = dout * out_g
    ds = r * (dsh - sh * jnp.mean(dsh * sh, axis=-1, keepdims=True))
    dl = ds * (sig * (1.0 + l * (1.0 - sig)))
    dn = dl * ln_g
    dc = rstd * (dn - jnp.mean(dn, axis=-1, keepdims=True) - nrm * jnp.mean(dn * nrm, axis=-1, keepdims=True))
    col_sum = lambda t: jnp.sum(t, axis=0, keepdims=True)
    return dc, col_sum(dout * sh), col_sum(dl * nrm), col_sum(dl)


def _conv_fwd(name, proj3, conv_w, conv_b, ln_g, ln_b, out_g):
    bsz, seq, _ = proj3.shape
    c = conv_w.shape[1]
    tt = _pick(seq, TILE["conv_t"], CONV_HALO)
    hb = tt // CONV_HALO
    first = CONV_HALO - (CONV_WIDTH - 1)

    def body(v_ref, g_ref, vp_ref, gp_ref, w_ref, cb_ref, lg_ref, lb_ref, og_ref, o_ref, cv_ref, a_ref, b_ref):
        keep = (pl.program_id(1) > 0).astype(F32)
        a_ref[pl.ds(0, CONV_HALO), :] = keep * vp_ref[0] * _sigmoid(gp_ref[0])
        a_ref[pl.ds(CONV_HALO, tt), :] = v_ref[0] * _sigmoid(g_ref[0])
        cv = _conv_taps(a_ref, w_ref, b_ref, first, tt) + cb_ref[...]
        cv_ref[0] = cv
        out, _ = _conv_post(cv, lg_ref[...], lb_ref[...], og_ref[...])
        o_ref[0] = out.astype(BF16)

    vec = pl.BlockSpec((1, c), lambda b, i: (0, 0))
    prev = lambda col: pl.BlockSpec((1, CONV_HALO, c), lambda b, i: (b, jnp.maximum(i * hb - 1, 0), col))
    tile = pl.BlockSpec((1, tt, c), lambda b, i: (b, i, 0))
    return pl.pallas_call(
        body, name=name, grid=(bsz, seq // tt),
        in_specs=[tile, pl.BlockSpec((1, tt, c), lambda b, i: (b, i, 1)),
                  prev(0), prev(1), pl.BlockSpec(conv_w.shape, lambda b, i: (0, 0)), vec, vec, vec, vec],
        out_specs=[tile, tile],
        out_shape=[_sds((bsz, seq, c), BF16), _sds((bsz, seq, c), F32)],
        scratch_shapes=[pltpu.VMEM((CONV_HALO + tt, c), F32), pltpu.VMEM((tt + SUBLANES, c), F32)],
        compiler_params=_cparams("parallel", "arbitrary"),
    )(proj3, proj3, proj3, proj3, conv_w, conv_b, ln_g, ln_b, out_g)


def _conv_bwd(name, dmix3, proj3, cv3, conv_w, ln_g, ln_b, out_g):
    bsz, seq, _ = proj3.shape
    c = conv_w.shape[1]
    tt = _pick(seq, TILE["conv_t"], CONV_HALO)
    hb = tt // CONV_HALO
    nt = seq // tt
    last_hb = seq // CONV_HALO - 1
    ext = tt + CONV_HALO
    first = CONV_HALO - (CONV_WIDTH - 1)

    def body(v_ref, g_ref, vp_ref, gp_ref, cv_ref, cvn_ref, d_ref, dn_ref, w_ref, lg_ref, lb_ref, og_ref,
             o_ref, dw_ref, dcb_ref, dlg_ref, dlb_ref, dog_ref, a_ref, dc_ref, b_ref, ds_ref):
        i = pl.program_id(1)

        @pl.when((pl.program_id(0) == 0) & (i == 0))
        def _():
            for r in (dw_ref, dcb_ref, dlg_ref, dlb_ref, dog_ref):
                r[...] = jnp.zeros_like(r)

        keep_prev = (i > 0).astype(F32)
        keep_next = (i < nt - 1).astype(F32)
        sig_g = _sigmoid(g_ref[0])
        a_ref[pl.ds(0, CONV_HALO), :] = keep_prev * vp_ref[0] * _sigmoid(gp_ref[0])
        a_ref[pl.ds(CONV_HALO, tt), :] = v_ref[0] * sig_g

        lg, lb, og = lg_ref[...], lb_ref[...], og_ref[...]
        dc_own, d_og, d_lg, d_lb = _conv_post_bwd(cv_ref[0], d_ref[0], lg, lb, og)
        dc_next, _, _, _ = _conv_post_bwd(cvn_ref[0], keep_next * dn_ref[0], lg, lb, og)
        dog_ref[...] += d_og
        dlg_ref[...] += d_lg
        dlb_ref[...] += d_lb
        dcb_ref[...] += jnp.sum(dc_own, axis=0, keepdims=True)
        dc_ref[pl.ds(0, tt), :] = dc_own
        dc_ref[pl.ds(tt, CONV_HALO), :] = dc_next

        da = _conv_taps(dc_ref, w_ref, b_ref, 0, tt, flip=True)

        for r, taps in _tap_groups(first):
            if r:
                ds_ref[pl.ds(0, SUBLANES), :] = jnp.zeros((SUBLANES, c), F32)
                ds_ref[pl.ds(tt, SUBLANES), :] = jnp.zeros((SUBLANES, c), F32)
                ds_ref[pl.ds(r, tt), :] = dc_own
            for base, k in taps:
                prod = (ds_ref[...] * a_ref[pl.ds(base, tt + SUBLANES), :]) if r else (dc_own * a_ref[pl.ds(base, tt), :])
                dw_ref[k:k + 1, :] += jnp.sum(prod, axis=0, keepdims=True)
        val = v_ref[0]
        o_ref[0] = jnp.concatenate([da * sig_g, da * val * sig_g * (1.0 - sig_g)], axis=-1).astype(BF16)

    vec = pl.BlockSpec((1, c), lambda b, i: (0, 0))
    cur = lambda col: pl.BlockSpec((1, tt, c), lambda b, i: (b, i, col))
    prev = lambda col: pl.BlockSpec((1, CONV_HALO, c), lambda b, i: (b, jnp.maximum(i * hb - 1, 0), col))
    nxt = lambda col: pl.BlockSpec((1, CONV_HALO, c), lambda b, i: (b, jnp.minimum((i + 1) * hb, last_hb), col))
    wspec = pl.BlockSpec(conv_w.shape, lambda b, i: (0, 0))
    return pl.pallas_call(
        body, name=name, grid=(bsz, nt),
        in_specs=[cur(0), cur(1), prev(0), prev(1), cur(0), nxt(0), cur(0), nxt(0), wspec, vec, vec, vec],
        out_specs=[pl.BlockSpec((1, tt, 2 * c), lambda b, i: (b, i, 0)), wspec, vec, vec, vec, vec],
        out_shape=[_sds((bsz, seq, 2 * c), BF16), _sds(conv_w.shape, F32)] + [_sds((1, c), F32)] * 4,
        scratch_shapes=[pltpu.VMEM((CONV_HALO + tt, c), F32), pltpu.VMEM((ext, c), F32),
                        pltpu.VMEM((tt + SUBLANES, c), F32), pltpu.VMEM((tt + SUBLANES, c), F32)],
        compiler_params=_cparams("arbitrary", "arbitrary"),
    )(proj3, proj3, proj3, proj3, cv3, cv3, dmix3, dmix3, conv_w, ln_g, ln_b, out_g)


def _ssm_discretise(a_re, a_im, log_dt):
    dt = jnp.exp(log_dt)
    zr, zi = a_re * dt, a_im * dt
    mag = jnp.exp(zr)
    ar, ai = mag * jnp.cos(zi), mag * jnp.sin(zi)
    den = a_re * a_re + a_im * a_im
    nr = ar - 1.0
    return ar, ai, (nr * a_re + ai * a_im) / den, (ai * a_re - nr * a_im) / den


def _ssm_system(a_re, a_im, log_dt, a_re_x, a_im_x, log_dt_x, bt_re, bt_im):
    ar, ai, _, _ = _ssm_discretise(a_re, a_im, log_dt)
    _, _, cr, ci = _ssm_discretise(a_re_x, a_im_x, log_dt_x)
    return ar, ai, cr * bt_re - ci * bt_im, cr * bt_im + ci * bt_re


def _ssm_prep(name, prim):
    g, p = prim[0].shape

    def body(*refs):
        pwr_ref, pwi_ref, bbr_ref, bbi_ref = refs[8:]
        ar, ai, bbr, bbi = _ssm_system(*[r[...] for r in refs[:8]])
        bbr_ref[...] = bbr
        bbi_ref[...] = bbi
        pr, pi = ar, ai
        for k in range(SUBLANES):
            pwr_ref[k] = pr
            pwi_ref[k] = pi
            pr, pi = pr * ar - pi * ai, pr * ai + pi * ar

    return pl.pallas_call(
        body, name=name,
        out_shape=[_sds((SUBLANES, g, p), F32)] * 2 + [_sds(prim[6].shape, F32)] * 2,
        compiler_params=pltpu.CompilerParams(vmem_limit_bytes=VMEM_LIMIT),
    )(*prim)


def _ssm_param_grads(name, prim, dab_r, dab_i, dbb_r, dbb_i):
    g, p = prim[0].shape
    h = prim[6].shape[0] // g

    def body(*refs):
        dar_ref, dai_ref, dbr_ref, dbi_ref = refs[8:12]
        o_ar, o_ai, o_dt, o_br, o_bi = refs[12:]
        _, vjp = jax.vjp(_ssm_system, *[r[...] for r in refs[:8]])
        ct = (jnp.sum(dar_ref[...], axis=0), jnp.sum(dai_ref[...], axis=0), dbr_ref[...], dbi_ref[...])
        d_ar, d_ai, d_dt, d_arx, d_aix, d_dtx, d_br, d_bi = vjp(ct)
        per_group = lambda t: jnp.sum(t.reshape(g, h, p), axis=1)
        o_ar[...] = d_ar + per_group(d_arx)
        o_ai[...] = d_ai + per_group(d_aix)
        o_dt[...] = d_dt + jnp.sum(per_group(d_dtx), axis=1, keepdims=True)
        o_br[...] = d_br
        o_bi[...] = d_bi

    return pl.pallas_call(
        body, name=name,
        out_shape=[_sds(prim[k].shape, F32) for k in (0, 1, 2, 6, 7)],
        compiler_params=pltpu.CompilerParams(vmem_limit_bytes=VMEM_LIMIT),
    )(*prim, dab_r, dab_i, dbb_r, dbb_i)


def _cfma(xr, xi, cr, ci, sr, si):
    return xr + (cr * sr - ci * si), xi + (cr * si + ci * sr)


def _scan_tables(pw_r, pw_i, reverse):
    gp = pw_r.shape[1] * pw_r.shape[2]
    pr, pi = pw_r.reshape(SUBLANES, gp), pw_i.reshape(SUBLANES, gp)
    if reverse:
        pi = -pi
    row = jnp.arange(SUBLANES)[:, None]
    tabs = []
    for d in (1, 2, 4):
        keep = (row < SUBLANES - d) if reverse else (row >= d)
        tabs += [jnp.where(keep, pr[d - 1][None, :], 0.0), jnp.where(keep, pi[d - 1][None, :], 0.0)]
    tabs += [pr[::-1], pi[::-1]] if reverse else [pr, pi]
    return jnp.concatenate(tabs, axis=0)


MXU_DEPTH = 256


def _bands(c, gp):
    bw = min(c, MXU_DEPTH)
    return c // bw, bw, gp * bw // c


def _band_expand(rows16, w_ref, put, c, gp):
    nb, bw, sw = _bands(c, gp)
    for s in range(nb):
        band = rows16[:, s * bw:(s + 1) * bw]
        for half in (0, gp):
            cols = pl.ds(half + s * sw, sw)
            put(cols, _dot(band, w_ref[pl.ds(s * bw, bw), cols]))


def _band_contract(get16, w_ref, c, gp):
    nb, bw, sw = _bands(c, gp)
    out = []
    for s in range(nb):
        acc = None
        for half in (0, gp):
            cols = pl.ds(half + s * sw, sw)
            t = _dot_nt(get16(cols), w_ref[pl.ds(s * bw, bw), cols])
            acc = t if acc is None else acc + t
        out.append(acc)
    return out[0] if nb == 1 else jnp.concatenate(out, axis=1)


def _band_wgrad(name, a, a_block, c, b):
    n = a.shape[0]
    gp = b.shape[1] // 2
    nb, bw, sw = _bands(c, gp)
    tk = _pick(n, TILE["mm_bytes"] // (c * a.dtype.itemsize + 2 * gp * b.dtype.itemsize), 16)

    def body(a_ref, b_ref, o_ref):
        @pl.when(pl.program_id(0) == 0)
        def _():
            o_ref[...] = jnp.zeros_like(o_ref)

        for s in range(nb):
            band = a_ref[:, s * bw:(s + 1) * bw].astype(BF16)
            for h, half in enumerate((0, gp)):
                o_ref[pl.ds(s * bw, bw), pl.ds(h * sw, sw)] += _dot_tn(
                    band, b_ref[:, pl.ds(half + s * sw, sw)].astype(BF16))

    return pl.pallas_call(
        body, name=name, grid=(n // tk,),
        in_specs=[pl.BlockSpec((tk, c), lambda k: (k, a_block)), pl.BlockSpec((tk, 2 * gp), lambda k: (k, 0))],
        out_specs=pl.BlockSpec((c, 2 * sw), lambda k: (0, 0)),
        out_shape=_sds((c, 2 * sw), F32),
        compiler_params=_cparams("arbitrary"),
    )(a, b)


def _band_diag_take(comp, half, c, gp):
    nb, bw, sw = _bands(c, gp)
    return jnp.concatenate([_block_diag_take(comp[s * bw:(s + 1) * bw, half * sw:(half + 1) * sw], bw // SSM_GROUP)
                            for s in range(nb)], axis=0)


def _scan_fwd(name, tab, proj3, u_block, bbd, cdt):
    bsz, seq, _ = proj3.shape
    c, w = bbd.shape
    gp = w // 2
    tt = _pick(seq, TILE["scan_fwd_t"], 16)
    nblk = tt // SUBLANES
    cw = _pick(gp, TILE["scan_w"], LANES)

    def body(tab_ref, u_ref, bbd_ref, cdt_ref, xs_ref, xs16_ref, y_ref, carry_ref, bu_ref):
        @pl.when(pl.program_id(1) == 0)
        def _():
            carry_ref[...] = jnp.zeros_like(carry_ref)

        def put_bu(cols, val):
            bu_ref[0, :, cols] = val

        _band_expand(u_ref[0].astype(BF16), bbd_ref, put_bu, c, gp)

        for ch in range(gp // cw):
            re, im = pl.ds(ch * cw, cw), pl.ds(gp + ch * cw, cw)

            def blk(r, carry, re=re, im=im):
                tabs = [tab_ref[pl.ds(SUBLANES * k, SUBLANES), re] for k in range(8)]
                rows = pl.ds(pl.multiple_of(r * SUBLANES, SUBLANES), SUBLANES)
                xr, xi = bu_ref[0, rows, re], bu_ref[0, rows, im]
                for j, d in enumerate((1, 2, 4)):
                    xr, xi = _cfma(xr, xi, tabs[2 * j], tabs[2 * j + 1], pltpu.roll(xr, d, 0), pltpu.roll(xi, d, 0))
                xr, xi = _cfma(xr, xi, tabs[6], tabs[7], carry[0], carry[1])
                xs_ref[0, rows, re] = xr
                xs_ref[0, rows, im] = xi
                last = SUBLANES - 1
                return (jnp.broadcast_to(xr[last:, :], xr.shape), jnp.broadcast_to(xi[last:, :], xi.shape))

            cr, ci = lax.fori_loop(0, nblk, blk, (carry_ref[:, re], carry_ref[:, im]))
            carry_ref[:, re] = cr
            carry_ref[:, im] = ci

        xs16_ref[0] = xs_ref[0].astype(BF16)
        y_ref[0] = _band_contract(lambda cols: xs16_ref[0, :, cols], cdt_ref, c, gp)

    whole = lambda arr: pl.BlockSpec(arr.shape, lambda b, t: (0, 0), pipeline_mode=pl.Buffered(1))
    wide = pl.BlockSpec((1, tt, w), lambda b, t: (b, t, 0))
    return pl.pallas_call(
        body, name=name, grid=(bsz, seq // tt),
        in_specs=[whole(tab), pl.BlockSpec((1, tt, c), lambda b, t: (b, t, u_block)), whole(bbd), whole(cdt)],
        out_specs=[wide, wide, pl.BlockSpec((1, tt, c), lambda b, t: (b, t, 0))],
        out_shape=[_sds((bsz, seq, w), F32), _sds((bsz, seq, w), BF16), _sds((bsz, seq, c), F32)],
        scratch_shapes=[pltpu.VMEM((SUBLANES, w), F32), pltpu.VMEM((1, tt, w), F32)],
        compiler_params=_cparams("arbitrary", "arbitrary"),
    )(tab, proj3, bbd, cdt)


def _scan_bwd(name, tab, dy3, xs3, du_skip3, bbd, cdt, exchange=None):
    bsz, seq, w = xs3.shape
    c = bbd.shape[0]
    gp = w // 2
    tt = _pick(seq, TILE["scan_t"], 16)
    nblk = tt // SUBLANES
    cw = _pick(gp, TILE["scan_w"], LANES)
    nt = seq // tt

    def body(tab_ref, dy_ref, xs_ref, halo_ref, skip_ref, bbd_ref, cdt_ref, lam16_ref, du_ref, dar_ref, dai_ref,
             carry_ref, g_ref, lam_ref):
        t = pl.program_id(1)

        @pl.when(t == 0)
        def _():
            carry_ref[...] = jnp.zeros_like(carry_ref)

        @pl.when((pl.program_id(0) == 0) & (t == 0))
        def _():
            dar_ref[...] = jnp.zeros_like(dar_ref)
            dai_ref[...] = jnp.zeros_like(dai_ref)

        def put_g(cols, val):
            g_ref[0, :, cols] = val

        _band_expand(dy_ref[0], cdt_ref, put_g, c, gp)

        has_prev = (t < nt - 1).astype(F32)
        row0 = lax.broadcasted_iota(jnp.int32, (SUBLANES, cw), 0) == 0
        last = SUBLANES - 1

        for ch in range(gp // cw):
            re, im = pl.ds(ch * cw, cw), pl.ds(gp + ch * cw, cw)

            def step(rows, xm1r, xm1i, state, re=re, im=im):
                tabs = [tab_ref[pl.ds(SUBLANES * k, SUBLANES), re] for k in range(8)]
                cr, ci, accr, acci = state
                lr, li = g_ref[0, rows, re], g_ref[0, rows, im]
                for j, d in enumerate((1, 2, 4)):
                    lr, li = _cfma(lr, li, tabs[2 * j], tabs[2 * j + 1],
                                   pltpu.roll(lr, SUBLANES - d, 0), pltpu.roll(li, SUBLANES - d, 0))
                lr, li = _cfma(lr, li, tabs[6], tabs[7], cr, ci)
                lam_ref[0, rows, re] = lr
                lam_ref[0, rows, im] = li
                xr, xi = xs_ref[0, rows, re], xs_ref[0, rows, im]
                xpr = jnp.where(row0, jnp.broadcast_to(xm1r[last:, :], xr.shape), pltpu.roll(xr, 1, 0))
                xpi = jnp.where(row0, jnp.broadcast_to(xm1i[last:, :], xi.shape), pltpu.roll(xi, 1, 0))
                accr = accr + (lr * xpr + li * xpi)
                acci = acci + (li * xpr - lr * xpi)
                return (jnp.broadcast_to(lr[:1, :], lr.shape), jnp.broadcast_to(li[:1, :], li.shape), accr, acci)

            def blk(k, state, re=re, im=im, step=step):
                r = nblk - 1 - k
                rows = pl.ds(pl.multiple_of(r * SUBLANES, SUBLANES), SUBLANES)
                prev = pl.ds(pl.multiple_of((r - 1) * SUBLANES, SUBLANES), SUBLANES)
                return step(rows, xs_ref[0, prev, re], xs_ref[0, prev, im], state)

            zero = jnp.zeros((SUBLANES, cw), F32)
            state = lax.fori_loop(0, nblk - 1, blk, (carry_ref[:, re], carry_ref[:, im], zero, zero))
            cr, ci, accr, acci = step(pl.ds(0, SUBLANES), has_prev * halo_ref[0, :, re], has_prev * halo_ref[0, :, im], state)
            carry_ref[:, re] = cr
            carry_ref[:, im] = ci
            dar_ref[:, re] += accr
            dai_ref[:, re] += acci

        lam16_ref[0] = lam_ref[0].astype(BF16)
        du = _band_contract(lambda cols: lam16_ref[0, :, cols], bbd_ref, c, gp)
        du_ref[0] = (du + skip_ref[0]).astype(BF16)

    tile = pl.BlockSpec((1, tt, w), lambda b, t: (b, nt - 1 - t, 0))
    thin = pl.BlockSpec((1, tt, c), lambda b, t: (b, nt - 1 - t, 0))
    halo = pl.BlockSpec((1, SUBLANES, w), lambda b, t: (b, jnp.maximum((nt - 1 - t) * nblk - 1, 0), 0))
    acc = pl.BlockSpec((SUBLANES, gp), lambda b, t: (0, 0))
    whole = lambda arr: pl.BlockSpec(arr.shape, lambda b, t: (0, 0), pipeline_mode=pl.Buffered(1))
    return _call(
        name, body, (bsz, nt), [whole(tab), thin, tile, halo, thin, whole(bbd), whole(cdt)], [tile, thin, acc, acc],
        [_sds(xs3.shape, BF16), _sds((bsz, seq, c), BF16), _sds((SUBLANES, gp), F32), _sds((SUBLANES, gp), F32)],
        (tab, dy3, xs3, xs3, du_skip3, bbd, cdt), ("arbitrary", "arbitrary"),
        scratch=[pltpu.VMEM((SUBLANES, w), F32), pltpu.VMEM((1, tt, w), F32), pltpu.VMEM((1, tt, w), F32)],
        exchange=exchange)


def _gelu_parts(y):
    inner = _GELU_K * (y + _GELU_C * y * y * y)
    t = jnp.tanh(inner)
    return 0.5 * y * (1.0 + t), t


def _ssm_out_fwd(name, cx, proj, u_block, d_skip, glu_w, glu_b, out_g, x, conv_out, w_out):
    n, c = cx.shape
    d = x.shape[1]
    tm = _pick(n, TILE["row"], 16)

    def body(cx_ref, u_ref, d_ref, gw_ref, gb_ref, og_ref, x_ref, a_ref, wo_ref, y_ref, o_ref, xo_ref):
        y = cx_ref[...] + d_ref[...] * u_ref[...]
        y_ref[...] = y
        gy, _ = _gelu_parts(y)
        z = _dot(gy.astype(BF16), gw_ref[...]) + gb_ref[...]
        _, sh = _rms_stats(gy * _sigmoid(z))
        out = (sh * og_ref[...]).astype(BF16)
        o_ref[...] = out
        xo_ref[...] = x_ref[...] + _dot(a_ref[...], wo_ref[pl.ds(0, c), :]) + _dot(out, wo_ref[pl.ds(c, c), :])

    vec = pl.BlockSpec((1, c), lambda i: (0, 0))
    row = pl.BlockSpec((tm, c), lambda i: (i, 0))
    wide = pl.BlockSpec((tm, d), lambda i: (i, 0))
    held = lambda arr: pl.BlockSpec(arr.shape, lambda i: (0, 0), pipeline_mode=pl.Buffered(1))
    return pl.pallas_call(
        body, name=name, grid=(n // tm,),
        in_specs=[row, pl.BlockSpec((tm, c), lambda i: (i, u_block)), vec, held(glu_w), vec, vec, wide, row, held(w_out)],
        out_specs=[row, row, wide],
        out_shape=[_sds((n, c), F32), _sds((n, c), BF16), _sds((n, d), F32)],
        compiler_params=_cparams("parallel"),
    )(cx, proj, d_skip, glu_w, glu_b, out_g, x, conv_out, w_out)


def _ssm_out_bwd(name, dmix, d_block, y, proj, u_block, d_skip, glu_w, glu_b, out_g):
    n, c = y.shape
    tm = _pick(n, TILE["row"], 16)

    def body(d_ref, y_ref, u_ref, dk_ref, gw_ref, gb_ref, og_ref, dy_ref, du_ref, dgw_ref, dgb_ref, dog_ref, dd_ref):
        @pl.when(pl.program_id(0) == 0)
        def _():
            for r in (dgw_ref, dgb_ref, dog_ref, dd_ref):
                r[...] = jnp.zeros_like(r)

        yv = y_ref[...]
        gy, th = _gelu_parts(yv)
        gy16 = gy.astype(BF16)
        sz = _sigmoid(_dot(gy16, gw_ref[...]) + gb_ref[...])
        r, sh = _rms_stats(gy * sz)
        dout = d_ref[...]
        dog_ref[...] += jnp.sum(dout * sh, axis=0, keepdims=True)
        dsh = dout * og_ref[...]
        ds = r * (dsh - sh * jnp.mean(dsh * sh, axis=-1, keepdims=True))
        dz = ds * gy * sz * (1.0 - sz)
        dz16 = dz.astype(BF16)
        dgb_ref[...] += jnp.sum(dz, axis=0, keepdims=True)
        dgw_ref[...] += _dot_tn(gy16, dz16)
        dgy = ds * sz + _dot_nt(dz16, gw_ref[...])
        dgelu = 0.5 * (1.0 + th) + 0.5 * yv * (1.0 - th * th) * (_GELU_K * (1.0 + 3.0 * _GELU_C * yv * yv))
        dy = dgy * dgelu
        dy_ref[...] = dy.astype(BF16)
        du_ref[...] = dy * dk_ref[...]
        dd_ref[...] += jnp.sum(dy * u_ref[...], axis=0, keepdims=True)

    vec = pl.BlockSpec((1, c), lambda i: (0, 0))
    row = pl.BlockSpec((tm, c), lambda i: (i, 0))
    mat = pl.BlockSpec(glu_w.shape, lambda i: (0, 0))
    return pl.pallas_call(
        body, name=name, grid=(n // tm,),
        in_specs=[pl.BlockSpec((tm, c), lambda i: (i, d_block)), row, pl.BlockSpec((tm, c), lambda i: (i, u_block)),
                  vec, mat, vec, vec],
        out_specs=[row, row, mat, vec, vec, vec],
        out_shape=[_sds((n, c), BF16), _sds((n, c), F32), _sds(glu_w.shape, F32)] + [_sds((1, c), F32)] * 3,
        compiler_params=_cparams("arbitrary"),
    )(dmix, y, proj, d_skip, glu_w, glu_b, out_g)


def _mesh_pos():
    return tuple(lax.axis_index(a) for a in MESH_AXES)


def _other_chips(x, y):
    return [(1 - x, y), (x, 1 - y), (1 - x, 1 - y)]


def _remote(src, dst, send_sem, recv_sem, dev):
    return pltpu.make_async_remote_copy(src_ref=src, dst_ref=dst, send_sem=send_sem, recv_sem=recv_sem,
                                        device_id=dev, device_id_type=pl.DeviceIdType.MESH)


def _hbm_call(name, body, operands, out_shapes, scratch):
    hbm = pl.BlockSpec(memory_space=pltpu.HBM)
    return pl.pallas_call(body, name=name, in_specs=[hbm] * len(operands), out_specs=[hbm] * len(out_shapes),
                          out_shape=out_shapes, scratch_shapes=scratch)(*operands)


_Exchange = collections.namedtuple("_Exchange", "operands out_shapes scratch start finish")


def _merge_plans(p, q):
    cut = len(p.operands), len(p.out_shapes), len(p.scratch)

    def both(which):
        def run(x_refs, o_refs, sems):
            getattr(p, which)(x_refs[:cut[0]], o_refs[:cut[1]], sems[:cut[2]])
            getattr(q, which)(x_refs[cut[0]:], o_refs[cut[1]:], sems[cut[2]:])
        return run

    return _Exchange(p.operands + q.operands, p.out_shapes + q.out_shapes, p.scratch + q.scratch,
                     both("start"), both("finish"))


def _run_exchange(name, plan):
    nin, nout = len(plan.operands), len(plan.out_shapes)

    def body(*refs):
        parts = refs[:nin], refs[nin:nin + nout], refs[nin + nout:]
        plan.start(*parts)
        plan.finish(*parts)

    return _hbm_call(name, body, plan.operands, plan.out_shapes, plan.scratch)


def _gather_plan(blocks):
    nop = len(blocks)

    def copies(x_refs, o_refs, sems):
        send_sems, recv_sems, local_sems = sems
        x, y, c = _mesh_pos()
        me, sibling = (x, y, c), (x, y, 1 - c)
        chips = _other_chips(x, y)

        def copy(i, k, block_of, to, src=None):
            dst = o_refs[i].at[4 * block_of[0] + 2 * block_of[1] + block_of[2]]
            return _remote(dst if src is None else src, dst, send_sems.at[i, k], recv_sems.at[i, k], to)

        own = [pltpu.make_async_copy(x_refs[i], o_refs[i].at[4 * x + 2 * y + c], local_sems.at[i]) for i in range(nop)]
        first = []
        for i in range(nop):
            first.append(copy(i, 0, me, sibling, src=x_refs[i]))
            first += [copy(i, 1 + j, me, (*chip, c), src=x_refs[i]) for j, chip in enumerate(chips)]
        return copy, own, first, me, sibling, chips, c

    def start(x_refs, o_refs, sems):
        _, own, first, *_ = copies(x_refs, o_refs, sems)
        for cp in own + first:
            cp.start()

    def finish(x_refs, o_refs, sems):
        copy, own, first, me, sibling, chips, c = copies(x_refs, o_refs, sems)
        passed = []
        for i in range(nop):
            for j, chip in enumerate(chips):
                copy(i, 1 + j, (*chip, c), me).wait_recv()
                passed.append(copy(i, 4 + j, (*chip, c), sibling))
                passed[-1].start()
        for i in range(nop):
            copy(i, 0, sibling, me).wait_recv()
            for j, chip in enumerate(chips):
                copy(i, 4 + j, (*chip, 1 - c), me).wait_recv()
        for cp in first + passed:
            cp.wait_send()
        for cp in own:
            cp.wait()

    return _Exchange(list(blocks), [_sds((N_DEV,) + b.shape, b.dtype) for b in blocks],
                     [pltpu.SemaphoreType.DMA((nop, N_DEV - 1)), pltpu.SemaphoreType.DMA((nop, N_DEV - 1)),
                      pltpu.SemaphoreType.DMA((nop,))], start, finish)


def _core_exchange_plan(grads):
    nop = len(grads)

    def copies(x_refs, o_refs, sems):
        send_sems, recv_sems = sems
        x, y, c = _mesh_pos()
        return [_remote(x_refs[i].at[2 * q + (1 - c)], o_refs[i].at[q], send_sems.at[i, q], recv_sems.at[i, q],
                        (x, y, 1 - c)) for i in range(nop) for q in range(N_DEV // 2)]

    def start(x_refs, o_refs, sems):
        for cp in copies(x_refs, o_refs, sems):
            cp.start()

    def finish(x_refs, o_refs, sems):
        for cp in copies(x_refs, o_refs, sems):
            cp.wait()

    return _Exchange(list(grads), [_sds((N_DEV // 2,) + g.shape[1:], g.dtype) for g in grads],
                     [pltpu.SemaphoreType.DMA((nop, N_DEV // 2)), pltpu.SemaphoreType.DMA((nop, N_DEV // 2))],
                     start, finish)


def _pair_sum(name, grad, other):
    nchip, _, r, c = grad.shape
    tr = _pick(r, max(SUBLANES, TILE["sum_bytes"] // (4 * c)), SUBLANES)
    core = lax.axis_index("c").astype(jnp.int32).reshape(1)

    def body(core_ref, g_ref, o_ref, s_ref):
        s_ref[0] = (g_ref[0, 0] + o_ref[0]).astype(s_ref.dtype)

    tile = pl.BlockSpec((1, tr, c), lambda q, t, core_ref: (q, t, 0))
    return pl.pallas_call(
        body, name=name,
        grid_spec=pltpu.PrefetchScalarGridSpec(
            num_scalar_prefetch=1, grid=(nchip, r // tr),
            in_specs=[pl.BlockSpec((1, 1, tr, c), lambda q, t, core_ref: (q, core_ref[0], t, 0)), tile],
            out_specs=tile),
        out_shape=_sds((nchip, r, c), BF16),
        compiler_params=_cparams("parallel", "parallel"),
    )(core, grad, other)


def _chip_exchange_plan(sums):
    nop = len(sums)

    def copies(x_refs, o_refs, sems, arriving):
        send_sems, recv_sems, local_sems = sems
        x, y, c = _mesh_pos()
        mine = 2 * x + y
        out = []
        for i in range(nop):
            for j, (px, py) in enumerate(_other_chips(x, y)):
                theirs = 2 * px + py
                src, dst = (mine, theirs) if arriving else (theirs, mine)
                out.append(_remote(x_refs[i].at[src], o_refs[i].at[dst], send_sems.at[i, j], recv_sems.at[i, j],
                                   (px, py, c)))
        if not arriving:
            out += [pltpu.make_async_copy(x_refs[i].at[mine], o_refs[i].at[mine], local_sems.at[i]) for i in range(nop)]
        return out

    def start(x_refs, o_refs, sems):
        for cp in copies(x_refs, o_refs, sems, False):
            cp.start()

    def finish(x_refs, o_refs, sems):
        for cp in copies(x_refs, o_refs, sems, True):
            cp.wait_recv()
        mine = copies(x_refs, o_refs, sems, False)
        for cp in mine[:3 * nop]:
            cp.wait_send()
        for cp in mine[3 * nop:]:
            cp.wait()

    return _Exchange(list(sums), [_sds(s.shape, s.dtype) for s in sums],
                     [pltpu.SemaphoreType.DMA((nop, 3)), pltpu.SemaphoreType.DMA((nop, 3)), pltpu.SemaphoreType.DMA((nop,))],
                     start, finish)


def _part_rows(npart, r, c):
    return _pick(r, max(SUBLANES, TILE["sum_bytes"] // (4 * npart * c)), SUBLANES)


def _sum_slots(p_ref):
    g = p_ref[0].astype(F32)
    for k in range(1, p_ref.shape[0]):
        g = g + p_ref[k].astype(F32)
    return g


def _adamw_step(g, w, m, v):
    c1 = 1.0 - ADAM_B1 ** ADAM_STEP
    c2 = 1.0 - ADAM_B2 ** ADAM_STEP
    nm = ADAM_B1 * m + (1.0 - ADAM_B1) * g
    nv = ADAM_B2 * v + (1.0 - ADAM_B2) * (g * g)
    return -ADAM_LR * ((nm / c1) / (jnp.sqrt(nv / c2) + ADAM_EPS) + ADAM_WD * w), nm, nv


def _adamw_small(name, parts, ws, ms, vs):
    nparam, nall = len(ws), len(parts)

    def body(*refs):
        p_refs = refs[:nall]
        w_refs, m_refs, v_refs = (refs[nall + k * nparam:nall + (k + 1) * nparam] for k in range(3))
        outs = refs[nall + 3 * nparam:]
        for p in range(nall):
            g = _sum_slots(p_refs[p])
            if p < nparam:
                delta, nm, nv = _adamw_step(g, w_refs[p][...], m_refs[p][...], v_refs[p][...])
                for o_ref, val in zip(outs[4 * p:4 * p + 4], (g, delta, nm, nv)):
                    o_ref[...] = val
            else:
                outs[4 * nparam + p - nparam][...] = g

    shapes = [_sds(w.shape, F32) for w in ws for _ in range(4)] + [_sds(p.shape[1:], F32) for p in parts[nparam:]]
    res = pl.pallas_call(body, name=name, out_shape=shapes,
                         compiler_params=pltpu.CompilerParams(vmem_limit_bytes=VMEM_LIMIT))(*parts, *ws, *ms, *vs)
    return [res[4 * p:4 * p + 4] for p in range(nparam)] + [[r] for r in res[4 * nparam:]]


def _adamw(name, parts, w, m, v):
    npart, r, c = parts.shape
    lead = len(w.shape) - 2
    tr = _part_rows(npart, r, c)
    at = (0,) * lead + (slice(None), slice(None))

    def body(p_ref, w_ref, m_ref, v_ref, g_ref, d_ref, nm_ref, nv_ref):
        g = _sum_slots(p_ref)
        delta, nm, nv = _adamw_step(g, w_ref[at], m_ref[at], v_ref[at])
        g_ref[at] = g
        nm_ref[at] = nm
        nv_ref[at] = nv
        d_ref[at] = delta

    row = pl.BlockSpec((1,) * lead + (tr, c), lambda i: (0,) * lead + (i, 0))
    return pl.pallas_call(
        body, name=name, grid=(r // tr,),
        in_specs=[pl.BlockSpec((npart, tr, c), lambda i: (0, i, 0)), row, row, row],
        out_specs=[row] * 4,
        out_shape=[_sds(w.shape, F32)] * 4,
        compiler_params=_cparams("parallel"),
    )(parts, w, m, v)


def _block_diag(rows_gh, groups):
    gh, p = rows_gh.shape
    own = (jnp.arange(gh)[:, None] // (gh // groups) == jnp.arange(groups)[None, :]).astype(rows_gh.dtype)
    return (own[:, :, None] * rows_gh[:, None, :]).reshape(gh, groups * p)


def _block_diag_take(dense, groups):
    gh = dense.shape[0]
    p = dense.shape[1] // groups
    own = (jnp.arange(gh)[:, None] // (gh // groups) == jnp.arange(groups)[None, :]).astype(dense.dtype)
    return jnp.sum(dense.reshape(gh, groups, p) * own[:, :, None], axis=1)


FFN1 = ("ffn1_w1", "ffn1_w3", "ffn1_w2")
MIXER = ("w_in", "ssm_glu_w", "w_out")
FFN2 = ("ffn2_w1", "ffn2_w3", "ffn2_w2")
BIG = FFN1 + MIXER + FFN2
COL_SHARDED = ("ffn1_w1", "ffn1_w3", "w_in", "ffn2_w1", "ffn2_w3", "conv_w")
SMALL = ("norm_ffn1", "norm_mix", "conv_b", "conv_ln_g", "conv_ln_b", "conv_out_g", "ssm_A_re", "ssm_A_im",
         "ssm_log_dt", "ssm_B_re", "ssm_B_im", "ssm_C_re", "ssm_C_im", "ssm_D", "ssm_glu_b", "ssm_out_g",
         "norm_ffn2", "norm_final")
WEIGHTS = ("norm_ffn1", "ffn1_w1", "ffn1_w3", "ffn1_w2", "norm_mix", "w_in", "conv_w", "conv_b", "conv_ln_g",
           "conv_ln_b", "conv_out_g", "ssm_A_re", "ssm_A_im", "ssm_log_dt", "ssm_B_re", "ssm_B_im", "ssm_C_re",
           "ssm_C_im", "ssm_D", "ssm_glu_w", "ssm_glu_b", "ssm_out_g", "w_out", "norm_ffn2", "ffn2_w1", "ffn2_w3",
           "ffn2_w2", "norm_final")


def _ffn_backward(tag, dxo, x, g, w1, w3, w2, saved, exchange=None, reduce_names=None):
    a, b, h = saved
    (da, db, hid, dxh), got = _ffn_bwd_hidden(tag + "_bwd_hidden", dxo, a, b, w2, exchange=exchange)
    dw1, dw3 = _mm_tn(tag + "_dw1", da, h), _mm_tn(tag + "_dw3", db, h)
    across = None
    if reduce_names:
        send = [_row_blocks(dw1), _row_blocks(dw3)]
        dw2, from_core = _mm_tn(tag + "_dw2", hid, dxh, exchange=_core_exchange_plan(send))
        send.append(_row_blocks(dw2))
        from_core += _run_exchange("exchange_core_" + tag, _core_exchange_plan(send[2:]))
        across = _across_chips(reduce_names, send, from_core)
    else:
        dw2 = _mm_tn(tag + "_dw2", hid, dxh)
    f = a.shape[1]
    (dx, dg), reduced = _dx_rms_bwd(tag + "_bwd_dx", [(da, f, 0, w1, f, 0), (db, f, 0, w3, f, 0)], dxo, x, g,
                                    exchange=across)
    return (dx, dg, [dw1, dw3, dw2]), got, reduced


def _row_blocks(grad):
    return grad.reshape((N_DEV, -1) + grad.shape[1:])


def _across_chips(names, send, from_core):
    return _chip_exchange_plan([_pair_sum("pair_sum_" + k, s.reshape((N_DEV // 2, 2) + s.shape[1:]), o)
                                for k, s, o in zip(names, send, from_core)])


def _reduce_in_chip(names, grads):
    send = [_row_blocks(g) for g in grads]
    return _core_exchange_plan(send), functools.partial(_across_chips, names, send)


def kernel(x, norm_ffn1, ffn1_w1, ffn1_w3, ffn1_w2, norm_mix, w_in, conv_w, conv_b, conv_ln_g, conv_ln_b, conv_out_g, ssm_A_re, ssm_A_im, ssm_log_dt, ssm_B_re, ssm_B_im, ssm_C_re, ssm_C_im, ssm_D, ssm_glu_w, ssm_glu_b, ssm_out_g, w_out, norm_ffn2, ffn2_w1, ffn2_w3, ffn2_w2, norm_final, loss_target, m_norm_ffn1, m_ffn1_w1, m_ffn1_w3, m_ffn1_w2, m_norm_mix, m_w_in, m_conv_w, m_conv_b, m_conv_ln_g, m_conv_ln_b, m_conv_out_g, m_ssm_A_re, m_ssm_A_im, m_ssm_log_dt, m_ssm_B_re, m_ssm_B_im, m_ssm_C_re, m_ssm_C_im, m_ssm_D, m_ssm_glu_w, m_ssm_glu_b, m_ssm_out_g, m_w_out, m_norm_ffn2, m_ffn2_w1, m_ffn2_w3, m_ffn2_w2, m_norm_final, v_norm_ffn1, v_ffn1_w1, v_ffn1_w3, v_ffn1_w2, v_norm_mix, v_w_in, v_conv_w, v_conv_b, v_conv_ln_g, v_conv_ln_b, v_conv_out_g, v_ssm_A_re, v_ssm_A_im, v_ssm_log_dt, v_ssm_B_re, v_ssm_B_im, v_ssm_C_re, v_ssm_C_im, v_ssm_D, v_ssm_glu_w, v_ssm_glu_b, v_ssm_out_g, v_w_out, v_norm_ffn2, v_ffn2_w1, v_ffn2_w3, v_ffn2_w2, v_norm_final):
    args = dict(locals())
    wt = {n: args[n] for n in WEIGHTS}
    mom = {n: args["m_" + n] for n in WEIGHTS}
    var = {n: args["v_" + n] for n in WEIGHTS}

    bsz, seq, d = x.shape
    n = bsz * seq
    c = conv_b.shape[-1]
    groups = c // SSM_GROUP
    gp = groups * SSM_STATE
    u_b = 2

    shard = {k: (wt[k][0].T if k in COL_SHARDED else wt[k][0]).astype(BF16) for k in BIG}
    gathered = _run_exchange("gather_weights_ffn1", _gather_plan([shard[k] for k in FFN1]))
    full = {k: g.reshape(-1, g.shape[-1]) for k, g in zip(FFN1, gathered)}
    gather_rest = _gather_plan([shard[k] for k in MIXER + FFN2] + [wt["conv_w"][0]])

    vec = lambda k: wt[k].reshape(1, -1)
    g_ffn1, g_mix, g_ffn2, g_fin = vec("norm_ffn1"), vec("norm_mix"), vec("norm_ffn2"), vec("norm_final")
    cb, lng, lnb, cog = vec("conv_b"), vec("conv_ln_g"), vec("conv_ln_b"), vec("conv_out_g")
    d_skip, glu_b, sog = vec("ssm_D"), vec("ssm_glu_b"), vec("ssm_out_g")

    a_re, a_im = wt["ssm_A_re"][0], wt["ssm_A_im"][0]
    log_dt = wt["ssm_log_dt"][0].reshape(groups, 1)
    bt_re = wt["ssm_B_re"][0].transpose(0, 2, 1).reshape(groups * SSM_GROUP, SSM_STATE)
    bt_im = wt["ssm_B_im"][0].transpose(0, 2, 1).reshape(groups * SSM_GROUP, SSM_STATE)
    c_re = wt["ssm_C_re"][0].reshape(groups * SSM_GROUP, SSM_STATE)
    c_im = wt["ssm_C_im"][0].reshape(groups * SSM_GROUP, SSM_STATE)
    per_chan = lambda t: jnp.repeat(t, SSM_GROUP, axis=0)
    ssm_prim = (a_re, a_im, log_dt, per_chan(a_re), per_chan(a_im), per_chan(jnp.broadcast_to(log_dt, a_re.shape)),
                bt_re, bt_im)
    pw_r, pw_i, bb_r, bb_i = _ssm_prep("ssm_prep", ssm_prim)
    tab_f = _scan_tables(pw_r, pw_i, False)
    tab_b = _scan_tables(pw_r, pw_i, True)
    bbd = jnp.concatenate([_block_diag(bb_r, groups), _block_diag(bb_i, groups)], axis=1).astype(BF16)
    cdt = jnp.concatenate([_block_diag(c_re, groups), -_block_diag(c_im, groups)], axis=1).astype(BF16)

    x0 = x.reshape(n, d)
    (x1, *ffn1_saved), gathered = _ffn_fwd("ffn1_fwd", x0, g_ffn1, full["ffn1_w1"], full["ffn1_w3"], full["ffn1_w2"],
                                           exchange=gather_rest)
    full.update({k: g.reshape(-1, g.shape[-1]) for k, g in zip(MIXER + FFN2, gathered)})
    conv_w_full = gathered[-1].transpose(1, 0, 2).reshape(CONV_WIDTH, c)
    conv_w_pad = jnp.pad(conv_w_full, ((0, CONV_HALO - CONV_WIDTH), (0, 0)))
    (proj,), h2 = _rms_mm("mix_in", x1, g_mix, [full["w_in"]], F32)
    proj3 = proj.reshape(bsz, seq, 3 * c)
    an3, cv3 = _conv_fwd("conv_fwd", proj3, conv_w_pad, cb, lng, lnb, cog)
    an = an3.reshape(n, c)
    xs3, xs16, cx3 = _scan_fwd("scan_fwd", tab_f, proj3, u_b, bbd, cdt)
    w_o = full["w_out"]
    y, sn, x2 = _ssm_out_fwd("ssm_out_fwd", cx3.reshape(n, c), proj, u_b, d_skip, full["ssm_glu_w"], glu_b, sog,
                             x1, an, w_o)
    (dx3, *ffn2_saved, loss_tile, d_gfin), _ = _ffn_fwd(
        "ffn2_fwd", x2, g_ffn2, full["ffn2_w1"], full["ffn2_w3"], full["ffn2_w2"],
        head=(g_fin, loss_target.reshape(n, d)))

    grads, from_chips = {}, {}
    (dx2, grads["norm_ffn2"], dws), _, _ = _ffn_backward(
        "ffn2", dx3, x2, g_ffn2, full["ffn2_w1"], full["ffn2_w3"], full["ffn2_w2"], ffn2_saved)
    in_chip, across_chips = _reduce_in_chip(FFN2, dws)

    dmix, got = _mm_nt("mix_out_bwd", dx2, w_o, exchange=in_chip)
    reduce_ffn2 = across_chips(got)
    grads["w_out"] = _mm_tn("dw_out", (an, sn), dx2)

    dy, du_skip, grads["ssm_glu_w"], grads["ssm_glu_b"], grads["ssm_out_g"], grads["ssm_D"] = _ssm_out_bwd(
        "ssm_out_bwd", dmix, 1, y, proj, u_b, d_skip, full["ssm_glu_w"], glu_b, sog)
    (lam3, du3, dab_r, dab_i), got = _scan_bwd("scan_bwd", tab_b, dy.reshape(bsz, seq, c), xs3,
                                               du_skip.reshape(bsz, seq, c), bbd, cdt, exchange=reduce_ffn2)
    from_chips.update(zip(FFN2, got))
    lam, du = lam3.reshape(n, 2 * gp), du3.reshape(n, c)
    d_bbd = _band_wgrad("ssm_dbb", proj, u_b, c, lam)
    d_cdt = _band_wgrad("ssm_dc", dy, 0, c, xs16.reshape(n, 2 * gp))
    d_are, d_aim, d_ldt, d_btr, d_bti = _ssm_param_grads(
        "ssm_param_grads", ssm_prim,
        dab_r.reshape(SUBLANES, groups, SSM_STATE), dab_i.reshape(SUBLANES, groups, SSM_STATE),
        _band_diag_take(d_bbd, 0, c, gp), _band_diag_take(d_bbd, 1, c, gp))
    grads["ssm_A_re"], grads["ssm_A_im"], grads["ssm_log_dt"] = d_are, d_aim, d_ldt
    grads["ssm_B_re"], grads["ssm_B_im"] = d_btr, d_bti
    grads["ssm_C_re"] = _band_diag_take(d_cdt, 0, c, gp)
    grads["ssm_C_im"] = -_band_diag_take(d_cdt, 1, c, gp)

    dconv3, d_cw, grads["conv_b"], grads["conv_ln_g"], grads["conv_ln_b"], grads["conv_out_g"] = _conv_bwd(
        "conv_bwd", dmix.reshape(bsz, seq, 2 * c), proj3, cv3, conv_w_pad, lng, lnb, cog)
    dconv = dconv3.reshape(n, 2 * c)
    grads["conv_w"] = d_cw[:CONV_WIDTH]
    grads["w_in"] = _mm_tn("dw_in", (dconv, du), h2)
    w_i = full["w_in"]
    in_chip, across_chips = _reduce_in_chip(MIXER, [grads[k] for k in MIXER])
    (dx1, grads["norm_mix"]), got = _dx_rms_bwd("mix_in_bwd", [(dconv, 2 * c, 0, w_i, 2 * c, 0), (du, c, 0, w_i, c, 2)],
                                                dx2, x1, g_mix, exchange=in_chip)
    reduce_mixer = across_chips(got)

    grads["norm_final"] = d_gfin
    early = tuple(k for k in SMALL if k != "norm_ffn1")
    gather_small = _gather_plan([grads[k] for k in early] + [grads["conv_w"], loss_tile])

    (dx0, grads["norm_ffn1"], _), got, reduced = _ffn_backward(
        "ffn1", dx1, x0, g_ffn1, full["ffn1_w1"], full["ffn1_w3"], full["ffn1_w2"], ffn1_saved,
        exchange=_merge_plans(reduce_mixer, gather_small), reduce_names=FFN1)
    from_chips.update(zip(MIXER, got))
    small_parts = got[len(MIXER):]
    from_chips.update(zip(FFN1, reduced))

    res = {}
    for k in BIG:
        parts = from_chips[k]
        if k in COL_SHARDED:
            swap = lambda t: jnp.swapaxes(t, -1, -2)
            res[k] = [swap(t) for t in _adamw("adamw_" + k, parts, swap(wt[k]), swap(mom[k]), swap(var[k]))]
        else:
            res[k] = _adamw("adamw_" + k, parts, wt[k], mom[k], var[k])

    def as_2d(k, t):
        if k in ("ssm_B_re", "ssm_B_im"):
            return t[0].transpose(0, 2, 1).reshape(-1, SSM_STATE)
        if k in ("ssm_C_re", "ssm_C_im"):
            return t[0].reshape(-1, SSM_STATE)
        if k in ("ssm_A_re", "ssm_A_im"):
            return t[0]
        return t.reshape(-1, 1) if k == "ssm_log_dt" else t.reshape(1, -1)

    def as_param(k, t):
        if k in ("ssm_B_re", "ssm_B_im"):
            t = t.reshape(groups, SSM_GROUP, SSM_STATE).transpose(0, 2, 1)
        return t.reshape(wt[k].shape)

    (last_part,) = _run_exchange("gather_norm_ffn1_grad", _gather_plan([grads["norm_ffn1"]]))
    order = ("norm_ffn1",) + early
    updated = _adamw_small("adamw_replicated", [last_part] + small_parts,
                           *[[as_2d(k, src[k]) for k in order] for src in (wt, mom, var)])
    res.update({k: [as_param(k, t) for t in upd] for k, upd in zip(order, updated)})
    (conv_w_grad,), (loss_sum,) = updated[-2:]
    loss = loss_sum[0, 0]
    x_pos, y_pos, c_pos = (lax.axis_index(a) for a in MESH_AXES)
    cw_cols = c // N_DEV
    own_cw = lax.dynamic_slice_in_dim(conv_w_grad, (4 * x_pos + 2 * y_pos + c_pos) * cw_cols, cw_cols, axis=1)
    res["conv_w"] = _adamw("adamw_conv_w", own_cw[None], wt["conv_w"], mom["conv_w"], var["conv_w"])

    outs = [loss, dx0.reshape(bsz, seq, d)]
    for kind in range(4):
        outs += [res[k][kind] for k in WEIGHTS]
    return tuple(outs)
```

```python
import collections
import functools
import math

import jax
import jax.numpy as jnp
from jax import lax
from jax.experimental import pallas as pl
from jax.experimental.pallas import tpu as pltpu

F32 = jnp.float32
BF16 = jnp.bfloat16

EPS = 1e-6
FFN_RES = 0.5
CONV_WIDTH = 31
CONV_HALO = 32
SSM_GROUP = 16
SSM_STATE = 64
ADAM_LR, ADAM_B1, ADAM_B2, ADAM_EPS, ADAM_WD, ADAM_STEP = 0.001, 0.9, 0.999, 1e-08, 0.01, 10

N_DEV = 8
MESH_AXES = ("x", "y", "c")
SUBLANES = 8
LANES = 128
V7X_VMEM_BYTES = 64 * 2**20
VMEM_LIMIT = V7X_VMEM_BYTES - 8 * 2**20

TILE = dict(row=512, hid_m=512, ffn_m=512, mm_bytes=8 * 2**20, up_m=1024, up_n=256, wide_n=2048, conv_t=1024,
            scan_fwd_t=512, scan_t=256, scan_w=512, sum_bytes=4 * 2**20)

_GELU_K = math.sqrt(2.0 / math.pi)
_GELU_C = 0.044715


def _pick(n, target, mult):
    best = None
    for t in range(mult, min(n, target) + 1, mult):
        if n % t == 0:
            best = t
    return n if best is None else best


def _cparams(*sem):
    return pltpu.CompilerParams(dimension_semantics=sem, vmem_limit_bytes=VMEM_LIMIT)


def _sds(shape, dtype):
    return jax.ShapeDtypeStruct(shape, dtype)


def _call(name, body, grid, in_specs, out_specs, out_shape, operands, sem, scratch=(), exchange=None):
    if exchange is None:
        res = pl.pallas_call(body, name=name, grid=grid, in_specs=list(in_specs), out_specs=list(out_specs),
                             out_shape=list(out_shape), scratch_shapes=list(scratch),
                             compiler_params=_cparams(*sem))(*operands)
        return list(res), None
    n_in, n_out, n_scr = len(in_specs), len(out_specs), len(scratch)
    n_xin, n_xout = len(exchange.operands), len(exchange.out_shapes)
    hbm = pl.BlockSpec(memory_space=pltpu.HBM)

    def with_exchange(*refs):
        cuts, pos = [], 0
        for size in (n_in, n_xin, n_out, n_xout, n_scr):
            cuts.append(refs[pos:pos + size])
            pos += size
        ins, x_in, outs, x_out, scr = cuts
        sems = refs[pos:]
        ids = [pl.program_id(axis) for axis in range(len(grid))]
        first = functools.reduce(lambda p, q: p & q, [i == 0 for i in ids])
        last = functools.reduce(lambda p, q: p & q, [i == g - 1 for i, g in zip(ids, grid)])

        @pl.when(first)
        def _():
            exchange.start(x_in, x_out, sems)

        body(*ins, *outs, *scr)

        @pl.when(last)
        def _():
            exchange.finish(x_in, x_out, sems)

    res = pl.pallas_call(
        with_exchange, name=name, grid=grid, in_specs=list(in_specs) + [hbm] * n_xin,
        out_specs=list(out_specs) + [hbm] * n_xout, out_shape=list(out_shape) + list(exchange.out_shapes),
        scratch_shapes=list(scratch) + list(exchange.scratch),
        compiler_params=_cparams(*["arbitrary"] * len(grid)))(*operands, *exchange.operands)
    return list(res[:n_out]), list(res[n_out:])


def _dot(a, b):
    return jnp.dot(a, b, preferred_element_type=F32)


def _dot_nt(a, b):
    return lax.dot_general(a, b, (((1,), (1,)), ((), ())), preferred_element_type=F32)


def _dot_tn(a, b):
    return lax.dot_general(a, b, (((0,), (0,)), ((), ())), preferred_element_type=F32)


def _sigmoid(x):
    return 0.5 * jnp.tanh(0.5 * x) + 0.5


def _rms_stats(x):
    r = lax.rsqrt(jnp.mean(x * x, axis=-1, keepdims=True) + EPS)
    return r, x * r


def _rms_bwd(x, g, dy):
    r, xh = _rms_stats(x)
    dxh = dy * g
    dx = r * (dxh - xh * jnp.mean(dxh * xh, axis=-1, keepdims=True))
    return dx, jnp.sum(dy * xh, axis=0, keepdims=True)


def _rms_mm(name, x, g, ws, out_dtype):
    n, d = x.shape
    f = ws[0].shape[0]
    nw = len(ws)
    tm, tn = _pick(n, TILE["up_m"], 16), _pick(f, TILE["wide_n"], LANES)

    def body(x_ref, g_ref, *refs):
        w_refs, o_refs, h_ref = refs[:nw], refs[nw:2 * nw], refs[2 * nw]

        @pl.when(pl.program_id(1) == 0)
        def _():
            _, xh = _rms_stats(x_ref[...])
            h_ref[...] = (xh * g_ref[...]).astype(BF16)

        h = h_ref[...]
        for w_ref, o_ref in zip(w_refs, o_refs):
            o_ref[...] = _dot_nt(h, w_ref[...]).astype(o_ref.dtype)

    outs = pl.pallas_call(
        body, name=name, grid=(n // tm, f // tn),
        in_specs=[pl.BlockSpec((tm, d), lambda i, j: (i, 0)), pl.BlockSpec((1, d), lambda i, j: (0, 0))]
        + [pl.BlockSpec((tn, d), lambda i, j: (j, 0))] * nw,
        out_specs=[pl.BlockSpec((tm, tn), lambda i, j: (i, j))] * nw + [pl.BlockSpec((tm, d), lambda i, j: (i, 0))],
        out_shape=[_sds((n, f), out_dtype)] * nw + [_sds((n, d), BF16)],
        compiler_params=_cparams("parallel", "arbitrary"),
    )(x, g, *ws)
    return outs[:nw], outs[nw]


def _ffn_fwd(name, x, g, w1t, w3t, w2, exchange=None, head=None):
    n, d = x.shape
    f = w2.shape[0]
    tm, tn = _pick(n, TILE["ffn_m"], 16), _pick(f, TILE["up_n"], LANES)

    def body(x_ref, g_ref, w1_ref, w3_ref, w2_ref, *refs):
        (gf_ref, t_ref), refs = (refs[:2], refs[2:]) if head else ((None, None), refs)
        o_ref, a_ref, b_ref, h_ref = refs[:4]
        xv = x_ref[...]
        _, xh = _rms_stats(xv)
        h = (xh * g_ref[...]).astype(BF16)
        h_ref[...] = h
        acc = None
        for c0 in range(0, f, tn):
            cols = pl.ds(c0, tn)
            av, bv = _dot_nt(h, w1_ref[cols, :]), _dot_nt(h, w3_ref[cols, :])
            a_ref[:, cols] = av.astype(BF16)
            b_ref[:, cols] = bv.astype(BF16)
            t = _dot((av * _sigmoid(av) * bv).astype(BF16), w2_ref[cols, :])
            acc = t if acc is None else acc + t
        out = xv + FFN_RES * acc
        if head is None:
            o_ref[...] = out
        else:
            loss_ref, dg_ref = refs[4:]

            @pl.when(pl.program_id(0) == 0)
            def _():
                loss_ref[...] = jnp.zeros_like(loss_ref)
                dg_ref[...] = jnp.zeros_like(dg_ref)

            dx, loss, dg = _loss_head_rows(out, gf_ref[...], t_ref[...])
            o_ref[...] = dx
            loss_ref[...] += loss
            dg_ref[...] += dg

    row = pl.BlockSpec((tm, d), lambda i: (i, 0))
    wide = pl.BlockSpec((tm, f), lambda i: (i, 0))
    vec = pl.BlockSpec((1, d), lambda i: (0, 0))
    held = pl.BlockSpec((f, d), lambda i: (0, 0), pipeline_mode=pl.Buffered(1))
    extra_in, extra_out, extra_shape = ([vec, row], [pl.BlockSpec((SUBLANES, LANES), lambda i: (0, 0)), vec],
                                        [_sds((SUBLANES, LANES), F32), _sds((1, d), F32)]) if head else ([], [], [])
    return _call(
        name, body, (n // tm,), [row, vec, held, held, held] + extra_in, [row, wide, wide, row] + extra_out,
        [_sds((n, d), F32), _sds((n, f), BF16), _sds((n, f), BF16), _sds((n, d), BF16)] + extra_shape,
        (x, g, w1t, w3t, w2) + (tuple(head) if head else ()), ("arbitrary",) if head else ("parallel",),
        exchange=exchange)


def _ffn_bwd_hidden(name, dxo, a, b, w2, exchange=None):
    n, d = dxo.shape
    f = a.shape[1]
    tm, tn = _pick(n, TILE["hid_m"], 16), _pick(f, TILE["up_n"], LANES)

    def body(dx_ref, a_ref, b_ref, w_ref, da_ref, db_ref, hid_ref, dxh_ref):
        dxh = (FFN_RES * dx_ref[...]).astype(BF16)
        dxh_ref[...] = dxh
        for c0 in range(0, f, tn):
            cols = pl.ds(c0, tn)
            dhid = _dot_nt(dxh, w_ref[cols, :])
            av, bv = a_ref[:, cols].astype(F32), b_ref[:, cols].astype(F32)
            sig = _sigmoid(av)
            silu = av * sig
            da_ref[:, cols] = (dhid * bv * (sig * (1.0 + av - silu))).astype(BF16)
            db_ref[:, cols] = (dhid * silu).astype(BF16)
            hid_ref[:, cols] = (silu * bv).astype(BF16)

    wide = pl.BlockSpec((tm, f), lambda i: (i, 0))
    row = pl.BlockSpec((tm, d), lambda i: (i, 0))
    return _call(
        name, body, (n // tm,),
        [row, wide, wide, pl.BlockSpec((f, d), lambda i: (0, 0), pipeline_mode=pl.Buffered(1))], [wide, wide, wide, row],
        [_sds((n, f), BF16)] * 3 + [_sds((n, d), BF16)], (dxo, a, b, w2), ("parallel",), exchange=exchange)


def _loss_head_rows(x, g, target):
    r, xh = _rms_stats(x)
    err = xh * g - target
    dy = err * (1.0 / x.shape[-1])
    dxh = dy * g
    dx = r * (dxh - xh * jnp.mean(dxh * xh, axis=-1, keepdims=True))
    return dx, 0.5 * jnp.sum(jnp.mean(err * err, axis=-1, keepdims=True)), jnp.sum(dy * xh, axis=0, keepdims=True)


def _dx_rms_bwd(name, pairs, dxo, x, g, exchange=None):
    n, dm = x.shape
    tm = _pick(n, TILE["ffn_m"], 16)
    npair = len(pairs)

    def body(*refs):
        d_refs, w_refs = refs[:npair], refs[npair:2 * npair]
        dxo_ref, x_ref, g_ref, dx_ref, dg_ref = refs[2 * npair:]

        @pl.when(pl.program_id(0) == 0)
        def _():
            dg_ref[...] = jnp.zeros_like(dg_ref)

        dh = None
        for d_ref, w_ref in zip(d_refs, w_refs):
            t = _dot(d_ref[...].astype(BF16), w_ref[...])
            dh = t if dh is None else dh + t
        dx, dg = _rms_bwd(x_ref[...], g_ref[...], dh)
        dx_ref[...] = dxo_ref[...] + dx
        dg_ref[...] += dg

    row = pl.BlockSpec((tm, dm), lambda i: (i, 0))
    d_specs = [pl.BlockSpec((tm, p[1]), functools.partial(lambda i, cb: (i, cb), cb=p[2])) for p in pairs]
    w_specs = [pl.BlockSpec((p[4], dm), functools.partial(lambda i, rb: (rb, 0), rb=p[5]), pipeline_mode=pl.Buffered(1))
               for p in pairs]
    return _call(
        name, body, (n // tm,), d_specs + w_specs + [row, row, pl.BlockSpec((1, dm), lambda i: (0, 0))],
        [row, pl.BlockSpec((1, dm), lambda i: (0, 0))], [_sds((n, dm), F32), _sds((1, dm), F32)],
        (*[p[0] for p in pairs], *[p[3] for p in pairs], dxo, x, g), ("arbitrary",), exchange=exchange)


def _mm_tn(name, a, b, exchange=None):
    parts = tuple(a) if isinstance(a, (tuple, list)) else (a,)
    n, mb = b.shape
    widths = [p.shape[1] for p in parts]
    tk = _pick(n, TILE["mm_bytes"] // (sum(p.shape[1] * p.dtype.itemsize for p in parts) + mb * b.dtype.itemsize), 16)

    def body(*refs):
        a_refs, b_ref, o_ref = refs[:-2], refs[-2], refs[-1]

        @pl.when(pl.program_id(0) == 0)
        def _():
            o_ref[...] = jnp.zeros_like(o_ref)

        bv = b_ref[...].astype(BF16)
        row0 = 0
        for a_ref, width in zip(a_refs, widths):
            o_ref[pl.ds(row0, width), :] += _dot_tn(a_ref[...].astype(BF16), bv)
            row0 += width

    (out,), got = _call(
        name, body, (n // tk,),
        [pl.BlockSpec((tk, w), lambda k: (k, 0)) for w in widths] + [pl.BlockSpec((tk, mb), lambda k: (k, 0))],
        [pl.BlockSpec((sum(widths), mb), lambda k: (0, 0))], [_sds((sum(widths), mb), F32)], (*parts, b), ("arbitrary",),
        exchange=exchange)
    return out if exchange is None else (out, got)


def _mm_nt(name, a, w, exchange=None):
    n, k = a.shape
    m = w.shape[0]
    tm = _pick(n, TILE["row"], 16)

    def body(a_ref, w_ref, o_ref):
        o_ref[...] = _dot_nt(a_ref[...].astype(BF16), w_ref[...])

    (out,), got = _call(
        name, body, (n // tm,),
        [pl.BlockSpec((tm, k), lambda i: (i, 0)), pl.BlockSpec((m, k), lambda i: (0, 0), pipeline_mode=pl.Buffered(1))],
        [pl.BlockSpec((tm, m), lambda i: (i, 0))], [_sds((n, m), F32)], (a, w), ("parallel",), exchange=exchange)
    return out, got


def _conv_post(c, ln_g, ln_b, out_g):
    mu = jnp.mean(c, axis=-1, keepdims=True)
    xc = c - mu
    rstd = lax.rsqrt(jnp.mean(xc * xc, axis=-1, keepdims=True) + EPS)
    nrm = xc * rstd
    l = nrm * ln_g + ln_b
    sig = _sigmoid(l)
    s = l * sig
    r, sh = _rms_stats(s)
    return sh * out_g, (rstd, nrm, l, sig, r, sh)


def _tap_groups(first):
    groups = []
    for r in range(SUBLANES):
        taps = [(s - r, s - first) for s in range(first, first + CONV_WIDTH) if s % SUBLANES == r]
        if taps:
            groups.append((r, taps))
    return groups


def _conv_taps(a_ref, w_ref, b_ref, first, rows, flip=False):
    acc = None
    for r, taps in _tap_groups(first):
        ext = rows if r == 0 else rows + SUBLANES
        part = None
        for base, k in taps:
            kk = CONV_WIDTH - 1 - k if flip else k
            t = w_ref[kk:kk + 1, :] * a_ref[pl.ds(base, ext), :]
            part = t if part is None else part + t
        if r:
            b_ref[...] = part
            part = b_ref[pl.ds(r, rows), :]
        acc = part if acc is None else acc + part
    return acc


def _conv_post_bwd(cv, dout, ln_g, ln_b, out_g):
    _, (rstd, nrm, l, sig, r, sh) = _conv_post(cv, ln_g, ln_b, out_g)
    dsh = dout * out_g
    ds = r * (dsh - sh * jnp.mean(dsh * sh, axis=-1, keepdims=True))
    dl = ds * (sig * (1.0 + l * (1.0 - sig)))
    dn = dl * ln_g
    dc = rstd * (dn - jnp.mean(dn, axis=-1, keepdims=True) - nrm * jnp.mean(dn * nrm, axis=-1, keepdims=True))
    col_sum = lambda t: jnp.sum(t, axis=0, keepdims=True)
    return dc, col_sum(dout * sh), col_sum(dl * nrm), col_sum(dl)


def _conv_fwd(name, proj3, conv_w, conv_b, ln_g, ln_b, out_g):
    bsz, seq, _ = proj3.shape
    c = conv_w.shape[1]
    tt = _pick(seq, TILE["conv_t"], CONV_HALO)
    hb = tt // CONV_HALO
    first = CONV_HALO - (CONV_WIDTH - 1)

    def body(v_ref, g_ref, vp_ref, gp_ref, w_ref, cb_ref, lg_ref, lb_ref, og_ref, o_ref, cv_ref, a_ref, b_ref):
        keep = (pl.program_id(1) > 0).astype(F32)
        a_ref[pl.ds(0, CONV_HALO), :] = keep * vp_ref[0] * _sigmoid(gp_ref[0])
        a_ref[pl.ds(CONV_HALO, tt), :] = v_ref[0] * _sigmoid(g_ref[0])
        cv = _conv_taps(a_ref, w_ref, b_ref, first, tt) + cb_ref[...]
        cv_ref[0] = cv
        out, _ = _conv_post(cv, lg_ref[...], lb_ref[...], og_ref[...])
        o_ref[0] = out.astype(BF16)

    vec = pl.BlockSpec((1, c), lambda b, i: (0, 0))
    prev = lambda col: pl.BlockSpec((1, CONV_HALO, c), lambda b, i: (b, jnp.maximum(i * hb - 1, 0), col))
    tile = pl.BlockSpec((1, tt, c), lambda b, i: (b, i, 0))
    return pl.pallas_call(
        body, name=name, grid=(bsz, seq // tt),
        in_specs=[tile, pl.BlockSpec((1, tt, c), lambda b, i: (b, i, 1)),
                  prev(0), prev(1), pl.BlockSpec(conv_w.shape, lambda b, i: (0, 0)), vec, vec, vec, vec],
        out_specs=[tile, tile],
        out_shape=[_sds((bsz, seq, c), BF16), _sds((bsz, seq, c), F32)],
        scratch_shapes=[pltpu.VMEM((CONV_HALO + tt, c), F32), pltpu.VMEM((tt + SUBLANES, c), F32)],
        compiler_params=_cparams("parallel", "arbitrary"),
    )(proj3, proj3, proj3, proj3, conv_w, conv_b, ln_g, ln_b, out_g)


def _conv_bwd(name, dmix3, proj3, cv3, conv_w, ln_g, ln_b, out_g):
    bsz, seq, _ = proj3.shape
    c = conv_w.shape[1]
    tt = _pick(seq, TILE["conv_t"], CONV_HALO)
    hb = tt // CONV_HALO
    nt = seq // tt
    last_hb = seq // CONV_HALO - 1
    ext = tt + CONV_HALO
    first = CONV_HALO - (CONV_WIDTH - 1)

    def body(v_ref, g_ref, vp_ref, gp_ref, cv_ref, cvn_ref, d_ref, dn_ref, w_ref, lg_ref, lb_ref, og_ref,
             o_ref, dw_ref, dcb_ref, dlg_ref, dlb_ref, dog_ref, a_ref, dc_ref, b_ref, ds_ref):
        i = pl.program_id(1)

        @pl.when((pl.program_id(0) == 0) & (i == 0))
        def _():
            for r in (dw_ref, dcb_ref, dlg_ref, dlb_ref, dog_ref):
                r[...] = jnp.zeros_like(r)

        keep_prev = (i > 0).astype(F32)
        keep_next = (i < nt - 1).astype(F32)
        sig_g = _sigmoid(g_ref[0])
        a_ref[pl.ds(0, CONV_HALO), :] = keep_prev * vp_ref[0] * _sigmoid(gp_ref[0])
        a_ref[pl.ds(CONV_HALO, tt), :] = v_ref[0] * sig_g

        lg, lb, og = lg_ref[...], lb_ref[...], og_ref[...]
        dc_own, d_og, d_lg, d_lb = _conv_post_bwd(cv_ref[0], d_ref[0], lg, lb, og)
        dc_next, _, _, _ = _conv_post_bwd(cvn_ref[0], keep_next * dn_ref[0], lg, lb, og)
        dog_ref[...] += d_og
        dlg_ref[...] += d_lg
        dlb_ref[...] += d_lb
        dcb_ref[...] += jnp.sum(dc_own, axis=0, keepdims=True)
        dc_ref[pl.ds(0, tt), :] = dc_own
        dc_ref[pl.ds(tt, CONV_HALO), :] = dc_next

        da = _conv_taps(dc_ref, w_ref, b_ref, 0, tt, flip=True)

        for r, taps in _tap_groups(first):
            if r:
                ds_ref[pl.ds(0, SUBLANES), :] = jnp.zeros((SUBLANES, c), F32)
                ds_ref[pl.ds(tt, SUBLANES), :] = jnp.zeros((SUBLANES, c), F32)
                ds_ref[pl.ds(r, tt), :] = dc_own
            for base, k in taps:
                prod = (ds_ref[...] * a_ref[pl.ds(base, tt + SUBLANES), :]) if r else (dc_own * a_ref[pl.ds(base, tt), :])
                dw_ref[k:k + 1, :] += jnp.sum(prod, axis=0, keepdims=True)
        val = v_ref[0]
        o_ref[0] = jnp.concatenate([da * sig_g, da * val * sig_g * (1.0 - sig_g)], axis=-1).astype(BF16)

    vec = pl.BlockSpec((1, c), lambda b, i: (0, 0))
    cur = lambda col: pl.BlockSpec((1, tt, c), lambda b, i: (b, i, col))
    prev = lambda col: pl.BlockSpec((1, CONV_HALO, c), lambda b, i: (b, jnp.maximum(i * hb - 1, 0), col))
    nxt = lambda col: pl.BlockSpec((1, CONV_HALO, c), lambda b, i: (b, jnp.minimum((i + 1) * hb, last_hb), col))
    wspec = pl.BlockSpec(conv_w.shape, lambda b, i: (0, 0))
    return pl.pallas_call(
        body, name=name, grid=(bsz, nt),
        in_specs=[cur(0), cur(1), prev(0), prev(1), cur(0), nxt(0), cur(0), nxt(0), wspec, vec, vec, vec],
        out_specs=[pl.BlockSpec((1, tt, 2 * c), lambda b, i: (b, i, 0)), wspec, vec, vec, vec, vec],
        out_shape=[_sds((bsz, seq, 2 * c), BF16), _sds(conv_w.shape, F32)] + [_sds((1, c), F32)] * 4,
        scratch_shapes=[pltpu.VMEM((CONV_HALO + tt, c), F32), pltpu.VMEM((ext, c), F32),
                        pltpu.VMEM((tt + SUBLANES, c), F32), pltpu.VMEM((tt + SUBLANES, c), F32)],
        compiler_params=_cparams("arbitrary", "arbitrary"),
    )(proj3, proj3, proj3, proj3, cv3, cv3, dmix3, dmix3, conv_w, ln_g, ln_b, out_g)


def _ssm_discretise(a_re, a_im, log_dt):
    dt = jnp.exp(log_dt)
    zr, zi = a_re * dt, a_im * dt
    mag = jnp.exp(zr)
    ar, ai = mag * jnp.cos(zi), mag * jnp.sin(zi)
    den = a_re * a_re + a_im * a_im
    nr = ar - 1.0
    return ar, ai, (nr * a_re + ai * a_im) / den, (ai * a_re - nr * a_im) / den


def _ssm_system(a_re, a_im, log_dt, a_re_x, a_im_x, log_dt_x, bt_re, bt_im):
    ar, ai, _, _ = _ssm_discretise(a_re, a_im, log_dt)
    _, _, cr, ci = _ssm_discretise(a_re_x, a_im_x, log_dt_x)
    return ar, ai, cr * bt_re - ci * bt_im, cr * bt_im + ci * bt_re


def _ssm_prep(name, prim):
    g, p = prim[0].shape

    def body(*refs):
        pwr_ref, pwi_ref, bbr_ref, bbi_ref = refs[8:]
        ar, ai, bbr, bbi = _ssm_system(*[r[...] for r in refs[:8]])
        bbr_ref[...] = bbr
        bbi_ref[...] = bbi
        pr, pi = ar, ai
        for k in range(SUBLANES):
            pwr_ref[k] = pr
            pwi_ref[k] = pi
            pr, pi = pr * ar - pi * ai, pr * ai + pi * ar

    return pl.pallas_call(
        body, name=name,
        out_shape=[_sds((SUBLANES, g, p), F32)] * 2 + [_sds(prim[6].shape, F32)] * 2,
        compiler_params=pltpu.CompilerParams(vmem_limit_bytes=VMEM_LIMIT),
    )(*prim)


def _ssm_param_grads(name, prim, dab_r, dab_i, dbb_r, dbb_i):
    g, p = prim[0].shape
    h = prim[6].shape[0] // g

    def body(*refs):
        dar_ref, dai_ref, dbr_ref, dbi_ref = refs[8:12]
        o_ar, o_ai, o_dt, o_br, o_bi = refs[12:]
        _, vjp = jax.vjp(_ssm_system, *[r[...] for r in refs[:8]])
        ct = (jnp.sum(dar_ref[...], axis=0), jnp.sum(dai_ref[...], axis=0), dbr_ref[...], dbi_ref[...])
        d_ar, d_ai, d_dt, d_arx, d_aix, d_dtx, d_br, d_bi = vjp(ct)
        per_group = lambda t: jnp.sum(t.reshape(g, h, p), axis=1)
        o_ar[...] = d_ar + per_group(d_arx)
        o_ai[...] = d_ai + per_group(d_aix)
        o_dt[...] = d_dt + jnp.sum(per_group(d_dtx), axis=1, keepdims=True)
        o_br[...] = d_br
        o_bi[...] = d_bi

    return pl.pallas_call(
        body, name=name,
        out_shape=[_sds(prim[k].shape, F32) for k in (0, 1, 2, 6, 7)],
        compiler_params=pltpu.CompilerParams(vmem_limit_bytes=VMEM_LIMIT),
    )(*prim, dab_r, dab_i, dbb_r, dbb_i)


def _cfma(xr, xi, cr, ci, sr, si):
    return xr + (cr * sr - ci * si), xi + (cr * si + ci * sr)


def _scan_tables(pw_r, pw_i, reverse):
    gp = pw_r.shape[1] * pw_r.shape[2]
    pr, pi = pw_r.reshape(SUBLANES, gp), pw_i.reshape(SUBLANES, gp)
    if reverse:
        pi = -pi
    row = jnp.arange(SUBLANES)[:, None]
    tabs = []
    for d in (1, 2, 4):
        keep = (row < SUBLANES - d) if reverse else (row >= d)
        tabs += [jnp.where(keep, pr[d - 1][None, :], 0.0), jnp.where(keep, pi[d - 1][None, :], 0.0)]
    tabs += [pr[::-1], pi[::-1]] if reverse else [pr, pi]
    return jnp.concatenate(tabs, axis=0)


MXU_DEPTH = 256


def _bands(c, gp):
    bw = min(c, MXU_DEPTH)
    return c // bw, bw, gp * bw // c


def _band_expand(rows16, w_ref, put, c, gp):
    nb, bw, sw = _bands(c, gp)
    for s in range(nb):
        band = rows16[:, s * bw:(s + 1) * bw]
        for half in (0, gp):
            cols = pl.ds(half + s * sw, sw)
            put(cols, _dot(band, w_ref[pl.ds(s * bw, bw), cols]))


def _band_contract(get16, w_ref, c, gp):
    nb, bw, sw = _bands(c, gp)
    out = []
    for s in range(nb):
        acc = None
        for half in (0, gp):
            cols = pl.ds(half + s * sw, sw)
            t = _dot_nt(get16(cols), w_ref[pl.ds(s * bw, bw), cols])
            acc = t if acc is None else acc + t
        out.append(acc)
    return out[0] if nb == 1 else jnp.concatenate(out, axis=1)


def _band_wgrad(name, a, a_block, c, b):
    n = a.shape[0]
    gp = b.shape[1] // 2
    nb, bw, sw = _bands(c, gp)
    tk = _pick(n, TILE["mm_bytes"] // (c * a.dtype.itemsize + 2 * gp * b.dtype.itemsize), 16)

    def body(a_ref, b_ref, o_ref):
        @pl.when(pl.program_id(0) == 0)
        def _():
            o_ref[...] = jnp.zeros_like(o_ref)

        for s in range(nb):
            band = a_ref[:, s * bw:(s + 1) * bw].astype(BF16)
            for h, half in enumerate((0, gp)):
                o_ref[pl.ds(s * bw, bw), pl.ds(h * sw, sw)] += _dot_tn(
                    band, b_ref[:, pl.ds(half + s * sw, sw)].astype(BF16))

    return pl.pallas_call(
        body, name=name, grid=(n // tk,),
        in_specs=[pl.BlockSpec((tk, c), lambda k: (k, a_block)), pl.BlockSpec((tk, 2 * gp), lambda k: (k, 0))],
        out_specs=pl.BlockSpec((c, 2 * sw), lambda k: (0, 0)),
        out_shape=_sds((c, 2 * sw), F32),
        compiler_params=_cparams("arbitrary"),
    )(a, b)


def _band_diag_take(comp, half, c, gp):
    nb, bw, sw = _bands(c, gp)
    return jnp.concatenate([_block_diag_take(comp[s * bw:(s + 1) * bw, half * sw:(half + 1) * sw], bw // SSM_GROUP)
                            for s in range(nb)], axis=0)


def _scan_fwd(name, tab, proj3, u_block, bbd, cdt):
    bsz, seq, _ = proj3.shape
    c, w = bbd.shape
    gp = w // 2
    tt = _pick(seq, TILE["scan_fwd_t"], 16)
    nblk = tt // SUBLANES
    cw = _pick(gp, TILE["scan_w"], LANES)

    def body(tab_ref, u_ref, bbd_ref, cdt_ref, xs_ref, xs16_ref, y_ref, carry_ref, bu_ref):
        @pl.when(pl.program_id(1) == 0)
        def _():
            carry_ref[...] = jnp.zeros_like(carry_ref)

        def put_bu(cols, val):
            bu_ref[0, :, cols] = val

        _band_expand(u_ref[0].astype(BF16), bbd_ref, put_bu, c, gp)

        for ch in range(gp // cw):
            re, im = pl.ds(ch * cw, cw), pl.ds(gp + ch * cw, cw)

            def blk(r, carry, re=re, im=im):
                tabs = [tab_ref[pl.ds(SUBLANES * k, SUBLANES), re] for k in range(8)]
                rows = pl.ds(pl.multiple_of(r * SUBLANES, SUBLANES), SUBLANES)
                xr, xi = bu_ref[0, rows, re], bu_ref[0, rows, im]
                for j, d in enumerate((1, 2, 4)):
                    xr, xi = _cfma(xr, xi, tabs[2 * j], tabs[2 * j + 1], pltpu.roll(xr, d, 0), pltpu.roll(xi, d, 0))
                xr, xi = _cfma(xr, xi, tabs[6], tabs[7], carry[0], carry[1])
                xs_ref[0, rows, re] = xr
                xs_ref[0, rows, im] = xi
                last = SUBLANES - 1
                return (jnp.broadcast_to(xr[last:, :], xr.shape), jnp.broadcast_to(xi[last:, :], xi.shape))

            cr, ci = lax.fori_loop(0, nblk, blk, (carry_ref[:, re], carry_ref[:, im]))
            carry_ref[:, re] = cr
            carry_ref[:, im] = ci

        xs16_ref[0] = xs_ref[0].astype(BF16)
        y_ref[0] = _band_contract(lambda cols: xs16_ref[0, :, cols], cdt_ref, c, gp)

    whole = lambda arr: pl.BlockSpec(arr.shape, lambda b, t: (0, 0), pipeline_mode=pl.Buffered(1))
    wide = pl.BlockSpec((1, tt, w), lambda b, t: (b, t, 0))
    return pl.pallas_call(
        body, name=name, grid=(bsz, seq // tt),
        in_specs=[whole(tab), pl.BlockSpec((1, tt, c), lambda b, t: (b, t, u_block)), whole(bbd), whole(cdt)],
        out_specs=[wide, wide, pl.BlockSpec((1, tt, c), lambda b, t: (b, t, 0))],
        out_shape=[_sds((bsz, seq, w), F32), _sds((bsz, seq, w), BF16), _sds((bsz, seq, c), F32)],
        scratch_shapes=[pltpu.VMEM((SUBLANES, w), F32), pltpu.VMEM((1, tt, w), F32)],
        compiler_params=_cparams("arbitrary", "arbitrary"),
    )(tab, proj3, bbd, cdt)


def _scan_bwd(name, tab, dy3, xs3, du_skip3, bbd, cdt, exchange=None):
    bsz, seq, w = xs3.shape
    c = bbd.shape[0]
    gp = w // 2
    tt = _pick(seq, TILE["scan_t"], 16)
    nblk = tt // SUBLANES
    cw = _pick(gp, TILE["scan_w"], LANES)
    nt = seq // tt

    def body(tab_ref, dy_ref, xs_ref, halo_ref, skip_ref, bbd_ref, cdt_ref, lam16_ref, du_ref, dar_ref, dai_ref,
             carry_ref, g_ref, lam_ref):
        t = pl.program_id(1)

        @pl.when(t == 0)
        def _():
            carry_ref[...] = jnp.zeros_like(carry_ref)

        @pl.when((pl.program_id(0) == 0) & (t == 0))
        def _():
            dar_ref[...] = jnp.zeros_like(dar_ref)
            dai_ref[...] = jnp.zeros_like(dai_ref)

        def put_g(cols, val):
            g_ref[0, :, cols] = val

        _band_expand(dy_ref[0], cdt_ref, put_g, c, gp)

        has_prev = (t < nt - 1).astype(F32)
        row0 = lax.broadcasted_iota(jnp.int32, (SUBLANES, cw), 0) == 0
        last = SUBLANES - 1

        for ch in range(gp // cw):
            re, im = pl.ds(ch * cw, cw), pl.ds(gp + ch * cw, cw)

            def step(rows, xm1r, xm1i, state, re=re, im=im):
                tabs = [tab_ref[pl.ds(SUBLANES * k, SUBLANES), re] for k in range(8)]
                cr, ci, accr, acci = state
                lr, li = g_ref[0, rows, re], g_ref[0, rows, im]
                for j, d in enumerate((1, 2, 4)):
                    lr, li = _cfma(lr, li, tabs[2 * j], tabs[2 * j + 1],
                                   pltpu.roll(lr, SUBLANES - d, 0), pltpu.roll(li, SUBLANES - d, 0))
                lr, li = _cfma(lr, li, tabs[6], tabs[7], cr, ci)
                lam_ref[0, rows, re] = lr
                lam_ref[0, rows, im] = li
                xr, xi = xs_ref[0, rows, re], xs_ref[0, rows, im]
                xpr = jnp.where(row0, jnp.broadcast_to(xm1r[last:, :], xr.shape), pltpu.roll(xr, 1, 0))
                xpi = jnp.where(row0, jnp.broadcast_to(xm1i[last:, :], xi.shape), pltpu.roll(xi, 1, 0))
                accr = accr + (lr * xpr + li * xpi)
                acci = acci + (li * xpr - lr * xpi)
                return (jnp.broadcast_to(lr[:1, :], lr.shape), jnp.broadcast_to(li[:1, :], li.shape), accr, acci)

            def blk(k, state, re=re, im=im, step=step):
                r = nblk - 1 - k
                rows = pl.ds(pl.multiple_of(r * SUBLANES, SUBLANES), SUBLANES)
                prev = pl.ds(pl.multiple_of((r - 1) * SUBLANES, SUBLANES), SUBLANES)
                return step(rows, xs_ref[0, prev, re], xs_ref[0, prev, im], state)

            zero = jnp.zeros((SUBLANES, cw), F32)
            state = lax.fori_loop(0, nblk - 1, blk, (carry_ref[:, re], carry_ref[:, im], zero, zero))
            cr, ci, accr, acci = step(pl.ds(0, SUBLANES), has_prev * halo_ref[0, :, re], has_prev * halo_ref[0, :, im], state)
            carry_ref[:, re] = cr
            carry_ref[:, im] = ci
            dar_ref[:, re] += accr
            dai_ref[:, re] += acci

        lam16_ref[0] = lam_ref[0].astype(BF16)
        du = _band_contract(lambda cols: lam16_ref[0, :, cols], bbd_ref, c, gp)
        du_ref[0] = (du + skip_ref[0]).astype(BF16)

    tile = pl.BlockSpec((1, tt, w), lambda b, t: (b, nt - 1 - t, 0))
    thin = pl.BlockSpec((1, tt, c), lambda b, t: (b, nt - 1 - t, 0))
    halo = pl.BlockSpec((1, SUBLANES, w), lambda b, t: (b, jnp.maximum((nt - 1 - t) * nblk - 1, 0), 0))
    acc = pl.BlockSpec((SUBLANES, gp), lambda b, t: (0, 0))
    whole = lambda arr: pl.BlockSpec(arr.shape, lambda b, t: (0, 0), pipeline_mode=pl.Buffered(1))
    return _call(
        name, body, (bsz, nt), [whole(tab), thin, tile, halo, thin, whole(bbd), whole(cdt)], [tile, thin, acc, acc],
        [_sds(xs3.shape, BF16), _sds((bsz, seq, c), BF16), _sds((SUBLANES, gp), F32), _sds((SUBLANES, gp), F32)],
        (tab, dy3, xs3, xs3, du_skip3, bbd, cdt), ("arbitrary", "arbitrary"),
        scratch=[pltpu.VMEM((SUBLANES, w), F32), pltpu.VMEM((1, tt, w), F32), pltpu.VMEM((1, tt, w), F32)],
        exchange=exchange)


def _gelu_parts(y):
    inner = _GELU_K * (y + _GELU_C * y * y * y)
    t = jnp.tanh(inner)
    return 0.5 * y * (1.0 + t), t


def _ssm_out_fwd(name, cx, proj, u_block, d_skip, glu_w, glu_b, out_g, x, conv_out, w_out):
    n, c = cx.shape
    d = x.shape[1]
    tm = _pick(n, TILE["row"], 16)

    def body(cx_ref, u_ref, d_ref, gw_ref, gb_ref, og_ref, x_ref, a_ref, wo_ref, y_ref, o_ref, xo_ref):
        y = cx_ref[...] + d_ref[...] * u_ref[...]
        y_ref[...] = y
        gy, _ = _gelu_parts(y)
        z = _dot(gy.astype(BF16), gw_ref[...]) + gb_ref[...]
        _, sh = _rms_stats(gy * _sigmoid(z))
        out = (sh * og_ref[...]).astype(BF16)
        o_ref[...] = out
        xo_ref[...] = x_ref[...] + _dot(a_ref[...], wo_ref[pl.ds(0, c), :]) + _dot(out, wo_ref[pl.ds(c, c), :])

    vec = pl.BlockSpec((1, c), lambda i: (0, 0))
    row = pl.BlockSpec((tm, c), lambda i: (i, 0))
    wide = pl.BlockSpec((tm, d), lambda i: (i, 0))
    held = lambda arr: pl.BlockSpec(arr.shape, lambda i: (0, 0), pipeline_mode=pl.Buffered(1))
    return pl.pallas_call(
        body, name=name, grid=(n // tm,),
        in_specs=[row, pl.BlockSpec((tm, c), lambda i: (i, u_block)), vec, held(glu_w), vec, vec, wide, row, held(w_out)],
        out_specs=[row, row, wide],
        out_shape=[_sds((n, c), F32), _sds((n, c), BF16), _sds((n, d), F32)],
        compiler_params=_cparams("parallel"),
    )(cx, proj, d_skip, glu_w, glu_b, out_g, x, conv_out, w_out)


def _ssm_out_bwd(name, dmix, d_block, y, proj, u_block, d_skip, glu_w, glu_b, out_g):
    n, c = y.shape
    tm = _pick(n, TILE["row"], 16)

    def body(d_ref, y_ref, u_ref, dk_ref, gw_ref, gb_ref, og_ref, dy_ref, du_ref, dgw_ref, dgb_ref, dog_ref, dd_ref):
        @pl.when(pl.program_id(0) == 0)
        def _():
            for r in (dgw_ref, dgb_ref, dog_ref, dd_ref):
                r[...] = jnp.zeros_like(r)

        yv = y_ref[...]
        gy, th = _gelu_parts(yv)
        gy16 = gy.astype(BF16)
        sz = _sigmoid(_dot(gy16, gw_ref[...]) + gb_ref[...])
        r, sh = _rms_stats(gy * sz)
        dout = d_ref[...]
        dog_ref[...] += jnp.sum(dout * sh, axis=0, keepdims=True)
        dsh = dout * og_ref[...]
        ds = r * (dsh - sh * jnp.mean(dsh * sh, axis=-1, keepdims=True))
        dz = ds * gy * sz * (1.0 - sz)
        dz16 = dz.astype(BF16)
        dgb_ref[...] += jnp.sum(dz, axis=0, keepdims=True)
        dgw_ref[...] += _dot_tn(gy16, dz16)
        dgy = ds * sz + _dot_nt(dz16, gw_ref[...])
        dgelu = 0.5 * (1.0 + th) + 0.5 * yv * (1.0 - th * th) * (_GELU_K * (1.0 + 3.0 * _GELU_C * yv * yv))
        dy = dgy * dgelu
        dy_ref[...] = dy.astype(BF16)
        du_ref[...] = dy * dk_ref[...]
        dd_ref[...] += jnp.sum(dy * u_ref[...], axis=0, keepdims=True)

    vec = pl.BlockSpec((1, c), lambda i: (0, 0))
    row = pl.BlockSpec((tm, c), lambda i: (i, 0))
    mat = pl.BlockSpec(glu_w.shape, lambda i: (0, 0))
    return pl.pallas_call(
        body, name=name, grid=(n // tm,),
        in_specs=[pl.BlockSpec((tm, c), lambda i: (i, d_block)), row, pl.BlockSpec((tm, c), lambda i: (i, u_block)),
                  vec, mat, vec, vec],
        out_specs=[row, row, mat, vec, vec, vec],
        out_shape=[_sds((n, c), BF16), _sds((n, c), F32), _sds(glu_w.shape, F32)] + [_sds((1, c), F32)] * 3,
        compiler_params=_cparams("arbitrary"),
    )(dmix, y, proj, d_skip, glu_w, glu_b, out_g)


def _mesh_pos():
    return tuple(lax.axis_index(a) for a in MESH_AXES)


def _other_chips(x, y):
    return [(1 - x, y), (x, 1 - y), (1 - x, 1 - y)]


def _remote(src, dst, send_sem, recv_sem, dev):
    return pltpu.make_async_remote_copy(src_ref=src, dst_ref=dst, send_sem=send_sem, recv_sem=recv_sem,
                                        device_id=dev, device_id_type=pl.DeviceIdType.MESH)


def _hbm_call(name, body, operands, out_shapes, scratch):
    hbm = pl.BlockSpec(memory_space=pltpu.HBM)
    return pl.pallas_call(body, name=name, in_specs=[hbm] * len(operands), out_specs=[hbm] * len(out_shapes),
                          out_shape=out_shapes, scratch_shapes=scratch)(*operands)


_Exchange = collections.namedtuple("_Exchange", "operands out_shapes scratch start finish")


def _merge_plans(p, q):
    cut = len(p.operands), len(p.out_shapes), len(p.scratch)

    def both(which):
        def run(x_refs, o_refs, sems):
            getattr(p, which)(x_refs[:cut[0]], o_refs[:cut[1]], sems[:cut[2]])
            getattr(q, which)(x_refs[cut[0]:], o_refs[cut[1]:], sems[cut[2]:])
        return run

    return _Exchange(p.operands + q.operands, p.out_shapes + q.out_shapes, p.scratch + q.scratch,
                     both("start"), both("finish"))


def _run_exchange(name, plan):
    nin, nout = len(plan.operands), len(plan.out_shapes)

    def body(*refs):
        parts = refs[:nin], refs[nin:nin + nout], refs[nin + nout:]
        plan.start(*parts)
        plan.finish(*parts)

    return _hbm_call(name, body, plan.operands, plan.out_shapes, plan.scratch)


def _gather_plan(blocks):
    nop = len(blocks)

    def copies(x_refs, o_refs, sems):
        send_sems, recv_sems, local_sems = sems
        x, y, c = _mesh_pos()
        me, sibling = (x, y, c), (x, y, 1 - c)
        chips = _other_chips(x, y)

        def copy(i, k, block_of, to, src=None):
            dst = o_refs[i].at[4 * block_of[0] + 2 * block_of[1] + block_of[2]]
            return _remote(dst if src is None else src, dst, send_sems.at[i, k], recv_sems.at[i, k], to)

        own = [pltpu.make_async_copy(x_refs[i], o_refs[i].at[4 * x + 2 * y + c], local_sems.at[i]) for i in range(nop)]
        first = []
        for i in range(nop):
            first.append(copy(i, 0, me, sibling, src=x_refs[i]))
            first += [copy(i, 1 + j, me, (*chip, c), src=x_refs[i]) for j, chip in enumerate(chips)]
        return copy, own, first, me, sibling, chips, c

    def start(x_refs, o_refs, sems):
        _, own, first, *_ = copies(x_refs, o_refs, sems)
        for cp in own + first:
            cp.start()

    def finish(x_refs, o_refs, sems):
        copy, own, first, me, sibling, chips, c = copies(x_refs, o_refs, sems)
        passed = []
        for i in range(nop):
            for j, chip in enumerate(chips):
                copy(i, 1 + j, (*chip, c), me).wait_recv()
                passed.append(copy(i, 4 + j, (*chip, c), sibling))
                passed[-1].start()
        for i in range(nop):
            copy(i, 0, sibling, me).wait_recv()
            for j, chip in enumerate(chips):
                copy(i, 4 + j, (*chip, 1 - c), me).wait_recv()
        for cp in first + passed:
            cp.wait_send()
        for cp in own:
            cp.wait()

    return _Exchange(list(blocks), [_sds((N_DEV,) + b.shape, b.dtype) for b in blocks],
                     [pltpu.SemaphoreType.DMA((nop, N_DEV - 1)), pltpu.SemaphoreType.DMA((nop, N_DEV - 1)),
                      pltpu.SemaphoreType.DMA((nop,))], start, finish)


def _core_exchange_plan(grads):
    nop = len(grads)

    def copies(x_refs, o_refs, sems):
        send_sems, recv_sems = sems
        x, y, c = _mesh_pos()
        return [_remote(x_refs[i].at[2 * q + (1 - c)], o_refs[i].at[q], send_sems.at[i, q], recv_sems.at[i, q],
                        (x, y, 1 - c)) for i in range(nop) for q in range(N_DEV // 2)]

    def start(x_refs, o_refs, sems):
        for cp in copies(x_refs, o_refs, sems):
            cp.start()

    def finish(x_refs, o_refs, sems):
        for cp in copies(x_refs, o_refs, sems):
            cp.wait()

    return _Exchange(list(grads), [_sds((N_DEV // 2,) + g.shape[1:], g.dtype) for g in grads],
                     [pltpu.SemaphoreType.DMA((nop, N_DEV // 2)), pltpu.SemaphoreType.DMA((nop, N_DEV // 2))],
                     start, finish)


def _pair_sum(name, grad, other):
    nchip, _, r, c = grad.shape
    tr = _pick(r, max(SUBLANES, TILE["sum_bytes"] // (4 * c)), SUBLANES)
    core = lax.axis_index("c").astype(jnp.int32).reshape(1)

    def body(core_ref, g_ref, o_ref, s_ref):
        s_ref[0] = (g_ref[0, 0] + o_ref[0]).astype(s_ref.dtype)

    tile = pl.BlockSpec((1, tr, c), lambda q, t, core_ref: (q, t, 0))
    return pl.pallas_call(
        body, name=name,
        grid_spec=pltpu.PrefetchScalarGridSpec(
            num_scalar_prefetch=1, grid=(nchip, r // tr),
            in_specs=[pl.BlockSpec((1, 1, tr, c), lambda q, t, core_ref: (q, core_ref[0], t, 0)), tile],
            out_specs=tile),
        out_shape=_sds((nchip, r, c), BF16),
        compiler_params=_cparams("parallel", "parallel"),
    )(core, grad, other)


def _chip_exchange_plan(sums):
    nop = len(sums)

    def copies(x_refs, o_refs, sems, arriving):
        send_sems, recv_sems, local_sems = sems
        x, y, c = _mesh_pos()
        mine = 2 * x + y
        out = []
        for i in range(nop):
            for j, (px, py) in enumerate(_other_chips(x, y)):
                theirs = 2 * px + py
                src, dst = (mine, theirs) if arriving else (theirs, mine)
                out.append(_remote(x_refs[i].at[src], o_refs[i].at[dst], send_sems.at[i, j], recv_sems.at[i, j],
                                   (px, py, c)))
        if not arriving:
            out += [pltpu.make_async_copy(x_refs[i].at[mine], o_refs[i].at[mine], local_sems.at[i]) for i in range(nop)]
        return out

    def start(x_refs, o_refs, sems):
        for cp in copies(x_refs, o_refs, sems, False):
            cp.start()

    def finish(x_refs, o_refs, sems):
        for cp in copies(x_refs, o_refs, sems, True):
            cp.wait_recv()
        mine = copies(x_refs, o_refs, sems, False)
        for cp in mine[:3 * nop]:
            cp.wait_send()
        for cp in mine[3 * nop:]:
            cp.wait()

    return _Exchange(list(sums), [_sds(s.shape, s.dtype) for s in sums],
                     [pltpu.SemaphoreType.DMA((nop, 3)), pltpu.SemaphoreType.DMA((nop, 3)), pltpu.SemaphoreType.DMA((nop,))],
                     start, finish)


def _part_rows(npart, r, c):
    return _pick(r, max(SUBLANES, TILE["sum_bytes"] // (4 * npart * c)), SUBLANES)


def _sum_slots(p_ref):
    g = p_ref[0].astype(F32)
    for k in range(1, p_ref.shape[0]):
        g = g + p_ref[k].astype(F32)
    return g


def _adamw_step(g, w, m, v):
    c1 = 1.0 - ADAM_B1 ** ADAM_STEP
    c2 = 1.0 - ADAM_B2 ** ADAM_STEP
    nm = ADAM_B1 * m + (1.0 - ADAM_B1) * g
    nv = ADAM_B2 * v + (1.0 - ADAM_B2) * (g * g)
    return -ADAM_LR * ((nm / c1) / (jnp.sqrt(nv / c2) + ADAM_EPS) + ADAM_WD * w), nm, nv


def _adamw_small(name, parts, ws, ms, vs):
    nparam, nall = len(ws), len(parts)

    def body(*refs):
        p_refs = refs[:nall]
        w_refs, m_refs, v_refs = (refs[nall + k * nparam:nall + (k + 1) * nparam] for k in range(3))
        outs = refs[nall + 3 * nparam:]
        for p in range(nall):
            g = _sum_slots(p_refs[p])
            if p < nparam:
                delta, nm, nv = _adamw_step(g, w_refs[p][...], m_refs[p][...], v_refs[p][...])
                for o_ref, val in zip(outs[4 * p:4 * p + 4], (g, delta, nm, nv)):
                    o_ref[...] = val
            else:
                outs[4 * nparam + p - nparam][...] = g

    shapes = [_sds(w.shape, F32) for w in ws for _ in range(4)] + [_sds(p.shape[1:], F32) for p in parts[nparam:]]
    res = pl.pallas_call(body, name=name, out_shape=shapes,
                         compiler_params=pltpu.CompilerParams(vmem_limit_bytes=VMEM_LIMIT))(*parts, *ws, *ms, *vs)
    return [res[4 * p:4 * p + 4] for p in range(nparam)] + [[r] for r in res[4 * nparam:]]


def _adamw(name, parts, w, m, v):
    npart, r, c = parts.shape
    lead = len(w.shape) - 2
    tr = _part_rows(npart, r, c)
    at = (0,) * lead + (slice(None), slice(None))

    def body(p_ref, w_ref, m_ref, v_ref, g_ref, d_ref, nm_ref, nv_ref):
        g = _sum_slots(p_ref)
        delta, nm, nv = _adamw_step(g, w_ref[at], m_ref[at], v_ref[at])
        g_ref[at] = g
        nm_ref[at] = nm
        nv_ref[at] = nv
        d_ref[at] = delta

    row = pl.BlockSpec((1,) * lead + (tr, c), lambda i: (0,) * lead + (i, 0))
    return pl.pallas_call(
        body, name=name, grid=(r // tr,),
        in_specs=[pl.BlockSpec((npart, tr, c), lambda i: (0, i, 0)), row, row, row],
        out_specs=[row] * 4,
        out_shape=[_sds(w.shape, F32)] * 4,
        compiler_params=_cparams("parallel"),
    )(parts, w, m, v)


def _block_diag(rows_gh, groups):
    gh, p = rows_gh.shape
    own = (jnp.arange(gh)[:, None] // (gh // groups) == jnp.arange(groups)[None, :]).astype(rows_gh.dtype)
    return (own[:, :, None] * rows_gh[:, None, :]).reshape(gh, groups * p)


def _block_diag_take(dense, groups):
    gh = dense.shape[0]
    p = dense.shape[1] // groups
    own = (jnp.arange(gh)[:, None] // (gh // groups) == jnp.arange(groups)[None, :]).astype(dense.dtype)
    return jnp.sum(dense.reshape(gh, groups, p) * own[:, :, None], axis=1)


FFN1 = ("ffn1_w1", "ffn1_w3", "ffn1_w2")
MIXER = ("w_in", "ssm_glu_w", "w_out")
FFN2 = ("ffn2_w1", "ffn2_w3", "ffn2_w2")
BIG = FFN1 + MIXER + FFN2
COL_SHARDED = ("ffn1_w1", "ffn1_w3", "w_in", "ffn2_w1", "ffn2_w3", "conv_w")
SMALL = ("norm_ffn1", "norm_mix", "conv_b", "conv_ln_g", "conv_ln_b", "conv_out_g", "ssm_A_re", "ssm_A_im",
         "ssm_log_dt", "ssm_B_re", "ssm_B_im", "ssm_C_re", "ssm_C_im", "ssm_D", "ssm_glu_b", "ssm_out_g",
         "norm_ffn2", "norm_final")
WEIGHTS = ("norm_ffn1", "ffn1_w1", "ffn1_w3", "ffn1_w2", "norm_mix", "w_in", "conv_w", "conv_b", "conv_ln_g",
           "conv_ln_b", "conv_out_g", "ssm_A_re", "ssm_A_im", "ssm_log_dt", "ssm_B_re", "ssm_B_im", "ssm_C_re",
           "ssm_C_im", "ssm_D", "ssm_glu_w", "ssm_glu_b", "ssm_out_g", "w_out", "norm_ffn2", "ffn2_w1", "ffn2_w3",
           "ffn2_w2", "norm_final")


def _ffn_backward(tag, dxo, x, g, w1, w3, w2, saved, exchange=None, reduce_names=None):
    a, b, h = saved
    (da, db, hid, dxh), got = _ffn_bwd_hidden(tag + "_bwd_hidden", dxo, a, b, w2, exchange=exchange)
    dw1, dw3 = _mm_tn(tag + "_dw1", da, h), _mm_tn(tag + "_dw3", db, h)
    across = None
    if reduce_names:
        send = [_row_blocks(dw1), _row_blocks(dw3)]
        dw2, from_core = _mm_tn(tag + "_dw2", hid, dxh, exchange=_core_exchange_plan(send))
        send.append(_row_blocks(dw2))
        from_core += _run_exchange("exchange_core_" + tag, _core_exchange_plan(send[2:]))
        across = _across_chips(reduce_names, send, from_core)
    else:
        dw2 = _mm_tn(tag + "_dw2", hid, dxh)
    f = a.shape[1]
    (dx, dg), reduced = _dx_rms_bwd(tag + "_bwd_dx", [(da, f, 0, w1, f, 0), (db, f, 0, w3, f, 0)], dxo, x, g,
                                    exchange=across)
    return (dx, dg, [dw1, dw3, dw2]), got, reduced


def _row_blocks(grad):
    return grad.reshape((N_DEV, -1) + grad.shape[1:])


def _across_chips(names, send, from_core):
    return _chip_exchange_plan([_pair_sum("pair_sum_" + k, s.reshape((N_DEV // 2, 2) + s.shape[1:]), o)
                                for k, s, o in zip(names, send, from_core)])


def _reduce_in_chip(names, grads):
    send = [_row_blocks(g) for g in grads]
    return _core_exchange_plan(send), functools.partial(_across_chips, names, send)


def kernel(x, norm_ffn1, ffn1_w1, ffn1_w3, ffn1_w2, norm_mix, w_in, conv_w, conv_b, conv_ln_g, conv_ln_b, conv_out_g, ssm_A_re, ssm_A_im, ssm_log_dt, ssm_B_re, ssm_B_im, ssm_C_re, ssm_C_im, ssm_D, ssm_glu_w, ssm_glu_b, ssm_out_g, w_out, norm_ffn2, ffn2_w1, ffn2_w3, ffn2_w2, norm_final, loss_target, m_norm_ffn1, m_ffn1_w1, m_ffn1_w3, m_ffn1_w2, m_norm_mix, m_w_in, m_conv_w, m_conv_b, m_conv_ln_g, m_conv_ln_b, m_conv_out_g, m_ssm_A_re, m_ssm_A_im, m_ssm_log_dt, m_ssm_B_re, m_ssm_B_im, m_ssm_C_re, m_ssm_C_im, m_ssm_D, m_ssm_glu_w, m_ssm_glu_b, m_ssm_out_g, m_w_out, m_norm_ffn2, m_ffn2_w1, m_ffn2_w3, m_ffn2_w2, m_norm_final, v_norm_ffn1, v_ffn1_w1, v_ffn1_w3, v_ffn1_w2, v_norm_mix, v_w_in, v_conv_w, v_conv_b, v_conv_ln_g, v_conv_ln_b, v_conv_out_g, v_ssm_A_re, v_ssm_A_im, v_ssm_log_dt, v_ssm_B_re, v_ssm_B_im, v_ssm_C_re, v_ssm_C_im, v_ssm_D, v_ssm_glu_w, v_ssm_glu_b, v_ssm_out_g, v_w_out, v_norm_ffn2, v_ffn2_w1, v_ffn2_w3, v_ffn2_w2, v_norm_final):
    args = dict(locals())
    wt = {n: args[n] for n in WEIGHTS}
    mom = {n: args["m_" + n] for n in WEIGHTS}
    var = {n: args["v_" + n] for n in WEIGHTS}

    bsz, seq, d = x.shape
    n = bsz * seq
    c = conv_b.shape[-1]
    groups = c // SSM_GROUP
    gp = groups * SSM_STATE
    u_b = 2

    shard = {k: (wt[k][0].T if k in COL_SHARDED else wt[k][0]).astype(BF16) for k in BIG}
    gathered = _run_exchange("gather_weights_ffn1", _gather_plan([shard[k] for k in FFN1]))
    full = {k: g.reshape(-1, g.shape[-1]) for k, g in zip(FFN1, gathered)}
    gather_rest = _gather_plan([shard[k] for k in MIXER + FFN2] + [wt["conv_w"][0]])

    vec = lambda k: wt[k].reshape(1, -1)
    g_ffn1, g_mix, g_ffn2, g_fin = vec("norm_ffn1"), vec("norm_mix"), vec("norm_ffn2"), vec("norm_final")
    cb, lng, lnb, cog = vec("conv_b"), vec("conv_ln_g"), vec("conv_ln_b"), vec("conv_out_g")
    d_skip, glu_b, sog = vec("ssm_D"), vec("ssm_glu_b"), vec("ssm_out_g")

    a_re, a_im = wt["ssm_A_re"][0], wt["ssm_A_im"][0]
    log_dt = wt["ssm_log_dt"][0].reshape(groups, 1)
    bt_re = wt["ssm_B_re"][0].transpose(0, 2, 1).reshape(groups * SSM_GROUP, SSM_STATE)
    bt_im = wt["ssm_B_im"][0].transpose(0, 2, 1).reshape(groups * SSM_GROUP, SSM_STATE)
    c_re = wt["ssm_C_re"][0].reshape(groups * SSM_GROUP, SSM_STATE)
    c_im = wt["ssm_C_im"][0].reshape(groups * SSM_GROUP, SSM_STATE)
    per_chan = lambda t: jnp.repeat(t, SSM_GROUP, axis=0)
    ssm_prim = (a_re, a_im, log_dt, per_chan(a_re), per_chan(a_im), per_chan(jnp.broadcast_to(log_dt, a_re.shape)),
                bt_re, bt_im)
    pw_r, pw_i, bb_r, bb_i = _ssm_prep("ssm_prep", ssm_prim)
    tab_f = _scan_tables(pw_r, pw_i, False)
    tab_b = _scan_tables(pw_r, pw_i, True)
    bbd = jnp.concatenate([_block_diag(bb_r, groups), _block_diag(bb_i, groups)], axis=1).astype(BF16)
    cdt = jnp.concatenate([_block_diag(c_re, groups), -_block_diag(c_im, groups)], axis=1).astype(BF16)

    x0 = x.reshape(n, d)
    (x1, *ffn1_saved), gathered = _ffn_fwd("ffn1_fwd", x0, g_ffn1, full["ffn1_w1"], full["ffn1_w3"], full["ffn1_w2"],
                                           exchange=gather_rest)
    full.update({k: g.reshape(-1, g.shape[-1]) for k, g in zip(MIXER + FFN2, gathered)})
    conv_w_full = gathered[-1].transpose(1, 0, 2).reshape(CONV_WIDTH, c)
    conv_w_pad = jnp.pad(conv_w_full, ((0, CONV_HALO - CONV_WIDTH), (0, 0)))
    (proj,), h2 = _rms_mm("mix_in", x1, g_mix, [full["w_in"]], F32)
    proj3 = proj.reshape(bsz, seq, 3 * c)
    an3, cv3 = _conv_fwd("conv_fwd", proj3, conv_w_pad, cb, lng, lnb, cog)
    an = an3.reshape(n, c)
    xs3, xs16, cx3 = _scan_fwd("scan_fwd", tab_f, proj3, u_b, bbd, cdt)
    w_o = full["w_out"]
    y, sn, x2 = _ssm_out_fwd("ssm_out_fwd", cx3.reshape(n, c), proj, u_b, d_skip, full["ssm_glu_w"], glu_b, sog,
                             x1, an, w_o)
    (dx3, *ffn2_saved, loss_tile, d_gfin), _ = _ffn_fwd(
        "ffn2_fwd", x2, g_ffn2, full["ffn2_w1"], full["ffn2_w3"], full["ffn2_w2"],
        head=(g_fin, loss_target.reshape(n, d)))

    grads, from_chips = {}, {}
    (dx2, grads["norm_ffn2"], dws), _, _ = _ffn_backward(
        "ffn2", dx3, x2, g_ffn2, full["ffn2_w1"], full["ffn2_w3"], full["ffn2_w2"], ffn2_saved)
    in_chip, across_chips = _reduce_in_chip(FFN2, dws)

    dmix, got = _mm_nt("mix_out_bwd", dx2, w_o, exchange=in_chip)
    reduce_ffn2 = across_chips(got)
    grads["w_out"] = _mm_tn("dw_out", (an, sn), dx2)

    dy, du_skip, grads["ssm_glu_w"], grads["ssm_glu_b"], grads["ssm_out_g"], grads["ssm_D"] = _ssm_out_bwd(
        "ssm_out_bwd", dmix, 1, y, proj, u_b, d_skip, full["ssm_glu_w"], glu_b, sog)
    (lam3, du3, dab_r, dab_i), got = _scan_bwd("scan_bwd", tab_b, dy.reshape(bsz, seq, c), xs3,
                                               du_skip.reshape(bsz, seq, c), bbd, cdt, exchange=reduce_ffn2)
    from_chips.update(zip(FFN2, got))
    lam, du = lam3.reshape(n, 2 * gp), du3.reshape(n, c)
    d_bbd = _band_wgrad("ssm_dbb", proj, u_b, c, lam)
    d_cdt = _band_wgrad("ssm_dc", dy, 0, c, xs16.reshape(n, 2 * gp))
    d_are, d_aim, d_ldt, d_btr, d_bti = _ssm_param_grads(
        "ssm_param_grads", ssm_prim,
        dab_r.reshape(SUBLANES, groups, SSM_STATE), dab_i.reshape(SUBLANES, groups, SSM_STATE),
        _band_diag_take(d_bbd, 0, c, gp), _band_diag_take(d_bbd, 1, c, gp))
    grads["ssm_A_re"], grads["ssm_A_im"], grads["ssm_log_dt"] = d_are, d_aim, d_ldt
    grads["ssm_B_re"], grads["ssm_B_im"] = d_btr, d_bti
    grads["ssm_C_re"] = _band_diag_take(d_cdt, 0, c, gp)
    grads["ssm_C_im"] = -_band_diag_take(d_cdt, 1, c, gp)

    dconv3, d_cw, grads["conv_b"], grads["conv_ln_g"], grads["conv_ln_b"], grads["conv_out_g"] = _conv_bwd(
        "conv_bwd", dmix.reshape(bsz, seq, 2 * c), proj3, cv3, conv_w_pad, lng, lnb, cog)
    dconv = dconv3.reshape(n, 2 * c)
    grads["conv_w"] = d_cw[:CONV_WIDTH]
    grads["w_in"] = _mm_tn("dw_in", (dconv, du), h2)
    w_i = full["w_in"]
    in_chip, across_chips = _reduce_in_chip(MIXER, [grads[k] for k in MIXER])
    (dx1, grads["norm_mix"]), got = _dx_rms_bwd("mix_in_bwd", [(dconv, 2 * c, 0, w_i, 2 * c, 0), (du, c, 0, w_i, c, 2)],
                                                dx2, x1, g_mix, exchange=in_chip)
    reduce_mixer = across_chips(got)

    grads["norm_final"] = d_gfin
    early = tuple(k for k in SMALL if k != "norm_ffn1")
    gather_small = _gather_plan([grads[k] for k in early] + [grads["conv_w"], loss_tile])

    (dx0, grads["norm_ffn1"], _), got, reduced = _ffn_backward(
        "ffn1", dx1, x0, g_ffn1, full["ffn1_w1"], full["ffn1_w3"], full["ffn1_w2"], ffn1_saved,
        exchange=_merge_plans(reduce_mixer, gather_small), reduce_names=FFN1)
    from_chips.update(zip(MIXER, got))
    small_parts = got[len(MIXER):]
    from_chips.update(zip(FFN1, reduced))

    res = {}
    for k in BIG:
        parts = from_chips[k]
        if k in COL_SHARDED:
            swap = lambda t: jnp.swapaxes(t, -1, -2)
            res[k] = [swap(t) for t in _adamw("adamw_" + k, parts, swap(wt[k]), swap(mom[k]), swap(var[k]))]
        else:
            res[k] = _adamw("adamw_" + k, parts, wt[k], mom[k], var[k])

    def as_2d(k, t):
        if k in ("ssm_B_re", "ssm_B_im"):
            return t[0].transpose(0, 2, 1).reshape(-1, SSM_STATE)
        if k in ("ssm_C_re", "ssm_C_im"):
            return t[0].reshape(-1, SSM_STATE)
        if k in ("ssm_A_re", "ssm_A_im"):
            return t[0]
        return t.reshape(-1, 1) if k == "ssm_log_dt" else t.reshape(1, -1)

    def as_param(k, t):
        if k in ("ssm_B_re", "ssm_B_im"):
            t = t.reshape(groups, SSM_GROUP, SSM_STATE).transpose(0, 2, 1)
        return t.reshape(wt[k].shape)

    (last_part,) = _run_exchange("gather_norm_ffn1_grad", _gather_plan([grads["norm_ffn1"]]))
    order = ("norm_ffn1",) + early
    updated = _adamw_small("adamw_replicated", [last_part] + small_parts,
                           *[[as_2d(k, src[k]) for k in order] for src in (wt, mom, var)])
    res.update({k: [as_param(k, t) for t in upd] for k, upd in zip(order, updated)})
    (conv_w_grad,), (loss_sum,) = updated[-2:]
    loss = loss_sum[0, 0]
    x_pos, y_pos, c_pos = (lax.axis_index(a) for a in MESH_AXES)
    cw_cols = c // N_DEV
    own_cw = lax.dynamic_slice_in_dim(conv_w_grad, (4 * x_pos + 2 * y_pos + c_pos) * cw_cols, cw_cols, axis=1)
    res["conv_w"] = _adamw("adamw_conv_w", own_cw[None], wt["conv_w"], mom["conv_w"], var["conv_w"])

    outs = [loss, dx0.reshape(bsz, seq, d)]
    for kind in range(4):
        outs += [res[k][kind] for k in WEIGHTS]
    return tuple(outs)
```

```python
import collections
import functools
import math

import jax
import jax.numpy as jnp
from jax import lax
from jax.experimental import pallas as pl
from jax.experimental.pallas import tpu as pltpu

F32 = jnp.float32
BF16 = jnp.bfloat16

EPS = 1e-6
FFN_RES = 0.5
CONV_WIDTH = 31
CONV_HALO = 32
SSM_GROUP = 16
SSM_STATE = 64
ADAM_LR, ADAM_B1, ADAM_B2, ADAM_EPS, ADAM_WD, ADAM_STEP = 0.001, 0.9, 0.999, 1e-08, 0.01, 10

N_DEV = 8
MESH_AXES = ("x", "y", "c")
SUBLANES = 8
LANES = 128
V7X_VMEM_BYTES = 64 * 2**20
VMEM_LIMIT = V7X_VMEM_BYTES - 8 * 2**20

TILE = dict(row=512, hid_m=512, ffn_m=512, mm_bytes=8 * 2**20, up_m=1024, up_n=256, wide_n=2048, conv_t=256,
            scan_fwd_t=512, scan_t=256, scan_w=512, sum_bytes=4 * 2**20)

_GELU_K = math.sqrt(2.0 / math.pi)
_GELU_C = 0.044715


def _pick(n, target, mult):
    best = None
    for t in range(mult, min(n, target) + 1, mult):
        if n % t == 0:
            best = t
    return n if best is None else best


def _cparams(*sem):
    return pltpu.CompilerParams(dimension_semantics=sem, vmem_limit_bytes=VMEM_LIMIT)


def _sds(shape, dtype):
    return jax.ShapeDtypeStruct(shape, dtype)


def _call(name, body, grid, in_specs, out_specs, out_shape, operands, sem, scratch=(), exchange=None):
    if exchange is None:
        res = pl.pallas_call(body, name=name, grid=grid, in_specs=list(in_specs), out_specs=list(out_specs),
                             out_shape=list(out_shape), scratch_shapes=list(scratch),
                             compiler_params=_cparams(*sem))(*operands)
        return list(res), None
    n_in, n_out, n_scr = len(in_specs), len(out_specs), len(scratch)
    n_xin, n_xout = len(exchange.operands), len(exchange.out_shapes)
    hbm = pl.BlockSpec(memory_space=pltpu.HBM)

    def with_exchange(*refs):
        cuts, pos = [], 0
        for size in (n_in, n_xin, n_out, n_xout, n_scr):
            cuts.append(refs[pos:pos + size])
            pos += size
        ins, x_in, outs, x_out, scr = cuts
        sems = refs[pos:]
        ids = [pl.program_id(axis) for axis in range(len(grid))]
        first = functools.reduce(lambda p, q: p & q, [i == 0 for i in ids])
        last = functools.reduce(lambda p, q: p & q, [i == g - 1 for i, g in zip(ids, grid)])

        @pl.when(first)
        def _():
            exchange.start(x_in, x_out, sems)

        body(*ins, *outs, *scr)

        @pl.when(last)
        def _():
            exchange.finish(x_in, x_out, sems)

    res = pl.pallas_call(
        with_exchange, name=name, grid=grid, in_specs=list(in_specs) + [hbm] * n_xin,
        out_specs=list(out_specs) + [hbm] * n_xout, out_shape=list(out_shape) + list(exchange.out_shapes),
        scratch_shapes=list(scratch) + list(exchange.scratch),
        compiler_params=_cparams(*["arbitrary"] * len(grid)))(*operands, *exchange.operands)
    return list(res[:n_out]), list(res[n_out:])


def _dot(a, b):
    return jnp.dot(a, b, preferred_element_type=F32)


def _dot_nt(a, b):
    return lax.dot_general(a, b, (((1,), (1,)), ((), ())), preferred_element_type=F32)


def _dot_tn(a, b):
    return lax.dot_general(a, b, (((0,), (0,)), ((), ())), preferred_element_type=F32)


def _sigmoid(x):
    return 0.5 * jnp.tanh(0.5 * x) + 0.5


def _rms_stats(x):
    r = lax.rsqrt(jnp.mean(x * x, axis=-1, keepdims=True) + EPS)
    return r, x * r


def _rms_bwd(x, g, dy):
    r, xh = _rms_stats(x)
    dxh = dy * g
    dx = r * (dxh - xh * jnp.mean(dxh * xh, axis=-1, keepdims=True))
    return dx, jnp.sum(dy * xh, axis=0, keepdims=True)


def _rms_mm(name, x, g, ws, out_dtype):
    n, d = x.shape
    f = ws[0].shape[0]
    nw = len(ws)
    tm, tn = _pick(n, TILE["up_m"], 16), _pick(f, TILE["wide_n"], LANES)

    def body(x_ref, g_ref, *refs):
        w_refs, o_refs, h_ref = refs[:nw], refs[nw:2 * nw], refs[2 * nw]

        @pl.when(pl.program_id(1) == 0)
        def _():
            _, xh = _rms_stats(x_ref[...])
            h_ref[...] = (xh * g_ref[...]).astype(BF16)

        h = h_ref[...]
        for w_ref, o_ref in zip(w_refs, o_refs):
            o_ref[...] = _dot_nt(h, w_ref[...]).astype(o_ref.dtype)

    outs = pl.pallas_call(
        body, name=name, grid=(n // tm, f // tn),
        in_specs=[pl.BlockSpec((tm, d), lambda i, j: (i, 0)), pl.BlockSpec((1, d), lambda i, j: (0, 0))]
        + [pl.BlockSpec((tn, d), lambda i, j: (j, 0))] * nw,
        out_specs=[pl.BlockSpec((tm, tn), lambda i, j: (i, j))] * nw + [pl.BlockSpec((tm, d), lambda i, j: (i, 0))],
        out_shape=[_sds((n, f), out_dtype)] * nw + [_sds((n, d), BF16)],
        compiler_params=_cparams("parallel", "arbitrary"),
    )(x, g, *ws)
    return outs[:nw], outs[nw]


def _ffn_fwd(name, x, g, w1t, w3t, w2, exchange=None, head=None):
    n, d = x.shape
    f = w2.shape[0]
    tm, tn = _pick(n, TILE["ffn_m"], 16), _pick(f, TILE["up_n"], LANES)

    def body(x_ref, g_ref, w1_ref, w3_ref, w2_ref, *refs):
        (gf_ref, t_ref), refs = (refs[:2], refs[2:]) if head else ((None, None), refs)
        o_ref, a_ref, b_ref, h_ref = refs[:4]
        xv = x_ref[...]
        _, xh = _rms_stats(xv)
        h = (xh * g_ref[...]).astype(BF16)
        h_ref[...] = h
        acc = None
        for c0 in range(0, f, tn):
            cols = pl.ds(c0, tn)
            av, bv = _dot_nt(h, w1_ref[cols, :]), _dot_nt(h, w3_ref[cols, :])
            a_ref[:, cols] = av.astype(BF16)
            b_ref[:, cols] = bv.astype(BF16)
            t = _dot((av * _sigmoid(av) * bv).astype(BF16), w2_ref[cols, :])
            acc = t if acc is None else acc + t
        out = xv + FFN_RES * acc
        if head is None:
            o_ref[...] = out
        else:
            loss_ref, dg_ref = refs[4:]

            @pl.when(pl.program_id(0) == 0)
            def _():
                loss_ref[...] = jnp.zeros_like(loss_ref)
                dg_ref[...] = jnp.zeros_like(dg_ref)

            dx, loss, dg = _loss_head_rows(out, gf_ref[...], t_ref[...])
            o_ref[...] = dx
            loss_ref[...] += loss
            dg_ref[...] += dg

    row = pl.BlockSpec((tm, d), lambda i: (i, 0))
    wide = pl.BlockSpec((tm, f), lambda i: (i, 0))
    vec = pl.BlockSpec((1, d), lambda i: (0, 0))
    held = pl.BlockSpec((f, d), lambda i: (0, 0), pipeline_mode=pl.Buffered(1))
    extra_in, extra_out, extra_shape = ([vec, row], [pl.BlockSpec((SUBLANES, LANES), lambda i: (0, 0)), vec],
                                        [_sds((SUBLANES, LANES), F32), _sds((1, d), F32)]) if head else ([], [], [])
    return _call(
        name, body, (n // tm,), [row, vec, held, held, held] + extra_in, [row, wide, wide, row] + extra_out,
        [_sds((n, d), F32), _sds((n, f), BF16), _sds((n, f), BF16), _sds((n, d), BF16)] + extra_shape,
        (x, g, w1t, w3t, w2) + (tuple(head) if head else ()), ("arbitrary",) if head else ("parallel",),
        exchange=exchange)


def _ffn_bwd_hidden(name, dxo, a, b, w2, exchange=None):
    n, d = dxo.shape
    f = a.shape[1]
    tm, tn = _pick(n, TILE["hid_m"], 16), _pick(f, TILE["up_n"], LANES)

    def body(dx_ref, a_ref, b_ref, w_ref, da_ref, db_ref, hid_ref, dxh_ref):
        dxh = (FFN_RES * dx_ref[...]).astype(BF16)
        dxh_ref[...] = dxh
        for c0 in range(0, f, tn):
            cols = pl.ds(c0, tn)
            dhid = _dot_nt(dxh, w_ref[cols, :])
            av, bv = a_ref[:, cols].astype(F32), b_ref[:, cols].astype(F32)
            sig = _sigmoid(av)
            silu = av * sig
            da_ref[:, cols] = (dhid * bv * (sig * (1.0 + av - silu))).astype(BF16)
            db_ref[:, cols] = (dhid * silu).astype(BF16)
            hid_ref[:, cols] = (silu * bv).astype(BF16)

    wide = pl.BlockSpec((tm, f), lambda i: (i, 0))
    row = pl.BlockSpec((tm, d), lambda i: (i, 0))
    return _call(
        name, body, (n // tm,),
        [row, wide, wide, pl.BlockSpec((f, d), lambda i: (0, 0), pipeline_mode=pl.Buffered(1))], [wide, wide, wide, row],
        [_sds((n, f), BF16)] * 3 + [_sds((n, d), BF16)], (dxo, a, b, w2), ("parallel",), exchange=exchange)


def _loss_head_rows(x, g, target):
    r, xh = _rms_stats(x)
    err = xh * g - target
    dy = err * (1.0 / x.shape[-1])
    dxh = dy * g
    dx = r * (dxh - xh * jnp.mean(dxh * xh, axis=-1, keepdims=True))
    return dx, 0.5 * jnp.sum(jnp.mean(err * err, axis=-1, keepdims=True)), jnp.sum(dy * xh, axis=0, keepdims=True)


def _dx_rms_bwd(name, pairs, dxo, x, g, exchange=None):
    n, dm = x.shape
    tm = _pick(n, TILE["ffn_m"], 16)
    npair = len(pairs)

    def body(*refs):
        d_refs, w_refs = refs[:npair], refs[npair:2 * npair]
        dxo_ref, x_ref, g_ref, dx_ref, dg_ref = refs[2 * npair:]

        @pl.when(pl.program_id(0) == 0)
        def _():
            dg_ref[...] = jnp.zeros_like(dg_ref)

        dh = None
        for d_ref, w_ref in zip(d_refs, w_refs):
            t = _dot(d_ref[...].astype(BF16), w_ref[...])
            dh = t if dh is None else dh + t
        dx, dg = _rms_bwd(x_ref[...], g_ref[...], dh)
        dx_ref[...] = dxo_ref[...] + dx
        dg_ref[...] += dg

    row = pl.BlockSpec((tm, dm), lambda i: (i, 0))
    d_specs = [pl.BlockSpec((tm, p[1]), functools.partial(lambda i, cb: (i, cb), cb=p[2])) for p in pairs]
    w_specs = [pl.BlockSpec((p[4], dm), functools.partial(lambda i, rb: (rb, 0), rb=p[5]), pipeline_mode=pl.Buffered(1))
               for p in pairs]
    return _call(
        name, body, (n // tm,), d_specs + w_specs + [row, row, pl.BlockSpec((1, dm), lambda i: (0, 0))],
        [row, pl.BlockSpec((1, dm), lambda i: (0, 0))], [_sds((n, dm), F32), _sds((1, dm), F32)],
        (*[p[0] for p in pairs], *[p[3] for p in pairs], dxo, x, g), ("arbitrary",), exchange=exchange)


def _mm_tn(name, a, b, exchange=None):
    parts = tuple(a) if isinstance(a, (tuple, list)) else (a,)
    n, mb = b.shape
    widths = [p.shape[1] for p in parts]
    tk = _pick(n, TILE["mm_bytes"] // (sum(p.shape[1] * p.dtype.itemsize for p in parts) + mb * b.dtype.itemsize), 16)

    def body(*refs):
        a_refs, b_ref, o_ref = refs[:-2], refs[-2], refs[-1]

        @pl.when(pl.program_id(0) == 0)
        def _():
            o_ref[...] = jnp.zeros_like(o_ref)

        bv = b_ref[...].astype(BF16)
        row0 = 0
        for a_ref, width in zip(a_refs, widths):
            o_ref[pl.ds(row0, width), :] += _dot_tn(a_ref[...].astype(BF16), bv)
            row0 += width

    (out,), got = _call(
        name, body, (n // tk,),
        [pl.BlockSpec((tk, w), lambda k: (k, 0)) for w in widths] + [pl.BlockSpec((tk, mb), lambda k: (k, 0))],
        [pl.BlockSpec((sum(widths), mb), lambda k: (0, 0))], [_sds((sum(widths), mb), F32)], (*parts, b), ("arbitrary",),
        exchange=exchange)
    return out if exchange is None else (out, got)


def _mm_nt(name, a, w, exchange=None):
    n, k = a.shape
    m = w.shape[0]
    tm = _pick(n, TILE["row"], 16)

    def body(a_ref, w_ref, o_ref):
        o_ref[...] = _dot_nt(a_ref[...].astype(BF16), w_ref[...])

    (out,), got = _call(
        name, body, (n // tm,),
        [pl.BlockSpec((tm, k), lambda i: (i, 0)), pl.BlockSpec((m, k), lambda i: (0, 0), pipeline_mode=pl.Buffered(1))],
        [pl.BlockSpec((tm, m), lambda i: (i, 0))], [_sds((n, m), F32)], (a, w), ("parallel",), exchange=exchange)
    return out, got


def _conv_post(c, ln_g, ln_b, out_g):
    mu = jnp.mean(c, axis=-1, keepdims=True)
    xc = c - mu
    rstd = lax.rsqrt(jnp.mean(xc * xc, axis=-1, keepdims=True) + EPS)
    nrm = xc * rstd
    l = nrm * ln_g + ln_b
    sig = _sigmoid(l)
    s = l * sig
    r, sh = _rms_stats(s)
    return sh * out_g, (rstd, nrm, l, sig, r, sh)


def _tap_groups(first):
    groups = []
    for r in range(SUBLANES):
        taps = [(s - r, s - first) for s in range(first, first + CONV_WIDTH) if s % SUBLANES == r]
        if taps:
            groups.append((r, taps))
    return groups


def _conv_taps(a_ref, w_ref, b_ref, first, rows, flip=False):
    acc = None
    for r, taps in _tap_groups(first):
        ext = rows if r == 0 else rows + SUBLANES
        part = None
        for base, k in taps:
            kk = CONV_WIDTH - 1 - k if flip else k
            t = w_ref[kk:kk + 1, :] * a_ref[pl.ds(base, ext), :]
            part = t if part is None else part + t
        if r:
            b_ref[...] = part
            part = b_ref[pl.ds(r, rows), :]
        acc = part if acc is None else acc + part
    return acc


def _conv_post_bwd(cv, dout, ln_g, ln_b, out_g):
    _, (rstd, nrm, l, sig, r, sh) = _conv_post(cv, ln_g, ln_b, out_g)
    dsh = dout * out_g
    ds = r * (dsh - sh * jnp.mean(dsh * sh, axis=-1, keepdims=True))
    dl = ds * (sig * (1.0 + l * (1.0 - sig)))
    dn = dl * ln_g
    dc = rstd * (dn - jnp.mean(dn, axis=-1, keepdims=True) - nrm * jnp.mean(dn * nrm, axis=-1, keepdims=True))
    col_sum = lambda t: jnp.sum(t, axis=0, keepdims=True)
    return dc, col_sum(dout * sh), col_sum(dl * nrm), col_sum(dl)


def _conv_fwd(name, proj3, conv_w, conv_b, ln_g, ln_b, out_g):
    bsz, seq, _ = proj3.shape
    c = conv_w.shape[1]
    tt = _pick(seq, TILE["conv_t"], CONV_HALO)
    hb = tt // CONV_HALO
    first = CONV_HALO - (CONV_WIDTH - 1)

    def body(v_ref, g_ref, vp_ref, gp_ref, w_ref, cb_ref, lg_ref, lb_ref, og_ref, o_ref, cv_ref, a_ref, b_ref):
        keep = (pl.program_id(1) > 0).astype(F32)
        a_ref[pl.ds(0, CONV_HALO), :] = keep * vp_ref[0] * _sigmoid(gp_ref[0])
        a_ref[pl.ds(CONV_HALO, tt), :] = v_ref[0] * _sigmoid(g_ref[0])
        cv = _conv_taps(a_ref, w_ref, b_ref, first, tt) + cb_ref[...]
        cv_ref[0] = cv
        out, _ = _conv_post(cv, lg_ref[...], lb_ref[...], og_ref[...])
        o_ref[0] = out.astype(BF16)

    vec = pl.BlockSpec((1, c), lambda b, i: (0, 0))
    prev = lambda col: pl.BlockSpec((1, CONV_HALO, c), lambda b, i: (b, jnp.maximum(i * hb - 1, 0), col))
    tile = pl.BlockSpec((1, tt, c), lambda b, i: (b, i, 0))
    return pl.pallas_call(
        body, name=name, grid=(bsz, seq // tt),
        in_specs=[tile, pl.BlockSpec((1, tt, c), lambda b, i: (b, i, 1)),
                  prev(0), prev(1), pl.BlockSpec(conv_w.shape, lambda b, i: (0, 0)), vec, vec, vec, vec],
        out_specs=[tile, tile],
        out_shape=[_sds((bsz, seq, c), BF16), _sds((bsz, seq, c), F32)],
        scratch_shapes=[pltpu.VMEM((CONV_HALO + tt, c), F32), pltpu.VMEM((tt + SUBLANES, c), F32)],
        compiler_params=_cparams("parallel", "arbitrary"),
    )(proj3, proj3, proj3, proj3, conv_w, conv_b, ln_g, ln_b, out_g)


def _conv_bwd(name, dmix3, proj3, cv3, conv_w, ln_g, ln_b, out_g):
    bsz, seq, _ = proj3.shape
    c = conv_w.shape[1]
    tt = _pick(seq, TILE["conv_t"], CONV_HALO)
    hb = tt // CONV_HALO
    nt = seq // tt
    last_hb = seq // CONV_HALO - 1
    ext = tt + CONV_HALO
    first = CONV_HALO - (CONV_WIDTH - 1)

    def body(v_ref, g_ref, vp_ref, gp_ref, cv_ref, cvn_ref, d_ref, dn_ref, w_ref, lg_ref, lb_ref, og_ref,
             o_ref, dw_ref, dcb_ref, dlg_ref, dlb_ref, dog_ref, a_ref, dc_ref, b_ref, ds_ref):
        i = pl.program_id(1)

        @pl.when((pl.program_id(0) == 0) & (i == 0))
        def _():
            for r in (dw_ref, dcb_ref, dlg_ref, dlb_ref, dog_ref):
                r[...] = jnp.zeros_like(r)

        keep_prev = (i > 0).astype(F32)
        keep_next = (i < nt - 1).astype(F32)
        sig_g = _sigmoid(g_ref[0])
        a_ref[pl.ds(0, CONV_HALO), :] = keep_prev * vp_ref[0] * _sigmoid(gp_ref[0])
        a_ref[pl.ds(CONV_HALO, tt), :] = v_ref[0] * sig_g

        lg, lb, og = lg_ref[...], lb_ref[...], og_ref[...]
        dc_own, d_og, d_lg, d_lb = _conv_post_bwd(cv_ref[0], d_ref[0], lg, lb, og)
        dc_next, _, _, _ = _conv_post_bwd(cvn_ref[0], keep_next * dn_ref[0], lg, lb, og)
        dog_ref[...] += d_og
        dlg_ref[...] += d_lg
        dlb_ref[...] += d_lb
        dcb_ref[...] += jnp.sum(dc_own, axis=0, keepdims=True)
        dc_ref[pl.ds(0, tt), :] = dc_own
        dc_ref[pl.ds(tt, CONV_HALO), :] = dc_next

        da = _conv_taps(dc_ref, w_ref, b_ref, 0, tt, flip=True)

        for r, taps in _tap_groups(first):
            if r:
                ds_ref[pl.ds(0, SUBLANES), :] = jnp.zeros((SUBLANES, c), F32)
                ds_ref[pl.ds(tt, SUBLANES), :] = jnp.zeros((SUBLANES, c), F32)
                ds_ref[pl.ds(r, tt), :] = dc_own
            for base, k in taps:
                prod = (ds_ref[...] * a_ref[pl.ds(base, tt + SUBLANES), :]) if r else (dc_own * a_ref[pl.ds(base, tt), :])
                dw_ref[k:k + 1, :] += jnp.sum(prod, axis=0, keepdims=True)
        val = v_ref[0]
        o_ref[0] = jnp.concatenate([da * sig_g, da * val * sig_g * (1.0 - sig_g)], axis=-1).astype(BF16)

    vec = pl.BlockSpec((1, c), lambda b, i: (0, 0))
    cur = lambda col: pl.BlockSpec((1, tt, c), lambda b, i: (b, i, col))
    prev = lambda col: pl.BlockSpec((1, CONV_HALO, c), lambda b, i: (b, jnp.maximum(i * hb - 1, 0), col))
    nxt = lambda col: pl.BlockSpec((1, CONV_HALO, c), lambda b, i: (b, jnp.minimum((i + 1) * hb, last_hb), col))
    wspec = pl.BlockSpec(conv_w.shape, lambda b, i: (0, 0))
    return pl.pallas_call(
        body, name=name, grid=(bsz, nt),
        in_specs=[cur(0), cur(1), prev(0), prev(1), cur(0), nxt(0), cur(0), nxt(0), wspec, vec, vec, vec],
        out_specs=[pl.BlockSpec((1, tt, 2 * c), lambda b, i: (b, i, 0)), wspec, vec, vec, vec, vec],
        out_shape=[_sds((bsz, seq, 2 * c), BF16), _sds(conv_w.shape, F32)] + [_sds((1, c), F32)] * 4,
        scratch_shapes=[pltpu.VMEM((CONV_HALO + tt, c), F32), pltpu.VMEM((ext, c), F32),
                        pltpu.VMEM((tt + SUBLANES, c), F32), pltpu.VMEM((tt + SUBLANES, c), F32)],
        compiler_params=_cparams("arbitrary", "arbitrary"),
    )(proj3, proj3, proj3, proj3, cv3, cv3, dmix3, dmix3, conv_w, ln_g, ln_b, out_g)


def _ssm_discretise(a_re, a_im, log_dt):
    dt = jnp.exp(log_dt)
    zr, zi = a_re * dt, a_im * dt
    mag = jnp.exp(zr)
    ar, ai = mag * jnp.cos(zi), mag * jnp.sin(zi)
    den = a_re * a_re + a_im * a_im
    nr = ar - 1.0
    return ar, ai, (nr * a_re + ai * a_im) / den, (ai * a_re - nr * a_im) / den


def _ssm_system(a_re, a_im, log_dt, a_re_x, a_im_x, log_dt_x, bt_re, bt_im):
    ar, ai, _, _ = _ssm_discretise(a_re, a_im, log_dt)
    _, _, cr, ci = _ssm_discretise(a_re_x, a_im_x, log_dt_x)
    return ar, ai, cr * bt_re - ci * bt_im, cr * bt_im + ci * bt_re


def _ssm_prep(name, prim):
    g, p = prim[0].shape

    def body(*refs):
        pwr_ref, pwi_ref, bbr_ref, bbi_ref = refs[8:]
        ar, ai, bbr, bbi = _ssm_system(*[r[...] for r in refs[:8]])
        bbr_ref[...] = bbr
        bbi_ref[...] = bbi
        pr, pi = ar, ai
        for k in range(SUBLANES):
            pwr_ref[k] = pr
            pwi_ref[k] = pi
            pr, pi = pr * ar - pi * ai, pr * ai + pi * ar

    return pl.pallas_call(
        body, name=name,
        out_shape=[_sds((SUBLANES, g, p), F32)] * 2 + [_sds(prim[6].shape, F32)] * 2,
        compiler_params=pltpu.CompilerParams(vmem_limit_bytes=VMEM_LIMIT),
    )(*prim)


def _ssm_param_grads(name, prim, dab_r, dab_i, dbb_r, dbb_i):
    g, p = prim[0].shape
    h = prim[6].shape[0] // g

    def body(*refs):
        dar_ref, dai_ref, dbr_ref, dbi_ref = refs[8:12]
        o_ar, o_ai, o_dt, o_br, o_bi = refs[12:]
        _, vjp = jax.vjp(_ssm_system, *[r[...] for r in refs[:8]])
        ct = (jnp.sum(dar_ref[...], axis=0), jnp.sum(dai_ref[...], axis=0), dbr_ref[...], dbi_ref[...])
        d_ar, d_ai, d_dt, d_arx, d_aix, d_dtx, d_br, d_bi = vjp(ct)
        per_group = lambda t: jnp.sum(t.reshape(g, h, p), axis=1)
        o_ar[...] = d_ar + per_group(d_arx)
        o_ai[...] = d_ai + per_group(d_aix)
        o_dt[...] = d_dt + jnp.sum(per_group(d_dtx), axis=1, keepdims=True)
        o_br[...] = d_br
        o_bi[...] = d_bi

    return pl.pallas_call(
        body, name=name,
        out_shape=[_sds(prim[k].shape, F32) for k in (0, 1, 2, 6, 7)],
        compiler_params=pltpu.CompilerParams(vmem_limit_bytes=VMEM_LIMIT),
    )(*prim, dab_r, dab_i, dbb_r, dbb_i)


def _cfma(xr, xi, cr, ci, sr, si):
    return xr + (cr * sr - ci * si), xi + (cr * si + ci * sr)


def _scan_tables(pw_r, pw_i, reverse):
    gp = pw_r.shape[1] * pw_r.shape[2]
    pr, pi = pw_r.reshape(SUBLANES, gp), pw_i.reshape(SUBLANES, gp)
    if reverse:
        pi = -pi
    row = jnp.arange(SUBLANES)[:, None]
    tabs = []
    for d in (1, 2, 4):
        keep = (row < SUBLANES - d) if reverse else (row >= d)
        tabs += [jnp.where(keep, pr[d - 1][None, :], 0.0), jnp.where(keep, pi[d - 1][None, :], 0.0)]
    tabs += [pr[::-1], pi[::-1]] if reverse else [pr, pi]
    return jnp.concatenate(tabs, axis=0)


MXU_DEPTH = 256


def _bands(c, gp):
    bw = min(c, MXU_DEPTH)
    return c // bw, bw, gp * bw // c


def _band_expand(rows16, w_ref, put, c, gp):
    nb, bw, sw = _bands(c, gp)
    for s in range(nb):
        band = rows16[:, s * bw:(s + 1) * bw]
        for half in (0, gp):
            cols = pl.ds(half + s * sw, sw)
            put(cols, _dot(band, w_ref[pl.ds(s * bw, bw), cols]))


def _band_contract(get16, w_ref, c, gp):
    nb, bw, sw = _bands(c, gp)
    out = []
    for s in range(nb):
        acc = None
        for half in (0, gp):
            cols = pl.ds(half + s * sw, sw)
            t = _dot_nt(get16(cols), w_ref[pl.ds(s * bw, bw), cols])
            acc = t if acc is None else acc + t
        out.append(acc)
    return out[0] if nb == 1 else jnp.concatenate(out, axis=1)


def _band_wgrad(name, a, a_block, c, b):
    n = a.shape[0]
    gp = b.shape[1] // 2
    nb, bw, sw = _bands(c, gp)
    tk = _pick(n, TILE["mm_bytes"] // (c * a.dtype.itemsize + 2 * gp * b.dtype.itemsize), 16)

    def body(a_ref, b_ref, o_ref):
        @pl.when(pl.program_id(0) == 0)
        def _():
            o_ref[...] = jnp.zeros_like(o_ref)

        for s in range(nb):
            band = a_ref[:, s * bw:(s + 1) * bw].astype(BF16)
            for h, half in enumerate((0, gp)):
                o_ref[pl.ds(s * bw, bw), pl.ds(h * sw, sw)] += _dot_tn(
                    band, b_ref[:, pl.ds(half + s * sw, sw)].astype(BF16))

    return pl.pallas_call(
        body, name=name, grid=(n // tk,),
        in_specs=[pl.BlockSpec((tk, c), lambda k: (k, a_block)), pl.BlockSpec((tk, 2 * gp), lambda k: (k, 0))],
        out_specs=pl.BlockSpec((c, 2 * sw), lambda k: (0, 0)),
        out_shape=_sds((c, 2 * sw), F32),
        compiler_params=_cparams("arbitrary"),
    )(a, b)


def _band_diag_take(comp, half, c, gp):
    nb, bw, sw = _bands(c, gp)
    return jnp.concatenate([_block_diag_take(comp[s * bw:(s + 1) * bw, half * sw:(half + 1) * sw], bw // SSM_GROUP)
                            for s in range(nb)], axis=0)


def _scan_fwd(name, tab, proj3, u_block, bbd, cdt):
    bsz, seq, _ = proj3.shape
    c, w = bbd.shape
    gp = w // 2
    tt = _pick(seq, TILE["scan_fwd_t"], 16)
    nblk = tt // SUBLANES
    cw = _pick(gp, TILE["scan_w"], LANES)

    def body(tab_ref, u_ref, bbd_ref, cdt_ref, xs_ref, xs16_ref, y_ref, carry_ref, bu_ref):
        @pl.when(pl.program_id(1) == 0)
        def _():
            carry_ref[...] = jnp.zeros_like(carry_ref)

        def put_bu(cols, val):
            bu_ref[0, :, cols] = val

        _band_expand(u_ref[0].astype(BF16), bbd_ref, put_bu, c, gp)

        for ch in range(gp // cw):
            re, im = pl.ds(ch * cw, cw), pl.ds(gp + ch * cw, cw)

            def blk(r, carry, re=re, im=im):
                tabs = [tab_ref[pl.ds(SUBLANES * k, SUBLANES), re] for k in range(8)]
                rows = pl.ds(pl.multiple_of(r * SUBLANES, SUBLANES), SUBLANES)
                xr, xi = bu_ref[0, rows, re], bu_ref[0, rows, im]
                for j, d in enumerate((1, 2, 4)):
                    xr, xi = _cfma(xr, xi, tabs[2 * j], tabs[2 * j + 1], pltpu.roll(xr, d, 0), pltpu.roll(xi, d, 0))
                xr, xi = _cfma(xr, xi, tabs[6], tabs[7], carry[0], carry[1])
                xs_ref[0, rows, re] = xr
                xs_ref[0, rows, im] = xi
                last = SUBLANES - 1
                return (jnp.broadcast_to(xr[last:, :], xr.shape), jnp.broadcast_to(xi[last:, :], xi.shape))

            cr, ci = lax.fori_loop(0, nblk, blk, (carry_ref[:, re], carry_ref[:, im]))
            carry_ref[:, re] = cr
            carry_ref[:, im] = ci

        xs16_ref[0] = xs_ref[0].astype(BF16)
        y_ref[0] = _band_contract(lambda cols: xs16_ref[0, :, cols], cdt_ref, c, gp)

    whole = lambda arr: pl.BlockSpec(arr.shape, lambda b, t: (0, 0), pipeline_mode=pl.Buffered(1))
    wide = pl.BlockSpec((1, tt, w), lambda b, t: (b, t, 0))
    return pl.pallas_call(
        body, name=name, grid=(bsz, seq // tt),
        in_specs=[whole(tab), pl.BlockSpec((1, tt, c), lambda b, t: (b, t, u_block)), whole(bbd), whole(cdt)],
        out_specs=[wide, wide, pl.BlockSpec((1, tt, c), lambda b, t: (b, t, 0))],
        out_shape=[_sds((bsz, seq, w), F32), _sds((bsz, seq, w), BF16), _sds((bsz, seq, c), F32)],
        scratch_shapes=[pltpu.VMEM((SUBLANES, w), F32), pltpu.VMEM((1, tt, w), F32)],
        compiler_params=_cparams("arbitrary", "arbitrary"),
    )(tab, proj3, bbd, cdt)


def _scan_bwd(name, tab, dy3, xs3, du_skip3, bbd, cdt, exchange=None):
    bsz, seq, w = xs3.shape
    c = bbd.shape[0]
    gp = w // 2
    tt = _pick(seq, TILE["scan_t"], 16)
    nblk = tt // SUBLANES
    cw = _pick(gp, TILE["scan_w"], LANES)
    nt = seq // tt

    def body(tab_ref, dy_ref, xs_ref, halo_ref, skip_ref, bbd_ref, cdt_ref, lam16_ref, du_ref, dar_ref, dai_ref,
             carry_ref, g_ref, lam_ref):
        t = pl.program_id(1)

        @pl.when(t == 0)
        def _():
            carry_ref[...] = jnp.zeros_like(carry_ref)

        @pl.when((pl.program_id(0) == 0) & (t == 0))
        def _():
            dar_ref[...] = jnp.zeros_like(dar_ref)
            dai_ref[...] = jnp.zeros_like(dai_ref)

        def put_g(cols, val):
            g_ref[0, :, cols] = val

        _band_expand(dy_ref[0], cdt_ref, put_g, c, gp)

        has_prev = (t < nt - 1).astype(F32)
        row0 = lax.broadcasted_iota(jnp.int32, (SUBLANES, cw), 0) == 0
        last = SUBLANES - 1

        for ch in range(gp // cw):
            re, im = pl.ds(ch * cw, cw), pl.ds(gp + ch * cw, cw)

            def step(rows, xm1r, xm1i, state, re=re, im=im):
                tabs = [tab_ref[pl.ds(SUBLANES * k, SUBLANES), re] for k in range(8)]
                cr, ci, accr, acci = state
                lr, li = g_ref[0, rows, re], g_ref[0, rows, im]
                for j, d in enumerate((1, 2, 4)):
                    lr, li = _cfma(lr, li, tabs[2 * j], tabs[2 * j + 1],
                                   pltpu.roll(lr, SUBLANES - d, 0), pltpu.roll(li, SUBLANES - d, 0))
                lr, li = _cfma(lr, li, tabs[6], tabs[7], cr, ci)
                lam_ref[0, rows, re] = lr
                lam_ref[0, rows, im] = li
                xr, xi = xs_ref[0, rows, re], xs_ref[0, rows, im]
                xpr = jnp.where(row0, jnp.broadcast_to(xm1r[last:, :], xr.shape), pltpu.roll(xr, 1, 0))
                xpi = jnp.where(row0, jnp.broadcast_to(xm1i[last:, :], xi.shape), pltpu.roll(xi, 1, 0))
                accr = accr + (lr * xpr + li * xpi)
                acci = acci + (li * xpr - lr * xpi)
                return (jnp.broadcast_to(lr[:1, :], lr.shape), jnp.broadcast_to(li[:1, :], li.shape), accr, acci)

            def blk(k, state, re=re, im=im, step=step):
                r = nblk - 1 - k
                rows = pl.ds(pl.multiple_of(r * SUBLANES, SUBLANES), SUBLANES)
                prev = pl.ds(pl.multiple_of((r - 1) * SUBLANES, SUBLANES), SUBLANES)
                return step(rows, xs_ref[0, prev, re], xs_ref[0, prev, im], state)

            zero = jnp.zeros((SUBLANES, cw), F32)
            state = lax.fori_loop(0, nblk - 1, blk, (carry_ref[:, re], carry_ref[:, im], zero, zero))
            cr, ci, accr, acci = step(pl.ds(0, SUBLANES), has_prev * halo_ref[0, :, re], has_prev * halo_ref[0, :, im], state)
            carry_ref[:, re] = cr
            carry_ref[:, im] = ci
            dar_ref[:, re] += accr
            dai_ref[:, re] += acci

        lam16_ref[0] = lam_ref[0].astype(BF16)
        du = _band_contract(lambda cols: lam16_ref[0, :, cols], bbd_ref, c, gp)
        du_ref[0] = (du + skip_ref[0]).astype(BF16)

    tile = pl.BlockSpec((1, tt, w), lambda b, t: (b, nt - 1 - t, 0))
    thin = pl.BlockSpec((1, tt, c), lambda b, t: (b, nt - 1 - t, 0))
    halo = pl.BlockSpec((1, SUBLANES, w), lambda b, t: (b, jnp.maximum((nt - 1 - t) * nblk - 1, 0), 0))
    acc = pl.BlockSpec((SUBLANES, gp), lambda b, t: (0, 0))
    whole = lambda arr: pl.BlockSpec(arr.shape, lambda b, t: (0, 0), pipeline_mode=pl.Buffered(1))
    return _call(
        name, body, (bsz, nt), [whole(tab), thin, tile, halo, thin, whole(bbd), whole(cdt)], [tile, thin, acc, acc],
        [_sds(xs3.shape, BF16), _sds((bsz, seq, c), BF16), _sds((SUBLANES, gp), F32), _sds((SUBLANES, gp), F32)],
        (tab, dy3, xs3, xs3, du_skip3, bbd, cdt), ("arbitrary", "arbitrary"),
        scratch=[pltpu.VMEM((SUBLANES, w), F32), pltpu.VMEM((1, tt, w), F32), pltpu.VMEM((1, tt, w), F32)],
        exchange=exchange)


def _gelu_parts(y):
    inner = _GELU_K * (y + _GELU_C * y * y * y)
    t = jnp.tanh(inner)
    return 0.5 * y * (1.0 + t), t


def _ssm_out_fwd(name, cx, proj, u_block, d_skip, glu_w, glu_b, out_g, x, conv_out, w_out):
    n, c = cx.shape
    d = x.shape[1]
    tm = _pick(n, TILE["row"], 16)

    def body(cx_ref, u_ref, d_ref, gw_ref, gb_ref, og_ref, x_ref, a_ref, wo_ref, y_ref, o_ref, xo_ref):
        y = cx_ref[...] + d_ref[...] * u_ref[...]
        y_ref[...] = y
        gy, _ = _gelu_parts(y)
        z = _dot(gy.astype(BF16), gw_ref[...]) + gb_ref[...]
        _, sh = _rms_stats(gy * _sigmoid(z))
        out = (sh * og_ref[...]).astype(BF16)
        o_ref[...] = out
        xo_ref[...] = x_ref[...] + _dot(a_ref[...], wo_ref[pl.ds(0, c), :]) + _dot(out, wo_ref[pl.ds(c, c), :])

    vec = pl.BlockSpec((1, c), lambda i: (0, 0))
    row = pl.BlockSpec((tm, c), lambda i: (i, 0))
    wide = pl.BlockSpec((tm, d), lambda i: (i, 0))
    held = lambda arr: pl.BlockSpec(arr.shape, lambda i: (0, 0), pipeline_mode=pl.Buffered(1))
    return pl.pallas_call(
        body, name=name, grid=(n // tm,),
        in_specs=[row, pl.BlockSpec((tm, c), lambda i: (i, u_block)), vec, held(glu_w), vec, vec, wide, row, held(w_out)],
        out_specs=[row, row, wide],
        out_shape=[_sds((n, c), F32), _sds((n, c), BF16), _sds((n, d), F32)],
        compiler_params=_cparams("parallel"),
    )(cx, proj, d_skip, glu_w, glu_b, out_g, x, conv_out, w_out)


def _ssm_out_bwd(name, dmix, d_block, y, proj, u_block, d_skip, glu_w, glu_b, out_g):
    n, c = y.shape
    tm = _pick(n, TILE["row"], 16)

    def body(d_ref, y_ref, u_ref, dk_ref, gw_ref, gb_ref, og_ref, dy_ref, du_ref, dgw_ref, dgb_ref, dog_ref, dd_ref):
        @pl.when(pl.program_id(0) == 0)
        def _():
            for r in (dgw_ref, dgb_ref, dog_ref, dd_ref):
                r[...] = jnp.zeros_like(r)

        yv = y_ref[...]
        gy, th = _gelu_parts(yv)
        gy16 = gy.astype(BF16)
        sz = _sigmoid(_dot(gy16, gw_ref[...]) + gb_ref[...])
        r, sh = _rms_stats(gy * sz)
        dout = d_ref[...]
        dog_ref[...] += jnp.sum(dout * sh, axis=0, keepdims=True)
        dsh = dout * og_ref[...]
        ds = r * (dsh - sh * jnp.mean(dsh * sh, axis=-1, keepdims=True))
        dz = ds * gy * sz * (1.0 - sz)
        dz16 = dz.astype(BF16)
        dgb_ref[...] += jnp.sum(dz, axis=0, keepdims=True)
        dgw_ref[...] += _dot_tn(gy16, dz16)
        dgy = ds * sz + _dot_nt(dz16, gw_ref[...])
        dgelu = 0.5 * (1.0 + th) + 0.5 * yv * (1.0 - th * th) * (_GELU_K * (1.0 + 3.0 * _GELU_C * yv * yv))
        dy = dgy * dgelu
        dy_ref[...] = dy.astype(BF16)
        du_ref[...] = dy * dk_ref[...]
        dd_ref[...] += jnp.sum(dy * u_ref[...], axis=0, keepdims=True)

    vec = pl.BlockSpec((1, c), lambda i: (0, 0))
    row = pl.BlockSpec((tm, c), lambda i: (i, 0))
    mat = pl.BlockSpec(glu_w.shape, lambda i: (0, 0))
    return pl.pallas_call(
        body, name=name, grid=(n // tm,),
        in_specs=[pl.BlockSpec((tm, c), lambda i: (i, d_block)), row, pl.BlockSpec((tm, c), lambda i: (i, u_block)),
                  vec, mat, vec, vec],
        out_specs=[row, row, mat, vec, vec, vec],
        out_shape=[_sds((n, c), BF16), _sds((n, c), F32), _sds(glu_w.shape, F32)] + [_sds((1, c), F32)] * 3,
        compiler_params=_cparams("arbitrary"),
    )(dmix, y, proj, d_skip, glu_w, glu_b, out_g)


def _mesh_pos():
    return tuple(lax.axis_index(a) for a in MESH_AXES)


def _other_chips(x, y):
    return [(1 - x, y), (x, 1 - y), (1 - x, 1 - y)]


def _remote(src, dst, send_sem, recv_sem, dev):
    return pltpu.make_async_remote_copy(src_ref=src, dst_ref=dst, send_sem=send_sem, recv_sem=recv_sem,
                                        device_id=dev, device_id_type=pl.DeviceIdType.MESH)


def _hbm_call(name, body, operands, out_shapes, scratch):
    hbm = pl.BlockSpec(memory_space=pltpu.HBM)
    return pl.pallas_call(body, name=name, in_specs=[hbm] * len(operands), out_specs=[hbm] * len(out_shapes),
                          out_shape=out_shapes, scratch_shapes=scratch)(*operands)


_Exchange = collections.namedtuple("_Exchange", "operands out_shapes scratch start finish")


def _merge_plans(p, q):
    cut = len(p.operands), len(p.out_shapes), len(p.scratch)

    def both(which):
        def run(x_refs, o_refs, sems):
            getattr(p, which)(x_refs[:cut[0]], o_refs[:cut[1]], sems[:cut[2]])
            getattr(q, which)(x_refs[cut[0]:], o_refs[cut[1]:], sems[cut[2]:])
        return run

    return _Exchange(p.operands + q.operands, p.out_shapes + q.out_shapes, p.scratch + q.scratch,
                     both("start"), both("finish"))


def _run_exchange(name, plan):
    nin, nout = len(plan.operands), len(plan.out_shapes)

    def body(*refs):
        parts = refs[:nin], refs[nin:nin + nout], refs[nin + nout:]
        plan.start(*parts)
        plan.finish(*parts)

    return _hbm_call(name, body, plan.operands, plan.out_shapes, plan.scratch)


def _gather_plan(blocks):
    nop = len(blocks)

    def copies(x_refs, o_refs, sems):
        send_sems, recv_sems, local_sems = sems
        x, y, c = _mesh_pos()
        me, sibling = (x, y, c), (x, y, 1 - c)
        chips = _other_chips(x, y)

        def copy(i, k, block_of, to, src=None):
            dst = o_refs[i].at[4 * block_of[0] + 2 * block_of[1] + block_of[2]]
            return _remote(dst if src is None else src, dst, send_sems.at[i, k], recv_sems.at[i, k], to)

        own = [pltpu.make_async_copy(x_refs[i], o_refs[i].at[4 * x + 2 * y + c], local_sems.at[i]) for i in range(nop)]
        first = []
        for i in range(nop):
            first.append(copy(i, 0, me, sibling, src=x_refs[i]))
            first += [copy(i, 1 + j, me, (*chip, c), src=x_refs[i]) for j, chip in enumerate(chips)]
        return copy, own, first, me, sibling, chips, c

    def start(x_refs, o_refs, sems):
        _, own, first, *_ = copies(x_refs, o_refs, sems)
        for cp in own + first:
            cp.start()

    def finish(x_refs, o_refs, sems):
        copy, own, first, me, sibling, chips, c = copies(x_refs, o_refs, sems)
        passed = []
        for i in range(nop):
            for j, chip in enumerate(chips):
                copy(i, 1 + j, (*chip, c), me).wait_recv()
                passed.append(copy(i, 4 + j, (*chip, c), sibling))
                passed[-1].start()
        for i in range(nop):
            copy(i, 0, sibling, me).wait_recv()
            for j, chip in enumerate(chips):
                copy(i, 4 + j, (*chip, 1 - c), me).wait_recv()
        for cp in first + passed:
            cp.wait_send()
        for cp in own:
            cp.wait()

    return _Exchange(list(blocks), [_sds((N_DEV,) + b.shape, b.dtype) for b in blocks],
                     [pltpu.SemaphoreType.DMA((nop, N_DEV - 1)), pltpu.SemaphoreType.DMA((nop, N_DEV - 1)),
                      pltpu.SemaphoreType.DMA((nop,))], start, finish)


def _core_exchange_plan(grads):
    nop = len(grads)

    def copies(x_refs, o_refs, sems):
        send_sems, recv_sems = sems
        x, y, c = _mesh_pos()
        return [_remote(x_refs[i].at[2 * q + (1 - c)], o_refs[i].at[q], send_sems.at[i, q], recv_sems.at[i, q],
                        (x, y, 1 - c)) for i in range(nop) for q in range(N_DEV // 2)]

    def start(x_refs, o_refs, sems):
        for cp in copies(x_refs, o_refs, sems):
            cp.start()

    def finish(x_refs, o_refs, sems):
        for cp in copies(x_refs, o_refs, sems):
            cp.wait()

    return _Exchange(list(grads), [_sds((N_DEV // 2,) + g.shape[1:], g.dtype) for g in grads],
                     [pltpu.SemaphoreType.DMA((nop, N_DEV // 2)), pltpu.SemaphoreType.DMA((nop, N_DEV // 2))],
                     start, finish)


def _pair_sum(name, grad, other):
    nchip, _, r, c = grad.shape
    tr = _pick(r, max(SUBLANES, TILE["sum_bytes"] // (4 * c)), SUBLANES)
    core = lax.axis_index("c").astype(jnp.int32).reshape(1)

    def body(core_ref, g_ref, o_ref, s_ref):
        s_ref[0] = (g_ref[0, 0] + o_ref[0]).astype(s_ref.dtype)

    tile = pl.BlockSpec((1, tr, c), lambda q, t, core_ref: (q, t, 0))
    return pl.pallas_call(
        body, name=name,
        grid_spec=pltpu.PrefetchScalarGridSpec(
            num_scalar_prefetch=1, grid=(nchip, r // tr),
            in_specs=[pl.BlockSpec((1, 1, tr, c), lambda q, t, core_ref: (q, core_ref[0], t, 0)), tile],
            out_specs=tile),
        out_shape=_sds((nchip, r, c), BF16),
        compiler_params=_cparams("parallel", "parallel"),
    )(core, grad, other)


def _chip_exchange_plan(sums):
    nop = len(sums)

    def copies(x_refs, o_refs, sems, arriving):
        send_sems, recv_sems, local_sems = sems
        x, y, c = _mesh_pos()
        mine = 2 * x + y
        out = []
        for i in range(nop):
            for j, (px, py) in enumerate(_other_chips(x, y)):
                theirs = 2 * px + py
                src, dst = (mine, theirs) if arriving else (theirs, mine)
                out.append(_remote(x_refs[i].at[src], o_refs[i].at[dst], send_sems.at[i, j], recv_sems.at[i, j],
                                   (px, py, c)))
        if not arriving:
            out += [pltpu.make_async_copy(x_refs[i].at[mine], o_refs[i].at[mine], local_sems.at[i]) for i in range(nop)]
        return out

    def start(x_refs, o_refs, sems):
        for cp in copies(x_refs, o_refs, sems, False):
            cp.start()

    def finish(x_refs, o_refs, sems):
        for cp in copies(x_refs, o_refs, sems, True):
            cp.wait_recv()
        mine = copies(x_refs, o_refs, sems, False)
        for cp in mine[:3 * nop]:
            cp.wait_send()
        for cp in mine[3 * nop:]:
            cp.wait()

    return _Exchange(list(sums), [_sds(s.shape, s.dtype) for s in sums],
                     [pltpu.SemaphoreType.DMA((nop, 3)), pltpu.SemaphoreType.DMA((nop, 3)), pltpu.SemaphoreType.DMA((nop,))],
                     start, finish)


def _part_rows(npart, r, c):
    return _pick(r, max(SUBLANES, TILE["sum_bytes"] // (4 * npart * c)), SUBLANES)


def _sum_slots(p_ref):
    g = p_ref[0].astype(F32)
    for k in range(1, p_ref.shape[0]):
        g = g + p_ref[k].astype(F32)
    return g


def _adamw_step(g, w, m, v):
    c1 = 1.0 - ADAM_B1 ** ADAM_STEP
    c2 = 1.0 - ADAM_B2 ** ADAM_STEP
    nm = ADAM_B1 * m + (1.0 - ADAM_B1) * g
    nv = ADAM_B2 * v + (1.0 - ADAM_B2) * (g * g)
    return -ADAM_LR * ((nm / c1) / (jnp.sqrt(nv / c2) + ADAM_EPS) + ADAM_WD * w), nm, nv


def _adamw_small(name, parts, ws, ms, vs):
    nparam, nall = len(ws), len(parts)

    def body(*refs):
        p_refs = refs[:nall]
        w_refs, m_refs, v_refs = (refs[nall + k * nparam:nall + (k + 1) * nparam] for k in range(3))
        outs = refs[nall + 3 * nparam:]
        for p in range(nall):
            g = _sum_slots(p_refs[p])
            if p < nparam:
                delta, nm, nv = _adamw_step(g, w_refs[p][...], m_refs[p][...], v_refs[p][...])
                for o_ref, val in zip(outs[4 * p:4 * p + 4], (g, delta, nm, nv)):
                    o_ref[...] = val
            else:
                outs[4 * nparam + p - nparam][...] = g

    shapes = [_sds(w.shape, F32) for w in ws for _ in range(4)] + [_sds(p.shape[1:], F32) for p in parts[nparam:]]
    res = pl.pallas_call(body, name=name, out_shape=shapes,
                         compiler_params=pltpu.CompilerParams(vmem_limit_bytes=VMEM_LIMIT))(*parts, *ws, *ms, *vs)
    return [res[4 * p:4 * p + 4] for p in range(nparam)] + [[r] for r in res[4 * nparam:]]


def _adamw(name, parts, w, m, v):
    npart, r, c = parts.shape
    lead = len(w.shape) - 2
    tr = _part_rows(npart, r, c)
    at = (0,) * lead + (slice(None), slice(None))

    def body(p_ref, w_ref, m_ref, v_ref, g_ref, d_ref, nm_ref, nv_ref):
        g = _sum_slots(p_ref)
        delta, nm, nv = _adamw_step(g, w_ref[at], m_ref[at], v_ref[at])
        g_ref[at] = g
        nm_ref[at] = nm
        nv_ref[at] = nv
        d_ref[at] = delta

    row = pl.BlockSpec((1,) * lead + (tr, c), lambda i: (0,) * lead + (i, 0))
    return pl.pallas_call(
        body, name=name, grid=(r // tr,),
        in_specs=[pl.BlockSpec((npart, tr, c), lambda i: (0, i, 0)), row, row, row],
        out_specs=[row] * 4,
        out_shape=[_sds(w.shape, F32)] * 4,
        compiler_params=_cparams("parallel"),
    )(parts, w, m, v)


def _block_diag(rows_gh, groups):
    gh, p = rows_gh.shape
    own = (jnp.arange(gh)[:, None] // (gh // groups) == jnp.arange(groups)[None, :]).astype(rows_gh.dtype)
    return (own[:, :, None] * rows_gh[:, None, :]).reshape(gh, groups * p)


def _block_diag_take(dense, groups):
    gh = dense.shape[0]
    p = dense.shape[1] // groups
    own = (jnp.arange(gh)[:, None] // (gh // groups) == jnp.arange(groups)[None, :]).astype(dense.dtype)
    return jnp.sum(dense.reshape(gh, groups, p) * own[:, :, None], axis=1)


FFN1 = ("ffn1_w1", "ffn1_w3", "ffn1_w2")
MIXER = ("w_in", "ssm_glu_w", "w_out")
FFN2 = ("ffn2_w1", "ffn2_w3", "ffn2_w2")
BIG = FFN1 + MIXER + FFN2
COL_SHARDED = ("ffn1_w1", "ffn1_w3", "w_in", "ffn2_w1", "ffn2_w3", "conv_w")
SMALL = ("norm_ffn1", "norm_mix", "conv_b", "conv_ln_g", "conv_ln_b", "conv_out_g", "ssm_A_re", "ssm_A_im",
         "ssm_log_dt", "ssm_B_re", "ssm_B_im", "ssm_C_re", "ssm_C_im", "ssm_D", "ssm_glu_b", "ssm_out_g",
         "norm_ffn2", "norm_final")
WEIGHTS = ("norm_ffn1", "ffn1_w1", "ffn1_w3", "ffn1_w2", "norm_mix", "w_in", "conv_w", "conv_b", "conv_ln_g",
           "conv_ln_b", "conv_out_g", "ssm_A_re", "ssm_A_im", "ssm_log_dt", "ssm_B_re", "ssm_B_im", "ssm_C_re",
           "ssm_C_im", "ssm_D", "ssm_glu_w", "ssm_glu_b", "ssm_out_g", "w_out", "norm_ffn2", "ffn2_w1", "ffn2_w3",
           "ffn2_w2", "norm_final")


def _ffn_backward(tag, dxo, x, g, w1, w3, w2, saved, exchange=None, reduce_names=None):
    a, b, h = saved
    (da, db, hid, dxh), got = _ffn_bwd_hidden(tag + "_bwd_hidden", dxo, a, b, w2, exchange=exchange)
    dw1, dw3 = _mm_tn(tag + "_dw1", da, h), _mm_tn(tag + "_dw3", db, h)
    across = None
    if reduce_names:
        send = [_row_blocks(dw1), _row_blocks(dw3)]
        dw2, from_core = _mm_tn(tag + "_dw2", hid, dxh, exchange=_core_exchange_plan(send))
        send.append(_row_blocks(dw2))
        from_core += _run_exchange("exchange_core_" + tag, _core_exchange_plan(send[2:]))
        across = _across_chips(reduce_names, send, from_core)
    else:
        dw2 = _mm_tn(tag + "_dw2", hid, dxh)
    f = a.shape[1]
    (dx, dg), reduced = _dx_rms_bwd(tag + "_bwd_dx", [(da, f, 0, w1, f, 0), (db, f, 0, w3, f, 0)], dxo, x, g,
                                    exchange=across)
    return (dx, dg, [dw1, dw3, dw2]), got, reduced


def _row_blocks(grad):
    return grad.reshape((N_DEV, -1) + grad.shape[1:])


def _across_chips(names, send, from_core):
    return _chip_exchange_plan([_pair_sum("pair_sum_" + k, s.reshape((N_DEV // 2, 2) + s.shape[1:]), o)
                                for k, s, o in zip(names, send, from_core)])


def _reduce_in_chip(names, grads):
    send = [_row_blocks(g) for g in grads]
    return _core_exchange_plan(send), functools.partial(_across_chips, names, send)


def kernel(x, norm_ffn1, ffn1_w1, ffn1_w3, ffn1_w2, norm_mix, w_in, conv_w, conv_b, conv_ln_g, conv_ln_b, conv_out_g, ssm_A_re, ssm_A_im, ssm_log_dt, ssm_B_re, ssm_B_im, ssm_C_re, ssm_C_im, ssm_D, ssm_glu_w, ssm_glu_b, ssm_out_g, w_out, norm_ffn2, ffn2_w1, ffn2_w3, ffn2_w2, norm_final, loss_target, m_norm_ffn1, m_ffn1_w1, m_ffn1_w3, m_ffn1_w2, m_norm_mix, m_w_in, m_conv_w, m_conv_b, m_conv_ln_g, m_conv_ln_b, m_conv_out_g, m_ssm_A_re, m_ssm_A_im, m_ssm_log_dt, m_ssm_B_re, m_ssm_B_im, m_ssm_C_re, m_ssm_C_im, m_ssm_D, m_ssm_glu_w, m_ssm_glu_b, m_ssm_out_g, m_w_out, m_norm_ffn2, m_ffn2_w1, m_ffn2_w3, m_ffn2_w2, m_norm_final, v_norm_ffn1, v_ffn1_w1, v_ffn1_w3, v_ffn1_w2, v_norm_mix, v_w_in, v_conv_w, v_conv_b, v_conv_ln_g, v_conv_ln_b, v_conv_out_g, v_ssm_A_re, v_ssm_A_im, v_ssm_log_dt, v_ssm_B_re, v_ssm_B_im, v_ssm_C_re, v_ssm_C_im, v_ssm_D, v_ssm_glu_w, v_ssm_glu_b, v_ssm_out_g, v_w_out, v_norm_ffn2, v_ffn2_w1, v_ffn2_w3, v_ffn2_w2, v_norm_final):
    args = dict(locals())
    wt = {n: args[n] for n in WEIGHTS}
    mom = {n: args["m_" + n] for n in WEIGHTS}
    var = {n: args["v_" + n] for n in WEIGHTS}

    bsz, seq, d = x.shape
    n = bsz * seq
    c = conv_b.shape[-1]
    groups = c // SSM_GROUP
    gp = groups * SSM_STATE
    u_b = 2

    shard = {k: (wt[k][0].T if k in COL_SHARDED else wt[k][0]).astype(BF16) for k in BIG}
    gathered = _run_exchange("gather_weights_ffn1", _gather_plan([shard[k] for k in FFN1]))
    full = {k: g.reshape(-1, g.shape[-1]) for k, g in zip(FFN1, gathered)}
    gather_rest = _gather_plan([shard[k] for k in MIXER + FFN2] + [wt["conv_w"][0]])

    vec = lambda k: wt[k].reshape(1, -1)
    g_ffn1, g_mix, g_ffn2, g_fin = vec("norm_ffn1"), vec("norm_mix"), vec("norm_ffn2"), vec("norm_final")
    cb, lng, lnb, cog = vec("conv_b"), vec("conv_ln_g"), vec("conv_ln_b"), vec("conv_out_g")
    d_skip, glu_b, sog = vec("ssm_D"), vec("ssm_glu_b"), vec("ssm_out_g")

    a_re, a_im = wt["ssm_A_re"][0], wt["ssm_A_im"][0]
    log_dt = wt["ssm_log_dt"][0].reshape(groups, 1)
    bt_re = wt["ssm_B_re"][0].transpose(0, 2, 1).reshape(groups * SSM_GROUP, SSM_STATE)
    bt_im = wt["ssm_B_im"][0].transpose(0, 2, 1).reshape(groups * SSM_GROUP, SSM_STATE)
    c_re = wt["ssm_C_re"][0].reshape(groups * SSM_GROUP, SSM_STATE)
    c_im = wt["ssm_C_im"][0].reshape(groups * SSM_GROUP, SSM_STATE)
    per_chan = lambda t: jnp.repeat(t, SSM_GROUP, axis=0)
    ssm_prim = (a_re, a_im, log_dt, per_chan(a_re), per_chan(a_im), per_chan(jnp.broadcast_to(log_dt, a_re.shape)),
                bt_re, bt_im)
    pw_r, pw_i, bb_r, bb_i = _ssm_prep("ssm_prep", ssm_prim)
    tab_f = _scan_tables(pw_r, pw_i, False)
    tab_b = _scan_tables(pw_r, pw_i, True)
    bbd = jnp.concatenate([_block_diag(bb_r, groups), _block_diag(bb_i, groups)], axis=1).astype(BF16)
    cdt = jnp.concatenate([_block_diag(c_re, groups), -_block_diag(c_im, groups)], axis=1).astype(BF16)

    x0 = x.reshape(n, d)
    (x1, *ffn1_saved), gathered = _ffn_fwd("ffn1_fwd", x0, g_ffn1, full["ffn1_w1"], full["ffn1_w3"], full["ffn1_w2"],
                                           exchange=gather_rest)
    full.update({k: g.reshape(-1, g.shape[-1]) for k, g in zip(MIXER + FFN2, gathered)})
    conv_w_full = gathered[-1].transpose(1, 0, 2).reshape(CONV_WIDTH, c)
    conv_w_pad = jnp.pad(conv_w_full, ((0, CONV_HALO - CONV_WIDTH), (0, 0)))
    (proj,), h2 = _rms_mm("mix_in", x1, g_mix, [full["w_in"]], F32)
    proj3 = proj.reshape(bsz, seq, 3 * c)
    an3, cv3 = _conv_fwd("conv_fwd", proj3, conv_w_pad, cb, lng, lnb, cog)
    an = an3.reshape(n, c)
    xs3, xs16, cx3 = _scan_fwd("scan_fwd", tab_f, proj3, u_b, bbd, cdt)
    w_o = full["w_out"]
    y, sn, x2 = _ssm_out_fwd("ssm_out_fwd", cx3.reshape(n, c), proj, u_b, d_skip, full["ssm_glu_w"], glu_b, sog,
                             x1, an, w_o)
    (dx3, *ffn2_saved, loss_tile, d_gfin), _ = _ffn_fwd(
        "ffn2_fwd", x2, g_ffn2, full["ffn2_w1"], full["ffn2_w3"], full["ffn2_w2"],
        head=(g_fin, loss_target.reshape(n, d)))

    grads, from_chips = {}, {}
    (dx2, grads["norm_ffn2"], dws), _, _ = _ffn_backward(
        "ffn2", dx3, x2, g_ffn2, full["ffn2_w1"], full["ffn2_w3"], full["ffn2_w2"], ffn2_saved)
    in_chip, across_chips = _reduce_in_chip(FFN2, dws)

    dmix, got = _mm_nt("mix_out_bwd", dx2, w_o, exchange=in_chip)
    reduce_ffn2 = across_chips(got)
    grads["w_out"] = _mm_tn("dw_out", (an, sn), dx2)

    dy, du_skip, grads["ssm_glu_w"], grads["ssm_glu_b"], grads["ssm_out_g"], grads["ssm_D"] = _ssm_out_bwd(
        "ssm_out_bwd", dmix, 1, y, proj, u_b, d_skip, full["ssm_glu_w"], glu_b, sog)
    (lam3, du3, dab_r, dab_i), got = _scan_bwd("scan_bwd", tab_b, dy.reshape(bsz, seq, c), xs3,
                                               du_skip.reshape(bsz, seq, c), bbd, cdt, exchange=reduce_ffn2)
    from_chips.update(zip(FFN2, got))
    lam, du = lam3.reshape(n, 2 * gp), du3.reshape(n, c)
    d_bbd = _band_wgrad("ssm_dbb", proj, u_b, c, lam)
    d_cdt = _band_wgrad("ssm_dc", dy, 0, c, xs16.reshape(n, 2 * gp))
    d_are, d_aim, d_ldt, d_btr, d_bti = _ssm_param_grads(
        "ssm_param_grads", ssm_prim,
        dab_r.reshape(SUBLANES, groups, SSM_STATE), dab_i.reshape(SUBLANES, groups, SSM_STATE),
        _band_diag_take(d_bbd, 0, c, gp), _band_diag_take(d_bbd, 1, c, gp))
    grads["ssm_A_re"], grads["ssm_A_im"], grads["ssm_log_dt"] = d_are, d_aim, d_ldt
    grads["ssm_B_re"], grads["ssm_B_im"] = d_btr, d_bti
    grads["ssm_C_re"] = _band_diag_take(d_cdt, 0, c, gp)
    grads["ssm_C_im"] = -_band_diag_take(d_cdt, 1, c, gp)

    dconv3, d_cw, grads["conv_b"], grads["conv_ln_g"], grads["conv_ln_b"], grads["conv_out_g"] = _conv_bwd(
        "conv_bwd", dmix.reshape(bsz, seq, 2 * c), proj3, cv3, conv_w_pad, lng, lnb, cog)
    dconv = dconv3.reshape(n, 2 * c)
    grads["conv_w"] = d_cw[:CONV_WIDTH]
    grads["w_in"] = _mm_tn("dw_in", (dconv, du), h2)
    w_i = full["w_in"]
    in_chip, across_chips = _reduce_in_chip(MIXER, [grads[k] for k in MIXER])
    (dx1, grads["norm_mix"]), got = _dx_rms_bwd("mix_in_bwd", [(dconv, 2 * c, 0, w_i, 2 * c, 0), (du, c, 0, w_i, c, 2)],
                                                dx2, x1, g_mix, exchange=in_chip)
    reduce_mixer = across_chips(got)

    grads["norm_final"] = d_gfin
    early = tuple(k for k in SMALL if k != "norm_ffn1")
    gather_small = _gather_plan([grads[k] for k in early] + [grads["conv_w"], loss_tile])

    (dx0, grads["norm_ffn1"], _), got, reduced = _ffn_backward(
        "ffn1", dx1, x0, g_ffn1, full["ffn1_w1"], full["ffn1_w3"], full["ffn1_w2"], ffn1_saved,
        exchange=_merge_plans(reduce_mixer, gather_small), reduce_names=FFN1)
    from_chips.update(zip(MIXER, got))
    small_parts = got[len(MIXER):]
    from_chips.update(zip(FFN1, reduced))

    res = {}
    for k in BIG:
        parts = from_chips[k]
        if k in COL_SHARDED:
            swap = lambda t: jnp.swapaxes(t, -1, -2)
            res[k] = [swap(t) for t in _adamw("adamw_" + k, parts, swap(wt[k]), swap(mom[k]), swap(var[k]))]
        else:
            res[k] = _adamw("adamw_" + k, parts, wt[k], mom[k], var[k])

    def as_2d(k, t):
        if k in ("ssm_B_re", "ssm_B_im"):
            return t[0].transpose(0, 2, 1).reshape(-1, SSM_STATE)
        if k in ("ssm_C_re", "ssm_C_im"):
            return t[0].reshape(-1, SSM_STATE)
        if k in ("ssm_A_re", "ssm_A_im"):
            return t[0]
        return t.reshape(-1, 1) if k == "ssm_log_dt" else t.reshape(1, -1)

    def as_param(k, t):
        if k in ("ssm_B_re", "ssm_B_im"):
            t = t.reshape(groups, SSM_GROUP, SSM_STATE).transpose(0, 2, 1)
        return t.reshape(wt[k].shape)

    (last_part,) = _run_exchange("gather_norm_ffn1_grad", _gather_plan([grads["norm_ffn1"]]))
    order = ("norm_ffn1",) + early
    updated = _adamw_small("adamw_replicated", [last_part] + small_parts,
                           *[[as_2d(k, src[k]) for k in order] for src in (wt, mom, var)])
    res.update({k: [as_param(k, t) for t in upd] for k, upd in zip(order, updated)})
    (conv_w_grad,), (loss_sum,) = updated[-2:]
    loss = loss_sum[0, 0]
    x_pos, y_pos, c_pos = (lax.axis_index(a) for a in MESH_AXES)
    cw_cols = c // N_DEV
    own_cw = lax.dynamic_slice_in_dim(conv_w_grad, (4 * x_pos + 2 * y_pos + c_pos) * cw_cols, cw_cols, axis=1)
    res["conv_w"] = _adamw("adamw_conv_w", own_cw[None], wt["conv_w"], mom["conv_w"], var["conv_w"])

    outs = [loss, dx0.reshape(bsz, seq, d)]
    for kind in range(4):
        outs += [res[k][kind] for k in WEIGHTS]
    return tuple(outs)
```

```python
import collections
import functools
import math

import jax
import jax.numpy as jnp
from jax import lax
from jax.experimental import pallas as pl
from jax.experimental.pallas import tpu as pltpu

F32 = jnp.float32
BF16 = jnp.bfloat16

EPS = 1e-6
FFN_RES = 0.5
CONV_WIDTH = 31
CONV_HALO = 32
SSM_GROUP = 16
SSM_STATE = 64
ADAM_LR, ADAM_B1, ADAM_B2, ADAM_EPS, ADAM_WD, ADAM_STEP = 0.001, 0.9, 0.999, 1e-08, 0.01, 10

N_DEV = 8
MESH_AXES = ("x", "y", "c")
SUBLANES = 8
LANES = 128
V7X_VMEM_BYTES = 64 * 2**20
VMEM_LIMIT = V7X_VMEM_BYTES - 8 * 2**20

TILE = dict(row=1024, hid_m=512, ffn_m=512, mm_bytes=8 * 2**20, up_m=1024, up_n=256, wide_n=2048, conv_t=256,
            scan_fwd_t=512, scan_t=256, scan_w=512, sum_bytes=4 * 2**20)

_GELU_K = math.sqrt(2.0 / math.pi)
_GELU_C = 0.044715


def _pick(n, target, mult):
    best = None
    for t in range(mult, min(n, target) + 1, mult):
        if n % t == 0:
            best = t
    return n if best is None else best


def _cparams(*sem):
    return pltpu.CompilerParams(dimension_semantics=sem, vmem_limit_bytes=VMEM_LIMIT)


def _sds(shape, dtype):
    return jax.ShapeDtypeStruct(shape, dtype)


def _call(name, body, grid, in_specs, out_specs, out_shape, operands, sem, scratch=(), exchange=None):
    if exchange is None:
        res = pl.pallas_call(body, name=name, grid=grid, in_specs=list(in_specs), out_specs=list(out_specs),
                             out_shape=list(out_shape), scratch_shapes=list(scratch),
                             compiler_params=_cparams(*sem))(*operands)
        return list(res), None
    n_in, n_out, n_scr = len(in_specs), len(out_specs), len(scratch)
    n_xin, n_xout = len(exchange.operands), len(exchange.out_shapes)
    hbm = pl.BlockSpec(memory_space=pltpu.HBM)

    def with_exchange(*refs):
        cuts, pos = [], 0
        for size in (n_in, n_xin, n_out, n_xout, n_scr):
            cuts.append(refs[pos:pos + size])
            pos += size
        ins, x_in, outs, x_out, scr = cuts
        sems = refs[pos:]
        ids = [pl.program_id(axis) for axis in range(len(grid))]
        first = functools.reduce(lambda p, q: p & q, [i == 0 for i in ids])
        last = functools.reduce(lambda p, q: p & q, [i == g - 1 for i, g in zip(ids, grid)])

        @pl.when(first)
        def _():
            exchange.start(x_in, x_out, sems)

        body(*ins, *outs, *scr)

        @pl.when(last)
        def _():
            exchange.finish(x_in, x_out, sems)

    res = pl.pallas_call(
        with_exchange, name=name, grid=grid, in_specs=list(in_specs) + [hbm] * n_xin,
        out_specs=list(out_specs) + [hbm] * n_xout, out_shape=list(out_shape) + list(exchange.out_shapes),
        scratch_shapes=list(scratch) + list(exchange.scratch),
        compiler_params=_cparams(*["arbitrary"] * len(grid)))(*operands, *exchange.operands)
    return list(res[:n_out]), list(res[n_out:])


def _dot(a, b):
    return jnp.dot(a, b, preferred_element_type=F32)


def _dot_nt(a, b):
    return lax.dot_general(a, b, (((1,), (1,)), ((), ())), preferred_element_type=F32)


def _dot_tn(a, b):
    return lax.dot_general(a, b, (((0,), (0,)), ((), ())), preferred_element_type=F32)


def _sigmoid(x):
    return 0.5 * jnp.tanh(0.5 * x) + 0.5


def _rms_stats(x):
    r = lax.rsqrt(jnp.mean(x * x, axis=-1, keepdims=True) + EPS)
    return r, x * r


def _rms_bwd(x, g, dy):
    r, xh = _rms_stats(x)
    dxh = dy * g
    dx = r * (dxh - xh * jnp.mean(dxh * xh, axis=-1, keepdims=True))
    return dx, jnp.sum(dy * xh, axis=0, keepdims=True)


def _rms_mm(name, x, g, ws, out_dtype):
    n, d = x.shape
    f = ws[0].shape[0]
    nw = len(ws)
    tm, tn = _pick(n, TILE["up_m"], 16), _pick(f, TILE["wide_n"], LANES)

    def body(x_ref, g_ref, *refs):
        w_refs, o_refs, h_ref = refs[:nw], refs[nw:2 * nw], refs[2 * nw]

        @pl.when(pl.program_id(1) == 0)
        def _():
            _, xh = _rms_stats(x_ref[...])
            h_ref[...] = (xh * g_ref[...]).astype(BF16)

        h = h_ref[...]
        for w_ref, o_ref in zip(w_refs, o_refs):
            o_ref[...] = _dot_nt(h, w_ref[...]).astype(o_ref.dtype)

    outs = pl.pallas_call(
        body, name=name, grid=(n // tm, f // tn),
        in_specs=[pl.BlockSpec((tm, d), lambda i, j: (i, 0)), pl.BlockSpec((1, d), lambda i, j: (0, 0))]
        + [pl.BlockSpec((tn, d), lambda i, j: (j, 0))] * nw,
        out_specs=[pl.BlockSpec((tm, tn), lambda i, j: (i, j))] * nw + [pl.BlockSpec((tm, d), lambda i, j: (i, 0))],
        out_shape=[_sds((n, f), out_dtype)] * nw + [_sds((n, d), BF16)],
        compiler_params=_cparams("parallel", "arbitrary"),
    )(x, g, *ws)
    return outs[:nw], outs[nw]


def _ffn_fwd(name, x, g, w1t, w3t, w2, exchange=None, head=None):
    n, d = x.shape
    f = w2.shape[0]
    tm, tn = _pick(n, TILE["ffn_m"], 16), _pick(f, TILE["up_n"], LANES)

    def body(x_ref, g_ref, w1_ref, w3_ref, w2_ref, *refs):
        (gf_ref, t_ref), refs = (refs[:2], refs[2:]) if head else ((None, None), refs)
        o_ref, a_ref, b_ref, h_ref = refs[:4]
        xv = x_ref[...]
        _, xh = _rms_stats(xv)
        h = (xh * g_ref[...]).astype(BF16)
        h_ref[...] = h
        acc = None
        for c0 in range(0, f, tn):
            cols = pl.ds(c0, tn)
            av, bv = _dot_nt(h, w1_ref[cols, :]), _dot_nt(h, w3_ref[cols, :])
            a_ref[:, cols] = av.astype(BF16)
            b_ref[:, cols] = bv.astype(BF16)
            t = _dot((av * _sigmoid(av) * bv).astype(BF16), w2_ref[cols, :])
            acc = t if acc is None else acc + t
        out = xv + FFN_RES * acc
        if head is None:
            o_ref[...] = out
        else:
            loss_ref, dg_ref = refs[4:]

            @pl.when(pl.program_id(0) == 0)
            def _():
                loss_ref[...] = jnp.zeros_like(loss_ref)
                dg_ref[...] = jnp.zeros_like(dg_ref)

            dx, loss, dg = _loss_head_rows(out, gf_ref[...], t_ref[...])
            o_ref[...] = dx
            loss_ref[...] += loss
            dg_ref[...] += dg

    row = pl.BlockSpec((tm, d), lambda i: (i, 0))
    wide = pl.BlockSpec((tm, f), lambda i: (i, 0))
    vec = pl.BlockSpec((1, d), lambda i: (0, 0))
    held = pl.BlockSpec((f, d), lambda i: (0, 0), pipeline_mode=pl.Buffered(1))
    extra_in, extra_out, extra_shape = ([vec, row], [pl.BlockSpec((SUBLANES, LANES), lambda i: (0, 0)), vec],
                                        [_sds((SUBLANES, LANES), F32), _sds((1, d), F32)]) if head else ([], [], [])
    return _call(
        name, body, (n // tm,), [row, vec, held, held, held] + extra_in, [row, wide, wide, row] + extra_out,
        [_sds((n, d), F32), _sds((n, f), BF16), _sds((n, f), BF16), _sds((n, d), BF16)] + extra_shape,
        (x, g, w1t, w3t, w2) + (tuple(head) if head else ()), ("arbitrary",) if head else ("parallel",),
        exchange=exchange)


def _ffn_bwd_hidden(name, dxo, a, b, w2, exchange=None):
    n, d = dxo.shape
    f = a.shape[1]
    tm, tn = _pick(n, TILE["hid_m"], 16), _pick(f, TILE["up_n"], LANES)

    def body(dx_ref, a_ref, b_ref, w_ref, da_ref, db_ref, hid_ref, dxh_ref):
        dxh = (FFN_RES * dx_ref[...]).astype(BF16)
        dxh_ref[...] = dxh
        for c0 in range(0, f, tn):
            cols = pl.ds(c0, tn)
            dhid = _dot_nt(dxh, w_ref[cols, :])
            av, bv = a_ref[:, cols].astype(F32), b_ref[:, cols].astype(F32)
            sig = _sigmoid(av)
            silu = av * sig
            da_ref[:, cols] = (dhid * bv * (sig * (1.0 + av - silu))).astype(BF16)
            db_ref[:, cols] = (dhid * silu).astype(BF16)
            hid_ref[:, cols] = (silu * bv).astype(BF16)

    wide = pl.BlockSpec((tm, f), lambda i: (i, 0))
    row = pl.BlockSpec((tm, d), lambda i: (i, 0))
    return _call(
        name, body, (n // tm,),
        [row, wide, wide, pl.BlockSpec((f, d), lambda i: (0, 0), pipeline_mode=pl.Buffered(1))], [wide, wide, wide, row],
        [_sds((n, f), BF16)] * 3 + [_sds((n, d), BF16)], (dxo, a, b, w2), ("parallel",), exchange=exchange)


def _loss_head_rows(x, g, target):
    r, xh = _rms_stats(x)
    err = xh * g - target
    dy = err * (1.0 / x.shape[-1])
    dxh = dy * g
    dx = r * (dxh - xh * jnp.mean(dxh * xh, axis=-1, keepdims=True))
    return dx, 0.5 * jnp.sum(jnp.mean(err * err, axis=-1, keepdims=True)), jnp.sum(dy * xh, axis=0, keepdims=True)


def _dx_rms_bwd(name, pairs, dxo, x, g, exchange=None):
    n, dm = x.shape
    tm = _pick(n, TILE["ffn_m"], 16)
    npair = len(pairs)

    def body(*refs):
        d_refs, w_refs = refs[:npair], refs[npair:2 * npair]
        dxo_ref, x_ref, g_ref, dx_ref, dg_ref = refs[2 * npair:]

        @pl.when(pl.program_id(0) == 0)
        def _():
            dg_ref[...] = jnp.zeros_like(dg_ref)

        dh = None
        for d_ref, w_ref in zip(d_refs, w_refs):
            t = _dot(d_ref[...].astype(BF16), w_ref[...])
            dh = t if dh is None else dh + t
        dx, dg = _rms_bwd(x_ref[...], g_ref[...], dh)
        dx_ref[...] = dxo_ref[...] + dx
        dg_ref[...] += dg

    row = pl.BlockSpec((tm, dm), lambda i: (i, 0))
    d_specs = [pl.BlockSpec((tm, p[1]), functools.partial(lambda i, cb: (i, cb), cb=p[2])) for p in pairs]
    w_specs = [pl.BlockSpec((p[4], dm), functools.partial(lambda i, rb: (rb, 0), rb=p[5]), pipeline_mode=pl.Buffered(1))
               for p in pairs]
    return _call(
        name, body, (n // tm,), d_specs + w_specs + [row, row, pl.BlockSpec((1, dm), lambda i: (0, 0))],
        [row, pl.BlockSpec((1, dm), lambda i: (0, 0))], [_sds((n, dm), F32), _sds((1, dm), F32)],
        (*[p[0] for p in pairs], *[p[3] for p in pairs], dxo, x, g), ("arbitrary",), exchange=exchange)


def _mm_tn(name, a, b, exchange=None):
    parts = tuple(a) if isinstance(a, (tuple, list)) else (a,)
    n, mb = b.shape
    widths = [p.shape[1] for p in parts]
    tk = _pick(n, TILE["mm_bytes"] // (sum(p.shape[1] * p.dtype.itemsize for p in parts) + mb * b.dtype.itemsize), 16)

    def body(*refs):
        a_refs, b_ref, o_ref = refs[:-2], refs[-2], refs[-1]

        @pl.when(pl.program_id(0) == 0)
        def _():
            o_ref[...] = jnp.zeros_like(o_ref)

        bv = b_ref[...].astype(BF16)
        row0 = 0
        for a_ref, width in zip(a_refs, widths):
            o_ref[pl.ds(row0, width), :] += _dot_tn(a_ref[...].astype(BF16), bv)
            row0 += width

    (out,), got = _call(
        name, body, (n // tk,),
        [pl.BlockSpec((tk, w), lambda k: (k, 0)) for w in widths] + [pl.BlockSpec((tk, mb), lambda k: (k, 0))],
        [pl.BlockSpec((sum(widths), mb), lambda k: (0, 0))], [_sds((sum(widths), mb), F32)], (*parts, b), ("arbitrary",),
        exchange=exchange)
    return out if exchange is None else (out, got)


def _mm_nt(name, a, w, exchange=None):
    n, k = a.shape
    m = w.shape[0]
    tm = _pick(n, TILE["row"], 16)

    def body(a_ref, w_ref, o_ref):
        o_ref[...] = _dot_nt(a_ref[...].astype(BF16), w_ref[...])

    (out,), got = _call(
        name, body, (n // tm,),
        [pl.BlockSpec((tm, k), lambda i: (i, 0)), pl.BlockSpec((m, k), lambda i: (0, 0), pipeline_mode=pl.Buffered(1))],
        [pl.BlockSpec((tm, m), lambda i: (i, 0))], [_sds((n, m), F32)], (a, w), ("parallel",), exchange=exchange)
    return out, got


def _conv_post(c, ln_g, ln_b, out_g):
    mu = jnp.mean(c, axis=-1, keepdims=True)
    xc = c - mu
    rstd = lax.rsqrt(jnp.mean(xc * xc, axis=-1, keepdims=True) + EPS)
    nrm = xc * rstd
    l = nrm * ln_g + ln_b
    sig = _sigmoid(l)
    s = l * sig
    r, sh = _rms_stats(s)
    return sh * out_g, (rstd, nrm, l, sig, r, sh)


def _tap_groups(first):
    groups = []
    for r in range(SUBLANES):
        taps = [(s - r, s - first) for s in range(first, first + CONV_WIDTH) if s % SUBLANES == r]
        if taps:
            groups.append((r, taps))
    return groups


def _conv_taps(a_ref, w_ref, b_ref, first, rows, flip=False):
    acc = None
    for r, taps in _tap_groups(first):
        ext = rows if r == 0 else rows + SUBLANES
        part = None
        for base, k in taps:
            kk = CONV_WIDTH - 1 - k if flip else k
            t = w_ref[kk:kk + 1, :] * a_ref[pl.ds(base, ext), :]
            part = t if part is None else part + t
        if r:
            b_ref[...] = part
            part = b_ref[pl.ds(r, rows), :]
        acc = part if acc is None else acc + part
    return acc


def _conv_post_bwd(cv, dout, ln_g, ln_b, out_g):
    _, (rstd, nrm, l, sig, r, sh) = _conv_post(cv, ln_g, ln_b, out_g)
    dsh = dout * out_g
    ds = r * (dsh - sh * jnp.mean(dsh * sh, axis=-1, keepdims=True))
    dl = ds * (sig * (1.0 + l * (1.0 - sig)))
    dn = dl * ln_g
    dc = rstd * (dn - jnp.mean(dn, axis=-1, keepdims=True) - nrm * jnp.mean(dn * nrm, axis=-1, keepdims=True))
    col_sum = lambda t: jnp.sum(t, axis=0, keepdims=True)
    return dc, col_sum(dout * sh), col_sum(dl * nrm), col_sum(dl)


def _conv_fwd(name, proj3, conv_w, conv_b, ln_g, ln_b, out_g):
    bsz, seq, _ = proj3.shape
    c = conv_w.shape[1]
    tt = _pick(seq, TILE["conv_t"], CONV_HALO)
    hb = tt // CONV_HALO
    first = CONV_HALO - (CONV_WIDTH - 1)

    def body(v_ref, g_ref, vp_ref, gp_ref, w_ref, cb_ref, lg_ref, lb_ref, og_ref, o_ref, cv_ref, a_ref, b_ref):
        keep = (pl.program_id(1) > 0).astype(F32)
        a_ref[pl.ds(0, CONV_HALO), :] = keep * vp_ref[0] * _sigmoid(gp_ref[0])
        a_ref[pl.ds(CONV_HALO, tt), :] = v_ref[0] * _sigmoid(g_ref[0])
        cv = _conv_taps(a_ref, w_ref, b_ref, first, tt) + cb_ref[...]
        cv_ref[0] = cv
        out, _ = _conv_post(cv, lg_ref[...], lb_ref[...], og_ref[...])
        o_ref[0] = out.astype(BF16)

    vec = pl.BlockSpec((1, c), lambda b, i: (0, 0))
    prev = lambda col: pl.BlockSpec((1, CONV_HALO, c), lambda b, i: (b, jnp.maximum(i * hb - 1, 0), col))
    tile = pl.BlockSpec((1, tt, c), lambda b, i: (b, i, 0))
    return pl.pallas_call(
        body, name=name, grid=(bsz, seq // tt),
        in_specs=[tile, pl.BlockSpec((1, tt, c), lambda b, i: (b, i, 1)),
                  prev(0), prev(1), pl.BlockSpec(conv_w.shape, lambda b, i: (0, 0)), vec, vec, vec, vec],
        out_specs=[tile, tile],
        out_shape=[_sds((bsz, seq, c), BF16), _sds((bsz, seq, c), F32)],
        scratch_shapes=[pltpu.VMEM((CONV_HALO + tt, c), F32), pltpu.VMEM((tt + SUBLANES, c), F32)],
        compiler_params=_cparams("parallel", "arbitrary"),
    )(proj3, proj3, proj3, proj3, conv_w, conv_b, ln_g, ln_b, out_g)


def _conv_bwd(name, dmix3, proj3, cv3, conv_w, ln_g, ln_b, out_g):
    bsz, seq, _ = proj3.shape
    c = conv_w.shape[1]
    tt = _pick(seq, TILE["conv_t"], CONV_HALO)
    hb = tt // CONV_HALO
    nt = seq // tt
    last_hb = seq // CONV_HALO - 1
    ext = tt + CONV_HALO
    first = CONV_HALO - (CONV_WIDTH - 1)

    def body(v_ref, g_ref, vp_ref, gp_ref, cv_ref, cvn_ref, d_ref, dn_ref, w_ref, lg_ref, lb_ref, og_ref,
             o_ref, dw_ref, dcb_ref, dlg_ref, dlb_ref, dog_ref, a_ref, dc_ref, b_ref, ds_ref):
        i = pl.program_id(1)

        @pl.when((pl.program_id(0) == 0) & (i == 0))
        def _():
            for r in (dw_ref, dcb_ref, dlg_ref, dlb_ref, dog_ref):
                r[...] = jnp.zeros_like(r)

        keep_prev = (i > 0).astype(F32)
        keep_next = (i < nt - 1).astype(F32)
        sig_g = _sigmoid(g_ref[0])
        a_ref[pl.ds(0, CONV_HALO), :] = keep_prev * vp_ref[0] * _sigmoid(gp_ref[0])
        a_ref[pl.ds(CONV_HALO, tt), :] = v_ref[0] * sig_g

        lg, lb, og = lg_ref[...], lb_ref[...], og_ref[...]
        dc_own, d_og, d_lg, d_lb = _conv_post_bwd(cv_ref[0], d_ref[0], lg, lb, og)
        dc_next, _, _, _ = _conv_post_bwd(cvn_ref[0], keep_next * dn_ref[0], lg, lb, og)
        dog_ref[...] += d_og
        dlg_ref[...] += d_lg
        dlb_ref[...] += d_lb
        dcb_ref[...] += jnp.sum(dc_own, axis=0, keepdims=True)
        dc_ref[pl.ds(0, tt), :] = dc_own
        dc_ref[pl.ds(tt, CONV_HALO), :] = dc_next

        da = _conv_taps(dc_ref, w_ref, b_ref, 0, tt, flip=True)

        for r, taps in _tap_groups(first):
            if r:
                ds_ref[pl.ds(0, SUBLANES), :] = jnp.zeros((SUBLANES, c), F32)
                ds_ref[pl.ds(tt, SUBLANES), :] = jnp.zeros((SUBLANES, c), F32)
                ds_ref[pl.ds(r, tt), :] = dc_own
            for base, k in taps:
                prod = (ds_ref[...] * a_ref[pl.ds(base, tt + SUBLANES), :]) if r else (dc_own * a_ref[pl.ds(base, tt), :])
                dw_ref[k:k + 1, :] += jnp.sum(prod, axis=0, keepdims=True)
        val = v_ref[0]
        o_ref[0] = jnp.concatenate([da * sig_g, da * val * sig_g * (1.0 - sig_g)], axis=-1).astype(BF16)

    vec = pl.BlockSpec((1, c), lambda b, i: (0, 0))
    cur = lambda col: pl.BlockSpec((1, tt, c), lambda b, i: (b, i, col))
    prev = lambda col: pl.BlockSpec((1, CONV_HALO, c), lambda b, i: (b, jnp.maximum(i * hb - 1, 0), col))
    nxt = lambda col: pl.BlockSpec((1, CONV_HALO, c), lambda b, i: (b, jnp.minimum((i + 1) * hb, last_hb), col))
    wspec = pl.BlockSpec(conv_w.shape, lambda b, i: (0, 0))
    return pl.pallas_call(
        body, name=name, grid=(bsz, nt),
        in_specs=[cur(0), cur(1), prev(0), prev(1), cur(0), nxt(0), cur(0), nxt(0), wspec, vec, vec, vec],
        out_specs=[pl.BlockSpec((1, tt, 2 * c), lambda b, i: (b, i, 0)), wspec, vec, vec, vec, vec],
        out_shape=[_sds((bsz, seq, 2 * c), BF16), _sds(conv_w.shape, F32)] + [_sds((1, c), F32)] * 4,
        scratch_shapes=[pltpu.VMEM((CONV_HALO + tt, c), F32), pltpu.VMEM((ext, c), F32),
                        pltpu.VMEM((tt + SUBLANES, c), F32), pltpu.VMEM((tt + SUBLANES, c), F32)],
        compiler_params=_cparams("arbitrary", "arbitrary"),
    )(proj3, proj3, proj3, proj3, cv3, cv3, dmix3, dmix3, conv_w, ln_g, ln_b, out_g)


def _ssm_discretise(a_re, a_im, log_dt):
    dt = jnp.exp(log_dt)
    zr, zi = a_re * dt, a_im * dt
    mag = jnp.exp(zr)
    ar, ai = mag * jnp.cos(zi), mag * jnp.sin(zi)
    den = a_re * a_re + a_im * a_im
    nr = ar - 1.0
    return ar, ai, (nr * a_re + ai * a_im) / den, (ai * a_re - nr * a_im) / den


def _ssm_system(a_re, a_im, log_dt, a_re_x, a_im_x, log_dt_x, bt_re, bt_im):
    ar, ai, _, _ = _ssm_discretise(a_re, a_im, log_dt)
    _, _, cr, ci = _ssm_discretise(a_re_x, a_im_x, log_dt_x)
    return ar, ai, cr * bt_re - ci * bt_im, cr * bt_im + ci * bt_re


def _ssm_prep(name, prim):
    g, p = prim[0].shape

    def body(*refs):
        pwr_ref, pwi_ref, bbr_ref, bbi_ref = refs[8:]
        ar, ai, bbr, bbi = _ssm_system(*[r[...] for r in refs[:8]])
        bbr_ref[...] = bbr
        bbi_ref[...] = bbi
        pr, pi = ar, ai
        for k in range(SUBLANES):
            pwr_ref[k] = pr
            pwi_ref[k] = pi
            pr, pi = pr * ar - pi * ai, pr * ai + pi * ar

    return pl.pallas_call(
        body, name=name,
        out_shape=[_sds((SUBLANES, g, p), F32)] * 2 + [_sds(prim[6].shape, F32)] * 2,
        compiler_params=pltpu.CompilerParams(vmem_limit_bytes=VMEM_LIMIT),
    )(*prim)


def _ssm_param_grads(name, prim, dab_r, dab_i, dbb_r, dbb_i):
    g, p = prim[0].shape
    h = prim[6].shape[0] // g

    def body(*refs):
        dar_ref, dai_ref, dbr_ref, dbi_ref = refs[8:12]
        o_ar, o_ai, o_dt, o_br, o_bi = refs[12:]
        _, vjp = jax.vjp(_ssm_system, *[r[...] for r in refs[:8]])
        ct = (jnp.sum(dar_ref[...], axis=0), jnp.sum(dai_ref[...], axis=0), dbr_ref[...], dbi_ref[...])
        d_ar, d_ai, d_dt, d_arx, d_aix, d_dtx, d_br, d_bi = vjp(ct)
        per_group = lambda t: jnp.sum(t.reshape(g, h, p), axis=1)
        o_ar[...] = d_ar + per_group(d_arx)
        o_ai[...] = d_ai + per_group(d_aix)
        o_dt[...] = d_dt + jnp.sum(per_group(d_dtx), axis=1, keepdims=True)
        o_br[...] = d_br
        o_bi[...] = d_bi

    return pl.pallas_call(
        body, name=name,
        out_shape=[_sds(prim[k].shape, F32) for k in (0, 1, 2, 6, 7)],
        compiler_params=pltpu.CompilerParams(vmem_limit_bytes=VMEM_LIMIT),
    )(*prim, dab_r, dab_i, dbb_r, dbb_i)


def _cfma(xr, xi, cr, ci, sr, si):
    return xr + (cr * sr - ci * si), xi + (cr * si + ci * sr)


def _scan_tables(pw_r, pw_i, reverse):
    gp = pw_r.shape[1] * pw_r.shape[2]
    pr, pi = pw_r.reshape(SUBLANES, gp), pw_i.reshape(SUBLANES, gp)
    if reverse:
        pi = -pi
    row = jnp.arange(SUBLANES)[:, None]
    tabs = []
    for d in (1, 2, 4):
        keep = (row < SUBLANES - d) if reverse else (row >= d)
        tabs += [jnp.where(keep, pr[d - 1][None, :], 0.0), jnp.where(keep, pi[d - 1][None, :], 0.0)]
    tabs += [pr[::-1], pi[::-1]] if reverse else [pr, pi]
    return jnp.concatenate(tabs, axis=0)


MXU_DEPTH = 256


def _bands(c, gp):
    bw = min(c, MXU_DEPTH)
    return c // bw, bw, gp * bw // c


def _band_expand(rows16, w_ref, put, c, gp):
    nb, bw, sw = _bands(c, gp)
    for s in range(nb):
        band = rows16[:, s * bw:(s + 1) * bw]
        for half in (0, gp):
            cols = pl.ds(half + s * sw, sw)
            put(cols, _dot(band, w_ref[pl.ds(s * bw, bw), cols]))


def _band_contract(get16, w_ref, c, gp):
    nb, bw, sw = _bands(c, gp)
    out = []
    for s in range(nb):
        acc = None
        for half in (0, gp):
            cols = pl.ds(half + s * sw, sw)
            t = _dot_nt(get16(cols), w_ref[pl.ds(s * bw, bw), cols])
            acc = t if acc is None else acc + t
        out.append(acc)
    return out[0] if nb == 1 else jnp.concatenate(out, axis=1)


def _band_wgrad(name, a, a_block, c, b):
    n = a.shape[0]
    gp = b.shape[1] // 2
    nb, bw, sw = _bands(c, gp)
    tk = _pick(n, TILE["mm_bytes"] // (c * a.dtype.itemsize + 2 * gp * b.dtype.itemsize), 16)

    def body(a_ref, b_ref, o_ref):
        @pl.when(pl.program_id(0) == 0)
        def _():
            o_ref[...] = jnp.zeros_like(o_ref)

        for s in range(nb):
            band = a_ref[:, s * bw:(s + 1) * bw].astype(BF16)
            for h, half in enumerate((0, gp)):
                o_ref[pl.ds(s * bw, bw), pl.ds(h * sw, sw)] += _dot_tn(
                    band, b_ref[:, pl.ds(half + s * sw, sw)].astype(BF16))

    return pl.pallas_call(
        body, name=name, grid=(n // tk,),
        in_specs=[pl.BlockSpec((tk, c), lambda k: (k, a_block)), pl.BlockSpec((tk, 2 * gp), lambda k: (k, 0))],
        out_specs=pl.BlockSpec((c, 2 * sw), lambda k: (0, 0)),
        out_shape=_sds((c, 2 * sw), F32),
        compiler_params=_cparams("arbitrary"),
    )(a, b)


def _band_diag_take(comp, half, c, gp):
    nb, bw, sw = _bands(c, gp)
    return jnp.concatenate([_block_diag_take(comp[s * bw:(s + 1) * bw, half * sw:(half + 1) * sw], bw // SSM_GROUP)
                            for s in range(nb)], axis=0)


def _scan_fwd(name, tab, proj3, u_block, bbd, cdt):
    bsz, seq, _ = proj3.shape
    c, w = bbd.shape
    gp = w // 2
    tt = _pick(seq, TILE["scan_fwd_t"], 16)
    nblk = tt // SUBLANES
    cw = _pick(gp, TILE["scan_w"], LANES)

    def body(tab_ref, u_ref, bbd_ref, cdt_ref, xs_ref, xs16_ref, y_ref, carry_ref, bu_ref):
        @pl.when(pl.program_id(1) == 0)
        def _():
            carry_ref[...] = jnp.zeros_like(carry_ref)

        def put_bu(cols, val):
            bu_ref[0, :, cols] = val

        _band_expand(u_ref[0].astype(BF16), bbd_ref, put_bu, c, gp)

        for ch in range(gp // cw):
            re, im = pl.ds(ch * cw, cw), pl.ds(gp + ch * cw, cw)

            def blk(r, carry, re=re, im=im):
                tabs = [tab_ref[pl.ds(SUBLANES * k, SUBLANES), re] for k in range(8)]
                rows = pl.ds(pl.multiple_of(r * SUBLANES, SUBLANES), SUBLANES)
                xr, xi = bu_ref[0, rows, re], bu_ref[0, rows, im]
                for j, d in enumerate((1, 2, 4)):
                    xr, xi = _cfma(xr, xi, tabs[2 * j], tabs[2 * j + 1], pltpu.roll(xr, d, 0), pltpu.roll(xi, d, 0))
                xr, xi = _cfma(xr, xi, tabs[6], tabs[7], carry[0], carry[1])
                xs_ref[0, rows, re] = xr
                xs_ref[0, rows, im] = xi
                last = SUBLANES - 1
                return (jnp.broadcast_to(xr[last:, :], xr.shape), jnp.broadcast_to(xi[last:, :], xi.shape))

            cr, ci = lax.fori_loop(0, nblk, blk, (carry_ref[:, re], carry_ref[:, im]))
            carry_ref[:, re] = cr
            carry_ref[:, im] = ci

        xs16_ref[0] = xs_ref[0].astype(BF16)
        y_ref[0] = _band_contract(lambda cols: xs16_ref[0, :, cols], cdt_ref, c, gp)

    whole = lambda arr: pl.BlockSpec(arr.shape, lambda b, t: (0, 0), pipeline_mode=pl.Buffered(1))
    wide = pl.BlockSpec((1, tt, w), lambda b, t: (b, t, 0))
    return pl.pallas_call(
        body, name=name, grid=(bsz, seq // tt),
        in_specs=[whole(tab), pl.BlockSpec((1, tt, c), lambda b, t: (b, t, u_block)), whole(bbd), whole(cdt)],
        out_specs=[wide, wide, pl.BlockSpec((1, tt, c), lambda b, t: (b, t, 0))],
        out_shape=[_sds((bsz, seq, w), F32), _sds((bsz, seq, w), BF16), _sds((bsz, seq, c), F32)],
        scratch_shapes=[pltpu.VMEM((SUBLANES, w), F32), pltpu.VMEM((1, tt, w), F32)],
        compiler_params=_cparams("arbitrary", "arbitrary"),
    )(tab, proj3, bbd, cdt)


def _scan_bwd(name, tab, dy3, xs3, du_skip3, bbd, cdt, exchange=None):
    bsz, seq, w = xs3.shape
    c = bbd.shape[0]
    gp = w // 2
    tt = _pick(seq, TILE["scan_t"], 16)
    nblk = tt // SUBLANES
    cw = _pick(gp, TILE["scan_w"], LANES)
    nt = seq // tt

    def body(tab_ref, dy_ref, xs_ref, halo_ref, skip_ref, bbd_ref, cdt_ref, lam16_ref, du_ref, dar_ref, dai_ref,
             carry_ref, g_ref, lam_ref):
        t = pl.program_id(1)

        @pl.when(t == 0)
        def _():
            carry_ref[...] = jnp.zeros_like(carry_ref)

        @pl.when((pl.program_id(0) == 0) & (t == 0))
        def _():
            dar_ref[...] = jnp.zeros_like(dar_ref)
            dai_ref[...] = jnp.zeros_like(dai_ref)

        def put_g(cols, val):
            g_ref[0, :, cols] = val

        _band_expand(dy_ref[0], cdt_ref, put_g, c, gp)

        has_prev = (t < nt - 1).astype(F32)
        row0 = lax.broadcasted_iota(jnp.int32, (SUBLANES, cw), 0) == 0
        last = SUBLANES - 1

        for ch in range(gp // cw):
            re, im = pl.ds(ch * cw, cw), pl.ds(gp + ch * cw, cw)

            def step(rows, xm1r, xm1i, state, re=re, im=im):
                tabs = [tab_ref[pl.ds(SUBLANES * k, SUBLANES), re] for k in range(8)]
                cr, ci, accr, acci = state
                lr, li = g_ref[0, rows, re], g_ref[0, rows, im]
                for j, d in enumerate((1, 2, 4)):
                    lr, li = _cfma(lr, li, tabs[2 * j], tabs[2 * j + 1],
                                   pltpu.roll(lr, SUBLANES - d, 0), pltpu.roll(li, SUBLANES - d, 0))
                lr, li = _cfma(lr, li, tabs[6], tabs[7], cr, ci)
                lam_ref[0, rows, re] = lr
                lam_ref[0, rows, im] = li
                xr, xi = xs_ref[0, rows, re], xs_ref[0, rows, im]
                xpr = jnp.where(row0, jnp.broadcast_to(xm1r[last:, :], xr.shape), pltpu.roll(xr, 1, 0))
                xpi = jnp.where(row0, jnp.broadcast_to(xm1i[last:, :], xi.shape), pltpu.roll(xi, 1, 0))
                accr = accr + (lr * xpr + li * xpi)
                acci = acci + (li * xpr - lr * xpi)
                return (jnp.broadcast_to(lr[:1, :], lr.shape), jnp.broadcast_to(li[:1, :], li.shape), accr, acci)

            def blk(k, state, re=re, im=im, step=step):
                r = nblk - 1 - k
                rows = pl.ds(pl.multiple_of(r * SUBLANES, SUBLANES), SUBLANES)
                prev = pl.ds(pl.multiple_of((r - 1) * SUBLANES, SUBLANES), SUBLANES)
                return step(rows, xs_ref[0, prev, re], xs_ref[0, prev, im], state)

            zero = jnp.zeros((SUBLANES, cw), F32)
            state = lax.fori_loop(0, nblk - 1, blk, (carry_ref[:, re], carry_ref[:, im], zero, zero))
            cr, ci, accr, acci = step(pl.ds(0, SUBLANES), has_prev * halo_ref[0, :, re], has_prev * halo_ref[0, :, im], state)
            carry_ref[:, re] = cr
            carry_ref[:, im] = ci
            dar_ref[:, re] += accr
            dai_ref[:, re] += acci

        lam16_ref[0] = lam_ref[0].astype(BF16)
        du = _band_contract(lambda cols: lam16_ref[0, :, cols], bbd_ref, c, gp)
        du_ref[0] = (du + skip_ref[0]).astype(BF16)

    tile = pl.BlockSpec((1, tt, w), lambda b, t: (b, nt - 1 - t, 0))
    thin = pl.BlockSpec((1, tt, c), lambda b, t: (b, nt - 1 - t, 0))
    halo = pl.BlockSpec((1, SUBLANES, w), lambda b, t: (b, jnp.maximum((nt - 1 - t) * nblk - 1, 0), 0))
    acc = pl.BlockSpec((SUBLANES, gp), lambda b, t: (0, 0))
    whole = lambda arr: pl.BlockSpec(arr.shape, lambda b, t: (0, 0), pipeline_mode=pl.Buffered(1))
    return _call(
        name, body, (bsz, nt), [whole(tab), thin, tile, halo, thin, whole(bbd), whole(cdt)], [tile, thin, acc, acc],
        [_sds(xs3.shape, BF16), _sds((bsz, seq, c), BF16), _sds((SUBLANES, gp), F32), _sds((SUBLANES, gp), F32)],
        (tab, dy3, xs3, xs3, du_skip3, bbd, cdt), ("arbitrary", "arbitrary"),
        scratch=[pltpu.VMEM((SUBLANES, w), F32), pltpu.VMEM((1, tt, w), F32), pltpu.VMEM((1, tt, w), F32)],
        exchange=exchange)


def _gelu_parts(y):
    inner = _GELU_K * (y + _GELU_C * y * y * y)
    t = jnp.tanh(inner)
    return 0.5 * y * (1.0 + t), t


def _ssm_out_fwd(name, cx, proj, u_block, d_skip, glu_w, glu_b, out_g, x, conv_out, w_out):
    n, c = cx.shape
    d = x.shape[1]
    tm = _pick(n, TILE["row"], 16)

    def body(cx_ref, u_ref, d_ref, gw_ref, gb_ref, og_ref, x_ref, a_ref, wo_ref, y_ref, o_ref, xo_ref):
        y = cx_ref[...] + d_ref[...] * u_ref[...]
        y_ref[...] = y
        gy, _ = _gelu_parts(y)
        z = _dot(gy.astype(BF16), gw_ref[...]) + gb_ref[...]
        _, sh = _rms_stats(gy * _sigmoid(z))
        out = (sh * og_ref[...]).astype(BF16)
        o_ref[...] = out
        xo_ref[...] = x_ref[...] + _dot(a_ref[...], wo_ref[pl.ds(0, c), :]) + _dot(out, wo_ref[pl.ds(c, c), :])

    vec = pl.BlockSpec((1, c), lambda i: (0, 0))
    row = pl.BlockSpec((tm, c), lambda i: (i, 0))
    wide = pl.BlockSpec((tm, d), lambda i: (i, 0))
    held = lambda arr: pl.BlockSpec(arr.shape, lambda i: (0, 0), pipeline_mode=pl.Buffered(1))
    return pl.pallas_call(
        body, name=name, grid=(n // tm,),
        in_specs=[row, pl.BlockSpec((tm, c), lambda i: (i, u_block)), vec, held(glu_w), vec, vec, wide, row, held(w_out)],
        out_specs=[row, row, wide],
        out_shape=[_sds((n, c), F32), _sds((n, c), BF16), _sds((n, d), F32)],
        compiler_params=_cparams("parallel"),
    )(cx, proj, d_skip, glu_w, glu_b, out_g, x, conv_out, w_out)


def _ssm_out_bwd(name, dmix, d_block, y, proj, u_block, d_skip, glu_w, glu_b, out_g):
    n, c = y.shape
    tm = _pick(n, TILE["row"], 16)

    def body(d_ref, y_ref, u_ref, dk_ref, gw_ref, gb_ref, og_ref, dy_ref, du_ref, dgw_ref, dgb_ref, dog_ref, dd_ref):
        @pl.when(pl.program_id(0) == 0)
        def _():
            for r in (dgw_ref, dgb_ref, dog_ref, dd_ref):
                r[...] = jnp.zeros_like(r)

        yv = y_ref[...]
        gy, th = _gelu_parts(yv)
        gy16 = gy.astype(BF16)
        sz = _sigmoid(_dot(gy16, gw_ref[...]) + gb_ref[...])
        r, sh = _rms_stats(gy * sz)
        dout = d_ref[...]
        dog_ref[...] += jnp.sum(dout * sh, axis=0, keepdims=True)
        dsh = dout * og_ref[...]
        ds = r * (dsh - sh * jnp.mean(dsh * sh, axis=-1, keepdims=True))
        dz = ds * gy * sz * (1.0 - sz)
        dz16 = dz.astype(BF16)
        dgb_ref[...] += jnp.sum(dz, axis=0, keepdims=True)
        dgw_ref[...] += _dot_tn(gy16, dz16)
        dgy = ds * sz + _dot_nt(dz16, gw_ref[...])
        dgelu = 0.5 * (1.0 + th) + 0.5 * yv * (1.0 - th * th) * (_GELU_K * (1.0 + 3.0 * _GELU_C * yv * yv))
        dy = dgy * dgelu
        dy_ref[...] = dy.astype(BF16)
        du_ref[...] = dy * dk_ref[...]
        dd_ref[...] += jnp.sum(dy * u_ref[...], axis=0, keepdims=True)

    vec = pl.BlockSpec((1, c), lambda i: (0, 0))
    row = pl.BlockSpec((tm, c), lambda i: (i, 0))
    mat = pl.BlockSpec(glu_w.shape, lambda i: (0, 0))
    return pl.pallas_call(
        body, name=name, grid=(n // tm,),
        in_specs=[pl.BlockSpec((tm, c), lambda i: (i, d_block)), row, pl.BlockSpec((tm, c), lambda i: (i, u_block)),
                  vec, mat, vec, vec],
        out_specs=[row, row, mat, vec, vec, vec],
        out_shape=[_sds((n, c), BF16), _sds((n, c), F32), _sds(glu_w.shape, F32)] + [_sds((1, c), F32)] * 3,
        compiler_params=_cparams("arbitrary"),
    )(dmix, y, proj, d_skip, glu_w, glu_b, out_g)


def _mesh_pos():
    return tuple(lax.axis_index(a) for a in MESH_AXES)


def _other_chips(x, y):
    return [(1 - x, y), (x, 1 - y), (1 - x, 1 - y)]


def _remote(src, dst, send_sem, recv_sem, dev):
    return pltpu.make_async_remote_copy(src_ref=src, dst_ref=dst, send_sem=send_sem, recv_sem=recv_sem,
                                        device_id=dev, device_id_type=pl.DeviceIdType.MESH)


def _hbm_call(name, body, operands, out_shapes, scratch):
    hbm = pl.BlockSpec(memory_space=pltpu.HBM)
    return pl.pallas_call(body, name=name, in_specs=[hbm] * len(operands), out_specs=[hbm] * len(out_shapes),
                          out_shape=out_shapes, scratch_shapes=scratch)(*operands)


_Exchange = collections.namedtuple("_Exchange", "operands out_shapes scratch start finish")


def _merge_plans(p, q):
    cut = len(p.operands), len(p.out_shapes), len(p.scratch)

    def both(which):
        def run(x_refs, o_refs, sems):
            getattr(p, which)(x_refs[:cut[0]], o_refs[:cut[1]], sems[:cut[2]])
            getattr(q, which)(x_refs[cut[0]:], o_refs[cut[1]:], sems[cut[2]:])
        return run

    return _Exchange(p.operands + q.operands, p.out_shapes + q.out_shapes, p.scratch + q.scratch,
                     both("start"), both("finish"))


def _run_exchange(name, plan):
    nin, nout = len(plan.operands), len(plan.out_shapes)

    def body(*refs):
        parts = refs[:nin], refs[nin:nin + nout], refs[nin + nout:]
        plan.start(*parts)
        plan.finish(*parts)

    return _hbm_call(name, body, plan.operands, plan.out_shapes, plan.scratch)


def _gather_plan(blocks):
    nop = len(blocks)

    def copies(x_refs, o_refs, sems):
        send_sems, recv_sems, local_sems = sems
        x, y, c = _mesh_pos()
        me, sibling = (x, y, c), (x, y, 1 - c)
        chips = _other_chips(x, y)

        def copy(i, k, block_of, to, src=None):
            dst = o_refs[i].at[4 * block_of[0] + 2 * block_of[1] + block_of[2]]
            return _remote(dst if src is None else src, dst, send_sems.at[i, k], recv_sems.at[i, k], to)

        own = [pltpu.make_async_copy(x_refs[i], o_refs[i].at[4 * x + 2 * y + c], local_sems.at[i]) for i in range(nop)]
        first = []
        for i in range(nop):
            first.append(copy(i, 0, me, sibling, src=x_refs[i]))
            first += [copy(i, 1 + j, me, (*chip, c), src=x_refs[i]) for j, chip in enumerate(chips)]
        return copy, own, first, me, sibling, chips, c

    def start(x_refs, o_refs, sems):
        _, own, first, *_ = copies(x_refs, o_refs, sems)
        for cp in own + first:
            cp.start()

    def finish(x_refs, o_refs, sems):
        copy, own, first, me, sibling, chips, c = copies(x_refs, o_refs, sems)
        passed = []
        for i in range(nop):
            for j, chip in enumerate(chips):
                copy(i, 1 + j, (*chip, c), me).wait_recv()
                passed.append(copy(i, 4 + j, (*chip, c), sibling))
                passed[-1].start()
        for i in range(nop):
            copy(i, 0, sibling, me).wait_recv()
            for j, chip in enumerate(chips):
                copy(i, 4 + j, (*chip, 1 - c), me).wait_recv()
        for cp in first + passed:
            cp.wait_send()
        for cp in own:
            cp.wait()

    return _Exchange(list(blocks), [_sds((N_DEV,) + b.shape, b.dtype) for b in blocks],
                     [pltpu.SemaphoreType.DMA((nop, N_DEV - 1)), pltpu.SemaphoreType.DMA((nop, N_DEV - 1)),
                      pltpu.SemaphoreType.DMA((nop,))], start, finish)


def _core_exchange_plan(grads):
    nop = len(grads)

    def copies(x_refs, o_refs, sems):
        send_sems, recv_sems = sems
        x, y, c = _mesh_pos()
        return [_remote(x_refs[i].at[2 * q + (1 - c)], o_refs[i].at[q], send_sems.at[i, q], recv_sems.at[i, q],
                        (x, y, 1 - c)) for i in range(nop) for q in range(N_DEV // 2)]

    def start(x_refs, o_refs, sems):
        for cp in copies(x_refs, o_refs, sems):
            cp.start()

    def finish(x_refs, o_refs, sems):
        for cp in copies(x_refs, o_refs, sems):
            cp.wait()

    return _Exchange(list(grads), [_sds((N_DEV // 2,) + g.shape[1:], g.dtype) for g in grads],
                     [pltpu.SemaphoreType.DMA((nop, N_DEV // 2)), pltpu.SemaphoreType.DMA((nop, N_DEV // 2))],
                     start, finish)


def _pair_sum(name, grad, other):
    nchip, _, r, c = grad.shape
    tr = _pick(r, max(SUBLANES, TILE["sum_bytes"] // (4 * c)), SUBLANES)
    core = lax.axis_index("c").astype(jnp.int32).reshape(1)

    def body(core_ref, g_ref, o_ref, s_ref):
        s_ref[0] = (g_ref[0, 0] + o_ref[0]).astype(s_ref.dtype)

    tile = pl.BlockSpec((1, tr, c), lambda q, t, core_ref: (q, t, 0))
    return pl.pallas_call(
        body, name=name,
        grid_spec=pltpu.PrefetchScalarGridSpec(
            num_scalar_prefetch=1, grid=(nchip, r // tr),
            in_specs=[pl.BlockSpec((1, 1, tr, c), lambda q, t, core_ref: (q, core_ref[0], t, 0)), tile],
            out_specs=tile),
        out_shape=_sds((nchip, r, c), BF16),
        compiler_params=_cparams("parallel", "parallel"),
    )(core, grad, other)


def _chip_exchange_plan(sums):
    nop = len(sums)

    def copies(x_refs, o_refs, sems, arriving):
        send_sems, recv_sems, local_sems = sems
        x, y, c = _mesh_pos()
        mine = 2 * x + y
        out = []
        for i in range(nop):
            for j, (px, py) in enumerate(_other_chips(x, y)):
                theirs = 2 * px + py
                src, dst = (mine, theirs) if arriving else (theirs, mine)
                out.append(_remote(x_refs[i].at[src], o_refs[i].at[dst], send_sems.at[i, j], recv_sems.at[i, j],
                                   (px, py, c)))
        if not arriving:
            out += [pltpu.make_async_copy(x_refs[i].at[mine], o_refs[i].at[mine], local_sems.at[i]) for i in range(nop)]
        return out

    def start(x_refs, o_refs, sems):
        for cp in copies(x_refs, o_refs, sems, False):
            cp.start()

    def finish(x_refs, o_refs, sems):
        for cp in copies(x_refs, o_refs, sems, True):
            cp.wait_recv()
        mine = copies(x_refs, o_refs, sems, False)
        for cp in mine[:3 * nop]:
            cp.wait_send()
        for cp in mine[3 * nop:]:
            cp.wait()

    return _Exchange(list(sums), [_sds(s.shape, s.dtype) for s in sums],
                     [pltpu.SemaphoreType.DMA((nop, 3)), pltpu.SemaphoreType.DMA((nop, 3)), pltpu.SemaphoreType.DMA((nop,))],
                     start, finish)


def _part_rows(npart, r, c):
    return _pick(r, max(SUBLANES, TILE["sum_bytes"] // (4 * npart * c)), SUBLANES)


def _sum_slots(p_ref):
    g = p_ref[0].astype(F32)
    for k in range(1, p_ref.shape[0]):
        g = g + p_ref[k].astype(F32)
    return g


def _adamw_step(g, w, m, v):
    c1 = 1.0 - ADAM_B1 ** ADAM_STEP
    c2 = 1.0 - ADAM_B2 ** ADAM_STEP
    nm = ADAM_B1 * m + (1.0 - ADAM_B1) * g
    nv = ADAM_B2 * v + (1.0 - ADAM_B2) * (g * g)
    return -ADAM_LR * ((nm / c1) / (jnp.sqrt(nv / c2) + ADAM_EPS) + ADAM_WD * w), nm, nv


def _adamw_small(name, parts, ws, ms, vs):
    nparam, nall = len(ws), len(parts)

    def body(*refs):
        p_refs = refs[:nall]
        w_refs, m_refs, v_refs = (refs[nall + k * nparam:nall + (k + 1) * nparam] for k in range(3))
        outs = refs[nall + 3 * nparam:]
        for p in range(nall):
            g = _sum_slots(p_refs[p])
            if p < nparam:
                delta, nm, nv = _adamw_step(g, w_refs[p][...], m_refs[p][...], v_refs[p][...])
                for o_ref, val in zip(outs[4 * p:4 * p + 4], (g, delta, nm, nv)):
                    o_ref[...] = val
            else:
                outs[4 * nparam + p - nparam][...] = g

    shapes = [_sds(w.shape, F32) for w in ws for _ in range(4)] + [_sds(p.shape[1:], F32) for p in parts[nparam:]]
    res = pl.pallas_call(body, name=name, out_shape=shapes,
                         compiler_params=pltpu.CompilerParams(vmem_limit_bytes=VMEM_LIMIT))(*parts, *ws, *ms, *vs)
    return [res[4 * p:4 * p + 4] for p in range(nparam)] + [[r] for r in res[4 * nparam:]]


def _adamw(name, parts, w, m, v):
    npart, r, c = parts.shape
    lead = len(w.shape) - 2
    tr = _part_rows(npart, r, c)
    at = (0,) * lead + (slice(None), slice(None))

    def body(p_ref, w_ref, m_ref, v_ref, g_ref, d_ref, nm_ref, nv_ref):
        g = _sum_slots(p_ref)
        delta, nm, nv = _adamw_step(g, w_ref[at], m_ref[at], v_ref[at])
        g_ref[at] = g
        nm_ref[at] = nm
        nv_ref[at] = nv
        d_ref[at] = delta

    row = pl.BlockSpec((1,) * lead + (tr, c), lambda i: (0,) * lead + (i, 0))
    return pl.pallas_call(
        body, name=name, grid=(r // tr,),
        in_specs=[pl.BlockSpec((npart, tr, c), lambda i: (0, i, 0)), row, row, row],
        out_specs=[row] * 4,
        out_shape=[_sds(w.shape, F32)] * 4,
        compiler_params=_cparams("parallel"),
    )(parts, w, m, v)


def _block_diag(rows_gh, groups):
    gh, p = rows_gh.shape
    own = (jnp.arange(gh)[:, None] // (gh // groups) == jnp.arange(groups)[None, :]).astype(rows_gh.dtype)
    return (own[:, :, None] * rows_gh[:, None, :]).reshape(gh, groups * p)


def _block_diag_take(dense, groups):
    gh = dense.shape[0]
    p = dense.shape[1] // groups
    own = (jnp.arange(gh)[:, None] // (gh // groups) == jnp.arange(groups)[None, :]).astype(dense.dtype)
    return jnp.sum(dense.reshape(gh, groups, p) * own[:, :, None], axis=1)


FFN1 = ("ffn1_w1", "ffn1_w3", "ffn1_w2")
MIXER = ("w_in", "ssm_glu_w", "w_out")
FFN2 = ("ffn2_w1", "ffn2_w3", "ffn2_w2")
BIG = FFN1 + MIXER + FFN2
COL_SHARDED = ("ffn1_w1", "ffn1_w3", "w_in", "ffn2_w1", "ffn2_w3", "conv_w")
SMALL = ("norm_ffn1", "norm_mix", "conv_b", "conv_ln_g", "conv_ln_b", "conv_out_g", "ssm_A_re", "ssm_A_im",
         "ssm_log_dt", "ssm_B_re", "ssm_B_im", "ssm_C_re", "ssm_C_im", "ssm_D", "ssm_glu_b", "ssm_out_g",
         "norm_ffn2", "norm_final")
WEIGHTS = ("norm_ffn1", "ffn1_w1", "ffn1_w3", "ffn1_w2", "norm_mix", "w_in", "conv_w", "conv_b", "conv_ln_g",
           "conv_ln_b", "conv_out_g", "ssm_A_re", "ssm_A_im", "ssm_log_dt", "ssm_B_re", "ssm_B_im", "ssm_C_re",
           "ssm_C_im", "ssm_D", "ssm_glu_w", "ssm_glu_b", "ssm_out_g", "w_out", "norm_ffn2", "ffn2_w1", "ffn2_w3",
           "ffn2_w2", "norm_final")


def _ffn_backward(tag, dxo, x, g, w1, w3, w2, saved, exchange=None, reduce_names=None):
    a, b, h = saved
    (da, db, hid, dxh), got = _ffn_bwd_hidden(tag + "_bwd_hidden", dxo, a, b, w2, exchange=exchange)
    dw1, dw3 = _mm_tn(tag + "_dw1", da, h), _mm_tn(tag + "_dw3", db, h)
    across = None
    if reduce_names:
        send = [_row_blocks(dw1), _row_blocks(dw3)]
        dw2, from_core = _mm_tn(tag + "_dw2", hid, dxh, exchange=_core_exchange_plan(send))
        send.append(_row_blocks(dw2))
        from_core += _run_exchange("exchange_core_" + tag, _core_exchange_plan(send[2:]))
        across = _across_chips(reduce_names, send, from_core)
    else:
        dw2 = _mm_tn(tag + "_dw2", hid, dxh)
    f = a.shape[1]
    (dx, dg), reduced = _dx_rms_bwd(tag + "_bwd_dx", [(da, f, 0, w1, f, 0), (db, f, 0, w3, f, 0)], dxo, x, g,
                                    exchange=across)
    return (dx, dg, [dw1, dw3, dw2]), got, reduced


def _row_blocks(grad):
    return grad.reshape((N_DEV, -1) + grad.shape[1:])


def _across_chips(names, send, from_core):
    return _chip_exchange_plan([_pair_sum("pair_sum_" + k, s.reshape((N_DEV // 2, 2) + s.shape[1:]), o)
                                for k, s, o in zip(names, send, from_core)])


def _reduce_in_chip(names, grads):
    send = [_row_blocks(g) for g in grads]
    return _core_exchange_plan(send), functools.partial(_across_chips, names, send)


def kernel(x, norm_ffn1, ffn1_w1, ffn1_w3, ffn1_w2, norm_mix, w_in, conv_w, conv_b, conv_ln_g, conv_ln_b, conv_out_g, ssm_A_re, ssm_A_im, ssm_log_dt, ssm_B_re, ssm_B_im, ssm_C_re, ssm_C_im, ssm_D, ssm_glu_w, ssm_glu_b, ssm_out_g, w_out, norm_ffn2, ffn2_w1, ffn2_w3, ffn2_w2, norm_final, loss_target, m_norm_ffn1, m_ffn1_w1, m_ffn1_w3, m_ffn1_w2, m_norm_mix, m_w_in, m_conv_w, m_conv_b, m_conv_ln_g, m_conv_ln_b, m_conv_out_g, m_ssm_A_re, m_ssm_A_im, m_ssm_log_dt, m_ssm_B_re, m_ssm_B_im, m_ssm_C_re, m_ssm_C_im, m_ssm_D, m_ssm_glu_w, m_ssm_glu_b, m_ssm_out_g, m_w_out, m_norm_ffn2, m_ffn2_w1, m_ffn2_w3, m_ffn2_w2, m_norm_final, v_norm_ffn1, v_ffn1_w1, v_ffn1_w3, v_ffn1_w2, v_norm_mix, v_w_in, v_conv_w, v_conv_b, v_conv_ln_g, v_conv_ln_b, v_conv_out_g, v_ssm_A_re, v_ssm_A_im, v_ssm_log_dt, v_ssm_B_re, v_ssm_B_im, v_ssm_C_re, v_ssm_C_im, v_ssm_D, v_ssm_glu_w, v_ssm_glu_b, v_ssm_out_g, v_w_out, v_norm_ffn2, v_ffn2_w1, v_ffn2_w3, v_ffn2_w2, v_norm_final):
    args = dict(locals())
    wt = {n: args[n] for n in WEIGHTS}
    mom = {n: args["m_" + n] for n in WEIGHTS}
    var = {n: args["v_" + n] for n in WEIGHTS}

    bsz, seq, d = x.shape
    n = bsz * seq
    c = conv_b.shape[-1]
    groups = c // SSM_GROUP
    gp = groups * SSM_STATE
    u_b = 2

    shard = {k: (wt[k][0].T if k in COL_SHARDED else wt[k][0]).astype(BF16) for k in BIG}
    gathered = _run_exchange("gather_weights_ffn1", _gather_plan([shard[k] for k in FFN1]))
    full = {k: g.reshape(-1, g.shape[-1]) for k, g in zip(FFN1, gathered)}
    gather_rest = _gather_plan([shard[k] for k in MIXER + FFN2] + [wt["conv_w"][0]])

    vec = lambda k: wt[k].reshape(1, -1)
    g_ffn1, g_mix, g_ffn2, g_fin = vec("norm_ffn1"), vec("norm_mix"), vec("norm_ffn2"), vec("norm_final")
    cb, lng, lnb, cog = vec("conv_b"), vec("conv_ln_g"), vec("conv_ln_b"), vec("conv_out_g")
    d_skip, glu_b, sog = vec("ssm_D"), vec("ssm_glu_b"), vec("ssm_out_g")

    a_re, a_im = wt["ssm_A_re"][0], wt["ssm_A_im"][0]
    log_dt = wt["ssm_log_dt"][0].reshape(groups, 1)
    bt_re = wt["ssm_B_re"][0].transpose(0, 2, 1).reshape(groups * SSM_GROUP, SSM_STATE)
    bt_im = wt["ssm_B_im"][0].transpose(0, 2, 1).reshape(groups * SSM_GROUP, SSM_STATE)
    c_re = wt["ssm_C_re"][0].reshape(groups * SSM_GROUP, SSM_STATE)
    c_im = wt["ssm_C_im"][0].reshape(groups * SSM_GROUP, SSM_STATE)
    per_chan = lambda t: jnp.repeat(t, SSM_GROUP, axis=0)
    ssm_prim = (a_re, a_im, log_dt, per_chan(a_re), per_chan(a_im), per_chan(jnp.broadcast_to(log_dt, a_re.shape)),
                bt_re, bt_im)
    pw_r, pw_i, bb_r, bb_i = _ssm_prep("ssm_prep", ssm_prim)
    tab_f = _scan_tables(pw_r, pw_i, False)
    tab_b = _scan_tables(pw_r, pw_i, True)
    bbd = jnp.concatenate([_block_diag(bb_r, groups), _block_diag(bb_i, groups)], axis=1).astype(BF16)
    cdt = jnp.concatenate([_block_diag(c_re, groups), -_block_diag(c_im, groups)], axis=1).astype(BF16)

    x0 = x.reshape(n, d)
    (x1, *ffn1_saved), gathered = _ffn_fwd("ffn1_fwd", x0, g_ffn1, full["ffn1_w1"], full["ffn1_w3"], full["ffn1_w2"],
                                           exchange=gather_rest)
    full.update({k: g.reshape(-1, g.shape[-1]) for k, g in zip(MIXER + FFN2, gathered)})
    conv_w_full = gathered[-1].transpose(1, 0, 2).reshape(CONV_WIDTH, c)
    conv_w_pad = jnp.pad(conv_w_full, ((0, CONV_HALO - CONV_WIDTH), (0, 0)))
    (proj,), h2 = _rms_mm("mix_in", x1, g_mix, [full["w_in"]], F32)
    proj3 = proj.reshape(bsz, seq, 3 * c)
    an3, cv3 = _conv_fwd("conv_fwd", proj3, conv_w_pad, cb, lng, lnb, cog)
    an = an3.reshape(n, c)
    xs3, xs16, cx3 = _scan_fwd("scan_fwd", tab_f, proj3, u_b, bbd, cdt)
    w_o = full["w_out"]
    y, sn, x2 = _ssm_out_fwd("ssm_out_fwd", cx3.reshape(n, c), proj, u_b, d_skip, full["ssm_glu_w"], glu_b, sog,
                             x1, an, w_o)
    (dx3, *ffn2_saved, loss_tile, d_gfin), _ = _ffn_fwd(
        "ffn2_fwd", x2, g_ffn2, full["ffn2_w1"], full["ffn2_w3"], full["ffn2_w2"],
        head=(g_fin, loss_target.reshape(n, d)))

    grads, from_chips = {}, {}
    (dx2, grads["norm_ffn2"], dws), _, _ = _ffn_backward(
        "ffn2", dx3, x2, g_ffn2, full["ffn2_w1"], full["ffn2_w3"], full["ffn2_w2"], ffn2_saved)
    in_chip, across_chips = _reduce_in_chip(FFN2, dws)

    dmix, got = _mm_nt("mix_out_bwd", dx2, w_o, exchange=in_chip)
    reduce_ffn2 = across_chips(got)
    grads["w_out"] = _mm_tn("dw_out", (an, sn), dx2)

    dy, du_skip, grads["ssm_glu_w"], grads["ssm_glu_b"], grads["ssm_out_g"], grads["ssm_D"] = _ssm_out_bwd(
        "ssm_out_bwd", dmix, 1, y, proj, u_b, d_skip, full["ssm_glu_w"], glu_b, sog)
    (lam3, du3, dab_r, dab_i), got = _scan_bwd("scan_bwd", tab_b, dy.reshape(bsz, seq, c), xs3,
                                               du_skip.reshape(bsz, seq, c), bbd, cdt, exchange=reduce_ffn2)
    from_chips.update(zip(FFN2, got))
    lam, du = lam3.reshape(n, 2 * gp), du3.reshape(n, c)
    d_bbd = _band_wgrad("ssm_dbb", proj, u_b, c, lam)
    d_cdt = _band_wgrad("ssm_dc", dy, 0, c, xs16.reshape(n, 2 * gp))
    d_are, d_aim, d_ldt, d_btr, d_bti = _ssm_param_grads(
        "ssm_param_grads", ssm_prim,
        dab_r.reshape(SUBLANES, groups, SSM_STATE), dab_i.reshape(SUBLANES, groups, SSM_STATE),
        _band_diag_take(d_bbd, 0, c, gp), _band_diag_take(d_bbd, 1, c, gp))
    grads["ssm_A_re"], grads["ssm_A_im"], grads["ssm_log_dt"] = d_are, d_aim, d_ldt
    grads["ssm_B_re"], grads["ssm_B_im"] = d_btr, d_bti
    grads["ssm_C_re"] = _band_diag_take(d_cdt, 0, c, gp)
    grads["ssm_C_im"] = -_band_diag_take(d_cdt, 1, c, gp)

    dconv3, d_cw, grads["conv_b"], grads["conv_ln_g"], grads["conv_ln_b"], grads["conv_out_g"] = _conv_bwd(
        "conv_bwd", dmix.reshape(bsz, seq, 2 * c), proj3, cv3, conv_w_pad, lng, lnb, cog)
    dconv = dconv3.reshape(n, 2 * c)
    grads["conv_w"] = d_cw[:CONV_WIDTH]
    grads["w_in"] = _mm_tn("dw_in", (dconv, du), h2)
    w_i = full["w_in"]
    in_chip, across_chips = _reduce_in_chip(MIXER, [grads[k] for k in MIXER])
    (dx1, grads["norm_mix"]), got = _dx_rms_bwd("mix_in_bwd", [(dconv, 2 * c, 0, w_i, 2 * c, 0), (du, c, 0, w_i, c, 2)],
                                                dx2, x1, g_mix, exchange=in_chip)
    reduce_mixer = across_chips(got)

    grads["norm_final"] = d_gfin
    early = tuple(k for k in SMALL if k != "norm_ffn1")
    gather_small = _gather_plan([grads[k] for k in early] + [grads["conv_w"], loss_tile])

    (dx0, grads["norm_ffn1"], _), got, reduced = _ffn_backward(
        "ffn1", dx1, x0, g_ffn1, full["ffn1_w1"], full["ffn1_w3"], full["ffn1_w2"], ffn1_saved,
        exchange=_merge_plans(reduce_mixer, gather_small), reduce_names=FFN1)
    from_chips.update(zip(MIXER, got))
    small_parts = got[len(MIXER):]
    from_chips.update(zip(FFN1, reduced))

    res = {}
    for k in BIG:
        parts = from_chips[k]
        if k in COL_SHARDED:
            swap = lambda t: jnp.swapaxes(t, -1, -2)
            res[k] = [swap(t) for t in _adamw("adamw_" + k, parts, swap(wt[k]), swap(mom[k]), swap(var[k]))]
        else:
            res[k] = _adamw("adamw_" + k, parts, wt[k], mom[k], var[k])

    def as_2d(k, t):
        if k in ("ssm_B_re", "ssm_B_im"):
            return t[0].transpose(0, 2, 1).reshape(-1, SSM_STATE)
        if k in ("ssm_C_re", "ssm_C_im"):
            return t[0].reshape(-1, SSM_STATE)
        if k in ("ssm_A_re", "ssm_A_im"):
            return t[0]
        return t.reshape(-1, 1) if k == "ssm_log_dt" else t.reshape(1, -1)

    def as_param(k, t):
        if k in ("ssm_B_re", "ssm_B_im"):
            t = t.reshape(groups, SSM_GROUP, SSM_STATE).transpose(0, 2, 1)
        return t.reshape(wt[k].shape)

    (last_part,) = _run_exchange("gather_norm_ffn1_grad", _gather_plan([grads["norm_ffn1"]]))
    order = ("norm_ffn1",) + early
    updated = _adamw_small("adamw_replicated", [last_part] + small_parts,
                           *[[as_2d(k, src[k]) for k in order] for src in (wt, mom, var)])
    res.update({k: [as_param(k, t) for t in upd] for k, upd in zip(order, updated)})
    (conv_w_grad,), (loss_sum,) = updated[-2:]
    loss = loss_sum[0, 0]
    x_pos, y_pos, c_pos = (lax.axis_index(a) for a in MESH_AXES)
    cw_cols = c // N_DEV
    own_cw = lax.dynamic_slice_in_dim(conv_w_grad, (4 * x_pos + 2 * y_pos + c_pos) * cw_cols, cw_cols, axis=1)
    res["conv_w"] = _adamw("adamw_conv_w", own_cw[None], wt["conv_w"], mom["conv_w"], var["conv_w"])

    outs = [loss, dx0.reshape(bsz, seq, d)]
    for kind in range(4):
        outs += [res[k][kind] for k in WEIGHTS]
    return tuple(outs)
```

```python
import collections
import functools
import math

import jax
import jax.numpy as jnp
from jax import lax
from jax.experimental import pallas as pl
from jax.experimental.pallas import tpu as pltpu

F32 = jnp.float32
BF16 = jnp.bfloat16

EPS = 1e-6
FFN_RES = 0.5
CONV_WIDTH = 31
CONV_HALO = 32
SSM_GROUP = 16
SSM_STATE = 64
ADAM_LR, ADAM_B1, ADAM_B2, ADAM_EPS, ADAM_WD, ADAM_STEP = 0.001, 0.9, 0.999, 1e-08, 0.01, 10

N_DEV = 8
MESH_AXES = ("x", "y", "c")
SUBLANES = 8
LANES = 128
V7X_VMEM_BYTES = 64 * 2**20
VMEM_LIMIT = V7X_VMEM_BYTES - 8 * 2**20

TILE = dict(row=1024, hid_m=512, ffn_m=512, mm_bytes=8 * 2**20, up_m=1024, up_n=256, wide_n=2048, conv_t=256,
            scan_fwd_t=512, scan_t=256, scan_w=256, sum_bytes=4 * 2**20)

_GELU_K = math.sqrt(2.0 / math.pi)
_GELU_C = 0.044715


def _pick(n, target, mult):
    best = None
    for t in range(mult, min(n, target) + 1, mult):
        if n % t == 0:
            best = t
    return n if best is None else best


def _cparams(*sem):
    return pltpu.CompilerParams(dimension_semantics=sem, vmem_limit_bytes=VMEM_LIMIT)


def _sds(shape, dtype):
    return jax.ShapeDtypeStruct(shape, dtype)


def _call(name, body, grid, in_specs, out_specs, out_shape, operands, sem, scratch=(), exchange=None):
    if exchange is None:
        res = pl.pallas_call(body, name=name, grid=grid, in_specs=list(in_specs), out_specs=list(out_specs),
                             out_shape=list(out_shape), scratch_shapes=list(scratch),
                             compiler_params=_cparams(*sem))(*operands)
        return list(res), None
    n_in, n_out, n_scr = len(in_specs), len(out_specs), len(scratch)
    n_xin, n_xout = len(exchange.operands), len(exchange.out_shapes)
    hbm = pl.BlockSpec(memory_space=pltpu.HBM)

    def with_exchange(*refs):
        cuts, pos = [], 0
        for size in (n_in, n_xin, n_out, n_xout, n_scr):
            cuts.append(refs[pos:pos + size])
            pos += size
        ins, x_in, outs, x_out, scr = cuts
        sems = refs[pos:]
        ids = [pl.program_id(axis) for axis in range(len(grid))]
        first = functools.reduce(lambda p, q: p & q, [i == 0 for i in ids])
        last = functools.reduce(lambda p, q: p & q, [i == g - 1 for i, g in zip(ids, grid)])

        @pl.when(first)
        def _():
            exchange.start(x_in, x_out, sems)

        body(*ins, *outs, *scr)

        @pl.when(last)
        def _():
            exchange.finish(x_in, x_out, sems)

    res = pl.pallas_call(
        with_exchange, name=name, grid=grid, in_specs=list(in_specs) + [hbm] * n_xin,
        out_specs=list(out_specs) + [hbm] * n_xout, out_shape=list(out_shape) + list(exchange.out_shapes),
        scratch_shapes=list(scratch) + list(exchange.scratch),
        compiler_params=_cparams(*["arbitrary"] * len(grid)))(*operands, *exchange.operands)
    return list(res[:n_out]), list(res[n_out:])


def _dot(a, b):
    return jnp.dot(a, b, preferred_element_type=F32)


def _dot_nt(a, b):
    return lax.dot_general(a, b, (((1,), (1,)), ((), ())), preferred_element_type=F32)


def _dot_tn(a, b):
    return lax.dot_general(a, b, (((0,), (0,)), ((), ())), preferred_element_type=F32)


def _sigmoid(x):
    return 0.5 * jnp.tanh(0.5 * x) + 0.5


def _rms_stats(x):
    r = lax.rsqrt(jnp.mean(x * x, axis=-1, keepdims=True) + EPS)
    return r, x * r


def _rms_bwd(x, g, dy):
    r, xh = _rms_stats(x)
    dxh = dy * g
    dx = r * (dxh - xh * jnp.mean(dxh * xh, axis=-1, keepdims=True))
    return dx, jnp.sum(dy * xh, axis=0, keepdims=True)


def _rms_mm(name, x, g, ws, out_dtype):
    n, d = x.shape
    f = ws[0].shape[0]
    nw = len(ws)
    tm, tn = _pick(n, TILE["up_m"], 16), _pick(f, TILE["wide_n"], LANES)

    def body(x_ref, g_ref, *refs):
        w_refs, o_refs, h_ref = refs[:nw], refs[nw:2 * nw], refs[2 * nw]

        @pl.when(pl.program_id(1) == 0)
        def _():
            _, xh = _rms_stats(x_ref[...])
            h_ref[...] = (xh * g_ref[...]).astype(BF16)

        h = h_ref[...]
        for w_ref, o_ref in zip(w_refs, o_refs):
            o_ref[...] = _dot_nt(h, w_ref[...]).astype(o_ref.dtype)

    outs = pl.pallas_call(
        body, name=name, grid=(n // tm, f // tn),
        in_specs=[pl.BlockSpec((tm, d), lambda i, j: (i, 0)), pl.BlockSpec((1, d), lambda i, j: (0, 0))]
        + [pl.BlockSpec((tn, d), lambda i, j: (j, 0))] * nw,
        out_specs=[pl.BlockSpec((tm, tn), lambda i, j: (i, j))] * nw + [pl.BlockSpec((tm, d), lambda i, j: (i, 0))],
        out_shape=[_sds((n, f), out_dtype)] * nw + [_sds((n, d), BF16)],
        compiler_params=_cparams("parallel", "arbitrary"),
    )(x, g, *ws)
    return outs[:nw], outs[nw]


def _ffn_fwd(name, x, g, w1t, w3t, w2, exchange=None, head=None):
    n, d = x.shape
    f = w2.shape[0]
    tm, tn = _pick(n, TILE["ffn_m"], 16), _pick(f, TILE["up_n"], LANES)

    def body(x_ref, g_ref, w1_ref, w3_ref, w2_ref, *refs):
        (gf_ref, t_ref), refs = (refs[:2], refs[2:]) if head else ((None, None), refs)
        o_ref, a_ref, b_ref, h_ref = refs[:4]
        xv = x_ref[...]
        _, xh = _rms_stats(xv)
        h = (xh * g_ref[...]).astype(BF16)
        h_ref[...] = h
        acc = None
        for c0 in range(0, f, tn):
            cols = pl.ds(c0, tn)
            av, bv = _dot_nt(h, w1_ref[cols, :]), _dot_nt(h, w3_ref[cols, :])
            a_ref[:, cols] = av.astype(BF16)
            b_ref[:, cols] = bv.astype(BF16)
            t = _dot((av * _sigmoid(av) * bv).astype(BF16), w2_ref[cols, :])
            acc = t if acc is None else acc + t
        out = xv + FFN_RES * acc
        if head is None:
            o_ref[...] = out
        else:
            loss_ref, dg_ref = refs[4:]

            @pl.when(pl.program_id(0) == 0)
            def _():
                loss_ref[...] = jnp.zeros_like(loss_ref)
                dg_ref[...] = jnp.zeros_like(dg_ref)

            dx, loss, dg = _loss_head_rows(out, gf_ref[...], t_ref[...])
            o_ref[...] = dx
            loss_ref[...] += loss
            dg_ref[...] += dg

    row = pl.BlockSpec((tm, d), lambda i: (i, 0))
    wide = pl.BlockSpec((tm, f), lambda i: (i, 0))
    vec = pl.BlockSpec((1, d), lambda i: (0, 0))
    held = pl.BlockSpec((f, d), lambda i: (0, 0), pipeline_mode=pl.Buffered(1))
    extra_in, extra_out, extra_shape = ([vec, row], [pl.BlockSpec((SUBLANES, LANES), lambda i: (0, 0)), vec],
                                        [_sds((SUBLANES, LANES), F32), _sds((1, d), F32)]) if head else ([], [], [])
    return _call(
        name, body, (n // tm,), [row, vec, held, held, held] + extra_in, [row, wide, wide, row] + extra_out,
        [_sds((n, d), F32), _sds((n, f), BF16), _sds((n, f), BF16), _sds((n, d), BF16)] + extra_shape,
        (x, g, w1t, w3t, w2) + (tuple(head) if head else ()), ("arbitrary",) if head else ("parallel",),
        exchange=exchange)


def _ffn_bwd_hidden(name, dxo, a, b, w2, exchange=None):
    n, d = dxo.shape
    f = a.shape[1]
    tm, tn = _pick(n, TILE["hid_m"], 16), _pick(f, TILE["up_n"], LANES)

    def body(dx_ref, a_ref, b_ref, w_ref, da_ref, db_ref, hid_ref, dxh_ref):
        dxh = (FFN_RES * dx_ref[...]).astype(BF16)
        dxh_ref[...] = dxh
        for c0 in range(0, f, tn):
            cols = pl.ds(c0, tn)
            dhid = _dot_nt(dxh, w_ref[cols, :])
            av, bv = a_ref[:, cols].astype(F32), b_ref[:, cols].astype(F32)
            sig = _sigmoid(av)
            silu = av * sig
            da_ref[:, cols] = (dhid * bv * (sig * (1.0 + av - silu))).astype(BF16)
            db_ref[:, cols] = (dhid * silu).astype(BF16)
            hid_ref[:, cols] = (silu * bv).astype(BF16)

    wide = pl.BlockSpec((tm, f), lambda i: (i, 0))
    row = pl.BlockSpec((tm, d), lambda i: (i, 0))
    return _call(
        name, body, (n // tm,),
        [row, wide, wide, pl.BlockSpec((f, d), lambda i: (0, 0), pipeline_mode=pl.Buffered(1))], [wide, wide, wide, row],
        [_sds((n, f), BF16)] * 3 + [_sds((n, d), BF16)], (dxo, a, b, w2), ("parallel",), exchange=exchange)


def _loss_head_rows(x, g, target):
    r, xh = _rms_stats(x)
    err = xh * g - target
    dy = err * (1.0 / x.shape[-1])
    dxh = dy * g
    dx = r * (dxh - xh * jnp.mean(dxh * xh, axis=-1, keepdims=True))
    return dx, 0.5 * jnp.sum(jnp.mean(err * err, axis=-1, keepdims=True)), jnp.sum(dy * xh, axis=0, keepdims=True)


def _dx_rms_bwd(name, pairs, dxo, x, g, exchange=None):
    n, dm = x.shape
    tm = _pick(n, TILE["ffn_m"], 16)
    npair = len(pairs)

    def body(*refs):
        d_refs, w_refs = refs[:npair], refs[npair:2 * npair]
        dxo_ref, x_ref, g_ref, dx_ref, dg_ref = refs[2 * npair:]

        @pl.when(pl.program_id(0) == 0)
        def _():
            dg_ref[...] = jnp.zeros_like(dg_ref)

        dh = None
        for d_ref, w_ref in zip(d_refs, w_refs):
            t = _dot(d_ref[...].astype(BF16), w_ref[...])
            dh = t if dh is None else dh + t
        dx, dg = _rms_bwd(x_ref[...], g_ref[...], dh)
        dx_ref[...] = dxo_ref[...] + dx
        dg_ref[...] += dg

    row = pl.BlockSpec((tm, dm), lambda i: (i, 0))
    d_specs = [pl.BlockSpec((tm, p[1]), functools.partial(lambda i, cb: (i, cb), cb=p[2])) for p in pairs]
    w_specs = [pl.BlockSpec((p[4], dm), functools.partial(lambda i, rb: (rb, 0), rb=p[5]), pipeline_mode=pl.Buffered(1))
               for p in pairs]
    return _call(
        name, body, (n // tm,), d_specs + w_specs + [row, row, pl.BlockSpec((1, dm), lambda i: (0, 0))],
        [row, pl.BlockSpec((1, dm), lambda i: (0, 0))], [_sds((n, dm), F32), _sds((1, dm), F32)],
        (*[p[0] for p in pairs], *[p[3] for p in pairs], dxo, x, g), ("arbitrary",), exchange=exchange)


def _mm_tn(name, a, b, exchange=None):
    parts = tuple(a) if isinstance(a, (tuple, list)) else (a,)
    n, mb = b.shape
    widths = [p.shape[1] for p in parts]
    tk = _pick(n, TILE["mm_bytes"] // (sum(p.shape[1] * p.dtype.itemsize for p in parts) + mb * b.dtype.itemsize), 16)

    def body(*refs):
        a_refs, b_ref, o_ref = refs[:-2], refs[-2], refs[-1]

        @pl.when(pl.program_id(0) == 0)
        def _():
            o_ref[...] = jnp.zeros_like(o_ref)

        bv = b_ref[...].astype(BF16)
        row0 = 0
        for a_ref, width in zip(a_refs, widths):
            o_ref[pl.ds(row0, width), :] += _dot_tn(a_ref[...].astype(BF16), bv)
            row0 += width

    (out,), got = _call(
        name, body, (n // tk,),
        [pl.BlockSpec((tk, w), lambda k: (k, 0)) for w in widths] + [pl.BlockSpec((tk, mb), lambda k: (k, 0))],
        [pl.BlockSpec((sum(widths), mb), lambda k: (0, 0))], [_sds((sum(widths), mb), F32)], (*parts, b), ("arbitrary",),
        exchange=exchange)
    return out if exchange is None else (out, got)


def _mm_nt(name, a, w, exchange=None):
    n, k = a.shape
    m = w.shape[0]
    tm = _pick(n, TILE["row"], 16)

    def body(a_ref, w_ref, o_ref):
        o_ref[...] = _dot_nt(a_ref[...].astype(BF16), w_ref[...])

    (out,), got = _call(
        name, body, (n // tm,),
        [pl.BlockSpec((tm, k), lambda i: (i, 0)), pl.BlockSpec((m, k), lambda i: (0, 0), pipeline_mode=pl.Buffered(1))],
        [pl.BlockSpec((tm, m), lambda i: (i, 0))], [_sds((n, m), F32)], (a, w), ("parallel",), exchange=exchange)
    return out, got


def _conv_post(c, ln_g, ln_b, out_g):
    mu = jnp.mean(c, axis=-1, keepdims=True)
    xc = c - mu
    rstd = lax.rsqrt(jnp.mean(xc * xc, axis=-1, keepdims=True) + EPS)
    nrm = xc * rstd
    l = nrm * ln_g + ln_b
    sig = _sigmoid(l)
    s = l * sig
    r, sh = _rms_stats(s)
    return sh * out_g, (rstd, nrm, l, sig, r, sh)


def _tap_groups(first):
    groups = []
    for r in range(SUBLANES):
        taps = [(s - r, s - first) for s in range(first, first + CONV_WIDTH) if s % SUBLANES == r]
        if taps:
            groups.append((r, taps))
    return groups


def _conv_taps(a_ref, w_ref, b_ref, first, rows, flip=False):
    acc = None
    for r, taps in _tap_groups(first):
        ext = rows if r == 0 else rows + SUBLANES
        part = None
        for base, k in taps:
            kk = CONV_WIDTH - 1 - k if flip else k
            t = w_ref[kk:kk + 1, :] * a_ref[pl.ds(base, ext), :]
            part = t if part is None else part + t
        if r:
            b_ref[...] = part
            part = b_ref[pl.ds(r, rows), :]
        acc = part if acc is None else acc + part
    return acc


def _conv_post_bwd(cv, dout, ln_g, ln_b, out_g):
    _, (rstd, nrm, l, sig, r, sh) = _conv_post(cv, ln_g, ln_b, out_g)
    dsh = dout * out_g
    ds = r * (dsh - sh * jnp.mean(dsh * sh, axis=-1, keepdims=True))
    dl = ds * (sig * (1.0 + l * (1.0 - sig)))
    dn = dl * ln_g
    dc = rstd * (dn - jnp.mean(dn, axis=-1, keepdims=True) - nrm * jnp.mean(dn * nrm, axis=-1, keepdims=True))
    col_sum = lambda t: jnp.sum(t, axis=0, keepdims=True)
    return dc, col_sum(dout * sh), col_sum(dl * nrm), col_sum(dl)


def _conv_fwd(name, proj3, conv_w, conv_b, ln_g, ln_b, out_g):
    bsz, seq, _ = proj3.shape
    c = conv_w.shape[1]
    tt = _pick(seq, TILE["conv_t"], CONV_HALO)
    hb = tt // CONV_HALO
    first = CONV_HALO - (CONV_WIDTH - 1)

    def body(v_ref, g_ref, vp_ref, gp_ref, w_ref, cb_ref, lg_ref, lb_ref, og_ref, o_ref, cv_ref, a_ref, b_ref):
        keep = (pl.program_id(1) > 0).astype(F32)
        a_ref[pl.ds(0, CONV_HALO), :] = keep * vp_ref[0] * _sigmoid(gp_ref[0])
        a_ref[pl.ds(CONV_HALO, tt), :] = v_ref[0] * _sigmoid(g_ref[0])
        cv = _conv_taps(a_ref, w_ref, b_ref, first, tt) + cb_ref[...]
        cv_ref[0] = cv
        out, _ = _conv_post(cv, lg_ref[...], lb_ref[...], og_ref[...])
        o_ref[0] = out.astype(BF16)

    vec = pl.BlockSpec((1, c), lambda b, i: (0, 0))
    prev = lambda col: pl.BlockSpec((1, CONV_HALO, c), lambda b, i: (b, jnp.maximum(i * hb - 1, 0), col))
    tile = pl.BlockSpec((1, tt, c), lambda b, i: (b, i, 0))
    return pl.pallas_call(
        body, name=name, grid=(bsz, seq // tt),
        in_specs=[tile, pl.BlockSpec((1, tt, c), lambda b, i: (b, i, 1)),
                  prev(0), prev(1), pl.BlockSpec(conv_w.shape, lambda b, i: (0, 0)), vec, vec, vec, vec],
        out_specs=[tile, tile],
        out_shape=[_sds((bsz, seq, c), BF16), _sds((bsz, seq, c), F32)],
        scratch_shapes=[pltpu.VMEM((CONV_HALO + tt, c), F32), pltpu.VMEM((tt + SUBLANES, c), F32)],
        compiler_params=_cparams("parallel", "arbitrary"),
    )(proj3, proj3, proj3, proj3, conv_w, conv_b, ln_g, ln_b, out_g)


def _conv_bwd(name, dmix3, proj3, cv3, conv_w, ln_g, ln_b, out_g):
    bsz, seq, _ = proj3.shape
    c = conv_w.shape[1]
    tt = _pick(seq, TILE["conv_t"], CONV_HALO)
    hb = tt // CONV_HALO
    nt = seq // tt
    last_hb = seq // CONV_HALO - 1
    ext = tt + CONV_HALO
    first = CONV_HALO - (CONV_WIDTH - 1)

    def body(v_ref, g_ref, vp_ref, gp_ref, cv_ref, cvn_ref, d_ref, dn_ref, w_ref, lg_ref, lb_ref, og_ref,
             o_ref, dw_ref, dcb_ref, dlg_ref, dlb_ref, dog_ref, a_ref, dc_ref, b_ref, ds_ref):
        i = pl.program_id(1)

        @pl.when((pl.program_id(0) == 0) & (i == 0))
        def _():
            for r in (dw_ref, dcb_ref, dlg_ref, dlb_ref, dog_ref):
                r[...] = jnp.zeros_like(r)

        keep_prev = (i > 0).astype(F32)
        keep_next = (i < nt - 1).astype(F32)
        sig_g = _sigmoid(g_ref[0])
        a_ref[pl.ds(0, CONV_HALO), :] = keep_prev * vp_ref[0] * _sigmoid(gp_ref[0])
        a_ref[pl.ds(CONV_HALO, tt), :] = v_ref[0] * sig_g

        lg, lb, og = lg_ref[...], lb_ref[...], og_ref[...]
        dc_own, d_og, d_lg, d_lb = _conv_post_bwd(cv_ref[0], d_ref[0], lg, lb, og)
        dc_next, _, _, _ = _conv_post_bwd(cvn_ref[0], keep_next * dn_ref[0], lg, lb, og)
        dog_ref[...] += d_og
        dlg_ref[...] += d_lg
        dlb_ref[...] += d_lb
        dcb_ref[...] += jnp.sum(dc_own, axis=0, keepdims=True)
        dc_ref[pl.ds(0, tt), :] = dc_own
        dc_ref[pl.ds(tt, CONV_HALO), :] = dc_next

        da = _conv_taps(dc_ref, w_ref, b_ref, 0, tt, flip=True)

        for r, taps in _tap_groups(first):
            if r:
                ds_ref[pl.ds(0, SUBLANES), :] = jnp.zeros((SUBLANES, c), F32)
                ds_ref[pl.ds(tt, SUBLANES), :] = jnp.zeros((SUBLANES, c), F32)
                ds_ref[pl.ds(r, tt), :] = dc_own
            for base, k in taps:
                prod = (ds_ref[...] * a_ref[pl.ds(base, tt + SUBLANES), :]) if r else (dc_own * a_ref[pl.ds(base, tt), :])
                dw_ref[k:k + 1, :] += jnp.sum(prod, axis=0, keepdims=True)
        val = v_ref[0]
        o_ref[0] = jnp.concatenate([da * sig_g, da * val * sig_g * (1.0 - sig_g)], axis=-1).astype(BF16)

    vec = pl.BlockSpec((1, c), lambda b, i: (0, 0))
    cur = lambda col: pl.BlockSpec((1, tt, c), lambda b, i: (b, i, col))
    prev = lambda col: pl.BlockSpec((1, CONV_HALO, c), lambda b, i: (b, jnp.maximum(i * hb - 1, 0), col))
    nxt = lambda col: pl.BlockSpec((1, CONV_HALO, c), lambda b, i: (b, jnp.minimum((i + 1) * hb, last_hb), col))
    wspec = pl.BlockSpec(conv_w.shape, lambda b, i: (0, 0))
    return pl.pallas_call(
        body, name=name, grid=(bsz, nt),
        in_specs=[cur(0), cur(1), prev(0), prev(1), cur(0), nxt(0), cur(0), nxt(0), wspec, vec, vec, vec],
        out_specs=[pl.BlockSpec((1, tt, 2 * c), lambda b, i: (b, i, 0)), wspec, vec, vec, vec, vec],
        out_shape=[_sds((bsz, seq, 2 * c), BF16), _sds(conv_w.shape, F32)] + [_sds((1, c), F32)] * 4,
        scratch_shapes=[pltpu.VMEM((CONV_HALO + tt, c), F32), pltpu.VMEM((ext, c), F32),
                        pltpu.VMEM((tt + SUBLANES, c), F32), pltpu.VMEM((tt + SUBLANES, c), F32)],
        compiler_params=_cparams("arbitrary", "arbitrary"),
    )(proj3, proj3, proj3, proj3, cv3, cv3, dmix3, dmix3, conv_w, ln_g, ln_b, out_g)


def _ssm_discretise(a_re, a_im, log_dt):
    dt = jnp.exp(log_dt)
    zr, zi = a_re * dt, a_im * dt
    mag = jnp.exp(zr)
    ar, ai = mag * jnp.cos(zi), mag * jnp.sin(zi)
    den = a_re * a_re + a_im * a_im
    nr = ar - 1.0
    return ar, ai, (nr * a_re + ai * a_im) / den, (ai * a_re - nr * a_im) / den


def _ssm_system(a_re, a_im, log_dt, a_re_x, a_im_x, log_dt_x, bt_re, bt_im):
    ar, ai, _, _ = _ssm_discretise(a_re, a_im, log_dt)
    _, _, cr, ci = _ssm_discretise(a_re_x, a_im_x, log_dt_x)
    return ar, ai, cr * bt_re - ci * bt_im, cr * bt_im + ci * bt_re


def _ssm_prep(name, prim):
    g, p = prim[0].shape

    def body(*refs):
        pwr_ref, pwi_ref, bbr_ref, bbi_ref = refs[8:]
        ar, ai, bbr, bbi = _ssm_system(*[r[...] for r in refs[:8]])
        bbr_ref[...] = bbr
        bbi_ref[...] = bbi
        pr, pi = ar, ai
        for k in range(SUBLANES):
            pwr_ref[k] = pr
            pwi_ref[k] = pi
            pr, pi = pr * ar - pi * ai, pr * ai + pi * ar

    return pl.pallas_call(
        body, name=name,
        out_shape=[_sds((SUBLANES, g, p), F32)] * 2 + [_sds(prim[6].shape, F32)] * 2,
        compiler_params=pltpu.CompilerParams(vmem_limit_bytes=VMEM_LIMIT),
    )(*prim)


def _ssm_param_grads(name, prim, dab_r, dab_i, dbb_r, dbb_i):
    g, p = prim[0].shape
    h = prim[6].shape[0] // g

    def body(*refs):
        dar_ref, dai_ref, dbr_ref, dbi_ref = refs[8:12]
        o_ar, o_ai, o_dt, o_br, o_bi = refs[12:]
        _, vjp = jax.vjp(_ssm_system, *[r[...] for r in refs[:8]])
        ct = (jnp.sum(dar_ref[...], axis=0), jnp.sum(dai_ref[...], axis=0), dbr_ref[...], dbi_ref[...])
        d_ar, d_ai, d_dt, d_arx, d_aix, d_dtx, d_br, d_bi = vjp(ct)
        per_group = lambda t: jnp.sum(t.reshape(g, h, p), axis=1)
        o_ar[...] = d_ar + per_group(d_arx)
        o_ai[...] = d_ai + per_group(d_aix)
        o_dt[...] = d_dt + jnp.sum(per_group(d_dtx), axis=1, keepdims=True)
        o_br[...] = d_br
        o_bi[...] = d_bi

    return pl.pallas_call(
        body, name=name,
        out_shape=[_sds(prim[k].shape, F32) for k in (0, 1, 2, 6, 7)],
        compiler_params=pltpu.CompilerParams(vmem_limit_bytes=VMEM_LIMIT),
    )(*prim, dab_r, dab_i, dbb_r, dbb_i)


def _cfma(xr, xi, cr, ci, sr, si):
    return xr + (cr * sr - ci * si), xi + (cr * si + ci * sr)


def _scan_tables(pw_r, pw_i, reverse):
    gp = pw_r.shape[1] * pw_r.shape[2]
    pr, pi = pw_r.reshape(SUBLANES, gp), pw_i.reshape(SUBLANES, gp)
    if reverse:
        pi = -pi
    row = jnp.arange(SUBLANES)[:, None]
    tabs = []
    for d in (1, 2, 4):
        keep = (row < SUBLANES - d) if reverse else (row >= d)
        tabs += [jnp.where(keep, pr[d - 1][None, :], 0.0), jnp.where(keep, pi[d - 1][None, :], 0.0)]
    tabs += [pr[::-1], pi[::-1]] if reverse else [pr, pi]
    return jnp.concatenate(tabs, axis=0)


MXU_DEPTH = 256


def _bands(c, gp):
    bw = min(c, MXU_DEPTH)
    return c // bw, bw, gp * bw // c


def _band_expand(rows16, w_ref, put, c, gp):
    nb, bw, sw = _bands(c, gp)
    for s in range(nb):
        band = rows16[:, s * bw:(s + 1) * bw]
        for half in (0, gp):
            cols = pl.ds(half + s * sw, sw)
            put(cols, _dot(band, w_ref[pl.ds(s * bw, bw), cols]))


def _band_contract(get16, w_ref, c, gp):
    nb, bw, sw = _bands(c, gp)
    out = []
    for s in range(nb):
        acc = None
        for half in (0, gp):
            cols = pl.ds(half + s * sw, sw)
            t = _dot_nt(get16(cols), w_ref[pl.ds(s * bw, bw), cols])
            acc = t if acc is None else acc + t
        out.append(acc)
    return out[0] if nb == 1 else jnp.concatenate(out, axis=1)


def _band_wgrad(name, a, a_block, c, b):
    n = a.shape[0]
    gp = b.shape[1] // 2
    nb, bw, sw = _bands(c, gp)
    tk = _pick(n, TILE["mm_bytes"] // (c * a.dtype.itemsize + 2 * gp * b.dtype.itemsize), 16)

    def body(a_ref, b_ref, o_ref):
        @pl.when(pl.program_id(0) == 0)
        def _():
            o_ref[...] = jnp.zeros_like(o_ref)

        for s in range(nb):
            band = a_ref[:, s * bw:(s + 1) * bw].astype(BF16)
            for h, half in enumerate((0, gp)):
                o_ref[pl.ds(s * bw, bw), pl.ds(h * sw, sw)] += _dot_tn(
                    band, b_ref[:, pl.ds(half + s * sw, sw)].astype(BF16))

    return pl.pallas_call(
        body, name=name, grid=(n // tk,),
        in_specs=[pl.BlockSpec((tk, c), lambda k: (k, a_block)), pl.BlockSpec((tk, 2 * gp), lambda k: (k, 0))],
        out_specs=pl.BlockSpec((c, 2 * sw), lambda k: (0, 0)),
        out_shape=_sds((c, 2 * sw), F32),
        compiler_params=_cparams("arbitrary"),
    )(a, b)


def _band_diag_take(comp, half, c, gp):
    nb, bw, sw = _bands(c, gp)
    return jnp.concatenate([_block_diag_take(comp[s * bw:(s + 1) * bw, half * sw:(half + 1) * sw], bw // SSM_GROUP)
                            for s in range(nb)], axis=0)


def _scan_fwd(name, tab, proj3, u_block, bbd, cdt):
    bsz, seq, _ = proj3.shape
    c, w = bbd.shape
    gp = w // 2
    tt = _pick(seq, TILE["scan_fwd_t"], 16)
    nblk = tt // SUBLANES
    cw = _pick(gp, TILE["scan_w"], LANES)

    def body(tab_ref, u_ref, bbd_ref, cdt_ref, xs_ref, xs16_ref, y_ref, carry_ref, bu_ref):
        @pl.when(pl.program_id(1) == 0)
        def _():
            carry_ref[...] = jnp.zeros_like(carry_ref)

        def put_bu(cols, val):
            bu_ref[0, :, cols] = val

        _band_expand(u_ref[0].astype(BF16), bbd_ref, put_bu, c, gp)

        for ch in range(gp // cw):
            re, im = pl.ds(ch * cw, cw), pl.ds(gp + ch * cw, cw)

            def blk(r, carry, re=re, im=im):
                tabs = [tab_ref[pl.ds(SUBLANES * k, SUBLANES), re] for k in range(8)]
                rows = pl.ds(pl.multiple_of(r * SUBLANES, SUBLANES), SUBLANES)
                xr, xi = bu_ref[0, rows, re], bu_ref[0, rows, im]
                for j, d in enumerate((1, 2, 4)):
                    xr, xi = _cfma(xr, xi, tabs[2 * j], tabs[2 * j + 1], pltpu.roll(xr, d, 0), pltpu.roll(xi, d, 0))
                xr, xi = _cfma(xr, xi, tabs[6], tabs[7], carry[0], carry[1])
                xs_ref[0, rows, re] = xr
                xs_ref[0, rows, im] = xi
                last = SUBLANES - 1
                return (jnp.broadcast_to(xr[last:, :], xr.shape), jnp.broadcast_to(xi[last:, :], xi.shape))

            cr, ci = lax.fori_loop(0, nblk, blk, (carry_ref[:, re], carry_ref[:, im]))
            carry_ref[:, re] = cr
            carry_ref[:, im] = ci

        xs16_ref[0] = xs_ref[0].astype(BF16)
        y_ref[0] = _band_contract(lambda cols: xs16_ref[0, :, cols], cdt_ref, c, gp)

    whole = lambda arr: pl.BlockSpec(arr.shape, lambda b, t: (0, 0), pipeline_mode=pl.Buffered(1))
    wide = pl.BlockSpec((1, tt, w), lambda b, t: (b, t, 0))
    return pl.pallas_call(
        body, name=name, grid=(bsz, seq // tt),
        in_specs=[whole(tab), pl.BlockSpec((1, tt, c), lambda b, t: (b, t, u_block)), whole(bbd), whole(cdt)],
        out_specs=[wide, wide, pl.BlockSpec((1, tt, c), lambda b, t: (b, t, 0))],
        out_shape=[_sds((bsz, seq, w), F32), _sds((bsz, seq, w), BF16), _sds((bsz, seq, c), F32)],
        scratch_shapes=[pltpu.VMEM((SUBLANES, w), F32), pltpu.VMEM((1, tt, w), F32)],
        compiler_params=_cparams("arbitrary", "arbitrary"),
    )(tab, proj3, bbd, cdt)


def _scan_bwd(name, tab, dy3, xs3, du_skip3, bbd, cdt, exchange=None):
    bsz, seq, w = xs3.shape
    c = bbd.shape[0]
    gp = w // 2
    tt = _pick(seq, TILE["scan_t"], 16)
    nblk = tt // SUBLANES
    cw = _pick(gp, TILE["scan_w"], LANES)
    nt = seq // tt

    def body(tab_ref, dy_ref, xs_ref, halo_ref, skip_ref, bbd_ref, cdt_ref, lam16_ref, du_ref, dar_ref, dai_ref,
             carry_ref, g_ref, lam_ref):
        t = pl.program_id(1)

        @pl.when(t == 0)
        def _():
            carry_ref[...] = jnp.zeros_like(carry_ref)

        @pl.when((pl.program_id(0) == 0) & (t == 0))
        def _():
            dar_ref[...] = jnp.zeros_like(dar_ref)
            dai_ref[...] = jnp.zeros_like(dai_ref)

        def put_g(cols, val):
            g_ref[0, :, cols] = val

        _band_expand(dy_ref[0], cdt_ref, put_g, c, gp)

        has_prev = (t < nt - 1).astype(F32)
        row0 = lax.broadcasted_iota(jnp.int32, (SUBLANES, cw), 0) == 0
        last = SUBLANES - 1

        for ch in range(gp // cw):
            re, im = pl.ds(ch * cw, cw), pl.ds(gp + ch * cw, cw)

            def step(rows, xm1r, xm1i, state, re=re, im=im):
                tabs = [tab_ref[pl.ds(SUBLANES * k, SUBLANES), re] for k in range(8)]
                cr, ci, accr, acci = state
                lr, li = g_ref[0, rows, re], g_ref[0, rows, im]
                for j, d in enumerate((1, 2, 4)):
                    lr, li = _cfma(lr, li, tabs[2 * j], tabs[2 * j + 1],
                                   pltpu.roll(lr, SUBLANES - d, 0), pltpu.roll(li, SUBLANES - d, 0))
                lr, li = _cfma(lr, li, tabs[6], tabs[7], cr, ci)
                lam_ref[0, rows, re] = lr
                lam_ref[0, rows, im] = li
                xr, xi = xs_ref[0, rows, re], xs_ref[0, rows, im]
                xpr = jnp.where(row0, jnp.broadcast_to(xm1r[last:, :], xr.shape), pltpu.roll(xr, 1, 0))
                xpi = jnp.where(row0, jnp.broadcast_to(xm1i[last:, :], xi.shape), pltpu.roll(xi, 1, 0))
                accr = accr + (lr * xpr + li * xpi)
                acci = acci + (li * xpr - lr * xpi)
                return (jnp.broadcast_to(lr[:1, :], lr.shape), jnp.broadcast_to(li[:1, :], li.shape), accr, acci)

            def blk(k, state, re=re, im=im, step=step):
                r = nblk - 1 - k
                rows = pl.ds(pl.multiple_of(r * SUBLANES, SUBLANES), SUBLANES)
                prev = pl.ds(pl.multiple_of((r - 1) * SUBLANES, SUBLANES), SUBLANES)
                return step(rows, xs_ref[0, prev, re], xs_ref[0, prev, im], state)

            zero = jnp.zeros((SUBLANES, cw), F32)
            state = lax.fori_loop(0, nblk - 1, blk, (carry_ref[:, re], carry_ref[:, im], zero, zero))
            cr, ci, accr, acci = step(pl.ds(0, SUBLANES), has_prev * halo_ref[0, :, re], has_prev * halo_ref[0, :, im], state)
            carry_ref[:, re] = cr
            carry_ref[:, im] = ci
            dar_ref[:, re] += accr
            dai_ref[:, re] += acci

        lam16_ref[0] = lam_ref[0].astype(BF16)
        du = _band_contract(lambda cols: lam16_ref[0, :, cols], bbd_ref, c, gp)
        du_ref[0] = (du + skip_ref[0]).astype(BF16)

    tile = pl.BlockSpec((1, tt, w), lambda b, t: (b, nt - 1 - t, 0))
    thin = pl.BlockSpec((1, tt, c), lambda b, t: (b, nt - 1 - t, 0))
    halo = pl.BlockSpec((1, SUBLANES, w), lambda b, t: (b, jnp.maximum((nt - 1 - t) * nblk - 1, 0), 0))
    acc = pl.BlockSpec((SUBLANES, gp), lambda b, t: (0, 0))
    whole = lambda arr: pl.BlockSpec(arr.shape, lambda b, t: (0, 0), pipeline_mode=pl.Buffered(1))
    return _call(
        name, body, (bsz, nt), [whole(tab), thin, tile, halo, thin, whole(bbd), whole(cdt)], [tile, thin, acc, acc],
        [_sds(xs3.shape, BF16), _sds((bsz, seq, c), BF16), _sds((SUBLANES, gp), F32), _sds((SUBLANES, gp), F32)],
        (tab, dy3, xs3, xs3, du_skip3, bbd, cdt), ("arbitrary", "arbitrary"),
        scratch=[pltpu.VMEM((SUBLANES, w), F32), pltpu.VMEM((1, tt, w), F32), pltpu.VMEM((1, tt, w), F32)],
        exchange=exchange)


def _gelu_parts(y):
    inner = _GELU_K * (y + _GELU_C * y * y * y)
    t = jnp.tanh(inner)
    return 0.5 * y * (1.0 + t), t


def _ssm_out_fwd(name, cx, proj, u_block, d_skip, glu_w, glu_b, out_g, x, conv_out, w_out):
    n, c = cx.shape
    d = x.shape[1]
    tm = _pick(n, TILE["row"], 16)

    def body(cx_ref, u_ref, d_ref, gw_ref, gb_ref, og_ref, x_ref, a_ref, wo_ref, y_ref, o_ref, xo_ref):
        y = cx_ref[...] + d_ref[...] * u_ref[...]
        y_ref[...] = y
        gy, _ = _gelu_parts(y)
        z = _dot(gy.astype(BF16), gw_ref[...]) + gb_ref[...]
        _, sh = _rms_stats(gy * _sigmoid(z))
        out = (sh * og_ref[...]).astype(BF16)
        o_ref[...] = out
        xo_ref[...] = x_ref[...] + _dot(a_ref[...], wo_ref[pl.ds(0, c), :]) + _dot(out, wo_ref[pl.ds(c, c), :])

    vec = pl.BlockSpec((1, c), lambda i: (0, 0))
    row = pl.BlockSpec((tm, c), lambda i: (i, 0))
    wide = pl.BlockSpec((tm, d), lambda i: (i, 0))
    held = lambda arr: pl.BlockSpec(arr.shape, lambda i: (0, 0), pipeline_mode=pl.Buffered(1))
    return pl.pallas_call(
        body, name=name, grid=(n // tm,),
        in_specs=[row, pl.BlockSpec((tm, c), lambda i: (i, u_block)), vec, held(glu_w), vec, vec, wide, row, held(w_out)],
        out_specs=[row, row, wide],
        out_shape=[_sds((n, c), F32), _sds((n, c), BF16), _sds((n, d), F32)],
        compiler_params=_cparams("parallel"),
    )(cx, proj, d_skip, glu_w, glu_b, out_g, x, conv_out, w_out)


def _ssm_out_bwd(name, dmix, d_block, y, proj, u_block, d_skip, glu_w, glu_b, out_g):
    n, c = y.shape
    tm = _pick(n, TILE["row"], 16)

    def body(d_ref, y_ref, u_ref, dk_ref, gw_ref, gb_ref, og_ref, dy_ref, du_ref, dgw_ref, dgb_ref, dog_ref, dd_ref):
        @pl.when(pl.program_id(0) == 0)
        def _():
            for r in (dgw_ref, dgb_ref, dog_ref, dd_ref):
                r[...] = jnp.zeros_like(r)

        yv = y_ref[...]
        gy, th = _gelu_parts(yv)
        gy16 = gy.astype(BF16)
        sz = _sigmoid(_dot(gy16, gw_ref[...]) + gb_ref[...])
        r, sh = _rms_stats(gy * sz)
        dout = d_ref[...]
        dog_ref[...] += jnp.sum(dout * sh, axis=0, keepdims=True)
        dsh = dout * og_ref[...]
        ds = r * (dsh - sh * jnp.mean(dsh * sh, axis=-1, keepdims=True))
        dz = ds * gy * sz * (1.0 - sz)
        dz16 = dz.astype(BF16)
        dgb_ref[...] += jnp.sum(dz, axis=0, keepdims=True)
        dgw_ref[...] += _dot_tn(gy16, dz16)
        dgy = ds * sz + _dot_nt(dz16, gw_ref[...])
        dgelu = 0.5 * (1.0 + th) + 0.5 * yv * (1.0 - th * th) * (_GELU_K * (1.0 + 3.0 * _GELU_C * yv * yv))
        dy = dgy * dgelu
        dy_ref[...] = dy.astype(BF16)
        du_ref[...] = dy * dk_ref[...]
        dd_ref[...] += jnp.sum(dy * u_ref[...], axis=0, keepdims=True)

    vec = pl.BlockSpec((1, c), lambda i: (0, 0))
    row = pl.BlockSpec((tm, c), lambda i: (i, 0))
    mat = pl.BlockSpec(glu_w.shape, lambda i: (0, 0))
    return pl.pallas_call(
        body, name=name, grid=(n // tm,),
        in_specs=[pl.BlockSpec((tm, c), lambda i: (i, d_block)), row, pl.BlockSpec((tm, c), lambda i: (i, u_block)),
                  vec, mat, vec, vec],
        out_specs=[row, row, mat, vec, vec, vec],
        out_shape=[_sds((n, c), BF16), _sds((n, c), F32), _sds(glu_w.shape, F32)] + [_sds((1, c), F32)] * 3,
        compiler_params=_cparams("arbitrary"),
    )(dmix, y, proj, d_skip, glu_w, glu_b, out_g)


def _mesh_pos():
    return tuple(lax.axis_index(a) for a in MESH_AXES)


def _other_chips(x, y):
    return [(1 - x, y), (x, 1 - y), (1 - x, 1 - y)]


def _remote(src, dst, send_sem, recv_sem, dev):
    return pltpu.make_async_remote_copy(src_ref=src, dst_ref=dst, send_sem=send_sem, recv_sem=recv_sem,
                                        device_id=dev, device_id_type=pl.DeviceIdType.MESH)


def _hbm_call(name, body, operands, out_shapes, scratch):
    hbm = pl.BlockSpec(memory_space=pltpu.HBM)
    return pl.pallas_call(body, name=name, in_specs=[hbm] * len(operands), out_specs=[hbm] * len(out_shapes),
                          out_shape=out_shapes, scratch_shapes=scratch)(*operands)


_Exchange = collections.namedtuple("_Exchange", "operands out_shapes scratch start finish")


def _merge_plans(p, q):
    cut = len(p.operands), len(p.out_shapes), len(p.scratch)

    def both(which):
        def run(x_refs, o_refs, sems):
            getattr(p, which)(x_refs[:cut[0]], o_refs[:cut[1]], sems[:cut[2]])
            getattr(q, which)(x_refs[cut[0]:], o_refs[cut[1]:], sems[cut[2]:])
        return run

    return _Exchange(p.operands + q.operands, p.out_shapes + q.out_shapes, p.scratch + q.scratch,
                     both("start"), both("finish"))


def _run_exchange(name, plan):
    nin, nout = len(plan.operands), len(plan.out_shapes)

    def body(*refs):
        parts = refs[:nin], refs[nin:nin + nout], refs[nin + nout:]
        plan.start(*parts)
        plan.finish(*parts)

    return _hbm_call(name, body, plan.operands, plan.out_shapes, plan.scratch)


def _gather_plan(blocks):
    nop = len(blocks)

    def copies(x_refs, o_refs, sems):
        send_sems, recv_sems, local_sems = sems
        x, y, c = _mesh_pos()
        me, sibling = (x, y, c), (x, y, 1 - c)
        chips = _other_chips(x, y)

        def copy(i, k, block_of, to, src=None):
            dst = o_refs[i].at[4 * block_of[0] + 2 * block_of[1] + block_of[2]]
            return _remote(dst if src is None else src, dst, send_sems.at[i, k], recv_sems.at[i, k], to)

        own = [pltpu.make_async_copy(x_refs[i], o_refs[i].at[4 * x + 2 * y + c], local_sems.at[i]) for i in range(nop)]
        first = []
        for i in range(nop):
            first.append(copy(i, 0, me, sibling, src=x_refs[i]))
            first += [copy(i, 1 + j, me, (*chip, c), src=x_refs[i]) for j, chip in enumerate(chips)]
        return copy, own, first, me, sibling, chips, c

    def start(x_refs, o_refs, sems):
        _, own, first, *_ = copies(x_refs, o_refs, sems)
        for cp in own + first:
            cp.start()

    def finish(x_refs, o_refs, sems):
        copy, own, first, me, sibling, chips, c = copies(x_refs, o_refs, sems)
        passed = []
        for i in range(nop):
            for j, chip in enumerate(chips):
                copy(i, 1 + j, (*chip, c), me).wait_recv()
                passed.append(copy(i, 4 + j, (*chip, c), sibling))
                passed[-1].start()
        for i in range(nop):
            copy(i, 0, sibling, me).wait_recv()
            for j, chip in enumerate(chips):
                copy(i, 4 + j, (*chip, 1 - c), me).wait_recv()
        for cp in first + passed:
            cp.wait_send()
        for cp in own:
            cp.wait()

    return _Exchange(list(blocks), [_sds((N_DEV,) + b.shape, b.dtype) for b in blocks],
                     [pltpu.SemaphoreType.DMA((nop, N_DEV - 1)), pltpu.SemaphoreType.DMA((nop, N_DEV - 1)),
                      pltpu.SemaphoreType.DMA((nop,))], start, finish)


def _core_exchange_plan(grads):
    nop = len(grads)

    def copies(x_refs, o_refs, sems):
        send_sems, recv_sems = sems
        x, y, c = _mesh_pos()
        return [_remote(x_refs[i].at[2 * q + (1 - c)], o_refs[i].at[q], send_sems.at[i, q], recv_sems.at[i, q],
                        (x, y, 1 - c)) for i in range(nop) for q in range(N_DEV // 2)]

    def start(x_refs, o_refs, sems):
        for cp in copies(x_refs, o_refs, sems):
            cp.start()

    def finish(x_refs, o_refs, sems):
        for cp in copies(x_refs, o_refs, sems):
            cp.wait()

    return _Exchange(list(grads), [_sds((N_DEV // 2,) + g.shape[1:], g.dtype) for g in grads],
                     [pltpu.SemaphoreType.DMA((nop, N_DEV // 2)), pltpu.SemaphoreType.DMA((nop, N_DEV // 2))],
                     start, finish)


def _pair_sum(name, grad, other):
    nchip, _, r, c = grad.shape
    tr = _pick(r, max(SUBLANES, TILE["sum_bytes"] // (4 * c)), SUBLANES)
    core = lax.axis_index("c").astype(jnp.int32).reshape(1)

    def body(core_ref, g_ref, o_ref, s_ref):
        s_ref[0] = (g_ref[0, 0] + o_ref[0]).astype(s_ref.dtype)

    tile = pl.BlockSpec((1, tr, c), lambda q, t, core_ref: (q, t, 0))
    return pl.pallas_call(
        body, name=name,
        grid_spec=pltpu.PrefetchScalarGridSpec(
            num_scalar_prefetch=1, grid=(nchip, r // tr),
            in_specs=[pl.BlockSpec((1, 1, tr, c), lambda q, t, core_ref: (q, core_ref[0], t, 0)), tile],
            out_specs=tile),
        out_shape=_sds((nchip, r, c), BF16),
        compiler_params=_cparams("parallel", "parallel"),
    )(core, grad, other)


def _chip_exchange_plan(sums):
    nop = len(sums)

    def copies(x_refs, o_refs, sems, arriving):
        send_sems, recv_sems, local_sems = sems
        x, y, c = _mesh_pos()
        mine = 2 * x + y
        out = []
        for i in range(nop):
            for j, (px, py) in enumerate(_other_chips(x, y)):
                theirs = 2 * px + py
                src, dst = (mine, theirs) if arriving else (theirs, mine)
                out.append(_remote(x_refs[i].at[src], o_refs[i].at[dst], send_sems.at[i, j], recv_sems.at[i, j],
                                   (px, py, c)))
        if not arriving:
            out += [pltpu.make_async_copy(x_refs[i].at[mine], o_refs[i].at[mine], local_sems.at[i]) for i in range(nop)]
        return out

    def start(x_refs, o_refs, sems):
        for cp in copies(x_refs, o_refs, sems, False):
            cp.start()

    def finish(x_refs, o_refs, sems):
        for cp in copies(x_refs, o_refs, sems, True):
            cp.wait_recv()
        mine = copies(x_refs, o_refs, sems, False)
        for cp in mine[:3 * nop]:
            cp.wait_send()
        for cp in mine[3 * nop:]:
            cp.wait()

    return _Exchange(list(sums), [_sds(s.shape, s.dtype) for s in sums],
                     [pltpu.SemaphoreType.DMA((nop, 3)), pltpu.SemaphoreType.DMA((nop, 3)), pltpu.SemaphoreType.DMA((nop,))],
                     start, finish)


def _part_rows(npart, r, c):
    return _pick(r, max(SUBLANES, TILE["sum_bytes"] // (4 * npart * c)), SUBLANES)


def _sum_slots(p_ref):
    g = p_ref[0].astype(F32)
    for k in range(1, p_ref.shape[0]):
        g = g + p_ref[k].astype(F32)
    return g


def _adamw_step(g, w, m, v):
    c1 = 1.0 - ADAM_B1 ** ADAM_STEP
    c2 = 1.0 - ADAM_B2 ** ADAM_STEP
    nm = ADAM_B1 * m + (1.0 - ADAM_B1) * g
    nv = ADAM_B2 * v + (1.0 - ADAM_B2) * (g * g)
    return -ADAM_LR * ((nm / c1) / (jnp.sqrt(nv / c2) + ADAM_EPS) + ADAM_WD * w), nm, nv


def _adamw_small(name, parts, ws, ms, vs):
    nparam, nall = len(ws), len(parts)

    def body(*refs):
        p_refs = refs[:nall]
        w_refs, m_refs, v_refs = (refs[nall + k * nparam:nall + (k + 1) * nparam] for k in range(3))
        outs = refs[nall + 3 * nparam:]
        for p in range(nall):
            g = _sum_slots(p_refs[p])
            if p < nparam:
                delta, nm, nv = _adamw_step(g, w_refs[p][...], m_refs[p][...], v_refs[p][...])
                for o_ref, val in zip(outs[4 * p:4 * p + 4], (g, delta, nm, nv)):
                    o_ref[...] = val
            else:
                outs[4 * nparam + p - nparam][...] = g

    shapes = [_sds(w.shape, F32) for w in ws for _ in range(4)] + [_sds(p.shape[1:], F32) for p in parts[nparam:]]
    res = pl.pallas_call(body, name=name, out_shape=shapes,
                         compiler_params=pltpu.CompilerParams(vmem_limit_bytes=VMEM_LIMIT))(*parts, *ws, *ms, *vs)
    return [res[4 * p:4 * p + 4] for p in range(nparam)] + [[r] for r in res[4 * nparam:]]


def _adamw(name, parts, w, m, v):
    npart, r, c = parts.shape
    lead = len(w.shape) - 2
    tr = _part_rows(npart, r, c)
    at = (0,) * lead + (slice(None), slice(None))

    def body(p_ref, w_ref, m_ref, v_ref, g_ref, d_ref, nm_ref, nv_ref):
        g = _sum_slots(p_ref)
        delta, nm, nv = _adamw_step(g, w_ref[at], m_ref[at], v_ref[at])
        g_ref[at] = g
        nm_ref[at] = nm
        nv_ref[at] = nv
        d_ref[at] = delta

    row = pl.BlockSpec((1,) * lead + (tr, c), lambda i: (0,) * lead + (i, 0))
    return pl.pallas_call(
        body, name=name, grid=(r // tr,),
        in_specs=[pl.BlockSpec((npart, tr, c), lambda i: (0, i, 0)), row, row, row],
        out_specs=[row] * 4,
        out_shape=[_sds(w.shape, F32)] * 4,
        compiler_params=_cparams("parallel"),
    )(parts, w, m, v)


def _block_diag(rows_gh, groups):
    gh, p = rows_gh.shape
    own = (jnp.arange(gh)[:, None] // (gh // groups) == jnp.arange(groups)[None, :]).astype(rows_gh.dtype)
    return (own[:, :, None] * rows_gh[:, None, :]).reshape(gh, groups * p)


def _block_diag_take(dense, groups):
    gh = dense.shape[0]
    p = dense.shape[1] // groups
    own = (jnp.arange(gh)[:, None] // (gh // groups) == jnp.arange(groups)[None, :]).astype(dense.dtype)
    return jnp.sum(dense.reshape(gh, groups, p) * own[:, :, None], axis=1)


FFN1 = ("ffn1_w1", "ffn1_w3", "ffn1_w2")
MIXER = ("w_in", "ssm_glu_w", "w_out")
FFN2 = ("ffn2_w1", "ffn2_w3", "ffn2_w2")
BIG = FFN1 + MIXER + FFN2
COL_SHARDED = ("ffn1_w1", "ffn1_w3", "w_in", "ffn2_w1", "ffn2_w3", "conv_w")
SMALL = ("norm_ffn1", "norm_mix", "conv_b", "conv_ln_g", "conv_ln_b", "conv_out_g", "ssm_A_re", "ssm_A_im",
         "ssm_log_dt", "ssm_B_re", "ssm_B_im", "ssm_C_re", "ssm_C_im", "ssm_D", "ssm_glu_b", "ssm_out_g",
         "norm_ffn2", "norm_final")
WEIGHTS = ("norm_ffn1", "ffn1_w1", "ffn1_w3", "ffn1_w2", "norm_mix", "w_in", "conv_w", "conv_b", "conv_ln_g",
           "conv_ln_b", "conv_out_g", "ssm_A_re", "ssm_A_im", "ssm_log_dt", "ssm_B_re", "ssm_B_im", "ssm_C_re",
           "ssm_C_im", "ssm_D", "ssm_glu_w", "ssm_glu_b", "ssm_out_g", "w_out", "norm_ffn2", "ffn2_w1", "ffn2_w3",
           "ffn2_w2", "norm_final")


def _ffn_backward(tag, dxo, x, g, w1, w3, w2, saved, exchange=None, reduce_names=None):
    a, b, h = saved
    (da, db, hid, dxh), got = _ffn_bwd_hidden(tag + "_bwd_hidden", dxo, a, b, w2, exchange=exchange)
    dw1, dw3 = _mm_tn(tag + "_dw1", da, h), _mm_tn(tag + "_dw3", db, h)
    across = None
    if reduce_names:
        send = [_row_blocks(dw1), _row_blocks(dw3)]
        dw2, from_core = _mm_tn(tag + "_dw2", hid, dxh, exchange=_core_exchange_plan(send))
        send.append(_row_blocks(dw2))
        from_core += _run_exchange("exchange_core_" + tag, _core_exchange_plan(send[2:]))
        across = _across_chips(reduce_names, send, from_core)
    else:
        dw2 = _mm_tn(tag + "_dw2", hid, dxh)
    f = a.shape[1]
    (dx, dg), reduced = _dx_rms_bwd(tag + "_bwd_dx", [(da, f, 0, w1, f, 0), (db, f, 0, w3, f, 0)], dxo, x, g,
                                    exchange=across)
    return (dx, dg, [dw1, dw3, dw2]), got, reduced


def _row_blocks(grad):
    return grad.reshape((N_DEV, -1) + grad.shape[1:])


def _across_chips(names, send, from_core):
    return _chip_exchange_plan([_pair_sum("pair_sum_" + k, s.reshape((N_DEV // 2, 2) + s.shape[1:]), o)
                                for k, s, o in zip(names, send, from_core)])


def _reduce_in_chip(names, grads):
    send = [_row_blocks(g) for g in grads]
    return _core_exchange_plan(send), functools.partial(_across_chips, names, send)


def kernel(x, norm_ffn1, ffn1_w1, ffn1_w3, ffn1_w2, norm_mix, w_in, conv_w, conv_b, conv_ln_g, conv_ln_b, conv_out_g, ssm_A_re, ssm_A_im, ssm_log_dt, ssm_B_re, ssm_B_im, ssm_C_re, ssm_C_im, ssm_D, ssm_glu_w, ssm_glu_b, ssm_out_g, w_out, norm_ffn2, ffn2_w1, ffn2_w3, ffn2_w2, norm_final, loss_target, m_norm_ffn1, m_ffn1_w1, m_ffn1_w3, m_ffn1_w2, m_norm_mix, m_w_in, m_conv_w, m_conv_b, m_conv_ln_g, m_conv_ln_b, m_conv_out_g, m_ssm_A_re, m_ssm_A_im, m_ssm_log_dt, m_ssm_B_re, m_ssm_B_im, m_ssm_C_re, m_ssm_C_im, m_ssm_D, m_ssm_glu_w, m_ssm_glu_b, m_ssm_out_g, m_w_out, m_norm_ffn2, m_ffn2_w1, m_ffn2_w3, m_ffn2_w2, m_norm_final, v_norm_ffn1, v_ffn1_w1, v_ffn1_w3, v_ffn1_w2, v_norm_mix, v_w_in, v_conv_w, v_conv_b, v_conv_ln_g, v_conv_ln_b, v_conv_out_g, v_ssm_A_re, v_ssm_A_im, v_ssm_log_dt, v_ssm_B_re, v_ssm_B_im, v_ssm_C_re, v_ssm_C_im, v_ssm_D, v_ssm_glu_w, v_ssm_glu_b, v_ssm_out_g, v_w_out, v_norm_ffn2, v_ffn2_w1, v_ffn2_w3, v_ffn2_w2, v_norm_final):
    args = dict(locals())
    wt = {n: args[n] for n in WEIGHTS}
    mom = {n: args["m_" + n] for n in WEIGHTS}
    var = {n: args["v_" + n] for n in WEIGHTS}

    bsz, seq, d = x.shape
    n = bsz * seq
    c = conv_b.shape[-1]
    groups = c // SSM_GROUP
    gp = groups * SSM_STATE
    u_b = 2

    shard = {k: (wt[k][0].T if k in COL_SHARDED else wt[k][0]).astype(BF16) for k in BIG}
    gathered = _run_exchange("gather_weights_ffn1", _gather_plan([shard[k] for k in FFN1]))
    full = {k: g.reshape(-1, g.shape[-1]) for k, g in zip(FFN1, gathered)}
    gather_rest = _gather_plan([shard[k] for k in MIXER + FFN2] + [wt["conv_w"][0]])

    vec = lambda k: wt[k].reshape(1, -1)
    g_ffn1, g_mix, g_ffn2, g_fin = vec("norm_ffn1"), vec("norm_mix"), vec("norm_ffn2"), vec("norm_final")
    cb, lng, lnb, cog = vec("conv_b"), vec("conv_ln_g"), vec("conv_ln_b"), vec("conv_out_g")
    d_skip, glu_b, sog = vec("ssm_D"), vec("ssm_glu_b"), vec("ssm_out_g")

    a_re, a_im = wt["ssm_A_re"][0], wt["ssm_A_im"][0]
    log_dt = wt["ssm_log_dt"][0].reshape(groups, 1)
    bt_re = wt["ssm_B_re"][0].transpose(0, 2, 1).reshape(groups * SSM_GROUP, SSM_STATE)
    bt_im = wt["ssm_B_im"][0].transpose(0, 2, 1).reshape(groups * SSM_GROUP, SSM_STATE)
    c_re = wt["ssm_C_re"][0].reshape(groups * SSM_GROUP, SSM_STATE)
    c_im = wt["ssm_C_im"][0].reshape(groups * SSM_GROUP, SSM_STATE)
    per_chan = lambda t: jnp.repeat(t, SSM_GROUP, axis=0)
    ssm_prim = (a_re, a_im, log_dt, per_chan(a_re), per_chan(a_im), per_chan(jnp.broadcast_to(log_dt, a_re.shape)),
                bt_re, bt_im)
    pw_r, pw_i, bb_r, bb_i = _ssm_prep("ssm_prep", ssm_prim)
    tab_f = _scan_tables(pw_r, pw_i, False)
    tab_b = _scan_tables(pw_r, pw_i, True)
    bbd = jnp.concatenate([_block_diag(bb_r, groups), _block_diag(bb_i, groups)], axis=1).astype(BF16)
    cdt = jnp.concatenate([_block_diag(c_re, groups), -_block_diag(c_im, groups)], axis=1).astype(BF16)

    x0 = x.reshape(n, d)
    (x1, *ffn1_saved), gathered = _ffn_fwd("ffn1_fwd", x0, g_ffn1, full["ffn1_w1"], full["ffn1_w3"], full["ffn1_w2"],
                                           exchange=gather_rest)
    full.update({k: g.reshape(-1, g.shape[-1]) for k, g in zip(MIXER + FFN2, gathered)})
    conv_w_full = gathered[-1].transpose(1, 0, 2).reshape(CONV_WIDTH, c)
    conv_w_pad = jnp.pad(conv_w_full, ((0, CONV_HALO - CONV_WIDTH), (0, 0)))
    (proj,), h2 = _rms_mm("mix_in", x1, g_mix, [full["w_in"]], F32)
    proj3 = proj.reshape(bsz, seq, 3 * c)
    an3, cv3 = _conv_fwd("conv_fwd", proj3, conv_w_pad, cb, lng, lnb, cog)
    an = an3.reshape(n, c)
    xs3, xs16, cx3 = _scan_fwd("scan_fwd", tab_f, proj3, u_b, bbd, cdt)
    w_o = full["w_out"]
    y, sn, x2 = _ssm_out_fwd("ssm_out_fwd", cx3.reshape(n, c), proj, u_b, d_skip, full["ssm_glu_w"], glu_b, sog,
                             x1, an, w_o)
    (dx3, *ffn2_saved, loss_tile, d_gfin), _ = _ffn_fwd(
        "ffn2_fwd", x2, g_ffn2, full["ffn2_w1"], full["ffn2_w3"], full["ffn2_w2"],
        head=(g_fin, loss_target.reshape(n, d)))

    grads, from_chips = {}, {}
    (dx2, grads["norm_ffn2"], dws), _, _ = _ffn_backward(
        "ffn2", dx3, x2, g_ffn2, full["ffn2_w1"], full["ffn2_w3"], full["ffn2_w2"], ffn2_saved)
    in_chip, across_chips = _reduce_in_chip(FFN2, dws)

    dmix, got = _mm_nt("mix_out_bwd", dx2, w_o, exchange=in_chip)
    reduce_ffn2 = across_chips(got)
    grads["w_out"] = _mm_tn("dw_out", (an, sn), dx2)

    dy, du_skip, grads["ssm_glu_w"], grads["ssm_glu_b"], grads["ssm_out_g"], grads["ssm_D"] = _ssm_out_bwd(
        "ssm_out_bwd", dmix, 1, y, proj, u_b, d_skip, full["ssm_glu_w"], glu_b, sog)
    (lam3, du3, dab_r, dab_i), got = _scan_bwd("scan_bwd", tab_b, dy.reshape(bsz, seq, c), xs3,
                                               du_skip.reshape(bsz, seq, c), bbd, cdt, exchange=reduce_ffn2)
    from_chips.update(zip(FFN2, got))
    lam, du = lam3.reshape(n, 2 * gp), du3.reshape(n, c)
    d_bbd = _band_wgrad("ssm_dbb", proj, u_b, c, lam)
    d_cdt = _band_wgrad("ssm_dc", dy, 0, c, xs16.reshape(n, 2 * gp))
    d_are, d_aim, d_ldt, d_btr, d_bti = _ssm_param_grads(
        "ssm_param_grads", ssm_prim,
        dab_r.reshape(SUBLANES, groups, SSM_STATE), dab_i.reshape(SUBLANES, groups, SSM_STATE),
        _band_diag_take(d_bbd, 0, c, gp), _band_diag_take(d_bbd, 1, c, gp))
    grads["ssm_A_re"], grads["ssm_A_im"], grads["ssm_log_dt"] = d_are, d_aim, d_ldt
    grads["ssm_B_re"], grads["ssm_B_im"] = d_btr, d_bti
    grads["ssm_C_re"] = _band_diag_take(d_cdt, 0, c, gp)
    grads["ssm_C_im"] = -_band_diag_take(d_cdt, 1, c, gp)

    dconv3, d_cw, grads["conv_b"], grads["conv_ln_g"], grads["conv_ln_b"], grads["conv_out_g"] = _conv_bwd(
        "conv_bwd", dmix.reshape(bsz, seq, 2 * c), proj3, cv3, conv_w_pad, lng, lnb, cog)
    dconv = dconv3.reshape(n, 2 * c)
    grads["conv_w"] = d_cw[:CONV_WIDTH]
    grads["w_in"] = _mm_tn("dw_in", (dconv, du), h2)
    w_i = full["w_in"]
    in_chip, across_chips = _reduce_in_chip(MIXER, [grads[k] for k in MIXER])
    (dx1, grads["norm_mix"]), got = _dx_rms_bwd("mix_in_bwd", [(dconv, 2 * c, 0, w_i, 2 * c, 0), (du, c, 0, w_i, c, 2)],
                                                dx2, x1, g_mix, exchange=in_chip)
    reduce_mixer = across_chips(got)

    grads["norm_final"] = d_gfin
    early = tuple(k for k in SMALL if k != "norm_ffn1")
    gather_small = _gather_plan([grads[k] for k in early] + [grads["conv_w"], loss_tile])

    (dx0, grads["norm_ffn1"], _), got, reduced = _ffn_backward(
        "ffn1", dx1, x0, g_ffn1, full["ffn1_w1"], full["ffn1_w3"], full["ffn1_w2"], ffn1_saved,
        exchange=_merge_plans(reduce_mixer, gather_small), reduce_names=FFN1)
    from_chips.update(zip(MIXER, got))
    small_parts = got[len(MIXER):]
    from_chips.update(zip(FFN1, reduced))

    res = {}
    for k in BIG:
        parts = from_chips[k]
        if k in COL_SHARDED:
            swap = lambda t: jnp.swapaxes(t, -1, -2)
            res[k] = [swap(t) for t in _adamw("adamw_" + k, parts, swap(wt[k]), swap(mom[k]), swap(var[k]))]
        else:
            res[k] = _adamw("adamw_" + k, parts, wt[k], mom[k], var[k])

    def as_2d(k, t):
        if k in ("ssm_B_re", "ssm_B_im"):
            return t[0].transpose(0, 2, 1).reshape(-1, SSM_STATE)
        if k in ("ssm_C_re", "ssm_C_im"):
            return t[0].reshape(-1, SSM_STATE)
        if k in ("ssm_A_re", "ssm_A_im"):
            return t[0]
        return t.reshape(-1, 1) if k == "ssm_log_dt" else t.reshape(1, -1)

    def as_param(k, t):
        if k in ("ssm_B_re", "ssm_B_im"):
            t = t.reshape(groups, SSM_GROUP, SSM_STATE).transpose(0, 2, 1)
        return t.reshape(wt[k].shape)

    (last_part,) = _run_exchange("gather_norm_ffn1_grad", _gather_plan([grads["norm_ffn1"]]))
    order = ("norm_ffn1",) + early
    updated = _adamw_small("adamw_replicated", [last_part] + small_parts,
                           *[[as_2d(k, src[k]) for k in order] for src in (wt, mom, var)])
    res.update({k: [as_param(k, t) for t in upd] for k, upd in zip(order, updated)})
    (conv_w_grad,), (loss_sum,) = updated[-2:]
    loss = loss_sum[0, 0]
    x_pos, y_pos, c_pos = (lax.axis_index(a) for a in MESH_AXES)
    cw_cols = c // N_DEV
    own_cw = lax.dynamic_slice_in_dim(conv_w_grad, (4 * x_pos + 2 * y_pos + c_pos) * cw_cols, cw_cols, axis=1)
    res["conv_w"] = _adamw("adamw_conv_w", own_cw[None], wt["conv_w"], mom["conv_w"], var["conv_w"])

    outs = [loss, dx0.reshape(bsz, seq, d)]
    for kind in range(4):
        outs += [res[k][kind] for k in WEIGHTS]
    return tuple(outs)
```
